```python
import jax, jax.numpy as jnp
from jax import lax
import numpy as np

D_MODEL = 1024
BATCH = 32
SEQ = 256
DEPTH = 2
DEC_BATCH = 4
DEC_SEQ = 2048
PAST_LEN = 256

GRID_W = 64
HEAD_DIM = 64
ATTN_SCALE = HEAD_DIM ** -0.5
H_A = 4
W_A = H_A * HEAD_DIM
RET_CHUNK = 64
GN_EPS = 1e-5
H_B = 6
KV_B = 2
W_B = H_B * HEAD_DIM
KV_W_B = KV_B * HEAD_DIM
WINDOW = 128
BAND_BLOCK = 128
ROPE_BASE = 10000.0
H_C = 6
W_C = H_C * HEAD_DIM
NA_ROWS = 8
NA_COLS = 16
NA_QBLK = 16
NA_KCOLS = 32
MIX_WIDTH = W_A + W_B + W_C
SPLIT_POINTS = (W_A, 2 * W_A, 3 * W_A, 4 * W_A, 4 * W_A + W_B, 4 * W_A + W_B + KV_W_B,
                4 * W_A + W_B + 2 * KV_W_B, 4 * W_A + W_B + 2 * KV_W_B + W_C,
                4 * W_A + W_B + 2 * KV_W_B + 2 * W_C)
IN_WIDTH = 4 * W_A + W_B + 2 * KV_W_B + 3 * W_C
N_EXPERTS = 64
TOP_K = 6
N_GROUPS = 8
TOPK_GROUPS = 4
D_EXPERT = 256
D_SHARED = 256
ROUTE_SCALE = 2.5
MOE_BLOCK = 128
ALPHA = (2 * DEPTH) ** 0.25
BETA = (8 * DEPTH) ** -0.25
LN_EPS = 1e-5
NEG_INF = -1e30

kernel_name = "hybrid_ret_win_na_moe_diffusion_step"


def layer_norm(x, g, b):
    xf = x.astype(jnp.float32)
    mu = jnp.mean(xf, -1, keepdims=True)
    var = jnp.mean(jnp.square(xf - mu), -1, keepdims=True)
    return ((xf - mu) * lax.rsqrt(var + LN_EPS) * g + b).astype(x.dtype)


def heads(t, n):
    b, l, _ = t.shape
    return t.reshape(b, l, n, HEAD_DIM).transpose(0, 2, 1, 3)


def merge_heads(t):
    b, n, l, d = t.shape
    return t.transpose(0, 2, 1, 3).reshape(b, l, n * d)


def ada_params(cond, w_mod, b_mod):
    m = jax.nn.silu(cond) @ w_mod + b_mod
    return jnp.split(m[:, None, :], 6, axis=-1)


def axial_rope(l):
    t = jnp.arange(l)
    n_freq = HEAD_DIM // 4
    inv = ROPE_BASE ** (-jnp.arange(n_freq, dtype=jnp.float32) / n_freq)
    row = (t // GRID_W).astype(jnp.float32)[:, None] * inv
    col = (t % GRID_W).astype(jnp.float32)[:, None] * inv
    ang = jnp.concatenate([row, col], -1)
    return jnp.cos(ang), jnp.sin(ang)


def apply_rope(x, cos, sin):
    half = HEAD_DIM // 2
    xf = x.astype(jnp.float32)
    x1, x2 = xf[..., :half], xf[..., half:]
    return jnp.concatenate([x1 * cos - x2 * sin, x1 * sin + x2 * cos], -1).astype(x.dtype)


def retention_scan(q, k, v, log_gamma, s0):
    b, h, l, d = q.shape
    n = l // RET_CHUNK
    qc = q.reshape(b, h, n, RET_CHUNK, d)
    kc = k.reshape(b, h, n, RET_CHUNK, d)
    vc = v.reshape(b, h, n, RET_CHUNK, d)
    pos = jnp.arange(RET_CHUNK, dtype=jnp.float32)
    diff = pos[:, None] - pos[None, :]
    dmat = jnp.where(diff >= 0, jnp.exp(log_gamma[:, None, None] * jnp.maximum(diff, 0.0)), 0.0)
    inner = jnp.einsum('bhncd,bhnmd->bhncm', qc, kc) * dmat[None, :, None]
    o_inner = jnp.einsum('bhncm,bhnme->bhnce', inner, vc)
    zeta = jnp.exp(log_gamma[:, None] * (RET_CHUNK - 1 - pos))
    xi = jnp.exp(log_gamma[:, None] * (pos + 1.0))
    g_chunk = jnp.exp(log_gamma * RET_CHUNK)[None, :, None, None]
    kv = jnp.einsum('bhncd,bhnce->bhnde', kc * zeta[None, :, None, :, None], vc)

    def step(s, kv_i):
        return g_chunk * s + kv_i, s

    s_fin, s_prev = lax.scan(step, s0, jnp.moveaxis(kv, 2, 0))
    s_prev = jnp.moveaxis(s_prev, 0, 2)
    o_cross = jnp.einsum('bhncd,bhnde->bhnce', qc * xi[None, :, None, :, None], s_prev)
    return (o_inner + o_cross).reshape(b, h, l, d), s_fin


def retention_bidir(q, k, v, lg_f, lg_b, s0_f, s0_b):
    o_f, s_f = retention_scan(q, k, v, lg_f, s0_f)
    o_b, s_b = retention_scan(q[:, :, ::-1], k[:, :, ::-1], v[:, :, ::-1], lg_b, s0_b)
    return o_f + o_b[:, :, ::-1], s_f, s_b


def retention_readout(o, gate, gn_g, gn_b):
    mu = jnp.mean(o, -1, keepdims=True)
    var = jnp.mean(jnp.square(o - mu), -1, keepdims=True)
    on = merge_heads((o - mu) * lax.rsqrt(var + GN_EPS)) * gn_g + gn_b
    return (on * jax.nn.silu(gate.astype(jnp.float32))).astype(gate.dtype)


def dense_attn(q, k, v, sink):
    b, h, lq, d = q.shape
    n_kv = k.shape[1]
    g = h // n_kv
    qg = q.reshape(b, n_kv, g, lq, d)
    s = jnp.einsum('bkgqd,bksd->bkgqs', qg, k).astype(jnp.float32) * ATTN_SCALE
    if sink is not None:
        col = jnp.broadcast_to(sink.astype(jnp.float32).reshape(1, n_kv, g, 1, 1), (b, n_kv, g, lq, 1))
        p = jax.nn.softmax(jnp.concatenate([s, col], -1), axis=-1)[..., :-1]
    else:
        p = jax.nn.softmax(s, axis=-1)
    o = jnp.einsum('bkgqs,bksd->bkgqd', p.astype(v.dtype), v)
    return o.reshape(b, h, lq, d)


def banded_window_attn(q, k, v, k_ctx, v_ctx, sink):
    b, h, l, d = q.shape
    n_kv = k.shape[1]
    g = h // n_kv
    nb = l // BAND_BLOCK
    pad = ((0, 0), (0, 0), (BAND_BLOCK, BAND_BLOCK), (0, 0))

    def band(t):
        tp = jnp.pad(t, pad).reshape(b, n_kv, nb + 2, BAND_BLOCK, d)
        return jnp.concatenate([tp[:, :, 0:nb], tp[:, :, 1:nb + 1], tp[:, :, 2:nb + 2]], axis=3)

    kb, vb = band(k), band(v)
    qb = q.reshape(b, n_kv, g, nb, BAND_BLOCK, d)
    s_loc = jnp.einsum('bkgnqd,bknsd->bkgnqs', qb, kb).astype(jnp.float32) * ATTN_SCALE
    blk = jnp.arange(nb)[:, None, None] * BAND_BLOCK
    key_pos = blk - BAND_BLOCK + jnp.arange(3 * BAND_BLOCK)[None, None, :]
    q_pos = blk + jnp.arange(BAND_BLOCK)[None, :, None]
    valid = (jnp.abs(key_pos - q_pos) <= WINDOW) & (key_pos >= 0) & (key_pos < l)
    s_loc = jnp.where(valid, s_loc, NEG_INF)
    s_ctx = jnp.einsum('bkgnqd,bksd->bkgnqs', qb, k_ctx).astype(jnp.float32) * ATTN_SCALE
    col = jnp.broadcast_to(sink.astype(jnp.float32).reshape(1, n_kv, g, 1, 1, 1),
                           (b, n_kv, g, nb, BAND_BLOCK, 1))
    p = jax.nn.softmax(jnp.concatenate([s_loc, s_ctx, col], -1), axis=-1)
    n_loc = 3 * BAND_BLOCK
    lc = k_ctx.shape[2]
    o = (jnp.einsum('bkgnqs,bknsd->bkgnqd', p[..., :n_loc].astype(v.dtype), vb)
         + jnp.einsum('bkgnqs,bksd->bkgnqd', p[..., n_loc:n_loc + lc].astype(v.dtype), v_ctx))
    return o.reshape(b, h, l, d)


def neighbourhood_attn(q, k, v, k_ctx, v_ctx, rpb):
    b, h, l, d = q.shape
    rows = l // GRID_W
    kh = min(NA_ROWS, rows)
    ncb = GRID_W // NA_QBLK
    r = jnp.arange(rows)
    r0 = jnp.clip(r - kh // 2, 0, rows - kh)
    key_rows = r0[:, None] + jnp.arange(kh)[None, :]
    cstart = jnp.clip(jnp.arange(ncb) * NA_QBLK - (NA_KCOLS - NA_QBLK) // 2, 0, GRID_W - NA_KCOLS)
    key_cols = cstart[:, None] + jnp.arange(NA_KCOLS)[None, :]
    idx = (key_rows[:, None, :, None] * GRID_W + key_cols[None, :, None, :]).reshape(rows, ncb, kh * NA_KCOLS)
    kg = jnp.take(k, idx, axis=2)
    vg = jnp.take(v, idx, axis=2)
    qb = q.reshape(b, h, rows, ncb, NA_QBLK, d)
    s_loc = jnp.einsum('bhrjqd,bhrjsd->bhrjqs', qb, kg).astype(jnp.float32) * ATTN_SCALE
    q_cols = jnp.arange(ncb)[:, None] * NA_QBLK + jnp.arange(NA_QBLK)[None, :]
    c0 = jnp.clip(q_cols - NA_COLS // 2, 0, GRID_W - NA_COLS)
    kc_b = key_cols[:, None, :]
    in_win = (kc_b >= c0[:, :, None]) & (kc_b < c0[:, :, None] + NA_COLS)
    ri = (key_rows - r[:, None] + NA_ROWS - 1)[:, None, None, :, None]
    ci = jnp.clip(kc_b - q_cols[:, :, None] + NA_COLS - 1, 0, 2 * NA_COLS - 2)[None, :, :, None, :]
    bias = rpb[:, ri, ci].reshape(h, rows, ncb, NA_QBLK, kh * NA_KCOLS).astype(jnp.float32)
    mask = jnp.broadcast_to(in_win[:, :, None, :], (ncb, NA_QBLK, kh, NA_KCOLS)).reshape(ncb, NA_QBLK, kh * NA_KCOLS)
    s_loc = jnp.where(mask[None, None, None], s_loc + bias[None], NEG_INF)
    s_ctx = jnp.einsum('bhrjqd,bhsd->bhrjqs', qb, k_ctx).astype(jnp.float32) * ATTN_SCALE
    p = jax.nn.softmax(jnp.concatenate([s_loc, s_ctx], -1), axis=-1)
    n_loc = kh * NA_KCOLS
    o = (jnp.einsum('bhrjqs,bhrjsd->bhrjqd', p[..., :n_loc].astype(v.dtype), vg)
         + jnp.einsum('bhrjqs,bhsd->bhrjqd', p[..., n_loc:].astype(v.dtype), v_ctx))
    return o.reshape(b, h, l, d)


def mixer_context(h, w_in, w_out, dec_f, dec_b, gn_g, gn_b, sink):
    b, l, _ = h.shape
    qa, ka, va, ga, qb, kb, vb, qc, kc, vc = jnp.split(h @ w_in, SPLIT_POINTS, axis=-1)
    s0 = jnp.zeros((b, H_A, HEAD_DIM, HEAD_DIM), jnp.float32)
    oa, st_f, st_b = retention_bidir(heads(qa, H_A).astype(jnp.float32),
                                     heads(ka, H_A).astype(jnp.float32) * ATTN_SCALE,
                                     heads(va, H_A).astype(jnp.float32),
                                     jax.nn.log_sigmoid(dec_f.astype(jnp.float32)),
                                     jax.nn.log_sigmoid(dec_b.astype(jnp.float32)), s0, s0)
    ya = retention_readout(oa, ga, gn_g, gn_b)
    kb_h, vb_h = heads(kb, KV_B), heads(vb, KV_B)
    yb = merge_heads(dense_attn(heads(qb, H_B), kb_h, vb_h, sink))
    kc_h, vc_h = heads(kc, H_C), heads(vc, H_C)
    yc = merge_heads(dense_attn(heads(qc, H_C), kc_h, vc_h, None))
    y = jnp.concatenate([ya, yb, yc], -1) @ w_out
    return y, st_f.astype(h.dtype), st_b.astype(h.dtype), kb_h, vb_h, kc_h, vc_h


def mixer_latent(h, st_f, st_b, kb_ctx, vb_ctx, kc_ctx, vc_ctx, w_in, w_out, dec_f, dec_b, gn_g, gn_b, sink, rpb):
    b, l, _ = h.shape
    qa, ka, va, ga, qb, kb, vb, qc, kc, vc = jnp.split(h @ w_in, SPLIT_POINTS, axis=-1)
    oa, _, _ = retention_bidir(heads(qa, H_A).astype(jnp.float32),
                               heads(ka, H_A).astype(jnp.float32) * ATTN_SCALE,
                               heads(va, H_A).astype(jnp.float32),
                               jax.nn.log_sigmoid(dec_f.astype(jnp.float32)),
                               jax.nn.log_sigmoid(dec_b.astype(jnp.float32)),
                               st_f.astype(jnp.float32), st_b.astype(jnp.float32))
    ya = retention_readout(oa, ga, gn_g, gn_b)
    cos, sin = axial_rope(l)
    yb = merge_heads(banded_window_attn(apply_rope(heads(qb, H_B), cos, sin),
                                        apply_rope(heads(kb, KV_B), cos, sin),
                                        heads(vb, KV_B), kb_ctx, vb_ctx, sink))
    yc = merge_heads(neighbourhood_attn(heads(qc, H_C), heads(kc, H_C), heads(vc, H_C),
                                        kc_ctx, vc_ctx, rpb))
    return jnp.concatenate([ya, yb, yc], -1) @ w_out


def moe_ffn(h, w_router, b_router, w_gu, w_down, w_sgu, w_sdown):
    b, l, dm = h.shape
    t = b * l
    x = h.reshape(t, dm)
    scores = jax.nn.sigmoid((x @ w_router).astype(jnp.float32))
    sel = scores + b_router.astype(jnp.float32)
    grp_score = lax.top_k(sel.reshape(t, N_GROUPS, N_EXPERTS // N_GROUPS), 2)[0].sum(-1)
    _, gidx = lax.top_k(grp_score, TOPK_GROUPS)
    gmask = jax.nn.one_hot(gidx, N_GROUPS, dtype=jnp.float32).sum(1) > 0
    emask = jnp.repeat(gmask, N_EXPERTS // N_GROUPS, axis=1)
    _, eidx = lax.top_k(jnp.where(emask, sel, NEG_INF), TOP_K)
    wsel = jnp.take_along_axis(scores, eidx, axis=1)
    wsel = wsel / jnp.sum(wsel, -1, keepdims=True) * ROUTE_SCALE
    n_assign = t * TOP_K
    e_flat = eidx.reshape(n_assign).astype(jnp.int32)
    tok_flat = jnp.repeat(jnp.arange(t, dtype=jnp.int32), TOP_K)
    w_flat = wsel.reshape(n_assign)
    order = jnp.argsort(e_flat)
    e_s = e_flat[order]
    counts = jnp.bincount(e_flat, length=N_EXPERTS)
    start = jnp.cumsum(counts) - counts
    pcounts = (counts + MOE_BLOCK - 1) // MOE_BLOCK * MOE_BLOCK
    pend = jnp.cumsum(pcounts)
    pstart = pend - pcounts
    dest = pstart[e_s] + jnp.arange(n_assign) - start[e_s]
    n_blocks = (n_assign + N_EXPERTS * (MOE_BLOCK - 1) + MOE_BLOCK - 1) // MOE_BLOCK
    n_slots = n_blocks * MOE_BLOCK
    buf_tok = jnp.zeros((n_slots,), jnp.int32).at[dest].set(tok_flat[order])
    buf_w = jnp.zeros((n_slots,), jnp.float32).at[dest].set(w_flat[order])
    blk_e = jnp.minimum(jnp.searchsorted(pend, jnp.arange(n_blocks) * MOE_BLOCK, side='right'), N_EXPERTS - 1)

    def expert_block(args):
        tok, e = args
        gu = x[tok] @ w_gu[e]
        return (jax.nn.silu(gu[:, :D_EXPERT]) * gu[:, D_EXPERT:]) @ w_down[e]

    out = lax.map(expert_block, (buf_tok.reshape(n_blocks, MOE_BLOCK), blk_e))
    routed = jnp.zeros_like(x).at[buf_tok].add(out.reshape(n_slots, dm) * buf_w[:, None].astype(x.dtype))
    sgu = x @ w_sgu
    shared = (jax.nn.silu(sgu[:, :D_SHARED]) * sgu[:, D_SHARED:]) @ w_sdown
    return (routed + shared).reshape(b, l, dm)


def setup_inputs(seed: int = 0) -> dict:
    key = jax.random.key(seed)
    ks = jax.random.split(key, 30)
    f32 = jnp.float32

    def nrm(i, shape, scale):
        return jax.random.normal(ks[i], shape, f32) * scale

    decay_base = jnp.log(2.0 ** (5.0 + jnp.arange(H_A, dtype=f32)) - 1.0)
    return {
        "x_prompt": nrm(0, (BATCH, SEQ, D_MODEL), 1.0),
        "x_sample": nrm(1, (DEC_BATCH, DEC_SEQ, D_MODEL), 1.0),
        "state_ret_fwd": nrm(2, (DEC_BATCH, DEPTH, H_A, HEAD_DIM, HEAD_DIM), 0.5),
        "state_ret_bwd": nrm(3, (DEC_BATCH, DEPTH, H_A, HEAD_DIM, HEAD_DIM), 0.5),
        "cache_win_k": nrm(4, (DEC_BATCH, DEPTH, KV_B, PAST_LEN, HEAD_DIM), 1.0),
        "cache_win_v": nrm(5, (DEC_BATCH, DEPTH, KV_B, PAST_LEN, HEAD_DIM), 1.0),
        "cache_na_k": nrm(6, (DEC_BATCH, DEPTH, H_C, PAST_LEN, HEAD_DIM), 1.0),
        "cache_na_v": nrm(7, (DEC_BATCH, DEPTH, H_C, PAST_LEN, HEAD_DIM), 1.0),
        "c": nrm(8, (DEC_BATCH, D_MODEL), 1.0),
        "c_ctx": nrm(9, (D_MODEL,), 1.0),
        "w_in": nrm(10, (DEPTH, D_MODEL, IN_WIDTH), D_MODEL ** -0.5),
        "w_out": nrm(11, (DEPTH, MIX_WIDTH, D_MODEL), MIX_WIDTH ** -0.5 * BETA),
        "ret_decay_fwd": decay_base + nrm(12, (DEPTH, H_A), 0.1),
        "ret_decay_bwd": decay_base + nrm(13, (DEPTH, H_A), 0.1),
        "ret_gn_g": 1.0 + nrm(14, (DEPTH, W_A), 0.02),
        "ret_gn_b": nrm(15, (DEPTH, W_A), 0.02),
        "win_sink": nrm(16, (DEPTH, H_B), 0.5),
        "na_rpb": nrm(17, (DEPTH, H_C, 2 * NA_ROWS - 1, 2 * NA_COLS - 1), 0.1),
        "w_mod": nrm(18, (DEPTH, D_MODEL, 6 * D_MODEL), 0.5 * D_MODEL ** -0.5),
        "b_mod": nrm(19, (DEPTH, 6 * D_MODEL), 0.02),
        "ln1_g": 1.0 + nrm(20, (DEPTH, D_MODEL), 0.02),
        "ln1_b": nrm(21, (DEPTH, D_MODEL), 0.02),
        "ln2_g": 1.0 + nrm(22, (DEPTH, D_MODEL), 0.02),
        "ln2_b": nrm(23, (DEPTH, D_MODEL), 0.02),
        "w_router": nrm(24, (DEPTH, D_MODEL, N_EXPERTS), D_MODEL ** -0.5),
        "b_router": nrm(25, (DEPTH, N_EXPERTS), 0.01),
        "w_expert_gu": nrm(26, (DEPTH, N_EXPERTS, D_MODEL, 2 * D_EXPERT), D_MODEL ** -0.5),
        "w_expert_down": nrm(27, (DEPTH, N_EXPERTS, D_EXPERT, D_MODEL), D_EXPERT ** -0.5 * BETA),
        "w_shared_gu": nrm(28, (DEPTH, D_MODEL, 2 * D_SHARED), D_MODEL ** -0.5),
        "w_shared_down": nrm(29, (DEPTH, D_SHARED, D_MODEL), D_SHARED ** -0.5 * BETA),
    }


def reference(x_prompt, x_sample, state_ret_fwd, state_ret_bwd, cache_win_k, cache_win_v,
              cache_na_k, cache_na_v, c, c_ctx, w_in, w_out, ret_decay_fwd, ret_decay_bwd,
              ret_gn_g, ret_gn_b, win_sink, na_rpb, w_mod, b_mod, ln1_g, ln1_b, ln2_g, ln2_b,
              w_router, b_router, w_expert_gu, w_expert_down, w_shared_gu, w_shared_down):
    xp = x_prompt
    sf_l, sb_l, kb_l, vb_l, kc_l, vc_l = [], [], [], [], [], []
    for l in range(DEPTH):
        sh1, sc1, g1, sh2, sc2, g2 = ada_params(c_ctx[None, :], w_mod[l], b_mod[l])
        y, st_f, st_b, kb_h, vb_h, kc_h, vc_h = mixer_context(
            xp * (1.0 + sc1) + sh1, w_in[l], w_out[l], ret_decay_fwd[l], ret_decay_bwd[l],
            ret_gn_g[l], ret_gn_b[l], win_sink[l])
        xp = layer_norm(ALPHA * xp + g1 * y, ln1_g[l], ln1_b[l])
        f = moe_ffn(xp * (1.0 + sc2) + sh2, w_router[l], b_router[l], w_expert_gu[l],
                    w_expert_down[l], w_shared_gu[l], w_shared_down[l])
        xp = layer_norm(ALPHA * xp + g2 * f, ln2_g[l], ln2_b[l])
        sf_l.append(st_f); sb_l.append(st_b)
        kb_l.append(kb_h); vb_l.append(vb_h); kc_l.append(kc_h); vc_l.append(vc_h)
    new_state_ret_fwd = jnp.stack(sf_l, axis=1)
    new_state_ret_bwd = jnp.stack(sb_l, axis=1)
    new_cache_win_k = jnp.stack(kb_l, axis=1)
    new_cache_win_v = jnp.stack(vb_l, axis=1)
    new_cache_na_k = jnp.stack(kc_l, axis=1)
    new_cache_na_v = jnp.stack(vc_l, axis=1)

    xs = x_sample
    for l in range(DEPTH):
        sh1, sc1, g1, sh2, sc2, g2 = ada_params(c, w_mod[l], b_mod[l])
        y = mixer_latent(xs * (1.0 + sc1) + sh1, state_ret_fwd[:, l], state_ret_bwd[:, l],
                         cache_win_k[:, l], cache_win_v[:, l], cache_na_k[:, l], cache_na_v[:, l],
                         w_in[l], w_out[l], ret_decay_fwd[l], ret_decay_bwd[l],
                         ret_gn_g[l], ret_gn_b[l], win_sink[l], na_rpb[l])
        xs = layer_norm(ALPHA * xs + g1 * y, ln1_g[l], ln1_b[l])
        f = moe_ffn(xs * (1.0 + sc2) + sh2, w_router[l], b_router[l], w_expert_gu[l],
                    w_expert_down[l], w_shared_gu[l], w_shared_down[l])
        xs = layer_norm(ALPHA * xs + g2 * f, ln2_g[l], ln2_b[l])

    return (xp, xs, new_state_ret_fwd, new_state_ret_bwd, new_cache_win_k, new_cache_win_v,
            new_cache_na_k, new_cache_na_v)
```

```python
import functools

import numpy as np
import jax
import jax.numpy as jnp
from jax import lax
from jax.experimental import pallas as pl
from jax.experimental.pallas import tpu as pltpu

F32 = jnp.float32
BF16 = jnp.bfloat16
I32 = jnp.int32

D_MODEL = 1024
BATCH = 32
SEQ = 256
DEPTH = 2
DEC_BATCH = 4
DEC_SEQ = 2048
PAST_LEN = 256
GRID_W = 64
HEAD_DIM = 64
ATTN_SCALE = HEAD_DIM ** -0.5
H_A = 4
W_A = H_A * HEAD_DIM
GN_EPS = 1e-5
H_B = 6
KV_B = 2
W_B = H_B * HEAD_DIM
KV_W_B = KV_B * HEAD_DIM
WINDOW = 128
ROPE_BASE = 10000.0
H_C = 6
W_C = H_C * HEAD_DIM
NA_ROWS = 8
NA_COLS = 16
IN_WIDTH = 4 * W_A + W_B + 2 * KV_W_B + 3 * W_C
N_EXPERTS = 64
TOP_K = 6
N_GROUPS = 8
GROUP_SIZE = N_EXPERTS // N_GROUPS
TOPK_GROUPS = 4
D_EXPERT = 256
D_SHARED = 256
ROUTE_SCALE = 2.5
ALPHA = (2 * DEPTH) ** 0.25
LN_EPS = 1e-5
NEG_INF = -1e30
PICKED = -3e38

T_CTX = BATCH * SEQ
T_LAT = DEC_BATCH * DEC_SEQ
T_ALL = T_CTX + T_LAT
N_COND = 8

PA_W = 4 * W_A
PB_W = W_B + 2 * KV_W_B
PC_W = 3 * W_C
CACHE_W = 2 * KV_W_B + 2 * W_C

LANES = 128
SUBLANES = 8
VMEM_LIMIT = 56 * 1024 * 1024

TM_PROJ = 512
TM_TOK = 256
RET_CHUNK = 256
WIN_QB = 128
NA_TILE_ROWS = 4
NA_KEY_ROWS = 11
BM = 256
N_ASSIGN = T_ALL * TOP_K
N_BLOCKS = (N_ASSIGN + N_EXPERTS * (BM - 1) + BM - 1) // BM
N_SLOTS = N_BLOCKS * BM
N_TOK_TILES = T_ALL // TM_TOK


def _dot(a, b):
    return jnp.dot(a, b, preferred_element_type=F32)


def _dot_nt(a, b):
    return lax.dot_general(a, b, (((1,), (1,)), ((), ())), preferred_element_type=F32)


def _silu(x):
    return x * jax.nn.sigmoid(x)


def _log_sigmoid(x):
    return jnp.minimum(x, 0.0) - jnp.log(1.0 + jnp.exp(-jnp.abs(x)))


def _cond_row(tile, tile_rows):
    n_ctx = T_CTX // tile_rows
    per_lat = DEC_SEQ // tile_rows
    return jnp.where(tile < n_ctx, 0, 1 + (tile - n_ctx) // per_lat)


def _layer_norm(x, g, b):
    mu = jnp.mean(x, -1, keepdims=True)
    xc = x - mu
    var = jnp.mean(xc * xc, -1, keepdims=True)
    return xc * lax.rsqrt(var + LN_EPS) * g + b


MOD_TN = 1536


def _mod_kernel(cond_ref, w_ref, b_ref, o_ref):
    s = _silu(cond_ref[...])
    s_hi = s.astype(BF16)
    s_lo = (s - s_hi.astype(F32)).astype(BF16)
    w = w_ref[0]
    w_hi = w.astype(BF16)
    w_lo = (w - w_hi.astype(F32)).astype(BF16)
    o_ref[0] = _dot(s_hi, w_hi) + _dot(s_lo, w_hi) + _dot(s_hi, w_lo) + b_ref[0]


def _modulation(cond, w_mod, b_mod):
    n_out = 6 * D_MODEL
    return pl.pallas_call(
        _mod_kernel,
        out_shape=jax.ShapeDtypeStruct((DEPTH, N_COND, n_out), F32),
        grid=(DEPTH, n_out // MOD_TN),
        in_specs=[
            pl.BlockSpec((N_COND, D_MODEL), lambda l, j: (0, 0)),
            pl.BlockSpec((1, D_MODEL, MOD_TN), lambda l, j: (l, 0, j)),
            pl.BlockSpec((1, 1, MOD_TN), lambda l, j: (l, 0, j)),
        ],
        out_specs=pl.BlockSpec((1, N_COND, MOD_TN), lambda l, j: (l, 0, j)),
        compiler_params=pltpu.CompilerParams(
            dimension_semantics=("arbitrary", "arbitrary"), vmem_limit_bytes=VMEM_LIMIT),
        name="modulation",
    )(cond, w_mod, b_mod.reshape(DEPTH, 1, n_out))


def _inproj_kernel(x_ref, mod_ref, w_ref, pa_ref, pb_ref, pc_ref, cache_ref):
    i = pl.program_id(0)
    ci = _cond_row(i, TM_PROJ)
    sh = mod_ref[pl.ds(ci, 1), 0:D_MODEL]
    sc = mod_ref[pl.ds(ci, 1), D_MODEL:2 * D_MODEL]
    h = x_ref[...] * (1.0 + sc) + sh
    p = _dot(h.astype(BF16), w_ref[...])
    pa_ref[...] = p[:, 0:PA_W].astype(BF16)
    pb_ref[...] = p[:, PA_W:PA_W + PB_W].astype(BF16)
    pc_ref[...] = p[:, PA_W + PB_W:IN_WIDTH].astype(BF16)

    @pl.when(i < T_CTX // TM_PROJ)
    def _():
        cache_ref[:, 0:2 * KV_W_B] = p[:, PA_W + W_B:PA_W + PB_W]
        cache_ref[:, 2 * KV_W_B:CACHE_W] = p[:, PA_W + PB_W + W_C:IN_WIDTH]


def _in_projection(x, mod, w_in_bf16):
    n_ctx_tiles = T_CTX // TM_PROJ
    return pl.pallas_call(
        _inproj_kernel,
        out_shape=(
            jax.ShapeDtypeStruct((T_ALL, PA_W), BF16),
            jax.ShapeDtypeStruct((T_ALL, PB_W), BF16),
            jax.ShapeDtypeStruct((T_ALL, PC_W), BF16),
            jax.ShapeDtypeStruct((T_CTX, CACHE_W), F32),
        ),
        grid=(T_ALL // TM_PROJ,),
        in_specs=[
            pl.BlockSpec((TM_PROJ, D_MODEL), lambda i: (i, 0)),
            pl.BlockSpec((N_COND, 6 * D_MODEL), lambda i: (0, 0)),
            pl.BlockSpec((D_MODEL, IN_WIDTH), lambda i: (0, 0)),
        ],
        out_specs=(
            pl.BlockSpec((TM_PROJ, PA_W), lambda i: (i, 0)),
            pl.BlockSpec((TM_PROJ, PB_W), lambda i: (i, 0)),
            pl.BlockSpec((TM_PROJ, PC_W), lambda i: (i, 0)),
            pl.BlockSpec((TM_PROJ, CACHE_W), lambda i: (jnp.minimum(i, n_ctx_tiles - 1), 0)),
        ),
        compiler_params=pltpu.CompilerParams(
            dimension_semantics=("arbitrary",), vmem_limit_bytes=VMEM_LIMIT),
        name="in_projection",
    )(x, mod, w_in_bf16)


def _decay_matrix(lg_f, lg_b, n):
    row = lax.broadcasted_iota(I32, (n, n), 0)
    col = lax.broadcasted_iota(I32, (n, n), 1)
    diff = (row - col).astype(F32)
    fwd = jnp.where(diff >= 0, jnp.exp(lg_f * jnp.maximum(diff, 0.0)), 0.0)
    bwd = jnp.where(diff <= 0, jnp.exp(lg_b * jnp.maximum(-diff, 0.0)), 0.0)
    return (fwd + bwd) * ATTN_SCALE


def _retention_readout(o, gate, g, b):
    mu = jnp.mean(o, -1, keepdims=True)
    oc = o - mu
    var = jnp.mean(oc * oc, -1, keepdims=True)
    on = oc * lax.rsqrt(var + GN_EPS) * g + b
    return on * _silu(gate.astype(F32))


def _softmax_attend(s, v, extra_logit=None):
    m = jnp.max(s, -1, keepdims=True)
    if extra_logit is not None:
        m = jnp.maximum(m, extra_logit)
    p = jnp.exp(s - m)
    den = jnp.sum(p, -1, keepdims=True)
    if extra_logit is not None:
        den = den + jnp.exp(extra_logit - m)
    return _dot(p.astype(BF16), v) / den


def _ctx_mixer_kernel(sink_ref, pa_ref, pb_ref, pc_ref, decf_ref, decb_ref, gng_ref, gnb_ref,
                      mix_ref, sf_ref, sb_ref):
    n = SEQ
    hd = HEAD_DIM
    pos = lax.broadcasted_iota(I32, (n, hd), 0).astype(F32)
    for h in range(H_A):
        c0 = h * hd
        q = pa_ref[:, c0:c0 + hd]
        k = pa_ref[:, W_A + c0:W_A + c0 + hd]
        v = pa_ref[:, 2 * W_A + c0:2 * W_A + c0 + hd]
        gate = pa_ref[:, 3 * W_A + c0:3 * W_A + c0 + hd]
        lg_f = _log_sigmoid(decf_ref[h])
        lg_b = _log_sigmoid(decb_ref[h])
        dmat = _decay_matrix(lg_f, lg_b, n)
        o = _dot((_dot_nt(q, k) * dmat).astype(BF16), v)
        kf = k.astype(F32)
        zf = jnp.exp(lg_f[:, 0:hd] * (n - 1.0 - pos)) * ATTN_SCALE
        zb = jnp.exp(lg_b[:, 0:hd] * pos) * ATTN_SCALE
        sf_ref[0, h] = _dot((kf * zf).T.astype(BF16), v)
        sb_ref[0, h] = _dot((kf * zb).T.astype(BF16), v)
        y = _retention_readout(o, gate, gng_ref[:, c0:c0 + hd], gnb_ref[:, c0:c0 + hd])
        mix_ref[:, c0:c0 + hd] = y.astype(BF16)
    for j in range(KV_B):
        k = pb_ref[:, W_B + j * hd:W_B + (j + 1) * hd]
        v = pb_ref[:, W_B + KV_W_B + j * hd:W_B + KV_W_B + (j + 1) * hd]
        for g in range(H_B // KV_B):
            hh = j * (H_B // KV_B) + g
            q = pb_ref[:, hh * hd:(hh + 1) * hd]
            o = _softmax_attend(_dot_nt(q, k) * ATTN_SCALE, v, sink_ref[hh])
            mix_ref[:, W_A + hh * hd:W_A + (hh + 1) * hd] = o.astype(BF16)
    for hh in range(H_C):
        q = pc_ref[:, hh * hd:(hh + 1) * hd]
        k = pc_ref[:, W_C + hh * hd:W_C + (hh + 1) * hd]
        v = pc_ref[:, 2 * W_C + hh * hd:2 * W_C + (hh + 1) * hd]
        o = _softmax_attend(_dot_nt(q, k) * ATTN_SCALE, v)
        mix_ref[:, W_A + W_B + hh * hd:W_A + W_B + (hh + 1) * hd] = o.astype(BF16)


def _ctx_mixers(pa, pb, pc, sink, decf, decb, gng, gnb):
    return pl.pallas_call(
        _ctx_mixer_kernel,
        out_shape=(
            jax.ShapeDtypeStruct((T_CTX, D_MODEL), BF16),
            jax.ShapeDtypeStruct((BATCH, H_A, HEAD_DIM, HEAD_DIM), F32),
            jax.ShapeDtypeStruct((BATCH, H_A, HEAD_DIM, HEAD_DIM), F32),
        ),
        grid=(BATCH,),
        in_specs=[
            pl.BlockSpec(memory_space=pltpu.SMEM),
            pl.BlockSpec((SEQ, PA_W), lambda b: (b, 0)),
            pl.BlockSpec((SEQ, PB_W), lambda b: (b, 0)),
            pl.BlockSpec((SEQ, PC_W), lambda b: (b, 0)),
            pl.BlockSpec((H_A, 1, SEQ), lambda b: (0, 0, 0)),
            pl.BlockSpec((H_A, 1, SEQ), lambda b: (0, 0, 0)),
            pl.BlockSpec((1, W_A), lambda b: (0, 0)),
            pl.BlockSpec((1, W_A), lambda b: (0, 0)),
        ],
        out_specs=(
            pl.BlockSpec((SEQ, D_MODEL), lambda b: (b, 0)),
            pl.BlockSpec((1, H_A, HEAD_DIM, HEAD_DIM), lambda b: (b, 0, 0, 0)),
            pl.BlockSpec((1, H_A, HEAD_DIM, HEAD_DIM), lambda b: (b, 0, 0, 0)),
        ),
        compiler_params=pltpu.CompilerParams(
            dimension_semantics=("arbitrary",), vmem_limit_bytes=VMEM_LIMIT),
        name="ctx_mixers",
    )(sink, pa, pb, pc, decf, decb, gng, gnb)


def _lat_ret_kernel(pa_ref, stf_ref, stb_ref, decf_ref, decb_ref, gng_ref, gnb_ref, ya_ref, acc_ref):
    c = RET_CHUNK
    hd = HEAD_DIM
    n_chunks = DEC_SEQ // c
    pos = lax.broadcasted_iota(I32, (c, hd), 0).astype(F32)
    for h in range(H_A):
        c0 = h * hd
        lg_f = _log_sigmoid(decf_ref[h])
        lg_b = _log_sigmoid(decb_ref[h])
        dmat = _decay_matrix(lg_f, lg_b, c)
        lf = lg_f[:, 0:hd]
        lb = lg_b[:, 0:hd]
        zf = jnp.exp(lf * (c - 1.0 - pos)) * ATTN_SCALE
        zb = jnp.exp(lb * pos) * ATTN_SCALE
        xf = jnp.exp(lf * (pos + 1.0))
        xb = jnp.exp(lb * (c - pos))
        gcf = jnp.exp(lf * float(c))
        gcb = jnp.exp(lb * float(c))
        g = gng_ref[:, c0:c0 + hd]
        b = gnb_ref[:, c0:c0 + hd]

        def load(i, off):
            rows = pl.ds(pl.multiple_of(i * c, c), c)
            return pa_ref[rows, off + c0:off + c0 + hd]

        def fwd(i, s):
            q, k, v = load(i, 0), load(i, W_A), load(i, 2 * W_A)
            o = _dot((_dot_nt(q, k) * dmat).astype(BF16), v)
            o = o + _dot((q.astype(F32) * xf).astype(BF16), s.astype(BF16))
            acc_ref[pl.ds(pl.multiple_of(i * c, c), c), c0:c0 + hd] = o
            return gcf * s + _dot((k.astype(F32) * zf).T.astype(BF16), v)

        lax.fori_loop(0, n_chunks, fwd, stf_ref[0, 0, h])

        def bwd(j, s):
            i = n_chunks - 1 - j
            rows = pl.ds(pl.multiple_of(i * c, c), c)
            q, k, v = load(i, 0), load(i, W_A), load(i, 2 * W_A)
            o = acc_ref[rows, c0:c0 + hd] + _dot((q.astype(F32) * xb).astype(BF16), s.astype(BF16))
            y = _retention_readout(o, load(i, 3 * W_A), g, b)
            ya_ref[rows, c0:c0 + hd] = y.astype(BF16)
            return gcb * s + _dot((k.astype(F32) * zb).T.astype(BF16), v)

        lax.fori_loop(0, n_chunks, bwd, stb_ref[0, 0, h])


def _lat_retention(pa, st_f, st_b, layer, decf, decb, gng, gnb):
    lat0 = T_CTX // DEC_SEQ
    st_spec = pl.BlockSpec((1, 1, H_A, HEAD_DIM, HEAD_DIM), lambda b: (b, layer, 0, 0, 0))
    return pl.pallas_call(
        _lat_ret_kernel,
        out_shape=jax.ShapeDtypeStruct((T_LAT, W_A), BF16),
        grid=(DEC_BATCH,),
        in_specs=[
            pl.BlockSpec((DEC_SEQ, PA_W), lambda b: (lat0 + b, 0)),
            st_spec, st_spec,
            pl.BlockSpec((H_A, 1, RET_CHUNK), lambda b: (0, 0, 0)),
            pl.BlockSpec((H_A, 1, RET_CHUNK), lambda b: (0, 0, 0)),
            pl.BlockSpec((1, W_A), lambda b: (0, 0)),
            pl.BlockSpec((1, W_A), lambda b: (0, 0)),
        ],
        out_specs=pl.BlockSpec((DEC_SEQ, W_A), lambda b: (b, 0)),
        scratch_shapes=[pltpu.VMEM((DEC_SEQ, W_A), F32)],
        compiler_params=pltpu.CompilerParams(
            dimension_semantics=("arbitrary",), vmem_limit_bytes=VMEM_LIMIT),
        name="lat_retention",
    )(pa, st_f, st_b, decf, decb, gng, gnb)


def _swap_halves_matrix(width):
    r = lax.broadcasted_iota(I32, (width, width), 0)
    c = lax.broadcasted_iota(I32, (width, width), 1)
    return jnp.where((r ^ (HEAD_DIM // 2)) == c, 1.0, 0.0).astype(BF16)


def _rope(x, cos, sin_signed, swap):
    return x.astype(F32) * cos + _dot(x, swap) * sin_signed


def _lat_win_kernel(sink_ref, pq_ref, pseq_ref, kctx_ref, vctx_ref, cos_ref, sin_ref, yb_ref, krope_ref):
    n = pl.program_id(1)
    hd = HEAD_DIM
    qb = WIN_QB
    n_blk = DEC_SEQ // qb
    group = H_B // KV_B
    swap = _swap_halves_matrix(LANES)

    @pl.when(n == 0)
    def _():
        k = pseq_ref[:, W_B:W_B + KV_W_B]
        krope_ref[...] = _rope(k, cos_ref[...], sin_ref[...], swap).astype(BF16)

    q_rows = pl.ds(pl.multiple_of(n * qb, qb), qb)
    cos_q = cos_ref[q_rows, :]
    sin_q = sin_ref[q_rows, :]
    qr = [_rope(pq_ref[:, p * LANES:(p + 1) * LANES], cos_q, sin_q, swap).astype(BF16)
          for p in range(W_B // LANES)]

    ws = jnp.clip(n - 1, 0, n_blk - 3) * qb
    k_rows = pl.ds(pl.multiple_of(ws, qb), 3 * qb)
    q_pos = n * qb + lax.broadcasted_iota(I32, (group * qb, 3 * qb), 0) % qb
    k_pos = ws + lax.broadcasted_iota(I32, (group * qb, 3 * qb), 1)
    valid = jnp.abs(k_pos - q_pos) <= WINDOW
    head_of_row = lax.broadcasted_iota(I32, (group * qb, 1), 0) // qb
    for j in range(KV_B):
        heads = [j * group + g for g in range(group)]
        qs = jnp.concatenate(
            [qr[hh // 2][:, (hh % 2) * hd:(hh % 2 + 1) * hd] for hh in heads], axis=0)
        kw = krope_ref[k_rows, j * hd:(j + 1) * hd]
        vw = pseq_ref[k_rows, W_B + KV_W_B + j * hd:W_B + KV_W_B + (j + 1) * hd]
        kc = kctx_ref[0, 0, j].astype(BF16)
        vc = vctx_ref[0, 0, j].astype(BF16)
        s_loc = jnp.where(valid, _dot_nt(qs, kw) * ATTN_SCALE, NEG_INF)
        s_ctx = _dot_nt(qs, kc) * ATTN_SCALE
        sink = jnp.zeros((group * qb, 1), F32)
        for g, hh in enumerate(heads):
            sink = jnp.where(head_of_row == g, sink_ref[hh], sink)
        m = jnp.maximum(jnp.maximum(jnp.max(s_loc, -1, keepdims=True),
                                    jnp.max(s_ctx, -1, keepdims=True)), sink)
        p_loc = jnp.exp(s_loc - m)
        p_ctx = jnp.exp(s_ctx - m)
        den = (jnp.sum(p_loc, -1, keepdims=True) + jnp.sum(p_ctx, -1, keepdims=True)
               + jnp.exp(sink - m))
        o = (_dot(p_loc.astype(BF16), vw) + _dot(p_ctx.astype(BF16), vc)) / den
        for g, hh in enumerate(heads):
            yb_ref[:, hh * hd:(hh + 1) * hd] = o[g * qb:(g + 1) * qb].astype(BF16)


def _lat_window_attn(pb, cache_k, cache_v, layer, sink, cos_t, sin_t):
    n_blk = DEC_SEQ // WIN_QB
    lat_blk0 = T_CTX // WIN_QB
    lat_seq0 = T_CTX // DEC_SEQ
    ctx_spec = pl.BlockSpec((1, 1, KV_B, PAST_LEN, HEAD_DIM), lambda b, n: (b, layer, 0, 0, 0))
    return pl.pallas_call(
        _lat_win_kernel,
        out_shape=jax.ShapeDtypeStruct((T_LAT, W_B), BF16),
        grid=(DEC_BATCH, n_blk),
        in_specs=[
            pl.BlockSpec(memory_space=pltpu.SMEM),
            pl.BlockSpec((WIN_QB, PB_W), lambda b, n: (lat_blk0 + b * n_blk + n, 0)),
            pl.BlockSpec((DEC_SEQ, PB_W), lambda b, n: (lat_seq0 + b, 0)),
            ctx_spec, ctx_spec,
            pl.BlockSpec((DEC_SEQ, LANES), lambda b, n: (0, 0)),
            pl.BlockSpec((DEC_SEQ, LANES), lambda b, n: (0, 0)),
        ],
        out_specs=pl.BlockSpec((WIN_QB, W_B), lambda b, n: (b * n_blk + n, 0)),
        scratch_shapes=[pltpu.VMEM((DEC_SEQ, KV_W_B), BF16)],
        compiler_params=pltpu.CompilerParams(
            dimension_semantics=("arbitrary", "arbitrary"), vmem_limit_bytes=VMEM_LIMIT),
        name="lat_window_attn",
    )(sink, pb, pb, cache_k, cache_v, cos_t, sin_t)


NA_Q = NA_TILE_ROWS * GRID_W
NA_K = NA_KEY_ROWS * GRID_W
NA_TILES = DEC_SEQ // NA_Q
LAT_ROWS = DEC_SEQ // GRID_W


def _na_window_start(tile):
    return jnp.clip(tile * NA_TILE_ROWS - NA_ROWS // 2, 0, LAT_ROWS - NA_KEY_ROWS)


def _lat_na_kernel(pq_ref, pseq_ref, kctx_ref, vctx_ref, bias_ref, yc_ref):
    t = pl.program_id(1)
    hd = HEAD_DIM
    k_rows = pl.ds(pl.multiple_of(_na_window_start(t) * GRID_W, GRID_W), NA_K)
    for hh in range(H_C):
        q = pq_ref[:, hh * hd:(hh + 1) * hd]
        kw = pseq_ref[k_rows, W_C + hh * hd:W_C + (hh + 1) * hd]
        vw = pseq_ref[k_rows, 2 * W_C + hh * hd:2 * W_C + (hh + 1) * hd]
        kc = kctx_ref[0, 0, hh].astype(BF16)
        vc = vctx_ref[0, 0, hh].astype(BF16)
        s_loc = _dot_nt(q, kw) * ATTN_SCALE + bias_ref[0, hh]
        s_ctx = _dot_nt(q, kc) * ATTN_SCALE
        m = jnp.maximum(jnp.max(s_loc, -1, keepdims=True), jnp.max(s_ctx, -1, keepdims=True))
        p_loc = jnp.exp(s_loc - m)
        p_ctx = jnp.exp(s_ctx - m)
        den = jnp.sum(p_loc, -1, keepdims=True) + jnp.sum(p_ctx, -1, keepdims=True)
        o = (_dot(p_loc.astype(BF16), vw) + _dot(p_ctx.astype(BF16), vc)) / den
        yc_ref[:, hh * hd:(hh + 1) * hd] = o.astype(BF16)


def _na_tile_type(t):
    return jnp.where(t == 0, 0, jnp.where(t == NA_TILES - 1, 2, 1))


def _lat_na_attn(pc, cache_k, cache_v, layer, maskbias):
    lat_tile0 = T_CTX // NA_Q
    lat_seq0 = T_CTX // DEC_SEQ
    ctx_spec = pl.BlockSpec((1, 1, H_C, PAST_LEN, HEAD_DIM), lambda b, t: (b, layer, 0, 0, 0))
    return pl.pallas_call(
        _lat_na_kernel,
        out_shape=jax.ShapeDtypeStruct((T_LAT, W_C), BF16),
        grid=(DEC_BATCH, NA_TILES),
        in_specs=[
            pl.BlockSpec((NA_Q, PC_W), lambda b, t: (lat_tile0 + b * NA_TILES + t, 0)),
            pl.BlockSpec((DEC_SEQ, PC_W), lambda b, t: (lat_seq0 + b, 0)),
            ctx_spec, ctx_spec,
            pl.BlockSpec((1, H_C, NA_Q, NA_K), lambda b, t: (_na_tile_type(t), 0, 0, 0)),
        ],
        out_specs=pl.BlockSpec((NA_Q, W_C), lambda b, t: (b * NA_TILES + t, 0)),
        compiler_params=pltpu.CompilerParams(
            dimension_semantics=("arbitrary", "arbitrary"), vmem_limit_bytes=VMEM_LIMIT),
        name="lat_na_attn",
    )(pc, pc, cache_k, cache_v, maskbias)


def _na_block_index():
    out = np.zeros((3, NA_TILE_ROWS, NA_KEY_ROWS), np.int32)
    for ty, tile in enumerate((0, 1, NA_TILES - 1)):
        r = tile * NA_TILE_ROWS
        ws = int(np.clip(r - NA_ROWS // 2, 0, LAT_ROWS - NA_KEY_ROWS))
        for qq in range(NA_TILE_ROWS):
            qr = r + qq
            r0 = int(np.clip(qr - NA_ROWS // 2, 0, LAT_ROWS - NA_ROWS))
            for kk in range(NA_KEY_ROWS):
                kr = ws + kk
                out[ty, qq, kk] = kr - qr + NA_ROWS - 1 if r0 <= kr < r0 + NA_ROWS else 2 * NA_ROWS - 1
    return out


def _na_maskbias(rpb):
    qc = np.arange(GRID_W)[:, None]
    kc = np.arange(GRID_W)[None, :]
    c0 = np.clip(qc - NA_COLS // 2, 0, GRID_W - NA_COLS)
    col_ok = (kc >= c0) & (kc < c0 + NA_COLS)
    ci = np.clip(kc - qc + NA_COLS - 1, 0, 2 * NA_COLS - 2)
    onehot = (ci[None] == np.arange(2 * NA_COLS - 1)[:, None, None]).astype(np.float32)
    cols = jnp.einsum("hab,bqk->haqk", rpb, jnp.asarray(onehot), precision=lax.Precision.HIGHEST)
    cols = jnp.where(jnp.asarray(col_ok)[None, None], cols, NEG_INF)
    cols = jnp.concatenate([cols, jnp.full((H_C, 1, GRID_W, GRID_W), NEG_INF, F32)], axis=1)
    blocks = jnp.take(cols, jnp.asarray(_na_block_index().reshape(-1)), axis=1)
    blocks = blocks.reshape(H_C, 3, NA_TILE_ROWS, NA_KEY_ROWS, GRID_W, GRID_W)
    return blocks.transpose(1, 0, 2, 4, 3, 5).reshape(3, H_C, NA_Q, NA_K)


def _rope_tables():
    t = np.arange(DEC_SEQ)
    n_freq = HEAD_DIM // 4
    inv = (ROPE_BASE ** (-np.arange(n_freq, dtype=np.float32) / n_freq)).astype(np.float32)
    row = (t // GRID_W).astype(np.float32)[:, None] * inv
    col = (t % GRID_W).astype(np.float32)[:, None] * inv
    ang = np.concatenate([row, col], -1)
    cos, sin = np.cos(ang), np.sin(ang)
    cos_h = np.concatenate([cos, cos], -1)
    sin_h = np.concatenate([-sin, sin], -1)
    reps = LANES // HEAD_DIM
    return (jnp.asarray(np.tile(cos_h, (1, reps)), F32), jnp.asarray(np.tile(sin_h, (1, reps)), F32))


def _first_index_of(mask, iota, sentinel):
    return jnp.min(jnp.where(mask, iota, sentinel), axis=0, keepdims=True)


def _route(logits, b_col):
    n = logits.shape[1]
    scores = jax.nn.sigmoid(logits)
    sel = scores + b_col
    io_g = lax.broadcasted_iota(I32, (GROUP_SIZE, n), 0)
    gs_rows = []
    for g in range(N_GROUPS):
        s = sel[g * GROUP_SIZE:(g + 1) * GROUP_SIZE]
        m1 = jnp.max(s, axis=0, keepdims=True)
        i1 = _first_index_of(s == m1, io_g, GROUP_SIZE)
        m2 = jnp.max(jnp.where(io_g == i1, PICKED, s), axis=0, keepdims=True)
        gs_rows.append(m1 + m2)
    gs = jnp.concatenate(gs_rows, axis=0)
    io_n = lax.broadcasted_iota(I32, (N_GROUPS, n), 0)
    gsel = jnp.zeros((N_GROUPS, n), F32)
    for _ in range(TOPK_GROUPS):
        mg = jnp.max(gs, axis=0, keepdims=True)
        gi = _first_index_of(gs == mg, io_n, N_GROUPS)
        hit = io_n == gi
        gsel = jnp.where(hit, 1.0, gsel)
        gs = jnp.where(hit, PICKED, gs)
    cand = jnp.concatenate(
        [jnp.where(gsel[g:g + 1] > 0.5, sel[g * GROUP_SIZE:(g + 1) * GROUP_SIZE], NEG_INF)
         for g in range(N_GROUPS)], axis=0)
    io_e = lax.broadcasted_iota(I32, (N_EXPERTS, n), 0)
    picks, raw = [], []
    for _ in range(TOP_K):
        mv = jnp.max(cand, axis=0, keepdims=True)
        ei = _first_index_of(cand == mv, io_e, N_EXPERTS)
        hit = io_e == ei
        picks.append((hit, ei))
        raw.append(jnp.sum(jnp.where(hit, scores, 0.0), axis=0, keepdims=True))
        cand = jnp.where(hit, PICKED, cand)
    return picks, raw


def _post_mixer_kernel(x_ref, mixc_ref, ya_ref, yb_ref, yc_ref, wout_ref, mod_ref, g_ref, b_ref,
                       wrh_ref, wrl_ref, br_ref,
                       x1_ref, h2_ref, eidx_ref, wsel_ref, rank_ref, cnt_ref, base_ref):
    i = pl.program_id(0)
    tm = TM_TOK

    @pl.when(i == 0)
    def _():
        base_ref[...] = jnp.zeros_like(base_ref)

    ci = _cond_row(i, tm)
    gate1 = mod_ref[pl.ds(ci, 1), 2 * D_MODEL:3 * D_MODEL]
    sh2 = mod_ref[pl.ds(ci, 1), 3 * D_MODEL:4 * D_MODEL]
    sc2 = mod_ref[pl.ds(ci, 1), 4 * D_MODEL:5 * D_MODEL]
    mix_lat = jnp.concatenate([ya_ref[...], yb_ref[...], yc_ref[...]], axis=-1)
    mix = jnp.where(i < T_CTX // tm, mixc_ref[...], mix_lat)
    y = _dot(mix, wout_ref[...])
    x1 = _layer_norm(ALPHA * x_ref[...] + gate1 * y, g_ref[...], b_ref[...])
    x1_ref[...] = x1
    h2 = x1 * (1.0 + sc2) + sh2
    h2_ref[...] = h2

    h_hi = h2.astype(BF16)
    h_lo = (h2 - h_hi.astype(F32)).astype(BF16)
    logits = (_dot_nt(wrh_ref[...], h_hi) + _dot_nt(wrh_ref[...], h_lo)
              + _dot_nt(wrl_ref[...], h_hi))
    picks, raw = _route(logits, br_ref[...])

    total = raw[0]
    for r in raw[1:]:
        total = total + r
    scale = ROUTE_SCALE / total
    multi = jnp.zeros((N_EXPERTS, tm), F32)
    for hit, _ in picks:
        multi = multi + jnp.where(hit, 1.0, 0.0)
    before = (lax.broadcasted_iota(I32, (tm, tm), 0) < lax.broadcasted_iota(I32, (tm, tm), 1))
    cum = _dot(multi.astype(BF16), jnp.where(before, 1.0, 0.0).astype(BF16)) + base_ref[:, 0:1]
    pad = jnp.zeros((SUBLANES - TOP_K, tm), F32)
    eidx_ref[...] = jnp.concatenate([ei for _, ei in picks] + [pad.astype(I32)], axis=0)
    wsel_ref[...] = jnp.concatenate([r * scale for r in raw] + [pad], axis=0)
    rank_ref[...] = jnp.concatenate(
        [jnp.sum(jnp.where(hit, cum, 0.0), axis=0, keepdims=True) for hit, _ in picks] + [pad],
        axis=0).astype(I32)
    base_ref[...] = base_ref[...] + jnp.sum(multi, axis=1, keepdims=True)
    cnt_ref[...] = base_ref[...]


def _post_mixer(x, mix_c, ya, yb, yc, w_out_bf16, mod, ln_g, ln_b, wr_hi_t, wr_lo_t, b_router_col):
    n_ctx = T_CTX // TM_TOK
    ctx_map = lambda i: (jnp.minimum(i, n_ctx - 1), 0)
    lat_map = lambda i: (jnp.maximum(i - n_ctx, 0), 0)
    row_map = lambda i: (i, 0)
    const = lambda i: (0, 0)
    tok_map = lambda i: (0, i)
    return pl.pallas_call(
        _post_mixer_kernel,
        out_shape=(
            jax.ShapeDtypeStruct((T_ALL, D_MODEL), F32),
            jax.ShapeDtypeStruct((T_ALL, D_MODEL), F32),
            jax.ShapeDtypeStruct((SUBLANES, T_ALL), I32),
            jax.ShapeDtypeStruct((SUBLANES, T_ALL), F32),
            jax.ShapeDtypeStruct((SUBLANES, T_ALL), I32),
            jax.ShapeDtypeStruct((N_EXPERTS, LANES), F32),
        ),
        grid=(N_TOK_TILES,),
        in_specs=[
            pl.BlockSpec((TM_TOK, D_MODEL), row_map),
            pl.BlockSpec((TM_TOK, D_MODEL), ctx_map),
            pl.BlockSpec((TM_TOK, W_A), lat_map),
            pl.BlockSpec((TM_TOK, W_B), lat_map),
            pl.BlockSpec((TM_TOK, W_C), lat_map),
            pl.BlockSpec((D_MODEL, D_MODEL), const),
            pl.BlockSpec((N_COND, 6 * D_MODEL), const),
            pl.BlockSpec((1, D_MODEL), const),
            pl.BlockSpec((1, D_MODEL), const),
            pl.BlockSpec((N_EXPERTS, D_MODEL), const),
            pl.BlockSpec((N_EXPERTS, D_MODEL), const),
            pl.BlockSpec((N_EXPERTS, 1), const),
        ],
        out_specs=(
            pl.BlockSpec((TM_TOK, D_MODEL), row_map),
            pl.BlockSpec((TM_TOK, D_MODEL), row_map),
            pl.BlockSpec((SUBLANES, TM_TOK), tok_map),
            pl.BlockSpec((SUBLANES, TM_TOK), tok_map),
            pl.BlockSpec((SUBLANES, TM_TOK), tok_map),
            pl.BlockSpec((N_EXPERTS, LANES), const),
        ),
        scratch_shapes=[pltpu.VMEM((N_EXPERTS, LANES), F32)],
        compiler_params=pltpu.CompilerParams(
            dimension_semantics=("arbitrary",), vmem_limit_bytes=VMEM_LIMIT),
        name="post_mixer",
    )(x, mix_c, ya, yb, yc, w_out_bf16, mod, ln_g, ln_b, wr_hi_t, wr_lo_t, b_router_col)


ROWS_PER_TILE = TM_TOK * TOP_K


def _row_copy(src, src_row, dst, dst_row, sem):
    return pltpu.make_async_copy(src.at[pl.ds(src_row, 1)], dst.at[pl.ds(dst_row, 1)], sem)


def _dispatch_kernel(pend_ref, cnt_ref, h2_ref, dest_hbm, xs_hbm, idx_smem, zero_ref, sem_idx, sem_zero,
                     sem_rows):
    i = pl.program_id(0)

    def zero_copy(e):
        return pltpu.make_async_copy(
            zero_ref, xs_hbm.at[pl.ds(pl.multiple_of(pend_ref[e] - BM, BM), BM)], sem_zero)

    @pl.when(i == 0)
    def _():
        zero_ref[...] = jnp.zeros_like(zero_ref)

        def start(e, c):
            @pl.when(cnt_ref[e] > 0)
            def _():
                zero_copy(e).start()
            return c

        def wait(e, c):
            @pl.when(cnt_ref[e] > 0)
            def _():
                zero_copy(e).wait()
            return c

        def tail_copy(blk):
            return pltpu.make_async_copy(
                zero_ref, xs_hbm.at[pl.ds(pl.multiple_of(blk * BM, BM), BM)], sem_zero)

        def start_tail(blk, c):
            tail_copy(blk).start()
            return c

        def wait_tail(blk, c):
            tail_copy(blk).wait()
            return c

        n_used = pend_ref[N_EXPERTS - 1] // BM
        lax.fori_loop(0, N_EXPERTS, start, 0)
        lax.fori_loop(n_used, N_BLOCKS, start_tail, 0)
        lax.fori_loop(0, N_EXPERTS, wait, 0)
        lax.fori_loop(n_used, N_BLOCKS, wait_tail, 0)

    idx_copy = pltpu.make_async_copy(dest_hbm.at[i], idx_smem, sem_idx)
    idx_copy.start()
    idx_copy.wait()

    def issue(t, c):
        for k in range(TOP_K):
            _row_copy(h2_ref, t, xs_hbm, idx_smem[0, k * TM_TOK + t], sem_rows).start()
        return c

    lax.fori_loop(0, TM_TOK, issue, 0)
    pltpu.make_async_copy(xs_hbm.at[pl.ds(0, ROWS_PER_TILE)], xs_hbm.at[pl.ds(0, ROWS_PER_TILE)],
                          sem_rows).wait()


def _dispatch(h2, dest_tiles, pend, counts):
    return pl.pallas_call(
        _dispatch_kernel,
        out_shape=jax.ShapeDtypeStruct((N_SLOTS, D_MODEL), F32),
        grid_spec=pltpu.PrefetchScalarGridSpec(
            num_scalar_prefetch=2,
            grid=(N_TOK_TILES,),
            in_specs=[
                pl.BlockSpec((TM_TOK, D_MODEL), lambda i, pe, cn: (i, 0)),
                pl.BlockSpec(memory_space=pl.ANY),
            ],
            out_specs=pl.BlockSpec(memory_space=pl.ANY),
            scratch_shapes=[
                pltpu.SMEM((1, ROWS_PER_TILE), I32),
                pltpu.VMEM((BM, D_MODEL), F32),
                pltpu.SemaphoreType.DMA,
                pltpu.SemaphoreType.DMA,
                pltpu.SemaphoreType.DMA,
            ],
        ),
        compiler_params=pltpu.CompilerParams(
            dimension_semantics=("arbitrary",), vmem_limit_bytes=VMEM_LIMIT),
        name="moe_dispatch",
    )(pend, counts, h2, dest_tiles)


def _expert_kernel(blk_e_ref, n_used_ref, xs_ref, wgu_ref, wdown_ref, ys_ref):
    i = pl.program_id(0)

    @pl.when(i < n_used_ref[0])
    def _():
        x = xs_ref[...].astype(BF16)
        gu = _dot(x, wgu_ref[0].astype(BF16))
        act = _silu(gu[:, 0:D_EXPERT]) * gu[:, D_EXPERT:2 * D_EXPERT]
        ys_ref[...] = _dot(act.astype(BF16), wdown_ref[0].astype(BF16))

    @pl.when(i >= n_used_ref[0])
    def _():
        ys_ref[...] = jnp.zeros_like(ys_ref)


def _experts(xs, w_gu, w_down, blk_e, n_used):
    def blk(i, be, nu):
        return jnp.minimum(i, nu[0] - 1)

    return pl.pallas_call(
        _expert_kernel,
        out_shape=jax.ShapeDtypeStruct((N_SLOTS, D_MODEL), F32),
        grid_spec=pltpu.PrefetchScalarGridSpec(
            num_scalar_prefetch=2,
            grid=(N_BLOCKS,),
            in_specs=[
                pl.BlockSpec((BM, D_MODEL), lambda i, be, nu: (blk(i, be, nu), 0)),
                pl.BlockSpec((1, D_MODEL, 2 * D_EXPERT), lambda i, be, nu: (be[blk(i, be, nu)], 0, 0)),
                pl.BlockSpec((1, D_EXPERT, D_MODEL), lambda i, be, nu: (be[blk(i, be, nu)], 0, 0)),
            ],
            out_specs=pl.BlockSpec((BM, D_MODEL), lambda i, be, nu: (i, 0)),
        ),
        compiler_params=pltpu.CompilerParams(
            dimension_semantics=("arbitrary",), vmem_limit_bytes=VMEM_LIMIT),
        name="moe_experts",
    )(blk_e, n_used, xs, w_gu, w_down)


def _combine_kernel(x1_ref, h2_ref, wsel_ref, dest_hbm, ys_hbm, wsgu_ref, wsdown_ref, mod_ref, g_ref, b_ref,
                    out_ref, idx_smem, rows_ref, sem_idx, sem_rows):
    i = pl.program_id(0)
    tm = TM_TOK
    idx_copy = pltpu.make_async_copy(dest_hbm.at[i], idx_smem, sem_idx)
    idx_copy.start()
    idx_copy.wait()

    def issue(r, c):
        _row_copy(ys_hbm, idx_smem[0, r], rows_ref, r, sem_rows).start()
        return c

    lax.fori_loop(0, ROWS_PER_TILE, issue, 0)

    sgu = _dot(h2_ref[...].astype(BF16), wsgu_ref[...])
    act = _silu(sgu[:, 0:D_SHARED]) * sgu[:, D_SHARED:2 * D_SHARED]
    f = _dot(act.astype(BF16), wsdown_ref[...])

    pltpu.make_async_copy(ys_hbm.at[pl.ds(0, ROWS_PER_TILE)], rows_ref, sem_rows).wait()
    w = wsel_ref[...]
    for k in range(TOP_K):
        f = f + w[:, k:k + 1] * rows_ref[k * tm:(k + 1) * tm, :]
    ci = _cond_row(i, tm)
    gate2 = mod_ref[pl.ds(ci, 1), 5 * D_MODEL:6 * D_MODEL]
    out_ref[...] = _layer_norm(ALPHA * x1_ref[...] + gate2 * f, g_ref[...], b_ref[...])


def _combine(x1, h2, wsel_rows, dest_tiles, ys, w_sgu_bf16, w_sdown_bf16, mod, ln_g, ln_b):
    row_map = lambda i: (i, 0)
    const = lambda i: (0, 0)
    return pl.pallas_call(
        _combine_kernel,
        out_shape=jax.ShapeDtypeStruct((T_ALL, D_MODEL), F32),
        grid=(N_TOK_TILES,),
        in_specs=[
            pl.BlockSpec((TM_TOK, D_MODEL), row_map),
            pl.BlockSpec((TM_TOK, D_MODEL), row_map),
            pl.BlockSpec((TM_TOK, SUBLANES), row_map),
            pl.BlockSpec(memory_space=pl.ANY),
            pl.BlockSpec(memory_space=pl.ANY),
            pl.BlockSpec((D_MODEL, 2 * D_SHARED), const),
            pl.BlockSpec((D_SHARED, D_MODEL), const),
            pl.BlockSpec((N_COND, 6 * D_MODEL), const),
            pl.BlockSpec((1, D_MODEL), const),
            pl.BlockSpec((1, D_MODEL), const),
        ],
        out_specs=pl.BlockSpec((TM_TOK, D_MODEL), row_map),
        scratch_shapes=[
            pltpu.SMEM((1, ROWS_PER_TILE), I32),
            pltpu.VMEM((ROWS_PER_TILE, D_MODEL), F32),
            pltpu.SemaphoreType.DMA,
            pltpu.SemaphoreType.DMA,
        ],
        compiler_params=pltpu.CompilerParams(
            dimension_semantics=("arbitrary",), vmem_limit_bytes=VMEM_LIMIT),
        name="moe_combine",
    )(x1, h2, wsel_rows, dest_tiles, ys, w_sgu_bf16, w_sdown_bf16, mod, ln_g, ln_b)


def _slot_plan(eidx, rank, counts):
    counts = counts.astype(I32)
    pcounts = (counts + BM - 1) // BM * BM
    pend = jnp.cumsum(pcounts)
    pstart = pend - pcounts
    dest = jnp.take(pstart, eidx[:TOP_K], axis=0) + rank[:TOP_K]
    dest_tiles = dest.reshape(TOP_K, N_TOK_TILES, TM_TOK).transpose(1, 0, 2).reshape(
        N_TOK_TILES, 1, ROWS_PER_TILE)
    blk_e = jnp.minimum(
        jnp.searchsorted(pend, jnp.arange(N_BLOCKS, dtype=I32) * BM, side="right"), N_EXPERTS - 1)
    n_used = (pend[-1] // BM).reshape(1)
    return dest_tiles, pend, counts, blk_e.astype(I32), n_used.astype(I32)


def _lane_rows(v, width):
    return jnp.broadcast_to(v.astype(F32)[:, None, None], (v.shape[0], 1, width))


def kernel(x_prompt, x_sample, state_ret_fwd, state_ret_bwd, cache_win_k, cache_win_v, cache_na_k, cache_na_v, c, c_ctx, w_in, w_out, ret_decay_fwd, ret_decay_bwd, ret_gn_g, ret_gn_b, win_sink, na_rpb, w_mod, b_mod, ln1_g, ln1_b, ln2_g, ln2_b, w_router, b_router, w_expert_gu, w_expert_down, w_shared_gu, w_shared_down):
    cond = jnp.concatenate(
        [c_ctx[None, :], c, jnp.zeros((N_COND - 1 - DEC_BATCH, D_MODEL), F32)], axis=0)
    mod_all = _modulation(cond, w_mod, b_mod)
    cos_t, sin_t = _rope_tables()

    x = jnp.concatenate([x_prompt.reshape(T_CTX, D_MODEL), x_sample.reshape(T_LAT, D_MODEL)], axis=0)
    sf_l, sb_l, cache_l = [], [], []
    for l in range(DEPTH):
        mod = mod_all[l]
        pa, pb, pc, cache = _in_projection(x, mod, w_in[l].astype(BF16))
        cache_l.append(cache)
        decf_s, decb_s = _lane_rows(ret_decay_fwd[l], SEQ), _lane_rows(ret_decay_bwd[l], SEQ)
        gng, gnb = ret_gn_g[l][None, :], ret_gn_b[l][None, :]
        mix_c, st_f, st_b = _ctx_mixers(pa, pb, pc, win_sink[l], decf_s, decb_s, gng, gnb)
        sf_l.append(st_f)
        sb_l.append(st_b)
        ya = _lat_retention(pa, state_ret_fwd, state_ret_bwd, l,
                            _lane_rows(ret_decay_fwd[l], RET_CHUNK), _lane_rows(ret_decay_bwd[l], RET_CHUNK),
                            gng, gnb)
        yb = _lat_window_attn(pb, cache_win_k, cache_win_v, l, win_sink[l], cos_t, sin_t)
        yc = _lat_na_attn(pc, cache_na_k, cache_na_v, l, _na_maskbias(na_rpb[l]))

        wr_t = w_router[l].T
        wr_hi = wr_t.astype(BF16)
        wr_lo = (wr_t - wr_hi.astype(F32)).astype(BF16)
        x1, h2, eidx, wsel, rank, counts = _post_mixer(
            x, mix_c, ya, yb, yc, w_out[l].astype(BF16), mod, ln1_g[l][None, :], ln1_b[l][None, :],
            wr_hi, wr_lo, b_router[l][:, None])
        dest_tiles, pend, cnt, blk_e, n_used = _slot_plan(eidx, rank, counts[:, 0])
        xs = _dispatch(h2, dest_tiles, pend, cnt)
        ys = _experts(xs, w_expert_gu[l], w_expert_down[l], blk_e, n_used)
        x = _combine(x1, h2, wsel.T, dest_tiles, ys, w_shared_gu[l].astype(BF16),
                     w_shared_down[l].astype(BF16), mod, ln2_g[l][None, :], ln2_b[l][None, :])

    y_prompt = x[:T_CTX].reshape(BATCH, SEQ, D_MODEL)
    y_sample = x[T_CTX:].reshape(DEC_BATCH, DEC_SEQ, D_MODEL)
    new_sf = jnp.stack(sf_l, axis=1)
    new_sb = jnp.stack(sb_l, axis=1)

    def heads_of(cols, n_heads):
        per_layer = [cache[:, cols[0]:cols[1]].reshape(BATCH, SEQ, n_heads, HEAD_DIM).transpose(0, 2, 1, 3)
                     for cache in cache_l]
        return jnp.stack(per_layer, axis=1)

    new_win_k = heads_of((0, KV_W_B), KV_B)
    new_win_v = heads_of((KV_W_B, 2 * KV_W_B), KV_B)
    new_na_k = heads_of((2 * KV_W_B, 2 * KV_W_B + W_C), H_C)
    new_na_v = heads_of((2 * KV_W_B + W_C, CACHE_W), H_C)
    return (y_prompt, y_sample, new_sf, new_sb, new_win_k, new_win_v, new_na_k, new_na_v)
```

```python
import functools

import numpy as np
import jax
import jax.numpy as jnp
from jax import lax
from jax.experimental import pallas as pl
from jax.experimental.pallas import tpu as pltpu

F32 = jnp.float32
BF16 = jnp.bfloat16
I32 = jnp.int32

D_MODEL = 1024
BATCH = 32
SEQ = 256
DEPTH = 2
DEC_BATCH = 4
DEC_SEQ = 2048
PAST_LEN = 256
GRID_W = 64
HEAD_DIM = 64
ATTN_SCALE = HEAD_DIM ** -0.5
H_A = 4
W_A = H_A * HEAD_DIM
GN_EPS = 1e-5
H_B = 6
KV_B = 2
W_B = H_B * HEAD_DIM
KV_W_B = KV_B * HEAD_DIM
WINDOW = 128
ROPE_BASE = 10000.0
H_C = 6
W_C = H_C * HEAD_DIM
NA_ROWS = 8
NA_COLS = 16
IN_WIDTH = 4 * W_A + W_B + 2 * KV_W_B + 3 * W_C
N_EXPERTS = 64
TOP_K = 6
N_GROUPS = 8
GROUP_SIZE = N_EXPERTS // N_GROUPS
TOPK_GROUPS = 4
D_EXPERT = 256
D_SHARED = 256
ROUTE_SCALE = 2.5
ALPHA = (2 * DEPTH) ** 0.25
LN_EPS = 1e-5
NEG_INF = -1e30
PICKED = -3e38

T_CTX = BATCH * SEQ
T_LAT = DEC_BATCH * DEC_SEQ
T_ALL = T_CTX + T_LAT
N_COND = 8

PA_W = 4 * W_A
PB_W = W_B + 2 * KV_W_B
PC_W = 3 * W_C
CACHE_W = 2 * KV_W_B + 2 * W_C

LANES = 128
SUBLANES = 8
VMEM_LIMIT = 56 * 1024 * 1024

TM_PROJ = 512
TM_TOK = 256
RET_CHUNK = 256
WIN_QB = 128
NA_TILE_ROWS = 4
NA_KEY_ROWS = 11
BM = 256
N_ASSIGN = T_ALL * TOP_K
N_BLOCKS = (N_ASSIGN + N_EXPERTS * (BM - 1) + BM - 1) // BM
N_SLOTS = N_BLOCKS * BM
N_TOK_TILES = T_ALL // TM_TOK


def _dot(a, b):
    return jnp.dot(a, b, preferred_element_type=F32)


def _dot_nt(a, b):
    return lax.dot_general(a, b, (((1,), (1,)), ((), ())), preferred_element_type=F32)


def _silu(x):
    return x * jax.nn.sigmoid(x)


def _log_sigmoid(x):
    return jnp.minimum(x, 0.0) - jnp.log(1.0 + jnp.exp(-jnp.abs(x)))


def _cond_row(tile, tile_rows):
    n_ctx = T_CTX // tile_rows
    per_lat = DEC_SEQ // tile_rows
    return jnp.where(tile < n_ctx, 0, 1 + (tile - n_ctx) // per_lat)


def _layer_norm(x, g, b):
    mu = jnp.mean(x, -1, keepdims=True)
    xc = x - mu
    var = jnp.mean(xc * xc, -1, keepdims=True)
    return xc * lax.rsqrt(var + LN_EPS) * g + b


MOD_TN = 1536


def _mod_kernel(cond_ref, w_ref, b_ref, o_ref):
    s = _silu(cond_ref[...])
    s_hi = s.astype(BF16)
    s_lo = (s - s_hi.astype(F32)).astype(BF16)
    w = w_ref[0]
    w_hi = w.astype(BF16)
    w_lo = (w - w_hi.astype(F32)).astype(BF16)
    o_ref[0] = _dot(s_hi, w_hi) + _dot(s_lo, w_hi) + _dot(s_hi, w_lo) + b_ref[0]


def _modulation(cond, w_mod, b_mod):
    n_out = 6 * D_MODEL
    return pl.pallas_call(
        _mod_kernel,
        out_shape=jax.ShapeDtypeStruct((DEPTH, N_COND, n_out), F32),
        grid=(DEPTH, n_out // MOD_TN),
        in_specs=[
            pl.BlockSpec((N_COND, D_MODEL), lambda l, j: (0, 0)),
            pl.BlockSpec((1, D_MODEL, MOD_TN), lambda l, j: (l, 0, j)),
            pl.BlockSpec((1, 1, MOD_TN), lambda l, j: (l, 0, j)),
        ],
        out_specs=pl.BlockSpec((1, N_COND, MOD_TN), lambda l, j: (l, 0, j)),
        compiler_params=pltpu.CompilerParams(
            dimension_semantics=("arbitrary", "arbitrary"), vmem_limit_bytes=VMEM_LIMIT),
        name="modulation",
    )(cond, w_mod, b_mod.reshape(DEPTH, 1, n_out))


def _inproj_kernel(xc_ref, xl_ref, mod_ref, w_ref, pa_ref, pb_ref, pc_ref, cache_ref):
    i = pl.program_id(0)
    ci = _cond_row(i, TM_PROJ)
    sh = mod_ref[pl.ds(ci, 1), 0:D_MODEL]
    sc = mod_ref[pl.ds(ci, 1), D_MODEL:2 * D_MODEL]
    x = jnp.where(i < T_CTX // TM_PROJ, xc_ref[...], xl_ref[...])
    h = x * (1.0 + sc) + sh
    p = _dot(h.astype(BF16), w_ref[...])
    pa_ref[...] = p[:, 0:PA_W].astype(BF16)
    pb_ref[...] = p[:, PA_W:PA_W + PB_W].astype(BF16)
    pc_ref[...] = p[:, PA_W + PB_W:IN_WIDTH].astype(BF16)

    @pl.when(i < T_CTX // TM_PROJ)
    def _():
        cache_ref[:, 0:2 * KV_W_B] = p[:, PA_W + W_B:PA_W + PB_W]
        cache_ref[:, 2 * KV_W_B:CACHE_W] = p[:, PA_W + PB_W + W_C:IN_WIDTH]


def _in_projection(x_ctx, x_lat, mod, w_in_bf16):
    n_ctx_tiles = T_CTX // TM_PROJ
    return pl.pallas_call(
        _inproj_kernel,
        out_shape=(
            jax.ShapeDtypeStruct((T_ALL, PA_W), BF16),
            jax.ShapeDtypeStruct((T_ALL, PB_W), BF16),
            jax.ShapeDtypeStruct((T_ALL, PC_W), BF16),
            jax.ShapeDtypeStruct((T_CTX, CACHE_W), F32),
        ),
        grid=(T_ALL // TM_PROJ,),
        in_specs=[
            pl.BlockSpec((TM_PROJ, D_MODEL), lambda i: (jnp.minimum(i, n_ctx_tiles - 1), 0)),
            pl.BlockSpec((TM_PROJ, D_MODEL), lambda i: (jnp.maximum(i - n_ctx_tiles, 0), 0)),
            pl.BlockSpec((N_COND, 6 * D_MODEL), lambda i: (0, 0)),
            pl.BlockSpec((D_MODEL, IN_WIDTH), lambda i: (0, 0)),
        ],
        out_specs=(
            pl.BlockSpec((TM_PROJ, PA_W), lambda i: (i, 0)),
            pl.BlockSpec((TM_PROJ, PB_W), lambda i: (i, 0)),
            pl.BlockSpec((TM_PROJ, PC_W), lambda i: (i, 0)),
            pl.BlockSpec((TM_PROJ, CACHE_W), lambda i: (jnp.minimum(i, n_ctx_tiles - 1), 0)),
        ),
        compiler_params=pltpu.CompilerParams(
            dimension_semantics=("arbitrary",), vmem_limit_bytes=VMEM_LIMIT),
        name="in_projection",
    )(x_ctx, x_lat, mod, w_in_bf16)


def _decay_matrix(lg_f, lg_b, n):
    row = lax.broadcasted_iota(I32, (n, n), 0)
    col = lax.broadcasted_iota(I32, (n, n), 1)
    diff = (row - col).astype(F32)
    fwd = jnp.where(diff >= 0, jnp.exp(lg_f * jnp.maximum(diff, 0.0)), 0.0)
    bwd = jnp.where(diff <= 0, jnp.exp(lg_b * jnp.maximum(-diff, 0.0)), 0.0)
    return (fwd + bwd) * ATTN_SCALE


def _retention_readout(o, gate, g, b):
    mu = jnp.mean(o, -1, keepdims=True)
    oc = o - mu
    var = jnp.mean(oc * oc, -1, keepdims=True)
    on = oc * lax.rsqrt(var + GN_EPS) * g + b
    return on * _silu(gate.astype(F32))


def _softmax_attend(s, v, extra_logit=None):
    m = jnp.max(s, -1, keepdims=True)
    if extra_logit is not None:
        m = jnp.maximum(m, extra_logit)
    p = jnp.exp(s - m)
    den = jnp.sum(p, -1, keepdims=True)
    if extra_logit is not None:
        den = den + jnp.exp(extra_logit - m)
    return _dot(p.astype(BF16), v) / den


def _ctx_mixer_kernel(sink_ref, pa_ref, pb_ref, pc_ref, decf_ref, decb_ref, gng_ref, gnb_ref,
                      mix_ref, sf_ref, sb_ref):
    n = SEQ
    hd = HEAD_DIM
    pos = lax.broadcasted_iota(I32, (n, hd), 0).astype(F32)
    for h in range(H_A):
        c0 = h * hd
        q = pa_ref[:, c0:c0 + hd]
        k = pa_ref[:, W_A + c0:W_A + c0 + hd]
        v = pa_ref[:, 2 * W_A + c0:2 * W_A + c0 + hd]
        gate = pa_ref[:, 3 * W_A + c0:3 * W_A + c0 + hd]
        lg_f = _log_sigmoid(decf_ref[h])
        lg_b = _log_sigmoid(decb_ref[h])
        dmat = _decay_matrix(lg_f, lg_b, n)
        o = _dot((_dot_nt(q, k) * dmat).astype(BF16), v)
        kf = k.astype(F32)
        zf = jnp.exp(lg_f[:, 0:hd] * (n - 1.0 - pos)) * ATTN_SCALE
        zb = jnp.exp(lg_b[:, 0:hd] * pos) * ATTN_SCALE
        sf_ref[0, h] = _dot((kf * zf).T.astype(BF16), v)
        sb_ref[0, h] = _dot((kf * zb).T.astype(BF16), v)
        y = _retention_readout(o, gate, gng_ref[:, c0:c0 + hd], gnb_ref[:, c0:c0 + hd])
        mix_ref[:, c0:c0 + hd] = y.astype(BF16)
    for j in range(KV_B):
        k = pb_ref[:, W_B + j * hd:W_B + (j + 1) * hd]
        v = pb_ref[:, W_B + KV_W_B + j * hd:W_B + KV_W_B + (j + 1) * hd]
        for g in range(H_B // KV_B):
            hh = j * (H_B // KV_B) + g
            q = pb_ref[:, hh * hd:(hh + 1) * hd]
            o = _softmax_attend(_dot_nt(q, k) * ATTN_SCALE, v, sink_ref[hh])
            mix_ref[:, W_A + hh * hd:W_A + (hh + 1) * hd] = o.astype(BF16)
    for hh in range(H_C):
        q = pc_ref[:, hh * hd:(hh + 1) * hd]
        k = pc_ref[:, W_C + hh * hd:W_C + (hh + 1) * hd]
        v = pc_ref[:, 2 * W_C + hh * hd:2 * W_C + (hh + 1) * hd]
        o = _softmax_attend(_dot_nt(q, k) * ATTN_SCALE, v)
        mix_ref[:, W_A + W_B + hh * hd:W_A + W_B + (hh + 1) * hd] = o.astype(BF16)


def _ctx_mixers(pa, pb, pc, sink, decf, decb, gng, gnb):
    return pl.pallas_call(
        _ctx_mixer_kernel,
        out_shape=(
            jax.ShapeDtypeStruct((T_CTX, D_MODEL), BF16),
            jax.ShapeDtypeStruct((BATCH, H_A, HEAD_DIM, HEAD_DIM), F32),
            jax.ShapeDtypeStruct((BATCH, H_A, HEAD_DIM, HEAD_DIM), F32),
        ),
        grid=(BATCH,),
        in_specs=[
            pl.BlockSpec(memory_space=pltpu.SMEM),
            pl.BlockSpec((SEQ, PA_W), lambda b: (b, 0)),
            pl.BlockSpec((SEQ, PB_W), lambda b: (b, 0)),
            pl.BlockSpec((SEQ, PC_W), lambda b: (b, 0)),
            pl.BlockSpec((H_A, 1, SEQ), lambda b: (0, 0, 0)),
            pl.BlockSpec((H_A, 1, SEQ), lambda b: (0, 0, 0)),
            pl.BlockSpec((1, W_A), lambda b: (0, 0)),
            pl.BlockSpec((1, W_A), lambda b: (0, 0)),
        ],
        out_specs=(
            pl.BlockSpec((SEQ, D_MODEL), lambda b: (b, 0)),
            pl.BlockSpec((1, H_A, HEAD_DIM, HEAD_DIM), lambda b: (b, 0, 0, 0)),
            pl.BlockSpec((1, H_A, HEAD_DIM, HEAD_DIM), lambda b: (b, 0, 0, 0)),
        ),
        compiler_params=pltpu.CompilerParams(
            dimension_semantics=("arbitrary",), vmem_limit_bytes=VMEM_LIMIT),
        name="ctx_mixers",
    )(sink, pa, pb, pc, decf, decb, gng, gnb)


def _lat_ret_kernel(pa_ref, stf_ref, stb_ref, decf_ref, decb_ref, gng_ref, gnb_ref, ya_ref, acc_ref):
    c = RET_CHUNK
    hd = HEAD_DIM
    n_chunks = DEC_SEQ // c
    pos = lax.broadcasted_iota(I32, (c, hd), 0).astype(F32)
    for h in range(H_A):
        c0 = h * hd
        lg_f = _log_sigmoid(decf_ref[h])
        lg_b = _log_sigmoid(decb_ref[h])
        dmat = _decay_matrix(lg_f, lg_b, c)
        lf = lg_f[:, 0:hd]
        lb = lg_b[:, 0:hd]
        zf = jnp.exp(lf * (c - 1.0 - pos)) * ATTN_SCALE
        zb = jnp.exp(lb * pos) * ATTN_SCALE
        xf = jnp.exp(lf * (pos + 1.0))
        xb = jnp.exp(lb * (c - pos))
        gcf = jnp.exp(lf * float(c))
        gcb = jnp.exp(lb * float(c))
        g = gng_ref[:, c0:c0 + hd]
        b = gnb_ref[:, c0:c0 + hd]

        def load(i, off):
            rows = pl.ds(pl.multiple_of(i * c, c), c)
            return pa_ref[rows, off + c0:off + c0 + hd]

        def fwd(i, s):
            q, k, v = load(i, 0), load(i, W_A), load(i, 2 * W_A)
            o = _dot((_dot_nt(q, k) * dmat).astype(BF16), v)
            o = o + _dot((q.astype(F32) * xf).astype(BF16), s.astype(BF16))
            acc_ref[pl.ds(pl.multiple_of(i * c, c), c), c0:c0 + hd] = o
            return gcf * s + _dot((k.astype(F32) * zf).T.astype(BF16), v)

        lax.fori_loop(0, n_chunks, fwd, stf_ref[0, 0, h])

        def bwd(j, s):
            i = n_chunks - 1 - j
            rows = pl.ds(pl.multiple_of(i * c, c), c)
            q, k, v = load(i, 0), load(i, W_A), load(i, 2 * W_A)
            o = acc_ref[rows, c0:c0 + hd] + _dot((q.astype(F32) * xb).astype(BF16), s.astype(BF16))
            y = _retention_readout(o, load(i, 3 * W_A), g, b)
            ya_ref[rows, c0:c0 + hd] = y.astype(BF16)
            return gcb * s + _dot((k.astype(F32) * zb).T.astype(BF16), v)

        lax.fori_loop(0, n_chunks, bwd, stb_ref[0, 0, h])


def _lat_retention(pa, st_f, st_b, layer, decf, decb, gng, gnb):
    lat0 = T_CTX // DEC_SEQ
    st_spec = pl.BlockSpec((1, 1, H_A, HEAD_DIM, HEAD_DIM), lambda b: (b, layer, 0, 0, 0))
    return pl.pallas_call(
        _lat_ret_kernel,
        out_shape=jax.ShapeDtypeStruct((T_LAT, W_A), BF16),
        grid=(DEC_BATCH,),
        in_specs=[
            pl.BlockSpec((DEC_SEQ, PA_W), lambda b: (lat0 + b, 0)),
            st_spec, st_spec,
            pl.BlockSpec((H_A, 1, RET_CHUNK), lambda b: (0, 0, 0)),
            pl.BlockSpec((H_A, 1, RET_CHUNK), lambda b: (0, 0, 0)),
            pl.BlockSpec((1, W_A), lambda b: (0, 0)),
            pl.BlockSpec((1, W_A), lambda b: (0, 0)),
        ],
        out_specs=pl.BlockSpec((DEC_SEQ, W_A), lambda b: (b, 0)),
        scratch_shapes=[pltpu.VMEM((DEC_SEQ, W_A), F32)],
        compiler_params=pltpu.CompilerParams(
            dimension_semantics=("arbitrary",), vmem_limit_bytes=VMEM_LIMIT),
        name="lat_retention",
    )(pa, st_f, st_b, decf, decb, gng, gnb)


def _swap_halves_matrix(width):
    r = lax.broadcasted_iota(I32, (width, width), 0)
    c = lax.broadcasted_iota(I32, (width, width), 1)
    return jnp.where((r ^ (HEAD_DIM // 2)) == c, 1.0, 0.0).astype(BF16)


def _rope(x, cos, sin_signed, swap):
    return x.astype(F32) * cos + _dot(x, swap) * sin_signed


def _lat_win_kernel(sink_ref, pq_ref, pseq_ref, kctx_ref, vctx_ref, cos_ref, sin_ref, yb_ref, krope_ref):
    n = pl.program_id(1)
    hd = HEAD_DIM
    qb = WIN_QB
    n_blk = DEC_SEQ // qb
    group = H_B // KV_B
    swap = _swap_halves_matrix(LANES)

    @pl.when(n == 0)
    def _():
        k = pseq_ref[:, W_B:W_B + KV_W_B]
        krope_ref[...] = _rope(k, cos_ref[...], sin_ref[...], swap).astype(BF16)

    q_rows = pl.ds(pl.multiple_of(n * qb, qb), qb)
    cos_q = cos_ref[q_rows, :]
    sin_q = sin_ref[q_rows, :]
    qr = [_rope(pq_ref[:, p * LANES:(p + 1) * LANES], cos_q, sin_q, swap).astype(BF16)
          for p in range(W_B // LANES)]

    ws = jnp.clip(n - 1, 0, n_blk - 3) * qb
    k_rows = pl.ds(pl.multiple_of(ws, qb), 3 * qb)
    q_pos = n * qb + lax.broadcasted_iota(I32, (group * qb, 3 * qb), 0) % qb
    k_pos = ws + lax.broadcasted_iota(I32, (group * qb, 3 * qb), 1)
    valid = jnp.abs(k_pos - q_pos) <= WINDOW
    head_of_row = lax.broadcasted_iota(I32, (group * qb, 1), 0) // qb
    for j in range(KV_B):
        heads = [j * group + g for g in range(group)]
        qs = jnp.concatenate(
            [qr[hh // 2][:, (hh % 2) * hd:(hh % 2 + 1) * hd] for hh in heads], axis=0)
        kw = krope_ref[k_rows, j * hd:(j + 1) * hd]
        vw = pseq_ref[k_rows, W_B + KV_W_B + j * hd:W_B + KV_W_B + (j + 1) * hd]
        kc = kctx_ref[0, 0, j].astype(BF16)
        vc = vctx_ref[0, 0, j].astype(BF16)
        s_loc = jnp.where(valid, _dot_nt(qs, kw) * ATTN_SCALE, NEG_INF)
        s_ctx = _dot_nt(qs, kc) * ATTN_SCALE
        sink = jnp.zeros((group * qb, 1), F32)
        for g, hh in enumerate(heads):
            sink = jnp.where(head_of_row == g, sink_ref[hh], sink)
        m = jnp.maximum(jnp.maximum(jnp.max(s_loc, -1, keepdims=True),
                                    jnp.max(s_ctx, -1, keepdims=True)), sink)
        p_loc = jnp.exp(s_loc - m)
        p_ctx = jnp.exp(s_ctx - m)
        den = (jnp.sum(p_loc, -1, keepdims=True) + jnp.sum(p_ctx, -1, keepdims=True)
               + jnp.exp(sink - m))
        o = (_dot(p_loc.astype(BF16), vw) + _dot(p_ctx.astype(BF16), vc)) / den
        for g, hh in enumerate(heads):
            yb_ref[:, hh * hd:(hh + 1) * hd] = o[g * qb:(g + 1) * qb].astype(BF16)


def _lat_window_attn(pb, cache_k, cache_v, layer, sink, cos_t, sin_t):
    n_blk = DEC_SEQ // WIN_QB
    lat_blk0 = T_CTX // WIN_QB
    lat_seq0 = T_CTX // DEC_SEQ
    ctx_spec = pl.BlockSpec((1, 1, KV_B, PAST_LEN, HEAD_DIM), lambda b, n: (b, layer, 0, 0, 0))
    return pl.pallas_call(
        _lat_win_kernel,
        out_shape=jax.ShapeDtypeStruct((T_LAT, W_B), BF16),
        grid=(DEC_BATCH, n_blk),
        in_specs=[
            pl.BlockSpec(memory_space=pltpu.SMEM),
            pl.BlockSpec((WIN_QB, PB_W), lambda b, n: (lat_blk0 + b * n_blk + n, 0)),
            pl.BlockSpec((DEC_SEQ, PB_W), lambda b, n: (lat_seq0 + b, 0)),
            ctx_spec, ctx_spec,
            pl.BlockSpec((DEC_SEQ, LANES), lambda b, n: (0, 0)),
            pl.BlockSpec((DEC_SEQ, LANES), lambda b, n: (0, 0)),
        ],
        out_specs=pl.BlockSpec((WIN_QB, W_B), lambda b, n: (b * n_blk + n, 0)),
        scratch_shapes=[pltpu.VMEM((DEC_SEQ, KV_W_B), BF16)],
        compiler_params=pltpu.CompilerParams(
            dimension_semantics=("arbitrary", "arbitrary"), vmem_limit_bytes=VMEM_LIMIT),
        name="lat_window_attn",
    )(sink, pb, pb, cache_k, cache_v, cos_t, sin_t)


NA_Q = NA_TILE_ROWS * GRID_W
NA_K = NA_KEY_ROWS * GRID_W
NA_TILES = DEC_SEQ // NA_Q
LAT_ROWS = DEC_SEQ // GRID_W


def _na_window_start(tile):
    return jnp.clip(tile * NA_TILE_ROWS - NA_ROWS // 2, 0, LAT_ROWS - NA_KEY_ROWS)


def _lat_na_kernel(pq_ref, pseq_ref, kctx_ref, vctx_ref, bias_ref, yc_ref):
    t = pl.program_id(1)
    hd = HEAD_DIM
    k_rows = pl.ds(pl.multiple_of(_na_window_start(t) * GRID_W, GRID_W), NA_K)
    for hh in range(H_C):
        q = pq_ref[:, hh * hd:(hh + 1) * hd]
        kw = pseq_ref[k_rows, W_C + hh * hd:W_C + (hh + 1) * hd]
        vw = pseq_ref[k_rows, 2 * W_C + hh * hd:2 * W_C + (hh + 1) * hd]
        kc = kctx_ref[0, 0, hh].astype(BF16)
        vc = vctx_ref[0, 0, hh].astype(BF16)
        s_loc = _dot_nt(q, kw) * ATTN_SCALE + bias_ref[0, hh]
        s_ctx = _dot_nt(q, kc) * ATTN_SCALE
        m = jnp.maximum(jnp.max(s_loc, -1, keepdims=True), jnp.max(s_ctx, -1, keepdims=True))
        p_loc = jnp.exp(s_loc - m)
        p_ctx = jnp.exp(s_ctx - m)
        den = jnp.sum(p_loc, -1, keepdims=True) + jnp.sum(p_ctx, -1, keepdims=True)
        o = (_dot(p_loc.astype(BF16), vw) + _dot(p_ctx.astype(BF16), vc)) / den
        yc_ref[:, hh * hd:(hh + 1) * hd] = o.astype(BF16)


def _na_tile_type(t):
    return jnp.where(t == 0, 0, jnp.where(t == NA_TILES - 1, 2, 1))


def _lat_na_attn(pc, cache_k, cache_v, layer, maskbias):
    lat_tile0 = T_CTX // NA_Q
    lat_seq0 = T_CTX // DEC_SEQ
    ctx_spec = pl.BlockSpec((1, 1, H_C, PAST_LEN, HEAD_DIM), lambda b, t: (b, layer, 0, 0, 0))
    return pl.pallas_call(
        _lat_na_kernel,
        out_shape=jax.ShapeDtypeStruct((T_LAT, W_C), BF16),
        grid=(DEC_BATCH, NA_TILES),
        in_specs=[
            pl.BlockSpec((NA_Q, PC_W), lambda b, t: (lat_tile0 + b * NA_TILES + t, 0)),
            pl.BlockSpec((DEC_SEQ, PC_W), lambda b, t: (lat_seq0 + b, 0)),
            ctx_spec, ctx_spec,
            pl.BlockSpec((1, H_C, NA_Q, NA_K), lambda b, t: (_na_tile_type(t), 0, 0, 0)),
        ],
        out_specs=pl.BlockSpec((NA_Q, W_C), lambda b, t: (b * NA_TILES + t, 0)),
        compiler_params=pltpu.CompilerParams(
            dimension_semantics=("arbitrary", "arbitrary"), vmem_limit_bytes=VMEM_LIMIT),
        name="lat_na_attn",
    )(pc, pc, cache_k, cache_v, maskbias)


def _na_block_index():
    out = np.zeros((3, NA_TILE_ROWS, NA_KEY_ROWS), np.int32)
    for ty, tile in enumerate((0, 1, NA_TILES - 1)):
        r = tile * NA_TILE_ROWS
        ws = int(np.clip(r - NA_ROWS // 2, 0, LAT_ROWS - NA_KEY_ROWS))
        for qq in range(NA_TILE_ROWS):
            qr = r + qq
            r0 = int(np.clip(qr - NA_ROWS // 2, 0, LAT_ROWS - NA_ROWS))
            for kk in range(NA_KEY_ROWS):
                kr = ws + kk
                out[ty, qq, kk] = kr - qr + NA_ROWS - 1 if r0 <= kr < r0 + NA_ROWS else 2 * NA_ROWS - 1
    return out


def _na_maskbias(rpb):
    qc = np.arange(GRID_W)[:, None]
    kc = np.arange(GRID_W)[None, :]
    c0 = np.clip(qc - NA_COLS // 2, 0, GRID_W - NA_COLS)
    col_ok = (kc >= c0) & (kc < c0 + NA_COLS)
    ci = np.clip(kc - qc + NA_COLS - 1, 0, 2 * NA_COLS - 2)
    onehot = (ci[None] == np.arange(2 * NA_COLS - 1)[:, None, None]).astype(np.float32)
    cols = jnp.einsum("hab,bqk->haqk", rpb, jnp.asarray(onehot), precision=lax.Precision.HIGHEST)
    cols = jnp.where(jnp.asarray(col_ok)[None, None], cols, NEG_INF)
    cols = jnp.concatenate([cols, jnp.full((H_C, 1, GRID_W, GRID_W), NEG_INF, F32)], axis=1)
    blocks = jnp.take(cols, jnp.asarray(_na_block_index().reshape(-1)), axis=1)
    blocks = blocks.reshape(H_C, 3, NA_TILE_ROWS, NA_KEY_ROWS, GRID_W, GRID_W)
    return blocks.transpose(1, 0, 2, 4, 3, 5).reshape(3, H_C, NA_Q, NA_K)


def _rope_tables():
    t = np.arange(DEC_SEQ)
    n_freq = HEAD_DIM // 4
    inv = (ROPE_BASE ** (-np.arange(n_freq, dtype=np.float32) / n_freq)).astype(np.float32)
    row = (t // GRID_W).astype(np.float32)[:, None] * inv
    col = (t % GRID_W).astype(np.float32)[:, None] * inv
    ang = np.concatenate([row, col], -1)
    cos, sin = np.cos(ang), np.sin(ang)
    cos_h = np.concatenate([cos, cos], -1)
    sin_h = np.concatenate([-sin, sin], -1)
    reps = LANES // HEAD_DIM
    return (jnp.asarray(np.tile(cos_h, (1, reps)), F32), jnp.asarray(np.tile(sin_h, (1, reps)), F32))


def _first_index_of(mask, iota, sentinel):
    return jnp.min(jnp.where(mask, iota, sentinel), axis=0, keepdims=True)


def _route(logits, b_col):
    n = logits.shape[1]
    scores = jax.nn.sigmoid(logits)
    sel = scores + b_col
    io_g = lax.broadcasted_iota(I32, (GROUP_SIZE, n), 0)
    gs_rows = []
    for g in range(N_GROUPS):
        s = sel[g * GROUP_SIZE:(g + 1) * GROUP_SIZE]
        m1 = jnp.max(s, axis=0, keepdims=True)
        i1 = _first_index_of(s == m1, io_g, GROUP_SIZE)
        m2 = jnp.max(jnp.where(io_g == i1, PICKED, s), axis=0, keepdims=True)
        gs_rows.append(m1 + m2)
    gs = jnp.concatenate(gs_rows, axis=0)
    io_n = lax.broadcasted_iota(I32, (N_GROUPS, n), 0)
    gsel = jnp.zeros((N_GROUPS, n), F32)
    for _ in range(TOPK_GROUPS):
        mg = jnp.max(gs, axis=0, keepdims=True)
        gi = _first_index_of(gs == mg, io_n, N_GROUPS)
        hit = io_n == gi
        gsel = jnp.where(hit, 1.0, gsel)
        gs = jnp.where(hit, PICKED, gs)
    cand = jnp.concatenate(
        [jnp.where(gsel[g:g + 1] > 0.5, sel[g * GROUP_SIZE:(g + 1) * GROUP_SIZE], NEG_INF)
         for g in range(N_GROUPS)], axis=0)
    io_e = lax.broadcasted_iota(I32, (N_EXPERTS, n), 0)
    picks, raw = [], []
    for _ in range(TOP_K):
        mv = jnp.max(cand, axis=0, keepdims=True)
        ei = _first_index_of(cand == mv, io_e, N_EXPERTS)
        hit = io_e == ei
        picks.append((hit, ei))
        raw.append(jnp.sum(jnp.where(hit, scores, 0.0), axis=0, keepdims=True))
        cand = jnp.where(hit, PICKED, cand)
    return picks, raw


def _post_mixer_kernel(xc_ref, xl_ref, mixc_ref, ya_ref, yb_ref, yc_ref, wout_ref, mod_ref, g_ref, b_ref,
                       wrh_ref, wrl_ref, br_ref,
                       x1_ref, h2_ref, eidx_ref, wsel_ref, rank_ref, cnt_ref, base_ref):
    i = pl.program_id(0)
    tm = TM_TOK

    @pl.when(i == 0)
    def _():
        base_ref[...] = jnp.zeros_like(base_ref)

    ci = _cond_row(i, tm)
    gate1 = mod_ref[pl.ds(ci, 1), 2 * D_MODEL:3 * D_MODEL]
    sh2 = mod_ref[pl.ds(ci, 1), 3 * D_MODEL:4 * D_MODEL]
    sc2 = mod_ref[pl.ds(ci, 1), 4 * D_MODEL:5 * D_MODEL]
    mix_lat = jnp.concatenate([ya_ref[...], yb_ref[...], yc_ref[...]], axis=-1)
    mix = jnp.where(i < T_CTX // tm, mixc_ref[...], mix_lat)
    y = _dot(mix, wout_ref[...])
    x = jnp.where(i < T_CTX // tm, xc_ref[...], xl_ref[...])
    x1 = _layer_norm(ALPHA * x + gate1 * y, g_ref[...], b_ref[...])
    x1_ref[...] = x1
    h2 = x1 * (1.0 + sc2) + sh2
    h2_ref[...] = h2

    h_hi = h2.astype(BF16)
    h_lo = (h2 - h_hi.astype(F32)).astype(BF16)
    logits = (_dot_nt(wrh_ref[...], h_hi) + _dot_nt(wrh_ref[...], h_lo)
              + _dot_nt(wrl_ref[...], h_hi))
    picks, raw = _route(logits, br_ref[...])

    total = raw[0]
    for r in raw[1:]:
        total = total + r
    scale = ROUTE_SCALE / total
    multi = jnp.zeros((N_EXPERTS, tm), F32)
    for hit, _ in picks:
        multi = multi + jnp.where(hit, 1.0, 0.0)
    before = (lax.broadcasted_iota(I32, (tm, tm), 0) < lax.broadcasted_iota(I32, (tm, tm), 1))
    cum = _dot(multi.astype(BF16), jnp.where(before, 1.0, 0.0).astype(BF16)) + base_ref[:, 0:1]
    pad = jnp.zeros((SUBLANES - TOP_K, tm), F32)
    eidx_ref[...] = jnp.concatenate([ei for _, ei in picks] + [pad.astype(I32)], axis=0)
    wsel_ref[...] = jnp.concatenate([r * scale for r in raw] + [pad], axis=0)
    rank_ref[...] = jnp.concatenate(
        [jnp.sum(jnp.where(hit, cum, 0.0), axis=0, keepdims=True) for hit, _ in picks] + [pad],
        axis=0).astype(I32)
    base_ref[...] = base_ref[...] + jnp.sum(multi, axis=1, keepdims=True)
    cnt_ref[...] = base_ref[...]


def _post_mixer(x_ctx, x_lat, mix_c, ya, yb, yc, w_out_bf16, mod, ln_g, ln_b, wr_hi_t, wr_lo_t, b_router_col):
    n_ctx = T_CTX // TM_TOK
    ctx_map = lambda i: (jnp.minimum(i, n_ctx - 1), 0)
    lat_map = lambda i: (jnp.maximum(i - n_ctx, 0), 0)
    row_map = lambda i: (i, 0)
    const = lambda i: (0, 0)
    tok_map = lambda i: (0, i)
    return pl.pallas_call(
        _post_mixer_kernel,
        out_shape=(
            jax.ShapeDtypeStruct((T_ALL, D_MODEL), F32),
            jax.ShapeDtypeStruct((T_ALL, D_MODEL), F32),
            jax.ShapeDtypeStruct((SUBLANES, T_ALL), I32),
            jax.ShapeDtypeStruct((SUBLANES, T_ALL), F32),
            jax.ShapeDtypeStruct((SUBLANES, T_ALL), I32),
            jax.ShapeDtypeStruct((N_EXPERTS, LANES), F32),
        ),
        grid=(N_TOK_TILES,),
        in_specs=[
            pl.BlockSpec((TM_TOK, D_MODEL), ctx_map),
            pl.BlockSpec((TM_TOK, D_MODEL), lat_map),
            pl.BlockSpec((TM_TOK, D_MODEL), ctx_map),
            pl.BlockSpec((TM_TOK, W_A), lat_map),
            pl.BlockSpec((TM_TOK, W_B), lat_map),
            pl.BlockSpec((TM_TOK, W_C), lat_map),
            pl.BlockSpec((D_MODEL, D_MODEL), const),
            pl.BlockSpec((N_COND, 6 * D_MODEL), const),
            pl.BlockSpec((1, D_MODEL), const),
            pl.BlockSpec((1, D_MODEL), const),
            pl.BlockSpec((N_EXPERTS, D_MODEL), const),
            pl.BlockSpec((N_EXPERTS, D_MODEL), const),
            pl.BlockSpec((N_EXPERTS, 1), const),
        ],
        out_specs=(
            pl.BlockSpec((TM_TOK, D_MODEL), row_map),
            pl.BlockSpec((TM_TOK, D_MODEL), row_map),
            pl.BlockSpec((SUBLANES, TM_TOK), tok_map),
            pl.BlockSpec((SUBLANES, TM_TOK), tok_map),
            pl.BlockSpec((SUBLANES, TM_TOK), tok_map),
            pl.BlockSpec((N_EXPERTS, LANES), const),
        ),
        scratch_shapes=[pltpu.VMEM((N_EXPERTS, LANES), F32)],
        compiler_params=pltpu.CompilerParams(
            dimension_semantics=("arbitrary",), vmem_limit_bytes=VMEM_LIMIT),
        name="post_mixer",
    )(x_ctx, x_lat, mix_c, ya, yb, yc, w_out_bf16, mod, ln_g, ln_b, wr_hi_t, wr_lo_t, b_router_col)


ROWS_PER_TILE = TM_TOK * TOP_K


def _row_copy(src, src_row, dst, dst_row, sem):
    return pltpu.make_async_copy(src.at[pl.ds(src_row, 1)], dst.at[pl.ds(dst_row, 1)], sem)


def _dispatch_kernel(pend_ref, cnt_ref, h2_ref, dest_hbm, xs_hbm, idx_smem, zero_ref, sem_idx, sem_zero,
                     sem_rows):
    i = pl.program_id(0)

    def zero_copy(e):
        return pltpu.make_async_copy(
            zero_ref, xs_hbm.at[pl.ds(pl.multiple_of(pend_ref[e] - BM, BM), BM)], sem_zero)

    @pl.when(i == 0)
    def _():
        zero_ref[...] = jnp.zeros_like(zero_ref)

        def start(e, c):
            @pl.when(cnt_ref[e] > 0)
            def _():
                zero_copy(e).start()
            return c

        def wait(e, c):
            @pl.when(cnt_ref[e] > 0)
            def _():
                zero_copy(e).wait()
            return c

        def tail_copy(blk):
            return pltpu.make_async_copy(
                zero_ref, xs_hbm.at[pl.ds(pl.multiple_of(blk * BM, BM), BM)], sem_zero)

        def start_tail(blk, c):
            tail_copy(blk).start()
            return c

        def wait_tail(blk, c):
            tail_copy(blk).wait()
            return c

        n_used = pend_ref[N_EXPERTS - 1] // BM
        lax.fori_loop(0, N_EXPERTS, start, 0)
        lax.fori_loop(n_used, N_BLOCKS, start_tail, 0)
        lax.fori_loop(0, N_EXPERTS, wait, 0)
        lax.fori_loop(n_used, N_BLOCKS, wait_tail, 0)

    idx_copy = pltpu.make_async_copy(dest_hbm.at[i], idx_smem, sem_idx)
    idx_copy.start()
    idx_copy.wait()

    def issue(t, c):
        for k in range(TOP_K):
            _row_copy(h2_ref, t, xs_hbm, idx_smem[0, k * TM_TOK + t], sem_rows).start()
        return c

    lax.fori_loop(0, TM_TOK, issue, 0)
    pltpu.make_async_copy(xs_hbm.at[pl.ds(0, ROWS_PER_TILE)], xs_hbm.at[pl.ds(0, ROWS_PER_TILE)],
                          sem_rows).wait()


def _dispatch(h2, dest_tiles, pend, counts):
    return pl.pallas_call(
        _dispatch_kernel,
        out_shape=jax.ShapeDtypeStruct((N_SLOTS, D_MODEL), F32),
        grid_spec=pltpu.PrefetchScalarGridSpec(
            num_scalar_prefetch=2,
            grid=(N_TOK_TILES,),
            in_specs=[
                pl.BlockSpec((TM_TOK, D_MODEL), lambda i, pe, cn: (i, 0)),
                pl.BlockSpec(memory_space=pl.ANY),
            ],
            out_specs=pl.BlockSpec(memory_space=pl.ANY),
            scratch_shapes=[
                pltpu.SMEM((1, ROWS_PER_TILE), I32),
                pltpu.VMEM((BM, D_MODEL), F32),
                pltpu.SemaphoreType.DMA,
                pltpu.SemaphoreType.DMA,
                pltpu.SemaphoreType.DMA,
            ],
        ),
        compiler_params=pltpu.CompilerParams(
            dimension_semantics=("arbitrary",), vmem_limit_bytes=VMEM_LIMIT),
        name="moe_dispatch",
    )(pend, counts, h2, dest_tiles)


def _expert_kernel(blk_e_ref, n_used_ref, xs_ref, wgu_ref, wdown_ref, ys_ref):
    i = pl.program_id(0)

    @pl.when(i < n_used_ref[0])
    def _():
        x = xs_ref[...].astype(BF16)
        gu = _dot(x, wgu_ref[0].astype(BF16))
        act = _silu(gu[:, 0:D_EXPERT]) * gu[:, D_EXPERT:2 * D_EXPERT]
        ys_ref[...] = _dot(act.astype(BF16), wdown_ref[0].astype(BF16))

    @pl.when(i >= n_used_ref[0])
    def _():
        ys_ref[...] = jnp.zeros_like(ys_ref)


def _experts(xs, w_gu, w_down, blk_e, n_used):
    def blk(i, be, nu):
        return jnp.minimum(i, nu[0] - 1)

    return pl.pallas_call(
        _expert_kernel,
        out_shape=jax.ShapeDtypeStruct((N_SLOTS, D_MODEL), F32),
        grid_spec=pltpu.PrefetchScalarGridSpec(
            num_scalar_prefetch=2,
            grid=(N_BLOCKS,),
            in_specs=[
                pl.BlockSpec((BM, D_MODEL), lambda i, be, nu: (blk(i, be, nu), 0)),
                pl.BlockSpec((1, D_MODEL, 2 * D_EXPERT), lambda i, be, nu: (be[blk(i, be, nu)], 0, 0)),
                pl.BlockSpec((1, D_EXPERT, D_MODEL), lambda i, be, nu: (be[blk(i, be, nu)], 0, 0)),
            ],
            out_specs=pl.BlockSpec((BM, D_MODEL), lambda i, be, nu: (i, 0)),
        ),
        compiler_params=pltpu.CompilerParams(
            dimension_semantics=("arbitrary",), vmem_limit_bytes=VMEM_LIMIT),
        name="moe_experts",
    )(blk_e, n_used, xs, w_gu, w_down)


def _combine_kernel(x1_ref, h2_ref, wsel_ref, dest_hbm, ys_hbm, wsgu_ref, wsdown_ref, mod_ref, g_ref, b_ref,
                    outc_ref, outl_ref, idx_smem, rows_ref, sem_idx, sem_rows):
    i = pl.program_id(0)
    tm = TM_TOK
    idx_copy = pltpu.make_async_copy(dest_hbm.at[i], idx_smem, sem_idx)
    idx_copy.start()
    idx_copy.wait()

    def issue(r, c):
        _row_copy(ys_hbm, idx_smem[0, r], rows_ref, r, sem_rows).start()
        return c

    lax.fori_loop(0, ROWS_PER_TILE, issue, 0)

    sgu = _dot(h2_ref[...].astype(BF16), wsgu_ref[...])
    act = _silu(sgu[:, 0:D_SHARED]) * sgu[:, D_SHARED:2 * D_SHARED]
    f = _dot(act.astype(BF16), wsdown_ref[...])

    pltpu.make_async_copy(ys_hbm.at[pl.ds(0, ROWS_PER_TILE)], rows_ref, sem_rows).wait()
    w = wsel_ref[...]
    for k in range(TOP_K):
        f = f + w[:, k:k + 1] * rows_ref[k * tm:(k + 1) * tm, :]
    ci = _cond_row(i, tm)
    gate2 = mod_ref[pl.ds(ci, 1), 5 * D_MODEL:6 * D_MODEL]
    out = _layer_norm(ALPHA * x1_ref[...] + gate2 * f, g_ref[...], b_ref[...])

    @pl.when(i < T_CTX // tm)
    def _():
        outc_ref[...] = out

    @pl.when(i >= T_CTX // tm)
    def _():
        outl_ref[...] = out


def _combine(x1, h2, wsel_rows, dest_tiles, ys, w_sgu_bf16, w_sdown_bf16, mod, ln_g, ln_b):
    n_ctx = T_CTX // TM_TOK
    row_map = lambda i: (i, 0)
    const = lambda i: (0, 0)
    return pl.pallas_call(
        _combine_kernel,
        out_shape=(jax.ShapeDtypeStruct((T_CTX, D_MODEL), F32),
                   jax.ShapeDtypeStruct((T_LAT, D_MODEL), F32)),
        grid=(N_TOK_TILES,),
        in_specs=[
            pl.BlockSpec((TM_TOK, D_MODEL), row_map),
            pl.BlockSpec((TM_TOK, D_MODEL), row_map),
            pl.BlockSpec((TM_TOK, SUBLANES), row_map),
            pl.BlockSpec(memory_space=pl.ANY),
            pl.BlockSpec(memory_space=pl.ANY),
            pl.BlockSpec((D_MODEL, 2 * D_SHARED), const),
            pl.BlockSpec((D_SHARED, D_MODEL), const),
            pl.BlockSpec((N_COND, 6 * D_MODEL), const),
            pl.BlockSpec((1, D_MODEL), const),
            pl.BlockSpec((1, D_MODEL), const),
        ],
        out_specs=(pl.BlockSpec((TM_TOK, D_MODEL), lambda i: (jnp.minimum(i, n_ctx - 1), 0)),
                   pl.BlockSpec((TM_TOK, D_MODEL), lambda i: (jnp.maximum(i - n_ctx, 0), 0))),
        scratch_shapes=[
            pltpu.SMEM((1, ROWS_PER_TILE), I32),
            pltpu.VMEM((ROWS_PER_TILE, D_MODEL), F32),
            pltpu.SemaphoreType.DMA,
            pltpu.SemaphoreType.DMA,
        ],
        compiler_params=pltpu.CompilerParams(
            dimension_semantics=("arbitrary",), vmem_limit_bytes=VMEM_LIMIT),
        name="moe_combine",
    )(x1, h2, wsel_rows, dest_tiles, ys, w_sgu_bf16, w_sdown_bf16, mod, ln_g, ln_b)


BLK_LANES = (N_BLOCKS + LANES - 1) // LANES * LANES
NB_SPLIT = 16.0


def _slot_plan_kernel(eidx_ref, rank_ref, cnt_ref, dest_ref, pend_ref, blk_ref):
    cnt = cnt_ref[...]
    nb = jnp.floor((cnt + (BM - 1.0)) * (1.0 / BM))
    nb_hi = jnp.floor(nb * (1.0 / NB_SPLIT))
    nb_lo = nb - NB_SPLIT * nb_hi
    earlier = (lax.broadcasted_iota(I32, (N_EXPERTS, N_EXPERTS), 1)
               < lax.broadcasted_iota(I32, (N_EXPERTS, N_EXPERTS), 0))
    tri = jnp.where(earlier, 1.0, 0.0).astype(BF16)
    start_blk = NB_SPLIT * _dot(tri, nb_hi.astype(BF16)) + _dot(tri, nb_lo.astype(BF16))
    pstart = start_blk * BM
    pend = pstart + nb * BM
    pend_ref[...] = pend.astype(I32)
    blk_start = (lax.broadcasted_iota(I32, (N_EXPERTS, BLK_LANES), 1) * BM).astype(F32)
    owner = jnp.sum(jnp.where(pend[:, 0:1] <= blk_start, 1.0, 0.0), axis=0, keepdims=True)
    blk_ref[...] = jnp.minimum(owner, N_EXPERTS - 1.0).astype(I32)
    io_e = lax.broadcasted_iota(I32, (N_EXPERTS, TM_TOK), 0)
    rows = []
    for k in range(TOP_K):
        hit = io_e == eidx_ref[k:k + 1, :]
        base = jnp.sum(jnp.where(hit, pstart[:, 0:1], 0.0), axis=0, keepdims=True)
        rows.append(base.astype(I32) + rank_ref[k:k + 1, :])
    dest_ref[0] = jnp.concatenate(rows, axis=1)


def _slot_plan(eidx, rank, counts):
    tok_map = lambda i: (0, i)
    const = lambda i: (0, 0)
    dest_tiles, pend, blk = pl.pallas_call(
        _slot_plan_kernel,
        out_shape=(
            jax.ShapeDtypeStruct((N_TOK_TILES, 1, ROWS_PER_TILE), I32),
            jax.ShapeDtypeStruct((N_EXPERTS, LANES), I32),
            jax.ShapeDtypeStruct((1, BLK_LANES), I32),
        ),
        grid=(N_TOK_TILES,),
        in_specs=[
            pl.BlockSpec((SUBLANES, TM_TOK), tok_map),
            pl.BlockSpec((SUBLANES, TM_TOK), tok_map),
            pl.BlockSpec((N_EXPERTS, LANES), const),
        ],
        out_specs=(
            pl.BlockSpec((1, 1, ROWS_PER_TILE), lambda i: (i, 0, 0)),
            pl.BlockSpec((N_EXPERTS, LANES), const),
            pl.BlockSpec((1, BLK_LANES), const),
        ),
        compiler_params=pltpu.CompilerParams(dimension_semantics=("arbitrary",)),
        name="slot_plan",
    )(eidx, rank, counts)
    pend = pend[:, 0]
    n_used = (pend[N_EXPERTS - 1:] // BM)
    return dest_tiles, pend, counts[:, 0].astype(I32), blk[0, :N_BLOCKS], n_used


def _lane_rows(v, width):
    return jnp.broadcast_to(v.astype(F32)[:, None, None], (v.shape[0], 1, width))


def kernel(x_prompt, x_sample, state_ret_fwd, state_ret_bwd, cache_win_k, cache_win_v, cache_na_k, cache_na_v, c, c_ctx, w_in, w_out, ret_decay_fwd, ret_decay_bwd, ret_gn_g, ret_gn_b, win_sink, na_rpb, w_mod, b_mod, ln1_g, ln1_b, ln2_g, ln2_b, w_router, b_router, w_expert_gu, w_expert_down, w_shared_gu, w_shared_down):
    cond = jnp.concatenate(
        [c_ctx[None, :], c, jnp.zeros((N_COND - 1 - DEC_BATCH, D_MODEL), F32)], axis=0)
    mod_all = _modulation(cond, w_mod, b_mod)
    cos_t, sin_t = _rope_tables()

    x_ctx = x_prompt.reshape(T_CTX, D_MODEL)
    x_lat = x_sample.reshape(T_LAT, D_MODEL)
    sf_l, sb_l, cache_l = [], [], []
    for l in range(DEPTH):
        mod = mod_all[l]
        pa, pb, pc, cache = _in_projection(x_ctx, x_lat, mod, w_in[l].astype(BF16))
        cache_l.append(cache)
        decf_s, decb_s = _lane_rows(ret_decay_fwd[l], SEQ), _lane_rows(ret_decay_bwd[l], SEQ)
        gng, gnb = ret_gn_g[l][None, :], ret_gn_b[l][None, :]
        mix_c, st_f, st_b = _ctx_mixers(pa, pb, pc, win_sink[l], decf_s, decb_s, gng, gnb)
        sf_l.append(st_f)
        sb_l.append(st_b)
        ya = _lat_retention(pa, state_ret_fwd, state_ret_bwd, l,
                            _lane_rows(ret_decay_fwd[l], RET_CHUNK), _lane_rows(ret_decay_bwd[l], RET_CHUNK),
                            gng, gnb)
        yb = _lat_window_attn(pb, cache_win_k, cache_win_v, l, win_sink[l], cos_t, sin_t)
        yc = _lat_na_attn(pc, cache_na_k, cache_na_v, l, _na_maskbias(na_rpb[l]))

        wr_t = w_router[l].T
        wr_hi = wr_t.astype(BF16)
        wr_lo = (wr_t - wr_hi.astype(F32)).astype(BF16)
        x1, h2, eidx, wsel, rank, counts = _post_mixer(
            x_ctx, x_lat, mix_c, ya, yb, yc, w_out[l].astype(BF16), mod, ln1_g[l][None, :],
            ln1_b[l][None, :], wr_hi, wr_lo, b_router[l][:, None])
        dest_tiles, pend, cnt, blk_e, n_used = _slot_plan(eidx, rank, counts)
        xs = _dispatch(h2, dest_tiles, pend, cnt)
        ys = _experts(xs, w_expert_gu[l], w_expert_down[l], blk_e, n_used)
        x_ctx, x_lat = _combine(x1, h2, wsel.T, dest_tiles, ys, w_shared_gu[l].astype(BF16),
                                w_shared_down[l].astype(BF16), mod, ln2_g[l][None, :], ln2_b[l][None, :])

    y_prompt = x_ctx.reshape(BATCH, SEQ, D_MODEL)
    y_sample = x_lat.reshape(DEC_BATCH, DEC_SEQ, D_MODEL)
    new_sf = jnp.stack(sf_l, axis=1)
    new_sb = jnp.stack(sb_l, axis=1)

    def heads_of(cols, n_heads):
        per_layer = [cache[:, cols[0]:cols[1]].reshape(BATCH, SEQ, n_heads, HEAD_DIM).transpose(0, 2, 1, 3)
                     for cache in cache_l]
        return jnp.stack(per_layer, axis=1)

    new_win_k = heads_of((0, KV_W_B), KV_B)
    new_win_v = heads_of((KV_W_B, 2 * KV_W_B), KV_B)
    new_na_k = heads_of((2 * KV_W_B, 2 * KV_W_B + W_C), H_C)
    new_na_v = heads_of((2 * KV_W_B + W_C, CACHE_W), H_C)
    return (y_prompt, y_sample, new_sf, new_sb, new_win_k, new_win_v, new_na_k, new_na_v)
```

```python
import functools

import numpy as np
import jax
import jax.numpy as jnp
from jax import lax
from jax.experimental import pallas as pl
from jax.experimental.pallas import tpu as pltpu

F32 = jnp.float32
BF16 = jnp.bfloat16
I32 = jnp.int32

D_MODEL = 1024
BATCH = 32
SEQ = 256
DEPTH = 2
DEC_BATCH = 4
DEC_SEQ = 2048
PAST_LEN = 256
GRID_W = 64
HEAD_DIM = 64
ATTN_SCALE = HEAD_DIM ** -0.5
H_A = 4
W_A = H_A * HEAD_DIM
GN_EPS = 1e-5
H_B = 6
KV_B = 2
W_B = H_B * HEAD_DIM
KV_W_B = KV_B * HEAD_DIM
WINDOW = 128
ROPE_BASE = 10000.0
H_C = 6
W_C = H_C * HEAD_DIM
NA_ROWS = 8
NA_COLS = 16
IN_WIDTH = 4 * W_A + W_B + 2 * KV_W_B + 3 * W_C
N_EXPERTS = 64
TOP_K = 6
N_GROUPS = 8
GROUP_SIZE = N_EXPERTS // N_GROUPS
TOPK_GROUPS = 4
D_EXPERT = 256
D_SHARED = 256
ROUTE_SCALE = 2.5
ALPHA = (2 * DEPTH) ** 0.25
LN_EPS = 1e-5
NEG_INF = -1e30
PICKED = -3e38

T_CTX = BATCH * SEQ
T_LAT = DEC_BATCH * DEC_SEQ
T_ALL = T_CTX + T_LAT
N_COND = 8

PA_W = 4 * W_A
PB_W = W_B + 2 * KV_W_B
PC_W = 3 * W_C
CACHE_W = 2 * KV_W_B + 2 * W_C

LANES = 128
SUBLANES = 8
VMEM_LIMIT = 56 * 1024 * 1024

TM_PROJ = 512
TM_TOK = 256
RET_CHUNK = 256
WIN_QB = 128
NA_TILE_ROWS = 4
NA_KEY_ROWS = 11
BM = 256
UNIT = SUBLANES
UNITS_PER_BLOCK = BM // UNIT
PERM_CHUNK = 256
N_TOK_TILES = T_ALL // TM_TOK
LOCAL_ROWS = -(-(TM_TOK * TOP_K + N_EXPERTS * (UNIT - 1)) // PERM_CHUNK) * PERM_CHUNK
MAX_UNITS = LOCAL_ROWS // UNIT
N_ASSIGN = T_ALL * TOP_K
N_BLOCKS = -(-(N_ASSIGN + N_TOK_TILES * N_EXPERTS * (UNIT - 1) + N_EXPERTS * (BM - 1)) // BM)
N_SLOTS = N_BLOCKS * BM
BLK_LANES = -(-N_BLOCKS // LANES) * LANES


def _dot(a, b):
    return jnp.dot(a, b, preferred_element_type=F32)


def _dot_nt(a, b):
    return lax.dot_general(a, b, (((1,), (1,)), ((), ())), preferred_element_type=F32)


def _silu(x):
    return x * jax.nn.sigmoid(x)


def _log_sigmoid(x):
    return jnp.minimum(x, 0.0) - jnp.log(1.0 + jnp.exp(-jnp.abs(x)))


def _cond_row(tile, tile_rows):
    n_ctx = T_CTX // tile_rows
    per_lat = DEC_SEQ // tile_rows
    return jnp.where(tile < n_ctx, 0, 1 + (tile - n_ctx) // per_lat)


def _layer_norm(x, g, b):
    mu = jnp.mean(x, -1, keepdims=True)
    xc = x - mu
    var = jnp.mean(xc * xc, -1, keepdims=True)
    return xc * lax.rsqrt(var + LN_EPS) * g + b


MOD_TN = 1536


def _mod_kernel(cond_ref, w_ref, b_ref, o_ref):
    s = _silu(cond_ref[...])
    s_hi = s.astype(BF16)
    s_lo = (s - s_hi.astype(F32)).astype(BF16)
    w = w_ref[0]
    w_hi = w.astype(BF16)
    w_lo = (w - w_hi.astype(F32)).astype(BF16)
    o_ref[0] = _dot(s_hi, w_hi) + _dot(s_lo, w_hi) + _dot(s_hi, w_lo) + b_ref[0]


def _modulation(cond, w_mod, b_mod):
    n_out = 6 * D_MODEL
    return pl.pallas_call(
        _mod_kernel,
        out_shape=jax.ShapeDtypeStruct((DEPTH, N_COND, n_out), F32),
        grid=(DEPTH, n_out // MOD_TN),
        in_specs=[
            pl.BlockSpec((N_COND, D_MODEL), lambda l, j: (0, 0)),
            pl.BlockSpec((1, D_MODEL, MOD_TN), lambda l, j: (l, 0, j)),
            pl.BlockSpec((1, 1, MOD_TN), lambda l, j: (l, 0, j)),
        ],
        out_specs=pl.BlockSpec((1, N_COND, MOD_TN), lambda l, j: (l, 0, j)),
        compiler_params=pltpu.CompilerParams(
            dimension_semantics=("arbitrary", "arbitrary"), vmem_limit_bytes=VMEM_LIMIT),
        name="modulation",
    )(cond, w_mod, b_mod.reshape(DEPTH, 1, n_out))


def _inproj_kernel(xc_ref, xl_ref, mod_ref, w_ref, pa_ref, pb_ref, pc_ref, cache_ref):
    i = pl.program_id(0)
    ci = _cond_row(i, TM_PROJ)
    sh = mod_ref[pl.ds(ci, 1), 0:D_MODEL]
    sc = mod_ref[pl.ds(ci, 1), D_MODEL:2 * D_MODEL]
    x = jnp.where(i < T_CTX // TM_PROJ, xc_ref[...], xl_ref[...])
    h = x * (1.0 + sc) + sh
    p = _dot(h.astype(BF16), w_ref[...])
    pa_ref[...] = p[:, 0:PA_W].astype(BF16)
    pb_ref[...] = p[:, PA_W:PA_W + PB_W].astype(BF16)
    pc_ref[...] = p[:, PA_W + PB_W:IN_WIDTH].astype(BF16)

    @pl.when(i < T_CTX // TM_PROJ)
    def _():
        cache_ref[:, 0:2 * KV_W_B] = p[:, PA_W + W_B:PA_W + PB_W]
        cache_ref[:, 2 * KV_W_B:CACHE_W] = p[:, PA_W + PB_W + W_C:IN_WIDTH]


def _in_projection(x_ctx, x_lat, mod, w_in_bf16):
    n_ctx_tiles = T_CTX // TM_PROJ
    return pl.pallas_call(
        _inproj_kernel,
        out_shape=(
            jax.ShapeDtypeStruct((T_ALL, PA_W), BF16),
            jax.ShapeDtypeStruct((T_ALL, PB_W), BF16),
            jax.ShapeDtypeStruct((T_ALL, PC_W), BF16),
            jax.ShapeDtypeStruct((T_CTX, CACHE_W), F32),
        ),
        grid=(T_ALL // TM_PROJ,),
        in_specs=[
            pl.BlockSpec((TM_PROJ, D_MODEL), lambda i: (jnp.minimum(i, n_ctx_tiles - 1), 0)),
            pl.BlockSpec((TM_PROJ, D_MODEL), lambda i: (jnp.maximum(i - n_ctx_tiles, 0), 0)),
            pl.BlockSpec((N_COND, 6 * D_MODEL), lambda i: (0, 0)),
            pl.BlockSpec((D_MODEL, IN_WIDTH), lambda i: (0, 0)),
        ],
        out_specs=(
            pl.BlockSpec((TM_PROJ, PA_W), lambda i: (i, 0)),
            pl.BlockSpec((TM_PROJ, PB_W), lambda i: (i, 0)),
            pl.BlockSpec((TM_PROJ, PC_W), lambda i: (i, 0)),
            pl.BlockSpec((TM_PROJ, CACHE_W), lambda i: (jnp.minimum(i, n_ctx_tiles - 1), 0)),
        ),
        compiler_params=pltpu.CompilerParams(
            dimension_semantics=("arbitrary",), vmem_limit_bytes=VMEM_LIMIT),
        name="in_projection",
    )(x_ctx, x_lat, mod, w_in_bf16)


def _decay_matrix(lg_f, lg_b, n):
    row = lax.broadcasted_iota(I32, (n, n), 0)
    col = lax.broadcasted_iota(I32, (n, n), 1)
    diff = (row - col).astype(F32)
    fwd = jnp.where(diff >= 0, jnp.exp(lg_f * jnp.maximum(diff, 0.0)), 0.0)
    bwd = jnp.where(diff <= 0, jnp.exp(lg_b * jnp.maximum(-diff, 0.0)), 0.0)
    return (fwd + bwd) * ATTN_SCALE


def _retention_readout(o, gate, g, b):
    mu = jnp.mean(o, -1, keepdims=True)
    oc = o - mu
    var = jnp.mean(oc * oc, -1, keepdims=True)
    on = oc * lax.rsqrt(var + GN_EPS) * g + b
    return on * _silu(gate.astype(F32))


def _softmax_attend(s, v, extra_logit=None):
    m = jnp.max(s, -1, keepdims=True)
    if extra_logit is not None:
        m = jnp.maximum(m, extra_logit)
    p = jnp.exp(s - m)
    den = jnp.sum(p, -1, keepdims=True)
    if extra_logit is not None:
        den = den + jnp.exp(extra_logit - m)
    return _dot(p.astype(BF16), v) / den


def _ctx_mixer_kernel(sink_ref, pa_ref, pb_ref, pc_ref, decf_ref, decb_ref, gng_ref, gnb_ref,
                      mix_ref, sf_ref, sb_ref):
    n = SEQ
    hd = HEAD_DIM
    pos = lax.broadcasted_iota(I32, (n, hd), 0).astype(F32)
    for h in range(H_A):
        c0 = h * hd
        q = pa_ref[:, c0:c0 + hd]
        k = pa_ref[:, W_A + c0:W_A + c0 + hd]
        v = pa_ref[:, 2 * W_A + c0:2 * W_A + c0 + hd]
        gate = pa_ref[:, 3 * W_A + c0:3 * W_A + c0 + hd]
        lg_f = _log_sigmoid(decf_ref[h])
        lg_b = _log_sigmoid(decb_ref[h])
        dmat = _decay_matrix(lg_f, lg_b, n)
        o = _dot((_dot_nt(q, k) * dmat).astype(BF16), v)
        kf = k.astype(F32)
        zf = jnp.exp(lg_f[:, 0:hd] * (n - 1.0 - pos)) * ATTN_SCALE
        zb = jnp.exp(lg_b[:, 0:hd] * pos) * ATTN_SCALE
        sf_ref[0, h] = _dot((kf * zf).T.astype(BF16), v)
        sb_ref[0, h] = _dot((kf * zb).T.astype(BF16), v)
        y = _retention_readout(o, gate, gng_ref[:, c0:c0 + hd], gnb_ref[:, c0:c0 + hd])
        mix_ref[:, c0:c0 + hd] = y.astype(BF16)
    for j in range(KV_B):
        k = pb_ref[:, W_B + j * hd:W_B + (j + 1) * hd]
        v = pb_ref[:, W_B + KV_W_B + j * hd:W_B + KV_W_B + (j + 1) * hd]
        for g in range(H_B // KV_B):
            hh = j * (H_B // KV_B) + g
            q = pb_ref[:, hh * hd:(hh + 1) * hd]
            o = _softmax_attend(_dot_nt(q, k) * ATTN_SCALE, v, sink_ref[hh])
            mix_ref[:, W_A + hh * hd:W_A + (hh + 1) * hd] = o.astype(BF16)
    for hh in range(H_C):
        q = pc_ref[:, hh * hd:(hh + 1) * hd]
        k = pc_ref[:, W_C + hh * hd:W_C + (hh + 1) * hd]
        v = pc_ref[:, 2 * W_C + hh * hd:2 * W_C + (hh + 1) * hd]
        o = _softmax_attend(_dot_nt(q, k) * ATTN_SCALE, v)
        mix_ref[:, W_A + W_B + hh * hd:W_A + W_B + (hh + 1) * hd] = o.astype(BF16)


def _ctx_mixers(pa, pb, pc, sink, decf, decb, gng, gnb):
    return pl.pallas_call(
        _ctx_mixer_kernel,
        out_shape=(
            jax.ShapeDtypeStruct((T_CTX, D_MODEL), BF16),
            jax.ShapeDtypeStruct((BATCH, H_A, HEAD_DIM, HEAD_DIM), F32),
            jax.ShapeDtypeStruct((BATCH, H_A, HEAD_DIM, HEAD_DIM), F32),
        ),
        grid=(BATCH,),
        in_specs=[
            pl.BlockSpec(memory_space=pltpu.SMEM),
            pl.BlockSpec((SEQ, PA_W), lambda b: (b, 0)),
            pl.BlockSpec((SEQ, PB_W), lambda b: (b, 0)),
            pl.BlockSpec((SEQ, PC_W), lambda b: (b, 0)),
            pl.BlockSpec((H_A, 1, SEQ), lambda b: (0, 0, 0)),
            pl.BlockSpec((H_A, 1, SEQ), lambda b: (0, 0, 0)),
            pl.BlockSpec((1, W_A), lambda b: (0, 0)),
            pl.BlockSpec((1, W_A), lambda b: (0, 0)),
        ],
        out_specs=(
            pl.BlockSpec((SEQ, D_MODEL), lambda b: (b, 0)),
            pl.BlockSpec((1, H_A, HEAD_DIM, HEAD_DIM), lambda b: (b, 0, 0, 0)),
            pl.BlockSpec((1, H_A, HEAD_DIM, HEAD_DIM), lambda b: (b, 0, 0, 0)),
        ),
        compiler_params=pltpu.CompilerParams(
            dimension_semantics=("arbitrary",), vmem_limit_bytes=VMEM_LIMIT),
        name="ctx_mixers",
    )(sink, pa, pb, pc, decf, decb, gng, gnb)


def _lat_ret_kernel(pa_ref, stf_ref, stb_ref, decf_ref, decb_ref, gng_ref, gnb_ref, ya_ref, acc_ref):
    c = RET_CHUNK
    hd = HEAD_DIM
    n_chunks = DEC_SEQ // c
    pos = lax.broadcasted_iota(I32, (c, hd), 0).astype(F32)
    for h in range(H_A):
        c0 = h * hd
        lg_f = _log_sigmoid(decf_ref[h])
        lg_b = _log_sigmoid(decb_ref[h])
        dmat = _decay_matrix(lg_f, lg_b, c)
        lf = lg_f[:, 0:hd]
        lb = lg_b[:, 0:hd]
        zf = jnp.exp(lf * (c - 1.0 - pos)) * ATTN_SCALE
        zb = jnp.exp(lb * pos) * ATTN_SCALE
        xf = jnp.exp(lf * (pos + 1.0))
        xb = jnp.exp(lb * (c - pos))
        gcf = jnp.exp(lf * float(c))
        gcb = jnp.exp(lb * float(c))
        g = gng_ref[:, c0:c0 + hd]
        b = gnb_ref[:, c0:c0 + hd]

        def load(i, off):
            rows = pl.ds(pl.multiple_of(i * c, c), c)
            return pa_ref[rows, off + c0:off + c0 + hd]

        def fwd(i, s):
            q, k, v = load(i, 0), load(i, W_A), load(i, 2 * W_A)
            o = _dot((_dot_nt(q, k) * dmat).astype(BF16), v)
            o = o + _dot((q.astype(F32) * xf).astype(BF16), s.astype(BF16))
            acc_ref[pl.ds(pl.multiple_of(i * c, c), c), c0:c0 + hd] = o
            return gcf * s + _dot((k.astype(F32) * zf).T.astype(BF16), v)

        lax.fori_loop(0, n_chunks, fwd, stf_ref[0, 0, h])

        def bwd(j, s):
            i = n_chunks - 1 - j
            rows = pl.ds(pl.multiple_of(i * c, c), c)
            q, k, v = load(i, 0), load(i, W_A), load(i, 2 * W_A)
            o = acc_ref[rows, c0:c0 + hd] + _dot((q.astype(F32) * xb).astype(BF16), s.astype(BF16))
            y = _retention_readout(o, load(i, 3 * W_A), g, b)
            ya_ref[rows, c0:c0 + hd] = y.astype(BF16)
            return gcb * s + _dot((k.astype(F32) * zb).T.astype(BF16), v)

        lax.fori_loop(0, n_chunks, bwd, stb_ref[0, 0, h])


def _lat_retention(pa, st_f, st_b, layer, decf, decb, gng, gnb):
    lat0 = T_CTX // DEC_SEQ
    st_spec = pl.BlockSpec((1, 1, H_A, HEAD_DIM, HEAD_DIM), lambda b: (b, layer, 0, 0, 0))
    return pl.pallas_call(
        _lat_ret_kernel,
        out_shape=jax.ShapeDtypeStruct((T_LAT, W_A), BF16),
        grid=(DEC_BATCH,),
        in_specs=[
            pl.BlockSpec((DEC_SEQ, PA_W), lambda b: (lat0 + b, 0)),
            st_spec, st_spec,
            pl.BlockSpec((H_A, 1, RET_CHUNK), lambda b: (0, 0, 0)),
            pl.BlockSpec((H_A, 1, RET_CHUNK), lambda b: (0, 0, 0)),
            pl.BlockSpec((1, W_A), lambda b: (0, 0)),
            pl.BlockSpec((1, W_A), lambda b: (0, 0)),
        ],
        out_specs=pl.BlockSpec((DEC_SEQ, W_A), lambda b: (b, 0)),
        scratch_shapes=[pltpu.VMEM((DEC_SEQ, W_A), F32)],
        compiler_params=pltpu.CompilerParams(
            dimension_semantics=("arbitrary",), vmem_limit_bytes=VMEM_LIMIT),
        name="lat_retention",
    )(pa, st_f, st_b, decf, decb, gng, gnb)


def _swap_halves_matrix(width):
    r = lax.broadcasted_iota(I32, (width, width), 0)
    c = lax.broadcasted_iota(I32, (width, width), 1)
    return jnp.where((r ^ (HEAD_DIM // 2)) == c, 1.0, 0.0).astype(BF16)


def _rope(x, cos, sin_signed, swap):
    return x.astype(F32) * cos + _dot(x, swap) * sin_signed


def _lat_win_kernel(sink_ref, pq_ref, pseq_ref, kctx_ref, vctx_ref, cos_ref, sin_ref, yb_ref, krope_ref):
    n = pl.program_id(1)
    hd = HEAD_DIM
    qb = WIN_QB
    n_blk = DEC_SEQ // qb
    group = H_B // KV_B
    swap = _swap_halves_matrix(LANES)

    @pl.when(n == 0)
    def _():
        k = pseq_ref[:, W_B:W_B + KV_W_B]
        krope_ref[...] = _rope(k, cos_ref[...], sin_ref[...], swap).astype(BF16)

    q_rows = pl.ds(pl.multiple_of(n * qb, qb), qb)
    cos_q = cos_ref[q_rows, :]
    sin_q = sin_ref[q_rows, :]
    qr = [_rope(pq_ref[:, p * LANES:(p + 1) * LANES], cos_q, sin_q, swap).astype(BF16)
          for p in range(W_B // LANES)]

    ws = jnp.clip(n - 1, 0, n_blk - 3) * qb
    k_rows = pl.ds(pl.multiple_of(ws, qb), 3 * qb)
    q_pos = n * qb + lax.broadcasted_iota(I32, (group * qb, 3 * qb), 0) % qb
    k_pos = ws + lax.broadcasted_iota(I32, (group * qb, 3 * qb), 1)
    valid = jnp.abs(k_pos - q_pos) <= WINDOW
    head_of_row = lax.broadcasted_iota(I32, (group * qb, 1), 0) // qb
    for j in range(KV_B):
        heads = [j * group + g for g in range(group)]
        qs = jnp.concatenate(
            [qr[hh // 2][:, (hh % 2) * hd:(hh % 2 + 1) * hd] for hh in heads], axis=0)
        kw = krope_ref[k_rows, j * hd:(j + 1) * hd]
        vw = pseq_ref[k_rows, W_B + KV_W_B + j * hd:W_B + KV_W_B + (j + 1) * hd]
        kc = kctx_ref[0, 0, j].astype(BF16)
        vc = vctx_ref[0, 0, j].astype(BF16)
        s_loc = jnp.where(valid, _dot_nt(qs, kw) * ATTN_SCALE, NEG_INF)
        s_ctx = _dot_nt(qs, kc) * ATTN_SCALE
        sink = jnp.zeros((group * qb, 1), F32)
        for g, hh in enumerate(heads):
            sink = jnp.where(head_of_row == g, sink_ref[hh], sink)
        m = jnp.maximum(jnp.maximum(jnp.max(s_loc, -1, keepdims=True),
                                    jnp.max(s_ctx, -1, keepdims=True)), sink)
        p_loc = jnp.exp(s_loc - m)
        p_ctx = jnp.exp(s_ctx - m)
        den = (jnp.sum(p_loc, -1, keepdims=True) + jnp.sum(p_ctx, -1, keepdims=True)
               + jnp.exp(sink - m))
        o = (_dot(p_loc.astype(BF16), vw) + _dot(p_ctx.astype(BF16), vc)) / den
        for g, hh in enumerate(heads):
            yb_ref[:, hh * hd:(hh + 1) * hd] = o[g * qb:(g + 1) * qb].astype(BF16)


def _lat_window_attn(pb, cache_k, cache_v, layer, sink, cos_t, sin_t):
    n_blk = DEC_SEQ // WIN_QB
    lat_blk0 = T_CTX // WIN_QB
    lat_seq0 = T_CTX // DEC_SEQ
    ctx_spec = pl.BlockSpec((1, 1, KV_B, PAST_LEN, HEAD_DIM), lambda b, n: (b, layer, 0, 0, 0))
    return pl.pallas_call(
        _lat_win_kernel,
        out_shape=jax.ShapeDtypeStruct((T_LAT, W_B), BF16),
        grid=(DEC_BATCH, n_blk),
        in_specs=[
            pl.BlockSpec(memory_space=pltpu.SMEM),
            pl.BlockSpec((WIN_QB, PB_W), lambda b, n: (lat_blk0 + b * n_blk + n, 0)),
            pl.BlockSpec((DEC_SEQ, PB_W), lambda b, n: (lat_seq0 + b, 0)),
            ctx_spec, ctx_spec,
            pl.BlockSpec((DEC_SEQ, LANES), lambda b, n: (0, 0)),
            pl.BlockSpec((DEC_SEQ, LANES), lambda b, n: (0, 0)),
        ],
        out_specs=pl.BlockSpec((WIN_QB, W_B), lambda b, n: (b * n_blk + n, 0)),
        scratch_shapes=[pltpu.VMEM((DEC_SEQ, KV_W_B), BF16)],
        compiler_params=pltpu.CompilerParams(
            dimension_semantics=("arbitrary", "arbitrary"), vmem_limit_bytes=VMEM_LIMIT),
        name="lat_window_attn",
    )(sink, pb, pb, cache_k, cache_v, cos_t, sin_t)


NA_Q = NA_TILE_ROWS * GRID_W
NA_K = NA_KEY_ROWS * GRID_W
NA_TILES = DEC_SEQ // NA_Q
LAT_ROWS = DEC_SEQ // GRID_W


def _na_window_start(tile):
    return jnp.clip(tile * NA_TILE_ROWS - NA_ROWS // 2, 0, LAT_ROWS - NA_KEY_ROWS)


def _lat_na_kernel(pq_ref, pseq_ref, kctx_ref, vctx_ref, bias_ref, yc_ref):
    t = pl.program_id(1)
    hd = HEAD_DIM
    k_rows = pl.ds(pl.multiple_of(_na_window_start(t) * GRID_W, GRID_W), NA_K)
    for hh in range(H_C):
        q = pq_ref[:, hh * hd:(hh + 1) * hd]
        kw = pseq_ref[k_rows, W_C + hh * hd:W_C + (hh + 1) * hd]
        vw = pseq_ref[k_rows, 2 * W_C + hh * hd:2 * W_C + (hh + 1) * hd]
        kc = kctx_ref[0, 0, hh].astype(BF16)
        vc = vctx_ref[0, 0, hh].astype(BF16)
        s_loc = _dot_nt(q, kw) * ATTN_SCALE + bias_ref[0, hh]
        s_ctx = _dot_nt(q, kc) * ATTN_SCALE
        m = jnp.maximum(jnp.max(s_loc, -1, keepdims=True), jnp.max(s_ctx, -1, keepdims=True))
        p_loc = jnp.exp(s_loc - m)
        p_ctx = jnp.exp(s_ctx - m)
        den = jnp.sum(p_loc, -1, keepdims=True) + jnp.sum(p_ctx, -1, keepdims=True)
        o = (_dot(p_loc.astype(BF16), vw) + _dot(p_ctx.astype(BF16), vc)) / den
        yc_ref[:, hh * hd:(hh + 1) * hd] = o.astype(BF16)


def _na_tile_type(t):
    return jnp.where(t == 0, 0, jnp.where(t == NA_TILES - 1, 2, 1))


def _lat_na_attn(pc, cache_k, cache_v, layer, maskbias):
    lat_tile0 = T_CTX // NA_Q
    lat_seq0 = T_CTX // DEC_SEQ
    ctx_spec = pl.BlockSpec((1, 1, H_C, PAST_LEN, HEAD_DIM), lambda b, t: (b, layer, 0, 0, 0))
    return pl.pallas_call(
        _lat_na_kernel,
        out_shape=jax.ShapeDtypeStruct((T_LAT, W_C), BF16),
        grid=(DEC_BATCH, NA_TILES),
        in_specs=[
            pl.BlockSpec((NA_Q, PC_W), lambda b, t: (lat_tile0 + b * NA_TILES + t, 0)),
            pl.BlockSpec((DEC_SEQ, PC_W), lambda b, t: (lat_seq0 + b, 0)),
            ctx_spec, ctx_spec,
            pl.BlockSpec((1, H_C, NA_Q, NA_K), lambda b, t: (_na_tile_type(t), 0, 0, 0)),
        ],
        out_specs=pl.BlockSpec((NA_Q, W_C), lambda b, t: (b * NA_TILES + t, 0)),
        compiler_params=pltpu.CompilerParams(
            dimension_semantics=("arbitrary", "arbitrary"), vmem_limit_bytes=VMEM_LIMIT),
        name="lat_na_attn",
    )(pc, pc, cache_k, cache_v, maskbias)


def _na_block_index():
    out = np.zeros((3, NA_TILE_ROWS, NA_KEY_ROWS), np.int32)
    for ty, tile in enumerate((0, 1, NA_TILES - 1)):
        r = tile * NA_TILE_ROWS
        ws = int(np.clip(r - NA_ROWS // 2, 0, LAT_ROWS - NA_KEY_ROWS))
        for qq in range(NA_TILE_ROWS):
            qr = r + qq
            r0 = int(np.clip(qr - NA_ROWS // 2, 0, LAT_ROWS - NA_ROWS))
            for kk in range(NA_KEY_ROWS):
                kr = ws + kk
                out[ty, qq, kk] = kr - qr + NA_ROWS - 1 if r0 <= kr < r0 + NA_ROWS else 2 * NA_ROWS - 1
    return out


def _na_maskbias(rpb):
    qc = np.arange(GRID_W)[:, None]
    kc = np.arange(GRID_W)[None, :]
    c0 = np.clip(qc - NA_COLS // 2, 0, GRID_W - NA_COLS)
    col_ok = (kc >= c0) & (kc < c0 + NA_COLS)
    ci = np.clip(kc - qc + NA_COLS - 1, 0, 2 * NA_COLS - 2)
    onehot = (ci[None] == np.arange(2 * NA_COLS - 1)[:, None, None]).astype(np.float32)
    cols = jnp.einsum("hab,bqk->haqk", rpb, jnp.asarray(onehot), precision=lax.Precision.HIGHEST)
    cols = jnp.where(jnp.asarray(col_ok)[None, None], cols, NEG_INF)
    cols = jnp.concatenate([cols, jnp.full((H_C, 1, GRID_W, GRID_W), NEG_INF, F32)], axis=1)
    blocks = jnp.take(cols, jnp.asarray(_na_block_index().reshape(-1)), axis=1)
    blocks = blocks.reshape(H_C, 3, NA_TILE_ROWS, NA_KEY_ROWS, GRID_W, GRID_W)
    return blocks.transpose(1, 0, 2, 4, 3, 5).reshape(3, H_C, NA_Q, NA_K)


def _rope_tables():
    t = np.arange(DEC_SEQ)
    n_freq = HEAD_DIM // 4
    inv = (ROPE_BASE ** (-np.arange(n_freq, dtype=np.float32) / n_freq)).astype(np.float32)
    row = (t // GRID_W).astype(np.float32)[:, None] * inv
    col = (t % GRID_W).astype(np.float32)[:, None] * inv
    ang = np.concatenate([row, col], -1)
    cos, sin = np.cos(ang), np.sin(ang)
    cos_h = np.concatenate([cos, cos], -1)
    sin_h = np.concatenate([-sin, sin], -1)
    reps = LANES // HEAD_DIM
    return (jnp.asarray(np.tile(cos_h, (1, reps)), F32), jnp.asarray(np.tile(sin_h, (1, reps)), F32))


def _first_index_of(mask, iota, sentinel):
    return jnp.min(jnp.where(mask, iota, sentinel), axis=0, keepdims=True)


def _route(logits, b_col):
    n = logits.shape[1]
    scores = jax.nn.sigmoid(logits)
    sel = scores + b_col
    io_g = lax.broadcasted_iota(I32, (GROUP_SIZE, n), 0)
    gs_rows = []
    for g in range(N_GROUPS):
        s = sel[g * GROUP_SIZE:(g + 1) * GROUP_SIZE]
        m1 = jnp.max(s, axis=0, keepdims=True)
        i1 = _first_index_of(s == m1, io_g, GROUP_SIZE)
        m2 = jnp.max(jnp.where(io_g == i1, PICKED, s), axis=0, keepdims=True)
        gs_rows.append(m1 + m2)
    gs = jnp.concatenate(gs_rows, axis=0)
    io_n = lax.broadcasted_iota(I32, (N_GROUPS, n), 0)
    gsel = jnp.zeros((N_GROUPS, n), F32)
    for _ in range(TOPK_GROUPS):
        mg = jnp.max(gs, axis=0, keepdims=True)
        gi = _first_index_of(gs == mg, io_n, N_GROUPS)
        hit = io_n == gi
        gsel = jnp.where(hit, 1.0, gsel)
        gs = jnp.where(hit, PICKED, gs)
    cand = jnp.concatenate(
        [jnp.where(gsel[g:g + 1] > 0.5, sel[g * GROUP_SIZE:(g + 1) * GROUP_SIZE], NEG_INF)
         for g in range(N_GROUPS)], axis=0)
    io_e = lax.broadcasted_iota(I32, (N_EXPERTS, n), 0)
    picks, raw = [], []
    for _ in range(TOP_K):
        mv = jnp.max(cand, axis=0, keepdims=True)
        ei = _first_index_of(cand == mv, io_e, N_EXPERTS)
        hit = io_e == ei
        picks.append((hit, ei))
        raw.append(jnp.sum(jnp.where(hit, scores, 0.0), axis=0, keepdims=True))
        cand = jnp.where(hit, PICKED, cand)
    return picks, raw


def _post_mixer_kernel(xc_ref, xl_ref, mixc_ref, ya_ref, yb_ref, yc_ref, wout_ref, mod_ref, g_ref, b_ref,
                       wrh_ref, wrl_ref, br_ref,
                       x1_ref, h2_ref, eidx_ref, wsel_ref, rank_ref, cnt_ref):
    i = pl.program_id(0)
    tm = TM_TOK

    @pl.when(i == 0)
    def _():
        cnt_ref[...] = jnp.zeros_like(cnt_ref)

    ci = _cond_row(i, tm)
    gate1 = mod_ref[pl.ds(ci, 1), 2 * D_MODEL:3 * D_MODEL]
    sh2 = mod_ref[pl.ds(ci, 1), 3 * D_MODEL:4 * D_MODEL]
    sc2 = mod_ref[pl.ds(ci, 1), 4 * D_MODEL:5 * D_MODEL]
    mix_lat = jnp.concatenate([ya_ref[...], yb_ref[...], yc_ref[...]], axis=-1)
    mix = jnp.where(i < T_CTX // tm, mixc_ref[...], mix_lat)
    y = _dot(mix, wout_ref[...])
    x = jnp.where(i < T_CTX // tm, xc_ref[...], xl_ref[...])
    x1 = _layer_norm(ALPHA * x + gate1 * y, g_ref[...], b_ref[...])
    x1_ref[...] = x1
    h2 = x1 * (1.0 + sc2) + sh2
    h_hi = h2.astype(BF16)
    h2_ref[...] = h_hi
    h_lo = (h2 - h_hi.astype(F32)).astype(BF16)
    logits = (_dot_nt(wrh_ref[...], h_hi) + _dot_nt(wrh_ref[...], h_lo)
              + _dot_nt(wrl_ref[...], h_hi))
    picks, raw = _route(logits, br_ref[...])

    total = raw[0]
    for r in raw[1:]:
        total = total + r
    scale = ROUTE_SCALE / total
    multi = jnp.zeros((N_EXPERTS, tm), F32)
    for hit, _ in picks:
        multi = multi + jnp.where(hit, 1.0, 0.0)
    before = (lax.broadcasted_iota(I32, (tm, tm), 0) < lax.broadcasted_iota(I32, (tm, tm), 1))
    cum = _dot(multi.astype(BF16), jnp.where(before, 1.0, 0.0).astype(BF16))
    pad = jnp.zeros((SUBLANES - TOP_K, tm), F32)
    eidx_ref[...] = jnp.concatenate([ei for _, ei in picks] + [pad.astype(I32)], axis=0)
    wsel_ref[...] = jnp.concatenate([r * scale for r in raw] + [pad], axis=0)
    rank_ref[...] = jnp.concatenate(
        [jnp.sum(jnp.where(hit, cum, 0.0), axis=0, keepdims=True) for hit, _ in picks] + [pad],
        axis=0).astype(I32)
    tile_lane = lax.broadcasted_iota(I32, (N_EXPERTS, LANES), 1)
    cnt_ref[...] = jnp.where(tile_lane == i, jnp.sum(multi, axis=1, keepdims=True), cnt_ref[...])


def _post_mixer(x_ctx, x_lat, mix_c, ya, yb, yc, w_out_bf16, mod, ln_g, ln_b, wr_hi_t, wr_lo_t, b_router_col):
    n_ctx = T_CTX // TM_TOK
    ctx_map = lambda i: (jnp.minimum(i, n_ctx - 1), 0)
    lat_map = lambda i: (jnp.maximum(i - n_ctx, 0), 0)
    row_map = lambda i: (i, 0)
    const = lambda i: (0, 0)
    tok_map = lambda i: (0, i)
    return pl.pallas_call(
        _post_mixer_kernel,
        out_shape=(
            jax.ShapeDtypeStruct((T_ALL, D_MODEL), F32),
            jax.ShapeDtypeStruct((T_ALL, D_MODEL), BF16),
            jax.ShapeDtypeStruct((SUBLANES, T_ALL), I32),
            jax.ShapeDtypeStruct((SUBLANES, T_ALL), F32),
            jax.ShapeDtypeStruct((SUBLANES, T_ALL), I32),
            jax.ShapeDtypeStruct((N_EXPERTS, LANES), F32),
        ),
        grid=(N_TOK_TILES,),
        in_specs=[
            pl.BlockSpec((TM_TOK, D_MODEL), ctx_map),
            pl.BlockSpec((TM_TOK, D_MODEL), lat_map),
            pl.BlockSpec((TM_TOK, D_MODEL), ctx_map),
            pl.BlockSpec((TM_TOK, W_A), lat_map),
            pl.BlockSpec((TM_TOK, W_B), lat_map),
            pl.BlockSpec((TM_TOK, W_C), lat_map),
            pl.BlockSpec((D_MODEL, D_MODEL), const),
            pl.BlockSpec((N_COND, 6 * D_MODEL), const),
            pl.BlockSpec((1, D_MODEL), const),
            pl.BlockSpec((1, D_MODEL), const),
            pl.BlockSpec((N_EXPERTS, D_MODEL), const),
            pl.BlockSpec((N_EXPERTS, D_MODEL), const),
            pl.BlockSpec((N_EXPERTS, 1), const),
        ],
        out_specs=(
            pl.BlockSpec((TM_TOK, D_MODEL), row_map),
            pl.BlockSpec((TM_TOK, D_MODEL), row_map),
            pl.BlockSpec((SUBLANES, TM_TOK), tok_map),
            pl.BlockSpec((SUBLANES, TM_TOK), tok_map),
            pl.BlockSpec((SUBLANES, TM_TOK), tok_map),
            pl.BlockSpec((N_EXPERTS, LANES), const),
        ),
        compiler_params=pltpu.CompilerParams(
            dimension_semantics=("arbitrary",), vmem_limit_bytes=VMEM_LIMIT),
        name="post_mixer",
    )(x_ctx, x_lat, mix_c, ya, yb, yc, w_out_bf16, mod, ln_g, ln_b, wr_hi_t, wr_lo_t, b_router_col)


def _plan_kernel(eidx_ref, rank_ref, nmat_ref, lslot_ref, unit_ref, gend_ref, blk_ref):
    i = pl.program_id(0)
    units = jnp.floor((nmat_ref[...] + (UNIT - 1.0)) * (1.0 / UNIT))
    units_bf = units.astype(BF16)
    earlier_e = (lax.broadcasted_iota(I32, (N_EXPERTS, N_EXPERTS), 1)
                 < lax.broadcasted_iota(I32, (N_EXPERTS, N_EXPERTS), 0))
    tri_e = jnp.where(earlier_e, 1.0, 0.0).astype(BF16)
    earlier_t = (lax.broadcasted_iota(I32, (LANES, LANES), 0) < lax.broadcasted_iota(I32, (LANES, LANES), 1))
    tri_t = jnp.where(earlier_t, 1.0, 0.0).astype(BF16)
    local_off = _dot(tri_e, units_bf)
    tile_off = _dot(units_bf, tri_t)
    per_expert = jnp.sum(units, axis=1, keepdims=True)
    blocks = jnp.floor((per_expert + (UNITS_PER_BLOCK - 1.0)) * (1.0 / UNITS_PER_BLOCK))
    blocks_l = jnp.broadcast_to(blocks, (N_EXPERTS, LANES))
    start_blk = _dot(tri_e, blocks_l.astype(BF16))
    end_blk = start_blk + blocks_l
    gend_ref[...] = (end_blk * BM).astype(I32)
    blk_id = lax.broadcasted_iota(I32, (N_EXPERTS, BLK_LANES), 1).astype(F32)
    owner = jnp.sum(jnp.where(end_blk[:, 0:1] <= blk_id, 1.0, 0.0), axis=0, keepdims=True)
    blk_ref[...] = jnp.minimum(owner, N_EXPERTS - 1.0).astype(I32)

    this_tile = lax.broadcasted_iota(I32, (N_EXPERTS, LANES), 1) == i

    def column(a):
        return jnp.sum(jnp.where(this_tile, a, 0.0), axis=1, keepdims=True)

    lo, n_u = column(local_off), column(units)
    base_unit = start_blk[:, 0:1] * UNITS_PER_BLOCK + column(tile_off) - lo
    u = lax.broadcasted_iota(I32, (N_EXPERTS, MAX_UNITS), 1).astype(F32)
    inside = jnp.where(u >= lo, jnp.where(u < lo + n_u, 1.0, 0.0), 0.0)
    dst_unit = jnp.sum(inside * (base_unit + u), axis=0, keepdims=True)
    n_units = jnp.sum(n_u, axis=0, keepdims=True)
    last = lax.broadcasted_iota(I32, (1, MAX_UNITS), 1) == MAX_UNITS - 1
    unit_ref[0] = jnp.where(last, n_units, dst_unit).astype(I32)

    io_e = lax.broadcasted_iota(I32, (N_EXPERTS, TM_TOK), 0)
    rows = []
    for k in range(TOP_K):
        hit = io_e == eidx_ref[k:k + 1, :]
        seg = jnp.sum(jnp.where(hit, lo * UNIT, 0.0), axis=0, keepdims=True)
        rows.append(seg.astype(I32) + rank_ref[k:k + 1, :])
    rows.append(jnp.full((SUBLANES - TOP_K, TM_TOK), -1, I32))
    lslot_ref[...] = jnp.concatenate(rows, axis=0)


def _slot_plan(eidx, rank, nmat):
    tok_map = lambda i: (0, i)
    const = lambda i: (0, 0)
    lslot, unit_tab, gend, blk = pl.pallas_call(
        _plan_kernel,
        out_shape=(
            jax.ShapeDtypeStruct((SUBLANES, T_ALL), I32),
            jax.ShapeDtypeStruct((N_TOK_TILES, 1, MAX_UNITS), I32),
            jax.ShapeDtypeStruct((N_EXPERTS, LANES), I32),
            jax.ShapeDtypeStruct((1, BLK_LANES), I32),
        ),
        grid=(N_TOK_TILES,),
        in_specs=[
            pl.BlockSpec((SUBLANES, TM_TOK), tok_map),
            pl.BlockSpec((SUBLANES, TM_TOK), tok_map),
            pl.BlockSpec((N_EXPERTS, LANES), const),
        ],
        out_specs=(
            pl.BlockSpec((SUBLANES, TM_TOK), tok_map),
            pl.BlockSpec((1, 1, MAX_UNITS), lambda i: (i, 0, 0)),
            pl.BlockSpec((N_EXPERTS, LANES), const),
            pl.BlockSpec((1, BLK_LANES), const),
        ),
        compiler_params=pltpu.CompilerParams(dimension_semantics=("arbitrary",)),
        name="slot_plan",
    )(eidx, rank, nmat)
    gend = gend[:, 0]
    n_used = gend[N_EXPERTS - 1:] // BM
    return lslot, unit_tab, gend, blk[0, :N_BLOCKS], n_used


def _unit_copy(src, src_unit, dst, dst_unit, sem):
    return pltpu.make_async_copy(src.at[pl.ds(pl.multiple_of(src_unit * UNIT, UNIT), UNIT)],
                                 dst.at[pl.ds(pl.multiple_of(dst_unit * UNIT, UNIT), UNIT)], sem)


def _dispatch_kernel(gend_ref, h2_ref, lslot_ref, unit_hbm, xs_hbm, tab_smem, zero_ref, local_ref,
                     sem_tab, sem_zero, sem_rows):
    i = pl.program_id(0)
    tab_copy = pltpu.make_async_copy(unit_hbm.at[i], tab_smem, sem_tab)
    tab_copy.start()

    def has_rows(e):
        return gend_ref[e] > jnp.where(e == 0, 0, gend_ref[jnp.maximum(e - 1, 0)])

    def zero_copy(e):
        return pltpu.make_async_copy(
            zero_ref, xs_hbm.at[pl.ds(pl.multiple_of(gend_ref[e] - BM, BM), BM)], sem_zero)

    @pl.when(i == 0)
    def _():
        zero_ref[...] = jnp.zeros_like(zero_ref)

        def start(e, c):
            @pl.when(has_rows(e))
            def _():
                zero_copy(e).start()
            return c

        def wait(e, c):
            @pl.when(has_rows(e))
            def _():
                zero_copy(e).wait()
            return c

        def tail_copy(blk):
            return pltpu.make_async_copy(
                zero_ref, xs_hbm.at[pl.ds(pl.multiple_of(blk * BM, BM), BM)], sem_zero)

        def start_tail(blk, c):
            tail_copy(blk).start()
            return c

        def wait_tail(blk, c):
            tail_copy(blk).wait()
            return c

        n_used = gend_ref[N_EXPERTS - 1] // BM
        lax.fori_loop(0, N_EXPERTS, start, 0)
        lax.fori_loop(n_used, N_BLOCKS, start_tail, 0)
        lax.fori_loop(0, N_EXPERTS, wait, 0)
        lax.fori_loop(n_used, N_BLOCKS, wait_tail, 0)

    h2 = h2_ref[...]

    def permute(c, carry):
        slot = c * PERM_CHUNK + lax.broadcasted_iota(I32, (PERM_CHUNK, TM_TOK), 0)
        p = jnp.zeros((PERM_CHUNK, TM_TOK), F32)
        for k in range(TOP_K):
            p = jnp.where(slot == lslot_ref[k:k + 1, :], 1.0, p)
        local_ref[pl.ds(pl.multiple_of(c * PERM_CHUNK, PERM_CHUNK), PERM_CHUNK), :] = _dot(p.astype(BF16), h2)
        return carry

    lax.fori_loop(0, LOCAL_ROWS // PERM_CHUNK, permute, 0)
    tab_copy.wait()
    n_units = tab_smem[0, MAX_UNITS - 1]

    def issue(u, c):
        _unit_copy(local_ref, u, xs_hbm, tab_smem[0, u], sem_rows).start()
        return c

    def drain(u, c):
        _unit_copy(local_ref, u, xs_hbm, tab_smem[0, u], sem_rows).wait()
        return c

    lax.fori_loop(0, n_units, issue, 0)
    lax.fori_loop(0, n_units, drain, 0)


def _dispatch(h2, lslot, unit_tab, gend):
    return pl.pallas_call(
        _dispatch_kernel,
        out_shape=jax.ShapeDtypeStruct((N_SLOTS, D_MODEL), F32),
        grid_spec=pltpu.PrefetchScalarGridSpec(
            num_scalar_prefetch=1,
            grid=(N_TOK_TILES,),
            in_specs=[
                pl.BlockSpec((TM_TOK, D_MODEL), lambda i, ge: (i, 0)),
                pl.BlockSpec((SUBLANES, TM_TOK), lambda i, ge: (0, i)),
                pl.BlockSpec(memory_space=pl.ANY),
            ],
            out_specs=pl.BlockSpec(memory_space=pl.ANY),
            scratch_shapes=[
                pltpu.SMEM((1, MAX_UNITS), I32),
                pltpu.VMEM((BM, D_MODEL), F32),
                pltpu.VMEM((LOCAL_ROWS, D_MODEL), F32),
                pltpu.SemaphoreType.DMA,
                pltpu.SemaphoreType.DMA,
                pltpu.SemaphoreType.DMA,
            ],
        ),
        compiler_params=pltpu.CompilerParams(
            dimension_semantics=("arbitrary",), vmem_limit_bytes=VMEM_LIMIT),
        name="moe_dispatch",
    )(gend, h2, lslot, unit_tab)


def _expert_kernel(blk_e_ref, n_used_ref, xs_ref, wgu_ref, wdown_ref, ys_ref):
    i = pl.program_id(0)

    @pl.when(i < n_used_ref[0])
    def _():
        x = xs_ref[...].astype(BF16)
        gu = _dot(x, wgu_ref[0].astype(BF16))
        act = _silu(gu[:, 0:D_EXPERT]) * gu[:, D_EXPERT:2 * D_EXPERT]
        ys_ref[...] = _dot(act.astype(BF16), wdown_ref[0].astype(BF16))

    @pl.when(i >= n_used_ref[0])
    def _():
        ys_ref[...] = jnp.zeros_like(ys_ref)


def _experts(xs, w_gu, w_down, blk_e, n_used):
    def blk(i, be, nu):
        return jnp.minimum(i, nu[0] - 1)

    return pl.pallas_call(
        _expert_kernel,
        out_shape=jax.ShapeDtypeStruct((N_SLOTS, D_MODEL), F32),
        grid_spec=pltpu.PrefetchScalarGridSpec(
            num_scalar_prefetch=2,
            grid=(N_BLOCKS,),
            in_specs=[
                pl.BlockSpec((BM, D_MODEL), lambda i, be, nu: (blk(i, be, nu), 0)),
                pl.BlockSpec((1, D_MODEL, 2 * D_EXPERT), lambda i, be, nu: (be[blk(i, be, nu)], 0, 0)),
                pl.BlockSpec((1, D_EXPERT, D_MODEL), lambda i, be, nu: (be[blk(i, be, nu)], 0, 0)),
            ],
            out_specs=pl.BlockSpec((BM, D_MODEL), lambda i, be, nu: (i, 0)),
        ),
        compiler_params=pltpu.CompilerParams(
            dimension_semantics=("arbitrary",), vmem_limit_bytes=VMEM_LIMIT),
        name="moe_experts",
    )(blk_e, n_used, xs, w_gu, w_down)


def _combine_kernel(x1_ref, h2_ref, lslot_ref, wsel_ref, unit_hbm, ys_hbm, wsgu_ref, wsdown_ref, mod_ref,
                    g_ref, b_ref, outc_ref, outl_ref, tab_smem, local_ref, sem_tab, sem_rows):
    i = pl.program_id(0)
    tm = TM_TOK
    tab_copy = pltpu.make_async_copy(unit_hbm.at[i], tab_smem, sem_tab)
    tab_copy.start()

    @pl.when(i == 0)
    def _():
        local_ref[...] = jnp.zeros_like(local_ref)

    tab_copy.wait()
    n_units = tab_smem[0, MAX_UNITS - 1]

    def issue(u, c):
        _unit_copy(ys_hbm, tab_smem[0, u], local_ref, u, sem_rows).start()
        return c

    def drain(u, c):
        _unit_copy(ys_hbm, tab_smem[0, u], local_ref, u, sem_rows).wait()
        return c

    lax.fori_loop(0, n_units, issue, 0)

    sgu = _dot(h2_ref[...], wsgu_ref[...])
    act = _silu(sgu[:, 0:D_SHARED]) * sgu[:, D_SHARED:2 * D_SHARED]
    f = _dot(act.astype(BF16), wsdown_ref[...])

    lax.fori_loop(0, n_units, drain, 0)
    w = wsel_ref[...]
    ls = lslot_ref[...]
    for c in range(LOCAL_ROWS // PERM_CHUNK):
        slot = c * PERM_CHUNK + lax.broadcasted_iota(I32, (tm, PERM_CHUNK), 1)
        sel = jnp.zeros((tm, PERM_CHUNK), F32)
        for k in range(TOP_K):
            sel = jnp.where(slot == ls[:, k:k + 1], w[:, k:k + 1], sel)
        rows = local_ref[c * PERM_CHUNK:(c + 1) * PERM_CHUNK, :]
        f = f + _dot(sel.astype(BF16), rows.astype(BF16))
    ci = _cond_row(i, tm)
    gate2 = mod_ref[pl.ds(ci, 1), 5 * D_MODEL:6 * D_MODEL]
    out = _layer_norm(ALPHA * x1_ref[...] + gate2 * f, g_ref[...], b_ref[...])

    @pl.when(i < T_CTX // tm)
    def _():
        outc_ref[...] = out

    @pl.when(i >= T_CTX // tm)
    def _():
        outl_ref[...] = out


def _combine(x1, h2, lslot_rows, wsel_rows, unit_tab, ys, w_sgu_bf16, w_sdown_bf16, mod, ln_g, ln_b):
    n_ctx = T_CTX // TM_TOK
    row_map = lambda i: (i, 0)
    const = lambda i: (0, 0)
    return pl.pallas_call(
        _combine_kernel,
        out_shape=(jax.ShapeDtypeStruct((T_CTX, D_MODEL), F32),
                   jax.ShapeDtypeStruct((T_LAT, D_MODEL), F32)),
        grid=(N_TOK_TILES,),
        in_specs=[
            pl.BlockSpec((TM_TOK, D_MODEL), row_map),
            pl.BlockSpec((TM_TOK, D_MODEL), row_map),
            pl.BlockSpec((TM_TOK, SUBLANES), row_map),
            pl.BlockSpec((TM_TOK, SUBLANES), row_map),
            pl.BlockSpec(memory_space=pl.ANY),
            pl.BlockSpec(memory_space=pl.ANY),
            pl.BlockSpec((D_MODEL, 2 * D_SHARED), const),
            pl.BlockSpec((D_SHARED, D_MODEL), const),
            pl.BlockSpec((N_COND, 6 * D_MODEL), const),
            pl.BlockSpec((1, D_MODEL), const),
            pl.BlockSpec((1, D_MODEL), const),
        ],
        out_specs=(pl.BlockSpec((TM_TOK, D_MODEL), lambda i: (jnp.minimum(i, n_ctx - 1), 0)),
                   pl.BlockSpec((TM_TOK, D_MODEL), lambda i: (jnp.maximum(i - n_ctx, 0), 0))),
        scratch_shapes=[
            pltpu.SMEM((1, MAX_UNITS), I32),
            pltpu.VMEM((LOCAL_ROWS, D_MODEL), F32),
            pltpu.SemaphoreType.DMA,
            pltpu.SemaphoreType.DMA,
        ],
        compiler_params=pltpu.CompilerParams(
            dimension_semantics=("arbitrary",), vmem_limit_bytes=VMEM_LIMIT),
        name="moe_combine",
    )(x1, h2, lslot_rows, wsel_rows, unit_tab, ys, w_sgu_bf16, w_sdown_bf16, mod, ln_g, ln_b)


def _lane_rows(v, width):
    return jnp.broadcast_to(v.astype(F32)[:, None, None], (v.shape[0], 1, width))


def kernel(x_prompt, x_sample, state_ret_fwd, state_ret_bwd, cache_win_k, cache_win_v, cache_na_k, cache_na_v, c, c_ctx, w_in, w_out, ret_decay_fwd, ret_decay_bwd, ret_gn_g, ret_gn_b, win_sink, na_rpb, w_mod, b_mod, ln1_g, ln1_b, ln2_g, ln2_b, w_router, b_router, w_expert_gu, w_expert_down, w_shared_gu, w_shared_down):
    cond = jnp.concatenate(
        [c_ctx[None, :], c, jnp.zeros((N_COND - 1 - DEC_BATCH, D_MODEL), F32)], axis=0)
    mod_all = _modulation(cond, w_mod, b_mod)
    cos_t, sin_t = _rope_tables()

    x_ctx = x_prompt.reshape(T_CTX, D_MODEL)
    x_lat = x_sample.reshape(T_LAT, D_MODEL)
    sf_l, sb_l, cache_l = [], [], []
    for l in range(DEPTH):
        mod = mod_all[l]
        pa, pb, pc, cache = _in_projection(x_ctx, x_lat, mod, w_in[l].astype(BF16))
        cache_l.append(cache)
        decf_s, decb_s = _lane_rows(ret_decay_fwd[l], SEQ), _lane_rows(ret_decay_bwd[l], SEQ)
        gng, gnb = ret_gn_g[l][None, :], ret_gn_b[l][None, :]
        mix_c, st_f, st_b = _ctx_mixers(pa, pb, pc, win_sink[l], decf_s, decb_s, gng, gnb)
        sf_l.append(st_f)
        sb_l.append(st_b)
        ya = _lat_retention(pa, state_ret_fwd, state_ret_bwd, l,
                            _lane_rows(ret_decay_fwd[l], RET_CHUNK), _lane_rows(ret_decay_bwd[l], RET_CHUNK),
                            gng, gnb)
        yb = _lat_window_attn(pb, cache_win_k, cache_win_v, l, win_sink[l], cos_t, sin_t)
        yc = _lat_na_attn(pc, cache_na_k, cache_na_v, l, _na_maskbias(na_rpb[l]))

        wr_t = w_router[l].T
        wr_hi = wr_t.astype(BF16)
        wr_lo = (wr_t - wr_hi.astype(F32)).astype(BF16)
        x1, h2, eidx, wsel, rank, counts = _post_mixer(
            x_ctx, x_lat, mix_c, ya, yb, yc, w_out[l].astype(BF16), mod, ln1_g[l][None, :],
            ln1_b[l][None, :], wr_hi, wr_lo, b_router[l][:, None])
        lslot, unit_tab, gend, blk_e, n_used = _slot_plan(eidx, rank, counts)
        xs = _dispatch(h2, lslot, unit_tab, gend)
        ys = _experts(xs, w_expert_gu[l], w_expert_down[l], blk_e, n_used)
        x_ctx, x_lat = _combine(x1, h2, lslot.T, wsel.T, unit_tab, ys, w_shared_gu[l].astype(BF16),
                                w_shared_down[l].astype(BF16), mod, ln2_g[l][None, :], ln2_b[l][None, :])

    y_prompt = x_ctx.reshape(BATCH, SEQ, D_MODEL)
    y_sample = x_lat.reshape(DEC_BATCH, DEC_SEQ, D_MODEL)
    new_sf = jnp.stack(sf_l, axis=1)
    new_sb = jnp.stack(sb_l, axis=1)

    def heads_of(cols, n_heads):
        per_layer = [cache[:, cols[0]:cols[1]].reshape(BATCH, SEQ, n_heads, HEAD_DIM).transpose(0, 2, 1, 3)
                     for cache in cache_l]
        return jnp.stack(per_layer, axis=1)

    new_win_k = heads_of((0, KV_W_B), KV_B)
    new_win_v = heads_of((KV_W_B, 2 * KV_W_B), KV_B)
    new_na_k = heads_of((2 * KV_W_B, 2 * KV_W_B + W_C), H_C)
    new_na_v = heads_of((2 * KV_W_B + W_C, CACHE_W), H_C)
    return (y_prompt, y_sample, new_sf, new_sb, new_win_k, new_win_v, new_na_k, new_na_v)
```

```python
import functools

import numpy as np
import jax
import jax.numpy as jnp
from jax import lax
from jax.experimental import pallas as pl
from jax.experimental.pallas import tpu as pltpu

F32 = jnp.float32
BF16 = jnp.bfloat16
I32 = jnp.int32

D_MODEL = 1024
BATCH = 32
SEQ = 256
DEPTH = 2
DEC_BATCH = 4
DEC_SEQ = 2048
PAST_LEN = 256
GRID_W = 64
HEAD_DIM = 64
ATTN_SCALE = HEAD_DIM ** -0.5
H_A = 4
W_A = H_A * HEAD_DIM
GN_EPS = 1e-5
H_B = 6
KV_B = 2
W_B = H_B * HEAD_DIM
KV_W_B = KV_B * HEAD_DIM
WINDOW = 128
ROPE_BASE = 10000.0
H_C = 6
W_C = H_C * HEAD_DIM
NA_ROWS = 8
NA_COLS = 16
IN_WIDTH = 4 * W_A + W_B + 2 * KV_W_B + 3 * W_C
N_EXPERTS = 64
TOP_K = 6
N_GROUPS = 8
GROUP_SIZE = N_EXPERTS // N_GROUPS
TOPK_GROUPS = 4
D_EXPERT = 256
D_SHARED = 256
ROUTE_SCALE = 2.5
ALPHA = (2 * DEPTH) ** 0.25
LN_EPS = 1e-5
NEG_INF = -1e30
PICKED = -3e38

T_CTX = BATCH * SEQ
T_LAT = DEC_BATCH * DEC_SEQ
T_ALL = T_CTX + T_LAT
N_COND = 8

PA_W = 4 * W_A
PB_W = W_B + 2 * KV_W_B
PC_W = 3 * W_C
CACHE_W = 2 * KV_W_B + 2 * W_C

LANES = 128
SUBLANES = 8
VMEM_LIMIT = 56 * 1024 * 1024

TM_PROJ = 512
TM_TOK = 256
RET_CHUNK = 256
WIN_QB = 128
NA_TILE_ROWS = 4
NA_KEY_ROWS = 11
BM = 256
UNIT = SUBLANES
UNITS_PER_BLOCK = BM // UNIT
PERM_CHUNK = 256
N_TOK_TILES = T_ALL // TM_TOK
LOCAL_ROWS = -(-(TM_TOK * TOP_K + N_EXPERTS * (UNIT - 1)) // PERM_CHUNK) * PERM_CHUNK
MAX_UNITS = LOCAL_ROWS // UNIT
N_ASSIGN = T_ALL * TOP_K
N_BLOCKS = -(-(N_ASSIGN + N_TOK_TILES * N_EXPERTS * (UNIT - 1) + N_EXPERTS * (BM - 1)) // BM)
N_SLOTS = N_BLOCKS * BM
BLK_LANES = -(-N_BLOCKS // LANES) * LANES


def _dot(a, b):
    return jnp.dot(a, b, preferred_element_type=F32)


def _dot_nt(a, b):
    return lax.dot_general(a, b, (((1,), (1,)), ((), ())), preferred_element_type=F32)


def _silu(x):
    return x * jax.nn.sigmoid(x)


def _log_sigmoid(x):
    return jnp.minimum(x, 0.0) - jnp.log(1.0 + jnp.exp(-jnp.abs(x)))


def _cond_row(tile, tile_rows):
    n_ctx = T_CTX // tile_rows
    per_lat = DEC_SEQ // tile_rows
    return jnp.where(tile < n_ctx, 0, 1 + (tile - n_ctx) // per_lat)


def _layer_norm(x, g, b):
    mu = jnp.mean(x, -1, keepdims=True)
    xc = x - mu
    var = jnp.mean(xc * xc, -1, keepdims=True)
    return xc * lax.rsqrt(var + LN_EPS) * g + b


MOD_TN = 1536


def _mod_kernel(cond_ref, w_ref, b_ref, o_ref):
    s = _silu(cond_ref[...])
    s_hi = s.astype(BF16)
    s_lo = (s - s_hi.astype(F32)).astype(BF16)
    w = w_ref[0]
    w_hi = w.astype(BF16)
    w_lo = (w - w_hi.astype(F32)).astype(BF16)
    o_ref[0] = _dot(s_hi, w_hi) + _dot(s_lo, w_hi) + _dot(s_hi, w_lo) + b_ref[0]


def _modulation(cond, w_mod, b_mod):
    n_out = 6 * D_MODEL
    return pl.pallas_call(
        _mod_kernel,
        out_shape=jax.ShapeDtypeStruct((DEPTH, N_COND, n_out), F32),
        grid=(DEPTH, n_out // MOD_TN),
        in_specs=[
            pl.BlockSpec((N_COND, D_MODEL), lambda l, j: (0, 0)),
            pl.BlockSpec((1, D_MODEL, MOD_TN), lambda l, j: (l, 0, j)),
            pl.BlockSpec((1, 1, MOD_TN), lambda l, j: (l, 0, j)),
        ],
        out_specs=pl.BlockSpec((1, N_COND, MOD_TN), lambda l, j: (l, 0, j)),
        compiler_params=pltpu.CompilerParams(
            dimension_semantics=("arbitrary", "arbitrary"), vmem_limit_bytes=VMEM_LIMIT),
        name="modulation",
    )(cond, w_mod, b_mod.reshape(DEPTH, 1, n_out))


def _inproj_kernel(xc_ref, xl_ref, mod_ref, w_ref, pa_ref, pb_ref, pc_ref, cache_ref):
    i = pl.program_id(0)
    ci = _cond_row(i, TM_PROJ)
    sh = mod_ref[pl.ds(ci, 1), 0:D_MODEL]
    sc = mod_ref[pl.ds(ci, 1), D_MODEL:2 * D_MODEL]
    x = jnp.where(i < T_CTX // TM_PROJ, xc_ref[...], xl_ref[...])
    h = x * (1.0 + sc) + sh
    p = _dot(h.astype(BF16), w_ref[...])
    pa_ref[...] = p[:, 0:PA_W].astype(BF16)
    pb_ref[...] = p[:, PA_W:PA_W + PB_W].astype(BF16)
    pc_ref[...] = p[:, PA_W + PB_W:IN_WIDTH].astype(BF16)

    @pl.when(i < T_CTX // TM_PROJ)
    def _():
        cache_ref[:, 0:2 * KV_W_B] = p[:, PA_W + W_B:PA_W + PB_W]
        cache_ref[:, 2 * KV_W_B:CACHE_W] = p[:, PA_W + PB_W + W_C:IN_WIDTH]


def _in_projection(x_ctx, x_lat, mod, w_in_bf16):
    n_ctx_tiles = T_CTX // TM_PROJ
    return pl.pallas_call(
        _inproj_kernel,
        out_shape=(
            jax.ShapeDtypeStruct((T_ALL, PA_W), BF16),
            jax.ShapeDtypeStruct((T_ALL, PB_W), BF16),
            jax.ShapeDtypeStruct((T_ALL, PC_W), BF16),
            jax.ShapeDtypeStruct((T_CTX, CACHE_W), F32),
        ),
        grid=(T_ALL // TM_PROJ,),
        in_specs=[
            pl.BlockSpec((TM_PROJ, D_MODEL), lambda i: (jnp.minimum(i, n_ctx_tiles - 1), 0)),
            pl.BlockSpec((TM_PROJ, D_MODEL), lambda i: (jnp.maximum(i - n_ctx_tiles, 0), 0)),
            pl.BlockSpec((N_COND, 6 * D_MODEL), lambda i: (0, 0)),
            pl.BlockSpec((D_MODEL, IN_WIDTH), lambda i: (0, 0)),
        ],
        out_specs=(
            pl.BlockSpec((TM_PROJ, PA_W), lambda i: (i, 0)),
            pl.BlockSpec((TM_PROJ, PB_W), lambda i: (i, 0)),
            pl.BlockSpec((TM_PROJ, PC_W), lambda i: (i, 0)),
            pl.BlockSpec((TM_PROJ, CACHE_W), lambda i: (jnp.minimum(i, n_ctx_tiles - 1), 0)),
        ),
        compiler_params=pltpu.CompilerParams(
            dimension_semantics=("arbitrary",), vmem_limit_bytes=VMEM_LIMIT),
        name="in_projection",
    )(x_ctx, x_lat, mod, w_in_bf16)


def _decay_matrix(lg_f, lg_b, n):
    row = lax.broadcasted_iota(I32, (n, n), 0)
    col = lax.broadcasted_iota(I32, (n, n), 1)
    diff = (row - col).astype(F32)
    fwd = jnp.where(diff >= 0, jnp.exp(lg_f * jnp.maximum(diff, 0.0)), 0.0)
    bwd = jnp.where(diff <= 0, jnp.exp(lg_b * jnp.maximum(-diff, 0.0)), 0.0)
    return (fwd + bwd) * ATTN_SCALE


def _retention_readout(o, gate, g, b):
    mu = jnp.mean(o, -1, keepdims=True)
    oc = o - mu
    var = jnp.mean(oc * oc, -1, keepdims=True)
    on = oc * lax.rsqrt(var + GN_EPS) * g + b
    return on * _silu(gate.astype(F32))


def _softmax_attend(s, v, extra_logit=None):
    m = jnp.max(s, -1, keepdims=True)
    if extra_logit is not None:
        m = jnp.maximum(m, extra_logit)
    p = jnp.exp(s - m)
    den = jnp.sum(p, -1, keepdims=True)
    if extra_logit is not None:
        den = den + jnp.exp(extra_logit - m)
    return _dot(p.astype(BF16), v) / den


def _ctx_mixer_kernel(sink_ref, pa_ref, pb_ref, pc_ref, decf_ref, decb_ref, gng_ref, gnb_ref,
                      mix_ref, sf_ref, sb_ref):
    n = SEQ
    hd = HEAD_DIM
    pos = lax.broadcasted_iota(I32, (n, hd), 0).astype(F32)
    for h in range(H_A):
        c0 = h * hd
        q = pa_ref[:, c0:c0 + hd]
        k = pa_ref[:, W_A + c0:W_A + c0 + hd]
        v = pa_ref[:, 2 * W_A + c0:2 * W_A + c0 + hd]
        gate = pa_ref[:, 3 * W_A + c0:3 * W_A + c0 + hd]
        lg_f = _log_sigmoid(decf_ref[h])
        lg_b = _log_sigmoid(decb_ref[h])
        dmat = _decay_matrix(lg_f, lg_b, n)
        o = _dot((_dot_nt(q, k) * dmat).astype(BF16), v)
        kf = k.astype(F32)
        zf = jnp.exp(lg_f[:, 0:hd] * (n - 1.0 - pos)) * ATTN_SCALE
        zb = jnp.exp(lg_b[:, 0:hd] * pos) * ATTN_SCALE
        sf_ref[0, h] = _dot((kf * zf).T.astype(BF16), v)
        sb_ref[0, h] = _dot((kf * zb).T.astype(BF16), v)
        y = _retention_readout(o, gate, gng_ref[:, c0:c0 + hd], gnb_ref[:, c0:c0 + hd])
        mix_ref[:, c0:c0 + hd] = y.astype(BF16)
    for j in range(KV_B):
        k = pb_ref[:, W_B + j * hd:W_B + (j + 1) * hd]
        v = pb_ref[:, W_B + KV_W_B + j * hd:W_B + KV_W_B + (j + 1) * hd]
        for g in range(H_B // KV_B):
            hh = j * (H_B // KV_B) + g
            q = pb_ref[:, hh * hd:(hh + 1) * hd]
            o = _softmax_attend(_dot_nt(q, k) * ATTN_SCALE, v, sink_ref[hh])
            mix_ref[:, W_A + hh * hd:W_A + (hh + 1) * hd] = o.astype(BF16)
    for hh in range(H_C):
        q = pc_ref[:, hh * hd:(hh + 1) * hd]
        k = pc_ref[:, W_C + hh * hd:W_C + (hh + 1) * hd]
        v = pc_ref[:, 2 * W_C + hh * hd:2 * W_C + (hh + 1) * hd]
        o = _softmax_attend(_dot_nt(q, k) * ATTN_SCALE, v)
        mix_ref[:, W_A + W_B + hh * hd:W_A + W_B + (hh + 1) * hd] = o.astype(BF16)


def _ctx_mixers(pa, pb, pc, sink, decf, decb, gng, gnb):
    return pl.pallas_call(
        _ctx_mixer_kernel,
        out_shape=(
            jax.ShapeDtypeStruct((T_CTX, D_MODEL), BF16),
            jax.ShapeDtypeStruct((BATCH, H_A, HEAD_DIM, HEAD_DIM), F32),
            jax.ShapeDtypeStruct((BATCH, H_A, HEAD_DIM, HEAD_DIM), F32),
        ),
        grid=(BATCH,),
        in_specs=[
            pl.BlockSpec(memory_space=pltpu.SMEM),
            pl.BlockSpec((SEQ, PA_W), lambda b: (b, 0)),
            pl.BlockSpec((SEQ, PB_W), lambda b: (b, 0)),
            pl.BlockSpec((SEQ, PC_W), lambda b: (b, 0)),
            pl.BlockSpec((H_A, 1, SEQ), lambda b: (0, 0, 0)),
            pl.BlockSpec((H_A, 1, SEQ), lambda b: (0, 0, 0)),
            pl.BlockSpec((1, W_A), lambda b: (0, 0)),
            pl.BlockSpec((1, W_A), lambda b: (0, 0)),
        ],
        out_specs=(
            pl.BlockSpec((SEQ, D_MODEL), lambda b: (b, 0)),
            pl.BlockSpec((1, H_A, HEAD_DIM, HEAD_DIM), lambda b: (b, 0, 0, 0)),
            pl.BlockSpec((1, H_A, HEAD_DIM, HEAD_DIM), lambda b: (b, 0, 0, 0)),
        ),
        compiler_params=pltpu.CompilerParams(
            dimension_semantics=("arbitrary",), vmem_limit_bytes=VMEM_LIMIT),
        name="ctx_mixers",
    )(sink, pa, pb, pc, decf, decb, gng, gnb)


def _lat_ret_kernel(pa_ref, stf_ref, stb_ref, decf_ref, decb_ref, gng_ref, gnb_ref, ya_ref, acc_ref):
    c = RET_CHUNK
    hd = HEAD_DIM
    n_chunks = DEC_SEQ // c
    pos = lax.broadcasted_iota(I32, (c, hd), 0).astype(F32)
    for h in range(H_A):
        c0 = h * hd
        lg_f = _log_sigmoid(decf_ref[h])
        lg_b = _log_sigmoid(decb_ref[h])
        dmat = _decay_matrix(lg_f, lg_b, c)
        lf = lg_f[:, 0:hd]
        lb = lg_b[:, 0:hd]
        zf = jnp.exp(lf * (c - 1.0 - pos)) * ATTN_SCALE
        zb = jnp.exp(lb * pos) * ATTN_SCALE
        xf = jnp.exp(lf * (pos + 1.0))
        xb = jnp.exp(lb * (c - pos))
        gcf = jnp.exp(lf * float(c))
        gcb = jnp.exp(lb * float(c))
        g = gng_ref[:, c0:c0 + hd]
        b = gnb_ref[:, c0:c0 + hd]

        def load(i, off):
            rows = pl.ds(pl.multiple_of(i * c, c), c)
            return pa_ref[rows, off + c0:off + c0 + hd]

        def fwd(i, s):
            q, k, v = load(i, 0), load(i, W_A), load(i, 2 * W_A)
            o = _dot((_dot_nt(q, k) * dmat).astype(BF16), v)
            o = o + _dot((q.astype(F32) * xf).astype(BF16), s.astype(BF16))
            acc_ref[pl.ds(pl.multiple_of(i * c, c), c), c0:c0 + hd] = o
            return gcf * s + _dot((k.astype(F32) * zf).T.astype(BF16), v)

        lax.fori_loop(0, n_chunks, fwd, stf_ref[0, 0, h])

        def bwd(j, s):
            i = n_chunks - 1 - j
            rows = pl.ds(pl.multiple_of(i * c, c), c)
            q, k, v = load(i, 0), load(i, W_A), load(i, 2 * W_A)
            o = acc_ref[rows, c0:c0 + hd] + _dot((q.astype(F32) * xb).astype(BF16), s.astype(BF16))
            y = _retention_readout(o, load(i, 3 * W_A), g, b)
            ya_ref[rows, c0:c0 + hd] = y.astype(BF16)
            return gcb * s + _dot((k.astype(F32) * zb).T.astype(BF16), v)

        lax.fori_loop(0, n_chunks, bwd, stb_ref[0, 0, h])


def _lat_retention(pa, st_f, st_b, layer, decf, decb, gng, gnb):
    lat0 = T_CTX // DEC_SEQ
    st_spec = pl.BlockSpec((1, 1, H_A, HEAD_DIM, HEAD_DIM), lambda b: (b, layer, 0, 0, 0))
    return pl.pallas_call(
        _lat_ret_kernel,
        out_shape=jax.ShapeDtypeStruct((T_LAT, W_A), BF16),
        grid=(DEC_BATCH,),
        in_specs=[
            pl.BlockSpec((DEC_SEQ, PA_W), lambda b: (lat0 + b, 0)),
            st_spec, st_spec,
            pl.BlockSpec((H_A, 1, RET_CHUNK), lambda b: (0, 0, 0)),
            pl.BlockSpec((H_A, 1, RET_CHUNK), lambda b: (0, 0, 0)),
            pl.BlockSpec((1, W_A), lambda b: (0, 0)),
            pl.BlockSpec((1, W_A), lambda b: (0, 0)),
        ],
        out_specs=pl.BlockSpec((DEC_SEQ, W_A), lambda b: (b, 0)),
        scratch_shapes=[pltpu.VMEM((DEC_SEQ, W_A), F32)],
        compiler_params=pltpu.CompilerParams(
            dimension_semantics=("arbitrary",), vmem_limit_bytes=VMEM_LIMIT),
        name="lat_retention",
    )(pa, st_f, st_b, decf, decb, gng, gnb)


def _swap_halves_matrix(width):
    r = lax.broadcasted_iota(I32, (width, width), 0)
    c = lax.broadcasted_iota(I32, (width, width), 1)
    return jnp.where((r ^ (HEAD_DIM // 2)) == c, 1.0, 0.0).astype(BF16)


def _rope(x, cos, sin_signed, swap):
    return x.astype(F32) * cos + _dot(x, swap) * sin_signed


def _lat_win_kernel(sink_ref, pq_ref, pseq_ref, kctx_ref, vctx_ref, cos_ref, sin_ref, yb_ref, krope_ref):
    n = pl.program_id(1)
    hd = HEAD_DIM
    qb = WIN_QB
    n_blk = DEC_SEQ // qb
    group = H_B // KV_B
    swap = _swap_halves_matrix(LANES)

    @pl.when(n == 0)
    def _():
        k = pseq_ref[:, W_B:W_B + KV_W_B]
        krope_ref[...] = _rope(k, cos_ref[...], sin_ref[...], swap).astype(BF16)

    q_rows = pl.ds(pl.multiple_of(n * qb, qb), qb)
    cos_q = cos_ref[q_rows, :]
    sin_q = sin_ref[q_rows, :]
    qr = [_rope(pq_ref[:, p * LANES:(p + 1) * LANES], cos_q, sin_q, swap).astype(BF16)
          for p in range(W_B // LANES)]

    ws = jnp.clip(n - 1, 0, n_blk - 3) * qb
    k_rows = pl.ds(pl.multiple_of(ws, qb), 3 * qb)
    q_pos = n * qb + lax.broadcasted_iota(I32, (group * qb, 3 * qb), 0) % qb
    k_pos = ws + lax.broadcasted_iota(I32, (group * qb, 3 * qb), 1)
    valid = jnp.abs(k_pos - q_pos) <= WINDOW
    head_of_row = lax.broadcasted_iota(I32, (group * qb, 1), 0) // qb
    for j in range(KV_B):
        heads = [j * group + g for g in range(group)]
        qs = jnp.concatenate(
            [qr[hh // 2][:, (hh % 2) * hd:(hh % 2 + 1) * hd] for hh in heads], axis=0)
        kw = krope_ref[k_rows, j * hd:(j + 1) * hd]
        vw = pseq_ref[k_rows, W_B + KV_W_B + j * hd:W_B + KV_W_B + (j + 1) * hd]
        kc = kctx_ref[0, 0, j].astype(BF16)
        vc = vctx_ref[0, 0, j].astype(BF16)
        s_loc = jnp.where(valid, _dot_nt(qs, kw) * ATTN_SCALE, NEG_INF)
        s_ctx = _dot_nt(qs, kc) * ATTN_SCALE
        sink = jnp.zeros((group * qb, 1), F32)
        for g, hh in enumerate(heads):
            sink = jnp.where(head_of_row == g, sink_ref[hh], sink)
        m = jnp.maximum(jnp.maximum(jnp.max(s_loc, -1, keepdims=True),
                                    jnp.max(s_ctx, -1, keepdims=True)), sink)
        p_loc = jnp.exp(s_loc - m)
        p_ctx = jnp.exp(s_ctx - m)
        den = (jnp.sum(p_loc, -1, keepdims=True) + jnp.sum(p_ctx, -1, keepdims=True)
               + jnp.exp(sink - m))
        o = (_dot(p_loc.astype(BF16), vw) + _dot(p_ctx.astype(BF16), vc)) / den
        for g, hh in enumerate(heads):
            yb_ref[:, hh * hd:(hh + 1) * hd] = o[g * qb:(g + 1) * qb].astype(BF16)


def _lat_window_attn(pb, cache_k, cache_v, layer, sink, cos_t, sin_t):
    n_blk = DEC_SEQ // WIN_QB
    lat_blk0 = T_CTX // WIN_QB
    lat_seq0 = T_CTX // DEC_SEQ
    ctx_spec = pl.BlockSpec((1, 1, KV_B, PAST_LEN, HEAD_DIM), lambda b, n: (b, layer, 0, 0, 0))
    return pl.pallas_call(
        _lat_win_kernel,
        out_shape=jax.ShapeDtypeStruct((T_LAT, W_B), BF16),
        grid=(DEC_BATCH, n_blk),
        in_specs=[
            pl.BlockSpec(memory_space=pltpu.SMEM),
            pl.BlockSpec((WIN_QB, PB_W), lambda b, n: (lat_blk0 + b * n_blk + n, 0)),
            pl.BlockSpec((DEC_SEQ, PB_W), lambda b, n: (lat_seq0 + b, 0)),
            ctx_spec, ctx_spec,
            pl.BlockSpec((DEC_SEQ, LANES), lambda b, n: (0, 0)),
            pl.BlockSpec((DEC_SEQ, LANES), lambda b, n: (0, 0)),
        ],
        out_specs=pl.BlockSpec((WIN_QB, W_B), lambda b, n: (b * n_blk + n, 0)),
        scratch_shapes=[pltpu.VMEM((DEC_SEQ, KV_W_B), BF16)],
        compiler_params=pltpu.CompilerParams(
            dimension_semantics=("arbitrary", "arbitrary"), vmem_limit_bytes=VMEM_LIMIT),
        name="lat_window_attn",
    )(sink, pb, pb, cache_k, cache_v, cos_t, sin_t)


NA_Q = NA_TILE_ROWS * GRID_W
NA_K = NA_KEY_ROWS * GRID_W
NA_TILES = DEC_SEQ // NA_Q
LAT_ROWS = DEC_SEQ // GRID_W


def _na_window_start(tile):
    return jnp.clip(tile * NA_TILE_ROWS - NA_ROWS // 2, 0, LAT_ROWS - NA_KEY_ROWS)


def _lat_na_kernel(pq_ref, pseq_ref, kctx_ref, vctx_ref, bias_ref, yc_ref):
    t = pl.program_id(1)
    hd = HEAD_DIM
    k_rows = pl.ds(pl.multiple_of(_na_window_start(t) * GRID_W, GRID_W), NA_K)
    for hh in range(H_C):
        q = pq_ref[:, hh * hd:(hh + 1) * hd]
        kw = pseq_ref[k_rows, W_C + hh * hd:W_C + (hh + 1) * hd]
        vw = pseq_ref[k_rows, 2 * W_C + hh * hd:2 * W_C + (hh + 1) * hd]
        kc = kctx_ref[0, 0, hh].astype(BF16)
        vc = vctx_ref[0, 0, hh].astype(BF16)
        s_loc = _dot_nt(q, kw) * ATTN_SCALE + bias_ref[0, hh]
        s_ctx = _dot_nt(q, kc) * ATTN_SCALE
        m = jnp.maximum(jnp.max(s_loc, -1, keepdims=True), jnp.max(s_ctx, -1, keepdims=True))
        p_loc = jnp.exp(s_loc - m)
        p_ctx = jnp.exp(s_ctx - m)
        den = jnp.sum(p_loc, -1, keepdims=True) + jnp.sum(p_ctx, -1, keepdims=True)
        o = (_dot(p_loc.astype(BF16), vw) + _dot(p_ctx.astype(BF16), vc)) / den
        yc_ref[:, hh * hd:(hh + 1) * hd] = o.astype(BF16)


def _na_tile_type(t):
    return jnp.where(t == 0, 0, jnp.where(t == NA_TILES - 1, 2, 1))


def _lat_na_attn(pc, cache_k, cache_v, layer, maskbias):
    lat_tile0 = T_CTX // NA_Q
    lat_seq0 = T_CTX // DEC_SEQ
    ctx_spec = pl.BlockSpec((1, 1, H_C, PAST_LEN, HEAD_DIM), lambda b, t: (b, layer, 0, 0, 0))
    return pl.pallas_call(
        _lat_na_kernel,
        out_shape=jax.ShapeDtypeStruct((T_LAT, W_C), BF16),
        grid=(DEC_BATCH, NA_TILES),
        in_specs=[
            pl.BlockSpec((NA_Q, PC_W), lambda b, t: (lat_tile0 + b * NA_TILES + t, 0)),
            pl.BlockSpec((DEC_SEQ, PC_W), lambda b, t: (lat_seq0 + b, 0)),
            ctx_spec, ctx_spec,
            pl.BlockSpec((1, H_C, NA_Q, NA_K), lambda b, t: (_na_tile_type(t), 0, 0, 0)),
        ],
        out_specs=pl.BlockSpec((NA_Q, W_C), lambda b, t: (b * NA_TILES + t, 0)),
        compiler_params=pltpu.CompilerParams(
            dimension_semantics=("arbitrary", "arbitrary"), vmem_limit_bytes=VMEM_LIMIT),
        name="lat_na_attn",
    )(pc, pc, cache_k, cache_v, maskbias)


def _na_block_index():
    out = np.zeros((3, NA_TILE_ROWS, NA_KEY_ROWS), np.int32)
    for ty, tile in enumerate((0, 1, NA_TILES - 1)):
        r = tile * NA_TILE_ROWS
        ws = int(np.clip(r - NA_ROWS // 2, 0, LAT_ROWS - NA_KEY_ROWS))
        for qq in range(NA_TILE_ROWS):
            qr = r + qq
            r0 = int(np.clip(qr - NA_ROWS // 2, 0, LAT_ROWS - NA_ROWS))
            for kk in range(NA_KEY_ROWS):
                kr = ws + kk
                out[ty, qq, kk] = kr - qr + NA_ROWS - 1 if r0 <= kr < r0 + NA_ROWS else 2 * NA_ROWS - 1
    return out


def _na_maskbias(rpb):
    qc = np.arange(GRID_W)[:, None]
    kc = np.arange(GRID_W)[None, :]
    c0 = np.clip(qc - NA_COLS // 2, 0, GRID_W - NA_COLS)
    col_ok = (kc >= c0) & (kc < c0 + NA_COLS)
    ci = np.clip(kc - qc + NA_COLS - 1, 0, 2 * NA_COLS - 2)
    onehot = (ci[None] == np.arange(2 * NA_COLS - 1)[:, None, None]).astype(np.float32)
    cols = jnp.einsum("hab,bqk->haqk", rpb, jnp.asarray(onehot), precision=lax.Precision.HIGHEST)
    cols = jnp.where(jnp.asarray(col_ok)[None, None], cols, NEG_INF)
    cols = jnp.concatenate([cols, jnp.full((H_C, 1, GRID_W, GRID_W), NEG_INF, F32)], axis=1)
    blocks = jnp.take(cols, jnp.asarray(_na_block_index().reshape(-1)), axis=1)
    blocks = blocks.reshape(H_C, 3, NA_TILE_ROWS, NA_KEY_ROWS, GRID_W, GRID_W)
    return blocks.transpose(1, 0, 2, 4, 3, 5).reshape(3, H_C, NA_Q, NA_K)


def _rope_tables():
    t = np.arange(DEC_SEQ)
    n_freq = HEAD_DIM // 4
    inv = (ROPE_BASE ** (-np.arange(n_freq, dtype=np.float32) / n_freq)).astype(np.float32)
    row = (t // GRID_W).astype(np.float32)[:, None] * inv
    col = (t % GRID_W).astype(np.float32)[:, None] * inv
    ang = np.concatenate([row, col], -1)
    cos, sin = np.cos(ang), np.sin(ang)
    cos_h = np.concatenate([cos, cos], -1)
    sin_h = np.concatenate([-sin, sin], -1)
    reps = LANES // HEAD_DIM
    return (jnp.asarray(np.tile(cos_h, (1, reps)), F32), jnp.asarray(np.tile(sin_h, (1, reps)), F32))


def _first_index_of(mask, iota, sentinel):
    return jnp.min(jnp.where(mask, iota, sentinel), axis=0, keepdims=True)


def _route(logits, b_col):
    n = logits.shape[1]
    scores = jax.nn.sigmoid(logits)
    sel = scores + b_col
    io_g = lax.broadcasted_iota(I32, (GROUP_SIZE, n), 0)
    gs_rows = []
    for g in range(N_GROUPS):
        s = sel[g * GROUP_SIZE:(g + 1) * GROUP_SIZE]
        m1 = jnp.max(s, axis=0, keepdims=True)
        i1 = _first_index_of(s == m1, io_g, GROUP_SIZE)
        m2 = jnp.max(jnp.where(io_g == i1, PICKED, s), axis=0, keepdims=True)
        gs_rows.append(m1 + m2)
    gs = jnp.concatenate(gs_rows, axis=0)
    io_n = lax.broadcasted_iota(I32, (N_GROUPS, n), 0)
    gsel = jnp.zeros((N_GROUPS, n), F32)
    for _ in range(TOPK_GROUPS):
        mg = jnp.max(gs, axis=0, keepdims=True)
        gi = _first_index_of(gs == mg, io_n, N_GROUPS)
        hit = io_n == gi
        gsel = jnp.where(hit, 1.0, gsel)
        gs = jnp.where(hit, PICKED, gs)
    cand = jnp.concatenate(
        [jnp.where(gsel[g:g + 1] > 0.5, sel[g * GROUP_SIZE:(g + 1) * GROUP_SIZE], NEG_INF)
         for g in range(N_GROUPS)], axis=0)
    io_e = lax.broadcasted_iota(I32, (N_EXPERTS, n), 0)
    picks, raw = [], []
    for _ in range(TOP_K):
        mv = jnp.max(cand, axis=0, keepdims=True)
        ei = _first_index_of(cand == mv, io_e, N_EXPERTS)
        hit = io_e == ei
        picks.append((hit, ei))
        raw.append(jnp.sum(jnp.where(hit, scores, 0.0), axis=0, keepdims=True))
        cand = jnp.where(hit, PICKED, cand)
    return picks, raw


def _post_mixer_kernel(xc_ref, xl_ref, mixc_ref, ya_ref, yb_ref, yc_ref, wout_ref, mod_ref, g_ref, b_ref,
                       wrh_ref, wrl_ref, br_ref,
                       x1_ref, h2_ref, eidx_ref, wsel_ref, rank_ref, cnt_ref):
    i = pl.program_id(0)
    tm = TM_TOK

    @pl.when(i == 0)
    def _():
        cnt_ref[...] = jnp.zeros_like(cnt_ref)

    ci = _cond_row(i, tm)
    gate1 = mod_ref[pl.ds(ci, 1), 2 * D_MODEL:3 * D_MODEL]
    sh2 = mod_ref[pl.ds(ci, 1), 3 * D_MODEL:4 * D_MODEL]
    sc2 = mod_ref[pl.ds(ci, 1), 4 * D_MODEL:5 * D_MODEL]
    mix_lat = jnp.concatenate([ya_ref[...], yb_ref[...], yc_ref[...]], axis=-1)
    mix = jnp.where(i < T_CTX // tm, mixc_ref[...], mix_lat)
    y = _dot(mix, wout_ref[...])
    x = jnp.where(i < T_CTX // tm, xc_ref[...], xl_ref[...])
    x1 = _layer_norm(ALPHA * x + gate1 * y, g_ref[...], b_ref[...])
    x1_ref[...] = x1
    h2 = x1 * (1.0 + sc2) + sh2
    h_hi = h2.astype(BF16)
    h2_ref[...] = h_hi
    h_lo = (h2 - h_hi.astype(F32)).astype(BF16)
    logits = (_dot_nt(wrh_ref[...], h_hi) + _dot_nt(wrh_ref[...], h_lo)
              + _dot_nt(wrl_ref[...], h_hi))
    picks, raw = _route(logits, br_ref[...])

    total = raw[0]
    for r in raw[1:]:
        total = total + r
    scale = ROUTE_SCALE / total
    multi = jnp.zeros((N_EXPERTS, tm), F32)
    for hit, _ in picks:
        multi = multi + jnp.where(hit, 1.0, 0.0)
    before = (lax.broadcasted_iota(I32, (tm, tm), 0) < lax.broadcasted_iota(I32, (tm, tm), 1))
    cum = _dot(multi.astype(BF16), jnp.where(before, 1.0, 0.0).astype(BF16))
    pad = jnp.zeros((SUBLANES - TOP_K, tm), F32)
    eidx_ref[...] = jnp.concatenate([ei for _, ei in picks] + [pad.astype(I32)], axis=0)
    wsel_ref[...] = jnp.concatenate([r * scale for r in raw] + [pad], axis=0)
    rank_ref[...] = jnp.concatenate(
        [jnp.sum(jnp.where(hit, cum, 0.0), axis=0, keepdims=True) for hit, _ in picks] + [pad],
        axis=0).astype(I32)
    tile_lane = lax.broadcasted_iota(I32, (N_EXPERTS, LANES), 1)
    cnt_ref[...] = jnp.where(tile_lane == i, jnp.sum(multi, axis=1, keepdims=True), cnt_ref[...])


def _post_mixer(x_ctx, x_lat, mix_c, ya, yb, yc, w_out_bf16, mod, ln_g, ln_b, wr_hi_t, wr_lo_t, b_router_col):
    n_ctx = T_CTX // TM_TOK
    ctx_map = lambda i: (jnp.minimum(i, n_ctx - 1), 0)
    lat_map = lambda i: (jnp.maximum(i - n_ctx, 0), 0)
    row_map = lambda i: (i, 0)
    const = lambda i: (0, 0)
    tok_map = lambda i: (0, i)
    return pl.pallas_call(
        _post_mixer_kernel,
        out_shape=(
            jax.ShapeDtypeStruct((T_ALL, D_MODEL), F32),
            jax.ShapeDtypeStruct((T_ALL, D_MODEL), BF16),
            jax.ShapeDtypeStruct((SUBLANES, T_ALL), I32),
            jax.ShapeDtypeStruct((SUBLANES, T_ALL), F32),
            jax.ShapeDtypeStruct((SUBLANES, T_ALL), I32),
            jax.ShapeDtypeStruct((N_EXPERTS, LANES), F32),
        ),
        grid=(N_TOK_TILES,),
        in_specs=[
            pl.BlockSpec((TM_TOK, D_MODEL), ctx_map),
            pl.BlockSpec((TM_TOK, D_MODEL), lat_map),
            pl.BlockSpec((TM_TOK, D_MODEL), ctx_map),
            pl.BlockSpec((TM_TOK, W_A), lat_map),
            pl.BlockSpec((TM_TOK, W_B), lat_map),
            pl.BlockSpec((TM_TOK, W_C), lat_map),
            pl.BlockSpec((D_MODEL, D_MODEL), const),
            pl.BlockSpec((N_COND, 6 * D_MODEL), const),
            pl.BlockSpec((1, D_MODEL), const),
            pl.BlockSpec((1, D_MODEL), const),
            pl.BlockSpec((N_EXPERTS, D_MODEL), const),
            pl.BlockSpec((N_EXPERTS, D_MODEL), const),
            pl.BlockSpec((N_EXPERTS, 1), const),
        ],
        out_specs=(
            pl.BlockSpec((TM_TOK, D_MODEL), row_map),
            pl.BlockSpec((TM_TOK, D_MODEL), row_map),
            pl.BlockSpec((SUBLANES, TM_TOK), tok_map),
            pl.BlockSpec((SUBLANES, TM_TOK), tok_map),
            pl.BlockSpec((SUBLANES, TM_TOK), tok_map),
            pl.BlockSpec((N_EXPERTS, LANES), const),
        ),
        compiler_params=pltpu.CompilerParams(
            dimension_semantics=("arbitrary",), vmem_limit_bytes=VMEM_LIMIT),
        name="post_mixer",
    )(x_ctx, x_lat, mix_c, ya, yb, yc, w_out_bf16, mod, ln_g, ln_b, wr_hi_t, wr_lo_t, b_router_col)


def _plan_kernel(eidx_ref, rank_ref, nmat_ref, lslot_ref, unit_ref, gend_ref, blk_ref):
    i = pl.program_id(0)
    units = jnp.floor((nmat_ref[...] + (UNIT - 1.0)) * (1.0 / UNIT))
    units_bf = units.astype(BF16)
    earlier_e = (lax.broadcasted_iota(I32, (N_EXPERTS, N_EXPERTS), 1)
                 < lax.broadcasted_iota(I32, (N_EXPERTS, N_EXPERTS), 0))
    tri_e = jnp.where(earlier_e, 1.0, 0.0).astype(BF16)
    earlier_t = (lax.broadcasted_iota(I32, (LANES, LANES), 0) < lax.broadcasted_iota(I32, (LANES, LANES), 1))
    tri_t = jnp.where(earlier_t, 1.0, 0.0).astype(BF16)
    local_off = _dot(tri_e, units_bf)
    tile_off = _dot(units_bf, tri_t)
    per_expert = jnp.sum(units, axis=1, keepdims=True)
    blocks = jnp.floor((per_expert + (UNITS_PER_BLOCK - 1.0)) * (1.0 / UNITS_PER_BLOCK))
    blocks_l = jnp.broadcast_to(blocks, (N_EXPERTS, LANES))
    start_blk = _dot(tri_e, blocks_l.astype(BF16))
    end_blk = start_blk + blocks_l
    gend_ref[...] = (end_blk * BM).astype(I32)
    blk_id = lax.broadcasted_iota(I32, (N_EXPERTS, BLK_LANES), 1).astype(F32)
    owner = jnp.sum(jnp.where(end_blk[:, 0:1] <= blk_id, 1.0, 0.0), axis=0, keepdims=True)
    blk_ref[...] = jnp.minimum(owner, N_EXPERTS - 1.0).astype(I32)

    this_tile = lax.broadcasted_iota(I32, (N_EXPERTS, LANES), 1) == i

    def column(a):
        return jnp.sum(jnp.where(this_tile, a, 0.0), axis=1, keepdims=True)

    lo, n_u = column(local_off), column(units)
    base_unit = start_blk[:, 0:1] * UNITS_PER_BLOCK + column(tile_off) - lo
    u = lax.broadcasted_iota(I32, (N_EXPERTS, MAX_UNITS), 1).astype(F32)
    inside = jnp.where(u >= lo, jnp.where(u < lo + n_u, 1.0, 0.0), 0.0)
    dst_unit = jnp.sum(inside * (base_unit + u), axis=0, keepdims=True)
    n_units = jnp.sum(n_u, axis=0, keepdims=True)
    last = lax.broadcasted_iota(I32, (1, MAX_UNITS), 1) == MAX_UNITS - 1
    unit_ref[0] = jnp.where(last, n_units, dst_unit).astype(I32)

    io_e = lax.broadcasted_iota(I32, (N_EXPERTS, TM_TOK), 0)
    rows = []
    for k in range(TOP_K):
        hit = io_e == eidx_ref[k:k + 1, :]
        seg = jnp.sum(jnp.where(hit, lo * UNIT, 0.0), axis=0, keepdims=True)
        rows.append(seg.astype(I32) + rank_ref[k:k + 1, :])
    rows.append(jnp.full((SUBLANES - TOP_K, TM_TOK), -1, I32))
    lslot_ref[...] = jnp.concatenate(rows, axis=0)


def _slot_plan(eidx, rank, nmat):
    tok_map = lambda i: (0, i)
    const = lambda i: (0, 0)
    lslot, unit_tab, gend, blk = pl.pallas_call(
        _plan_kernel,
        out_shape=(
            jax.ShapeDtypeStruct((SUBLANES, T_ALL), I32),
            jax.ShapeDtypeStruct((N_TOK_TILES, 1, MAX_UNITS), I32),
            jax.ShapeDtypeStruct((N_EXPERTS, LANES), I32),
            jax.ShapeDtypeStruct((1, BLK_LANES), I32),
        ),
        grid=(N_TOK_TILES,),
        in_specs=[
            pl.BlockSpec((SUBLANES, TM_TOK), tok_map),
            pl.BlockSpec((SUBLANES, TM_TOK), tok_map),
            pl.BlockSpec((N_EXPERTS, LANES), const),
        ],
        out_specs=(
            pl.BlockSpec((SUBLANES, TM_TOK), tok_map),
            pl.BlockSpec((1, 1, MAX_UNITS), lambda i: (i, 0, 0)),
            pl.BlockSpec((N_EXPERTS, LANES), const),
            pl.BlockSpec((1, BLK_LANES), const),
        ),
        compiler_params=pltpu.CompilerParams(dimension_semantics=("arbitrary",)),
        name="slot_plan",
    )(eidx, rank, nmat)
    gend = gend[:, 0]
    n_used = gend[N_EXPERTS - 1:] // BM
    return lslot, unit_tab, gend, blk[0, :N_BLOCKS], n_used


PACK_W = D_MODEL // 2
HI_HALF = -65536


def _pack_pairs(x):
    lo = lax.bitcast_convert_type(x[:, 0:PACK_W], I32)
    hi = lax.bitcast_convert_type(x[:, PACK_W:D_MODEL], I32)
    return lax.shift_right_logical(lo, 16) | (hi & HI_HALF)


def _unpack_pairs(u):
    lo = lax.bitcast_convert_type(lax.shift_left(u, 16), F32).astype(BF16)
    hi = lax.bitcast_convert_type(u & HI_HALF, F32).astype(BF16)
    return lo, hi


def _unit_copy(src, src_unit, dst, dst_unit, sem):
    return pltpu.make_async_copy(src.at[pl.ds(pl.multiple_of(src_unit * UNIT, UNIT), UNIT)],
                                 dst.at[pl.ds(pl.multiple_of(dst_unit * UNIT, UNIT), UNIT)], sem)


def _dispatch_kernel(gend_ref, h2_ref, lslot_ref, unit_hbm, xs_hbm, tab_smem, zero_ref, local_ref,
                     sem_tab, sem_zero, sem_rows):
    i = pl.program_id(0)
    tab_copy = pltpu.make_async_copy(unit_hbm.at[i], tab_smem, sem_tab)
    tab_copy.start()

    def has_rows(e):
        return gend_ref[e] > jnp.where(e == 0, 0, gend_ref[jnp.maximum(e - 1, 0)])

    def zero_copy(e):
        return pltpu.make_async_copy(
            zero_ref, xs_hbm.at[pl.ds(pl.multiple_of(gend_ref[e] - BM, BM), BM)], sem_zero)

    @pl.when(i == 0)
    def _():
        zero_ref[...] = jnp.zeros_like(zero_ref)

        def start(e, c):
            @pl.when(has_rows(e))
            def _():
                zero_copy(e).start()
            return c

        def wait(e, c):
            @pl.when(has_rows(e))
            def _():
                zero_copy(e).wait()
            return c

        def tail_copy(blk):
            return pltpu.make_async_copy(
                zero_ref, xs_hbm.at[pl.ds(pl.multiple_of(blk * BM, BM), BM)], sem_zero)

        def start_tail(blk, c):
            tail_copy(blk).start()
            return c

        def wait_tail(blk, c):
            tail_copy(blk).wait()
            return c

        n_used = gend_ref[N_EXPERTS - 1] // BM
        lax.fori_loop(0, N_EXPERTS, start, 0)
        lax.fori_loop(n_used, N_BLOCKS, start_tail, 0)
        lax.fori_loop(0, N_EXPERTS, wait, 0)
        lax.fori_loop(n_used, N_BLOCKS, wait_tail, 0)

    h2 = h2_ref[...]

    def permute(c, carry):
        slot = c * PERM_CHUNK + lax.broadcasted_iota(I32, (PERM_CHUNK, TM_TOK), 0)
        p = jnp.zeros((PERM_CHUNK, TM_TOK), F32)
        for k in range(TOP_K):
            p = jnp.where(slot == lslot_ref[k:k + 1, :], 1.0, p)
        local_ref[pl.ds(pl.multiple_of(c * PERM_CHUNK, PERM_CHUNK), PERM_CHUNK), :] = _pack_pairs(
            _dot(p.astype(BF16), h2))
        return carry

    lax.fori_loop(0, LOCAL_ROWS // PERM_CHUNK, permute, 0)
    tab_copy.wait()
    n_units = tab_smem[0, MAX_UNITS - 1]

    def issue(u, c):
        _unit_copy(local_ref, u, xs_hbm, tab_smem[0, u], sem_rows).start()
        return c

    def drain(u, c):
        _unit_copy(local_ref, u, xs_hbm, tab_smem[0, u], sem_rows).wait()
        return c

    lax.fori_loop(0, n_units, issue, 0)
    lax.fori_loop(0, n_units, drain, 0)


def _dispatch(h2, lslot, unit_tab, gend):
    return pl.pallas_call(
        _dispatch_kernel,
        out_shape=jax.ShapeDtypeStruct((N_SLOTS, PACK_W), I32),
        grid_spec=pltpu.PrefetchScalarGridSpec(
            num_scalar_prefetch=1,
            grid=(N_TOK_TILES,),
            in_specs=[
                pl.BlockSpec((TM_TOK, D_MODEL), lambda i, ge: (i, 0)),
                pl.BlockSpec((SUBLANES, TM_TOK), lambda i, ge: (0, i)),
                pl.BlockSpec(memory_space=pl.ANY),
            ],
            out_specs=pl.BlockSpec(memory_space=pl.ANY),
            scratch_shapes=[
                pltpu.SMEM((1, MAX_UNITS), I32),
                pltpu.VMEM((BM, PACK_W), I32),
                pltpu.VMEM((LOCAL_ROWS, PACK_W), I32),
                pltpu.SemaphoreType.DMA,
                pltpu.SemaphoreType.DMA,
                pltpu.SemaphoreType.DMA,
            ],
        ),
        compiler_params=pltpu.CompilerParams(
            dimension_semantics=("arbitrary",), vmem_limit_bytes=VMEM_LIMIT),
        name="moe_dispatch",
    )(gend, h2, lslot, unit_tab)


def _expert_kernel(blk_e_ref, n_used_ref, xs_ref, wgu_ref, wdown_ref, ys_ref):
    i = pl.program_id(0)

    @pl.when(i < n_used_ref[0])
    def _():
        x_lo, x_hi = _unpack_pairs(xs_ref[...])
        gu = (_dot(x_lo, wgu_ref[0, 0:PACK_W, :].astype(BF16))
              + _dot(x_hi, wgu_ref[0, PACK_W:D_MODEL, :].astype(BF16)))
        act = _silu(gu[:, 0:D_EXPERT]) * gu[:, D_EXPERT:2 * D_EXPERT]
        y = _dot(act.astype(BF16), wdown_ref[0].astype(BF16))
        ys_ref[...] = _pack_pairs(y.astype(BF16).astype(F32))

    @pl.when(i >= n_used_ref[0])
    def _():
        ys_ref[...] = jnp.zeros_like(ys_ref)


def _experts(xs, w_gu, w_down, blk_e, n_used):
    def blk(i, be, nu):
        return jnp.minimum(i, nu[0] - 1)

    return pl.pallas_call(
        _expert_kernel,
        out_shape=jax.ShapeDtypeStruct((N_SLOTS, PACK_W), I32),
        grid_spec=pltpu.PrefetchScalarGridSpec(
            num_scalar_prefetch=2,
            grid=(N_BLOCKS,),
            in_specs=[
                pl.BlockSpec((BM, PACK_W), lambda i, be, nu: (blk(i, be, nu), 0)),
                pl.BlockSpec((1, D_MODEL, 2 * D_EXPERT), lambda i, be, nu: (be[blk(i, be, nu)], 0, 0)),
                pl.BlockSpec((1, D_EXPERT, D_MODEL), lambda i, be, nu: (be[blk(i, be, nu)], 0, 0)),
            ],
            out_specs=pl.BlockSpec((BM, PACK_W), lambda i, be, nu: (i, 0)),
        ),
        compiler_params=pltpu.CompilerParams(
            dimension_semantics=("arbitrary",), vmem_limit_bytes=VMEM_LIMIT),
        name="moe_experts",
    )(blk_e, n_used, xs, w_gu, w_down)


def _combine_kernel(x1_ref, h2_ref, lslot_ref, wsel_ref, unit_hbm, ys_hbm, wsgu_ref, wsdown_ref, mod_ref,
                    g_ref, b_ref, outc_ref, outl_ref, tab_smem, local_ref, sem_tab, sem_rows):
    i = pl.program_id(0)
    tm = TM_TOK
    tab_copy = pltpu.make_async_copy(unit_hbm.at[i], tab_smem, sem_tab)
    tab_copy.start()

    @pl.when(i == 0)
    def _():
        local_ref[...] = jnp.zeros_like(local_ref)

    tab_copy.wait()
    n_units = tab_smem[0, MAX_UNITS - 1]

    def issue(u, c):
        _unit_copy(ys_hbm, tab_smem[0, u], local_ref, u, sem_rows).start()
        return c

    def drain(u, c):
        _unit_copy(ys_hbm, tab_smem[0, u], local_ref, u, sem_rows).wait()
        return c

    lax.fori_loop(0, n_units, issue, 0)

    sgu = _dot(h2_ref[...], wsgu_ref[...])
    act = _silu(sgu[:, 0:D_SHARED]) * sgu[:, D_SHARED:2 * D_SHARED]
    f = _dot(act.astype(BF16), wsdown_ref[...])

    lax.fori_loop(0, n_units, drain, 0)
    w = wsel_ref[...]
    ls = lslot_ref[...]
    r_lo = jnp.zeros((tm, PACK_W), F32)
    r_hi = jnp.zeros((tm, PACK_W), F32)
    for c in range(LOCAL_ROWS // PERM_CHUNK):
        slot = c * PERM_CHUNK + lax.broadcasted_iota(I32, (tm, PERM_CHUNK), 1)
        sel = jnp.zeros((tm, PERM_CHUNK), F32)
        for k in range(TOP_K):
            sel = jnp.where(slot == ls[:, k:k + 1], w[:, k:k + 1], sel)
        sel = sel.astype(BF16)
        y_lo, y_hi = _unpack_pairs(local_ref[c * PERM_CHUNK:(c + 1) * PERM_CHUNK, :])
        r_lo = r_lo + _dot(sel, y_lo)
        r_hi = r_hi + _dot(sel, y_hi)
    f = f + jnp.concatenate([r_lo, r_hi], axis=-1)
    ci = _cond_row(i, tm)
    gate2 = mod_ref[pl.ds(ci, 1), 5 * D_MODEL:6 * D_MODEL]
    out = _layer_norm(ALPHA * x1_ref[...] + gate2 * f, g_ref[...], b_ref[...])

    @pl.when(i < T_CTX // tm)
    def _():
        outc_ref[...] = out

    @pl.when(i >= T_CTX // tm)
    def _():
        outl_ref[...] = out


def _combine(x1, h2, lslot_rows, wsel_rows, unit_tab, ys, w_sgu_bf16, w_sdown_bf16, mod, ln_g, ln_b):
    n_ctx = T_CTX // TM_TOK
    row_map = lambda i: (i, 0)
    const = lambda i: (0, 0)
    return pl.pallas_call(
        _combine_kernel,
        out_shape=(jax.ShapeDtypeStruct((T_CTX, D_MODEL), F32),
                   jax.ShapeDtypeStruct((T_LAT, D_MODEL), F32)),
        grid=(N_TOK_TILES,),
        in_specs=[
            pl.BlockSpec((TM_TOK, D_MODEL), row_map),
            pl.BlockSpec((TM_TOK, D_MODEL), row_map),
            pl.BlockSpec((TM_TOK, SUBLANES), row_map),
            pl.BlockSpec((TM_TOK, SUBLANES), row_map),
            pl.BlockSpec(memory_space=pl.ANY),
            pl.BlockSpec(memory_space=pl.ANY),
            pl.BlockSpec((D_MODEL, 2 * D_SHARED), const),
            pl.BlockSpec((D_SHARED, D_MODEL), const),
            pl.BlockSpec((N_COND, 6 * D_MODEL), const),
            pl.BlockSpec((1, D_MODEL), const),
            pl.BlockSpec((1, D_MODEL), const),
        ],
        out_specs=(pl.BlockSpec((TM_TOK, D_MODEL), lambda i: (jnp.minimum(i, n_ctx - 1), 0)),
                   pl.BlockSpec((TM_TOK, D_MODEL), lambda i: (jnp.maximum(i - n_ctx, 0), 0))),
        scratch_shapes=[
            pltpu.SMEM((1, MAX_UNITS), I32),
            pltpu.VMEM((LOCAL_ROWS, PACK_W), I32),
            pltpu.SemaphoreType.DMA,
            pltpu.SemaphoreType.DMA,
        ],
        compiler_params=pltpu.CompilerParams(
            dimension_semantics=("arbitrary",), vmem_limit_bytes=VMEM_LIMIT),
        name="moe_combine",
    )(x1, h2, lslot_rows, wsel_rows, unit_tab, ys, w_sgu_bf16, w_sdown_bf16, mod, ln_g, ln_b)


def _lane_rows(v, width):
    return jnp.broadcast_to(v.astype(F32)[:, None, None], (v.shape[0], 1, width))


def kernel(x_prompt, x_sample, state_ret_fwd, state_ret_bwd, cache_win_k, cache_win_v, cache_na_k, cache_na_v, c, c_ctx, w_in, w_out, ret_decay_fwd, ret_decay_bwd, ret_gn_g, ret_gn_b, win_sink, na_rpb, w_mod, b_mod, ln1_g, ln1_b, ln2_g, ln2_b, w_router, b_router, w_expert_gu, w_expert_down, w_shared_gu, w_shared_down):
    cond = jnp.concatenate(
        [c_ctx[None, :], c, jnp.zeros((N_COND - 1 - DEC_BATCH, D_MODEL), F32)], axis=0)
    mod_all = _modulation(cond, w_mod, b_mod)
    cos_t, sin_t = _rope_tables()

    x_ctx = x_prompt.reshape(T_CTX, D_MODEL)
    x_lat = x_sample.reshape(T_LAT, D_MODEL)
    sf_l, sb_l, cache_l = [], [], []
    for l in range(DEPTH):
        mod = mod_all[l]
        pa, pb, pc, cache = _in_projection(x_ctx, x_lat, mod, w_in[l].astype(BF16))
        cache_l.append(cache)
        decf_s, decb_s = _lane_rows(ret_decay_fwd[l], SEQ), _lane_rows(ret_decay_bwd[l], SEQ)
        gng, gnb = ret_gn_g[l][None, :], ret_gn_b[l][None, :]
        mix_c, st_f, st_b = _ctx_mixers(pa, pb, pc, win_sink[l], decf_s, decb_s, gng, gnb)
        sf_l.append(st_f)
        sb_l.append(st_b)
        ya = _lat_retention(pa, state_ret_fwd, state_ret_bwd, l,
                            _lane_rows(ret_decay_fwd[l], RET_CHUNK), _lane_rows(ret_decay_bwd[l], RET_CHUNK),
                            gng, gnb)
        yb = _lat_window_attn(pb, cache_win_k, cache_win_v, l, win_sink[l], cos_t, sin_t)
        yc = _lat_na_attn(pc, cache_na_k, cache_na_v, l, _na_maskbias(na_rpb[l]))

        wr_t = w_router[l].T
        wr_hi = wr_t.astype(BF16)
        wr_lo = (wr_t - wr_hi.astype(F32)).astype(BF16)
        x1, h2, eidx, wsel, rank, counts = _post_mixer(
            x_ctx, x_lat, mix_c, ya, yb, yc, w_out[l].astype(BF16), mod, ln1_g[l][None, :],
            ln1_b[l][None, :], wr_hi, wr_lo, b_router[l][:, None])
        lslot, unit_tab, gend, blk_e, n_used = _slot_plan(eidx, rank, counts)
        xs = _dispatch(h2, lslot, unit_tab, gend)
        ys = _experts(xs, w_expert_gu[l], w_expert_down[l], blk_e, n_used)
        x_ctx, x_lat = _combine(x1, h2, lslot.T, wsel.T, unit_tab, ys, w_shared_gu[l].astype(BF16),
                                w_shared_down[l].astype(BF16), mod, ln2_g[l][None, :], ln2_b[l][None, :])

    y_prompt = x_ctx.reshape(BATCH, SEQ, D_MODEL)
    y_sample = x_lat.reshape(DEC_BATCH, DEC_SEQ, D_MODEL)
    new_sf = jnp.stack(sf_l, axis=1)
    new_sb = jnp.stack(sb_l, axis=1)

    def heads_of(cols, n_heads):
        per_layer = [cache[:, cols[0]:cols[1]].reshape(BATCH, SEQ, n_heads, HEAD_DIM).transpose(0, 2, 1, 3)
                     for cache in cache_l]
        return jnp.stack(per_layer, axis=1)

    new_win_k = heads_of((0, KV_W_B), KV_B)
    new_win_v = heads_of((KV_W_B, 2 * KV_W_B), KV_B)
    new_na_k = heads_of((2 * KV_W_B, 2 * KV_W_B + W_C), H_C)
    new_na_v = heads_of((2 * KV_W_B + W_C, CACHE_W), H_C)
    return (y_prompt, y_sample, new_sf, new_sb, new_win_k, new_win_v, new_na_k, new_na_v)
```

```python
import functools

import numpy as np
import jax
import jax.numpy as jnp
from jax import lax
from jax.experimental import pallas as pl
from jax.experimental.pallas import tpu as pltpu

F32 = jnp.float32
BF16 = jnp.bfloat16
I32 = jnp.int32

D_MODEL = 1024
BATCH = 32
SEQ = 256
DEPTH = 2
DEC_BATCH = 4
DEC_SEQ = 2048
PAST_LEN = 256
GRID_W = 64
HEAD_DIM = 64
ATTN_SCALE = HEAD_DIM ** -0.5
H_A = 4
W_A = H_A * HEAD_DIM
GN_EPS = 1e-5
H_B = 6
KV_B = 2
W_B = H_B * HEAD_DIM
KV_W_B = KV_B * HEAD_DIM
WINDOW = 128
ROPE_BASE = 10000.0
H_C = 6
W_C = H_C * HEAD_DIM
NA_ROWS = 8
NA_COLS = 16
IN_WIDTH = 4 * W_A + W_B + 2 * KV_W_B + 3 * W_C
N_EXPERTS = 64
TOP_K = 6
N_GROUPS = 8
GROUP_SIZE = N_EXPERTS // N_GROUPS
TOPK_GROUPS = 4
D_EXPERT = 256
D_SHARED = 256
ROUTE_SCALE = 2.5
ALPHA = (2 * DEPTH) ** 0.25
LN_EPS = 1e-5
NEG_INF = -1e30
PICKED = -3e38

T_CTX = BATCH * SEQ
T_LAT = DEC_BATCH * DEC_SEQ
T_ALL = T_CTX + T_LAT
N_COND = 8

PA_W = 4 * W_A
PB_W = W_B + 2 * KV_W_B
PC_W = 3 * W_C
CACHE_W = 2 * KV_W_B + 2 * W_C

LANES = 128
SUBLANES = 8
VMEM_LIMIT = 56 * 1024 * 1024

TM_PROJ = 512
TM_TOK = 256
RET_CHUNK = 256
WIN_QB = 128
NA_TILE_ROWS = 4
NA_KEY_ROWS = 11
BM = 512
UNIT = SUBLANES
UNITS_PER_BLOCK = BM // UNIT
PERM_CHUNK = 256
N_TOK_TILES = T_ALL // TM_TOK
LOCAL_ROWS = -(-(TM_TOK * TOP_K + N_EXPERTS * (UNIT - 1)) // PERM_CHUNK) * PERM_CHUNK
MAX_UNITS = LOCAL_ROWS // UNIT
N_ASSIGN = T_ALL * TOP_K
N_BLOCKS = -(-(N_ASSIGN + N_TOK_TILES * N_EXPERTS * (UNIT - 1) + N_EXPERTS * (BM - 1)) // BM)
SPARE_BLOCKS = -(-LOCAL_ROWS // BM)
SPARE_UNIT0 = N_BLOCKS * UNITS_PER_BLOCK
N_BLOCKS_ALL = N_BLOCKS + SPARE_BLOCKS
N_SLOTS = N_BLOCKS_ALL * BM
BLK_LANES = -(-N_BLOCKS_ALL // LANES) * LANES


def _dot(a, b):
    return jnp.dot(a, b, preferred_element_type=F32)


def _dot_nt(a, b):
    return lax.dot_general(a, b, (((1,), (1,)), ((), ())), preferred_element_type=F32)


def _silu(x):
    return x * jax.nn.sigmoid(x)


def _log_sigmoid(x):
    return jnp.minimum(x, 0.0) - jnp.log(1.0 + jnp.exp(-jnp.abs(x)))


def _cond_row(tile, tile_rows):
    n_ctx = T_CTX // tile_rows
    per_lat = DEC_SEQ // tile_rows
    return jnp.where(tile < n_ctx, 0, 1 + (tile - n_ctx) // per_lat)


def _layer_norm(x, g, b):
    mu = jnp.mean(x, -1, keepdims=True)
    xc = x - mu
    var = jnp.mean(xc * xc, -1, keepdims=True)
    return xc * lax.rsqrt(var + LN_EPS) * g + b


MOD_TN = 1536


def _mod_kernel(cond_ref, w_ref, b_ref, o_ref):
    s = _silu(cond_ref[...])
    s_hi = s.astype(BF16)
    s_lo = (s - s_hi.astype(F32)).astype(BF16)
    w = w_ref[0]
    w_hi = w.astype(BF16)
    w_lo = (w - w_hi.astype(F32)).astype(BF16)
    o_ref[0] = _dot(s_hi, w_hi) + _dot(s_lo, w_hi) + _dot(s_hi, w_lo) + b_ref[0]


def _modulation(cond, w_mod, b_mod):
    n_out = 6 * D_MODEL
    return pl.pallas_call(
        _mod_kernel,
        out_shape=jax.ShapeDtypeStruct((DEPTH, N_COND, n_out), F32),
        grid=(DEPTH, n_out // MOD_TN),
        in_specs=[
            pl.BlockSpec((N_COND, D_MODEL), lambda l, j: (0, 0)),
            pl.BlockSpec((1, D_MODEL, MOD_TN), lambda l, j: (l, 0, j)),
            pl.BlockSpec((1, 1, MOD_TN), lambda l, j: (l, 0, j)),
        ],
        out_specs=pl.BlockSpec((1, N_COND, MOD_TN), lambda l, j: (l, 0, j)),
        compiler_params=pltpu.CompilerParams(
            dimension_semantics=("arbitrary", "arbitrary"), vmem_limit_bytes=VMEM_LIMIT),
        name="modulation",
    )(cond, w_mod, b_mod.reshape(DEPTH, 1, n_out))


def _inproj_kernel(xc_ref, xl_ref, mod_ref, w_ref, pa_ref, pb_ref, pc_ref, cache_ref):
    i = pl.program_id(0)
    ci = _cond_row(i, TM_PROJ)
    sh = mod_ref[pl.ds(ci, 1), 0:D_MODEL]
    sc = mod_ref[pl.ds(ci, 1), D_MODEL:2 * D_MODEL]
    x = jnp.where(i < T_CTX // TM_PROJ, xc_ref[...], xl_ref[...])
    h = x * (1.0 + sc) + sh
    p = _dot(h.astype(BF16), w_ref[...])
    pa_ref[...] = p[:, 0:PA_W].astype(BF16)
    pb_ref[...] = p[:, PA_W:PA_W + PB_W].astype(BF16)
    pc_ref[...] = p[:, PA_W + PB_W:IN_WIDTH].astype(BF16)

    @pl.when(i < T_CTX // TM_PROJ)
    def _():
        cache_ref[:, 0:2 * KV_W_B] = p[:, PA_W + W_B:PA_W + PB_W]
        cache_ref[:, 2 * KV_W_B:CACHE_W] = p[:, PA_W + PB_W + W_C:IN_WIDTH]


def _in_projection(x_ctx, x_lat, mod, w_in_bf16):
    n_ctx_tiles = T_CTX // TM_PROJ
    return pl.pallas_call(
        _inproj_kernel,
        out_shape=(
            jax.ShapeDtypeStruct((T_ALL, PA_W), BF16),
            jax.ShapeDtypeStruct((T_ALL, PB_W), BF16),
            jax.ShapeDtypeStruct((T_ALL, PC_W), BF16),
            jax.ShapeDtypeStruct((T_CTX, CACHE_W), F32),
        ),
        grid=(T_ALL // TM_PROJ,),
        in_specs=[
            pl.BlockSpec((TM_PROJ, D_MODEL), lambda i: (jnp.minimum(i, n_ctx_tiles - 1), 0)),
            pl.BlockSpec((TM_PROJ, D_MODEL), lambda i: (jnp.maximum(i - n_ctx_tiles, 0), 0)),
            pl.BlockSpec((N_COND, 6 * D_MODEL), lambda i: (0, 0)),
            pl.BlockSpec((D_MODEL, IN_WIDTH), lambda i: (0, 0)),
        ],
        out_specs=(
            pl.BlockSpec((TM_PROJ, PA_W), lambda i: (i, 0)),
            pl.BlockSpec((TM_PROJ, PB_W), lambda i: (i, 0)),
            pl.BlockSpec((TM_PROJ, PC_W), lambda i: (i, 0)),
            pl.BlockSpec((TM_PROJ, CACHE_W), lambda i: (jnp.minimum(i, n_ctx_tiles - 1), 0)),
        ),
        compiler_params=pltpu.CompilerParams(
            dimension_semantics=("arbitrary",), vmem_limit_bytes=VMEM_LIMIT),
        name="in_projection",
    )(x_ctx, x_lat, mod, w_in_bf16)


def _decay_matrix(lg_f, lg_b, n):
    row = lax.broadcasted_iota(I32, (n, n), 0)
    col = lax.broadcasted_iota(I32, (n, n), 1)
    diff = (row - col).astype(F32)
    fwd = jnp.where(diff >= 0, jnp.exp(lg_f * jnp.maximum(diff, 0.0)), 0.0)
    bwd = jnp.where(diff <= 0, jnp.exp(lg_b * jnp.maximum(-diff, 0.0)), 0.0)
    return (fwd + bwd) * ATTN_SCALE


def _retention_readout(o, gate, g, b):
    mu = jnp.mean(o, -1, keepdims=True)
    oc = o - mu
    var = jnp.mean(oc * oc, -1, keepdims=True)
    on = oc * lax.rsqrt(var + GN_EPS) * g + b
    return on * _silu(gate.astype(F32))


def _softmax_attend(s, v, extra_logit=None):
    m = jnp.max(s, -1, keepdims=True)
    if extra_logit is not None:
        m = jnp.maximum(m, extra_logit)
    p = jnp.exp(s - m)
    den = jnp.sum(p, -1, keepdims=True)
    if extra_logit is not None:
        den = den + jnp.exp(extra_logit - m)
    return _dot(p.astype(BF16), v) / den


def _ctx_mixer_kernel(sink_ref, pa_ref, pb_ref, pc_ref, decf_ref, decb_ref, gng_ref, gnb_ref,
                      mix_ref, sf_ref, sb_ref):
    n = SEQ
    hd = HEAD_DIM
    pos = lax.broadcasted_iota(I32, (n, hd), 0).astype(F32)
    for h in range(H_A):
        c0 = h * hd
        q = pa_ref[:, c0:c0 + hd]
        k = pa_ref[:, W_A + c0:W_A + c0 + hd]
        v = pa_ref[:, 2 * W_A + c0:2 * W_A + c0 + hd]
        gate = pa_ref[:, 3 * W_A + c0:3 * W_A + c0 + hd]
        lg_f = _log_sigmoid(decf_ref[h])
        lg_b = _log_sigmoid(decb_ref[h])
        dmat = _decay_matrix(lg_f, lg_b, n)
        o = _dot((_dot_nt(q, k) * dmat).astype(BF16), v)
        kf = k.astype(F32)
        zf = jnp.exp(lg_f[:, 0:hd] * (n - 1.0 - pos)) * ATTN_SCALE
        zb = jnp.exp(lg_b[:, 0:hd] * pos) * ATTN_SCALE
        sf_ref[0, h] = _dot((kf * zf).T.astype(BF16), v)
        sb_ref[0, h] = _dot((kf * zb).T.astype(BF16), v)
        y = _retention_readout(o, gate, gng_ref[:, c0:c0 + hd], gnb_ref[:, c0:c0 + hd])
        mix_ref[:, c0:c0 + hd] = y.astype(BF16)
    for j in range(KV_B):
        k = pb_ref[:, W_B + j * hd:W_B + (j + 1) * hd]
        v = pb_ref[:, W_B + KV_W_B + j * hd:W_B + KV_W_B + (j + 1) * hd]
        for g in range(H_B // KV_B):
            hh = j * (H_B // KV_B) + g
            q = pb_ref[:, hh * hd:(hh + 1) * hd]
            o = _softmax_attend(_dot_nt(q, k) * ATTN_SCALE, v, sink_ref[hh])
            mix_ref[:, W_A + hh * hd:W_A + (hh + 1) * hd] = o.astype(BF16)
    for hh in range(H_C):
        q = pc_ref[:, hh * hd:(hh + 1) * hd]
        k = pc_ref[:, W_C + hh * hd:W_C + (hh + 1) * hd]
        v = pc_ref[:, 2 * W_C + hh * hd:2 * W_C + (hh + 1) * hd]
        o = _softmax_attend(_dot_nt(q, k) * ATTN_SCALE, v)
        mix_ref[:, W_A + W_B + hh * hd:W_A + W_B + (hh + 1) * hd] = o.astype(BF16)


def _ctx_mixers(pa, pb, pc, sink, decf, decb, gng, gnb):
    return pl.pallas_call(
        _ctx_mixer_kernel,
        out_shape=(
            jax.ShapeDtypeStruct((T_CTX, D_MODEL), BF16),
            jax.ShapeDtypeStruct((BATCH, H_A, HEAD_DIM, HEAD_DIM), F32),
            jax.ShapeDtypeStruct((BATCH, H_A, HEAD_DIM, HEAD_DIM), F32),
        ),
        grid=(BATCH,),
        in_specs=[
            pl.BlockSpec(memory_space=pltpu.SMEM),
            pl.BlockSpec((SEQ, PA_W), lambda b: (b, 0)),
            pl.BlockSpec((SEQ, PB_W), lambda b: (b, 0)),
            pl.BlockSpec((SEQ, PC_W), lambda b: (b, 0)),
            pl.BlockSpec((H_A, 1, SEQ), lambda b: (0, 0, 0)),
            pl.BlockSpec((H_A, 1, SEQ), lambda b: (0, 0, 0)),
            pl.BlockSpec((1, W_A), lambda b: (0, 0)),
            pl.BlockSpec((1, W_A), lambda b: (0, 0)),
        ],
        out_specs=(
            pl.BlockSpec((SEQ, D_MODEL), lambda b: (b, 0)),
            pl.BlockSpec((1, H_A, HEAD_DIM, HEAD_DIM), lambda b: (b, 0, 0, 0)),
            pl.BlockSpec((1, H_A, HEAD_DIM, HEAD_DIM), lambda b: (b, 0, 0, 0)),
        ),
        compiler_params=pltpu.CompilerParams(
            dimension_semantics=("arbitrary",), vmem_limit_bytes=VMEM_LIMIT),
        name="ctx_mixers",
    )(sink, pa, pb, pc, decf, decb, gng, gnb)


def _lat_ret_kernel(pa_ref, stf_ref, stb_ref, decf_ref, decb_ref, gng_ref, gnb_ref, ya_ref, acc_ref):
    c = RET_CHUNK
    hd = HEAD_DIM
    n_chunks = DEC_SEQ // c
    pos = lax.broadcasted_iota(I32, (c, hd), 0).astype(F32)
    for h in range(H_A):
        c0 = h * hd
        lg_f = _log_sigmoid(decf_ref[h])
        lg_b = _log_sigmoid(decb_ref[h])
        dmat = _decay_matrix(lg_f, lg_b, c)
        lf = lg_f[:, 0:hd]
        lb = lg_b[:, 0:hd]
        zf = jnp.exp(lf * (c - 1.0 - pos)) * ATTN_SCALE
        zb = jnp.exp(lb * pos) * ATTN_SCALE
        xf = jnp.exp(lf * (pos + 1.0))
        xb = jnp.exp(lb * (c - pos))
        gcf = jnp.exp(lf * float(c))
        gcb = jnp.exp(lb * float(c))
        g = gng_ref[:, c0:c0 + hd]
        b = gnb_ref[:, c0:c0 + hd]

        def load(i, off):
            rows = pl.ds(pl.multiple_of(i * c, c), c)
            return pa_ref[rows, off + c0:off + c0 + hd]

        def fwd(i, s):
            q, k, v = load(i, 0), load(i, W_A), load(i, 2 * W_A)
            o = _dot((_dot_nt(q, k) * dmat).astype(BF16), v)
            o = o + _dot((q.astype(F32) * xf).astype(BF16), s.astype(BF16))
            acc_ref[pl.ds(pl.multiple_of(i * c, c), c), c0:c0 + hd] = o
            return gcf * s + _dot((k.astype(F32) * zf).T.astype(BF16), v)

        lax.fori_loop(0, n_chunks, fwd, stf_ref[0, 0, h])

        def bwd(j, s):
            i = n_chunks - 1 - j
            rows = pl.ds(pl.multiple_of(i * c, c), c)
            q, k, v = load(i, 0), load(i, W_A), load(i, 2 * W_A)
            o = acc_ref[rows, c0:c0 + hd] + _dot((q.astype(F32) * xb).astype(BF16), s.astype(BF16))
            y = _retention_readout(o, load(i, 3 * W_A), g, b)
            ya_ref[rows, c0:c0 + hd] = y.astype(BF16)
            return gcb * s + _dot((k.astype(F32) * zb).T.astype(BF16), v)

        lax.fori_loop(0, n_chunks, bwd, stb_ref[0, 0, h])


def _lat_retention(pa, st_f, st_b, layer, decf, decb, gng, gnb):
    lat0 = T_CTX // DEC_SEQ
    st_spec = pl.BlockSpec((1, 1, H_A, HEAD_DIM, HEAD_DIM), lambda b: (b, layer, 0, 0, 0))
    return pl.pallas_call(
        _lat_ret_kernel,
        out_shape=jax.ShapeDtypeStruct((T_LAT, W_A), BF16),
        grid=(DEC_BATCH,),
        in_specs=[
            pl.BlockSpec((DEC_SEQ, PA_W), lambda b: (lat0 + b, 0)),
            st_spec, st_spec,
            pl.BlockSpec((H_A, 1, RET_CHUNK), lambda b: (0, 0, 0)),
            pl.BlockSpec((H_A, 1, RET_CHUNK), lambda b: (0, 0, 0)),
            pl.BlockSpec((1, W_A), lambda b: (0, 0)),
            pl.BlockSpec((1, W_A), lambda b: (0, 0)),
        ],
        out_specs=pl.BlockSpec((DEC_SEQ, W_A), lambda b: (b, 0)),
        scratch_shapes=[pltpu.VMEM((DEC_SEQ, W_A), F32)],
        compiler_params=pltpu.CompilerParams(
            dimension_semantics=("arbitrary",), vmem_limit_bytes=VMEM_LIMIT),
        name="lat_retention",
    )(pa, st_f, st_b, decf, decb, gng, gnb)


def _swap_halves_matrix(width):
    r = lax.broadcasted_iota(I32, (width, width), 0)
    c = lax.broadcasted_iota(I32, (width, width), 1)
    return jnp.where((r ^ (HEAD_DIM // 2)) == c, 1.0, 0.0).astype(BF16)


def _rope(x, cos, sin_signed, swap):
    return x.astype(F32) * cos + _dot(x, swap) * sin_signed


def _lat_win_kernel(sink_ref, pq_ref, pseq_ref, kctx_ref, vctx_ref, cos_ref, sin_ref, yb_ref, krope_ref):
    n = pl.program_id(1)
    hd = HEAD_DIM
    qb = WIN_QB
    n_blk = DEC_SEQ // qb
    group = H_B // KV_B
    swap = _swap_halves_matrix(LANES)

    @pl.when(n == 0)
    def _():
        k = pseq_ref[:, W_B:W_B + KV_W_B]
        krope_ref[...] = _rope(k, cos_ref[...], sin_ref[...], swap).astype(BF16)

    q_rows = pl.ds(pl.multiple_of(n * qb, qb), qb)
    cos_q = cos_ref[q_rows, :]
    sin_q = sin_ref[q_rows, :]
    qr = [_rope(pq_ref[:, p * LANES:(p + 1) * LANES], cos_q, sin_q, swap).astype(BF16)
          for p in range(W_B // LANES)]

    ws = jnp.clip(n - 1, 0, n_blk - 3) * qb
    k_rows = pl.ds(pl.multiple_of(ws, qb), 3 * qb)
    q_pos = n * qb + lax.broadcasted_iota(I32, (group * qb, 3 * qb), 0) % qb
    k_pos = ws + lax.broadcasted_iota(I32, (group * qb, 3 * qb), 1)
    valid = jnp.abs(k_pos - q_pos) <= WINDOW
    head_of_row = lax.broadcasted_iota(I32, (group * qb, 1), 0) // qb
    for j in range(KV_B):
        heads = [j * group + g for g in range(group)]
        qs = jnp.concatenate(
            [qr[hh // 2][:, (hh % 2) * hd:(hh % 2 + 1) * hd] for hh in heads], axis=0)
        kw = krope_ref[k_rows, j * hd:(j + 1) * hd]
        vw = pseq_ref[k_rows, W_B + KV_W_B + j * hd:W_B + KV_W_B + (j + 1) * hd]
        kc = kctx_ref[0, 0, j].astype(BF16)
        vc = vctx_ref[0, 0, j].astype(BF16)
        s_loc = jnp.where(valid, _dot_nt(qs, kw) * ATTN_SCALE, NEG_INF)
        s_ctx = _dot_nt(qs, kc) * ATTN_SCALE
        sink = jnp.zeros((group * qb, 1), F32)
        for g, hh in enumerate(heads):
            sink = jnp.where(head_of_row == g, sink_ref[hh], sink)
        m = jnp.maximum(jnp.maximum(jnp.max(s_loc, -1, keepdims=True),
                                    jnp.max(s_ctx, -1, keepdims=True)), sink)
        p_loc = jnp.exp(s_loc - m)
        p_ctx = jnp.exp(s_ctx - m)
        den = (jnp.sum(p_loc, -1, keepdims=True) + jnp.sum(p_ctx, -1, keepdims=True)
               + jnp.exp(sink - m))
        o = (_dot(p_loc.astype(BF16), vw) + _dot(p_ctx.astype(BF16), vc)) / den
        for g, hh in enumerate(heads):
            yb_ref[:, hh * hd:(hh + 1) * hd] = o[g * qb:(g + 1) * qb].astype(BF16)


def _lat_window_attn(pb, cache_k, cache_v, layer, sink, cos_t, sin_t):
    n_blk = DEC_SEQ // WIN_QB
    lat_blk0 = T_CTX // WIN_QB
    lat_seq0 = T_CTX // DEC_SEQ
    ctx_spec = pl.BlockSpec((1, 1, KV_B, PAST_LEN, HEAD_DIM), lambda b, n: (b, layer, 0, 0, 0))
    return pl.pallas_call(
        _lat_win_kernel,
        out_shape=jax.ShapeDtypeStruct((T_LAT, W_B), BF16),
        grid=(DEC_BATCH, n_blk),
        in_specs=[
            pl.BlockSpec(memory_space=pltpu.SMEM),
            pl.BlockSpec((WIN_QB, PB_W), lambda b, n: (lat_blk0 + b * n_blk + n, 0)),
            pl.BlockSpec((DEC_SEQ, PB_W), lambda b, n: (lat_seq0 + b, 0)),
            ctx_spec, ctx_spec,
            pl.BlockSpec((DEC_SEQ, LANES), lambda b, n: (0, 0)),
            pl.BlockSpec((DEC_SEQ, LANES), lambda b, n: (0, 0)),
        ],
        out_specs=pl.BlockSpec((WIN_QB, W_B), lambda b, n: (b * n_blk + n, 0)),
        scratch_shapes=[pltpu.VMEM((DEC_SEQ, KV_W_B), BF16)],
        compiler_params=pltpu.CompilerParams(
            dimension_semantics=("arbitrary", "arbitrary"), vmem_limit_bytes=VMEM_LIMIT),
        name="lat_window_attn",
    )(sink, pb, pb, cache_k, cache_v, cos_t, sin_t)


NA_Q = NA_TILE_ROWS * GRID_W
NA_K = NA_KEY_ROWS * GRID_W
NA_TILES = DEC_SEQ // NA_Q
LAT_ROWS = DEC_SEQ // GRID_W


def _na_window_start(tile):
    return jnp.clip(tile * NA_TILE_ROWS - NA_ROWS // 2, 0, LAT_ROWS - NA_KEY_ROWS)


def _lat_na_kernel(pq_ref, pseq_ref, kctx_ref, vctx_ref, bias_ref, yc_ref):
    t = pl.program_id(1)
    hd = HEAD_DIM
    k_rows = pl.ds(pl.multiple_of(_na_window_start(t) * GRID_W, GRID_W), NA_K)
    for hh in range(H_C):
        q = pq_ref[:, hh * hd:(hh + 1) * hd]
        kw = pseq_ref[k_rows, W_C + hh * hd:W_C + (hh + 1) * hd]
        vw = pseq_ref[k_rows, 2 * W_C + hh * hd:2 * W_C + (hh + 1) * hd]
        kc = kctx_ref[0, 0, hh].astype(BF16)
        vc = vctx_ref[0, 0, hh].astype(BF16)
        s_loc = _dot_nt(q, kw) * ATTN_SCALE + bias_ref[0, hh]
        s_ctx = _dot_nt(q, kc) * ATTN_SCALE
        m = jnp.maximum(jnp.max(s_loc, -1, keepdims=True), jnp.max(s_ctx, -1, keepdims=True))
        p_loc = jnp.exp(s_loc - m)
        p_ctx = jnp.exp(s_ctx - m)
        den = jnp.sum(p_loc, -1, keepdims=True) + jnp.sum(p_ctx, -1, keepdims=True)
        o = (_dot(p_loc.astype(BF16), vw) + _dot(p_ctx.astype(BF16), vc)) / den
        yc_ref[:, hh * hd:(hh + 1) * hd] = o.astype(BF16)


def _na_tile_type(t):
    return jnp.where(t == 0, 0, jnp.where(t == NA_TILES - 1, 2, 1))


def _lat_na_attn(pc, cache_k, cache_v, layer, maskbias):
    lat_tile0 = T_CTX // NA_Q
    lat_seq0 = T_CTX // DEC_SEQ
    ctx_spec = pl.BlockSpec((1, 1, H_C, PAST_LEN, HEAD_DIM), lambda b, t: (b, layer, 0, 0, 0))
    return pl.pallas_call(
        _lat_na_kernel,
        out_shape=jax.ShapeDtypeStruct((T_LAT, W_C), BF16),
        grid=(DEC_BATCH, NA_TILES),
        in_specs=[
            pl.BlockSpec((NA_Q, PC_W), lambda b, t: (lat_tile0 + b * NA_TILES + t, 0)),
            pl.BlockSpec((DEC_SEQ, PC_W), lambda b, t: (lat_seq0 + b, 0)),
            ctx_spec, ctx_spec,
            pl.BlockSpec((1, H_C, NA_Q, NA_K), lambda b, t: (_na_tile_type(t), 0, 0, 0)),
        ],
        out_specs=pl.BlockSpec((NA_Q, W_C), lambda b, t: (b * NA_TILES + t, 0)),
        compiler_params=pltpu.CompilerParams(
            dimension_semantics=("arbitrary", "arbitrary"), vmem_limit_bytes=VMEM_LIMIT),
        name="lat_na_attn",
    )(pc, pc, cache_k, cache_v, maskbias)


def _na_block_index():
    out = np.zeros((3, NA_TILE_ROWS, NA_KEY_ROWS), np.int32)
    for ty, tile in enumerate((0, 1, NA_TILES - 1)):
        r = tile * NA_TILE_ROWS
        ws = int(np.clip(r - NA_ROWS // 2, 0, LAT_ROWS - NA_KEY_ROWS))
        for qq in range(NA_TILE_ROWS):
            qr = r + qq
            r0 = int(np.clip(qr - NA_ROWS // 2, 0, LAT_ROWS - NA_ROWS))
            for kk in range(NA_KEY_ROWS):
                kr = ws + kk
                out[ty, qq, kk] = kr - qr + NA_ROWS - 1 if r0 <= kr < r0 + NA_ROWS else 2 * NA_ROWS - 1
    return out


def _na_maskbias(rpb):
    qc = np.arange(GRID_W)[:, None]
    kc = np.arange(GRID_W)[None, :]
    c0 = np.clip(qc - NA_COLS // 2, 0, GRID_W - NA_COLS)
    col_ok = (kc >= c0) & (kc < c0 + NA_COLS)
    ci = np.clip(kc - qc + NA_COLS - 1, 0, 2 * NA_COLS - 2)
    onehot = (ci[None] == np.arange(2 * NA_COLS - 1)[:, None, None]).astype(np.float32)
    cols = jnp.einsum("hab,bqk->haqk", rpb, jnp.asarray(onehot), precision=lax.Precision.HIGHEST)
    cols = jnp.where(jnp.asarray(col_ok)[None, None], cols, NEG_INF)
    cols = jnp.concatenate([cols, jnp.full((H_C, 1, GRID_W, GRID_W), NEG_INF, F32)], axis=1)
    blocks = jnp.take(cols, jnp.asarray(_na_block_index().reshape(-1)), axis=1)
    blocks = blocks.reshape(H_C, 3, NA_TILE_ROWS, NA_KEY_ROWS, GRID_W, GRID_W)
    return blocks.transpose(1, 0, 2, 4, 3, 5).reshape(3, H_C, NA_Q, NA_K)


def _rope_tables():
    t = np.arange(DEC_SEQ)
    n_freq = HEAD_DIM // 4
    inv = (ROPE_BASE ** (-np.arange(n_freq, dtype=np.float32) / n_freq)).astype(np.float32)
    row = (t // GRID_W).astype(np.float32)[:, None] * inv
    col = (t % GRID_W).astype(np.float32)[:, None] * inv
    ang = np.concatenate([row, col], -1)
    cos, sin = np.cos(ang), np.sin(ang)
    cos_h = np.concatenate([cos, cos], -1)
    sin_h = np.concatenate([-sin, sin], -1)
    reps = LANES // HEAD_DIM
    return (jnp.asarray(np.tile(cos_h, (1, reps)), F32), jnp.asarray(np.tile(sin_h, (1, reps)), F32))


def _first_index_of(mask, iota, sentinel):
    return jnp.min(jnp.where(mask, iota, sentinel), axis=0, keepdims=True)


def _route(logits, b_col):
    n = logits.shape[1]
    scores = jax.nn.sigmoid(logits)
    sel = scores + b_col
    io_g = lax.broadcasted_iota(I32, (GROUP_SIZE, n), 0)
    gs_rows = []
    for g in range(N_GROUPS):
        s = sel[g * GROUP_SIZE:(g + 1) * GROUP_SIZE]
        m1 = jnp.max(s, axis=0, keepdims=True)
        i1 = _first_index_of(s == m1, io_g, GROUP_SIZE)
        m2 = jnp.max(jnp.where(io_g == i1, PICKED, s), axis=0, keepdims=True)
        gs_rows.append(m1 + m2)
    gs = jnp.concatenate(gs_rows, axis=0)
    io_n = lax.broadcasted_iota(I32, (N_GROUPS, n), 0)
    gsel = jnp.zeros((N_GROUPS, n), F32)
    for _ in range(TOPK_GROUPS):
        mg = jnp.max(gs, axis=0, keepdims=True)
        gi = _first_index_of(gs == mg, io_n, N_GROUPS)
        hit = io_n == gi
        gsel = jnp.where(hit, 1.0, gsel)
        gs = jnp.where(hit, PICKED, gs)
    cand = jnp.concatenate(
        [jnp.where(gsel[g:g + 1] > 0.5, sel[g * GROUP_SIZE:(g + 1) * GROUP_SIZE], NEG_INF)
         for g in range(N_GROUPS)], axis=0)
    io_e = lax.broadcasted_iota(I32, (N_EXPERTS, n), 0)
    picks, raw = [], []
    for _ in range(TOP_K):
        mv = jnp.max(cand, axis=0, keepdims=True)
        ei = _first_index_of(cand == mv, io_e, N_EXPERTS)
        hit = io_e == ei
        picks.append((hit, ei))
        raw.append(jnp.sum(jnp.where(hit, scores, 0.0), axis=0, keepdims=True))
        cand = jnp.where(hit, PICKED, cand)
    return picks, raw


def _post_mixer_kernel(xc_ref, xl_ref, mixc_ref, ya_ref, yb_ref, yc_ref, wout_ref, mod_ref, g_ref, b_ref,
                       wrh_ref, wrl_ref, br_ref,
                       x1_ref, h2_ref, eidx_ref, wsel_ref, rank_ref, cnt_ref):
    i = pl.program_id(0)
    tm = TM_TOK

    @pl.when(i == 0)
    def _():
        cnt_ref[...] = jnp.zeros_like(cnt_ref)

    ci = _cond_row(i, tm)
    gate1 = mod_ref[pl.ds(ci, 1), 2 * D_MODEL:3 * D_MODEL]
    sh2 = mod_ref[pl.ds(ci, 1), 3 * D_MODEL:4 * D_MODEL]
    sc2 = mod_ref[pl.ds(ci, 1), 4 * D_MODEL:5 * D_MODEL]
    mix_lat = jnp.concatenate([ya_ref[...], yb_ref[...], yc_ref[...]], axis=-1)
    mix = jnp.where(i < T_CTX // tm, mixc_ref[...], mix_lat)
    y = _dot(mix, wout_ref[...])
    x = jnp.where(i < T_CTX // tm, xc_ref[...], xl_ref[...])
    x1 = _layer_norm(ALPHA * x + gate1 * y, g_ref[...], b_ref[...])
    x1_ref[...] = x1
    h2 = x1 * (1.0 + sc2) + sh2
    h_hi = h2.astype(BF16)
    h2_ref[...] = h_hi
    h_lo = (h2 - h_hi.astype(F32)).astype(BF16)
    logits = (_dot_nt(wrh_ref[...], h_hi) + _dot_nt(wrh_ref[...], h_lo)
              + _dot_nt(wrl_ref[...], h_hi))
    picks, raw = _route(logits, br_ref[...])

    total = raw[0]
    for r in raw[1:]:
        total = total + r
    scale = ROUTE_SCALE / total
    multi = jnp.zeros((N_EXPERTS, tm), F32)
    for hit, _ in picks:
        multi = multi + jnp.where(hit, 1.0, 0.0)
    before = (lax.broadcasted_iota(I32, (tm, tm), 0) < lax.broadcasted_iota(I32, (tm, tm), 1))
    cum = _dot(multi.astype(BF16), jnp.where(before, 1.0, 0.0).astype(BF16))
    pad = jnp.zeros((SUBLANES - TOP_K, tm), F32)
    eidx_ref[...] = jnp.concatenate([ei for _, ei in picks] + [pad.astype(I32)], axis=0)
    wsel_ref[...] = jnp.concatenate([r * scale for r in raw] + [pad], axis=0)
    rank_ref[...] = jnp.concatenate(
        [jnp.sum(jnp.where(hit, cum, 0.0), axis=0, keepdims=True) for hit, _ in picks] + [pad],
        axis=0).astype(I32)
    tile_lane = lax.broadcasted_iota(I32, (N_EXPERTS, LANES), 1)
    cnt_ref[...] = jnp.where(tile_lane == i, jnp.sum(multi, axis=1, keepdims=True), cnt_ref[...])


def _post_mixer(x_ctx, x_lat, mix_c, ya, yb, yc, w_out_bf16, mod, ln_g, ln_b, wr_hi_t, wr_lo_t, b_router_col):
    n_ctx = T_CTX // TM_TOK
    ctx_map = lambda i: (jnp.minimum(i, n_ctx - 1), 0)
    lat_map = lambda i: (jnp.maximum(i - n_ctx, 0), 0)
    row_map = lambda i: (i, 0)
    const = lambda i: (0, 0)
    tok_map = lambda i: (0, i)
    return pl.pallas_call(
        _post_mixer_kernel,
        out_shape=(
            jax.ShapeDtypeStruct((T_ALL, D_MODEL), F32),
            jax.ShapeDtypeStruct((T_ALL, D_MODEL), BF16),
            jax.ShapeDtypeStruct((SUBLANES, T_ALL), I32),
            jax.ShapeDtypeStruct((SUBLANES, T_ALL), F32),
            jax.ShapeDtypeStruct((SUBLANES, T_ALL), I32),
            jax.ShapeDtypeStruct((N_EXPERTS, LANES), F32),
        ),
        grid=(N_TOK_TILES,),
        in_specs=[
            pl.BlockSpec((TM_TOK, D_MODEL), ctx_map),
            pl.BlockSpec((TM_TOK, D_MODEL), lat_map),
            pl.BlockSpec((TM_TOK, D_MODEL), ctx_map),
            pl.BlockSpec((TM_TOK, W_A), lat_map),
            pl.BlockSpec((TM_TOK, W_B), lat_map),
            pl.BlockSpec((TM_TOK, W_C), lat_map),
            pl.BlockSpec((D_MODEL, D_MODEL), const),
            pl.BlockSpec((N_COND, 6 * D_MODEL), const),
            pl.BlockSpec((1, D_MODEL), const),
            pl.BlockSpec((1, D_MODEL), const),
            pl.BlockSpec((N_EXPERTS, D_MODEL), const),
            pl.BlockSpec((N_EXPERTS, D_MODEL), const),
            pl.BlockSpec((N_EXPERTS, 1), const),
        ],
        out_specs=(
            pl.BlockSpec((TM_TOK, D_MODEL), row_map),
            pl.BlockSpec((TM_TOK, D_MODEL), row_map),
            pl.BlockSpec((SUBLANES, TM_TOK), tok_map),
            pl.BlockSpec((SUBLANES, TM_TOK), tok_map),
            pl.BlockSpec((SUBLANES, TM_TOK), tok_map),
            pl.BlockSpec((N_EXPERTS, LANES), const),
        ),
        compiler_params=pltpu.CompilerParams(
            dimension_semantics=("arbitrary",), vmem_limit_bytes=VMEM_LIMIT),
        name="post_mixer",
    )(x_ctx, x_lat, mix_c, ya, yb, yc, w_out_bf16, mod, ln_g, ln_b, wr_hi_t, wr_lo_t, b_router_col)


def _plan_kernel(eidx_ref, rank_ref, nmat_ref, lslot_ref, unit_ref, gend_ref, blk_ref):
    i = pl.program_id(0)
    units = jnp.floor((nmat_ref[...] + (UNIT - 1.0)) * (1.0 / UNIT))
    units_bf = units.astype(BF16)
    earlier_e = (lax.broadcasted_iota(I32, (N_EXPERTS, N_EXPERTS), 1)
                 < lax.broadcasted_iota(I32, (N_EXPERTS, N_EXPERTS), 0))
    tri_e = jnp.where(earlier_e, 1.0, 0.0).astype(BF16)
    earlier_t = (lax.broadcasted_iota(I32, (LANES, LANES), 0) < lax.broadcasted_iota(I32, (LANES, LANES), 1))
    tri_t = jnp.where(earlier_t, 1.0, 0.0).astype(BF16)
    local_off = _dot(tri_e, units_bf)
    tile_off = _dot(units_bf, tri_t)
    per_expert = jnp.sum(units, axis=1, keepdims=True)
    blocks = jnp.floor((per_expert + (UNITS_PER_BLOCK - 1.0)) * (1.0 / UNITS_PER_BLOCK))
    blocks_l = jnp.broadcast_to(blocks, (N_EXPERTS, LANES))
    start_blk = _dot(tri_e, blocks_l.astype(BF16))
    end_blk = start_blk + blocks_l
    gend_ref[...] = (end_blk * BM).astype(I32)
    blk_id = lax.broadcasted_iota(I32, (N_EXPERTS, BLK_LANES), 1).astype(F32)
    owner = jnp.sum(jnp.where(end_blk[:, 0:1] <= blk_id, 1.0, 0.0), axis=0, keepdims=True)
    blk_ref[...] = jnp.minimum(owner, N_EXPERTS - 1.0).astype(I32)

    this_tile = lax.broadcasted_iota(I32, (N_EXPERTS, LANES), 1) == i

    def column(a):
        return jnp.sum(jnp.where(this_tile, a, 0.0), axis=1, keepdims=True)

    lo, n_u = column(local_off), column(units)
    base_unit = start_blk[:, 0:1] * UNITS_PER_BLOCK + column(tile_off) - lo
    u = lax.broadcasted_iota(I32, (N_EXPERTS, MAX_UNITS), 1).astype(F32)
    inside = jnp.where(u >= lo, jnp.where(u < lo + n_u, 1.0, 0.0), 0.0)
    dst_unit = jnp.sum(inside * (base_unit + u), axis=0, keepdims=True)
    used = jnp.sum(inside, axis=0, keepdims=True) > 0.5
    unit_ref[0] = jnp.where(used, dst_unit, SPARE_UNIT0 + u[0:1, :]).astype(I32)

    io_e = lax.broadcasted_iota(I32, (N_EXPERTS, TM_TOK), 0)
    rows = []
    for k in range(TOP_K):
        hit = io_e == eidx_ref[k:k + 1, :]
        seg = jnp.sum(jnp.where(hit, lo * UNIT, 0.0), axis=0, keepdims=True)
        rows.append(seg.astype(I32) + rank_ref[k:k + 1, :])
    rows.append(jnp.full((SUBLANES - TOP_K, TM_TOK), -1, I32))
    lslot_ref[...] = jnp.concatenate(rows, axis=0)


def _slot_plan(eidx, rank, nmat):
    tok_map = lambda i: (0, i)
    const = lambda i: (0, 0)
    lslot, unit_tab, gend, blk = pl.pallas_call(
        _plan_kernel,
        out_shape=(
            jax.ShapeDtypeStruct((SUBLANES, T_ALL), I32),
            jax.ShapeDtypeStruct((N_TOK_TILES, 1, MAX_UNITS), I32),
            jax.ShapeDtypeStruct((N_EXPERTS, LANES), I32),
            jax.ShapeDtypeStruct((1, BLK_LANES), I32),
        ),
        grid=(N_TOK_TILES,),
        in_specs=[
            pl.BlockSpec((SUBLANES, TM_TOK), tok_map),
            pl.BlockSpec((SUBLANES, TM_TOK), tok_map),
            pl.BlockSpec((N_EXPERTS, LANES), const),
        ],
        out_specs=(
            pl.BlockSpec((SUBLANES, TM_TOK), tok_map),
            pl.BlockSpec((1, 1, MAX_UNITS), lambda i: (i, 0, 0)),
            pl.BlockSpec((N_EXPERTS, LANES), const),
            pl.BlockSpec((1, BLK_LANES), const),
        ),
        compiler_params=pltpu.CompilerParams(dimension_semantics=("arbitrary",)),
        name="slot_plan",
    )(eidx, rank, nmat)
    gend = gend[:, 0]
    n_used = gend[N_EXPERTS - 1:] // BM
    return lslot, unit_tab, gend, blk[0, :N_BLOCKS_ALL], n_used


PACK_W = D_MODEL // 2
HI_HALF = -65536


def _pack_pairs(x):
    lo = lax.bitcast_convert_type(x[:, 0:PACK_W], I32)
    hi = lax.bitcast_convert_type(x[:, PACK_W:D_MODEL], I32)
    return lax.shift_right_logical(lo, 16) | (hi & HI_HALF)


def _unpack_pairs(u):
    lo = lax.bitcast_convert_type(lax.shift_left(u, 16), F32).astype(BF16)
    hi = lax.bitcast_convert_type(u & HI_HALF, F32).astype(BF16)
    return lo, hi


def _unit_rows(unit):
    row = unit * UNIT
    return pl.ds(row if isinstance(unit, int) else pl.multiple_of(row, UNIT), UNIT)


def _unit_copy(src, src_unit, dst, dst_unit, sem):
    return pltpu.make_async_copy(src.at[_unit_rows(src_unit)], dst.at[_unit_rows(dst_unit)], sem)


def _dispatch_kernel(gend_ref, h2_ref, lslot_ref, unit_hbm, xs_hbm, tab_smem, zero_ref, local_ref,
                     sem_tab, sem_zero, sem_rows):
    i = pl.program_id(0)
    tab_copy = pltpu.make_async_copy(unit_hbm.at[i], tab_smem, sem_tab)
    tab_copy.start()

    def has_rows(e):
        return gend_ref[e] > jnp.where(e == 0, 0, gend_ref[jnp.maximum(e - 1, 0)])

    def zero_copy(e):
        return pltpu.make_async_copy(
            zero_ref, xs_hbm.at[pl.ds(pl.multiple_of(gend_ref[e] - BM, BM), BM)], sem_zero)

    @pl.when(i == 0)
    def _():
        zero_ref[...] = jnp.zeros_like(zero_ref)

        def start(e, c):
            @pl.when(has_rows(e))
            def _():
                zero_copy(e).start()
            return c

        def wait(e, c):
            @pl.when(has_rows(e))
            def _():
                zero_copy(e).wait()
            return c

        def tail_copy(blk):
            return pltpu.make_async_copy(
                zero_ref, xs_hbm.at[pl.ds(pl.multiple_of(blk * BM, BM), BM)], sem_zero)

        def start_tail(blk, c):
            tail_copy(blk).start()
            return c

        def wait_tail(blk, c):
            tail_copy(blk).wait()
            return c

        n_used = gend_ref[N_EXPERTS - 1] // BM
        lax.fori_loop(0, N_EXPERTS, start, 0)
        lax.fori_loop(n_used, N_BLOCKS_ALL, start_tail, 0)
        lax.fori_loop(0, N_EXPERTS, wait, 0)
        lax.fori_loop(n_used, N_BLOCKS_ALL, wait_tail, 0)

    tab_copy.wait()
    h2 = h2_ref[...]
    units_per_chunk = PERM_CHUNK // UNIT
    for c in range(LOCAL_ROWS // PERM_CHUNK):
        slot = c * PERM_CHUNK + lax.broadcasted_iota(I32, (PERM_CHUNK, TM_TOK), 0)
        p = jnp.zeros((PERM_CHUNK, TM_TOK), F32)
        for k in range(TOP_K):
            p = jnp.where(slot == lslot_ref[k:k + 1, :], 1.0, p)
        local_ref[c * PERM_CHUNK:(c + 1) * PERM_CHUNK, :] = _pack_pairs(_dot(p.astype(BF16), h2))
        for u in range(c * units_per_chunk, (c + 1) * units_per_chunk):
            _unit_copy(local_ref, u, xs_hbm, tab_smem[0, u], sem_rows).start()
    pltpu.make_async_copy(local_ref, xs_hbm.at[pl.ds(0, LOCAL_ROWS)], sem_rows).wait()


def _dispatch(h2, lslot, unit_tab, gend):
    return pl.pallas_call(
        _dispatch_kernel,
        out_shape=jax.ShapeDtypeStruct((N_SLOTS, PACK_W), I32),
        grid_spec=pltpu.PrefetchScalarGridSpec(
            num_scalar_prefetch=1,
            grid=(N_TOK_TILES,),
            in_specs=[
                pl.BlockSpec((TM_TOK, D_MODEL), lambda i, ge: (i, 0)),
                pl.BlockSpec((SUBLANES, TM_TOK), lambda i, ge: (0, i)),
                pl.BlockSpec(memory_space=pl.ANY),
            ],
            out_specs=pl.BlockSpec(memory_space=pl.ANY),
            scratch_shapes=[
                pltpu.SMEM((1, MAX_UNITS), I32),
                pltpu.VMEM((BM, PACK_W), I32),
                pltpu.VMEM((LOCAL_ROWS, PACK_W), I32),
                pltpu.SemaphoreType.DMA,
                pltpu.SemaphoreType.DMA,
                pltpu.SemaphoreType.DMA,
            ],
        ),
        compiler_params=pltpu.CompilerParams(
            dimension_semantics=("arbitrary",), vmem_limit_bytes=VMEM_LIMIT),
        name="moe_dispatch",
    )(gend, h2, lslot, unit_tab)


def _expert_kernel(blk_e_ref, n_used_ref, xs_ref, wgu_ref, wdown_ref, ys_ref, wgu_bf, wdown_bf):
    i = pl.program_id(0)
    live = i < n_used_ref[0]
    new_expert = jnp.logical_or(i == 0, blk_e_ref[i] != blk_e_ref[jnp.maximum(i - 1, 0)])

    @pl.when(jnp.logical_and(live, new_expert))
    def _():
        wgu_bf[...] = wgu_ref[0].astype(BF16)
        wdown_bf[...] = wdown_ref[0].astype(BF16)

    @pl.when(live)
    def _():
        x_lo, x_hi = _unpack_pairs(xs_ref[...])
        gu = _dot(x_lo, wgu_bf[0:PACK_W, :]) + _dot(x_hi, wgu_bf[PACK_W:D_MODEL, :])
        act = _silu(gu[:, 0:D_EXPERT]) * gu[:, D_EXPERT:2 * D_EXPERT]
        y = _dot(act.astype(BF16), wdown_bf[...])
        ys_ref[...] = _pack_pairs(y.astype(BF16).astype(F32))

    @pl.when(i >= n_used_ref[0])
    def _():
        ys_ref[...] = jnp.zeros_like(ys_ref)


def _experts(xs, w_gu, w_down, blk_e, n_used):
    def blk(i, be, nu):
        return jnp.minimum(i, nu[0] - 1)

    return pl.pallas_call(
        _expert_kernel,
        out_shape=jax.ShapeDtypeStruct((N_SLOTS, PACK_W), I32),
        grid_spec=pltpu.PrefetchScalarGridSpec(
            num_scalar_prefetch=2,
            grid=(N_BLOCKS_ALL,),
            in_specs=[
                pl.BlockSpec((BM, PACK_W), lambda i, be, nu: (blk(i, be, nu), 0)),
                pl.BlockSpec((1, D_MODEL, 2 * D_EXPERT), lambda i, be, nu: (be[blk(i, be, nu)], 0, 0)),
                pl.BlockSpec((1, D_EXPERT, D_MODEL), lambda i, be, nu: (be[blk(i, be, nu)], 0, 0)),
            ],
            out_specs=pl.BlockSpec((BM, PACK_W), lambda i, be, nu: (i, 0)),
            scratch_shapes=[
                pltpu.VMEM((D_MODEL, 2 * D_EXPERT), BF16),
                pltpu.VMEM((D_EXPERT, D_MODEL), BF16),
            ],
        ),
        compiler_params=pltpu.CompilerParams(
            dimension_semantics=("arbitrary",), vmem_limit_bytes=VMEM_LIMIT),
        name="moe_experts",
    )(blk_e, n_used, xs, w_gu, w_down)


def _combine_kernel(x1_ref, h2_ref, lslot_ref, wsel_ref, unit_hbm, ys_hbm, wsgu_ref, wsdown_ref, mod_ref,
                    g_ref, b_ref, outc_ref, outl_ref, tab_smem, local_ref, sel_ref, ylo_ref, yhi_ref,
                    sem_tab, sem_rows):
    i = pl.program_id(0)
    tm = TM_TOK
    tab_copy = pltpu.make_async_copy(unit_hbm.at[i], tab_smem, sem_tab)
    tab_copy.start()
    tab_copy.wait()
    for u in range(MAX_UNITS):
        _unit_copy(ys_hbm, tab_smem[0, u], local_ref, u, sem_rows).start()

    sgu = _dot(h2_ref[...], wsgu_ref[...])
    act = _silu(sgu[:, 0:D_SHARED]) * sgu[:, D_SHARED:2 * D_SHARED]
    f = _dot(act.astype(BF16), wsdown_ref[...])

    w = wsel_ref[...]
    ls = lslot_ref[...]
    for c in range(LOCAL_ROWS // PERM_CHUNK):
        slot = c * PERM_CHUNK + lax.broadcasted_iota(I32, (tm, PERM_CHUNK), 1)
        sel = jnp.zeros((tm, PERM_CHUNK), F32)
        for k in range(TOP_K):
            sel = jnp.where(slot == ls[:, k:k + 1], w[:, k:k + 1], sel)
        sel_ref[:, c * PERM_CHUNK:(c + 1) * PERM_CHUNK] = sel.astype(BF16)

    pltpu.make_async_copy(ys_hbm.at[pl.ds(0, LOCAL_ROWS)], local_ref, sem_rows).wait()
    for c in range(LOCAL_ROWS // PERM_CHUNK):
        rows = slice(c * PERM_CHUNK, (c + 1) * PERM_CHUNK)
        ylo_ref[rows, :], yhi_ref[rows, :] = _unpack_pairs(local_ref[rows, :])
    sel = sel_ref[...]
    f = f + jnp.concatenate([_dot(sel, ylo_ref[...]), _dot(sel, yhi_ref[...])], axis=-1)
    ci = _cond_row(i, tm)
    gate2 = mod_ref[pl.ds(ci, 1), 5 * D_MODEL:6 * D_MODEL]
    out = _layer_norm(ALPHA * x1_ref[...] + gate2 * f, g_ref[...], b_ref[...])

    @pl.when(i < T_CTX // tm)
    def _():
        outc_ref[...] = out

    @pl.when(i >= T_CTX // tm)
    def _():
        outl_ref[...] = out


def _combine(x1, h2, lslot_rows, wsel_rows, unit_tab, ys, w_sgu_bf16, w_sdown_bf16, mod, ln_g, ln_b):
    n_ctx = T_CTX // TM_TOK
    row_map = lambda i: (i, 0)
    const = lambda i: (0, 0)
    return pl.pallas_call(
        _combine_kernel,
        out_shape=(jax.ShapeDtypeStruct((T_CTX, D_MODEL), F32),
                   jax.ShapeDtypeStruct((T_LAT, D_MODEL), F32)),
        grid=(N_TOK_TILES,),
        in_specs=[
            pl.BlockSpec((TM_TOK, D_MODEL), row_map),
            pl.BlockSpec((TM_TOK, D_MODEL), row_map),
            pl.BlockSpec((TM_TOK, SUBLANES), row_map),
            pl.BlockSpec((TM_TOK, SUBLANES), row_map),
            pl.BlockSpec(memory_space=pl.ANY),
            pl.BlockSpec(memory_space=pl.ANY),
            pl.BlockSpec((D_MODEL, 2 * D_SHARED), const),
            pl.BlockSpec((D_SHARED, D_MODEL), const),
            pl.BlockSpec((N_COND, 6 * D_MODEL), const),
            pl.BlockSpec((1, D_MODEL), const),
            pl.BlockSpec((1, D_MODEL), const),
        ],
        out_specs=(pl.BlockSpec((TM_TOK, D_MODEL), lambda i: (jnp.minimum(i, n_ctx - 1), 0)),
                   pl.BlockSpec((TM_TOK, D_MODEL), lambda i: (jnp.maximum(i - n_ctx, 0), 0))),
        scratch_shapes=[
            pltpu.SMEM((1, MAX_UNITS), I32),
            pltpu.VMEM((LOCAL_ROWS, PACK_W), I32),
            pltpu.VMEM((TM_TOK, LOCAL_ROWS), BF16),
            pltpu.VMEM((LOCAL_ROWS, PACK_W), BF16),
            pltpu.VMEM((LOCAL_ROWS, PACK_W), BF16),
            pltpu.SemaphoreType.DMA,
            pltpu.SemaphoreType.DMA,
        ],
        compiler_params=pltpu.CompilerParams(
            dimension_semantics=("arbitrary",), vmem_limit_bytes=VMEM_LIMIT),
        name="moe_combine",
    )(x1, h2, lslot_rows, wsel_rows, unit_tab, ys, w_sgu_bf16, w_sdown_bf16, mod, ln_g, ln_b)


def _lane_rows(v, width):
    return jnp.broadcast_to(v.astype(F32)[:, None, None], (v.shape[0], 1, width))


def kernel(x_prompt, x_sample, state_ret_fwd, state_ret_bwd, cache_win_k, cache_win_v, cache_na_k, cache_na_v, c, c_ctx, w_in, w_out, ret_decay_fwd, ret_decay_bwd, ret_gn_g, ret_gn_b, win_sink, na_rpb, w_mod, b_mod, ln1_g, ln1_b, ln2_g, ln2_b, w_router, b_router, w_expert_gu, w_expert_down, w_shared_gu, w_shared_down):
    cond = jnp.concatenate(
        [c_ctx[None, :], c, jnp.zeros((N_COND - 1 - DEC_BATCH, D_MODEL), F32)], axis=0)
    mod_all = _modulation(cond, w_mod, b_mod)
    cos_t, sin_t = _rope_tables()

    x_ctx = x_prompt.reshape(T_CTX, D_MODEL)
    x_lat = x_sample.reshape(T_LAT, D_MODEL)
    sf_l, sb_l, cache_l = [], [], []
    for l in range(DEPTH):
        mod = mod_all[l]
        pa, pb, pc, cache = _in_projection(x_ctx, x_lat, mod, w_in[l].astype(BF16))
        cache_l.append(cache)
        decf_s, decb_s = _lane_rows(ret_decay_fwd[l], SEQ), _lane_rows(ret_decay_bwd[l], SEQ)
        gng, gnb = ret_gn_g[l][None, :], ret_gn_b[l][None, :]
        mix_c, st_f, st_b = _ctx_mixers(pa, pb, pc, win_sink[l], decf_s, decb_s, gng, gnb)
        sf_l.append(st_f)
        sb_l.append(st_b)
        ya = _lat_retention(pa, state_ret_fwd, state_ret_bwd, l,
                            _lane_rows(ret_decay_fwd[l], RET_CHUNK), _lane_rows(ret_decay_bwd[l], RET_CHUNK),
                            gng, gnb)
        yb = _lat_window_attn(pb, cache_win_k, cache_win_v, l, win_sink[l], cos_t, sin_t)
        yc = _lat_na_attn(pc, cache_na_k, cache_na_v, l, _na_maskbias(na_rpb[l]))

        wr_t = w_router[l].T
        wr_hi = wr_t.astype(BF16)
        wr_lo = (wr_t - wr_hi.astype(F32)).astype(BF16)
        x1, h2, eidx, wsel, rank, counts = _post_mixer(
            x_ctx, x_lat, mix_c, ya, yb, yc, w_out[l].astype(BF16), mod, ln1_g[l][None, :],
            ln1_b[l][None, :], wr_hi, wr_lo, b_router[l][:, None])
        lslot, unit_tab, gend, blk_e, n_used = _slot_plan(eidx, rank, counts)
        xs = _dispatch(h2, lslot, unit_tab, gend)
        ys = _experts(xs, w_expert_gu[l], w_expert_down[l], blk_e, n_used)
        x_ctx, x_lat = _combine(x1, h2, lslot.T, wsel.T, unit_tab, ys, w_shared_gu[l].astype(BF16),
                                w_shared_down[l].astype(BF16), mod, ln2_g[l][None, :], ln2_b[l][None, :])

    y_prompt = x_ctx.reshape(BATCH, SEQ, D_MODEL)
    y_sample = x_lat.reshape(DEC_BATCH, DEC_SEQ, D_MODEL)
    new_sf = jnp.stack(sf_l, axis=1)
    new_sb = jnp.stack(sb_l, axis=1)

    def heads_of(cols, n_heads):
        per_layer = [cache[:, cols[0]:cols[1]].reshape(BATCH, SEQ, n_heads, HEAD_DIM).transpose(0, 2, 1, 3)
                     for cache in cache_l]
        return jnp.stack(per_layer, axis=1)

    new_win_k = heads_of((0, KV_W_B), KV_B)
    new_win_v = heads_of((KV_W_B, 2 * KV_W_B), KV_B)
    new_na_k = heads_of((2 * KV_W_B, 2 * KV_W_B + W_C), H_C)
    new_na_v = heads_of((2 * KV_W_B + W_C, CACHE_W), H_C)
    return (y_prompt, y_sample, new_sf, new_sb, new_win_k, new_win_v, new_na_k, new_na_v)
```

```python
import functools

import numpy as np
import jax
import jax.numpy as jnp
from jax import lax
from jax.experimental import pallas as pl
from jax.experimental.pallas import tpu as pltpu

F32 = jnp.float32
BF16 = jnp.bfloat16
I32 = jnp.int32

D_MODEL = 1024
BATCH = 32
SEQ = 256
DEPTH = 2
DEC_BATCH = 4
DEC_SEQ = 2048
PAST_LEN = 256
GRID_W = 64
HEAD_DIM = 64
ATTN_SCALE = HEAD_DIM ** -0.5
H_A = 4
W_A = H_A * HEAD_DIM
GN_EPS = 1e-5
H_B = 6
KV_B = 2
W_B = H_B * HEAD_DIM
KV_W_B = KV_B * HEAD_DIM
WINDOW = 128
ROPE_BASE = 10000.0
H_C = 6
W_C = H_C * HEAD_DIM
NA_ROWS = 8
NA_COLS = 16
IN_WIDTH = 4 * W_A + W_B + 2 * KV_W_B + 3 * W_C
N_EXPERTS = 64
TOP_K = 6
N_GROUPS = 8
GROUP_SIZE = N_EXPERTS // N_GROUPS
TOPK_GROUPS = 4
D_EXPERT = 256
D_SHARED = 256
ROUTE_SCALE = 2.5
ALPHA = (2 * DEPTH) ** 0.25
LN_EPS = 1e-5
NEG_INF = -1e30
PICKED = -3e38

T_CTX = BATCH * SEQ
T_LAT = DEC_BATCH * DEC_SEQ
T_ALL = T_CTX + T_LAT
N_COND = 8

PA_W = 4 * W_A
PB_W = W_B + 2 * KV_W_B
PC_W = 3 * W_C

LANES = 128
SUBLANES = 8
VMEM_LIMIT = 56 * 1024 * 1024

TM_PROJ = 512
TM_TOK = 256
RET_CHUNK = 256
WIN_QB = 128
NA_TILE_ROWS = 4
NA_KEY_ROWS = 11
BM = 512
UNIT = SUBLANES
UNITS_PER_BLOCK = BM // UNIT
PERM_CHUNK = 256
N_TOK_TILES = T_ALL // TM_TOK
LOCAL_ROWS = -(-(TM_TOK * TOP_K + N_EXPERTS * (UNIT - 1)) // PERM_CHUNK) * PERM_CHUNK
MAX_UNITS = LOCAL_ROWS // UNIT
N_ASSIGN = T_ALL * TOP_K
N_BLOCKS = -(-(N_ASSIGN + N_TOK_TILES * N_EXPERTS * (UNIT - 1) + N_EXPERTS * (BM - 1)) // BM)
SPARE_BLOCKS = 2 * -(-LOCAL_ROWS // BM)
SPARE_UNIT0 = N_BLOCKS * UNITS_PER_BLOCK
N_BLOCKS_ALL = N_BLOCKS + SPARE_BLOCKS
N_SLOTS = N_BLOCKS_ALL * BM
BLK_LANES = -(-N_BLOCKS_ALL // LANES) * LANES


def _dot(a, b):
    return jnp.dot(a, b, preferred_element_type=F32)


def _dot_nt(a, b):
    return lax.dot_general(a, b, (((1,), (1,)), ((), ())), preferred_element_type=F32)


def _silu(x):
    return x * jax.nn.sigmoid(x)


def _log_sigmoid(x):
    return jnp.minimum(x, 0.0) - jnp.log(1.0 + jnp.exp(-jnp.abs(x)))


def _cond_row(tile, tile_rows):
    n_ctx = T_CTX // tile_rows
    per_lat = DEC_SEQ // tile_rows
    return jnp.where(tile < n_ctx, 0, 1 + (tile - n_ctx) // per_lat)


def _layer_norm(x, g, b):
    mu = jnp.mean(x, -1, keepdims=True)
    xc = x - mu
    var = jnp.mean(xc * xc, -1, keepdims=True)
    return xc * lax.rsqrt(var + LN_EPS) * g + b


MOD_TN = 1536


def _mod_kernel(cond_ref, w_ref, b_ref, o_ref):
    s = _silu(cond_ref[...])
    s_hi = s.astype(BF16)
    s_lo = (s - s_hi.astype(F32)).astype(BF16)
    w = w_ref[0]
    w_hi = w.astype(BF16)
    w_lo = (w - w_hi.astype(F32)).astype(BF16)
    o_ref[0] = _dot(s_hi, w_hi) + _dot(s_lo, w_hi) + _dot(s_hi, w_lo) + b_ref[0]


def _modulation(cond, w_mod, b_mod):
    n_out = 6 * D_MODEL
    return pl.pallas_call(
        _mod_kernel,
        out_shape=jax.ShapeDtypeStruct((DEPTH, N_COND, n_out), F32),
        grid=(DEPTH, n_out // MOD_TN),
        in_specs=[
            pl.BlockSpec((N_COND, D_MODEL), lambda l, j: (0, 0)),
            pl.BlockSpec((1, D_MODEL, MOD_TN), lambda l, j: (l, 0, j)),
            pl.BlockSpec((1, 1, MOD_TN), lambda l, j: (l, 0, j)),
        ],
        out_specs=pl.BlockSpec((1, N_COND, MOD_TN), lambda l, j: (l, 0, j)),
        compiler_params=pltpu.CompilerParams(
            dimension_semantics=("arbitrary", "arbitrary"), vmem_limit_bytes=VMEM_LIMIT),
        name="modulation",
    )(cond, w_mod, b_mod.reshape(DEPTH, 1, n_out))


SEQ_PER_PROJ = TM_PROJ // SEQ


def _inproj_kernel(xc_ref, xl_ref, mod_ref, w_ref, pa_ref, pb_ref, pc_ref, wk_ref, wv_ref, nk_ref, nv_ref):
    i = pl.program_id(0)
    ci = _cond_row(i, TM_PROJ)
    sh = mod_ref[pl.ds(ci, 1), 0:D_MODEL]
    sc = mod_ref[pl.ds(ci, 1), D_MODEL:2 * D_MODEL]
    x = jnp.where(i < T_CTX // TM_PROJ, xc_ref[...], xl_ref[...])
    h = x * (1.0 + sc) + sh
    p = _dot(h.astype(BF16), w_ref[...])
    pa_ref[...] = p[:, 0:PA_W].astype(BF16)
    pb_ref[...] = p[:, PA_W:PA_W + PB_W].astype(BF16)
    pc_ref[...] = p[:, PA_W + PB_W:IN_WIDTH].astype(BF16)

    @pl.when(i < T_CTX // TM_PROJ)
    def _():
        targets = ((wk_ref, PA_W + W_B, KV_B), (wv_ref, PA_W + W_B + KV_W_B, KV_B),
                   (nk_ref, PA_W + PB_W + W_C, H_C), (nv_ref, PA_W + PB_W + 2 * W_C, H_C))
        for ref, col0, n_heads in targets:
            for s in range(SEQ_PER_PROJ):
                for hh in range(n_heads):
                    ref[s, hh] = p[s * SEQ:(s + 1) * SEQ, col0 + hh * HEAD_DIM:col0 + (hh + 1) * HEAD_DIM]


def _in_projection(x_ctx, x_lat, mod, w_in_bf16):
    n_ctx_tiles = T_CTX // TM_PROJ

    def cache_spec(n_heads):
        return pl.BlockSpec((SEQ_PER_PROJ, n_heads, SEQ, HEAD_DIM),
                            lambda i: (jnp.minimum(i, n_ctx_tiles - 1), 0, 0, 0))

    return pl.pallas_call(
        _inproj_kernel,
        out_shape=(
            jax.ShapeDtypeStruct((T_ALL, PA_W), BF16),
            jax.ShapeDtypeStruct((T_ALL, PB_W), BF16),
            jax.ShapeDtypeStruct((T_ALL, PC_W), BF16),
            jax.ShapeDtypeStruct((BATCH, KV_B, SEQ, HEAD_DIM), F32),
            jax.ShapeDtypeStruct((BATCH, KV_B, SEQ, HEAD_DIM), F32),
            jax.ShapeDtypeStruct((BATCH, H_C, SEQ, HEAD_DIM), F32),
            jax.ShapeDtypeStruct((BATCH, H_C, SEQ, HEAD_DIM), F32),
        ),
        grid=(T_ALL // TM_PROJ,),
        in_specs=[
            pl.BlockSpec((TM_PROJ, D_MODEL), lambda i: (jnp.minimum(i, n_ctx_tiles - 1), 0)),
            pl.BlockSpec((TM_PROJ, D_MODEL), lambda i: (jnp.maximum(i - n_ctx_tiles, 0), 0)),
            pl.BlockSpec((N_COND, 6 * D_MODEL), lambda i: (0, 0)),
            pl.BlockSpec((D_MODEL, IN_WIDTH), lambda i: (0, 0)),
        ],
        out_specs=(
            pl.BlockSpec((TM_PROJ, PA_W), lambda i: (i, 0)),
            pl.BlockSpec((TM_PROJ, PB_W), lambda i: (i, 0)),
            pl.BlockSpec((TM_PROJ, PC_W), lambda i: (i, 0)),
            cache_spec(KV_B), cache_spec(KV_B), cache_spec(H_C), cache_spec(H_C),
        ),
        compiler_params=pltpu.CompilerParams(
            dimension_semantics=("arbitrary",), vmem_limit_bytes=VMEM_LIMIT),
        name="in_projection",
    )(x_ctx, x_lat, mod, w_in_bf16)


def _decay_matrix(lg_f, lg_b, n):
    row = lax.broadcasted_iota(I32, (n, n), 0)
    col = lax.broadcasted_iota(I32, (n, n), 1)
    diff = (row - col).astype(F32)
    fwd = jnp.where(diff >= 0, jnp.exp(lg_f * jnp.maximum(diff, 0.0)), 0.0)
    bwd = jnp.where(diff <= 0, jnp.exp(lg_b * jnp.maximum(-diff, 0.0)), 0.0)
    return (fwd + bwd) * ATTN_SCALE


def _retention_readout(o, gate, g, b):
    mu = jnp.mean(o, -1, keepdims=True)
    oc = o - mu
    var = jnp.mean(oc * oc, -1, keepdims=True)
    on = oc * lax.rsqrt(var + GN_EPS) * g + b
    return on * _silu(gate.astype(F32))


def _softmax_attend(s, v, extra_logit=None):
    m = jnp.max(s, -1, keepdims=True)
    if extra_logit is not None:
        m = jnp.maximum(m, extra_logit)
    p = jnp.exp(s - m)
    den = jnp.sum(p, -1, keepdims=True)
    if extra_logit is not None:
        den = den + jnp.exp(extra_logit - m)
    return _dot(p.astype(BF16), v) / den


def _ctx_mixer_kernel(sink_ref, pa_ref, pb_ref, pc_ref, decf_ref, decb_ref, gng_ref, gnb_ref,
                      mix_ref, sf_ref, sb_ref):
    n = SEQ
    hd = HEAD_DIM
    pos = lax.broadcasted_iota(I32, (n, hd), 0).astype(F32)
    for h in range(H_A):
        c0 = h * hd
        q = pa_ref[:, c0:c0 + hd]
        k = pa_ref[:, W_A + c0:W_A + c0 + hd]
        v = pa_ref[:, 2 * W_A + c0:2 * W_A + c0 + hd]
        gate = pa_ref[:, 3 * W_A + c0:3 * W_A + c0 + hd]
        lg_f = _log_sigmoid(decf_ref[h])
        lg_b = _log_sigmoid(decb_ref[h])
        dmat = _decay_matrix(lg_f, lg_b, n)
        o = _dot((_dot_nt(q, k) * dmat).astype(BF16), v)
        kf = k.astype(F32)
        zf = jnp.exp(lg_f[:, 0:hd] * (n - 1.0 - pos)) * ATTN_SCALE
        zb = jnp.exp(lg_b[:, 0:hd] * pos) * ATTN_SCALE
        sf_ref[0, h] = _dot((kf * zf).T.astype(BF16), v)
        sb_ref[0, h] = _dot((kf * zb).T.astype(BF16), v)
        y = _retention_readout(o, gate, gng_ref[:, c0:c0 + hd], gnb_ref[:, c0:c0 + hd])
        mix_ref[:, c0:c0 + hd] = y.astype(BF16)
    for j in range(KV_B):
        k = pb_ref[:, W_B + j * hd:W_B + (j + 1) * hd]
        v = pb_ref[:, W_B + KV_W_B + j * hd:W_B + KV_W_B + (j + 1) * hd]
        for g in range(H_B // KV_B):
            hh = j * (H_B // KV_B) + g
            q = pb_ref[:, hh * hd:(hh + 1) * hd]
            o = _softmax_attend(_dot_nt(q, k) * ATTN_SCALE, v, sink_ref[hh])
            mix_ref[:, W_A + hh * hd:W_A + (hh + 1) * hd] = o.astype(BF16)
    for hh in range(H_C):
        q = pc_ref[:, hh * hd:(hh + 1) * hd]
        k = pc_ref[:, W_C + hh * hd:W_C + (hh + 1) * hd]
        v = pc_ref[:, 2 * W_C + hh * hd:2 * W_C + (hh + 1) * hd]
        o = _softmax_attend(_dot_nt(q, k) * ATTN_SCALE, v)
        mix_ref[:, W_A + W_B + hh * hd:W_A + W_B + (hh + 1) * hd] = o.astype(BF16)


def _ctx_mixers(pa, pb, pc, sink, decf, decb, gng, gnb):
    return pl.pallas_call(
        _ctx_mixer_kernel,
        out_shape=(
            jax.ShapeDtypeStruct((T_CTX, D_MODEL), BF16),
            jax.ShapeDtypeStruct((BATCH, H_A, HEAD_DIM, HEAD_DIM), F32),
            jax.ShapeDtypeStruct((BATCH, H_A, HEAD_DIM, HEAD_DIM), F32),
        ),
        grid=(BATCH,),
        in_specs=[
            pl.BlockSpec(memory_space=pltpu.SMEM),
            pl.BlockSpec((SEQ, PA_W), lambda b: (b, 0)),
            pl.BlockSpec((SEQ, PB_W), lambda b: (b, 0)),
            pl.BlockSpec((SEQ, PC_W), lambda b: (b, 0)),
            pl.BlockSpec((H_A, 1, SEQ), lambda b: (0, 0, 0)),
            pl.BlockSpec((H_A, 1, SEQ), lambda b: (0, 0, 0)),
            pl.BlockSpec((1, W_A), lambda b: (0, 0)),
            pl.BlockSpec((1, W_A), lambda b: (0, 0)),
        ],
        out_specs=(
            pl.BlockSpec((SEQ, D_MODEL), lambda b: (b, 0)),
            pl.BlockSpec((1, H_A, HEAD_DIM, HEAD_DIM), lambda b: (b, 0, 0, 0)),
            pl.BlockSpec((1, H_A, HEAD_DIM, HEAD_DIM), lambda b: (b, 0, 0, 0)),
        ),
        compiler_params=pltpu.CompilerParams(
            dimension_semantics=("arbitrary",), vmem_limit_bytes=VMEM_LIMIT),
        name="ctx_mixers",
    )(sink, pa, pb, pc, decf, decb, gng, gnb)


def _lat_ret_kernel(pa_ref, stf_ref, stb_ref, decf_ref, decb_ref, gng_ref, gnb_ref, ya_ref, acc_ref):
    c = RET_CHUNK
    hd = HEAD_DIM
    n_chunks = DEC_SEQ // c
    pos = lax.broadcasted_iota(I32, (c, hd), 0).astype(F32)
    for h in range(H_A):
        c0 = h * hd
        lg_f = _log_sigmoid(decf_ref[h])
        lg_b = _log_sigmoid(decb_ref[h])
        dmat = _decay_matrix(lg_f, lg_b, c)
        lf = lg_f[:, 0:hd]
        lb = lg_b[:, 0:hd]
        zf = jnp.exp(lf * (c - 1.0 - pos)) * ATTN_SCALE
        zb = jnp.exp(lb * pos) * ATTN_SCALE
        xf = jnp.exp(lf * (pos + 1.0))
        xb = jnp.exp(lb * (c - pos))
        gcf = jnp.exp(lf * float(c))
        gcb = jnp.exp(lb * float(c))
        g = gng_ref[:, c0:c0 + hd]
        b = gnb_ref[:, c0:c0 + hd]

        def load(i, off):
            rows = pl.ds(pl.multiple_of(i * c, c), c)
            return pa_ref[rows, off + c0:off + c0 + hd]

        def fwd(i, s):
            q, k, v = load(i, 0), load(i, W_A), load(i, 2 * W_A)
            o = _dot((_dot_nt(q, k) * dmat).astype(BF16), v)
            o = o + _dot((q.astype(F32) * xf).astype(BF16), s.astype(BF16))
            acc_ref[pl.ds(pl.multiple_of(i * c, c), c), c0:c0 + hd] = o
            return gcf * s + _dot((k.astype(F32) * zf).T.astype(BF16), v)

        lax.fori_loop(0, n_chunks, fwd, stf_ref[0, 0, h])

        def bwd(j, s):
            i = n_chunks - 1 - j
            rows = pl.ds(pl.multiple_of(i * c, c), c)
            q, k, v = load(i, 0), load(i, W_A), load(i, 2 * W_A)
            o = acc_ref[rows, c0:c0 + hd] + _dot((q.astype(F32) * xb).astype(BF16), s.astype(BF16))
            y = _retention_readout(o, load(i, 3 * W_A), g, b)
            ya_ref[rows, c0:c0 + hd] = y.astype(BF16)
            return gcb * s + _dot((k.astype(F32) * zb).T.astype(BF16), v)

        lax.fori_loop(0, n_chunks, bwd, stb_ref[0, 0, h])


def _lat_retention(pa, st_f, st_b, layer, decf, decb, gng, gnb):
    lat0 = T_CTX // DEC_SEQ
    st_spec = pl.BlockSpec((1, 1, H_A, HEAD_DIM, HEAD_DIM), lambda b: (b, layer, 0, 0, 0))
    return pl.pallas_call(
        _lat_ret_kernel,
        out_shape=jax.ShapeDtypeStruct((T_LAT, W_A), BF16),
        grid=(DEC_BATCH,),
        in_specs=[
            pl.BlockSpec((DEC_SEQ, PA_W), lambda b: (lat0 + b, 0)),
            st_spec, st_spec,
            pl.BlockSpec((H_A, 1, RET_CHUNK), lambda b: (0, 0, 0)),
            pl.BlockSpec((H_A, 1, RET_CHUNK), lambda b: (0, 0, 0)),
            pl.BlockSpec((1, W_A), lambda b: (0, 0)),
            pl.BlockSpec((1, W_A), lambda b: (0, 0)),
        ],
        out_specs=pl.BlockSpec((DEC_SEQ, W_A), lambda b: (b, 0)),
        scratch_shapes=[pltpu.VMEM((DEC_SEQ, W_A), F32)],
        compiler_params=pltpu.CompilerParams(
            dimension_semantics=("arbitrary",), vmem_limit_bytes=VMEM_LIMIT),
        name="lat_retention",
    )(pa, st_f, st_b, decf, decb, gng, gnb)


def _swap_halves_matrix(width):
    r = lax.broadcasted_iota(I32, (width, width), 0)
    c = lax.broadcasted_iota(I32, (width, width), 1)
    return jnp.where((r ^ (HEAD_DIM // 2)) == c, 1.0, 0.0).astype(BF16)


def _rope(x, cos, sin_signed, swap):
    return x.astype(F32) * cos + _dot(x, swap) * sin_signed


def _lat_win_kernel(sink_ref, pq_ref, pseq_ref, kctx_ref, vctx_ref, cos_ref, sin_ref, yb_ref, krope_ref):
    n = pl.program_id(1)
    hd = HEAD_DIM
    qb = WIN_QB
    n_blk = DEC_SEQ // qb
    group = H_B // KV_B
    swap = _swap_halves_matrix(LANES)

    @pl.when(n == 0)
    def _():
        k = pseq_ref[:, W_B:W_B + KV_W_B]
        krope_ref[...] = _rope(k, cos_ref[...], sin_ref[...], swap).astype(BF16)

    q_rows = pl.ds(pl.multiple_of(n * qb, qb), qb)
    cos_q = cos_ref[q_rows, :]
    sin_q = sin_ref[q_rows, :]
    qr = [_rope(pq_ref[:, p * LANES:(p + 1) * LANES], cos_q, sin_q, swap).astype(BF16)
          for p in range(W_B // LANES)]

    ws = jnp.clip(n - 1, 0, n_blk - 3) * qb
    k_rows = pl.ds(pl.multiple_of(ws, qb), 3 * qb)
    q_pos = n * qb + lax.broadcasted_iota(I32, (group * qb, 3 * qb), 0) % qb
    k_pos = ws + lax.broadcasted_iota(I32, (group * qb, 3 * qb), 1)
    valid = jnp.abs(k_pos - q_pos) <= WINDOW
    head_of_row = lax.broadcasted_iota(I32, (group * qb, 1), 0) // qb
    for j in range(KV_B):
        heads = [j * group + g for g in range(group)]
        qs = jnp.concatenate(
            [qr[hh // 2][:, (hh % 2) * hd:(hh % 2 + 1) * hd] for hh in heads], axis=0)
        kw = krope_ref[k_rows, j * hd:(j + 1) * hd]
        vw = pseq_ref[k_rows, W_B + KV_W_B + j * hd:W_B + KV_W_B + (j + 1) * hd]
        kc = kctx_ref[0, 0, j].astype(BF16)
        vc = vctx_ref[0, 0, j].astype(BF16)
        s_loc = jnp.where(valid, _dot_nt(qs, kw) * ATTN_SCALE, NEG_INF)
        s_ctx = _dot_nt(qs, kc) * ATTN_SCALE
        sink = jnp.zeros((group * qb, 1), F32)
        for g, hh in enumerate(heads):
            sink = jnp.where(head_of_row == g, sink_ref[hh], sink)
        m = jnp.maximum(jnp.maximum(jnp.max(s_loc, -1, keepdims=True),
                                    jnp.max(s_ctx, -1, keepdims=True)), sink)
        p_loc = jnp.exp(s_loc - m)
        p_ctx = jnp.exp(s_ctx - m)
        den = (jnp.sum(p_loc, -1, keepdims=True) + jnp.sum(p_ctx, -1, keepdims=True)
               + jnp.exp(sink - m))
        o = (_dot(p_loc.astype(BF16), vw) + _dot(p_ctx.astype(BF16), vc)) / den
        for g, hh in enumerate(heads):
            yb_ref[:, hh * hd:(hh + 1) * hd] = o[g * qb:(g + 1) * qb].astype(BF16)


def _lat_window_attn(pb, cache_k, cache_v, layer, sink, cos_t, sin_t):
    n_blk = DEC_SEQ // WIN_QB
    lat_blk0 = T_CTX // WIN_QB
    lat_seq0 = T_CTX // DEC_SEQ
    ctx_spec = pl.BlockSpec((1, 1, KV_B, PAST_LEN, HEAD_DIM), lambda b, n: (b, layer, 0, 0, 0))
    return pl.pallas_call(
        _lat_win_kernel,
        out_shape=jax.ShapeDtypeStruct((T_LAT, W_B), BF16),
        grid=(DEC_BATCH, n_blk),
        in_specs=[
            pl.BlockSpec(memory_space=pltpu.SMEM),
            pl.BlockSpec((WIN_QB, PB_W), lambda b, n: (lat_blk0 + b * n_blk + n, 0)),
            pl.BlockSpec((DEC_SEQ, PB_W), lambda b, n: (lat_seq0 + b, 0)),
            ctx_spec, ctx_spec,
            pl.BlockSpec((DEC_SEQ, LANES), lambda b, n: (0, 0)),
            pl.BlockSpec((DEC_SEQ, LANES), lambda b, n: (0, 0)),
        ],
        out_specs=pl.BlockSpec((WIN_QB, W_B), lambda b, n: (b * n_blk + n, 0)),
        scratch_shapes=[pltpu.VMEM((DEC_SEQ, KV_W_B), BF16)],
        compiler_params=pltpu.CompilerParams(
            dimension_semantics=("arbitrary", "arbitrary"), vmem_limit_bytes=VMEM_LIMIT),
        name="lat_window_attn",
    )(sink, pb, pb, cache_k, cache_v, cos_t, sin_t)


NA_Q = NA_TILE_ROWS * GRID_W
NA_K = NA_KEY_ROWS * GRID_W
NA_TILES = DEC_SEQ // NA_Q
LAT_ROWS = DEC_SEQ // GRID_W


def _na_window_start(tile):
    return jnp.clip(tile * NA_TILE_ROWS - NA_ROWS // 2, 0, LAT_ROWS - NA_KEY_ROWS)


def _lat_na_kernel(pq_ref, pseq_ref, kctx_ref, vctx_ref, bias_ref, yc_ref):
    t = pl.program_id(1)
    hd = HEAD_DIM
    k_rows = pl.ds(pl.multiple_of(_na_window_start(t) * GRID_W, GRID_W), NA_K)
    for hh in range(H_C):
        q = pq_ref[:, hh * hd:(hh + 1) * hd]
        kw = pseq_ref[k_rows, W_C + hh * hd:W_C + (hh + 1) * hd]
        vw = pseq_ref[k_rows, 2 * W_C + hh * hd:2 * W_C + (hh + 1) * hd]
        kc = kctx_ref[0, 0, hh].astype(BF16)
        vc = vctx_ref[0, 0, hh].astype(BF16)
        s_loc = _dot_nt(q, kw) * ATTN_SCALE + bias_ref[0, hh]
        s_ctx = _dot_nt(q, kc) * ATTN_SCALE
        m = jnp.maximum(jnp.max(s_loc, -1, keepdims=True), jnp.max(s_ctx, -1, keepdims=True))
        p_loc = jnp.exp(s_loc - m)
        p_ctx = jnp.exp(s_ctx - m)
        den = jnp.sum(p_loc, -1, keepdims=True) + jnp.sum(p_ctx, -1, keepdims=True)
        o = (_dot(p_loc.astype(BF16), vw) + _dot(p_ctx.astype(BF16), vc)) / den
        yc_ref[:, hh * hd:(hh + 1) * hd] = o.astype(BF16)


def _na_tile_type(t):
    return jnp.where(t == 0, 0, jnp.where(t == NA_TILES - 1, 2, 1))


def _lat_na_attn(pc, cache_k, cache_v, layer, maskbias):
    lat_tile0 = T_CTX // NA_Q
    lat_seq0 = T_CTX // DEC_SEQ
    ctx_spec = pl.BlockSpec((1, 1, H_C, PAST_LEN, HEAD_DIM), lambda b, t: (b, layer, 0, 0, 0))
    return pl.pallas_call(
        _lat_na_kernel,
        out_shape=jax.ShapeDtypeStruct((T_LAT, W_C), BF16),
        grid=(DEC_BATCH, NA_TILES),
        in_specs=[
            pl.BlockSpec((NA_Q, PC_W), lambda b, t: (lat_tile0 + b * NA_TILES + t, 0)),
            pl.BlockSpec((DEC_SEQ, PC_W), lambda b, t: (lat_seq0 + b, 0)),
            ctx_spec, ctx_spec,
            pl.BlockSpec((1, H_C, NA_Q, NA_K), lambda b, t: (_na_tile_type(t), 0, 0, 0)),
        ],
        out_specs=pl.BlockSpec((NA_Q, W_C), lambda b, t: (b * NA_TILES + t, 0)),
        compiler_params=pltpu.CompilerParams(
            dimension_semantics=("arbitrary", "arbitrary"), vmem_limit_bytes=VMEM_LIMIT),
        name="lat_na_attn",
    )(pc, pc, cache_k, cache_v, maskbias)


def _na_block_index():
    out = np.zeros((3, NA_TILE_ROWS, NA_KEY_ROWS), np.int32)
    for ty, tile in enumerate((0, 1, NA_TILES - 1)):
        r = tile * NA_TILE_ROWS
        ws = int(np.clip(r - NA_ROWS // 2, 0, LAT_ROWS - NA_KEY_ROWS))
        for qq in range(NA_TILE_ROWS):
            qr = r + qq
            r0 = int(np.clip(qr - NA_ROWS // 2, 0, LAT_ROWS - NA_ROWS))
            for kk in range(NA_KEY_ROWS):
                kr = ws + kk
                out[ty, qq, kk] = kr - qr + NA_ROWS - 1 if r0 <= kr < r0 + NA_ROWS else 2 * NA_ROWS - 1
    return out


def _na_maskbias(rpb):
    qc = np.arange(GRID_W)[:, None]
    kc = np.arange(GRID_W)[None, :]
    c0 = np.clip(qc - NA_COLS // 2, 0, GRID_W - NA_COLS)
    col_ok = (kc >= c0) & (kc < c0 + NA_COLS)
    ci = np.clip(kc - qc + NA_COLS - 1, 0, 2 * NA_COLS - 2)
    onehot = (ci[None] == np.arange(2 * NA_COLS - 1)[:, None, None]).astype(np.float32)
    cols = jnp.einsum("hab,bqk->haqk", rpb, jnp.asarray(onehot), precision=lax.Precision.HIGHEST)
    cols = jnp.where(jnp.asarray(col_ok)[None, None], cols, NEG_INF)
    cols = jnp.concatenate([cols, jnp.full((H_C, 1, GRID_W, GRID_W), NEG_INF, F32)], axis=1)
    blocks = jnp.take(cols, jnp.asarray(_na_block_index().reshape(-1)), axis=1)
    blocks = blocks.reshape(H_C, 3, NA_TILE_ROWS, NA_KEY_ROWS, GRID_W, GRID_W)
    return blocks.transpose(1, 0, 2, 4, 3, 5).reshape(3, H_C, NA_Q, NA_K)


def _rope_tables():
    t = np.arange(DEC_SEQ)
    n_freq = HEAD_DIM // 4
    inv = (ROPE_BASE ** (-np.arange(n_freq, dtype=np.float32) / n_freq)).astype(np.float32)
    row = (t // GRID_W).astype(np.float32)[:, None] * inv
    col = (t % GRID_W).astype(np.float32)[:, None] * inv
    ang = np.concatenate([row, col], -1)
    cos, sin = np.cos(ang), np.sin(ang)
    cos_h = np.concatenate([cos, cos], -1)
    sin_h = np.concatenate([-sin, sin], -1)
    reps = LANES // HEAD_DIM
    return (jnp.asarray(np.tile(cos_h, (1, reps)), F32), jnp.asarray(np.tile(sin_h, (1, reps)), F32))


def _first_index_of(mask, iota, sentinel):
    return jnp.min(jnp.where(mask, iota, sentinel), axis=0, keepdims=True)


def _route(logits, b_col):
    n = logits.shape[1]
    scores = jax.nn.sigmoid(logits)
    sel = scores + b_col
    io_g = lax.broadcasted_iota(I32, (GROUP_SIZE, n), 0)
    gs_rows = []
    for g in range(N_GROUPS):
        s = sel[g * GROUP_SIZE:(g + 1) * GROUP_SIZE]
        m1 = jnp.max(s, axis=0, keepdims=True)
        i1 = _first_index_of(s == m1, io_g, GROUP_SIZE)
        m2 = jnp.max(jnp.where(io_g == i1, PICKED, s), axis=0, keepdims=True)
        gs_rows.append(m1 + m2)
    gs = jnp.concatenate(gs_rows, axis=0)
    io_n = lax.broadcasted_iota(I32, (N_GROUPS, n), 0)
    gsel = jnp.zeros((N_GROUPS, n), F32)
    for _ in range(TOPK_GROUPS):
        mg = jnp.max(gs, axis=0, keepdims=True)
        gi = _first_index_of(gs == mg, io_n, N_GROUPS)
        hit = io_n == gi
        gsel = jnp.where(hit, 1.0, gsel)
        gs = jnp.where(hit, PICKED, gs)
    cand = jnp.concatenate(
        [jnp.where(gsel[g:g + 1] > 0.5, sel[g * GROUP_SIZE:(g + 1) * GROUP_SIZE], NEG_INF)
         for g in range(N_GROUPS)], axis=0)
    io_e = lax.broadcasted_iota(I32, (N_EXPERTS, n), 0)
    picks, raw = [], []
    for _ in range(TOP_K):
        mv = jnp.max(cand, axis=0, keepdims=True)
        ei = _first_index_of(cand == mv, io_e, N_EXPERTS)
        hit = io_e == ei
        picks.append((hit, ei))
        raw.append(jnp.sum(jnp.where(hit, scores, 0.0), axis=0, keepdims=True))
        cand = jnp.where(hit, PICKED, cand)
    return picks, raw


def _post_mixer_kernel(xc_ref, xl_ref, mixc_ref, ya_ref, yb_ref, yc_ref, wout_ref, mod_ref, g_ref, b_ref,
                       wrh_ref, wrl_ref, br_ref,
                       x1_ref, h2_ref, eidx_ref, wsel_ref, rank_ref, cnt_ref):
    i = pl.program_id(0)
    tm = TM_TOK

    @pl.when(i == 0)
    def _():
        cnt_ref[...] = jnp.zeros_like(cnt_ref)

    ci = _cond_row(i, tm)
    gate1 = mod_ref[pl.ds(ci, 1), 2 * D_MODEL:3 * D_MODEL]
    sh2 = mod_ref[pl.ds(ci, 1), 3 * D_MODEL:4 * D_MODEL]
    sc2 = mod_ref[pl.ds(ci, 1), 4 * D_MODEL:5 * D_MODEL]
    mix_lat = jnp.concatenate([ya_ref[...], yb_ref[...], yc_ref[...]], axis=-1)
    mix = jnp.where(i < T_CTX // tm, mixc_ref[...], mix_lat)
    y = _dot(mix, wout_ref[...])
    x = jnp.where(i < T_CTX // tm, xc_ref[...], xl_ref[...])
    x1 = _layer_norm(ALPHA * x + gate1 * y, g_ref[...], b_ref[...])
    x1_ref[...] = x1
    h2 = x1 * (1.0 + sc2) + sh2
    h_hi = h2.astype(BF16)
    h2_ref[...] = h_hi
    h_lo = (h2 - h_hi.astype(F32)).astype(BF16)
    logits = (_dot_nt(wrh_ref[...], h_hi) + _dot_nt(wrh_ref[...], h_lo)
              + _dot_nt(wrl_ref[...], h_hi))
    picks, raw = _route(logits, br_ref[...])

    total = raw[0]
    for r in raw[1:]:
        total = total + r
    scale = ROUTE_SCALE / total
    multi = jnp.zeros((N_EXPERTS, tm), F32)
    for hit, _ in picks:
        multi = multi + jnp.where(hit, 1.0, 0.0)
    before = (lax.broadcasted_iota(I32, (tm, tm), 0) < lax.broadcasted_iota(I32, (tm, tm), 1))
    cum = _dot(multi.astype(BF16), jnp.where(before, 1.0, 0.0).astype(BF16))
    pad = jnp.zeros((SUBLANES - TOP_K, tm), F32)
    eidx_ref[...] = jnp.concatenate([ei for _, ei in picks] + [pad.astype(I32)], axis=0)
    wsel_ref[...] = jnp.concatenate([r * scale for r in raw] + [pad], axis=0)
    rank_ref[...] = jnp.concatenate(
        [jnp.sum(jnp.where(hit, cum, 0.0), axis=0, keepdims=True) for hit, _ in picks] + [pad],
        axis=0).astype(I32)
    tile_lane = lax.broadcasted_iota(I32, (N_EXPERTS, LANES), 1)
    cnt_ref[...] = jnp.where(tile_lane == i, jnp.sum(multi, axis=1, keepdims=True), cnt_ref[...])


def _post_mixer(x_ctx, x_lat, mix_c, ya, yb, yc, w_out_bf16, mod, ln_g, ln_b, wr_hi_t, wr_lo_t, b_router_col):
    n_ctx = T_CTX // TM_TOK
    ctx_map = lambda i: (jnp.minimum(i, n_ctx - 1), 0)
    lat_map = lambda i: (jnp.maximum(i - n_ctx, 0), 0)
    row_map = lambda i: (i, 0)
    const = lambda i: (0, 0)
    tok_map = lambda i: (0, i)
    return pl.pallas_call(
        _post_mixer_kernel,
        out_shape=(
            jax.ShapeDtypeStruct((T_ALL, D_MODEL), F32),
            jax.ShapeDtypeStruct((T_ALL, D_MODEL), BF16),
            jax.ShapeDtypeStruct((SUBLANES, T_ALL), I32),
            jax.ShapeDtypeStruct((SUBLANES, T_ALL), F32),
            jax.ShapeDtypeStruct((SUBLANES, T_ALL), I32),
            jax.ShapeDtypeStruct((N_EXPERTS, LANES), F32),
        ),
        grid=(N_TOK_TILES,),
        in_specs=[
            pl.BlockSpec((TM_TOK, D_MODEL), ctx_map),
            pl.BlockSpec((TM_TOK, D_MODEL), lat_map),
            pl.BlockSpec((TM_TOK, D_MODEL), ctx_map),
            pl.BlockSpec((TM_TOK, W_A), lat_map),
            pl.BlockSpec((TM_TOK, W_B), lat_map),
            pl.BlockSpec((TM_TOK, W_C), lat_map),
            pl.BlockSpec((D_MODEL, D_MODEL), const),
            pl.BlockSpec((N_COND, 6 * D_MODEL), const),
            pl.BlockSpec((1, D_MODEL), const),
            pl.BlockSpec((1, D_MODEL), const),
            pl.BlockSpec((N_EXPERTS, D_MODEL), const),
            pl.BlockSpec((N_EXPERTS, D_MODEL), const),
            pl.BlockSpec((N_EXPERTS, 1), const),
        ],
        out_specs=(
            pl.BlockSpec((TM_TOK, D_MODEL), row_map),
            pl.BlockSpec((TM_TOK, D_MODEL), row_map),
            pl.BlockSpec((SUBLANES, TM_TOK), tok_map),
            pl.BlockSpec((SUBLANES, TM_TOK), tok_map),
            pl.BlockSpec((SUBLANES, TM_TOK), tok_map),
            pl.BlockSpec((N_EXPERTS, LANES), const),
        ),
        compiler_params=pltpu.CompilerParams(
            dimension_semantics=("arbitrary",), vmem_limit_bytes=VMEM_LIMIT),
        name="post_mixer",
    )(x_ctx, x_lat, mix_c, ya, yb, yc, w_out_bf16, mod, ln_g, ln_b, wr_hi_t, wr_lo_t, b_router_col)


def _plan_kernel(eidx_ref, rank_ref, nmat_ref, lslot_ref, unit_ref, gend_ref, blk_ref):
    i = pl.program_id(0)
    units = jnp.floor((nmat_ref[...] + (UNIT - 1.0)) * (1.0 / UNIT))
    units_bf = units.astype(BF16)
    earlier_e = (lax.broadcasted_iota(I32, (N_EXPERTS, N_EXPERTS), 1)
                 < lax.broadcasted_iota(I32, (N_EXPERTS, N_EXPERTS), 0))
    tri_e = jnp.where(earlier_e, 1.0, 0.0).astype(BF16)
    earlier_t = (lax.broadcasted_iota(I32, (LANES, LANES), 0) < lax.broadcasted_iota(I32, (LANES, LANES), 1))
    tri_t = jnp.where(earlier_t, 1.0, 0.0).astype(BF16)
    local_off = _dot(tri_e, units_bf)
    tile_off = _dot(units_bf, tri_t)
    per_expert = jnp.sum(units, axis=1, keepdims=True)
    blocks = jnp.floor((per_expert + (UNITS_PER_BLOCK - 1.0)) * (1.0 / UNITS_PER_BLOCK))
    blocks_l = jnp.broadcast_to(blocks, (N_EXPERTS, LANES))
    start_blk = _dot(tri_e, blocks_l.astype(BF16))
    end_blk = start_blk + blocks_l
    gend_ref[...] = (end_blk * BM).astype(I32)
    blk_id = lax.broadcasted_iota(I32, (N_EXPERTS, BLK_LANES), 1).astype(F32)
    owner = jnp.sum(jnp.where(end_blk[:, 0:1] <= blk_id, 1.0, 0.0), axis=0, keepdims=True)
    blk_ref[...] = jnp.minimum(owner, N_EXPERTS - 1.0).astype(I32)

    this_tile = lax.broadcasted_iota(I32, (N_EXPERTS, LANES), 1) == i

    def column(a):
        return jnp.sum(jnp.where(this_tile, a, 0.0), axis=1, keepdims=True)

    lo, n_u = column(local_off), column(units)
    base_unit = start_blk[:, 0:1] * UNITS_PER_BLOCK + column(tile_off) - lo
    u = lax.broadcasted_iota(I32, (N_EXPERTS, MAX_UNITS), 1).astype(F32)
    inside = jnp.where(u >= lo, jnp.where(u < lo + n_u, 1.0, 0.0), 0.0)
    dst_unit = jnp.sum(inside * (base_unit + u), axis=0, keepdims=True)
    used = jnp.sum(inside, axis=0, keepdims=True) > 0.5
    spare = (SPARE_UNIT0 + (i % 2) * MAX_UNITS).astype(F32) + u[0:1, :]
    unit_ref[0] = jnp.where(used, dst_unit, spare).astype(I32)

    io_e = lax.broadcasted_iota(I32, (N_EXPERTS, TM_TOK), 0)
    rows = []
    for k in range(TOP_K):
        hit = io_e == eidx_ref[k:k + 1, :]
        seg = jnp.sum(jnp.where(hit, lo * UNIT, 0.0), axis=0, keepdims=True)
        rows.append(seg.astype(I32) + rank_ref[k:k + 1, :])
    rows.append(jnp.full((SUBLANES - TOP_K, TM_TOK), -1, I32))
    lslot_ref[...] = jnp.concatenate(rows, axis=0)


def _slot_plan(eidx, rank, nmat):
    tok_map = lambda i: (0, i)
    const = lambda i: (0, 0)
    lslot, unit_tab, gend, blk = pl.pallas_call(
        _plan_kernel,
        out_shape=(
            jax.ShapeDtypeStruct((SUBLANES, T_ALL), I32),
            jax.ShapeDtypeStruct((N_TOK_TILES, 1, MAX_UNITS), I32),
            jax.ShapeDtypeStruct((N_EXPERTS, LANES), I32),
            jax.ShapeDtypeStruct((1, BLK_LANES), I32),
        ),
        grid=(N_TOK_TILES,),
        in_specs=[
            pl.BlockSpec((SUBLANES, TM_TOK), tok_map),
            pl.BlockSpec((SUBLANES, TM_TOK), tok_map),
            pl.BlockSpec((N_EXPERTS, LANES), const),
        ],
        out_specs=(
            pl.BlockSpec((SUBLANES, TM_TOK), tok_map),
            pl.BlockSpec((1, 1, MAX_UNITS), lambda i: (i, 0, 0)),
            pl.BlockSpec((N_EXPERTS, LANES), const),
            pl.BlockSpec((1, BLK_LANES), const),
        ),
        compiler_params=pltpu.CompilerParams(dimension_semantics=("arbitrary",)),
        name="slot_plan",
    )(eidx, rank, nmat)
    gend = gend[:, 0]
    n_used = gend[N_EXPERTS - 1:] // BM
    return lslot, unit_tab.reshape(N_TOK_TILES * MAX_UNITS), gend, blk[0, :N_BLOCKS_ALL], n_used


PACK_W = D_MODEL // 2
HI_HALF = -65536


def _pack_pairs(x):
    lo = lax.bitcast_convert_type(x[:, 0:PACK_W], I32)
    hi = lax.bitcast_convert_type(x[:, PACK_W:D_MODEL], I32)
    return lax.shift_right_logical(lo, 16) | (hi & HI_HALF)


def _unpack_pairs(u):
    lo = lax.bitcast_convert_type(lax.shift_left(u, 16), F32).astype(BF16)
    hi = lax.bitcast_convert_type(u & HI_HALF, F32).astype(BF16)
    return lo, hi


def _unit_rows(unit):
    row = unit * UNIT
    return pl.ds(row if isinstance(unit, int) else pl.multiple_of(row, UNIT), UNIT)


def _unit_copy(src, src_unit, dst, dst_unit, sem):
    return pltpu.make_async_copy(src.at[_unit_rows(src_unit)], dst.at[_unit_rows(dst_unit)], sem)


def _dispatch_kernel(gend_ref, tab_ref, h2_ref, lslot_ref, xs_hbm, zero_ref, local_ref, sem_zero, sem_rows):
    i = pl.program_id(0)
    buf = i % 2

    def drain(b):
        pltpu.make_async_copy(local_ref.at[b], xs_hbm.at[pl.ds(0, LOCAL_ROWS)], sem_rows.at[b]).wait()

    def has_rows(e):
        return gend_ref[e] > jnp.where(e == 0, 0, gend_ref[jnp.maximum(e - 1, 0)])

    def zero_copy(e):
        return pltpu.make_async_copy(
            zero_ref, xs_hbm.at[pl.ds(pl.multiple_of(gend_ref[e] - BM, BM), BM)], sem_zero)

    @pl.when(i == 0)
    def _():
        zero_ref[...] = jnp.zeros_like(zero_ref)

        def start(e, c):
            @pl.when(has_rows(e))
            def _():
                zero_copy(e).start()
            return c

        def wait(e, c):
            @pl.when(has_rows(e))
            def _():
                zero_copy(e).wait()
            return c

        def tail_copy(blk):
            return pltpu.make_async_copy(
                zero_ref, xs_hbm.at[pl.ds(pl.multiple_of(blk * BM, BM), BM)], sem_zero)

        def start_tail(blk, c):
            tail_copy(blk).start()
            return c

        def wait_tail(blk, c):
            tail_copy(blk).wait()
            return c

        n_used = gend_ref[N_EXPERTS - 1] // BM
        lax.fori_loop(0, N_EXPERTS, start, 0)
        lax.fori_loop(n_used, N_BLOCKS_ALL, start_tail, 0)
        lax.fori_loop(0, N_EXPERTS, wait, 0)
        lax.fori_loop(n_used, N_BLOCKS_ALL, wait_tail, 0)

    @pl.when(i >= 2)
    def _():
        drain(buf)

    h2 = h2_ref[...]
    units_per_chunk = PERM_CHUNK // UNIT
    local = local_ref.at[buf]
    for c in range(LOCAL_ROWS // PERM_CHUNK):
        slot = c * PERM_CHUNK + lax.broadcasted_iota(I32, (PERM_CHUNK, TM_TOK), 0)
        p = jnp.zeros((PERM_CHUNK, TM_TOK), F32)
        for k in range(TOP_K):
            p = jnp.where(slot == lslot_ref[k:k + 1, :], 1.0, p)
        local[c * PERM_CHUNK:(c + 1) * PERM_CHUNK, :] = _pack_pairs(_dot(p.astype(BF16), h2))
        for u in range(c * units_per_chunk, (c + 1) * units_per_chunk):
            _unit_copy(local, u, xs_hbm, tab_ref[i * MAX_UNITS + u], sem_rows.at[buf]).start()

    @pl.when(i == N_TOK_TILES - 1)
    def _():
        drain(1 - buf)
        drain(buf)


def _dispatch(h2, lslot, unit_tab, gend):
    return pl.pallas_call(
        _dispatch_kernel,
        out_shape=jax.ShapeDtypeStruct((N_SLOTS, PACK_W), I32),
        grid_spec=pltpu.PrefetchScalarGridSpec(
            num_scalar_prefetch=2,
            grid=(N_TOK_TILES,),
            in_specs=[
                pl.BlockSpec((TM_TOK, D_MODEL), lambda i, ge, tab: (i, 0)),
                pl.BlockSpec((SUBLANES, TM_TOK), lambda i, ge, tab: (0, i)),
            ],
            out_specs=pl.BlockSpec(memory_space=pl.ANY),
            scratch_shapes=[
                pltpu.VMEM((BM, PACK_W), I32),
                pltpu.VMEM((2, LOCAL_ROWS, PACK_W), I32),
                pltpu.SemaphoreType.DMA,
                pltpu.SemaphoreType.DMA((2,)),
            ],
        ),
        compiler_params=pltpu.CompilerParams(
            dimension_semantics=("arbitrary",), vmem_limit_bytes=VMEM_LIMIT),
        name="moe_dispatch",
    )(gend, unit_tab, h2, lslot)


def _expert_kernel(blk_e_ref, n_used_ref, xs_ref, wgu_ref, wdown_ref, ys_ref, wgu_bf, wdown_bf):
    i = pl.program_id(0)
    live = i < n_used_ref[0]
    new_expert = jnp.logical_or(i == 0, blk_e_ref[i] != blk_e_ref[jnp.maximum(i - 1, 0)])

    @pl.when(jnp.logical_and(live, new_expert))
    def _():
        wgu_bf[...] = wgu_ref[0].astype(BF16)
        wdown_bf[...] = wdown_ref[0].astype(BF16)

    @pl.when(live)
    def _():
        x_lo, x_hi = _unpack_pairs(xs_ref[...])
        gu = _dot(x_lo, wgu_bf[0:PACK_W, :]) + _dot(x_hi, wgu_bf[PACK_W:D_MODEL, :])
        act = _silu(gu[:, 0:D_EXPERT]) * gu[:, D_EXPERT:2 * D_EXPERT]
        y = _dot(act.astype(BF16), wdown_bf[...])
        ys_ref[...] = _pack_pairs(y.astype(BF16).astype(F32))

    @pl.when(i >= n_used_ref[0])
    def _():
        ys_ref[...] = jnp.zeros_like(ys_ref)


def _experts(xs, w_gu, w_down, blk_e, n_used):
    def blk(i, be, nu):
        return jnp.minimum(i, nu[0] - 1)

    return pl.pallas_call(
        _expert_kernel,
        out_shape=jax.ShapeDtypeStruct((N_SLOTS, PACK_W), I32),
        grid_spec=pltpu.PrefetchScalarGridSpec(
            num_scalar_prefetch=2,
            grid=(N_BLOCKS_ALL,),
            in_specs=[
                pl.BlockSpec((BM, PACK_W), lambda i, be, nu: (blk(i, be, nu), 0)),
                pl.BlockSpec((1, D_MODEL, 2 * D_EXPERT), lambda i, be, nu: (be[blk(i, be, nu)], 0, 0)),
                pl.BlockSpec((1, D_EXPERT, D_MODEL), lambda i, be, nu: (be[blk(i, be, nu)], 0, 0)),
            ],
            out_specs=pl.BlockSpec((BM, PACK_W), lambda i, be, nu: (i, 0)),
            scratch_shapes=[
                pltpu.VMEM((D_MODEL, 2 * D_EXPERT), BF16),
                pltpu.VMEM((D_EXPERT, D_MODEL), BF16),
            ],
        ),
        compiler_params=pltpu.CompilerParams(
            dimension_semantics=("arbitrary",), vmem_limit_bytes=VMEM_LIMIT),
        name="moe_experts",
    )(blk_e, n_used, xs, w_gu, w_down)


def _combine_kernel(tab_ref, x1_ref, h2_ref, lslot_ref, wsel_ref, ys_hbm, wsgu_ref, wsdown_ref, mod_ref,
                    g_ref, b_ref, outc_ref, outl_ref, local_ref, sel_ref, ylo_ref, yhi_ref, sem_rows):
    i = pl.program_id(0)
    tm = TM_TOK
    buf = i % 2

    def fetch(tile, b):
        def body(u, c):
            _unit_copy(ys_hbm, tab_ref[tile * MAX_UNITS + u], local_ref.at[b], u, sem_rows.at[b]).start()
            return c

        lax.fori_loop(0, MAX_UNITS, body, 0, unroll=8)

    @pl.when(i == 0)
    def _():
        fetch(0, 0)

    @pl.when(i + 1 < N_TOK_TILES)
    def _():
        fetch(i + 1, 1 - buf)

    sgu = _dot(h2_ref[...], wsgu_ref[...])
    act = _silu(sgu[:, 0:D_SHARED]) * sgu[:, D_SHARED:2 * D_SHARED]
    f = _dot(act.astype(BF16), wsdown_ref[...])

    w = wsel_ref[...]
    ls = lslot_ref[...]
    for c in range(LOCAL_ROWS // PERM_CHUNK):
        slot = c * PERM_CHUNK + lax.broadcasted_iota(I32, (tm, PERM_CHUNK), 1)
        sel = jnp.zeros((tm, PERM_CHUNK), F32)
        for k in range(TOP_K):
            sel = jnp.where(slot == ls[:, k:k + 1], w[:, k:k + 1], sel)
        sel_ref[:, c * PERM_CHUNK:(c + 1) * PERM_CHUNK] = sel.astype(BF16)

    local = local_ref.at[buf]
    pltpu.make_async_copy(ys_hbm.at[pl.ds(0, LOCAL_ROWS)], local, sem_rows.at[buf]).wait()
    for c in range(LOCAL_ROWS // PERM_CHUNK):
        rows = slice(c * PERM_CHUNK, (c + 1) * PERM_CHUNK)
        ylo_ref[rows, :], yhi_ref[rows, :] = _unpack_pairs(local[rows, :])
    sel = sel_ref[...]
    f = f + jnp.concatenate([_dot(sel, ylo_ref[...]), _dot(sel, yhi_ref[...])], axis=-1)
    ci = _cond_row(i, tm)
    gate2 = mod_ref[pl.ds(ci, 1), 5 * D_MODEL:6 * D_MODEL]
    out = _layer_norm(ALPHA * x1_ref[...] + gate2 * f, g_ref[...], b_ref[...])

    @pl.when(i < T_CTX // tm)
    def _():
        outc_ref[...] = out

    @pl.when(i >= T_CTX // tm)
    def _():
        outl_ref[...] = out


def _combine(x1, h2, lslot_rows, wsel_rows, unit_tab, ys, w_sgu_bf16, w_sdown_bf16, mod, ln_g, ln_b):
    n_ctx = T_CTX // TM_TOK
    row_map = lambda i, tab: (i, 0)
    const = lambda i, tab: (0, 0)
    return pl.pallas_call(
        _combine_kernel,
        out_shape=(jax.ShapeDtypeStruct((T_CTX, D_MODEL), F32),
                   jax.ShapeDtypeStruct((T_LAT, D_MODEL), F32)),
        grid_spec=pltpu.PrefetchScalarGridSpec(
            num_scalar_prefetch=1,
            grid=(N_TOK_TILES,),
            in_specs=[
                pl.BlockSpec((TM_TOK, D_MODEL), row_map),
                pl.BlockSpec((TM_TOK, D_MODEL), row_map),
                pl.BlockSpec((TM_TOK, SUBLANES), row_map),
                pl.BlockSpec((TM_TOK, SUBLANES), row_map),
                pl.BlockSpec(memory_space=pl.ANY),
                pl.BlockSpec((D_MODEL, 2 * D_SHARED), const),
                pl.BlockSpec((D_SHARED, D_MODEL), const),
                pl.BlockSpec((N_COND, 6 * D_MODEL), const),
                pl.BlockSpec((1, D_MODEL), const),
                pl.BlockSpec((1, D_MODEL), const),
            ],
            out_specs=(pl.BlockSpec((TM_TOK, D_MODEL), lambda i, tab: (jnp.minimum(i, n_ctx - 1), 0)),
                       pl.BlockSpec((TM_TOK, D_MODEL), lambda i, tab: (jnp.maximum(i - n_ctx, 0), 0))),
            scratch_shapes=[
                pltpu.VMEM((2, LOCAL_ROWS, PACK_W), I32),
                pltpu.VMEM((TM_TOK, LOCAL_ROWS), BF16),
                pltpu.VMEM((LOCAL_ROWS, PACK_W), BF16),
                pltpu.VMEM((LOCAL_ROWS, PACK_W), BF16),
                pltpu.SemaphoreType.DMA((2,)),
            ],
        ),
        compiler_params=pltpu.CompilerParams(
            dimension_semantics=("arbitrary",), vmem_limit_bytes=VMEM_LIMIT),
        name="moe_combine",
    )(unit_tab, x1, h2, lslot_rows, wsel_rows, ys, w_sgu_bf16, w_sdown_bf16, mod, ln_g, ln_b)


def _lane_rows(v, width):
    return jnp.broadcast_to(v.astype(F32)[:, None, None], (v.shape[0], 1, width))


def kernel(x_prompt, x_sample, state_ret_fwd, state_ret_bwd, cache_win_k, cache_win_v, cache_na_k, cache_na_v, c, c_ctx, w_in, w_out, ret_decay_fwd, ret_decay_bwd, ret_gn_g, ret_gn_b, win_sink, na_rpb, w_mod, b_mod, ln1_g, ln1_b, ln2_g, ln2_b, w_router, b_router, w_expert_gu, w_expert_down, w_shared_gu, w_shared_down):
    cond = jnp.concatenate(
        [c_ctx[None, :], c, jnp.zeros((N_COND - 1 - DEC_BATCH, D_MODEL), F32)], axis=0)
    mod_all = _modulation(cond, w_mod, b_mod)
    cos_t, sin_t = _rope_tables()

    x_ctx = x_prompt.reshape(T_CTX, D_MODEL)
    x_lat = x_sample.reshape(T_LAT, D_MODEL)
    sf_l, sb_l, cache_l = [], [], []
    for l in range(DEPTH):
        mod = mod_all[l]
        pa, pb, pc, *caches = _in_projection(x_ctx, x_lat, mod, w_in[l].astype(BF16))
        cache_l.append(caches)
        decf_s, decb_s = _lane_rows(ret_decay_fwd[l], SEQ), _lane_rows(ret_decay_bwd[l], SEQ)
        gng, gnb = ret_gn_g[l][None, :], ret_gn_b[l][None, :]
        mix_c, st_f, st_b = _ctx_mixers(pa, pb, pc, win_sink[l], decf_s, decb_s, gng, gnb)
        sf_l.append(st_f)
        sb_l.append(st_b)
        ya = _lat_retention(pa, state_ret_fwd, state_ret_bwd, l,
                            _lane_rows(ret_decay_fwd[l], RET_CHUNK), _lane_rows(ret_decay_bwd[l], RET_CHUNK),
                            gng, gnb)
        yb = _lat_window_attn(pb, cache_win_k, cache_win_v, l, win_sink[l], cos_t, sin_t)
        yc = _lat_na_attn(pc, cache_na_k, cache_na_v, l, _na_maskbias(na_rpb[l]))

        wr_t = w_router[l].T
        wr_hi = wr_t.astype(BF16)
        wr_lo = (wr_t - wr_hi.astype(F32)).astype(BF16)
        x1, h2, eidx, wsel, rank, counts = _post_mixer(
            x_ctx, x_lat, mix_c, ya, yb, yc, w_out[l].astype(BF16), mod, ln1_g[l][None, :],
            ln1_b[l][None, :], wr_hi, wr_lo, b_router[l][:, None])
        lslot, unit_tab, gend, blk_e, n_used = _slot_plan(eidx, rank, counts)
        xs = _dispatch(h2, lslot, unit_tab, gend)
        ys = _experts(xs, w_expert_gu[l], w_expert_down[l], blk_e, n_used)
        x_ctx, x_lat = _combine(x1, h2, lslot.T, wsel.T, unit_tab, ys, w_shared_gu[l].astype(BF16),
                                w_shared_down[l].astype(BF16), mod, ln2_g[l][None, :], ln2_b[l][None, :])

    y_prompt = x_ctx.reshape(BATCH, SEQ, D_MODEL)
    y_sample = x_lat.reshape(DEC_BATCH, DEC_SEQ, D_MODEL)
    new_sf = jnp.stack(sf_l, axis=1)
    new_sb = jnp.stack(sb_l, axis=1)

    new_caches = [jnp.stack([layer[j] for layer in cache_l], axis=1) for j in range(4)]
    return (y_prompt, y_sample, new_sf, new_sb, *new_caches)
```

```python
import functools

import numpy as np
import jax
import jax.numpy as jnp
from jax import lax
from jax.experimental import pallas as pl
from jax.experimental.pallas import tpu as pltpu

F32 = jnp.float32
BF16 = jnp.bfloat16
I32 = jnp.int32

D_MODEL = 1024
BATCH = 32
SEQ = 256
DEPTH = 2
DEC_BATCH = 4
DEC_SEQ = 2048
PAST_LEN = 256
GRID_W = 64
HEAD_DIM = 64
ATTN_SCALE = HEAD_DIM ** -0.5
H_A = 4
W_A = H_A * HEAD_DIM
GN_EPS = 1e-5
H_B = 6
KV_B = 2
W_B = H_B * HEAD_DIM
KV_W_B = KV_B * HEAD_DIM
WINDOW = 128
ROPE_BASE = 10000.0
H_C = 6
W_C = H_C * HEAD_DIM
NA_ROWS = 8
NA_COLS = 16
IN_WIDTH = 4 * W_A + W_B + 2 * KV_W_B + 3 * W_C
N_EXPERTS = 64
TOP_K = 6
N_GROUPS = 8
GROUP_SIZE = N_EXPERTS // N_GROUPS
TOPK_GROUPS = 4
D_EXPERT = 256
D_SHARED = 256
ROUTE_SCALE = 2.5
ALPHA = (2 * DEPTH) ** 0.25
LN_EPS = 1e-5
NEG_INF = -1e30
PICKED = -3e38

T_CTX = BATCH * SEQ
T_LAT = DEC_BATCH * DEC_SEQ
T_ALL = T_CTX + T_LAT
N_COND = 8

PA_W = 4 * W_A
PB_W = W_B + 2 * KV_W_B
PC_W = 3 * W_C

LANES = 128
SUBLANES = 8
VMEM_LIMIT = 56 * 1024 * 1024

TM_PROJ = 512
TM_TOK = 256
RET_CHUNK = 256
WIN_QB = 128
NA_TILE_ROWS = 4
NA_KEY_ROWS = 11
BM = 512
EXPERT_SPLIT = 1
UNIT = SUBLANES
UNITS_PER_BLOCK = BM // UNIT
PERM_CHUNK = 256
N_TOK_TILES = T_ALL // TM_TOK
LOCAL_ROWS = -(-(TM_TOK * TOP_K + N_EXPERTS * (UNIT - 1)) // PERM_CHUNK) * PERM_CHUNK
MAX_UNITS = LOCAL_ROWS // UNIT
N_ASSIGN = T_ALL * TOP_K
N_BLOCKS = -(-(N_ASSIGN + N_TOK_TILES * N_EXPERTS * (UNIT - 1) + N_EXPERTS * (BM - 1)) // BM)
SPARE_BLOCKS = 2 * -(-LOCAL_ROWS // BM)
SPARE_UNIT0 = N_BLOCKS * UNITS_PER_BLOCK
N_BLOCKS_ALL = N_BLOCKS + SPARE_BLOCKS
N_SLOTS = N_BLOCKS_ALL * BM
BLK_LANES = -(-N_BLOCKS_ALL // LANES) * LANES


def _dot(a, b):
    return jnp.dot(a, b, preferred_element_type=F32)


def _dot_nt(a, b):
    return lax.dot_general(a, b, (((1,), (1,)), ((), ())), preferred_element_type=F32)


def _silu(x):
    return x * jax.nn.sigmoid(x)


def _log_sigmoid(x):
    return jnp.minimum(x, 0.0) - jnp.log(1.0 + jnp.exp(-jnp.abs(x)))


def _cond_row(tile, tile_rows):
    n_ctx = T_CTX // tile_rows
    per_lat = DEC_SEQ // tile_rows
    return jnp.where(tile < n_ctx, 0, 1 + (tile - n_ctx) // per_lat)


def _layer_norm(x, g, b):
    mu = jnp.mean(x, -1, keepdims=True)
    xc = x - mu
    var = jnp.mean(xc * xc, -1, keepdims=True)
    return xc * lax.rsqrt(var + LN_EPS) * g + b


MOD_TN = 1536


def _mod_kernel(cond_ref, w_ref, b_ref, o_ref):
    s = _silu(cond_ref[...])
    s_hi = s.astype(BF16)
    s_lo = (s - s_hi.astype(F32)).astype(BF16)
    w = w_ref[0]
    w_hi = w.astype(BF16)
    w_lo = (w - w_hi.astype(F32)).astype(BF16)
    o_ref[0] = _dot(s_hi, w_hi) + _dot(s_lo, w_hi) + _dot(s_hi, w_lo) + b_ref[0]


def _modulation(cond, w_mod, b_mod):
    n_out = 6 * D_MODEL
    return pl.pallas_call(
        _mod_kernel,
        out_shape=jax.ShapeDtypeStruct((DEPTH, N_COND, n_out), F32),
        grid=(DEPTH, n_out // MOD_TN),
        in_specs=[
            pl.BlockSpec((N_COND, D_MODEL), lambda l, j: (0, 0)),
            pl.BlockSpec((1, D_MODEL, MOD_TN), lambda l, j: (l, 0, j)),
            pl.BlockSpec((1, 1, MOD_TN), lambda l, j: (l, 0, j)),
        ],
        out_specs=pl.BlockSpec((1, N_COND, MOD_TN), lambda l, j: (l, 0, j)),
        compiler_params=pltpu.CompilerParams(
            dimension_semantics=("arbitrary", "arbitrary"), vmem_limit_bytes=VMEM_LIMIT),
        name="modulation",
    )(cond, w_mod, b_mod.reshape(DEPTH, 1, n_out))


SEQ_PER_PROJ = TM_PROJ // SEQ


def _inproj_kernel(xc_ref, xl_ref, mod_ref, w_ref, pa_ref, pb_ref, pc_ref, wk_ref, wv_ref, nk_ref, nv_ref):
    i = pl.program_id(0)
    ci = _cond_row(i, TM_PROJ)
    sh = mod_ref[pl.ds(ci, 1), 0:D_MODEL]
    sc = mod_ref[pl.ds(ci, 1), D_MODEL:2 * D_MODEL]
    x = jnp.where(i < T_CTX // TM_PROJ, xc_ref[...], xl_ref[...])
    h = x * (1.0 + sc) + sh
    p = _dot(h.astype(BF16), w_ref[...])
    pa_ref[...] = p[:, 0:PA_W].astype(BF16)
    pb_ref[...] = p[:, PA_W:PA_W + PB_W].astype(BF16)
    pc_ref[...] = p[:, PA_W + PB_W:IN_WIDTH].astype(BF16)

    @pl.when(i < T_CTX // TM_PROJ)
    def _():
        targets = ((wk_ref, PA_W + W_B, KV_B), (wv_ref, PA_W + W_B + KV_W_B, KV_B),
                   (nk_ref, PA_W + PB_W + W_C, H_C), (nv_ref, PA_W + PB_W + 2 * W_C, H_C))
        for ref, col0, n_heads in targets:
            for s in range(SEQ_PER_PROJ):
                for hh in range(n_heads):
                    ref[s, hh] = p[s * SEQ:(s + 1) * SEQ, col0 + hh * HEAD_DIM:col0 + (hh + 1) * HEAD_DIM]


def _in_projection(x_ctx, x_lat, mod, w_in_bf16):
    n_ctx_tiles = T_CTX // TM_PROJ

    def cache_spec(n_heads):
        return pl.BlockSpec((SEQ_PER_PROJ, n_heads, SEQ, HEAD_DIM),
                            lambda i: (jnp.minimum(i, n_ctx_tiles - 1), 0, 0, 0))

    return pl.pallas_call(
        _inproj_kernel,
        out_shape=(
            jax.ShapeDtypeStruct((T_ALL, PA_W), BF16),
            jax.ShapeDtypeStruct((T_ALL, PB_W), BF16),
            jax.ShapeDtypeStruct((T_ALL, PC_W), BF16),
            jax.ShapeDtypeStruct((BATCH, KV_B, SEQ, HEAD_DIM), F32),
            jax.ShapeDtypeStruct((BATCH, KV_B, SEQ, HEAD_DIM), F32),
            jax.ShapeDtypeStruct((BATCH, H_C, SEQ, HEAD_DIM), F32),
            jax.ShapeDtypeStruct((BATCH, H_C, SEQ, HEAD_DIM), F32),
        ),
        grid=(T_ALL // TM_PROJ,),
        in_specs=[
            pl.BlockSpec((TM_PROJ, D_MODEL), lambda i: (jnp.minimum(i, n_ctx_tiles - 1), 0)),
            pl.BlockSpec((TM_PROJ, D_MODEL), lambda i: (jnp.maximum(i - n_ctx_tiles, 0), 0)),
            pl.BlockSpec((N_COND, 6 * D_MODEL), lambda i: (0, 0)),
            pl.BlockSpec((D_MODEL, IN_WIDTH), lambda i: (0, 0)),
        ],
        out_specs=(
            pl.BlockSpec((TM_PROJ, PA_W), lambda i: (i, 0)),
            pl.BlockSpec((TM_PROJ, PB_W), lambda i: (i, 0)),
            pl.BlockSpec((TM_PROJ, PC_W), lambda i: (i, 0)),
            cache_spec(KV_B), cache_spec(KV_B), cache_spec(H_C), cache_spec(H_C),
        ),
        compiler_params=pltpu.CompilerParams(
            dimension_semantics=("arbitrary",), vmem_limit_bytes=VMEM_LIMIT),
        name="in_projection",
    )(x_ctx, x_lat, mod, w_in_bf16)


def _decay_matrix(lg_f, lg_b, n):
    row = lax.broadcasted_iota(I32, (n, n), 0)
    col = lax.broadcasted_iota(I32, (n, n), 1)
    diff = (row - col).astype(F32)
    fwd = jnp.where(diff >= 0, jnp.exp(lg_f * jnp.maximum(diff, 0.0)), 0.0)
    bwd = jnp.where(diff <= 0, jnp.exp(lg_b * jnp.maximum(-diff, 0.0)), 0.0)
    return (fwd + bwd) * ATTN_SCALE


def _retention_readout(o, gate, g, b):
    mu = jnp.mean(o, -1, keepdims=True)
    oc = o - mu
    var = jnp.mean(oc * oc, -1, keepdims=True)
    on = oc * lax.rsqrt(var + GN_EPS) * g + b
    return on * _silu(gate.astype(F32))


def _softmax_attend(s, v, extra_logit=None):
    m = jnp.max(s, -1, keepdims=True)
    if extra_logit is not None:
        m = jnp.maximum(m, extra_logit)
    p = jnp.exp(s - m)
    den = jnp.sum(p, -1, keepdims=True)
    if extra_logit is not None:
        den = den + jnp.exp(extra_logit - m)
    return _dot(p.astype(BF16), v) / den


def _ctx_mixer_kernel(sink_ref, pa_ref, pb_ref, pc_ref, decf_ref, decb_ref, gng_ref, gnb_ref,
                      mix_ref, sf_ref, sb_ref):
    n = SEQ
    hd = HEAD_DIM
    pos = lax.broadcasted_iota(I32, (n, hd), 0).astype(F32)
    for h in range(H_A):
        c0 = h * hd
        q = pa_ref[:, c0:c0 + hd]
        k = pa_ref[:, W_A + c0:W_A + c0 + hd]
        v = pa_ref[:, 2 * W_A + c0:2 * W_A + c0 + hd]
        gate = pa_ref[:, 3 * W_A + c0:3 * W_A + c0 + hd]
        lg_f = _log_sigmoid(decf_ref[h])
        lg_b = _log_sigmoid(decb_ref[h])
        dmat = _decay_matrix(lg_f, lg_b, n)
        o = _dot((_dot_nt(q, k) * dmat).astype(BF16), v)
        kf = k.astype(F32)
        zf = jnp.exp(lg_f[:, 0:hd] * (n - 1.0 - pos)) * ATTN_SCALE
        zb = jnp.exp(lg_b[:, 0:hd] * pos) * ATTN_SCALE
        sf_ref[0, h] = _dot((kf * zf).T.astype(BF16), v)
        sb_ref[0, h] = _dot((kf * zb).T.astype(BF16), v)
        y = _retention_readout(o, gate, gng_ref[:, c0:c0 + hd], gnb_ref[:, c0:c0 + hd])
        mix_ref[:, c0:c0 + hd] = y.astype(BF16)
    for j in range(KV_B):
        k = pb_ref[:, W_B + j * hd:W_B + (j + 1) * hd]
        v = pb_ref[:, W_B + KV_W_B + j * hd:W_B + KV_W_B + (j + 1) * hd]
        for g in range(H_B // KV_B):
            hh = j * (H_B // KV_B) + g
            q = pb_ref[:, hh * hd:(hh + 1) * hd]
            o = _softmax_attend(_dot_nt(q, k) * ATTN_SCALE, v, sink_ref[hh])
            mix_ref[:, W_A + hh * hd:W_A + (hh + 1) * hd] = o.astype(BF16)
    for hh in range(H_C):
        q = pc_ref[:, hh * hd:(hh + 1) * hd]
        k = pc_ref[:, W_C + hh * hd:W_C + (hh + 1) * hd]
        v = pc_ref[:, 2 * W_C + hh * hd:2 * W_C + (hh + 1) * hd]
        o = _softmax_attend(_dot_nt(q, k) * ATTN_SCALE, v)
        mix_ref[:, W_A + W_B + hh * hd:W_A + W_B + (hh + 1) * hd] = o.astype(BF16)


def _ctx_mixers(pa, pb, pc, sink, decf, decb, gng, gnb):
    return pl.pallas_call(
        _ctx_mixer_kernel,
        out_shape=(
            jax.ShapeDtypeStruct((T_CTX, D_MODEL), BF16),
            jax.ShapeDtypeStruct((BATCH, H_A, HEAD_DIM, HEAD_DIM), F32),
            jax.ShapeDtypeStruct((BATCH, H_A, HEAD_DIM, HEAD_DIM), F32),
        ),
        grid=(BATCH,),
        in_specs=[
            pl.BlockSpec(memory_space=pltpu.SMEM),
            pl.BlockSpec((SEQ, PA_W), lambda b: (b, 0)),
            pl.BlockSpec((SEQ, PB_W), lambda b: (b, 0)),
            pl.BlockSpec((SEQ, PC_W), lambda b: (b, 0)),
            pl.BlockSpec((H_A, 1, SEQ), lambda b: (0, 0, 0)),
            pl.BlockSpec((H_A, 1, SEQ), lambda b: (0, 0, 0)),
            pl.BlockSpec((1, W_A), lambda b: (0, 0)),
            pl.BlockSpec((1, W_A), lambda b: (0, 0)),
        ],
        out_specs=(
            pl.BlockSpec((SEQ, D_MODEL), lambda b: (b, 0)),
            pl.BlockSpec((1, H_A, HEAD_DIM, HEAD_DIM), lambda b: (b, 0, 0, 0)),
            pl.BlockSpec((1, H_A, HEAD_DIM, HEAD_DIM), lambda b: (b, 0, 0, 0)),
        ),
        compiler_params=pltpu.CompilerParams(
            dimension_semantics=("arbitrary",), vmem_limit_bytes=VMEM_LIMIT),
        name="ctx_mixers",
    )(sink, pa, pb, pc, decf, decb, gng, gnb)


def _lat_ret_kernel(pa_ref, stf_ref, stb_ref, decf_ref, decb_ref, gng_ref, gnb_ref, ya_ref, acc_ref):
    c = RET_CHUNK
    hd = HEAD_DIM
    n_chunks = DEC_SEQ // c
    pos = lax.broadcasted_iota(I32, (c, hd), 0).astype(F32)
    for h in range(H_A):
        c0 = h * hd
        lg_f = _log_sigmoid(decf_ref[h])
        lg_b = _log_sigmoid(decb_ref[h])
        dmat = _decay_matrix(lg_f, lg_b, c)
        lf = lg_f[:, 0:hd]
        lb = lg_b[:, 0:hd]
        zf = jnp.exp(lf * (c - 1.0 - pos)) * ATTN_SCALE
        zb = jnp.exp(lb * pos) * ATTN_SCALE
        xf = jnp.exp(lf * (pos + 1.0))
        xb = jnp.exp(lb * (c - pos))
        gcf = jnp.exp(lf * float(c))
        gcb = jnp.exp(lb * float(c))
        g = gng_ref[:, c0:c0 + hd]
        b = gnb_ref[:, c0:c0 + hd]

        def load(i, off):
            rows = pl.ds(pl.multiple_of(i * c, c), c)
            return pa_ref[rows, off + c0:off + c0 + hd]

        def fwd(i, s):
            q, k, v = load(i, 0), load(i, W_A), load(i, 2 * W_A)
            o = _dot((_dot_nt(q, k) * dmat).astype(BF16), v)
            o = o + _dot((q.astype(F32) * xf).astype(BF16), s.astype(BF16))
            acc_ref[pl.ds(pl.multiple_of(i * c, c), c), c0:c0 + hd] = o
            return gcf * s + _dot((k.astype(F32) * zf).T.astype(BF16), v)

        lax.fori_loop(0, n_chunks, fwd, stf_ref[0, 0, h])

        def bwd(j, s):
            i = n_chunks - 1 - j
            rows = pl.ds(pl.multiple_of(i * c, c), c)
            q, k, v = load(i, 0), load(i, W_A), load(i, 2 * W_A)
            o = acc_ref[rows, c0:c0 + hd] + _dot((q.astype(F32) * xb).astype(BF16), s.astype(BF16))
            y = _retention_readout(o, load(i, 3 * W_A), g, b)
            ya_ref[rows, c0:c0 + hd] = y.astype(BF16)
            return gcb * s + _dot((k.astype(F32) * zb).T.astype(BF16), v)

        lax.fori_loop(0, n_chunks, bwd, stb_ref[0, 0, h])


def _lat_retention(pa, st_f, st_b, layer, decf, decb, gng, gnb):
    lat0 = T_CTX // DEC_SEQ
    st_spec = pl.BlockSpec((1, 1, H_A, HEAD_DIM, HEAD_DIM), lambda b: (b, layer, 0, 0, 0))
    return pl.pallas_call(
        _lat_ret_kernel,
        out_shape=jax.ShapeDtypeStruct((T_LAT, W_A), BF16),
        grid=(DEC_BATCH,),
        in_specs=[
            pl.BlockSpec((DEC_SEQ, PA_W), lambda b: (lat0 + b, 0)),
            st_spec, st_spec,
            pl.BlockSpec((H_A, 1, RET_CHUNK), lambda b: (0, 0, 0)),
            pl.BlockSpec((H_A, 1, RET_CHUNK), lambda b: (0, 0, 0)),
            pl.BlockSpec((1, W_A), lambda b: (0, 0)),
            pl.BlockSpec((1, W_A), lambda b: (0, 0)),
        ],
        out_specs=pl.BlockSpec((DEC_SEQ, W_A), lambda b: (b, 0)),
        scratch_shapes=[pltpu.VMEM((DEC_SEQ, W_A), F32)],
        compiler_params=pltpu.CompilerParams(
            dimension_semantics=("arbitrary",), vmem_limit_bytes=VMEM_LIMIT),
        name="lat_retention",
    )(pa, st_f, st_b, decf, decb, gng, gnb)


def _swap_halves_matrix(width):
    r = lax.broadcasted_iota(I32, (width, width), 0)
    c = lax.broadcasted_iota(I32, (width, width), 1)
    return jnp.where((r ^ (HEAD_DIM // 2)) == c, 1.0, 0.0).astype(BF16)


def _rope(x, cos, sin_signed, swap):
    return x.astype(F32) * cos + _dot(x, swap) * sin_signed


def _lat_win_kernel(sink_ref, pq_ref, pseq_ref, kctx_ref, vctx_ref, cos_ref, sin_ref, yb_ref, krope_ref):
    n = pl.program_id(1)
    hd = HEAD_DIM
    qb = WIN_QB
    n_blk = DEC_SEQ // qb
    group = H_B // KV_B
    swap = _swap_halves_matrix(LANES)

    @pl.when(n == 0)
    def _():
        k = pseq_ref[:, W_B:W_B + KV_W_B]
        krope_ref[...] = _rope(k, cos_ref[...], sin_ref[...], swap).astype(BF16)

    q_rows = pl.ds(pl.multiple_of(n * qb, qb), qb)
    cos_q = cos_ref[q_rows, :]
    sin_q = sin_ref[q_rows, :]
    qr = [_rope(pq_ref[:, p * LANES:(p + 1) * LANES], cos_q, sin_q, swap).astype(BF16)
          for p in range(W_B // LANES)]

    ws = jnp.clip(n - 1, 0, n_blk - 3) * qb
    k_rows = pl.ds(pl.multiple_of(ws, qb), 3 * qb)
    q_pos = n * qb + lax.broadcasted_iota(I32, (group * qb, 3 * qb), 0) % qb
    k_pos = ws + lax.broadcasted_iota(I32, (group * qb, 3 * qb), 1)
    valid = jnp.abs(k_pos - q_pos) <= WINDOW
    head_of_row = lax.broadcasted_iota(I32, (group * qb, 1), 0) // qb
    for j in range(KV_B):
        heads = [j * group + g for g in range(group)]
        qs = jnp.concatenate(
            [qr[hh // 2][:, (hh % 2) * hd:(hh % 2 + 1) * hd] for hh in heads], axis=0)
        kw = krope_ref[k_rows, j * hd:(j + 1) * hd]
        vw = pseq_ref[k_rows, W_B + KV_W_B + j * hd:W_B + KV_W_B + (j + 1) * hd]
        kc = kctx_ref[0, 0, j].astype(BF16)
        vc = vctx_ref[0, 0, j].astype(BF16)
        s_loc = jnp.where(valid, _dot_nt(qs, kw) * ATTN_SCALE, NEG_INF)
        s_ctx = _dot_nt(qs, kc) * ATTN_SCALE
        sink = jnp.zeros((group * qb, 1), F32)
        for g, hh in enumerate(heads):
            sink = jnp.where(head_of_row == g, sink_ref[hh], sink)
        m = jnp.maximum(jnp.maximum(jnp.max(s_loc, -1, keepdims=True),
                                    jnp.max(s_ctx, -1, keepdims=True)), sink)
        p_loc = jnp.exp(s_loc - m)
        p_ctx = jnp.exp(s_ctx - m)
        den = (jnp.sum(p_loc, -1, keepdims=True) + jnp.sum(p_ctx, -1, keepdims=True)
               + jnp.exp(sink - m))
        o = (_dot(p_loc.astype(BF16), vw) + _dot(p_ctx.astype(BF16), vc)) / den
        for g, hh in enumerate(heads):
            yb_ref[:, hh * hd:(hh + 1) * hd] = o[g * qb:(g + 1) * qb].astype(BF16)


def _lat_window_attn(pb, cache_k, cache_v, layer, sink, cos_t, sin_t):
    n_blk = DEC_SEQ // WIN_QB
    lat_blk0 = T_CTX // WIN_QB
    lat_seq0 = T_CTX // DEC_SEQ
    ctx_spec = pl.BlockSpec((1, 1, KV_B, PAST_LEN, HEAD_DIM), lambda b, n: (b, layer, 0, 0, 0))
    return pl.pallas_call(
        _lat_win_kernel,
        out_shape=jax.ShapeDtypeStruct((T_LAT, W_B), BF16),
        grid=(DEC_BATCH, n_blk),
        in_specs=[
            pl.BlockSpec(memory_space=pltpu.SMEM),
            pl.BlockSpec((WIN_QB, PB_W), lambda b, n: (lat_blk0 + b * n_blk + n, 0)),
            pl.BlockSpec((DEC_SEQ, PB_W), lambda b, n: (lat_seq0 + b, 0)),
            ctx_spec, ctx_spec,
            pl.BlockSpec((DEC_SEQ, LANES), lambda b, n: (0, 0)),
            pl.BlockSpec((DEC_SEQ, LANES), lambda b, n: (0, 0)),
        ],
        out_specs=pl.BlockSpec((WIN_QB, W_B), lambda b, n: (b * n_blk + n, 0)),
        scratch_shapes=[pltpu.VMEM((DEC_SEQ, KV_W_B), BF16)],
        compiler_params=pltpu.CompilerParams(
            dimension_semantics=("arbitrary", "arbitrary"), vmem_limit_bytes=VMEM_LIMIT),
        name="lat_window_attn",
    )(sink, pb, pb, cache_k, cache_v, cos_t, sin_t)


NA_Q = NA_TILE_ROWS * GRID_W
NA_K = NA_KEY_ROWS * GRID_W
NA_TILES = DEC_SEQ // NA_Q
LAT_ROWS = DEC_SEQ // GRID_W


def _na_window_start(tile):
    return jnp.clip(tile * NA_TILE_ROWS - NA_ROWS // 2, 0, LAT_ROWS - NA_KEY_ROWS)


def _lat_na_kernel(pq_ref, pseq_ref, kctx_ref, vctx_ref, bias_ref, yc_ref):
    t = pl.program_id(1)
    hd = HEAD_DIM
    k_rows = pl.ds(pl.multiple_of(_na_window_start(t) * GRID_W, GRID_W), NA_K)
    for hh in range(H_C):
        q = pq_ref[:, hh * hd:(hh + 1) * hd]
        kw = pseq_ref[k_rows, W_C + hh * hd:W_C + (hh + 1) * hd]
        vw = pseq_ref[k_rows, 2 * W_C + hh * hd:2 * W_C + (hh + 1) * hd]
        kc = kctx_ref[0, 0, hh].astype(BF16)
        vc = vctx_ref[0, 0, hh].astype(BF16)
        s_loc = _dot_nt(q, kw) * ATTN_SCALE + bias_ref[0, hh]
        s_ctx = _dot_nt(q, kc) * ATTN_SCALE
        m = jnp.maximum(jnp.max(s_loc, -1, keepdims=True), jnp.max(s_ctx, -1, keepdims=True))
        p_loc = jnp.exp(s_loc - m)
        p_ctx = jnp.exp(s_ctx - m)
        den = jnp.sum(p_loc, -1, keepdims=True) + jnp.sum(p_ctx, -1, keepdims=True)
        o = (_dot(p_loc.astype(BF16), vw) + _dot(p_ctx.astype(BF16), vc)) / den
        yc_ref[:, hh * hd:(hh + 1) * hd] = o.astype(BF16)


def _na_tile_type(t):
    return jnp.where(t == 0, 0, jnp.where(t == NA_TILES - 1, 2, 1))


def _lat_na_attn(pc, cache_k, cache_v, layer, maskbias):
    lat_tile0 = T_CTX // NA_Q
    lat_seq0 = T_CTX // DEC_SEQ
    ctx_spec = pl.BlockSpec((1, 1, H_C, PAST_LEN, HEAD_DIM), lambda b, t: (b, layer, 0, 0, 0))
    return pl.pallas_call(
        _lat_na_kernel,
        out_shape=jax.ShapeDtypeStruct((T_LAT, W_C), BF16),
        grid=(DEC_BATCH, NA_TILES),
        in_specs=[
            pl.BlockSpec((NA_Q, PC_W), lambda b, t: (lat_tile0 + b * NA_TILES + t, 0)),
            pl.BlockSpec((DEC_SEQ, PC_W), lambda b, t: (lat_seq0 + b, 0)),
            ctx_spec, ctx_spec,
            pl.BlockSpec((1, H_C, NA_Q, NA_K), lambda b, t: (_na_tile_type(t), 0, 0, 0)),
        ],
        out_specs=pl.BlockSpec((NA_Q, W_C), lambda b, t: (b * NA_TILES + t, 0)),
        compiler_params=pltpu.CompilerParams(
            dimension_semantics=("arbitrary", "arbitrary"), vmem_limit_bytes=VMEM_LIMIT),
        name="lat_na_attn",
    )(pc, pc, cache_k, cache_v, maskbias)


def _na_block_index():
    out = np.zeros((3, NA_TILE_ROWS, NA_KEY_ROWS), np.int32)
    for ty, tile in enumerate((0, 1, NA_TILES - 1)):
        r = tile * NA_TILE_ROWS
        ws = int(np.clip(r - NA_ROWS // 2, 0, LAT_ROWS - NA_KEY_ROWS))
        for qq in range(NA_TILE_ROWS):
            qr = r + qq
            r0 = int(np.clip(qr - NA_ROWS // 2, 0, LAT_ROWS - NA_ROWS))
            for kk in range(NA_KEY_ROWS):
                kr = ws + kk
                out[ty, qq, kk] = kr - qr + NA_ROWS - 1 if r0 <= kr < r0 + NA_ROWS else 2 * NA_ROWS - 1
    return out


def _na_maskbias(rpb):
    qc = np.arange(GRID_W)[:, None]
    kc = np.arange(GRID_W)[None, :]
    c0 = np.clip(qc - NA_COLS // 2, 0, GRID_W - NA_COLS)
    col_ok = (kc >= c0) & (kc < c0 + NA_COLS)
    ci = np.clip(kc - qc + NA_COLS - 1, 0, 2 * NA_COLS - 2)
    onehot = (ci[None] == np.arange(2 * NA_COLS - 1)[:, None, None]).astype(np.float32)
    cols = jnp.einsum("hab,bqk->haqk", rpb, jnp.asarray(onehot), precision=lax.Precision.HIGHEST)
    cols = jnp.where(jnp.asarray(col_ok)[None, None], cols, NEG_INF)
    cols = jnp.concatenate([cols, jnp.full((H_C, 1, GRID_W, GRID_W), NEG_INF, F32)], axis=1)
    block_index = _na_block_index()

    def assemble(cols_ref, out_ref):
        for ty in range(3):
            for qq in range(NA_TILE_ROWS):
                for kk in range(NA_KEY_ROWS):
                    out_ref[ty, 0, qq * GRID_W:(qq + 1) * GRID_W, kk * GRID_W:(kk + 1) * GRID_W] = (
                        cols_ref[0, int(block_index[ty, qq, kk])])

    return pl.pallas_call(
        assemble,
        out_shape=jax.ShapeDtypeStruct((3, H_C, NA_Q, NA_K), F32),
        grid=(H_C,),
        in_specs=[pl.BlockSpec((1, 2 * NA_ROWS, GRID_W, GRID_W), lambda h: (h, 0, 0, 0))],
        out_specs=pl.BlockSpec((3, 1, NA_Q, NA_K), lambda h: (0, h, 0, 0)),
        compiler_params=pltpu.CompilerParams(dimension_semantics=("arbitrary",)),
        name="na_bias_assemble",
    )(cols)


def _rope_tables():
    t = np.arange(DEC_SEQ)
    n_freq = HEAD_DIM // 4
    inv = (ROPE_BASE ** (-np.arange(n_freq, dtype=np.float32) / n_freq)).astype(np.float32)
    row = (t // GRID_W).astype(np.float32)[:, None] * inv
    col = (t % GRID_W).astype(np.float32)[:, None] * inv
    ang = np.concatenate([row, col], -1)
    cos, sin = np.cos(ang), np.sin(ang)
    cos_h = np.concatenate([cos, cos], -1)
    sin_h = np.concatenate([-sin, sin], -1)
    reps = LANES // HEAD_DIM
    return (jnp.asarray(np.tile(cos_h, (1, reps)), F32), jnp.asarray(np.tile(sin_h, (1, reps)), F32))


def _first_index_of(mask, iota, sentinel):
    return jnp.min(jnp.where(mask, iota, sentinel), axis=0, keepdims=True)


def _route(logits, b_col):
    n = logits.shape[1]
    scores = jax.nn.sigmoid(logits)
    sel = scores + b_col
    io_g = lax.broadcasted_iota(I32, (GROUP_SIZE, n), 0)
    gs_rows = []
    for g in range(N_GROUPS):
        s = sel[g * GROUP_SIZE:(g + 1) * GROUP_SIZE]
        m1 = jnp.max(s, axis=0, keepdims=True)
        i1 = _first_index_of(s == m1, io_g, GROUP_SIZE)
        m2 = jnp.max(jnp.where(io_g == i1, PICKED, s), axis=0, keepdims=True)
        gs_rows.append(m1 + m2)
    gs = jnp.concatenate(gs_rows, axis=0)
    io_n = lax.broadcasted_iota(I32, (N_GROUPS, n), 0)
    gsel = jnp.zeros((N_GROUPS, n), F32)
    for _ in range(TOPK_GROUPS):
        mg = jnp.max(gs, axis=0, keepdims=True)
        gi = _first_index_of(gs == mg, io_n, N_GROUPS)
        hit = io_n == gi
        gsel = jnp.where(hit, 1.0, gsel)
        gs = jnp.where(hit, PICKED, gs)
    cand = jnp.concatenate(
        [jnp.where(gsel[g:g + 1] > 0.5, sel[g * GROUP_SIZE:(g + 1) * GROUP_SIZE], NEG_INF)
         for g in range(N_GROUPS)], axis=0)
    io_e = lax.broadcasted_iota(I32, (N_EXPERTS, n), 0)
    picks, raw = [], []
    for _ in range(TOP_K):
        mv = jnp.max(cand, axis=0, keepdims=True)
        ei = _first_index_of(cand == mv, io_e, N_EXPERTS)
        hit = io_e == ei
        picks.append((hit, ei))
        raw.append(jnp.sum(jnp.where(hit, scores, 0.0), axis=0, keepdims=True))
        cand = jnp.where(hit, PICKED, cand)
    return picks, raw


def _post_mixer_kernel(xc_ref, xl_ref, mixc_ref, ya_ref, yb_ref, yc_ref, wout_ref, mod_ref, g_ref, b_ref,
                       wrh_ref, wrl_ref, br_ref,
                       x1_ref, h2_ref, eidx_ref, wsel_ref, rank_ref, cnt_ref):
    i = pl.program_id(0)
    tm = TM_TOK

    @pl.when(i == 0)
    def _():
        cnt_ref[...] = jnp.zeros_like(cnt_ref)

    ci = _cond_row(i, tm)
    gate1 = mod_ref[pl.ds(ci, 1), 2 * D_MODEL:3 * D_MODEL]
    sh2 = mod_ref[pl.ds(ci, 1), 3 * D_MODEL:4 * D_MODEL]
    sc2 = mod_ref[pl.ds(ci, 1), 4 * D_MODEL:5 * D_MODEL]
    mix_lat = jnp.concatenate([ya_ref[...], yb_ref[...], yc_ref[...]], axis=-1)
    mix = jnp.where(i < T_CTX // tm, mixc_ref[...], mix_lat)
    y = _dot(mix, wout_ref[...])
    x = jnp.where(i < T_CTX // tm, xc_ref[...], xl_ref[...])
    x1 = _layer_norm(ALPHA * x + gate1 * y, g_ref[...], b_ref[...])
    x1_ref[...] = x1
    h2 = x1 * (1.0 + sc2) + sh2
    h_hi = h2.astype(BF16)
    h2_ref[...] = h_hi
    h_lo = (h2 - h_hi.astype(F32)).astype(BF16)
    logits = (_dot_nt(wrh_ref[...], h_hi) + _dot_nt(wrh_ref[...], h_lo)
              + _dot_nt(wrl_ref[...], h_hi))
    routed = [_route(logits[:, g * LANES:(g + 1) * LANES], br_ref[...]) for g in range(tm // LANES)]
    multi_g = []
    for picks, _ in routed:
        m = jnp.zeros((N_EXPERTS, LANES), F32)
        for hit, _ in picks:
            m = m + jnp.where(hit, 1.0, 0.0)
        multi_g.append(m)
    multi = jnp.concatenate(multi_g, axis=1)
    before = (lax.broadcasted_iota(I32, (tm, tm), 0) < lax.broadcasted_iota(I32, (tm, tm), 1))
    cum = _dot(multi.astype(BF16), jnp.where(before, 1.0, 0.0).astype(BF16))
    pad = jnp.zeros((SUBLANES - TOP_K, LANES), F32)
    for g, (picks, raw) in enumerate(routed):
        lanes = slice(g * LANES, (g + 1) * LANES)
        total = raw[0]
        for r in raw[1:]:
            total = total + r
        scale = ROUTE_SCALE / total
        cum_g = cum[:, lanes]
        eidx_ref[:, lanes] = jnp.concatenate([ei for _, ei in picks] + [pad.astype(I32)], axis=0)
        wsel_ref[:, lanes] = jnp.concatenate([r * scale for r in raw] + [pad], axis=0)
        rank_ref[:, lanes] = jnp.concatenate(
            [jnp.sum(jnp.where(hit, cum_g, 0.0), axis=0, keepdims=True) for hit, _ in picks] + [pad],
            axis=0).astype(I32)
    tile_lane = lax.broadcasted_iota(I32, (N_EXPERTS, LANES), 1)
    cnt_ref[...] = jnp.where(tile_lane == i, jnp.sum(multi, axis=1, keepdims=True), cnt_ref[...])


def _post_mixer(x_ctx, x_lat, mix_c, ya, yb, yc, w_out_bf16, mod, ln_g, ln_b, wr_hi_t, wr_lo_t, b_router_col):
    n_ctx = T_CTX // TM_TOK
    ctx_map = lambda i: (jnp.minimum(i, n_ctx - 1), 0)
    lat_map = lambda i: (jnp.maximum(i - n_ctx, 0), 0)
    row_map = lambda i: (i, 0)
    const = lambda i: (0, 0)
    tok_map = lambda i: (0, i)
    return pl.pallas_call(
        _post_mixer_kernel,
        out_shape=(
            jax.ShapeDtypeStruct((T_ALL, D_MODEL), F32),
            jax.ShapeDtypeStruct((T_ALL, D_MODEL), BF16),
            jax.ShapeDtypeStruct((SUBLANES, T_ALL), I32),
            jax.ShapeDtypeStruct((SUBLANES, T_ALL), F32),
            jax.ShapeDtypeStruct((SUBLANES, T_ALL), I32),
            jax.ShapeDtypeStruct((N_EXPERTS, LANES), F32),
        ),
        grid=(N_TOK_TILES,),
        in_specs=[
            pl.BlockSpec((TM_TOK, D_MODEL), ctx_map),
            pl.BlockSpec((TM_TOK, D_MODEL), lat_map),
            pl.BlockSpec((TM_TOK, D_MODEL), ctx_map),
            pl.BlockSpec((TM_TOK, W_A), lat_map),
            pl.BlockSpec((TM_TOK, W_B), lat_map),
            pl.BlockSpec((TM_TOK, W_C), lat_map),
            pl.BlockSpec((D_MODEL, D_MODEL), const),
            pl.BlockSpec((N_COND, 6 * D_MODEL), const),
            pl.BlockSpec((1, D_MODEL), const),
            pl.BlockSpec((1, D_MODEL), const),
            pl.BlockSpec((N_EXPERTS, D_MODEL), const),
            pl.BlockSpec((N_EXPERTS, D_MODEL), const),
            pl.BlockSpec((N_EXPERTS, 1), const),
        ],
        out_specs=(
            pl.BlockSpec((TM_TOK, D_MODEL), row_map),
            pl.BlockSpec((TM_TOK, D_MODEL), row_map),
            pl.BlockSpec((SUBLANES, TM_TOK), tok_map),
            pl.BlockSpec((SUBLANES, TM_TOK), tok_map),
            pl.BlockSpec((SUBLANES, TM_TOK), tok_map),
            pl.BlockSpec((N_EXPERTS, LANES), const),
        ),
        compiler_params=pltpu.CompilerParams(
            dimension_semantics=("arbitrary",), vmem_limit_bytes=VMEM_LIMIT),
        name="post_mixer",
    )(x_ctx, x_lat, mix_c, ya, yb, yc, w_out_bf16, mod, ln_g, ln_b, wr_hi_t, wr_lo_t, b_router_col)


def _plan_kernel(eidx_ref, rank_ref, nmat_ref, lslot_ref, unit_ref, gend_ref, blk_ref):
    i = pl.program_id(0)
    units = jnp.floor((nmat_ref[...] + (UNIT - 1.0)) * (1.0 / UNIT))
    units_bf = units.astype(BF16)
    earlier_e = (lax.broadcasted_iota(I32, (N_EXPERTS, N_EXPERTS), 1)
                 < lax.broadcasted_iota(I32, (N_EXPERTS, N_EXPERTS), 0))
    tri_e = jnp.where(earlier_e, 1.0, 0.0).astype(BF16)
    earlier_t = (lax.broadcasted_iota(I32, (LANES, LANES), 0) < lax.broadcasted_iota(I32, (LANES, LANES), 1))
    tri_t = jnp.where(earlier_t, 1.0, 0.0).astype(BF16)
    local_off = _dot(tri_e, units_bf)
    tile_off = _dot(units_bf, tri_t)
    per_expert = jnp.sum(units, axis=1, keepdims=True)
    blocks = jnp.floor((per_expert + (UNITS_PER_BLOCK - 1.0)) * (1.0 / UNITS_PER_BLOCK))
    blocks_l = jnp.broadcast_to(blocks, (N_EXPERTS, LANES))
    start_blk = _dot(tri_e, blocks_l.astype(BF16))
    end_blk = start_blk + blocks_l
    gend_ref[...] = (end_blk * BM).astype(I32)
    blk_id = lax.broadcasted_iota(I32, (N_EXPERTS, BLK_LANES), 1).astype(F32)
    owner = jnp.sum(jnp.where(end_blk[:, 0:1] <= blk_id, 1.0, 0.0), axis=0, keepdims=True)
    blk_ref[...] = jnp.minimum(owner, N_EXPERTS - 1.0).astype(I32)

    this_tile = lax.broadcasted_iota(I32, (N_EXPERTS, LANES), 1) == i

    def column(a):
        return jnp.sum(jnp.where(this_tile, a, 0.0), axis=1, keepdims=True)

    lo, n_u = column(local_off), column(units)
    base_unit = start_blk[:, 0:1] * UNITS_PER_BLOCK + column(tile_off) - lo
    u = lax.broadcasted_iota(I32, (N_EXPERTS, MAX_UNITS), 1).astype(F32)
    inside = jnp.where(u >= lo, jnp.where(u < lo + n_u, 1.0, 0.0), 0.0)
    dst_unit = jnp.sum(inside * (base_unit + u), axis=0, keepdims=True)
    used = jnp.sum(inside, axis=0, keepdims=True) > 0.5
    spare = (SPARE_UNIT0 + (i % 2) * MAX_UNITS).astype(F32) + u[0:1, :]
    unit_ref[0] = jnp.where(used, dst_unit, spare).astype(I32)

    io_e = lax.broadcasted_iota(I32, (N_EXPERTS, TM_TOK), 0)
    rows = []
    for k in range(TOP_K):
        hit = io_e == eidx_ref[k:k + 1, :]
        seg = jnp.sum(jnp.where(hit, lo * UNIT, 0.0), axis=0, keepdims=True)
        rows.append(seg.astype(I32) + rank_ref[k:k + 1, :])
    rows.append(jnp.full((SUBLANES - TOP_K, TM_TOK), -1, I32))
    lslot_ref[...] = jnp.concatenate(rows, axis=0)


def _slot_plan(eidx, rank, nmat):
    tok_map = lambda i: (0, i)
    const = lambda i: (0, 0)
    lslot, unit_tab, gend, blk = pl.pallas_call(
        _plan_kernel,
        out_shape=(
            jax.ShapeDtypeStruct((SUBLANES, T_ALL), I32),
            jax.ShapeDtypeStruct((N_TOK_TILES, 1, MAX_UNITS), I32),
            jax.ShapeDtypeStruct((N_EXPERTS, LANES), I32),
            jax.ShapeDtypeStruct((1, BLK_LANES), I32),
        ),
        grid=(N_TOK_TILES,),
        in_specs=[
            pl.BlockSpec((SUBLANES, TM_TOK), tok_map),
            pl.BlockSpec((SUBLANES, TM_TOK), tok_map),
            pl.BlockSpec((N_EXPERTS, LANES), const),
        ],
        out_specs=(
            pl.BlockSpec((SUBLANES, TM_TOK), tok_map),
            pl.BlockSpec((1, 1, MAX_UNITS), lambda i: (i, 0, 0)),
            pl.BlockSpec((N_EXPERTS, LANES), const),
            pl.BlockSpec((1, BLK_LANES), const),
        ),
        compiler_params=pltpu.CompilerParams(dimension_semantics=("arbitrary",)),
        name="slot_plan",
    )(eidx, rank, nmat)
    gend = gend[:, 0]
    n_used = gend[N_EXPERTS - 1:] // BM
    return lslot, unit_tab.reshape(N_TOK_TILES * MAX_UNITS), gend, blk[0, :N_BLOCKS_ALL], n_used


PACK_W = D_MODEL // 2
HI_HALF = -65536


def _pack_pairs(x):
    lo = lax.bitcast_convert_type(x[:, 0:PACK_W], I32)
    hi = lax.bitcast_convert_type(x[:, PACK_W:D_MODEL], I32)
    return lax.shift_right_logical(lo, 16) | (hi & HI_HALF)


def _unpack_pairs(u):
    lo = lax.bitcast_convert_type(lax.shift_left(u, 16), F32).astype(BF16)
    hi = lax.bitcast_convert_type(u & HI_HALF, F32).astype(BF16)
    return lo, hi


def _unit_rows(unit):
    row = unit * UNIT
    return pl.ds(row if isinstance(unit, int) else pl.multiple_of(row, UNIT), UNIT)


def _unit_copy(src, src_unit, dst, dst_unit, sem):
    return pltpu.make_async_copy(src.at[_unit_rows(src_unit)], dst.at[_unit_rows(dst_unit)], sem)


def _dispatch_kernel(gend_ref, tab_ref, h2_ref, lslot_ref, xs_hbm, zero_ref, local_ref, sem_zero, sem_rows):
    i = pl.program_id(0)
    buf = i % 2

    def drain(b):
        pltpu.make_async_copy(local_ref.at[b], xs_hbm.at[pl.ds(0, LOCAL_ROWS)], sem_rows.at[b]).wait()

    def has_rows(e):
        return gend_ref[e] > jnp.where(e == 0, 0, gend_ref[jnp.maximum(e - 1, 0)])

    def zero_copy(e):
        return pltpu.make_async_copy(
            zero_ref, xs_hbm.at[pl.ds(pl.multiple_of(gend_ref[e] - BM, BM), BM)], sem_zero)

    @pl.when(i == 0)
    def _():
        zero_ref[...] = jnp.zeros_like(zero_ref)

        def start(e, c):
            @pl.when(has_rows(e))
            def _():
                zero_copy(e).start()
            return c

        def wait(e, c):
            @pl.when(has_rows(e))
            def _():
                zero_copy(e).wait()
            return c

        def tail_copy(blk):
            return pltpu.make_async_copy(
                zero_ref, xs_hbm.at[pl.ds(pl.multiple_of(blk * BM, BM), BM)], sem_zero)

        def start_tail(blk, c):
            tail_copy(blk).start()
            return c

        def wait_tail(blk, c):
            tail_copy(blk).wait()
            return c

        n_used = gend_ref[N_EXPERTS - 1] // BM
        lax.fori_loop(0, N_EXPERTS, start, 0)
        lax.fori_loop(n_used, N_BLOCKS_ALL, start_tail, 0)
        lax.fori_loop(0, N_EXPERTS, wait, 0)
        lax.fori_loop(n_used, N_BLOCKS_ALL, wait_tail, 0)

    @pl.when(i >= 2)
    def _():
        drain(buf)

    h2 = h2_ref[...]
    units_per_chunk = PERM_CHUNK // UNIT
    local = local_ref.at[buf]
    for c in range(LOCAL_ROWS // PERM_CHUNK):
        slot = c * PERM_CHUNK + lax.broadcasted_iota(I32, (PERM_CHUNK, TM_TOK), 0)
        p = jnp.zeros((PERM_CHUNK, TM_TOK), F32)
        for k in range(TOP_K):
            p = jnp.where(slot == lslot_ref[k:k + 1, :], 1.0, p)
        local[c * PERM_CHUNK:(c + 1) * PERM_CHUNK, :] = _pack_pairs(_dot(p.astype(BF16), h2))
        for u in range(c * units_per_chunk, (c + 1) * units_per_chunk):
            _unit_copy(local, u, xs_hbm, tab_ref[i * MAX_UNITS + u], sem_rows.at[buf]).start()

    @pl.when(i == N_TOK_TILES - 1)
    def _():
        drain(1 - buf)
        drain(buf)


def _dispatch(h2, lslot, unit_tab, gend):
    return pl.pallas_call(
        _dispatch_kernel,
        out_shape=jax.ShapeDtypeStruct((N_SLOTS, PACK_W), I32),
        grid_spec=pltpu.PrefetchScalarGridSpec(
            num_scalar_prefetch=2,
            grid=(N_TOK_TILES,),
            in_specs=[
                pl.BlockSpec((TM_TOK, D_MODEL), lambda i, ge, tab: (i, 0)),
                pl.BlockSpec((SUBLANES, TM_TOK), lambda i, ge, tab: (0, i)),
            ],
            out_specs=pl.BlockSpec(memory_space=pl.ANY),
            scratch_shapes=[
                pltpu.VMEM((BM, PACK_W), I32),
                pltpu.VMEM((2, LOCAL_ROWS, PACK_W), I32),
                pltpu.SemaphoreType.DMA,
                pltpu.SemaphoreType.DMA((2,)),
            ],
        ),
        compiler_params=pltpu.CompilerParams(
            dimension_semantics=("arbitrary",), vmem_limit_bytes=VMEM_LIMIT),
        name="moe_dispatch",
    )(gend, unit_tab, h2, lslot)


def _expert_kernel(blk_e_ref, n_used_ref, xs_ref, wgu_ref, wdown_ref, ys_ref, wgu_bf, wdown_bf):
    i = pl.program_id(0)
    live = i < n_used_ref[0]
    new_expert = jnp.logical_or(i == 0, blk_e_ref[i] != blk_e_ref[jnp.maximum(i - 1, 0)])

    @pl.when(jnp.logical_and(live, new_expert))
    def _():
        wgu_bf[...] = wgu_ref[0, 0].astype(BF16)
        wdown_bf[...] = wdown_ref[0, 0].astype(BF16)

    @pl.when(live)
    def _():
        for r in range(EXPERT_SPLIT):
            rows = slice(r * (BM // EXPERT_SPLIT), (r + 1) * (BM // EXPERT_SPLIT))
            x_lo, x_hi = _unpack_pairs(xs_ref[rows, :])
            gu = _dot(x_lo, wgu_bf[0:PACK_W, :]) + _dot(x_hi, wgu_bf[PACK_W:D_MODEL, :])
            act = _silu(gu[:, 0:D_EXPERT]) * gu[:, D_EXPERT:2 * D_EXPERT]
            y = _dot(act.astype(BF16), wdown_bf[...])
            ys_ref[rows, :] = _pack_pairs(y.astype(BF16).astype(F32))

    @pl.when(i >= n_used_ref[0])
    def _():
        ys_ref[...] = jnp.zeros_like(ys_ref)


def _experts(xs, w_gu, w_down, layer, blk_e, n_used):
    def blk(i, be, nu):
        return jnp.minimum(i, nu[0] - 1)

    return pl.pallas_call(
        _expert_kernel,
        out_shape=jax.ShapeDtypeStruct((N_SLOTS, PACK_W), I32),
        grid_spec=pltpu.PrefetchScalarGridSpec(
            num_scalar_prefetch=2,
            grid=(N_BLOCKS_ALL,),
            in_specs=[
                pl.BlockSpec((BM, PACK_W), lambda i, be, nu: (blk(i, be, nu), 0)),
                pl.BlockSpec((1, 1, D_MODEL, 2 * D_EXPERT),
                             lambda i, be, nu: (layer, be[blk(i, be, nu)], 0, 0)),
                pl.BlockSpec((1, 1, D_EXPERT, D_MODEL),
                             lambda i, be, nu: (layer, be[blk(i, be, nu)], 0, 0)),
            ],
            out_specs=pl.BlockSpec((BM, PACK_W), lambda i, be, nu: (i, 0)),
            scratch_shapes=[
                pltpu.VMEM((D_MODEL, 2 * D_EXPERT), BF16),
                pltpu.VMEM((D_EXPERT, D_MODEL), BF16),
            ],
        ),
        compiler_params=pltpu.CompilerParams(
            dimension_semantics=("arbitrary",), vmem_limit_bytes=VMEM_LIMIT),
        name="moe_experts",
    )(blk_e, n_used, xs, w_gu, w_down)


def _combine_kernel(tab_ref, x1_ref, h2_ref, lslot_ref, wsel_ref, ys_hbm, wsgu_ref, wsdown_ref, mod_ref,
                    g_ref, b_ref, outc_ref, outl_ref, local_ref, sel_ref, ylo_ref, yhi_ref, sem_rows):
    i = pl.program_id(0)
    tm = TM_TOK
    buf = i % 2

    def fetch(tile, b):
        def body(u, c):
            _unit_copy(ys_hbm, tab_ref[tile * MAX_UNITS + u], local_ref.at[b], u, sem_rows.at[b]).start()
            return c

        lax.fori_loop(0, MAX_UNITS, body, 0, unroll=8)

    @pl.when(i == 0)
    def _():
        fetch(0, 0)

    @pl.when(i + 1 < N_TOK_TILES)
    def _():
        fetch(i + 1, 1 - buf)

    sgu = _dot(h2_ref[...], wsgu_ref[...])
    act = _silu(sgu[:, 0:D_SHARED]) * sgu[:, D_SHARED:2 * D_SHARED]
    f = _dot(act.astype(BF16), wsdown_ref[...])

    w = wsel_ref[...]
    ls = lslot_ref[...]
    for c in range(LOCAL_ROWS // PERM_CHUNK):
        slot = c * PERM_CHUNK + lax.broadcasted_iota(I32, (tm, PERM_CHUNK), 1)
        sel = jnp.zeros((tm, PERM_CHUNK), F32)
        for k in range(TOP_K):
            sel = jnp.where(slot == ls[:, k:k + 1], w[:, k:k + 1], sel)
        sel_ref[:, c * PERM_CHUNK:(c + 1) * PERM_CHUNK] = sel.astype(BF16)

    local = local_ref.at[buf]
    pltpu.make_async_copy(ys_hbm.at[pl.ds(0, LOCAL_ROWS)], local, sem_rows.at[buf]).wait()
    for c in range(LOCAL_ROWS // PERM_CHUNK):
        rows = slice(c * PERM_CHUNK, (c + 1) * PERM_CHUNK)
        ylo_ref[rows, :], yhi_ref[rows, :] = _unpack_pairs(local[rows, :])
    sel = sel_ref[...]
    f = f + jnp.concatenate([_dot(sel, ylo_ref[...]), _dot(sel, yhi_ref[...])], axis=-1)
    ci = _cond_row(i, tm)
    gate2 = mod_ref[pl.ds(ci, 1), 5 * D_MODEL:6 * D_MODEL]
    out = _layer_norm(ALPHA * x1_ref[...] + gate2 * f, g_ref[...], b_ref[...])

    @pl.when(i < T_CTX // tm)
    def _():
        outc_ref[...] = out

    @pl.when(i >= T_CTX // tm)
    def _():
        outl_ref[...] = out


def _combine(x1, h2, lslot_rows, wsel_rows, unit_tab, ys, w_sgu_bf16, w_sdown_bf16, mod, ln_g, ln_b):
    n_ctx = T_CTX // TM_TOK
    row_map = lambda i, tab: (i, 0)
    const = lambda i, tab: (0, 0)
    return pl.pallas_call(
        _combine_kernel,
        out_shape=(jax.ShapeDtypeStruct((T_CTX, D_MODEL), F32),
                   jax.ShapeDtypeStruct((T_LAT, D_MODEL), F32)),
        grid_spec=pltpu.PrefetchScalarGridSpec(
            num_scalar_prefetch=1,
            grid=(N_TOK_TILES,),
            in_specs=[
                pl.BlockSpec((TM_TOK, D_MODEL), row_map),
                pl.BlockSpec((TM_TOK, D_MODEL), row_map),
                pl.BlockSpec((TM_TOK, SUBLANES), row_map),
                pl.BlockSpec((TM_TOK, SUBLANES), row_map),
                pl.BlockSpec(memory_space=pl.ANY),
                pl.BlockSpec((D_MODEL, 2 * D_SHARED), const),
                pl.BlockSpec((D_SHARED, D_MODEL), const),
                pl.BlockSpec((N_COND, 6 * D_MODEL), const),
                pl.BlockSpec((1, D_MODEL), const),
                pl.BlockSpec((1, D_MODEL), const),
            ],
            out_specs=(pl.BlockSpec((TM_TOK, D_MODEL), lambda i, tab: (jnp.minimum(i, n_ctx - 1), 0)),
                       pl.BlockSpec((TM_TOK, D_MODEL), lambda i, tab: (jnp.maximum(i - n_ctx, 0), 0))),
            scratch_shapes=[
                pltpu.VMEM((2, LOCAL_ROWS, PACK_W), I32),
                pltpu.VMEM((TM_TOK, LOCAL_ROWS), BF16),
                pltpu.VMEM((LOCAL_ROWS, PACK_W), BF16),
                pltpu.VMEM((LOCAL_ROWS, PACK_W), BF16),
                pltpu.SemaphoreType.DMA((2,)),
            ],
        ),
        compiler_params=pltpu.CompilerParams(
            dimension_semantics=("arbitrary",), vmem_limit_bytes=VMEM_LIMIT),
        name="moe_combine",
    )(unit_tab, x1, h2, lslot_rows, wsel_rows, ys, w_sgu_bf16, w_sdown_bf16, mod, ln_g, ln_b)


def _lane_rows(v, width):
    return jnp.broadcast_to(v.astype(F32)[:, None, None], (v.shape[0], 1, width))


def kernel(x_prompt, x_sample, state_ret_fwd, state_ret_bwd, cache_win_k, cache_win_v, cache_na_k, cache_na_v, c, c_ctx, w_in, w_out, ret_decay_fwd, ret_decay_bwd, ret_gn_g, ret_gn_b, win_sink, na_rpb, w_mod, b_mod, ln1_g, ln1_b, ln2_g, ln2_b, w_router, b_router, w_expert_gu, w_expert_down, w_shared_gu, w_shared_down):
    cond = jnp.concatenate(
        [c_ctx[None, :], c, jnp.zeros((N_COND - 1 - DEC_BATCH, D_MODEL), F32)], axis=0)
    mod_all = _modulation(cond, w_mod, b_mod)
    cos_t, sin_t = _rope_tables()

    x_ctx = x_prompt.reshape(T_CTX, D_MODEL)
    x_lat = x_sample.reshape(T_LAT, D_MODEL)
    sf_l, sb_l, cache_l = [], [], []
    for l in range(DEPTH):
        mod = mod_all[l]
        pa, pb, pc, *caches = _in_projection(x_ctx, x_lat, mod, w_in[l].astype(BF16))
        cache_l.append(caches)
        decf_s, decb_s = _lane_rows(ret_decay_fwd[l], SEQ), _lane_rows(ret_decay_bwd[l], SEQ)
        gng, gnb = ret_gn_g[l][None, :], ret_gn_b[l][None, :]
        mix_c, st_f, st_b = _ctx_mixers(pa, pb, pc, win_sink[l], decf_s, decb_s, gng, gnb)
        sf_l.append(st_f)
        sb_l.append(st_b)
        ya = _lat_retention(pa, state_ret_fwd, state_ret_bwd, l,
                            _lane_rows(ret_decay_fwd[l], RET_CHUNK), _lane_rows(ret_decay_bwd[l], RET_CHUNK),
                            gng, gnb)
        yb = _lat_window_attn(pb, cache_win_k, cache_win_v, l, win_sink[l], cos_t, sin_t)
        yc = _lat_na_attn(pc, cache_na_k, cache_na_v, l, _na_maskbias(na_rpb[l]))

        wr_t = w_router[l].T
        wr_hi = wr_t.astype(BF16)
        wr_lo = (wr_t - wr_hi.astype(F32)).astype(BF16)
        x1, h2, eidx, wsel, rank, counts = _post_mixer(
            x_ctx, x_lat, mix_c, ya, yb, yc, w_out[l].astype(BF16), mod, ln1_g[l][None, :],
            ln1_b[l][None, :], wr_hi, wr_lo, b_router[l][:, None])
        lslot, unit_tab, gend, blk_e, n_used = _slot_plan(eidx, rank, counts)
        xs = _dispatch(h2, lslot, unit_tab, gend)
        ys = _experts(xs, w_expert_gu, w_expert_down, l, blk_e, n_used)
        x_ctx, x_lat = _combine(x1, h2, lslot.T, wsel.T, unit_tab, ys, w_shared_gu[l].astype(BF16),
                                w_shared_down[l].astype(BF16), mod, ln2_g[l][None, :], ln2_b[l][None, :])

    y_prompt = x_ctx.reshape(BATCH, SEQ, D_MODEL)
    y_sample = x_lat.reshape(DEC_BATCH, DEC_SEQ, D_MODEL)
    new_sf = jnp.stack(sf_l, axis=1)
    new_sb = jnp.stack(sb_l, axis=1)

    new_caches = [jnp.stack([layer[j] for layer in cache_l], axis=1) for j in range(4)]
    return (y_prompt, y_sample, new_sf, new_sb, *new_caches)
```

```python
import functools

import numpy as np
import jax
import jax.numpy as jnp
from jax import lax
from jax.experimental import pallas as pl
from jax.experimental.pallas import tpu as pltpu

F32 = jnp.float32
BF16 = jnp.bfloat16
I32 = jnp.int32

D_MODEL = 1024
BATCH = 32
SEQ = 256
DEPTH = 2
DEC_BATCH = 4
DEC_SEQ = 2048
PAST_LEN = 256
GRID_W = 64
HEAD_DIM = 64
ATTN_SCALE = HEAD_DIM ** -0.5
H_A = 4
W_A = H_A * HEAD_DIM
GN_EPS = 1e-5
H_B = 6
KV_B = 2
W_B = H_B * HEAD_DIM
KV_W_B = KV_B * HEAD_DIM
WINDOW = 128
ROPE_BASE = 10000.0
H_C = 6
W_C = H_C * HEAD_DIM
NA_ROWS = 8
NA_COLS = 16
IN_WIDTH = 4 * W_A + W_B + 2 * KV_W_B + 3 * W_C
N_EXPERTS = 64
TOP_K = 6
N_GROUPS = 8
GROUP_SIZE = N_EXPERTS // N_GROUPS
TOPK_GROUPS = 4
D_EXPERT = 256
D_SHARED = 256
ROUTE_SCALE = 2.5
ALPHA = (2 * DEPTH) ** 0.25
LN_EPS = 1e-5
NEG_INF = -1e30
PICKED = -3e38

T_CTX = BATCH * SEQ
T_LAT = DEC_BATCH * DEC_SEQ
T_ALL = T_CTX + T_LAT
N_COND = 8

PA_W = 4 * W_A
PB_W = W_B + 2 * KV_W_B
PC_W = 3 * W_C

LANES = 128
SUBLANES = 8
VMEM_LIMIT = 56 * 1024 * 1024

TM_PROJ = 512
TM_TOK = 256
RET_CHUNK = 256
WIN_QB = 128
NA_TILE_ROWS = 4
NA_KEY_ROWS = 11
BM = 512
UNIT = SUBLANES
UNITS_PER_BLOCK = BM // UNIT
PERM_CHUNK = 256
SEL_ROWS = 16
N_TOK_TILES = T_ALL // TM_TOK
LOCAL_ROWS = -(-(TM_TOK * TOP_K + N_EXPERTS * (UNIT - 1)) // PERM_CHUNK) * PERM_CHUNK
MAX_UNITS = LOCAL_ROWS // UNIT
N_ASSIGN = T_ALL * TOP_K
N_BLOCKS = -(-(N_ASSIGN + N_TOK_TILES * N_EXPERTS * (UNIT - 1) + N_EXPERTS * (BM - 1)) // BM)
SPARE_BLOCKS = 2 * -(-LOCAL_ROWS // BM)
SPARE_UNIT0 = N_BLOCKS * UNITS_PER_BLOCK
N_BLOCKS_ALL = N_BLOCKS + SPARE_BLOCKS
N_SLOTS = N_BLOCKS_ALL * BM
BLK_LANES = -(-N_BLOCKS_ALL // LANES) * LANES


def _dot(a, b):
    return jnp.dot(a, b, preferred_element_type=F32)


def _dot_nt(a, b):
    return lax.dot_general(a, b, (((1,), (1,)), ((), ())), preferred_element_type=F32)


def _silu(x):
    return x * jax.nn.sigmoid(x)


def _log_sigmoid(x):
    return jnp.minimum(x, 0.0) - jnp.log(1.0 + jnp.exp(-jnp.abs(x)))


def _cond_row(tile, tile_rows):
    n_ctx = T_CTX // tile_rows
    per_lat = DEC_SEQ // tile_rows
    return jnp.where(tile < n_ctx, 0, 1 + (tile - n_ctx) // per_lat)


def _layer_norm(x, g, b):
    mu = jnp.mean(x, -1, keepdims=True)
    xc = x - mu
    var = jnp.mean(xc * xc, -1, keepdims=True)
    return xc * lax.rsqrt(var + LN_EPS) * g + b


MOD_TN = 1536


def _mod_kernel(cond_ref, w_ref, b_ref, o_ref):
    s = _silu(cond_ref[...])
    s_hi = s.astype(BF16)
    s_lo = (s - s_hi.astype(F32)).astype(BF16)
    w = w_ref[0]
    w_hi = w.astype(BF16)
    w_lo = (w - w_hi.astype(F32)).astype(BF16)
    o_ref[0] = _dot(s_hi, w_hi) + _dot(s_lo, w_hi) + _dot(s_hi, w_lo) + b_ref[0]


def _modulation(cond, w_mod, b_mod):
    n_out = 6 * D_MODEL
    return pl.pallas_call(
        _mod_kernel,
        out_shape=jax.ShapeDtypeStruct((DEPTH, N_COND, n_out), F32),
        grid=(DEPTH, n_out // MOD_TN),
        in_specs=[
            pl.BlockSpec((N_COND, D_MODEL), lambda l, j: (0, 0)),
            pl.BlockSpec((1, D_MODEL, MOD_TN), lambda l, j: (l, 0, j)),
            pl.BlockSpec((1, 1, MOD_TN), lambda l, j: (l, 0, j)),
        ],
        out_specs=pl.BlockSpec((1, N_COND, MOD_TN), lambda l, j: (l, 0, j)),
        compiler_params=pltpu.CompilerParams(
            dimension_semantics=("arbitrary", "arbitrary"), vmem_limit_bytes=VMEM_LIMIT),
        name="modulation",
    )(cond, w_mod, b_mod.reshape(DEPTH, 1, n_out))


SEQ_PER_PROJ = TM_PROJ // SEQ


def _inproj_kernel(n_prev, xc_ref, xl_ref, mod_ref, w_ref, *refs):
    prev_refs = refs[:4] if n_prev else ()
    pa_ref, pb_ref, pc_ref, wk_ref, wv_ref, nk_ref, nv_ref = refs[len(prev_refs):]
    i = pl.program_id(0)
    ci = _cond_row(i, TM_PROJ)
    sh = mod_ref[pl.ds(ci, 1), 0:D_MODEL]
    sc = mod_ref[pl.ds(ci, 1), D_MODEL:2 * D_MODEL]
    x = jnp.where(i < T_CTX // TM_PROJ, xc_ref[...], xl_ref[...])
    h = x * (1.0 + sc) + sh
    p = _dot(h.astype(BF16), w_ref[...])
    pa_ref[...] = p[:, 0:PA_W].astype(BF16)
    pb_ref[...] = p[:, PA_W:PA_W + PB_W].astype(BF16)
    pc_ref[...] = p[:, PA_W + PB_W:IN_WIDTH].astype(BF16)

    @pl.when(i < T_CTX // TM_PROJ)
    def _():
        targets = ((wk_ref, PA_W + W_B, KV_B), (wv_ref, PA_W + W_B + KV_W_B, KV_B),
                   (nk_ref, PA_W + PB_W + W_C, H_C), (nv_ref, PA_W + PB_W + 2 * W_C, H_C))
        for j, (ref, col0, n_heads) in enumerate(targets):
            if n_prev:
                ref[:, 0:n_prev] = prev_refs[j][...]
            for s in range(SEQ_PER_PROJ):
                for hh in range(n_heads):
                    ref[s, n_prev, hh] = p[s * SEQ:(s + 1) * SEQ,
                                           col0 + hh * HEAD_DIM:col0 + (hh + 1) * HEAD_DIM]


def _in_projection(x_ctx, x_lat, mod, w_in_bf16, earlier):
    n_ctx_tiles = T_CTX // TM_PROJ
    n_prev = earlier[0].shape[1] if earlier else 0

    def cache_spec(n_layers, n_heads):
        return pl.BlockSpec((SEQ_PER_PROJ, n_layers, n_heads, SEQ, HEAD_DIM),
                            lambda i: (jnp.minimum(i, n_ctx_tiles - 1), 0, 0, 0, 0))

    cache_heads = (KV_B, KV_B, H_C, H_C)
    return pl.pallas_call(
        functools.partial(_inproj_kernel, n_prev),
        out_shape=(
            jax.ShapeDtypeStruct((T_ALL, PA_W), BF16),
            jax.ShapeDtypeStruct((T_ALL, PB_W), BF16),
            jax.ShapeDtypeStruct((T_ALL, PC_W), BF16),
        ) + tuple(jax.ShapeDtypeStruct((BATCH, n_prev + 1, nh, SEQ, HEAD_DIM), F32) for nh in cache_heads),
        grid=(T_ALL // TM_PROJ,),
        in_specs=[
            pl.BlockSpec((TM_PROJ, D_MODEL), lambda i: (jnp.minimum(i, n_ctx_tiles - 1), 0)),
            pl.BlockSpec((TM_PROJ, D_MODEL), lambda i: (jnp.maximum(i - n_ctx_tiles, 0), 0)),
            pl.BlockSpec((N_COND, 6 * D_MODEL), lambda i: (0, 0)),
            pl.BlockSpec((D_MODEL, IN_WIDTH), lambda i: (0, 0)),
        ] + [cache_spec(n_prev, nh) for nh in cache_heads if n_prev],
        out_specs=(
            pl.BlockSpec((TM_PROJ, PA_W), lambda i: (i, 0)),
            pl.BlockSpec((TM_PROJ, PB_W), lambda i: (i, 0)),
            pl.BlockSpec((TM_PROJ, PC_W), lambda i: (i, 0)),
        ) + tuple(cache_spec(n_prev + 1, nh) for nh in cache_heads),
        compiler_params=pltpu.CompilerParams(
            dimension_semantics=("arbitrary",), vmem_limit_bytes=VMEM_LIMIT),
        name="in_projection",
    )(x_ctx, x_lat, mod, w_in_bf16, *earlier)


def _decay_matrix(lg_f, lg_b, n):
    row = lax.broadcasted_iota(I32, (n, n), 0)
    col = lax.broadcasted_iota(I32, (n, n), 1)
    diff = (row - col).astype(F32)
    fwd = jnp.where(diff >= 0, jnp.exp(lg_f * jnp.maximum(diff, 0.0)), 0.0)
    bwd = jnp.where(diff <= 0, jnp.exp(lg_b * jnp.maximum(-diff, 0.0)), 0.0)
    return (fwd + bwd) * ATTN_SCALE


def _retention_readout(o, gate, g, b):
    mu = jnp.mean(o, -1, keepdims=True)
    oc = o - mu
    var = jnp.mean(oc * oc, -1, keepdims=True)
    on = oc * lax.rsqrt(var + GN_EPS) * g + b
    return on * _silu(gate.astype(F32))


def _softmax_attend(s, v, extra_logit=None):
    m = jnp.max(s, -1, keepdims=True)
    if extra_logit is not None:
        m = jnp.maximum(m, extra_logit)
    p = jnp.exp(s - m)
    den = jnp.sum(p, -1, keepdims=True)
    if extra_logit is not None:
        den = den + jnp.exp(extra_logit - m)
    return _dot(p.astype(BF16), v) / den


def _ctx_mixer_kernel(sink_ref, pa_ref, pb_ref, pc_ref, decf_ref, decb_ref, gng_ref, gnb_ref,
                      mix_ref, sf_ref, sb_ref):
    n = SEQ
    hd = HEAD_DIM
    pos = lax.broadcasted_iota(I32, (n, hd), 0).astype(F32)
    for h in range(H_A):
        c0 = h * hd
        q = pa_ref[:, c0:c0 + hd]
        k = pa_ref[:, W_A + c0:W_A + c0 + hd]
        v = pa_ref[:, 2 * W_A + c0:2 * W_A + c0 + hd]
        gate = pa_ref[:, 3 * W_A + c0:3 * W_A + c0 + hd]
        lg_f = _log_sigmoid(decf_ref[h])
        lg_b = _log_sigmoid(decb_ref[h])
        dmat = _decay_matrix(lg_f, lg_b, n)
        o = _dot((_dot_nt(q, k) * dmat).astype(BF16), v)
        kf = k.astype(F32)
        zf = jnp.exp(lg_f[:, 0:hd] * (n - 1.0 - pos)) * ATTN_SCALE
        zb = jnp.exp(lg_b[:, 0:hd] * pos) * ATTN_SCALE
        sf_ref[0, h] = _dot((kf * zf).T.astype(BF16), v)
        sb_ref[0, h] = _dot((kf * zb).T.astype(BF16), v)
        y = _retention_readout(o, gate, gng_ref[:, c0:c0 + hd], gnb_ref[:, c0:c0 + hd])
        mix_ref[:, c0:c0 + hd] = y.astype(BF16)
    for j in range(KV_B):
        k = pb_ref[:, W_B + j * hd:W_B + (j + 1) * hd]
        v = pb_ref[:, W_B + KV_W_B + j * hd:W_B + KV_W_B + (j + 1) * hd]
        for g in range(H_B // KV_B):
            hh = j * (H_B // KV_B) + g
            q = pb_ref[:, hh * hd:(hh + 1) * hd]
            o = _softmax_attend(_dot_nt(q, k) * ATTN_SCALE, v, sink_ref[hh])
            mix_ref[:, W_A + hh * hd:W_A + (hh + 1) * hd] = o.astype(BF16)
    for hh in range(H_C):
        q = pc_ref[:, hh * hd:(hh + 1) * hd]
        k = pc_ref[:, W_C + hh * hd:W_C + (hh + 1) * hd]
        v = pc_ref[:, 2 * W_C + hh * hd:2 * W_C + (hh + 1) * hd]
        o = _softmax_attend(_dot_nt(q, k) * ATTN_SCALE, v)
        mix_ref[:, W_A + W_B + hh * hd:W_A + W_B + (hh + 1) * hd] = o.astype(BF16)


def _ctx_mixers(pa, pb, pc, sink, decf, decb, gng, gnb):
    return pl.pallas_call(
        _ctx_mixer_kernel,
        out_shape=(
            jax.ShapeDtypeStruct((T_CTX, D_MODEL), BF16),
            jax.ShapeDtypeStruct((BATCH, H_A, HEAD_DIM, HEAD_DIM), F32),
            jax.ShapeDtypeStruct((BATCH, H_A, HEAD_DIM, HEAD_DIM), F32),
        ),
        grid=(BATCH,),
        in_specs=[
            pl.BlockSpec(memory_space=pltpu.SMEM),
            pl.BlockSpec((SEQ, PA_W), lambda b: (b, 0)),
            pl.BlockSpec((SEQ, PB_W), lambda b: (b, 0)),
            pl.BlockSpec((SEQ, PC_W), lambda b: (b, 0)),
            pl.BlockSpec((H_A, 1, SEQ), lambda b: (0, 0, 0)),
            pl.BlockSpec((H_A, 1, SEQ), lambda b: (0, 0, 0)),
            pl.BlockSpec((1, W_A), lambda b: (0, 0)),
            pl.BlockSpec((1, W_A), lambda b: (0, 0)),
        ],
        out_specs=(
            pl.BlockSpec((SEQ, D_MODEL), lambda b: (b, 0)),
            pl.BlockSpec((1, H_A, HEAD_DIM, HEAD_DIM), lambda b: (b, 0, 0, 0)),
            pl.BlockSpec((1, H_A, HEAD_DIM, HEAD_DIM), lambda b: (b, 0, 0, 0)),
        ),
        compiler_params=pltpu.CompilerParams(
            dimension_semantics=("arbitrary",), vmem_limit_bytes=VMEM_LIMIT),
        name="ctx_mixers",
    )(sink, pa, pb, pc, decf, decb, gng, gnb)


def _lat_ret_kernel(pa_ref, stf_ref, stb_ref, decf_ref, decb_ref, gng_ref, gnb_ref, ya_ref, acc_ref):
    c = RET_CHUNK
    hd = HEAD_DIM
    n_chunks = DEC_SEQ // c
    pos = lax.broadcasted_iota(I32, (c, hd), 0).astype(F32)
    for h in range(H_A):
        c0 = h * hd
        lg_f = _log_sigmoid(decf_ref[h])
        lg_b = _log_sigmoid(decb_ref[h])
        dmat = _decay_matrix(lg_f, lg_b, c)
        lf = lg_f[:, 0:hd]
        lb = lg_b[:, 0:hd]
        zf = jnp.exp(lf * (c - 1.0 - pos)) * ATTN_SCALE
        zb = jnp.exp(lb * pos) * ATTN_SCALE
        xf = jnp.exp(lf * (pos + 1.0))
        xb = jnp.exp(lb * (c - pos))
        gcf = jnp.exp(lf * float(c))
        gcb = jnp.exp(lb * float(c))
        g = gng_ref[:, c0:c0 + hd]
        b = gnb_ref[:, c0:c0 + hd]

        def load(i, off):
            rows = pl.ds(pl.multiple_of(i * c, c), c)
            return pa_ref[rows, off + c0:off + c0 + hd]

        def fwd(i, s):
            q, k, v = load(i, 0), load(i, W_A), load(i, 2 * W_A)
            o = _dot((_dot_nt(q, k) * dmat).astype(BF16), v)
            o = o + _dot((q.astype(F32) * xf).astype(BF16), s.astype(BF16))
            acc_ref[pl.ds(pl.multiple_of(i * c, c), c), c0:c0 + hd] = o
            return gcf * s + _dot((k.astype(F32) * zf).T.astype(BF16), v)

        lax.fori_loop(0, n_chunks, fwd, stf_ref[0, 0, h])

        def bwd(j, s):
            i = n_chunks - 1 - j
            rows = pl.ds(pl.multiple_of(i * c, c), c)
            q, k, v = load(i, 0), load(i, W_A), load(i, 2 * W_A)
            o = acc_ref[rows, c0:c0 + hd] + _dot((q.astype(F32) * xb).astype(BF16), s.astype(BF16))
            y = _retention_readout(o, load(i, 3 * W_A), g, b)
            ya_ref[rows, c0:c0 + hd] = y.astype(BF16)
            return gcb * s + _dot((k.astype(F32) * zb).T.astype(BF16), v)

        lax.fori_loop(0, n_chunks, bwd, stb_ref[0, 0, h])


def _lat_retention(pa, st_f, st_b, layer, decf, decb, gng, gnb):
    lat0 = T_CTX // DEC_SEQ
    st_spec = pl.BlockSpec((1, 1, H_A, HEAD_DIM, HEAD_DIM), lambda b: (b, layer, 0, 0, 0))
    return pl.pallas_call(
        _lat_ret_kernel,
        out_shape=jax.ShapeDtypeStruct((T_LAT, W_A), BF16),
        grid=(DEC_BATCH,),
        in_specs=[
            pl.BlockSpec((DEC_SEQ, PA_W), lambda b: (lat0 + b, 0)),
            st_spec, st_spec,
            pl.BlockSpec((H_A, 1, RET_CHUNK), lambda b: (0, 0, 0)),
            pl.BlockSpec((H_A, 1, RET_CHUNK), lambda b: (0, 0, 0)),
            pl.BlockSpec((1, W_A), lambda b: (0, 0)),
            pl.BlockSpec((1, W_A), lambda b: (0, 0)),
        ],
        out_specs=pl.BlockSpec((DEC_SEQ, W_A), lambda b: (b, 0)),
        scratch_shapes=[pltpu.VMEM((DEC_SEQ, W_A), F32)],
        compiler_params=pltpu.CompilerParams(
            dimension_semantics=("arbitrary",), vmem_limit_bytes=VMEM_LIMIT),
        name="lat_retention",
    )(pa, st_f, st_b, decf, decb, gng, gnb)


def _swap_halves_matrix(width):
    r = lax.broadcasted_iota(I32, (width, width), 0)
    c = lax.broadcasted_iota(I32, (width, width), 1)
    return jnp.where((r ^ (HEAD_DIM // 2)) == c, 1.0, 0.0).astype(BF16)


def _rope(x, cos, sin_signed, swap):
    return x.astype(F32) * cos + _dot(x, swap) * sin_signed


def _lat_win_kernel(sink_ref, pq_ref, pseq_ref, kctx_ref, vctx_ref, cos_ref, sin_ref, yb_ref, krope_ref):
    n = pl.program_id(1)
    hd = HEAD_DIM
    qb = WIN_QB
    n_blk = DEC_SEQ // qb
    group = H_B // KV_B
    swap = _swap_halves_matrix(LANES)

    @pl.when(n == 0)
    def _():
        k = pseq_ref[:, W_B:W_B + KV_W_B]
        krope_ref[...] = _rope(k, cos_ref[...], sin_ref[...], swap).astype(BF16)

    q_rows = pl.ds(pl.multiple_of(n * qb, qb), qb)
    cos_q = cos_ref[q_rows, :]
    sin_q = sin_ref[q_rows, :]
    qr = [_rope(pq_ref[:, p * LANES:(p + 1) * LANES], cos_q, sin_q, swap).astype(BF16)
          for p in range(W_B // LANES)]

    ws = jnp.clip(n - 1, 0, n_blk - 3) * qb
    k_rows = pl.ds(pl.multiple_of(ws, qb), 3 * qb)
    q_pos = n * qb + lax.broadcasted_iota(I32, (group * qb, 3 * qb), 0) % qb
    k_pos = ws + lax.broadcasted_iota(I32, (group * qb, 3 * qb), 1)
    valid = jnp.abs(k_pos - q_pos) <= WINDOW
    head_of_row = lax.broadcasted_iota(I32, (group * qb, 1), 0) // qb
    for j in range(KV_B):
        heads = [j * group + g for g in range(group)]
        qs = jnp.concatenate(
            [qr[hh // 2][:, (hh % 2) * hd:(hh % 2 + 1) * hd] for hh in heads], axis=0)
        kw = krope_ref[k_rows, j * hd:(j + 1) * hd]
        vw = pseq_ref[k_rows, W_B + KV_W_B + j * hd:W_B + KV_W_B + (j + 1) * hd]
        kc = kctx_ref[0, 0, j].astype(BF16)
        vc = vctx_ref[0, 0, j].astype(BF16)
        s_loc = jnp.where(valid, _dot_nt(qs, kw) * ATTN_SCALE, NEG_INF)
        s_ctx = _dot_nt(qs, kc) * ATTN_SCALE
        sink = jnp.zeros((group * qb, 1), F32)
        for g, hh in enumerate(heads):
            sink = jnp.where(head_of_row == g, sink_ref[hh], sink)
        m = jnp.maximum(jnp.maximum(jnp.max(s_loc, -1, keepdims=True),
                                    jnp.max(s_ctx, -1, keepdims=True)), sink)
        p_loc = jnp.exp(s_loc - m)
        p_ctx = jnp.exp(s_ctx - m)
        den = (jnp.sum(p_loc, -1, keepdims=True) + jnp.sum(p_ctx, -1, keepdims=True)
               + jnp.exp(sink - m))
        o = (_dot(p_loc.astype(BF16), vw) + _dot(p_ctx.astype(BF16), vc)) / den
        for g, hh in enumerate(heads):
            yb_ref[:, hh * hd:(hh + 1) * hd] = o[g * qb:(g + 1) * qb].astype(BF16)


def _lat_window_attn(pb, cache_k, cache_v, layer, sink, cos_t, sin_t):
    n_blk = DEC_SEQ // WIN_QB
    lat_blk0 = T_CTX // WIN_QB
    lat_seq0 = T_CTX // DEC_SEQ
    ctx_spec = pl.BlockSpec((1, 1, KV_B, PAST_LEN, HEAD_DIM), lambda b, n: (b, layer, 0, 0, 0))
    return pl.pallas_call(
        _lat_win_kernel,
        out_shape=jax.ShapeDtypeStruct((T_LAT, W_B), BF16),
        grid=(DEC_BATCH, n_blk),
        in_specs=[
            pl.BlockSpec(memory_space=pltpu.SMEM),
            pl.BlockSpec((WIN_QB, PB_W), lambda b, n: (lat_blk0 + b * n_blk + n, 0)),
            pl.BlockSpec((DEC_SEQ, PB_W), lambda b, n: (lat_seq0 + b, 0)),
            ctx_spec, ctx_spec,
            pl.BlockSpec((DEC_SEQ, LANES), lambda b, n: (0, 0)),
            pl.BlockSpec((DEC_SEQ, LANES), lambda b, n: (0, 0)),
        ],
        out_specs=pl.BlockSpec((WIN_QB, W_B), lambda b, n: (b * n_blk + n, 0)),
        scratch_shapes=[pltpu.VMEM((DEC_SEQ, KV_W_B), BF16)],
        compiler_params=pltpu.CompilerParams(
            dimension_semantics=("arbitrary", "arbitrary"), vmem_limit_bytes=VMEM_LIMIT),
        name="lat_window_attn",
    )(sink, pb, pb, cache_k, cache_v, cos_t, sin_t)


NA_Q = NA_TILE_ROWS * GRID_W
NA_K = NA_KEY_ROWS * GRID_W
NA_TILES = DEC_SEQ // NA_Q
LAT_ROWS = DEC_SEQ // GRID_W


def _na_window_start(tile):
    return jnp.clip(tile * NA_TILE_ROWS - NA_ROWS // 2, 0, LAT_ROWS - NA_KEY_ROWS)


def _lat_na_kernel(pq_ref, pseq_ref, kctx_ref, vctx_ref, bias_ref, yc_ref):
    t = pl.program_id(1)
    hd = HEAD_DIM
    k_rows = pl.ds(pl.multiple_of(_na_window_start(t) * GRID_W, GRID_W), NA_K)
    for hh in range(H_C):
        q = pq_ref[:, hh * hd:(hh + 1) * hd]
        kw = pseq_ref[k_rows, W_C + hh * hd:W_C + (hh + 1) * hd]
        vw = pseq_ref[k_rows, 2 * W_C + hh * hd:2 * W_C + (hh + 1) * hd]
        kc = kctx_ref[0, 0, hh].astype(BF16)
        vc = vctx_ref[0, 0, hh].astype(BF16)
        s_loc = _dot_nt(q, kw) * ATTN_SCALE + bias_ref[0, hh]
        s_ctx = _dot_nt(q, kc) * ATTN_SCALE
        m = jnp.maximum(jnp.max(s_loc, -1, keepdims=True), jnp.max(s_ctx, -1, keepdims=True))
        p_loc = jnp.exp(s_loc - m)
        p_ctx = jnp.exp(s_ctx - m)
        den = jnp.sum(p_loc, -1, keepdims=True) + jnp.sum(p_ctx, -1, keepdims=True)
        o = (_dot(p_loc.astype(BF16), vw) + _dot(p_ctx.astype(BF16), vc)) / den
        yc_ref[:, hh * hd:(hh + 1) * hd] = o.astype(BF16)


def _na_tile_type(t):
    return jnp.where(t == 0, 0, jnp.where(t == NA_TILES - 1, 2, 1))


def _lat_na_attn(pc, cache_k, cache_v, layer, maskbias):
    lat_tile0 = T_CTX // NA_Q
    lat_seq0 = T_CTX // DEC_SEQ
    ctx_spec = pl.BlockSpec((1, 1, H_C, PAST_LEN, HEAD_DIM), lambda b, t: (b, layer, 0, 0, 0))
    return pl.pallas_call(
        _lat_na_kernel,
        out_shape=jax.ShapeDtypeStruct((T_LAT, W_C), BF16),
        grid=(DEC_BATCH, NA_TILES),
        in_specs=[
            pl.BlockSpec((NA_Q, PC_W), lambda b, t: (lat_tile0 + b * NA_TILES + t, 0)),
            pl.BlockSpec((DEC_SEQ, PC_W), lambda b, t: (lat_seq0 + b, 0)),
            ctx_spec, ctx_spec,
            pl.BlockSpec((1, H_C, NA_Q, NA_K), lambda b, t: (_na_tile_type(t), 0, 0, 0)),
        ],
        out_specs=pl.BlockSpec((NA_Q, W_C), lambda b, t: (b * NA_TILES + t, 0)),
        compiler_params=pltpu.CompilerParams(
            dimension_semantics=("arbitrary", "arbitrary"), vmem_limit_bytes=VMEM_LIMIT),
        name="lat_na_attn",
    )(pc, pc, cache_k, cache_v, maskbias)


def _na_block_index():
    out = np.zeros((3, NA_TILE_ROWS, NA_KEY_ROWS), np.int32)
    for ty, tile in enumerate((0, 1, NA_TILES - 1)):
        r = tile * NA_TILE_ROWS
        ws = int(np.clip(r - NA_ROWS // 2, 0, LAT_ROWS - NA_KEY_ROWS))
        for qq in range(NA_TILE_ROWS):
            qr = r + qq
            r0 = int(np.clip(qr - NA_ROWS // 2, 0, LAT_ROWS - NA_ROWS))
            for kk in range(NA_KEY_ROWS):
                kr = ws + kk
                out[ty, qq, kk] = kr - qr + NA_ROWS - 1 if r0 <= kr < r0 + NA_ROWS else 2 * NA_ROWS - 1
    return out


def _na_maskbias(rpb):
    qc = np.arange(GRID_W)[:, None]
    kc = np.arange(GRID_W)[None, :]
    c0 = np.clip(qc - NA_COLS // 2, 0, GRID_W - NA_COLS)
    col_ok = (kc >= c0) & (kc < c0 + NA_COLS)
    ci = np.clip(kc - qc + NA_COLS - 1, 0, 2 * NA_COLS - 2)
    onehot = (ci[None] == np.arange(2 * NA_COLS - 1)[:, None, None]).astype(np.float32)
    cols = jnp.einsum("hab,bqk->haqk", rpb, jnp.asarray(onehot), precision=lax.Precision.HIGHEST)
    cols = jnp.where(jnp.asarray(col_ok)[None, None], cols, NEG_INF)
    cols = jnp.concatenate([cols, jnp.full((H_C, 1, GRID_W, GRID_W), NEG_INF, F32)], axis=1)
    block_index = _na_block_index()

    def assemble(cols_ref, out_ref):
        for ty in range(3):
            for qq in range(NA_TILE_ROWS):
                for kk in range(NA_KEY_ROWS):
                    out_ref[ty, 0, qq * GRID_W:(qq + 1) * GRID_W, kk * GRID_W:(kk + 1) * GRID_W] = (
                        cols_ref[0, int(block_index[ty, qq, kk])])

    return pl.pallas_call(
        assemble,
        out_shape=jax.ShapeDtypeStruct((3, H_C, NA_Q, NA_K), F32),
        grid=(H_C,),
        in_specs=[pl.BlockSpec((1, 2 * NA_ROWS, GRID_W, GRID_W), lambda h: (h, 0, 0, 0))],
        out_specs=pl.BlockSpec((3, 1, NA_Q, NA_K), lambda h: (0, h, 0, 0)),
        compiler_params=pltpu.CompilerParams(dimension_semantics=("arbitrary",)),
        name="na_bias_assemble",
    )(cols)


def _rope_tables():
    t = np.arange(DEC_SEQ)
    n_freq = HEAD_DIM // 4
    inv = (ROPE_BASE ** (-np.arange(n_freq, dtype=np.float32) / n_freq)).astype(np.float32)
    row = (t // GRID_W).astype(np.float32)[:, None] * inv
    col = (t % GRID_W).astype(np.float32)[:, None] * inv
    ang = np.concatenate([row, col], -1)
    cos, sin = np.cos(ang), np.sin(ang)
    cos_h = np.concatenate([cos, cos], -1)
    sin_h = np.concatenate([-sin, sin], -1)
    reps = LANES // HEAD_DIM
    return (jnp.asarray(np.tile(cos_h, (1, reps)), F32), jnp.asarray(np.tile(sin_h, (1, reps)), F32))


def _first_index_of(mask, iota, sentinel):
    return jnp.min(jnp.where(mask, iota, sentinel), axis=0, keepdims=True)


def _route(logits, b_col):
    n = logits.shape[1]
    scores = jax.nn.sigmoid(logits)
    sel = scores + b_col
    io_g = lax.broadcasted_iota(I32, (GROUP_SIZE, n), 0)
    gs_rows = []
    for g in range(N_GROUPS):
        s = sel[g * GROUP_SIZE:(g + 1) * GROUP_SIZE]
        m1 = jnp.max(s, axis=0, keepdims=True)
        i1 = _first_index_of(s == m1, io_g, GROUP_SIZE)
        m2 = jnp.max(jnp.where(io_g == i1, PICKED, s), axis=0, keepdims=True)
        gs_rows.append(m1 + m2)
    gs = jnp.concatenate(gs_rows, axis=0)
    io_n = lax.broadcasted_iota(I32, (N_GROUPS, n), 0)
    gsel = jnp.zeros((N_GROUPS, n), F32)
    for _ in range(TOPK_GROUPS):
        mg = jnp.max(gs, axis=0, keepdims=True)
        gi = _first_index_of(gs == mg, io_n, N_GROUPS)
        hit = io_n == gi
        gsel = jnp.where(hit, 1.0, gsel)
        gs = jnp.where(hit, PICKED, gs)
    cand = jnp.concatenate(
        [jnp.where(gsel[g:g + 1] > 0.5, sel[g * GROUP_SIZE:(g + 1) * GROUP_SIZE], NEG_INF)
         for g in range(N_GROUPS)], axis=0)
    io_e = lax.broadcasted_iota(I32, (N_EXPERTS, n), 0)
    picks, raw = [], []
    for _ in range(TOP_K):
        mv = jnp.max(cand, axis=0, keepdims=True)
        ei = _first_index_of(cand == mv, io_e, N_EXPERTS)
        hit = io_e == ei
        picks.append((hit, ei))
        raw.append(jnp.sum(jnp.where(hit, scores, 0.0), axis=0, keepdims=True))
        cand = jnp.where(hit, PICKED, cand)
    return picks, raw


def _post_mixer_kernel(xc_ref, xl_ref, mixc_ref, ya_ref, yb_ref, yc_ref, wout_ref, mod_ref, g_ref, b_ref,
                       wrh_ref, wrl_ref, br_ref,
                       x1_ref, h2_ref, eidx_ref, wsel_ref, rank_ref, cnt_ref):
    i = pl.program_id(0)
    tm = TM_TOK

    @pl.when(i == 0)
    def _():
        cnt_ref[...] = jnp.zeros_like(cnt_ref)

    ci = _cond_row(i, tm)
    gate1 = mod_ref[pl.ds(ci, 1), 2 * D_MODEL:3 * D_MODEL]
    sh2 = mod_ref[pl.ds(ci, 1), 3 * D_MODEL:4 * D_MODEL]
    sc2 = mod_ref[pl.ds(ci, 1), 4 * D_MODEL:5 * D_MODEL]
    mix_lat = jnp.concatenate([ya_ref[...], yb_ref[...], yc_ref[...]], axis=-1)
    mix = jnp.where(i < T_CTX // tm, mixc_ref[...], mix_lat)
    y = _dot(mix, wout_ref[...])
    x = jnp.where(i < T_CTX // tm, xc_ref[...], xl_ref[...])
    x1 = _layer_norm(ALPHA * x + gate1 * y, g_ref[...], b_ref[...])
    x1_ref[...] = x1
    h2 = x1 * (1.0 + sc2) + sh2
    h_hi = h2.astype(BF16)
    h2_ref[...] = h_hi
    h_lo = (h2 - h_hi.astype(F32)).astype(BF16)
    logits = (_dot_nt(wrh_ref[...], h_hi) + _dot_nt(wrh_ref[...], h_lo)
              + _dot_nt(wrl_ref[...], h_hi))
    routed = [_route(logits[:, g * LANES:(g + 1) * LANES], br_ref[...]) for g in range(tm // LANES)]
    multi_g = []
    for picks, _ in routed:
        m = jnp.zeros((N_EXPERTS, LANES), F32)
        for hit, _ in picks:
            m = m + jnp.where(hit, 1.0, 0.0)
        multi_g.append(m)
    multi = jnp.concatenate(multi_g, axis=1)
    before = (lax.broadcasted_iota(I32, (tm, tm), 0) < lax.broadcasted_iota(I32, (tm, tm), 1))
    cum = _dot(multi.astype(BF16), jnp.where(before, 1.0, 0.0).astype(BF16))
    pad = jnp.zeros((SUBLANES - TOP_K, LANES), F32)
    for g, (picks, raw) in enumerate(routed):
        lanes = slice(g * LANES, (g + 1) * LANES)
        total = raw[0]
        for r in raw[1:]:
            total = total + r
        scale = ROUTE_SCALE / total
        cum_g = cum[:, lanes]
        eidx_ref[:, lanes] = jnp.concatenate([ei for _, ei in picks] + [pad.astype(I32)], axis=0)
        wsel_ref[:, lanes] = jnp.concatenate([r * scale for r in raw] + [pad], axis=0)
        rank_ref[:, lanes] = jnp.concatenate(
            [jnp.sum(jnp.where(hit, cum_g, 0.0), axis=0, keepdims=True) for hit, _ in picks] + [pad],
            axis=0).astype(I32)
    tile_lane = lax.broadcasted_iota(I32, (N_EXPERTS, LANES), 1)
    cnt_ref[...] = jnp.where(tile_lane == i, jnp.sum(multi, axis=1, keepdims=True), cnt_ref[...])


def _post_mixer(x_ctx, x_lat, mix_c, ya, yb, yc, w_out_bf16, mod, ln_g, ln_b, wr_hi_t, wr_lo_t, b_router_col):
    n_ctx = T_CTX // TM_TOK
    ctx_map = lambda i: (jnp.minimum(i, n_ctx - 1), 0)
    lat_map = lambda i: (jnp.maximum(i - n_ctx, 0), 0)
    row_map = lambda i: (i, 0)
    const = lambda i: (0, 0)
    tok_map = lambda i: (0, i)
    return pl.pallas_call(
        _post_mixer_kernel,
        out_shape=(
            jax.ShapeDtypeStruct((T_ALL, D_MODEL), F32),
            jax.ShapeDtypeStruct((T_ALL, D_MODEL), BF16),
            jax.ShapeDtypeStruct((SUBLANES, T_ALL), I32),
            jax.ShapeDtypeStruct((SUBLANES, T_ALL), F32),
            jax.ShapeDtypeStruct((SUBLANES, T_ALL), I32),
            jax.ShapeDtypeStruct((N_EXPERTS, LANES), F32),
        ),
        grid=(N_TOK_TILES,),
        in_specs=[
            pl.BlockSpec((TM_TOK, D_MODEL), ctx_map),
            pl.BlockSpec((TM_TOK, D_MODEL), lat_map),
            pl.BlockSpec((TM_TOK, D_MODEL), ctx_map),
            pl.BlockSpec((TM_TOK, W_A), lat_map),
            pl.BlockSpec((TM_TOK, W_B), lat_map),
            pl.BlockSpec((TM_TOK, W_C), lat_map),
            pl.BlockSpec((D_MODEL, D_MODEL), const),
            pl.BlockSpec((N_COND, 6 * D_MODEL), const),
            pl.BlockSpec((1, D_MODEL), const),
            pl.BlockSpec((1, D_MODEL), const),
            pl.BlockSpec((N_EXPERTS, D_MODEL), const),
            pl.BlockSpec((N_EXPERTS, D_MODEL), const),
            pl.BlockSpec((N_EXPERTS, 1), const),
        ],
        out_specs=(
            pl.BlockSpec((TM_TOK, D_MODEL), row_map),
            pl.BlockSpec((TM_TOK, D_MODEL), row_map),
            pl.BlockSpec((SUBLANES, TM_TOK), tok_map),
            pl.BlockSpec((SUBLANES, TM_TOK), tok_map),
            pl.BlockSpec((SUBLANES, TM_TOK), tok_map),
            pl.BlockSpec((N_EXPERTS, LANES), const),
        ),
        compiler_params=pltpu.CompilerParams(
            dimension_semantics=("arbitrary",), vmem_limit_bytes=VMEM_LIMIT),
        name="post_mixer",
    )(x_ctx, x_lat, mix_c, ya, yb, yc, w_out_bf16, mod, ln_g, ln_b, wr_hi_t, wr_lo_t, b_router_col)


def _plan_kernel(eidx_ref, rank_ref, nmat_ref, lslot_ref, unit_ref, gend_ref, blk_ref):
    i = pl.program_id(0)
    units = jnp.floor((nmat_ref[...] + (UNIT - 1.0)) * (1.0 / UNIT))
    units_bf = units.astype(BF16)
    earlier_e = (lax.broadcasted_iota(I32, (N_EXPERTS, N_EXPERTS), 1)
                 < lax.broadcasted_iota(I32, (N_EXPERTS, N_EXPERTS), 0))
    tri_e = jnp.where(earlier_e, 1.0, 0.0).astype(BF16)
    earlier_t = (lax.broadcasted_iota(I32, (LANES, LANES), 0) < lax.broadcasted_iota(I32, (LANES, LANES), 1))
    tri_t = jnp.where(earlier_t, 1.0, 0.0).astype(BF16)
    local_off = _dot(tri_e, units_bf)
    tile_off = _dot(units_bf, tri_t)
    per_expert = jnp.sum(units, axis=1, keepdims=True)
    blocks = jnp.floor((per_expert + (UNITS_PER_BLOCK - 1.0)) * (1.0 / UNITS_PER_BLOCK))
    blocks_l = jnp.broadcast_to(blocks, (N_EXPERTS, LANES))
    start_blk = _dot(tri_e, blocks_l.astype(BF16))
    end_blk = start_blk + blocks_l
    gend_ref[...] = (end_blk * BM).astype(I32)
    blk_id = lax.broadcasted_iota(I32, (N_EXPERTS, BLK_LANES), 1).astype(F32)
    owner = jnp.sum(jnp.where(end_blk[:, 0:1] <= blk_id, 1.0, 0.0), axis=0, keepdims=True)
    blk_ref[...] = jnp.minimum(owner, N_EXPERTS - 1.0).astype(I32)

    this_tile = lax.broadcasted_iota(I32, (N_EXPERTS, LANES), 1) == i

    def column(a):
        return jnp.sum(jnp.where(this_tile, a, 0.0), axis=1, keepdims=True)

    lo, n_u = column(local_off), column(units)
    base_unit = start_blk[:, 0:1] * UNITS_PER_BLOCK + column(tile_off) - lo
    u = lax.broadcasted_iota(I32, (N_EXPERTS, MAX_UNITS), 1).astype(F32)
    inside = jnp.where(u >= lo, jnp.where(u < lo + n_u, 1.0, 0.0), 0.0)
    dst_unit = jnp.sum(inside * (base_unit + u), axis=0, keepdims=True)
    used = jnp.sum(inside, axis=0, keepdims=True) > 0.5
    spare = (SPARE_UNIT0 + (i % 2) * MAX_UNITS).astype(F32) + u[0:1, :]
    unit_ref[0] = jnp.where(used, dst_unit, spare).astype(I32)

    io_e = lax.broadcasted_iota(I32, (N_EXPERTS, TM_TOK), 0)
    rows = []
    for k in range(TOP_K):
        hit = io_e == eidx_ref[k:k + 1, :]
        seg = jnp.sum(jnp.where(hit, lo * UNIT, 0.0), axis=0, keepdims=True)
        rows.append(seg.astype(I32) + rank_ref[k:k + 1, :])
    rows.append(jnp.full((SUBLANES - TOP_K, TM_TOK), -1, I32))
    lslot_ref[...] = jnp.concatenate(rows, axis=0)


def _slot_plan(eidx, rank, nmat):
    tok_map = lambda i: (0, i)
    const = lambda i: (0, 0)
    lslot, unit_tab, gend, blk = pl.pallas_call(
        _plan_kernel,
        out_shape=(
            jax.ShapeDtypeStruct((SUBLANES, T_ALL), I32),
            jax.ShapeDtypeStruct((N_TOK_TILES, 1, MAX_UNITS), I32),
            jax.ShapeDtypeStruct((N_EXPERTS, LANES), I32),
            jax.ShapeDtypeStruct((1, BLK_LANES), I32),
        ),
        grid=(N_TOK_TILES,),
        in_specs=[
            pl.BlockSpec((SUBLANES, TM_TOK), tok_map),
            pl.BlockSpec((SUBLANES, TM_TOK), tok_map),
            pl.BlockSpec((N_EXPERTS, LANES), const),
        ],
        out_specs=(
            pl.BlockSpec((SUBLANES, TM_TOK), tok_map),
            pl.BlockSpec((1, 1, MAX_UNITS), lambda i: (i, 0, 0)),
            pl.BlockSpec((N_EXPERTS, LANES), const),
            pl.BlockSpec((1, BLK_LANES), const),
        ),
        compiler_params=pltpu.CompilerParams(dimension_semantics=("arbitrary",)),
        name="slot_plan",
    )(eidx, rank, nmat)
    gend = gend[:, 0]
    n_used = gend[N_EXPERTS - 1:] // BM
    return lslot, unit_tab.reshape(N_TOK_TILES * MAX_UNITS), gend, blk[0, :N_BLOCKS_ALL], n_used


PACK_W = D_MODEL // 2
HI_HALF = -65536


def _pack_pairs(x):
    lo = lax.bitcast_convert_type(x[:, 0:PACK_W], I32)
    hi = lax.bitcast_convert_type(x[:, PACK_W:D_MODEL], I32)
    return lax.shift_right_logical(lo, 16) | (hi & HI_HALF)


def _unpack_pairs(u):
    lo = lax.bitcast_convert_type(lax.shift_left(u, 16), F32).astype(BF16)
    hi = lax.bitcast_convert_type(u & HI_HALF, F32).astype(BF16)
    return lo, hi


def _unit_rows(unit):
    row = unit * UNIT
    return pl.ds(row if isinstance(unit, int) else pl.multiple_of(row, UNIT), UNIT)


def _unit_copy(src, src_unit, dst, dst_unit, sem):
    return pltpu.make_async_copy(src.at[_unit_rows(src_unit)], dst.at[_unit_rows(dst_unit)], sem)


def _dispatch_kernel(gend_ref, tab_ref, h2_ref, lslot_ref, xs_hbm, zero_ref, local_ref, sem_zero, sem_rows):
    i = pl.program_id(0)
    buf = i % 2

    def drain(b):
        pltpu.make_async_copy(local_ref.at[b], xs_hbm.at[pl.ds(0, LOCAL_ROWS)], sem_rows.at[b]).wait()

    def has_rows(e):
        return gend_ref[e] > jnp.where(e == 0, 0, gend_ref[jnp.maximum(e - 1, 0)])

    def zero_copy(e):
        return pltpu.make_async_copy(
            zero_ref, xs_hbm.at[pl.ds(pl.multiple_of(gend_ref[e] - BM, BM), BM)], sem_zero)

    @pl.when(i == 0)
    def _():
        zero_ref[...] = jnp.zeros_like(zero_ref)

        def start(e, c):
            @pl.when(has_rows(e))
            def _():
                zero_copy(e).start()
            return c

        def wait(e, c):
            @pl.when(has_rows(e))
            def _():
                zero_copy(e).wait()
            return c

        def tail_copy(blk):
            return pltpu.make_async_copy(
                zero_ref, xs_hbm.at[pl.ds(pl.multiple_of(blk * BM, BM), BM)], sem_zero)

        def start_tail(blk, c):
            tail_copy(blk).start()
            return c

        def wait_tail(blk, c):
            tail_copy(blk).wait()
            return c

        n_used = gend_ref[N_EXPERTS - 1] // BM
        lax.fori_loop(0, N_EXPERTS, start, 0)
        lax.fori_loop(n_used, N_BLOCKS_ALL, start_tail, 0)
        lax.fori_loop(0, N_EXPERTS, wait, 0)
        lax.fori_loop(n_used, N_BLOCKS_ALL, wait_tail, 0)

    @pl.when(i >= 2)
    def _():
        drain(buf)

    h2 = h2_ref[...]
    units_per_chunk = PERM_CHUNK // UNIT
    local = local_ref.at[buf]
    for c in range(LOCAL_ROWS // PERM_CHUNK):
        slot = c * PERM_CHUNK + lax.broadcasted_iota(I32, (PERM_CHUNK, TM_TOK), 0)
        p = jnp.zeros((PERM_CHUNK, TM_TOK), F32)
        for k in range(TOP_K):
            p = jnp.where(slot == lslot_ref[k:k + 1, :], 1.0, p)
        local[c * PERM_CHUNK:(c + 1) * PERM_CHUNK, :] = _pack_pairs(_dot(p.astype(BF16), h2))
        for u in range(c * units_per_chunk, (c + 1) * units_per_chunk):
            _unit_copy(local, u, xs_hbm, tab_ref[i * MAX_UNITS + u], sem_rows.at[buf]).start()

    @pl.when(i == N_TOK_TILES - 1)
    def _():
        drain(1 - buf)
        drain(buf)


def _dispatch(h2, lslot, unit_tab, gend):
    return pl.pallas_call(
        _dispatch_kernel,
        out_shape=jax.ShapeDtypeStruct((N_SLOTS, PACK_W), I32),
        grid_spec=pltpu.PrefetchScalarGridSpec(
            num_scalar_prefetch=2,
            grid=(N_TOK_TILES,),
            in_specs=[
                pl.BlockSpec((TM_TOK, D_MODEL), lambda i, ge, tab: (i, 0)),
                pl.BlockSpec((SUBLANES, TM_TOK), lambda i, ge, tab: (0, i)),
            ],
            out_specs=pl.BlockSpec(memory_space=pl.ANY),
            scratch_shapes=[
                pltpu.VMEM((BM, PACK_W), I32),
                pltpu.VMEM((2, LOCAL_ROWS, PACK_W), I32),
                pltpu.SemaphoreType.DMA,
                pltpu.SemaphoreType.DMA((2,)),
            ],
        ),
        compiler_params=pltpu.CompilerParams(
            dimension_semantics=("arbitrary",), vmem_limit_bytes=VMEM_LIMIT),
        name="moe_dispatch",
    )(gend, unit_tab, h2, lslot)


def _expert_kernel(blk_e_ref, n_used_ref, xs_ref, wgu_ref, wdown_ref, ys_ref, wgu_bf, wdown_bf):
    i = pl.program_id(0)
    live = i < n_used_ref[0]
    new_expert = jnp.logical_or(i == 0, blk_e_ref[i] != blk_e_ref[jnp.maximum(i - 1, 0)])

    @pl.when(jnp.logical_and(live, new_expert))
    def _():
        wgu_bf[...] = wgu_ref[0, 0].astype(BF16)
        wdown_bf[...] = wdown_ref[0, 0].astype(BF16)

    @pl.when(live)
    def _():
        x_lo, x_hi = _unpack_pairs(xs_ref[...])
        gu = _dot(x_lo, wgu_bf[0:PACK_W, :]) + _dot(x_hi, wgu_bf[PACK_W:D_MODEL, :])
        act = _silu(gu[:, 0:D_EXPERT]) * gu[:, D_EXPERT:2 * D_EXPERT]
        y = _dot(act.astype(BF16), wdown_bf[...])
        ys_ref[...] = _pack_pairs(y.astype(BF16).astype(F32))

    @pl.when(i >= n_used_ref[0])
    def _():
        ys_ref[...] = jnp.zeros_like(ys_ref)


def _experts(xs, w_gu, w_down, layer, blk_e, n_used):
    def blk(i, be, nu):
        return jnp.minimum(i, nu[0] - 1)

    return pl.pallas_call(
        _expert_kernel,
        out_shape=jax.ShapeDtypeStruct((N_SLOTS, PACK_W), I32),
        grid_spec=pltpu.PrefetchScalarGridSpec(
            num_scalar_prefetch=2,
            grid=(N_BLOCKS_ALL,),
            in_specs=[
                pl.BlockSpec((BM, PACK_W), lambda i, be, nu: (blk(i, be, nu), 0)),
                pl.BlockSpec((1, 1, D_MODEL, 2 * D_EXPERT),
                             lambda i, be, nu: (layer, be[blk(i, be, nu)], 0, 0)),
                pl.BlockSpec((1, 1, D_EXPERT, D_MODEL),
                             lambda i, be, nu: (layer, be[blk(i, be, nu)], 0, 0)),
            ],
            out_specs=pl.BlockSpec((BM, PACK_W), lambda i, be, nu: (i, 0)),
            scratch_shapes=[
                pltpu.VMEM((D_MODEL, 2 * D_EXPERT), BF16),
                pltpu.VMEM((D_EXPERT, D_MODEL), BF16),
            ],
        ),
        compiler_params=pltpu.CompilerParams(
            dimension_semantics=("arbitrary",), vmem_limit_bytes=VMEM_LIMIT),
        name="moe_experts",
    )(blk_e, n_used, xs, w_gu, w_down)


def _combine_kernel(tab_ref, x1_ref, h2_ref, lslot_ref, wsel_ref, ys_hbm, wsgu_ref, wsdown_ref, mod_ref,
                    g_ref, b_ref, outc_ref, outl_ref, local_ref, sel_ref, ylo_ref, yhi_ref, sem_rows):
    i = pl.program_id(0)
    tm = TM_TOK
    buf = i % 2

    def fetch_unit(tile, b, u):
        _unit_copy(ys_hbm, tab_ref[tile * MAX_UNITS + u], local_ref.at[b], u, sem_rows.at[b]).start()

    def drain(b):
        pltpu.make_async_copy(ys_hbm.at[pl.ds(0, LOCAL_ROWS)], local_ref.at[b], sem_rows.at[b]).wait()

    @pl.when(i == 0)
    def _():
        def body(u, c):
            fetch_unit(0, 0, u)
            return c

        lax.fori_loop(0, MAX_UNITS, body, 0, unroll=8)

    sgu = _dot(h2_ref[...], wsgu_ref[...])
    act = _silu(sgu[:, 0:D_SHARED]) * sgu[:, D_SHARED:2 * D_SHARED]
    f = _dot(act.astype(BF16), wsdown_ref[...])

    nxt = jnp.minimum(i + 1, N_TOK_TILES - 1)
    n_groups = tm // SEL_ROWS
    units_per_group = MAX_UNITS // n_groups
    slot = lax.broadcasted_iota(I32, (SEL_ROWS, LOCAL_ROWS), 1)
    for g in range(n_groups):
        rows = slice(g * SEL_ROWS, (g + 1) * SEL_ROWS)
        sel = jnp.zeros((SEL_ROWS, LOCAL_ROWS), F32)
        for k in range(TOP_K):
            sel = jnp.where(slot == lslot_ref[rows, k:k + 1], wsel_ref[rows, k:k + 1], sel)
        sel_ref[rows, :] = sel.astype(BF16)
        for u in range(g * units_per_group, (g + 1) * units_per_group):
            fetch_unit(nxt, 1 - buf, u)

    local = local_ref.at[buf]
    drain(buf)
    for c in range(LOCAL_ROWS // PERM_CHUNK):
        rows = slice(c * PERM_CHUNK, (c + 1) * PERM_CHUNK)
        ylo_ref[rows, :], yhi_ref[rows, :] = _unpack_pairs(local[rows, :])
    sel = sel_ref[...]
    f = f + jnp.concatenate([_dot(sel, ylo_ref[...]), _dot(sel, yhi_ref[...])], axis=-1)
    ci = _cond_row(i, tm)
    gate2 = mod_ref[pl.ds(ci, 1), 5 * D_MODEL:6 * D_MODEL]
    out = _layer_norm(ALPHA * x1_ref[...] + gate2 * f, g_ref[...], b_ref[...])

    @pl.when(i < T_CTX // tm)
    def _():
        outc_ref[...] = out

    @pl.when(i >= T_CTX // tm)
    def _():
        outl_ref[...] = out

    @pl.when(i == N_TOK_TILES - 1)
    def _():
        drain(1 - buf)


def _combine(x1, h2, lslot_rows, wsel_rows, unit_tab, ys, w_sgu_bf16, w_sdown_bf16, mod, ln_g, ln_b):
    n_ctx = T_CTX // TM_TOK
    row_map = lambda i, tab: (i, 0)
    const = lambda i, tab: (0, 0)
    return pl.pallas_call(
        _combine_kernel,
        out_shape=(jax.ShapeDtypeStruct((T_CTX, D_MODEL), F32),
                   jax.ShapeDtypeStruct((T_LAT, D_MODEL), F32)),
        grid_spec=pltpu.PrefetchScalarGridSpec(
            num_scalar_prefetch=1,
            grid=(N_TOK_TILES,),
            in_specs=[
                pl.BlockSpec((TM_TOK, D_MODEL), row_map),
                pl.BlockSpec((TM_TOK, D_MODEL), row_map),
                pl.BlockSpec((TM_TOK, SUBLANES), row_map),
                pl.BlockSpec((TM_TOK, SUBLANES), row_map),
                pl.BlockSpec(memory_space=pl.ANY),
                pl.BlockSpec((D_MODEL, 2 * D_SHARED), const),
                pl.BlockSpec((D_SHARED, D_MODEL), const),
                pl.BlockSpec((N_COND, 6 * D_MODEL), const),
                pl.BlockSpec((1, D_MODEL), const),
                pl.BlockSpec((1, D_MODEL), const),
            ],
            out_specs=(pl.BlockSpec((TM_TOK, D_MODEL), lambda i, tab: (jnp.minimum(i, n_ctx - 1), 0)),
                       pl.BlockSpec((TM_TOK, D_MODEL), lambda i, tab: (jnp.maximum(i - n_ctx, 0), 0))),
            scratch_shapes=[
                pltpu.VMEM((2, LOCAL_ROWS, PACK_W), I32),
                pltpu.VMEM((TM_TOK, LOCAL_ROWS), BF16),
                pltpu.VMEM((LOCAL_ROWS, PACK_W), BF16),
                pltpu.VMEM((LOCAL_ROWS, PACK_W), BF16),
                pltpu.SemaphoreType.DMA((2,)),
            ],
        ),
        compiler_params=pltpu.CompilerParams(
            dimension_semantics=("arbitrary",), vmem_limit_bytes=VMEM_LIMIT),
        name="moe_combine",
    )(unit_tab, x1, h2, lslot_rows, wsel_rows, ys, w_sgu_bf16, w_sdown_bf16, mod, ln_g, ln_b)


def _lane_rows(v, width):
    return jnp.broadcast_to(v.astype(F32)[:, None, None], (v.shape[0], 1, width))


def kernel(x_prompt, x_sample, state_ret_fwd, state_ret_bwd, cache_win_k, cache_win_v, cache_na_k, cache_na_v, c, c_ctx, w_in, w_out, ret_decay_fwd, ret_decay_bwd, ret_gn_g, ret_gn_b, win_sink, na_rpb, w_mod, b_mod, ln1_g, ln1_b, ln2_g, ln2_b, w_router, b_router, w_expert_gu, w_expert_down, w_shared_gu, w_shared_down):
    cond = jnp.concatenate(
        [c_ctx[None, :], c, jnp.zeros((N_COND - 1 - DEC_BATCH, D_MODEL), F32)], axis=0)
    mod_all = _modulation(cond, w_mod, b_mod)
    cos_t, sin_t = _rope_tables()

    x_ctx = x_prompt.reshape(T_CTX, D_MODEL)
    x_lat = x_sample.reshape(T_LAT, D_MODEL)
    sf_l, sb_l, caches = [], [], ()
    for l in range(DEPTH):
        mod = mod_all[l]
        pa, pb, pc, *caches = _in_projection(x_ctx, x_lat, mod, w_in[l].astype(BF16), tuple(caches))
        decf_s, decb_s = _lane_rows(ret_decay_fwd[l], SEQ), _lane_rows(ret_decay_bwd[l], SEQ)
        gng, gnb = ret_gn_g[l][None, :], ret_gn_b[l][None, :]
        mix_c, st_f, st_b = _ctx_mixers(pa, pb, pc, win_sink[l], decf_s, decb_s, gng, gnb)
        sf_l.append(st_f)
        sb_l.append(st_b)
        ya = _lat_retention(pa, state_ret_fwd, state_ret_bwd, l,
                            _lane_rows(ret_decay_fwd[l], RET_CHUNK), _lane_rows(ret_decay_bwd[l], RET_CHUNK),
                            gng, gnb)
        yb = _lat_window_attn(pb, cache_win_k, cache_win_v, l, win_sink[l], cos_t, sin_t)
        yc = _lat_na_attn(pc, cache_na_k, cache_na_v, l, _na_maskbias(na_rpb[l]))

        wr_t = w_router[l].T
        wr_hi = wr_t.astype(BF16)
        wr_lo = (wr_t - wr_hi.astype(F32)).astype(BF16)
        x1, h2, eidx, wsel, rank, counts = _post_mixer(
            x_ctx, x_lat, mix_c, ya, yb, yc, w_out[l].astype(BF16), mod, ln1_g[l][None, :],
            ln1_b[l][None, :], wr_hi, wr_lo, b_router[l][:, None])
        lslot, unit_tab, gend, blk_e, n_used = _slot_plan(eidx, rank, counts)
        xs = _dispatch(h2, lslot, unit_tab, gend)
        ys = _experts(xs, w_expert_gu, w_expert_down, l, blk_e, n_used)
        x_ctx, x_lat = _combine(x1, h2, lslot.T, wsel.T, unit_tab, ys, w_shared_gu[l].astype(BF16),
                                w_shared_down[l].astype(BF16), mod, ln2_g[l][None, :], ln2_b[l][None, :])

    y_prompt = x_ctx.reshape(BATCH, SEQ, D_MODEL)
    y_sample = x_lat.reshape(DEC_BATCH, DEC_SEQ, D_MODEL)
    new_sf = jnp.stack(sf_l, axis=1)
    new_sb = jnp.stack(sb_l, axis=1)

    return (y_prompt, y_sample, new_sf, new_sb, *caches)
```

```python
import functools

import numpy as np
import jax
import jax.numpy as jnp
from jax import lax
from jax.experimental import pallas as pl
from jax.experimental.pallas import tpu as pltpu

F32 = jnp.float32
BF16 = jnp.bfloat16
I32 = jnp.int32

D_MODEL = 1024
BATCH = 32
SEQ = 256
DEPTH = 2
DEC_BATCH = 4
DEC_SEQ = 2048
PAST_LEN = 256
GRID_W = 64
HEAD_DIM = 64
ATTN_SCALE = HEAD_DIM ** -0.5
H_A = 4
W_A = H_A * HEAD_DIM
GN_EPS = 1e-5
H_B = 6
KV_B = 2
W_B = H_B * HEAD_DIM
KV_W_B = KV_B * HEAD_DIM
WINDOW = 128
ROPE_BASE = 10000.0
H_C = 6
W_C = H_C * HEAD_DIM
NA_ROWS = 8
NA_COLS = 16
IN_WIDTH = 4 * W_A + W_B + 2 * KV_W_B + 3 * W_C
N_EXPERTS = 64
TOP_K = 6
N_GROUPS = 8
GROUP_SIZE = N_EXPERTS // N_GROUPS
TOPK_GROUPS = 4
D_EXPERT = 256
D_SHARED = 256
ROUTE_SCALE = 2.5
ALPHA = (2 * DEPTH) ** 0.25
LN_EPS = 1e-5
NEG_INF = -1e30
PICKED = -3e38

T_CTX = BATCH * SEQ
T_LAT = DEC_BATCH * DEC_SEQ
T_ALL = T_CTX + T_LAT
N_COND = 8

PA_W = 4 * W_A
PB_W = W_B + 2 * KV_W_B
PC_W = 3 * W_C

LANES = 128
SUBLANES = 8
VMEM_LIMIT = 56 * 1024 * 1024

TM_PROJ = 512
TM_TOK = 256
RET_CHUNK = 256
WIN_QB = 128
NA_TILE_ROWS = 4
NA_KEY_ROWS = 11
BM = 512
UNIT = SUBLANES
UNITS_PER_BLOCK = BM // UNIT
PERM_CHUNK = 256
SEL_ROWS = 16
N_TOK_TILES = T_ALL // TM_TOK
LOCAL_ROWS = -(-(TM_TOK * TOP_K + N_EXPERTS * (UNIT - 1)) // PERM_CHUNK) * PERM_CHUNK
MAX_UNITS = LOCAL_ROWS // UNIT
N_ASSIGN = T_ALL * TOP_K
N_BLOCKS = -(-(N_ASSIGN + N_TOK_TILES * N_EXPERTS * (UNIT - 1) + N_EXPERTS * (BM - 1)) // BM)
SPARE_BLOCKS = 2 * -(-LOCAL_ROWS // BM)
SPARE_UNIT0 = N_BLOCKS * UNITS_PER_BLOCK
N_BLOCKS_ALL = N_BLOCKS + SPARE_BLOCKS
N_SLOTS = N_BLOCKS_ALL * BM


def _dot(a, b):
    return jnp.dot(a, b, preferred_element_type=F32)


def _dot_nt(a, b):
    return lax.dot_general(a, b, (((1,), (1,)), ((), ())), preferred_element_type=F32)


def _silu(x):
    return x * jax.nn.sigmoid(x)


def _log_sigmoid(x):
    return jnp.minimum(x, 0.0) - jnp.log(1.0 + jnp.exp(-jnp.abs(x)))


def _cond_row(tile, tile_rows):
    n_ctx = T_CTX // tile_rows
    per_lat = DEC_SEQ // tile_rows
    return jnp.where(tile < n_ctx, 0, 1 + (tile - n_ctx) // per_lat)


def _layer_norm(x, g, b):
    mu = jnp.mean(x, -1, keepdims=True)
    xc = x - mu
    var = jnp.mean(xc * xc, -1, keepdims=True)
    return xc * lax.rsqrt(var + LN_EPS) * g + b


MOD_TN = 1536


def _mod_kernel(cond_ref, w_ref, b_ref, o_ref):
    s = _silu(cond_ref[...])
    s_hi = s.astype(BF16)
    s_lo = (s - s_hi.astype(F32)).astype(BF16)
    w = w_ref[0]
    w_hi = w.astype(BF16)
    w_lo = (w - w_hi.astype(F32)).astype(BF16)
    o_ref[0] = _dot(s_hi, w_hi) + _dot(s_lo, w_hi) + _dot(s_hi, w_lo) + b_ref[0]


def _modulation(cond, w_mod, b_mod):
    n_out = 6 * D_MODEL
    return pl.pallas_call(
        _mod_kernel,
        out_shape=jax.ShapeDtypeStruct((DEPTH, N_COND, n_out), F32),
        grid=(DEPTH, n_out // MOD_TN),
        in_specs=[
            pl.BlockSpec((N_COND, D_MODEL), lambda l, j: (0, 0)),
            pl.BlockSpec((1, D_MODEL, MOD_TN), lambda l, j: (l, 0, j)),
            pl.BlockSpec((1, 1, MOD_TN), lambda l, j: (l, 0, j)),
        ],
        out_specs=pl.BlockSpec((1, N_COND, MOD_TN), lambda l, j: (l, 0, j)),
        compiler_params=pltpu.CompilerParams(
            dimension_semantics=("arbitrary", "arbitrary"), vmem_limit_bytes=VMEM_LIMIT),
        name="modulation",
    )(cond, w_mod, b_mod.reshape(DEPTH, 1, n_out))


SEQ_PER_PROJ = TM_PROJ // SEQ


def _inproj_kernel(n_prev, xc_ref, xl_ref, mod_ref, w_ref, *refs):
    prev_refs = refs[:4] if n_prev else ()
    pa_ref, pb_ref, pc_ref, wk_ref, wv_ref, nk_ref, nv_ref = refs[len(prev_refs):]
    i = pl.program_id(0)
    ci = _cond_row(i, TM_PROJ)
    sh = mod_ref[pl.ds(ci, 1), 0:D_MODEL]
    sc = mod_ref[pl.ds(ci, 1), D_MODEL:2 * D_MODEL]
    x = jnp.where(i < T_CTX // TM_PROJ, xc_ref[...], xl_ref[...])
    h = x * (1.0 + sc) + sh
    p = _dot(h.astype(BF16), w_ref[...])
    pa_ref[...] = p[:, 0:PA_W].astype(BF16)
    pb_ref[...] = p[:, PA_W:PA_W + PB_W].astype(BF16)
    pc_ref[...] = p[:, PA_W + PB_W:IN_WIDTH].astype(BF16)

    @pl.when(i < T_CTX // TM_PROJ)
    def _():
        targets = ((wk_ref, PA_W + W_B, KV_B), (wv_ref, PA_W + W_B + KV_W_B, KV_B),
                   (nk_ref, PA_W + PB_W + W_C, H_C), (nv_ref, PA_W + PB_W + 2 * W_C, H_C))
        for j, (ref, col0, n_heads) in enumerate(targets):
            if n_prev:
                ref[:, 0:n_prev] = prev_refs[j][...]
            for s in range(SEQ_PER_PROJ):
                for hh in range(n_heads):
                    ref[s, n_prev, hh] = p[s * SEQ:(s + 1) * SEQ,
                                           col0 + hh * HEAD_DIM:col0 + (hh + 1) * HEAD_DIM]


def _in_projection(x_ctx, x_lat, mod, w_in_bf16, earlier):
    n_ctx_tiles = T_CTX // TM_PROJ
    n_prev = earlier[0].shape[1] if earlier else 0

    def cache_spec(n_layers, n_heads):
        return pl.BlockSpec((SEQ_PER_PROJ, n_layers, n_heads, SEQ, HEAD_DIM),
                            lambda i: (jnp.minimum(i, n_ctx_tiles - 1), 0, 0, 0, 0))

    cache_heads = (KV_B, KV_B, H_C, H_C)
    return pl.pallas_call(
        functools.partial(_inproj_kernel, n_prev),
        out_shape=(
            jax.ShapeDtypeStruct((T_ALL, PA_W), BF16),
            jax.ShapeDtypeStruct((T_ALL, PB_W), BF16),
            jax.ShapeDtypeStruct((T_ALL, PC_W), BF16),
        ) + tuple(jax.ShapeDtypeStruct((BATCH, n_prev + 1, nh, SEQ, HEAD_DIM), F32) for nh in cache_heads),
        grid=(T_ALL // TM_PROJ,),
        in_specs=[
            pl.BlockSpec((TM_PROJ, D_MODEL), lambda i: (jnp.minimum(i, n_ctx_tiles - 1), 0)),
            pl.BlockSpec((TM_PROJ, D_MODEL), lambda i: (jnp.maximum(i - n_ctx_tiles, 0), 0)),
            pl.BlockSpec((N_COND, 6 * D_MODEL), lambda i: (0, 0)),
            pl.BlockSpec((D_MODEL, IN_WIDTH), lambda i: (0, 0)),
        ] + [cache_spec(n_prev, nh) for nh in cache_heads if n_prev],
        out_specs=(
            pl.BlockSpec((TM_PROJ, PA_W), lambda i: (i, 0)),
            pl.BlockSpec((TM_PROJ, PB_W), lambda i: (i, 0)),
            pl.BlockSpec((TM_PROJ, PC_W), lambda i: (i, 0)),
        ) + tuple(cache_spec(n_prev + 1, nh) for nh in cache_heads),
        compiler_params=pltpu.CompilerParams(
            dimension_semantics=("arbitrary",), vmem_limit_bytes=VMEM_LIMIT),
        name="in_projection",
    )(x_ctx, x_lat, mod, w_in_bf16, *earlier)


def _decay_matrix(lg_f, lg_b, n):
    row = lax.broadcasted_iota(I32, (n, n), 0)
    col = lax.broadcasted_iota(I32, (n, n), 1)
    diff = (row - col).astype(F32)
    fwd = jnp.where(diff >= 0, jnp.exp(lg_f * jnp.maximum(diff, 0.0)), 0.0)
    bwd = jnp.where(diff <= 0, jnp.exp(lg_b * jnp.maximum(-diff, 0.0)), 0.0)
    return (fwd + bwd) * ATTN_SCALE


def _retention_readout(o, gate, g, b):
    mu = jnp.mean(o, -1, keepdims=True)
    oc = o - mu
    var = jnp.mean(oc * oc, -1, keepdims=True)
    on = oc * lax.rsqrt(var + GN_EPS) * g + b
    return on * _silu(gate.astype(F32))


def _softmax_attend(s, v, extra_logit=None):
    m = jnp.max(s, -1, keepdims=True)
    if extra_logit is not None:
        m = jnp.maximum(m, extra_logit)
    p = jnp.exp(s - m)
    den = jnp.sum(p, -1, keepdims=True)
    if extra_logit is not None:
        den = den + jnp.exp(extra_logit - m)
    return _dot(p.astype(BF16), v) / den


def _ctx_mixer_kernel(sink_ref, pa_ref, pb_ref, pc_ref, decf_ref, decb_ref, gng_ref, gnb_ref,
                      mix_ref, sf_ref, sb_ref):
    n = SEQ
    hd = HEAD_DIM
    pos = lax.broadcasted_iota(I32, (n, hd), 0).astype(F32)
    for h in range(H_A):
        c0 = h * hd
        q = pa_ref[:, c0:c0 + hd]
        k = pa_ref[:, W_A + c0:W_A + c0 + hd]
        v = pa_ref[:, 2 * W_A + c0:2 * W_A + c0 + hd]
        gate = pa_ref[:, 3 * W_A + c0:3 * W_A + c0 + hd]
        lg_f = _log_sigmoid(decf_ref[h])
        lg_b = _log_sigmoid(decb_ref[h])
        dmat = _decay_matrix(lg_f, lg_b, n)
        o = _dot((_dot_nt(q, k) * dmat).astype(BF16), v)
        kf = k.astype(F32)
        zf = jnp.exp(lg_f[:, 0:hd] * (n - 1.0 - pos)) * ATTN_SCALE
        zb = jnp.exp(lg_b[:, 0:hd] * pos) * ATTN_SCALE
        sf_ref[0, h] = _dot((kf * zf).T.astype(BF16), v)
        sb_ref[0, h] = _dot((kf * zb).T.astype(BF16), v)
        y = _retention_readout(o, gate, gng_ref[:, c0:c0 + hd], gnb_ref[:, c0:c0 + hd])
        mix_ref[:, c0:c0 + hd] = y.astype(BF16)
    for j in range(KV_B):
        k = pb_ref[:, W_B + j * hd:W_B + (j + 1) * hd]
        v = pb_ref[:, W_B + KV_W_B + j * hd:W_B + KV_W_B + (j + 1) * hd]
        for g in range(H_B // KV_B):
            hh = j * (H_B // KV_B) + g
            q = pb_ref[:, hh * hd:(hh + 1) * hd]
            o = _softmax_attend(_dot_nt(q, k) * ATTN_SCALE, v, sink_ref[hh])
            mix_ref[:, W_A + hh * hd:W_A + (hh + 1) * hd] = o.astype(BF16)
    for hh in range(H_C):
        q = pc_ref[:, hh * hd:(hh + 1) * hd]
        k = pc_ref[:, W_C + hh * hd:W_C + (hh + 1) * hd]
        v = pc_ref[:, 2 * W_C + hh * hd:2 * W_C + (hh + 1) * hd]
        o = _softmax_attend(_dot_nt(q, k) * ATTN_SCALE, v)
        mix_ref[:, W_A + W_B + hh * hd:W_A + W_B + (hh + 1) * hd] = o.astype(BF16)


def _ctx_mixers(pa, pb, pc, sink, decf, decb, gng, gnb):
    return pl.pallas_call(
        _ctx_mixer_kernel,
        out_shape=(
            jax.ShapeDtypeStruct((T_CTX, D_MODEL), BF16),
            jax.ShapeDtypeStruct((BATCH, H_A, HEAD_DIM, HEAD_DIM), F32),
            jax.ShapeDtypeStruct((BATCH, H_A, HEAD_DIM, HEAD_DIM), F32),
        ),
        grid=(BATCH,),
        in_specs=[
            pl.BlockSpec(memory_space=pltpu.SMEM),
            pl.BlockSpec((SEQ, PA_W), lambda b: (b, 0)),
            pl.BlockSpec((SEQ, PB_W), lambda b: (b, 0)),
            pl.BlockSpec((SEQ, PC_W), lambda b: (b, 0)),
            pl.BlockSpec((H_A, 1, SEQ), lambda b: (0, 0, 0)),
            pl.BlockSpec((H_A, 1, SEQ), lambda b: (0, 0, 0)),
            pl.BlockSpec((1, W_A), lambda b: (0, 0)),
            pl.BlockSpec((1, W_A), lambda b: (0, 0)),
        ],
        out_specs=(
            pl.BlockSpec((SEQ, D_MODEL), lambda b: (b, 0)),
            pl.BlockSpec((1, H_A, HEAD_DIM, HEAD_DIM), lambda b: (b, 0, 0, 0)),
            pl.BlockSpec((1, H_A, HEAD_DIM, HEAD_DIM), lambda b: (b, 0, 0, 0)),
        ),
        compiler_params=pltpu.CompilerParams(
            dimension_semantics=("arbitrary",), vmem_limit_bytes=VMEM_LIMIT),
        name="ctx_mixers",
    )(sink, pa, pb, pc, decf, decb, gng, gnb)


def _lat_ret_kernel(pa_ref, stf_ref, stb_ref, decf_ref, decb_ref, gng_ref, gnb_ref, ya_ref, acc_ref):
    c = RET_CHUNK
    hd = HEAD_DIM
    n_chunks = DEC_SEQ // c
    pos = lax.broadcasted_iota(I32, (c, hd), 0).astype(F32)
    for h in range(H_A):
        c0 = h * hd
        lg_f = _log_sigmoid(decf_ref[h])
        lg_b = _log_sigmoid(decb_ref[h])
        dmat = _decay_matrix(lg_f, lg_b, c)
        lf = lg_f[:, 0:hd]
        lb = lg_b[:, 0:hd]
        zf = jnp.exp(lf * (c - 1.0 - pos)) * ATTN_SCALE
        zb = jnp.exp(lb * pos) * ATTN_SCALE
        xf = jnp.exp(lf * (pos + 1.0))
        xb = jnp.exp(lb * (c - pos))
        gcf = jnp.exp(lf * float(c))
        gcb = jnp.exp(lb * float(c))
        g = gng_ref[:, c0:c0 + hd]
        b = gnb_ref[:, c0:c0 + hd]

        def load(i, off):
            rows = pl.ds(pl.multiple_of(i * c, c), c)
            return pa_ref[rows, off + c0:off + c0 + hd]

        def fwd(i, s):
            q, k, v = load(i, 0), load(i, W_A), load(i, 2 * W_A)
            o = _dot((_dot_nt(q, k) * dmat).astype(BF16), v)
            o = o + _dot((q.astype(F32) * xf).astype(BF16), s.astype(BF16))
            acc_ref[pl.ds(pl.multiple_of(i * c, c), c), c0:c0 + hd] = o
            return gcf * s + _dot((k.astype(F32) * zf).T.astype(BF16), v)

        lax.fori_loop(0, n_chunks, fwd, stf_ref[0, 0, h])

        def bwd(j, s):
            i = n_chunks - 1 - j
            rows = pl.ds(pl.multiple_of(i * c, c), c)
            q, k, v = load(i, 0), load(i, W_A), load(i, 2 * W_A)
            o = acc_ref[rows, c0:c0 + hd] + _dot((q.astype(F32) * xb).astype(BF16), s.astype(BF16))
            y = _retention_readout(o, load(i, 3 * W_A), g, b)
            ya_ref[rows, c0:c0 + hd] = y.astype(BF16)
            return gcb * s + _dot((k.astype(F32) * zb).T.astype(BF16), v)

        lax.fori_loop(0, n_chunks, bwd, stb_ref[0, 0, h])


def _lat_retention(pa, st_f, st_b, layer, decf, decb, gng, gnb):
    lat0 = T_CTX // DEC_SEQ
    st_spec = pl.BlockSpec((1, 1, H_A, HEAD_DIM, HEAD_DIM), lambda b: (b, layer, 0, 0, 0))
    return pl.pallas_call(
        _lat_ret_kernel,
        out_shape=jax.ShapeDtypeStruct((T_LAT, W_A), BF16),
        grid=(DEC_BATCH,),
        in_specs=[
            pl.BlockSpec((DEC_SEQ, PA_W), lambda b: (lat0 + b, 0)),
            st_spec, st_spec,
            pl.BlockSpec((H_A, 1, RET_CHUNK), lambda b: (0, 0, 0)),
            pl.BlockSpec((H_A, 1, RET_CHUNK), lambda b: (0, 0, 0)),
            pl.BlockSpec((1, W_A), lambda b: (0, 0)),
            pl.BlockSpec((1, W_A), lambda b: (0, 0)),
        ],
        out_specs=pl.BlockSpec((DEC_SEQ, W_A), lambda b: (b, 0)),
        scratch_shapes=[pltpu.VMEM((DEC_SEQ, W_A), F32)],
        compiler_params=pltpu.CompilerParams(
            dimension_semantics=("arbitrary",), vmem_limit_bytes=VMEM_LIMIT),
        name="lat_retention",
    )(pa, st_f, st_b, decf, decb, gng, gnb)


def _swap_halves_matrix(width):
    r = lax.broadcasted_iota(I32, (width, width), 0)
    c = lax.broadcasted_iota(I32, (width, width), 1)
    return jnp.where((r ^ (HEAD_DIM // 2)) == c, 1.0, 0.0).astype(BF16)


def _rope(x, cos, sin_signed, swap):
    return x.astype(F32) * cos + _dot(x, swap) * sin_signed


def _lat_win_kernel(sink_ref, pq_ref, pseq_ref, kctx_ref, vctx_ref, cos_ref, sin_ref, yb_ref, krope_ref):
    n = pl.program_id(1)
    hd = HEAD_DIM
    qb = WIN_QB
    n_blk = DEC_SEQ // qb
    group = H_B // KV_B
    swap = _swap_halves_matrix(LANES)

    @pl.when(n == 0)
    def _():
        k = pseq_ref[:, W_B:W_B + KV_W_B]
        krope_ref[...] = _rope(k, cos_ref[...], sin_ref[...], swap).astype(BF16)

    q_rows = pl.ds(pl.multiple_of(n * qb, qb), qb)
    cos_q = cos_ref[q_rows, :]
    sin_q = sin_ref[q_rows, :]
    qr = [_rope(pq_ref[:, p * LANES:(p + 1) * LANES], cos_q, sin_q, swap).astype(BF16)
          for p in range(W_B // LANES)]

    ws = jnp.clip(n - 1, 0, n_blk - 3) * qb
    k_rows = pl.ds(pl.multiple_of(ws, qb), 3 * qb)
    q_pos = n * qb + lax.broadcasted_iota(I32, (group * qb, 3 * qb), 0) % qb
    k_pos = ws + lax.broadcasted_iota(I32, (group * qb, 3 * qb), 1)
    valid = jnp.abs(k_pos - q_pos) <= WINDOW
    head_of_row = lax.broadcasted_iota(I32, (group * qb, 1), 0) // qb
    for j in range(KV_B):
        heads = [j * group + g for g in range(group)]
        qs = jnp.concatenate(
            [qr[hh // 2][:, (hh % 2) * hd:(hh % 2 + 1) * hd] for hh in heads], axis=0)
        kw = krope_ref[k_rows, j * hd:(j + 1) * hd]
        vw = pseq_ref[k_rows, W_B + KV_W_B + j * hd:W_B + KV_W_B + (j + 1) * hd]
        kc = kctx_ref[0, 0, j].astype(BF16)
        vc = vctx_ref[0, 0, j].astype(BF16)
        s_loc = jnp.where(valid, _dot_nt(qs, kw) * ATTN_SCALE, NEG_INF)
        s_ctx = _dot_nt(qs, kc) * ATTN_SCALE
        sink = jnp.zeros((group * qb, 1), F32)
        for g, hh in enumerate(heads):
            sink = jnp.where(head_of_row == g, sink_ref[hh], sink)
        m = jnp.maximum(jnp.maximum(jnp.max(s_loc, -1, keepdims=True),
                                    jnp.max(s_ctx, -1, keepdims=True)), sink)
        p_loc = jnp.exp(s_loc - m)
        p_ctx = jnp.exp(s_ctx - m)
        den = (jnp.sum(p_loc, -1, keepdims=True) + jnp.sum(p_ctx, -1, keepdims=True)
               + jnp.exp(sink - m))
        o = (_dot(p_loc.astype(BF16), vw) + _dot(p_ctx.astype(BF16), vc)) / den
        for g, hh in enumerate(heads):
            yb_ref[:, hh * hd:(hh + 1) * hd] = o[g * qb:(g + 1) * qb].astype(BF16)


def _lat_window_attn(pb, cache_k, cache_v, layer, sink, cos_t, sin_t):
    n_blk = DEC_SEQ // WIN_QB
    lat_blk0 = T_CTX // WIN_QB
    lat_seq0 = T_CTX // DEC_SEQ
    ctx_spec = pl.BlockSpec((1, 1, KV_B, PAST_LEN, HEAD_DIM), lambda b, n: (b, layer, 0, 0, 0))
    return pl.pallas_call(
        _lat_win_kernel,
        out_shape=jax.ShapeDtypeStruct((T_LAT, W_B), BF16),
        grid=(DEC_BATCH, n_blk),
        in_specs=[
            pl.BlockSpec(memory_space=pltpu.SMEM),
            pl.BlockSpec((WIN_QB, PB_W), lambda b, n: (lat_blk0 + b * n_blk + n, 0)),
            pl.BlockSpec((DEC_SEQ, PB_W), lambda b, n: (lat_seq0 + b, 0)),
            ctx_spec, ctx_spec,
            pl.BlockSpec((DEC_SEQ, LANES), lambda b, n: (0, 0)),
            pl.BlockSpec((DEC_SEQ, LANES), lambda b, n: (0, 0)),
        ],
        out_specs=pl.BlockSpec((WIN_QB, W_B), lambda b, n: (b * n_blk + n, 0)),
        scratch_shapes=[pltpu.VMEM((DEC_SEQ, KV_W_B), BF16)],
        compiler_params=pltpu.CompilerParams(
            dimension_semantics=("arbitrary", "arbitrary"), vmem_limit_bytes=VMEM_LIMIT),
        name="lat_window_attn",
    )(sink, pb, pb, cache_k, cache_v, cos_t, sin_t)


NA_Q = NA_TILE_ROWS * GRID_W
NA_K = NA_KEY_ROWS * GRID_W
NA_TILES = DEC_SEQ // NA_Q
LAT_ROWS = DEC_SEQ // GRID_W


def _na_window_start(tile):
    return jnp.clip(tile * NA_TILE_ROWS - NA_ROWS // 2, 0, LAT_ROWS - NA_KEY_ROWS)


def _lat_na_kernel(pq_ref, pseq_ref, kctx_ref, vctx_ref, bias_ref, yc_ref):
    t = pl.program_id(1)
    hd = HEAD_DIM
    k_rows = pl.ds(pl.multiple_of(_na_window_start(t) * GRID_W, GRID_W), NA_K)
    for hh in range(H_C):
        q = pq_ref[:, hh * hd:(hh + 1) * hd]
        kw = pseq_ref[k_rows, W_C + hh * hd:W_C + (hh + 1) * hd]
        vw = pseq_ref[k_rows, 2 * W_C + hh * hd:2 * W_C + (hh + 1) * hd]
        kc = kctx_ref[0, 0, hh].astype(BF16)
        vc = vctx_ref[0, 0, hh].astype(BF16)
        s_loc = _dot_nt(q, kw) * ATTN_SCALE + bias_ref[0, hh]
        s_ctx = _dot_nt(q, kc) * ATTN_SCALE
        m = jnp.maximum(jnp.max(s_loc, -1, keepdims=True), jnp.max(s_ctx, -1, keepdims=True))
        p_loc = jnp.exp(s_loc - m)
        p_ctx = jnp.exp(s_ctx - m)
        den = jnp.sum(p_loc, -1, keepdims=True) + jnp.sum(p_ctx, -1, keepdims=True)
        o = (_dot(p_loc.astype(BF16), vw) + _dot(p_ctx.astype(BF16), vc)) / den
        yc_ref[:, hh * hd:(hh + 1) * hd] = o.astype(BF16)


def _na_tile_type(t):
    return jnp.where(t == 0, 0, jnp.where(t == NA_TILES - 1, 2, 1))


def _lat_na_attn(pc, cache_k, cache_v, layer, maskbias):
    lat_tile0 = T_CTX // NA_Q
    lat_seq0 = T_CTX // DEC_SEQ
    ctx_spec = pl.BlockSpec((1, 1, H_C, PAST_LEN, HEAD_DIM), lambda b, t: (b, layer, 0, 0, 0))
    return pl.pallas_call(
        _lat_na_kernel,
        out_shape=jax.ShapeDtypeStruct((T_LAT, W_C), BF16),
        grid=(DEC_BATCH, NA_TILES),
        in_specs=[
            pl.BlockSpec((NA_Q, PC_W), lambda b, t: (lat_tile0 + b * NA_TILES + t, 0)),
            pl.BlockSpec((DEC_SEQ, PC_W), lambda b, t: (lat_seq0 + b, 0)),
            ctx_spec, ctx_spec,
            pl.BlockSpec((1, H_C, NA_Q, NA_K), lambda b, t: (_na_tile_type(t), 0, 0, 0)),
        ],
        out_specs=pl.BlockSpec((NA_Q, W_C), lambda b, t: (b * NA_TILES + t, 0)),
        compiler_params=pltpu.CompilerParams(
            dimension_semantics=("arbitrary", "arbitrary"), vmem_limit_bytes=VMEM_LIMIT),
        name="lat_na_attn",
    )(pc, pc, cache_k, cache_v, maskbias)


def _na_block_index():
    out = np.zeros((3, NA_TILE_ROWS, NA_KEY_ROWS), np.int32)
    for ty, tile in enumerate((0, 1, NA_TILES - 1)):
        r = tile * NA_TILE_ROWS
        ws = int(np.clip(r - NA_ROWS // 2, 0, LAT_ROWS - NA_KEY_ROWS))
        for qq in range(NA_TILE_ROWS):
            qr = r + qq
            r0 = int(np.clip(qr - NA_ROWS // 2, 0, LAT_ROWS - NA_ROWS))
            for kk in range(NA_KEY_ROWS):
                kr = ws + kk
                out[ty, qq, kk] = kr - qr + NA_ROWS - 1 if r0 <= kr < r0 + NA_ROWS else 2 * NA_ROWS - 1
    return out


def _na_maskbias(rpb):
    qc = np.arange(GRID_W)[:, None]
    kc = np.arange(GRID_W)[None, :]
    c0 = np.clip(qc - NA_COLS // 2, 0, GRID_W - NA_COLS)
    col_ok = (kc >= c0) & (kc < c0 + NA_COLS)
    ci = np.clip(kc - qc + NA_COLS - 1, 0, 2 * NA_COLS - 2)
    onehot = (ci[None] == np.arange(2 * NA_COLS - 1)[:, None, None]).astype(np.float32)
    cols = jnp.einsum("hab,bqk->haqk", rpb, jnp.asarray(onehot), precision=lax.Precision.HIGHEST)
    cols = jnp.where(jnp.asarray(col_ok)[None, None], cols, NEG_INF)
    cols = jnp.concatenate([cols, jnp.full((H_C, 1, GRID_W, GRID_W), NEG_INF, F32)], axis=1)
    block_index = _na_block_index()

    def assemble(cols_ref, out_ref):
        for ty in range(3):
            for qq in range(NA_TILE_ROWS):
                for kk in range(NA_KEY_ROWS):
                    out_ref[ty, 0, qq * GRID_W:(qq + 1) * GRID_W, kk * GRID_W:(kk + 1) * GRID_W] = (
                        cols_ref[0, int(block_index[ty, qq, kk])])

    return pl.pallas_call(
        assemble,
        out_shape=jax.ShapeDtypeStruct((3, H_C, NA_Q, NA_K), F32),
        grid=(H_C,),
        in_specs=[pl.BlockSpec((1, 2 * NA_ROWS, GRID_W, GRID_W), lambda h: (h, 0, 0, 0))],
        out_specs=pl.BlockSpec((3, 1, NA_Q, NA_K), lambda h: (0, h, 0, 0)),
        compiler_params=pltpu.CompilerParams(dimension_semantics=("arbitrary",)),
        name="na_bias_assemble",
    )(cols)


def _rope_tables():
    t = np.arange(DEC_SEQ)
    n_freq = HEAD_DIM // 4
    inv = (ROPE_BASE ** (-np.arange(n_freq, dtype=np.float32) / n_freq)).astype(np.float32)
    row = (t // GRID_W).astype(np.float32)[:, None] * inv
    col = (t % GRID_W).astype(np.float32)[:, None] * inv
    ang = np.concatenate([row, col], -1)
    cos, sin = np.cos(ang), np.sin(ang)
    cos_h = np.concatenate([cos, cos], -1)
    sin_h = np.concatenate([-sin, sin], -1)
    reps = LANES // HEAD_DIM
    return (jnp.asarray(np.tile(cos_h, (1, reps)), F32), jnp.asarray(np.tile(sin_h, (1, reps)), F32))


def _first_index_of(mask, iota, sentinel):
    return jnp.min(jnp.where(mask, iota, sentinel), axis=0, keepdims=True)


def _route(logits, b_col):
    n = logits.shape[1]
    scores = jax.nn.sigmoid(logits)
    sel = scores + b_col
    io_g = lax.broadcasted_iota(I32, (GROUP_SIZE, n), 0)
    gs_rows = []
    for g in range(N_GROUPS):
        s = sel[g * GROUP_SIZE:(g + 1) * GROUP_SIZE]
        m1 = jnp.max(s, axis=0, keepdims=True)
        i1 = _first_index_of(s == m1, io_g, GROUP_SIZE)
        m2 = jnp.max(jnp.where(io_g == i1, PICKED, s), axis=0, keepdims=True)
        gs_rows.append(m1 + m2)
    gs = jnp.concatenate(gs_rows, axis=0)
    io_n = lax.broadcasted_iota(I32, (N_GROUPS, n), 0)
    gsel = jnp.zeros((N_GROUPS, n), F32)
    for _ in range(TOPK_GROUPS):
        mg = jnp.max(gs, axis=0, keepdims=True)
        gi = _first_index_of(gs == mg, io_n, N_GROUPS)
        hit = io_n == gi
        gsel = jnp.where(hit, 1.0, gsel)
        gs = jnp.where(hit, PICKED, gs)
    cand = jnp.concatenate(
        [jnp.where(gsel[g:g + 1] > 0.5, sel[g * GROUP_SIZE:(g + 1) * GROUP_SIZE], NEG_INF)
         for g in range(N_GROUPS)], axis=0)
    io_e = lax.broadcasted_iota(I32, (N_EXPERTS, n), 0)
    picks, raw = [], []
    for _ in range(TOP_K):
        mv = jnp.max(cand, axis=0, keepdims=True)
        ei = _first_index_of(cand == mv, io_e, N_EXPERTS)
        hit = io_e == ei
        picks.append((hit, ei))
        raw.append(jnp.sum(jnp.where(hit, scores, 0.0), axis=0, keepdims=True))
        cand = jnp.where(hit, PICKED, cand)
    return picks, raw


def _post_mixer_kernel(xc_ref, xl_ref, mixc_ref, ya_ref, yb_ref, yc_ref, wout_ref, mod_ref, g_ref, b_ref,
                       wrh_ref, wrl_ref, br_ref,
                       x1_ref, h2_ref, eidx_ref, wsel_ref, rank_ref, cnt_ref):
    i = pl.program_id(0)
    tm = TM_TOK

    @pl.when(i == 0)
    def _():
        cnt_ref[...] = jnp.zeros_like(cnt_ref)

    ci = _cond_row(i, tm)
    gate1 = mod_ref[pl.ds(ci, 1), 2 * D_MODEL:3 * D_MODEL]
    sh2 = mod_ref[pl.ds(ci, 1), 3 * D_MODEL:4 * D_MODEL]
    sc2 = mod_ref[pl.ds(ci, 1), 4 * D_MODEL:5 * D_MODEL]
    mix_lat = jnp.concatenate([ya_ref[...], yb_ref[...], yc_ref[...]], axis=-1)
    mix = jnp.where(i < T_CTX // tm, mixc_ref[...], mix_lat)
    y = _dot(mix, wout_ref[...])
    x = jnp.where(i < T_CTX // tm, xc_ref[...], xl_ref[...])
    x1 = _layer_norm(ALPHA * x + gate1 * y, g_ref[...], b_ref[...])
    x1_ref[...] = x1
    h2 = x1 * (1.0 + sc2) + sh2
    h_hi = h2.astype(BF16)
    h2_ref[...] = h_hi
    h_lo = (h2 - h_hi.astype(F32)).astype(BF16)
    logits = (_dot_nt(wrh_ref[...], h_hi) + _dot_nt(wrh_ref[...], h_lo)
              + _dot_nt(wrl_ref[...], h_hi))
    routed = [_route(logits[:, g * LANES:(g + 1) * LANES], br_ref[...]) for g in range(tm // LANES)]
    multi_g = []
    for picks, _ in routed:
        m = jnp.zeros((N_EXPERTS, LANES), F32)
        for hit, _ in picks:
            m = m + jnp.where(hit, 1.0, 0.0)
        multi_g.append(m)
    multi = jnp.concatenate(multi_g, axis=1)
    before = (lax.broadcasted_iota(I32, (tm, tm), 0) < lax.broadcasted_iota(I32, (tm, tm), 1))
    cum = _dot(multi.astype(BF16), jnp.where(before, 1.0, 0.0).astype(BF16))
    pad = jnp.zeros((SUBLANES - TOP_K, LANES), F32)
    for g, (picks, raw) in enumerate(routed):
        lanes = slice(g * LANES, (g + 1) * LANES)
        total = raw[0]
        for r in raw[1:]:
            total = total + r
        scale = ROUTE_SCALE / total
        cum_g = cum[:, lanes]
        eidx_ref[:, lanes] = jnp.concatenate([ei for _, ei in picks] + [pad.astype(I32)], axis=0)
        wsel_ref[:, lanes] = jnp.concatenate([r * scale for r in raw] + [pad], axis=0)
        rank_ref[:, lanes] = jnp.concatenate(
            [jnp.sum(jnp.where(hit, cum_g, 0.0), axis=0, keepdims=True) for hit, _ in picks] + [pad],
            axis=0).astype(I32)
    tile_lane = lax.broadcasted_iota(I32, (N_EXPERTS, LANES), 1)
    cnt_ref[...] = jnp.where(tile_lane == i, jnp.sum(multi, axis=1, keepdims=True), cnt_ref[...])


def _post_mixer(x_ctx, x_lat, mix_c, ya, yb, yc, w_out_bf16, mod, ln_g, ln_b, wr_hi_t, wr_lo_t, b_router_col):
    n_ctx = T_CTX // TM_TOK
    ctx_map = lambda i: (jnp.minimum(i, n_ctx - 1), 0)
    lat_map = lambda i: (jnp.maximum(i - n_ctx, 0), 0)
    row_map = lambda i: (i, 0)
    const = lambda i: (0, 0)
    tok_map = lambda i: (0, i)
    return pl.pallas_call(
        _post_mixer_kernel,
        out_shape=(
            jax.ShapeDtypeStruct((T_ALL, D_MODEL), F32),
            jax.ShapeDtypeStruct((T_ALL, D_MODEL), BF16),
            jax.ShapeDtypeStruct((SUBLANES, T_ALL), I32),
            jax.ShapeDtypeStruct((SUBLANES, T_ALL), F32),
            jax.ShapeDtypeStruct((SUBLANES, T_ALL), I32),
            jax.ShapeDtypeStruct((N_EXPERTS, LANES), F32),
        ),
        grid=(N_TOK_TILES,),
        in_specs=[
            pl.BlockSpec((TM_TOK, D_MODEL), ctx_map),
            pl.BlockSpec((TM_TOK, D_MODEL), lat_map),
            pl.BlockSpec((TM_TOK, D_MODEL), ctx_map),
            pl.BlockSpec((TM_TOK, W_A), lat_map),
            pl.BlockSpec((TM_TOK, W_B), lat_map),
            pl.BlockSpec((TM_TOK, W_C), lat_map),
            pl.BlockSpec((D_MODEL, D_MODEL), const),
            pl.BlockSpec((N_COND, 6 * D_MODEL), const),
            pl.BlockSpec((1, D_MODEL), const),
            pl.BlockSpec((1, D_MODEL), const),
            pl.BlockSpec((N_EXPERTS, D_MODEL), const),
            pl.BlockSpec((N_EXPERTS, D_MODEL), const),
            pl.BlockSpec((N_EXPERTS, 1), const),
        ],
        out_specs=(
            pl.BlockSpec((TM_TOK, D_MODEL), row_map),
            pl.BlockSpec((TM_TOK, D_MODEL), row_map),
            pl.BlockSpec((SUBLANES, TM_TOK), tok_map),
            pl.BlockSpec((SUBLANES, TM_TOK), tok_map),
            pl.BlockSpec((SUBLANES, TM_TOK), tok_map),
            pl.BlockSpec((N_EXPERTS, LANES), const),
        ),
        compiler_params=pltpu.CompilerParams(
            dimension_semantics=("arbitrary",), vmem_limit_bytes=VMEM_LIMIT),
        name="post_mixer",
    )(x_ctx, x_lat, mix_c, ya, yb, yc, w_out_bf16, mod, ln_g, ln_b, wr_hi_t, wr_lo_t, b_router_col)


def _plan_kernel(eidx_ref, rank_ref, nmat_ref, lslot_ref, unit_ref, gend_ref):
    i = pl.program_id(0)
    units = jnp.floor((nmat_ref[...] + (UNIT - 1.0)) * (1.0 / UNIT))
    units_bf = units.astype(BF16)
    earlier_e = (lax.broadcasted_iota(I32, (N_EXPERTS, N_EXPERTS), 1)
                 < lax.broadcasted_iota(I32, (N_EXPERTS, N_EXPERTS), 0))
    tri_e = jnp.where(earlier_e, 1.0, 0.0).astype(BF16)
    earlier_t = (lax.broadcasted_iota(I32, (LANES, LANES), 0) < lax.broadcasted_iota(I32, (LANES, LANES), 1))
    tri_t = jnp.where(earlier_t, 1.0, 0.0).astype(BF16)
    local_off = _dot(tri_e, units_bf)
    tile_off = _dot(units_bf, tri_t)
    per_expert = jnp.sum(units, axis=1, keepdims=True)
    blocks = jnp.floor((per_expert + (UNITS_PER_BLOCK - 1.0)) * (1.0 / UNITS_PER_BLOCK))
    blocks_l = jnp.broadcast_to(blocks, (N_EXPERTS, LANES))
    start_blk = _dot(tri_e, blocks_l.astype(BF16))
    end_blk = start_blk + blocks_l
    gend_ref[...] = (end_blk * BM).astype(I32)

    this_tile = lax.broadcasted_iota(I32, (N_EXPERTS, LANES), 1) == i

    def column(a):
        return jnp.sum(jnp.where(this_tile, a, 0.0), axis=1, keepdims=True)

    lo, n_u = column(local_off), column(units)
    base_unit = start_blk[:, 0:1] * UNITS_PER_BLOCK + column(tile_off) - lo
    u = lax.broadcasted_iota(I32, (N_EXPERTS, MAX_UNITS), 1).astype(F32)
    inside = jnp.where(u >= lo, jnp.where(u < lo + n_u, 1.0, 0.0), 0.0)
    dst_unit = jnp.sum(inside * (base_unit + u), axis=0, keepdims=True)
    used = jnp.sum(inside, axis=0, keepdims=True) > 0.5
    spare = (SPARE_UNIT0 + (i % 2) * MAX_UNITS).astype(F32) + u[0:1, :]
    unit_ref[0] = jnp.where(used, dst_unit, spare).astype(I32)

    io_e = lax.broadcasted_iota(I32, (N_EXPERTS, TM_TOK), 0)
    rows = []
    for k in range(TOP_K):
        hit = io_e == eidx_ref[k:k + 1, :]
        seg = jnp.sum(jnp.where(hit, lo * UNIT, 0.0), axis=0, keepdims=True)
        rows.append(seg.astype(I32) + rank_ref[k:k + 1, :])
    rows.append(jnp.full((SUBLANES - TOP_K, TM_TOK), -1, I32))
    lslot_ref[...] = jnp.concatenate(rows, axis=0)


def _slot_plan(eidx, rank, nmat):
    tok_map = lambda i: (0, i)
    const = lambda i: (0, 0)
    lslot, unit_tab, gend = pl.pallas_call(
        _plan_kernel,
        out_shape=(
            jax.ShapeDtypeStruct((SUBLANES, T_ALL), I32),
            jax.ShapeDtypeStruct((N_TOK_TILES, 1, MAX_UNITS), I32),
            jax.ShapeDtypeStruct((N_EXPERTS, LANES), I32),
        ),
        grid=(N_TOK_TILES,),
        in_specs=[
            pl.BlockSpec((SUBLANES, TM_TOK), tok_map),
            pl.BlockSpec((SUBLANES, TM_TOK), tok_map),
            pl.BlockSpec((N_EXPERTS, LANES), const),
        ],
        out_specs=(
            pl.BlockSpec((SUBLANES, TM_TOK), tok_map),
            pl.BlockSpec((1, 1, MAX_UNITS), lambda i: (i, 0, 0)),
            pl.BlockSpec((N_EXPERTS, LANES), const),
        ),
        compiler_params=pltpu.CompilerParams(dimension_semantics=("arbitrary",)),
        name="slot_plan",
    )(eidx, rank, nmat)
    return lslot, unit_tab.reshape(N_TOK_TILES * MAX_UNITS), gend[:, 0]


PACK_W = D_MODEL // 2
HI_HALF = -65536


def _pack_pairs(x):
    lo = lax.bitcast_convert_type(x[:, 0:PACK_W], I32)
    hi = lax.bitcast_convert_type(x[:, PACK_W:D_MODEL], I32)
    return lax.shift_right_logical(lo, 16) | (hi & HI_HALF)


def _unpack_pairs(u):
    lo = lax.bitcast_convert_type(lax.shift_left(u, 16), F32).astype(BF16)
    hi = lax.bitcast_convert_type(u & HI_HALF, F32).astype(BF16)
    return lo, hi


def _unit_rows(unit):
    row = unit * UNIT
    return pl.ds(row if isinstance(unit, int) else pl.multiple_of(row, UNIT), UNIT)


def _unit_copy(src, src_unit, dst, dst_unit, sem):
    return pltpu.make_async_copy(src.at[_unit_rows(src_unit)], dst.at[_unit_rows(dst_unit)], sem)


def _dispatch_kernel(gend_ref, tab_ref, h2_ref, lslot_ref, xs_hbm, zero_ref, local_ref, sem_zero, sem_rows):
    i = pl.program_id(0)
    buf = i % 2

    def drain(b):
        pltpu.make_async_copy(local_ref.at[b], xs_hbm.at[pl.ds(0, LOCAL_ROWS)], sem_rows.at[b]).wait()

    def has_rows(e):
        return gend_ref[e] > jnp.where(e == 0, 0, gend_ref[jnp.maximum(e - 1, 0)])

    def zero_copy(e):
        return pltpu.make_async_copy(
            zero_ref, xs_hbm.at[pl.ds(pl.multiple_of(gend_ref[e] - BM, BM), BM)], sem_zero)

    @pl.when(i == 0)
    def _():
        zero_ref[...] = jnp.zeros_like(zero_ref)

        def start(e, c):
            @pl.when(has_rows(e))
            def _():
                zero_copy(e).start()
            return c

        def wait(e, c):
            @pl.when(has_rows(e))
            def _():
                zero_copy(e).wait()
            return c

        def tail_copy(blk):
            return pltpu.make_async_copy(
                zero_ref, xs_hbm.at[pl.ds(pl.multiple_of(blk * BM, BM), BM)], sem_zero)

        def start_tail(blk, c):
            tail_copy(blk).start()
            return c

        def wait_tail(blk, c):
            tail_copy(blk).wait()
            return c

        n_used = gend_ref[N_EXPERTS - 1] // BM
        lax.fori_loop(0, N_EXPERTS, start, 0)
        lax.fori_loop(n_used, N_BLOCKS_ALL, start_tail, 0)
        lax.fori_loop(0, N_EXPERTS, wait, 0)
        lax.fori_loop(n_used, N_BLOCKS_ALL, wait_tail, 0)

    @pl.when(i >= 2)
    def _():
        drain(buf)

    h2 = h2_ref[...]
    units_per_chunk = PERM_CHUNK // UNIT
    local = local_ref.at[buf]
    for c in range(LOCAL_ROWS // PERM_CHUNK):
        slot = c * PERM_CHUNK + lax.broadcasted_iota(I32, (PERM_CHUNK, TM_TOK), 0)
        p = jnp.zeros((PERM_CHUNK, TM_TOK), F32)
        for k in range(TOP_K):
            p = jnp.where(slot == lslot_ref[k:k + 1, :], 1.0, p)
        local[c * PERM_CHUNK:(c + 1) * PERM_CHUNK, :] = _pack_pairs(_dot(p.astype(BF16), h2))
        for u in range(c * units_per_chunk, (c + 1) * units_per_chunk):
            _unit_copy(local, u, xs_hbm, tab_ref[i * MAX_UNITS + u], sem_rows.at[buf]).start()

    @pl.when(i == N_TOK_TILES - 1)
    def _():
        drain(1 - buf)
        drain(buf)


def _dispatch(h2, lslot, unit_tab, gend):
    return pl.pallas_call(
        _dispatch_kernel,
        out_shape=jax.ShapeDtypeStruct((N_SLOTS, PACK_W), I32),
        grid_spec=pltpu.PrefetchScalarGridSpec(
            num_scalar_prefetch=2,
            grid=(N_TOK_TILES,),
            in_specs=[
                pl.BlockSpec((TM_TOK, D_MODEL), lambda i, ge, tab: (i, 0)),
                pl.BlockSpec((SUBLANES, TM_TOK), lambda i, ge, tab: (0, i)),
            ],
            out_specs=pl.BlockSpec(memory_space=pl.ANY),
            scratch_shapes=[
                pltpu.VMEM((BM, PACK_W), I32),
                pltpu.VMEM((2, LOCAL_ROWS, PACK_W), I32),
                pltpu.SemaphoreType.DMA,
                pltpu.SemaphoreType.DMA((2,)),
            ],
        ),
        compiler_params=pltpu.CompilerParams(
            dimension_semantics=("arbitrary",), vmem_limit_bytes=VMEM_LIMIT),
        name="moe_dispatch",
    )(gend, unit_tab, h2, lslot)


def _expert_kernel(gend_ref, xs_hbm, wgu_ref, wdown_ref, ys_hbm, xbuf, ybuf, wgu_bf, wdown_bf, sem_in, sem_out):
    e = pl.program_id(0)
    first = jnp.where(e == 0, 0, gend_ref[jnp.maximum(e - 1, 0)]) // BM
    n_blk = gend_ref[e] // BM - first

    def rows_of(j):
        return pl.ds(pl.multiple_of((first + j) * BM, BM), BM)

    def in_copy(j, b):
        return pltpu.make_async_copy(xs_hbm.at[rows_of(j)], xbuf.at[b], sem_in.at[b])

    def out_copy(j, b):
        return pltpu.make_async_copy(ybuf.at[b], ys_hbm.at[rows_of(j)], sem_out.at[b])

    @pl.when(n_blk > 0)
    def _():
        in_copy(0, 0).start()
        wgu_bf[...] = wgu_ref[0, 0].astype(BF16)
        wdown_bf[...] = wdown_ref[0, 0].astype(BF16)

    def block(j, carry):
        b = j % 2

        @pl.when(j + 1 < n_blk)
        def _():
            in_copy(j + 1, 1 - b).start()

        in_copy(j, b).wait()

        @pl.when(j >= 2)
        def _():
            out_copy(j - 2, b).wait()

        x_lo, x_hi = _unpack_pairs(xbuf[b])
        gu = _dot(x_lo, wgu_bf[0:PACK_W, :]) + _dot(x_hi, wgu_bf[PACK_W:D_MODEL, :])
        act = _silu(gu[:, 0:D_EXPERT]) * gu[:, D_EXPERT:2 * D_EXPERT]
        y = _dot(act.astype(BF16), wdown_bf[...])
        ybuf[b] = _pack_pairs(y.astype(BF16).astype(F32))
        out_copy(j, b).start()
        return carry

    lax.fori_loop(0, n_blk, block, 0)

    @pl.when(n_blk >= 2)
    def _():
        out_copy(n_blk - 2, n_blk % 2).wait()

    @pl.when(n_blk >= 1)
    def _():
        out_copy(n_blk - 1, (n_blk - 1) % 2).wait()

    @pl.when(e == N_EXPERTS - 1)
    def _():
        ybuf[0] = jnp.zeros((BM, PACK_W), I32)
        n_used = gend_ref[N_EXPERTS - 1] // BM

        def tail_copy(blk):
            return pltpu.make_async_copy(
                ybuf.at[0], ys_hbm.at[pl.ds(pl.multiple_of(blk * BM, BM), BM)], sem_out.at[0])

        def start(blk, c):
            tail_copy(blk).start()
            return c

        def wait(blk, c):
            tail_copy(blk).wait()
            return c

        lax.fori_loop(n_used, N_BLOCKS_ALL, start, 0)
        lax.fori_loop(n_used, N_BLOCKS_ALL, wait, 0)


def _experts(xs, w_gu, w_down, layer, gend):
    return pl.pallas_call(
        _expert_kernel,
        out_shape=jax.ShapeDtypeStruct((N_SLOTS, PACK_W), I32),
        grid_spec=pltpu.PrefetchScalarGridSpec(
            num_scalar_prefetch=1,
            grid=(N_EXPERTS,),
            in_specs=[
                pl.BlockSpec(memory_space=pl.ANY),
                pl.BlockSpec((1, 1, D_MODEL, 2 * D_EXPERT), lambda e, ge: (layer, e, 0, 0)),
                pl.BlockSpec((1, 1, D_EXPERT, D_MODEL), lambda e, ge: (layer, e, 0, 0)),
            ],
            out_specs=pl.BlockSpec(memory_space=pl.ANY),
            scratch_shapes=[
                pltpu.VMEM((2, BM, PACK_W), I32),
                pltpu.VMEM((2, BM, PACK_W), I32),
                pltpu.VMEM((D_MODEL, 2 * D_EXPERT), BF16),
                pltpu.VMEM((D_EXPERT, D_MODEL), BF16),
                pltpu.SemaphoreType.DMA((2,)),
                pltpu.SemaphoreType.DMA((2,)),
            ],
        ),
        compiler_params=pltpu.CompilerParams(
            dimension_semantics=("arbitrary",), vmem_limit_bytes=VMEM_LIMIT),
        name="moe_experts",
    )(gend, xs, w_gu, w_down)


def _combine_kernel(tab_ref, x1_ref, h2_ref, lslot_ref, wsel_ref, ys_hbm, wsgu_ref, wsdown_ref, mod_ref,
                    g_ref, b_ref, outc_ref, outl_ref, local_ref, sel_ref, ylo_ref, yhi_ref, sem_rows):
    i = pl.program_id(0)
    tm = TM_TOK
    buf = i % 2

    def fetch_unit(tile, b, u):
        _unit_copy(ys_hbm, tab_ref[tile * MAX_UNITS + u], local_ref.at[b], u, sem_rows.at[b]).start()

    def drain(b):
        pltpu.make_async_copy(ys_hbm.at[pl.ds(0, LOCAL_ROWS)], local_ref.at[b], sem_rows.at[b]).wait()

    @pl.when(i == 0)
    def _():
        def body(u, c):
            fetch_unit(0, 0, u)
            return c

        lax.fori_loop(0, MAX_UNITS, body, 0, unroll=8)

    sgu = _dot(h2_ref[...], wsgu_ref[...])
    act = _silu(sgu[:, 0:D_SHARED]) * sgu[:, D_SHARED:2 * D_SHARED]
    f = _dot(act.astype(BF16), wsdown_ref[...])

    nxt = jnp.minimum(i + 1, N_TOK_TILES - 1)
    n_groups = tm // SEL_ROWS
    units_per_group = MAX_UNITS // n_groups
    slot = lax.broadcasted_iota(I32, (SEL_ROWS, LOCAL_ROWS), 1)
    for g in range(n_groups):
        rows = slice(g * SEL_ROWS, (g + 1) * SEL_ROWS)
        sel = jnp.zeros((SEL_ROWS, LOCAL_ROWS), F32)
        for k in range(TOP_K):
            sel = jnp.where(slot == lslot_ref[rows, k:k + 1], wsel_ref[rows, k:k + 1], sel)
        sel_ref[rows, :] = sel.astype(BF16)
        for u in range(g * units_per_group, (g + 1) * units_per_group):
            fetch_unit(nxt, 1 - buf, u)

    local = local_ref.at[buf]
    drain(buf)
    for c in range(LOCAL_ROWS // PERM_CHUNK):
        rows = slice(c * PERM_CHUNK, (c + 1) * PERM_CHUNK)
        ylo_ref[rows, :], yhi_ref[rows, :] = _unpack_pairs(local[rows, :])
    sel = sel_ref[...]
    f = f + jnp.concatenate([_dot(sel, ylo_ref[...]), _dot(sel, yhi_ref[...])], axis=-1)
    ci = _cond_row(i, tm)
    gate2 = mod_ref[pl.ds(ci, 1), 5 * D_MODEL:6 * D_MODEL]
    out = _layer_norm(ALPHA * x1_ref[...] + gate2 * f, g_ref[...], b_ref[...])

    @pl.when(i < T_CTX // tm)
    def _():
        outc_ref[...] = out

    @pl.when(i >= T_CTX // tm)
    def _():
        outl_ref[...] = out

    @pl.when(i == N_TOK_TILES - 1)
    def _():
        drain(1 - buf)


def _combine(x1, h2, lslot_rows, wsel_rows, unit_tab, ys, w_sgu_bf16, w_sdown_bf16, mod, ln_g, ln_b):
    n_ctx = T_CTX // TM_TOK
    row_map = lambda i, tab: (i, 0)
    const = lambda i, tab: (0, 0)
    return pl.pallas_call(
        _combine_kernel,
        out_shape=(jax.ShapeDtypeStruct((T_CTX, D_MODEL), F32),
                   jax.ShapeDtypeStruct((T_LAT, D_MODEL), F32)),
        grid_spec=pltpu.PrefetchScalarGridSpec(
            num_scalar_prefetch=1,
            grid=(N_TOK_TILES,),
            in_specs=[
                pl.BlockSpec((TM_TOK, D_MODEL), row_map),
                pl.BlockSpec((TM_TOK, D_MODEL), row_map),
                pl.BlockSpec((TM_TOK, SUBLANES), row_map),
                pl.BlockSpec((TM_TOK, SUBLANES), row_map),
                pl.BlockSpec(memory_space=pl.ANY),
                pl.BlockSpec((D_MODEL, 2 * D_SHARED), const),
                pl.BlockSpec((D_SHARED, D_MODEL), const),
                pl.BlockSpec((N_COND, 6 * D_MODEL), const),
                pl.BlockSpec((1, D_MODEL), const),
                pl.BlockSpec((1, D_MODEL), const),
            ],
            out_specs=(pl.BlockSpec((TM_TOK, D_MODEL), lambda i, tab: (jnp.minimum(i, n_ctx - 1), 0)),
                       pl.BlockSpec((TM_TOK, D_MODEL), lambda i, tab: (jnp.maximum(i - n_ctx, 0), 0))),
            scratch_shapes=[
                pltpu.VMEM((2, LOCAL_ROWS, PACK_W), I32),
                pltpu.VMEM((TM_TOK, LOCAL_ROWS), BF16),
                pltpu.VMEM((LOCAL_ROWS, PACK_W), BF16),
                pltpu.VMEM((LOCAL_ROWS, PACK_W), BF16),
                pltpu.SemaphoreType.DMA((2,)),
            ],
        ),
        compiler_params=pltpu.CompilerParams(
            dimension_semantics=("arbitrary",), vmem_limit_bytes=VMEM_LIMIT),
        name="moe_combine",
    )(unit_tab, x1, h2, lslot_rows, wsel_rows, ys, w_sgu_bf16, w_sdown_bf16, mod, ln_g, ln_b)


def _lane_rows(v, width):
    return jnp.broadcast_to(v.astype(F32)[:, None, None], (v.shape[0], 1, width))


def kernel(x_prompt, x_sample, state_ret_fwd, state_ret_bwd, cache_win_k, cache_win_v, cache_na_k, cache_na_v, c, c_ctx, w_in, w_out, ret_decay_fwd, ret_decay_bwd, ret_gn_g, ret_gn_b, win_sink, na_rpb, w_mod, b_mod, ln1_g, ln1_b, ln2_g, ln2_b, w_router, b_router, w_expert_gu, w_expert_down, w_shared_gu, w_shared_down):
    cond = jnp.concatenate(
        [c_ctx[None, :], c, jnp.zeros((N_COND - 1 - DEC_BATCH, D_MODEL), F32)], axis=0)
    mod_all = _modulation(cond, w_mod, b_mod)
    cos_t, sin_t = _rope_tables()

    x_ctx = x_prompt.reshape(T_CTX, D_MODEL)
    x_lat = x_sample.reshape(T_LAT, D_MODEL)
    sf_l, sb_l, caches = [], [], ()
    for l in range(DEPTH):
        mod = mod_all[l]
        pa, pb, pc, *caches = _in_projection(x_ctx, x_lat, mod, w_in[l].astype(BF16), tuple(caches))
        decf_s, decb_s = _lane_rows(ret_decay_fwd[l], SEQ), _lane_rows(ret_decay_bwd[l], SEQ)
        gng, gnb = ret_gn_g[l][None, :], ret_gn_b[l][None, :]
        mix_c, st_f, st_b = _ctx_mixers(pa, pb, pc, win_sink[l], decf_s, decb_s, gng, gnb)
        sf_l.append(st_f)
        sb_l.append(st_b)
        ya = _lat_retention(pa, state_ret_fwd, state_ret_bwd, l,
                            _lane_rows(ret_decay_fwd[l], RET_CHUNK), _lane_rows(ret_decay_bwd[l], RET_CHUNK),
                            gng, gnb)
        yb = _lat_window_attn(pb, cache_win_k, cache_win_v, l, win_sink[l], cos_t, sin_t)
        yc = _lat_na_attn(pc, cache_na_k, cache_na_v, l, _na_maskbias(na_rpb[l]))

        wr_t = w_router[l].T
        wr_hi = wr_t.astype(BF16)
        wr_lo = (wr_t - wr_hi.astype(F32)).astype(BF16)
        x1, h2, eidx, wsel, rank, counts = _post_mixer(
            x_ctx, x_lat, mix_c, ya, yb, yc, w_out[l].astype(BF16), mod, ln1_g[l][None, :],
            ln1_b[l][None, :], wr_hi, wr_lo, b_router[l][:, None])
        lslot, unit_tab, gend = _slot_plan(eidx, rank, counts)
        xs = _dispatch(h2, lslot, unit_tab, gend)
        ys = _experts(xs, w_expert_gu, w_expert_down, l, gend)
        x_ctx, x_lat = _combine(x1, h2, lslot.T, wsel.T, unit_tab, ys, w_shared_gu[l].astype(BF16),
                                w_shared_down[l].astype(BF16), mod, ln2_g[l][None, :], ln2_b[l][None, :])

    y_prompt = x_ctx.reshape(BATCH, SEQ, D_MODEL)
    y_sample = x_lat.reshape(DEC_BATCH, DEC_SEQ, D_MODEL)
    new_sf = jnp.stack(sf_l, axis=1)
    new_sb = jnp.stack(sb_l, axis=1)

    return (y_prompt, y_sample, new_sf, new_sb, *caches)
```

```python
import functools

import numpy as np
import jax
import jax.numpy as jnp
from jax import lax
from jax.experimental import pallas as pl
from jax.experimental.pallas import tpu as pltpu

F32 = jnp.float32
BF16 = jnp.bfloat16
I32 = jnp.int32

D_MODEL = 1024
BATCH = 32
SEQ = 256
DEPTH = 2
DEC_BATCH = 4
DEC_SEQ = 2048
PAST_LEN = 256
GRID_W = 64
HEAD_DIM = 64
ATTN_SCALE = HEAD_DIM ** -0.5
H_A = 4
W_A = H_A * HEAD_DIM
GN_EPS = 1e-5
H_B = 6
KV_B = 2
W_B = H_B * HEAD_DIM
KV_W_B = KV_B * HEAD_DIM
WINDOW = 128
ROPE_BASE = 10000.0
H_C = 6
W_C = H_C * HEAD_DIM
NA_ROWS = 8
NA_COLS = 16
IN_WIDTH = 4 * W_A + W_B + 2 * KV_W_B + 3 * W_C
N_EXPERTS = 64
TOP_K = 6
N_GROUPS = 8
GROUP_SIZE = N_EXPERTS // N_GROUPS
TOPK_GROUPS = 4
D_EXPERT = 256
D_SHARED = 256
ROUTE_SCALE = 2.5
ALPHA = (2 * DEPTH) ** 0.25
LN_EPS = 1e-5
NEG_INF = -1e30
PICKED = -3e38

T_CTX = BATCH * SEQ
T_LAT = DEC_BATCH * DEC_SEQ
T_ALL = T_CTX + T_LAT
N_COND = 8

PA_W = 4 * W_A
PB_W = W_B + 2 * KV_W_B
PC_W = 3 * W_C

LANES = 128
SUBLANES = 8
VMEM_LIMIT = 56 * 1024 * 1024

TM_PROJ = 512
TM_TOK = 256
RET_CHUNK = 256
WIN_QB = 128
NA_TILE_ROWS = 4
NA_KEY_ROWS = 11
BM = 512
X_AHEAD = 2
X_BUFS = X_AHEAD + 1
UNIT = SUBLANES
UNITS_PER_BLOCK = BM // UNIT
PERM_CHUNK = 256
SEL_ROWS = 16
N_TOK_TILES = T_ALL // TM_TOK
LOCAL_ROWS = -(-(TM_TOK * TOP_K + N_EXPERTS * (UNIT - 1)) // PERM_CHUNK) * PERM_CHUNK
MAX_UNITS = LOCAL_ROWS // UNIT
N_ASSIGN = T_ALL * TOP_K
N_BLOCKS = -(-(N_ASSIGN + N_TOK_TILES * N_EXPERTS * (UNIT - 1) + N_EXPERTS * (BM - 1)) // BM)
SPARE_BLOCKS = 2 * -(-LOCAL_ROWS // BM)
SPARE_UNIT0 = N_BLOCKS * UNITS_PER_BLOCK
N_BLOCKS_ALL = N_BLOCKS + SPARE_BLOCKS
N_SLOTS = N_BLOCKS_ALL * BM


def _dot(a, b):
    return jnp.dot(a, b, preferred_element_type=F32)


def _dot_nt(a, b):
    return lax.dot_general(a, b, (((1,), (1,)), ((), ())), preferred_element_type=F32)


def _silu(x):
    return x * jax.nn.sigmoid(x)


def _log_sigmoid(x):
    return jnp.minimum(x, 0.0) - jnp.log(1.0 + jnp.exp(-jnp.abs(x)))


def _cond_row(tile, tile_rows):
    n_ctx = T_CTX // tile_rows
    per_lat = DEC_SEQ // tile_rows
    return jnp.where(tile < n_ctx, 0, 1 + (tile - n_ctx) // per_lat)


def _layer_norm(x, g, b):
    mu = jnp.mean(x, -1, keepdims=True)
    xc = x - mu
    var = jnp.mean(xc * xc, -1, keepdims=True)
    return xc * lax.rsqrt(var + LN_EPS) * g + b


MOD_TN = 1536


def _mod_kernel(cond_ref, w_ref, b_ref, o_ref):
    s = _silu(cond_ref[...])
    s_hi = s.astype(BF16)
    s_lo = (s - s_hi.astype(F32)).astype(BF16)
    w = w_ref[0]
    w_hi = w.astype(BF16)
    w_lo = (w - w_hi.astype(F32)).astype(BF16)
    o_ref[0] = _dot(s_hi, w_hi) + _dot(s_lo, w_hi) + _dot(s_hi, w_lo) + b_ref[0]


def _modulation(cond, w_mod, b_mod):
    n_out = 6 * D_MODEL
    return pl.pallas_call(
        _mod_kernel,
        out_shape=jax.ShapeDtypeStruct((DEPTH, N_COND, n_out), F32),
        grid=(DEPTH, n_out // MOD_TN),
        in_specs=[
            pl.BlockSpec((N_COND, D_MODEL), lambda l, j: (0, 0)),
            pl.BlockSpec((1, D_MODEL, MOD_TN), lambda l, j: (l, 0, j)),
            pl.BlockSpec((1, 1, MOD_TN), lambda l, j: (l, 0, j)),
        ],
        out_specs=pl.BlockSpec((1, N_COND, MOD_TN), lambda l, j: (l, 0, j)),
        compiler_params=pltpu.CompilerParams(
            dimension_semantics=("arbitrary", "arbitrary"), vmem_limit_bytes=VMEM_LIMIT),
        name="modulation",
    )(cond, w_mod, b_mod.reshape(DEPTH, 1, n_out))


SEQ_PER_PROJ = TM_PROJ // SEQ


def _inproj_kernel(n_prev, xc_ref, xl_ref, mod_ref, w_ref, *refs):
    prev_refs = refs[:4] if n_prev else ()
    pa_ref, pb_ref, pc_ref, wk_ref, wv_ref, nk_ref, nv_ref = refs[len(prev_refs):]
    i = pl.program_id(0)
    ci = _cond_row(i, TM_PROJ)
    sh = mod_ref[pl.ds(ci, 1), 0:D_MODEL]
    sc = mod_ref[pl.ds(ci, 1), D_MODEL:2 * D_MODEL]
    x = jnp.where(i < T_CTX // TM_PROJ, xc_ref[...], xl_ref[...])
    h = x * (1.0 + sc) + sh
    p = _dot(h.astype(BF16), w_ref[...])
    pa_ref[...] = p[:, 0:PA_W].astype(BF16)
    pb_ref[...] = p[:, PA_W:PA_W + PB_W].astype(BF16)
    pc_ref[...] = p[:, PA_W + PB_W:IN_WIDTH].astype(BF16)

    @pl.when(i < T_CTX // TM_PROJ)
    def _():
        targets = ((wk_ref, PA_W + W_B, KV_B), (wv_ref, PA_W + W_B + KV_W_B, KV_B),
                   (nk_ref, PA_W + PB_W + W_C, H_C), (nv_ref, PA_W + PB_W + 2 * W_C, H_C))
        for j, (ref, col0, n_heads) in enumerate(targets):
            if n_prev:
                ref[:, 0:n_prev] = prev_refs[j][...]
            for s in range(SEQ_PER_PROJ):
                for hh in range(n_heads):
                    ref[s, n_prev, hh] = p[s * SEQ:(s + 1) * SEQ,
                                           col0 + hh * HEAD_DIM:col0 + (hh + 1) * HEAD_DIM]


def _in_projection(x_ctx, x_lat, mod, w_in_bf16, earlier):
    n_ctx_tiles = T_CTX // TM_PROJ
    n_prev = earlier[0].shape[1] if earlier else 0

    def cache_spec(n_layers, n_heads):
        return pl.BlockSpec((SEQ_PER_PROJ, n_layers, n_heads, SEQ, HEAD_DIM),
                            lambda i: (jnp.minimum(i, n_ctx_tiles - 1), 0, 0, 0, 0))

    cache_heads = (KV_B, KV_B, H_C, H_C)
    return pl.pallas_call(
        functools.partial(_inproj_kernel, n_prev),
        out_shape=(
            jax.ShapeDtypeStruct((T_ALL, PA_W), BF16),
            jax.ShapeDtypeStruct((T_ALL, PB_W), BF16),
            jax.ShapeDtypeStruct((T_ALL, PC_W), BF16),
        ) + tuple(jax.ShapeDtypeStruct((BATCH, n_prev + 1, nh, SEQ, HEAD_DIM), F32) for nh in cache_heads),
        grid=(T_ALL // TM_PROJ,),
        in_specs=[
            pl.BlockSpec((TM_PROJ, D_MODEL), lambda i: (jnp.minimum(i, n_ctx_tiles - 1), 0)),
            pl.BlockSpec((TM_PROJ, D_MODEL), lambda i: (jnp.maximum(i - n_ctx_tiles, 0), 0)),
            pl.BlockSpec((N_COND, 6 * D_MODEL), lambda i: (0, 0)),
            pl.BlockSpec((D_MODEL, IN_WIDTH), lambda i: (0, 0)),
        ] + [cache_spec(n_prev, nh) for nh in cache_heads if n_prev],
        out_specs=(
            pl.BlockSpec((TM_PROJ, PA_W), lambda i: (i, 0)),
            pl.BlockSpec((TM_PROJ, PB_W), lambda i: (i, 0)),
            pl.BlockSpec((TM_PROJ, PC_W), lambda i: (i, 0)),
        ) + tuple(cache_spec(n_prev + 1, nh) for nh in cache_heads),
        compiler_params=pltpu.CompilerParams(
            dimension_semantics=("arbitrary",), vmem_limit_bytes=VMEM_LIMIT),
        name="in_projection",
    )(x_ctx, x_lat, mod, w_in_bf16, *earlier)


def _decay_matrix(lg_f, lg_b, n):
    row = lax.broadcasted_iota(I32, (n, n), 0)
    col = lax.broadcasted_iota(I32, (n, n), 1)
    diff = (row - col).astype(F32)
    fwd = jnp.where(diff >= 0, jnp.exp(lg_f * jnp.maximum(diff, 0.0)), 0.0)
    bwd = jnp.where(diff <= 0, jnp.exp(lg_b * jnp.maximum(-diff, 0.0)), 0.0)
    return (fwd + bwd) * ATTN_SCALE


def _retention_readout(o, gate, g, b):
    mu = jnp.mean(o, -1, keepdims=True)
    oc = o - mu
    var = jnp.mean(oc * oc, -1, keepdims=True)
    on = oc * lax.rsqrt(var + GN_EPS) * g + b
    return on * _silu(gate.astype(F32))


def _softmax_attend(s, v, extra_logit=None):
    m = jnp.max(s, -1, keepdims=True)
    if extra_logit is not None:
        m = jnp.maximum(m, extra_logit)
    p = jnp.exp(s - m)
    den = jnp.sum(p, -1, keepdims=True)
    if extra_logit is not None:
        den = den + jnp.exp(extra_logit - m)
    return _dot(p.astype(BF16), v) / den


def _ctx_mixer_kernel(sink_ref, pa_ref, pb_ref, pc_ref, decf_ref, decb_ref, gng_ref, gnb_ref,
                      mix_ref, sf_ref, sb_ref):
    n = SEQ
    hd = HEAD_DIM
    pos = lax.broadcasted_iota(I32, (n, hd), 0).astype(F32)
    for h in range(H_A):
        c0 = h * hd
        q = pa_ref[:, c0:c0 + hd]
        k = pa_ref[:, W_A + c0:W_A + c0 + hd]
        v = pa_ref[:, 2 * W_A + c0:2 * W_A + c0 + hd]
        gate = pa_ref[:, 3 * W_A + c0:3 * W_A + c0 + hd]
        lg_f = _log_sigmoid(decf_ref[h])
        lg_b = _log_sigmoid(decb_ref[h])
        dmat = _decay_matrix(lg_f, lg_b, n)
        o = _dot((_dot_nt(q, k) * dmat).astype(BF16), v)
        kf = k.astype(F32)
        zf = jnp.exp(lg_f[:, 0:hd] * (n - 1.0 - pos)) * ATTN_SCALE
        zb = jnp.exp(lg_b[:, 0:hd] * pos) * ATTN_SCALE
        sf_ref[0, h] = _dot((kf * zf).T.astype(BF16), v)
        sb_ref[0, h] = _dot((kf * zb).T.astype(BF16), v)
        y = _retention_readout(o, gate, gng_ref[:, c0:c0 + hd], gnb_ref[:, c0:c0 + hd])
        mix_ref[:, c0:c0 + hd] = y.astype(BF16)
    for j in range(KV_B):
        k = pb_ref[:, W_B + j * hd:W_B + (j + 1) * hd]
        v = pb_ref[:, W_B + KV_W_B + j * hd:W_B + KV_W_B + (j + 1) * hd]
        for g in range(H_B // KV_B):
            hh = j * (H_B // KV_B) + g
            q = pb_ref[:, hh * hd:(hh + 1) * hd]
            o = _softmax_attend(_dot_nt(q, k) * ATTN_SCALE, v, sink_ref[hh])
            mix_ref[:, W_A + hh * hd:W_A + (hh + 1) * hd] = o.astype(BF16)
    for hh in range(H_C):
        q = pc_ref[:, hh * hd:(hh + 1) * hd]
        k = pc_ref[:, W_C + hh * hd:W_C + (hh + 1) * hd]
        v = pc_ref[:, 2 * W_C + hh * hd:2 * W_C + (hh + 1) * hd]
        o = _softmax_attend(_dot_nt(q, k) * ATTN_SCALE, v)
        mix_ref[:, W_A + W_B + hh * hd:W_A + W_B + (hh + 1) * hd] = o.astype(BF16)


def _ctx_mixers(pa, pb, pc, sink, decf, decb, gng, gnb):
    return pl.pallas_call(
        _ctx_mixer_kernel,
        out_shape=(
            jax.ShapeDtypeStruct((T_CTX, D_MODEL), BF16),
            jax.ShapeDtypeStruct((BATCH, H_A, HEAD_DIM, HEAD_DIM), F32),
            jax.ShapeDtypeStruct((BATCH, H_A, HEAD_DIM, HEAD_DIM), F32),
        ),
        grid=(BATCH,),
        in_specs=[
            pl.BlockSpec(memory_space=pltpu.SMEM),
            pl.BlockSpec((SEQ, PA_W), lambda b: (b, 0)),
            pl.BlockSpec((SEQ, PB_W), lambda b: (b, 0)),
            pl.BlockSpec((SEQ, PC_W), lambda b: (b, 0)),
            pl.BlockSpec((H_A, 1, SEQ), lambda b: (0, 0, 0)),
            pl.BlockSpec((H_A, 1, SEQ), lambda b: (0, 0, 0)),
            pl.BlockSpec((1, W_A), lambda b: (0, 0)),
            pl.BlockSpec((1, W_A), lambda b: (0, 0)),
        ],
        out_specs=(
            pl.BlockSpec((SEQ, D_MODEL), lambda b: (b, 0)),
            pl.BlockSpec((1, H_A, HEAD_DIM, HEAD_DIM), lambda b: (b, 0, 0, 0)),
            pl.BlockSpec((1, H_A, HEAD_DIM, HEAD_DIM), lambda b: (b, 0, 0, 0)),
        ),
        compiler_params=pltpu.CompilerParams(
            dimension_semantics=("arbitrary",), vmem_limit_bytes=VMEM_LIMIT),
        name="ctx_mixers",
    )(sink, pa, pb, pc, decf, decb, gng, gnb)


def _lat_ret_kernel(pa_ref, stf_ref, stb_ref, decf_ref, decb_ref, gng_ref, gnb_ref, ya_ref, acc_ref):
    c = RET_CHUNK
    hd = HEAD_DIM
    n_chunks = DEC_SEQ // c
    pos = lax.broadcasted_iota(I32, (c, hd), 0).astype(F32)
    for h in range(H_A):
        c0 = h * hd
        lg_f = _log_sigmoid(decf_ref[h])
        lg_b = _log_sigmoid(decb_ref[h])
        dmat = _decay_matrix(lg_f, lg_b, c)
        lf = lg_f[:, 0:hd]
        lb = lg_b[:, 0:hd]
        zf = jnp.exp(lf * (c - 1.0 - pos)) * ATTN_SCALE
        zb = jnp.exp(lb * pos) * ATTN_SCALE
        xf = jnp.exp(lf * (pos + 1.0))
        xb = jnp.exp(lb * (c - pos))
        gcf = jnp.exp(lf * float(c))
        gcb = jnp.exp(lb * float(c))
        g = gng_ref[:, c0:c0 + hd]
        b = gnb_ref[:, c0:c0 + hd]

        def load(i, off):
            rows = pl.ds(pl.multiple_of(i * c, c), c)
            return pa_ref[rows, off + c0:off + c0 + hd]

        def fwd(i, s):
            q, k, v = load(i, 0), load(i, W_A), load(i, 2 * W_A)
            o = _dot((_dot_nt(q, k) * dmat).astype(BF16), v)
            o = o + _dot((q.astype(F32) * xf).astype(BF16), s.astype(BF16))
            acc_ref[pl.ds(pl.multiple_of(i * c, c), c), c0:c0 + hd] = o
            return gcf * s + _dot((k.astype(F32) * zf).T.astype(BF16), v)

        lax.fori_loop(0, n_chunks, fwd, stf_ref[0, 0, h])

        def bwd(j, s):
            i = n_chunks - 1 - j
            rows = pl.ds(pl.multiple_of(i * c, c), c)
            q, k, v = load(i, 0), load(i, W_A), load(i, 2 * W_A)
            o = acc_ref[rows, c0:c0 + hd] + _dot((q.astype(F32) * xb).astype(BF16), s.astype(BF16))
            y = _retention_readout(o, load(i, 3 * W_A), g, b)
            ya_ref[rows, c0:c0 + hd] = y.astype(BF16)
            return gcb * s + _dot((k.astype(F32) * zb).T.astype(BF16), v)

        lax.fori_loop(0, n_chunks, bwd, stb_ref[0, 0, h])


def _lat_retention(pa, st_f, st_b, layer, decf, decb, gng, gnb):
    lat0 = T_CTX // DEC_SEQ
    st_spec = pl.BlockSpec((1, 1, H_A, HEAD_DIM, HEAD_DIM), lambda b: (b, layer, 0, 0, 0))
    return pl.pallas_call(
        _lat_ret_kernel,
        out_shape=jax.ShapeDtypeStruct((T_LAT, W_A), BF16),
        grid=(DEC_BATCH,),
        in_specs=[
            pl.BlockSpec((DEC_SEQ, PA_W), lambda b: (lat0 + b, 0)),
            st_spec, st_spec,
            pl.BlockSpec((H_A, 1, RET_CHUNK), lambda b: (0, 0, 0)),
            pl.BlockSpec((H_A, 1, RET_CHUNK), lambda b: (0, 0, 0)),
            pl.BlockSpec((1, W_A), lambda b: (0, 0)),
            pl.BlockSpec((1, W_A), lambda b: (0, 0)),
        ],
        out_specs=pl.BlockSpec((DEC_SEQ, W_A), lambda b: (b, 0)),
        scratch_shapes=[pltpu.VMEM((DEC_SEQ, W_A), F32)],
        compiler_params=pltpu.CompilerParams(
            dimension_semantics=("arbitrary",), vmem_limit_bytes=VMEM_LIMIT),
        name="lat_retention",
    )(pa, st_f, st_b, decf, decb, gng, gnb)


def _swap_halves_matrix(width):
    r = lax.broadcasted_iota(I32, (width, width), 0)
    c = lax.broadcasted_iota(I32, (width, width), 1)
    return jnp.where((r ^ (HEAD_DIM // 2)) == c, 1.0, 0.0).astype(BF16)


def _rope(x, cos, sin_signed, swap):
    return x.astype(F32) * cos + _dot(x, swap) * sin_signed


def _lat_win_kernel(sink_ref, pq_ref, pseq_ref, kctx_ref, vctx_ref, cos_ref, sin_ref, yb_ref, krope_ref):
    n = pl.program_id(1)
    hd = HEAD_DIM
    qb = WIN_QB
    n_blk = DEC_SEQ // qb
    group = H_B // KV_B
    swap = _swap_halves_matrix(LANES)

    @pl.when(n == 0)
    def _():
        k = pseq_ref[:, W_B:W_B + KV_W_B]
        krope_ref[...] = _rope(k, cos_ref[...], sin_ref[...], swap).astype(BF16)

    q_rows = pl.ds(pl.multiple_of(n * qb, qb), qb)
    cos_q = cos_ref[q_rows, :]
    sin_q = sin_ref[q_rows, :]
    qr = [_rope(pq_ref[:, p * LANES:(p + 1) * LANES], cos_q, sin_q, swap).astype(BF16)
          for p in range(W_B // LANES)]

    ws = jnp.clip(n - 1, 0, n_blk - 3) * qb
    k_rows = pl.ds(pl.multiple_of(ws, qb), 3 * qb)
    q_pos = n * qb + lax.broadcasted_iota(I32, (group * qb, 3 * qb), 0) % qb
    k_pos = ws + lax.broadcasted_iota(I32, (group * qb, 3 * qb), 1)
    valid = jnp.abs(k_pos - q_pos) <= WINDOW
    head_of_row = lax.broadcasted_iota(I32, (group * qb, 1), 0) // qb
    for j in range(KV_B):
        heads = [j * group + g for g in range(group)]
        qs = jnp.concatenate(
            [qr[hh // 2][:, (hh % 2) * hd:(hh % 2 + 1) * hd] for hh in heads], axis=0)
        kw = krope_ref[k_rows, j * hd:(j + 1) * hd]
        vw = pseq_ref[k_rows, W_B + KV_W_B + j * hd:W_B + KV_W_B + (j + 1) * hd]
        kc = kctx_ref[0, 0, j].astype(BF16)
        vc = vctx_ref[0, 0, j].astype(BF16)
        s_loc = jnp.where(valid, _dot_nt(qs, kw) * ATTN_SCALE, NEG_INF)
        s_ctx = _dot_nt(qs, kc) * ATTN_SCALE
        sink = jnp.zeros((group * qb, 1), F32)
        for g, hh in enumerate(heads):
            sink = jnp.where(head_of_row == g, sink_ref[hh], sink)
        m = jnp.maximum(jnp.maximum(jnp.max(s_loc, -1, keepdims=True),
                                    jnp.max(s_ctx, -1, keepdims=True)), sink)
        p_loc = jnp.exp(s_loc - m)
        p_ctx = jnp.exp(s_ctx - m)
        den = (jnp.sum(p_loc, -1, keepdims=True) + jnp.sum(p_ctx, -1, keepdims=True)
               + jnp.exp(sink - m))
        o = (_dot(p_loc.astype(BF16), vw) + _dot(p_ctx.astype(BF16), vc)) / den
        for g, hh in enumerate(heads):
            yb_ref[:, hh * hd:(hh + 1) * hd] = o[g * qb:(g + 1) * qb].astype(BF16)


def _lat_window_attn(pb, cache_k, cache_v, layer, sink, cos_t, sin_t):
    n_blk = DEC_SEQ // WIN_QB
    lat_blk0 = T_CTX // WIN_QB
    lat_seq0 = T_CTX // DEC_SEQ
    ctx_spec = pl.BlockSpec((1, 1, KV_B, PAST_LEN, HEAD_DIM), lambda b, n: (b, layer, 0, 0, 0))
    return pl.pallas_call(
        _lat_win_kernel,
        out_shape=jax.ShapeDtypeStruct((T_LAT, W_B), BF16),
        grid=(DEC_BATCH, n_blk),
        in_specs=[
            pl.BlockSpec(memory_space=pltpu.SMEM),
            pl.BlockSpec((WIN_QB, PB_W), lambda b, n: (lat_blk0 + b * n_blk + n, 0)),
            pl.BlockSpec((DEC_SEQ, PB_W), lambda b, n: (lat_seq0 + b, 0)),
            ctx_spec, ctx_spec,
            pl.BlockSpec((DEC_SEQ, LANES), lambda b, n: (0, 0)),
            pl.BlockSpec((DEC_SEQ, LANES), lambda b, n: (0, 0)),
        ],
        out_specs=pl.BlockSpec((WIN_QB, W_B), lambda b, n: (b * n_blk + n, 0)),
        scratch_shapes=[pltpu.VMEM((DEC_SEQ, KV_W_B), BF16)],
        compiler_params=pltpu.CompilerParams(
            dimension_semantics=("arbitrary", "arbitrary"), vmem_limit_bytes=VMEM_LIMIT),
        name="lat_window_attn",
    )(sink, pb, pb, cache_k, cache_v, cos_t, sin_t)


NA_Q = NA_TILE_ROWS * GRID_W
NA_K = NA_KEY_ROWS * GRID_W
NA_TILES = DEC_SEQ // NA_Q
LAT_ROWS = DEC_SEQ // GRID_W


def _na_window_start(tile):
    return jnp.clip(tile * NA_TILE_ROWS - NA_ROWS // 2, 0, LAT_ROWS - NA_KEY_ROWS)


def _lat_na_kernel(pq_ref, pseq_ref, kctx_ref, vctx_ref, bias_ref, yc_ref):
    t = pl.program_id(1)
    hd = HEAD_DIM
    k_rows = pl.ds(pl.multiple_of(_na_window_start(t) * GRID_W, GRID_W), NA_K)
    for hh in range(H_C):
        q = pq_ref[:, hh * hd:(hh + 1) * hd]
        kw = pseq_ref[k_rows, W_C + hh * hd:W_C + (hh + 1) * hd]
        vw = pseq_ref[k_rows, 2 * W_C + hh * hd:2 * W_C + (hh + 1) * hd]
        kc = kctx_ref[0, 0, hh].astype(BF16)
        vc = vctx_ref[0, 0, hh].astype(BF16)
        s_loc = _dot_nt(q, kw) * ATTN_SCALE + bias_ref[0, hh]
        s_ctx = _dot_nt(q, kc) * ATTN_SCALE
        m = jnp.maximum(jnp.max(s_loc, -1, keepdims=True), jnp.max(s_ctx, -1, keepdims=True))
        p_loc = jnp.exp(s_loc - m)
        p_ctx = jnp.exp(s_ctx - m)
        den = jnp.sum(p_loc, -1, keepdims=True) + jnp.sum(p_ctx, -1, keepdims=True)
        o = (_dot(p_loc.astype(BF16), vw) + _dot(p_ctx.astype(BF16), vc)) / den
        yc_ref[:, hh * hd:(hh + 1) * hd] = o.astype(BF16)


def _na_tile_type(t):
    return jnp.where(t == 0, 0, jnp.where(t == NA_TILES - 1, 2, 1))


def _lat_na_attn(pc, cache_k, cache_v, layer, maskbias):
    lat_tile0 = T_CTX // NA_Q
    lat_seq0 = T_CTX // DEC_SEQ
    ctx_spec = pl.BlockSpec((1, 1, H_C, PAST_LEN, HEAD_DIM), lambda b, t: (b, layer, 0, 0, 0))
    return pl.pallas_call(
        _lat_na_kernel,
        out_shape=jax.ShapeDtypeStruct((T_LAT, W_C), BF16),
        grid=(DEC_BATCH, NA_TILES),
        in_specs=[
            pl.BlockSpec((NA_Q, PC_W), lambda b, t: (lat_tile0 + b * NA_TILES + t, 0)),
            pl.BlockSpec((DEC_SEQ, PC_W), lambda b, t: (lat_seq0 + b, 0)),
            ctx_spec, ctx_spec,
            pl.BlockSpec((1, H_C, NA_Q, NA_K), lambda b, t: (_na_tile_type(t), 0, 0, 0)),
        ],
        out_specs=pl.BlockSpec((NA_Q, W_C), lambda b, t: (b * NA_TILES + t, 0)),
        compiler_params=pltpu.CompilerParams(
            dimension_semantics=("arbitrary", "arbitrary"), vmem_limit_bytes=VMEM_LIMIT),
        name="lat_na_attn",
    )(pc, pc, cache_k, cache_v, maskbias)


def _na_block_index():
    out = np.zeros((3, NA_TILE_ROWS, NA_KEY_ROWS), np.int32)
    for ty, tile in enumerate((0, 1, NA_TILES - 1)):
        r = tile * NA_TILE_ROWS
        ws = int(np.clip(r - NA_ROWS // 2, 0, LAT_ROWS - NA_KEY_ROWS))
        for qq in range(NA_TILE_ROWS):
            qr = r + qq
            r0 = int(np.clip(qr - NA_ROWS // 2, 0, LAT_ROWS - NA_ROWS))
            for kk in range(NA_KEY_ROWS):
                kr = ws + kk
                out[ty, qq, kk] = kr - qr + NA_ROWS - 1 if r0 <= kr < r0 + NA_ROWS else 2 * NA_ROWS - 1
    return out


def _na_maskbias(rpb):
    qc = np.arange(GRID_W)[:, None]
    kc = np.arange(GRID_W)[None, :]
    c0 = np.clip(qc - NA_COLS // 2, 0, GRID_W - NA_COLS)
    col_ok = (kc >= c0) & (kc < c0 + NA_COLS)
    ci = np.clip(kc - qc + NA_COLS - 1, 0, 2 * NA_COLS - 2)
    onehot = (ci[None] == np.arange(2 * NA_COLS - 1)[:, None, None]).astype(np.float32)
    cols = jnp.einsum("hab,bqk->haqk", rpb, jnp.asarray(onehot), precision=lax.Precision.HIGHEST)
    cols = jnp.where(jnp.asarray(col_ok)[None, None], cols, NEG_INF)
    cols = jnp.concatenate([cols, jnp.full((H_C, 1, GRID_W, GRID_W), NEG_INF, F32)], axis=1)
    block_index = _na_block_index()

    def assemble(cols_ref, out_ref):
        for ty in range(3):
            for qq in range(NA_TILE_ROWS):
                for kk in range(NA_KEY_ROWS):
                    out_ref[ty, 0, qq * GRID_W:(qq + 1) * GRID_W, kk * GRID_W:(kk + 1) * GRID_W] = (
                        cols_ref[0, int(block_index[ty, qq, kk])])

    return pl.pallas_call(
        assemble,
        out_shape=jax.ShapeDtypeStruct((3, H_C, NA_Q, NA_K), F32),
        grid=(H_C,),
        in_specs=[pl.BlockSpec((1, 2 * NA_ROWS, GRID_W, GRID_W), lambda h: (h, 0, 0, 0))],
        out_specs=pl.BlockSpec((3, 1, NA_Q, NA_K), lambda h: (0, h, 0, 0)),
        compiler_params=pltpu.CompilerParams(dimension_semantics=("arbitrary",)),
        name="na_bias_assemble",
    )(cols)


def _rope_tables():
    t = np.arange(DEC_SEQ)
    n_freq = HEAD_DIM // 4
    inv = (ROPE_BASE ** (-np.arange(n_freq, dtype=np.float32) / n_freq)).astype(np.float32)
    row = (t // GRID_W).astype(np.float32)[:, None] * inv
    col = (t % GRID_W).astype(np.float32)[:, None] * inv
    ang = np.concatenate([row, col], -1)
    cos, sin = np.cos(ang), np.sin(ang)
    cos_h = np.concatenate([cos, cos], -1)
    sin_h = np.concatenate([-sin, sin], -1)
    reps = LANES // HEAD_DIM
    return (jnp.asarray(np.tile(cos_h, (1, reps)), F32), jnp.asarray(np.tile(sin_h, (1, reps)), F32))


def _first_index_of(mask, iota, sentinel):
    return jnp.min(jnp.where(mask, iota, sentinel), axis=0, keepdims=True)


def _route(logits, b_col):
    n = logits.shape[1]
    scores = jax.nn.sigmoid(logits)
    sel = scores + b_col
    io_g = lax.broadcasted_iota(I32, (GROUP_SIZE, n), 0)
    gs_rows = []
    for g in range(N_GROUPS):
        s = sel[g * GROUP_SIZE:(g + 1) * GROUP_SIZE]
        m1 = jnp.max(s, axis=0, keepdims=True)
        i1 = _first_index_of(s == m1, io_g, GROUP_SIZE)
        m2 = jnp.max(jnp.where(io_g == i1, PICKED, s), axis=0, keepdims=True)
        gs_rows.append(m1 + m2)
    gs = jnp.concatenate(gs_rows, axis=0)
    io_n = lax.broadcasted_iota(I32, (N_GROUPS, n), 0)
    gsel = jnp.zeros((N_GROUPS, n), F32)
    for _ in range(TOPK_GROUPS):
        mg = jnp.max(gs, axis=0, keepdims=True)
        gi = _first_index_of(gs == mg, io_n, N_GROUPS)
        hit = io_n == gi
        gsel = jnp.where(hit, 1.0, gsel)
        gs = jnp.where(hit, PICKED, gs)
    cand = jnp.concatenate(
        [jnp.where(gsel[g:g + 1] > 0.5, sel[g * GROUP_SIZE:(g + 1) * GROUP_SIZE], NEG_INF)
         for g in range(N_GROUPS)], axis=0)
    io_e = lax.broadcasted_iota(I32, (N_EXPERTS, n), 0)
    picks, raw = [], []
    for _ in range(TOP_K):
        mv = jnp.max(cand, axis=0, keepdims=True)
        ei = _first_index_of(cand == mv, io_e, N_EXPERTS)
        hit = io_e == ei
        picks.append((hit, ei))
        raw.append(jnp.sum(jnp.where(hit, scores, 0.0), axis=0, keepdims=True))
        cand = jnp.where(hit, PICKED, cand)
    return picks, raw


def _post_mixer_kernel(xc_ref, xl_ref, mixc_ref, ya_ref, yb_ref, yc_ref, wout_ref, mod_ref, g_ref, b_ref,
                       wrh_ref, wrl_ref, br_ref,
                       x1_ref, h2_ref, eidx_ref, wsel_ref, rank_ref, cnt_ref):
    i = pl.program_id(0)
    tm = TM_TOK

    @pl.when(i == 0)
    def _():
        cnt_ref[...] = jnp.zeros_like(cnt_ref)

    ci = _cond_row(i, tm)
    gate1 = mod_ref[pl.ds(ci, 1), 2 * D_MODEL:3 * D_MODEL]
    sh2 = mod_ref[pl.ds(ci, 1), 3 * D_MODEL:4 * D_MODEL]
    sc2 = mod_ref[pl.ds(ci, 1), 4 * D_MODEL:5 * D_MODEL]
    mix_lat = jnp.concatenate([ya_ref[...], yb_ref[...], yc_ref[...]], axis=-1)
    mix = jnp.where(i < T_CTX // tm, mixc_ref[...], mix_lat)
    y = _dot(mix, wout_ref[...])
    x = jnp.where(i < T_CTX // tm, xc_ref[...], xl_ref[...])
    x1 = _layer_norm(ALPHA * x + gate1 * y, g_ref[...], b_ref[...])
    x1_ref[...] = x1
    h2 = x1 * (1.0 + sc2) + sh2
    h_hi = h2.astype(BF16)
    h2_ref[...] = h_hi
    h_lo = (h2 - h_hi.astype(F32)).astype(BF16)
    logits = (_dot_nt(wrh_ref[...], h_hi) + _dot_nt(wrh_ref[...], h_lo)
              + _dot_nt(wrl_ref[...], h_hi))
    routed = [_route(logits[:, g * LANES:(g + 1) * LANES], br_ref[...]) for g in range(tm // LANES)]
    multi_g = []
    for picks, _ in routed:
        m = jnp.zeros((N_EXPERTS, LANES), F32)
        for hit, _ in picks:
            m = m + jnp.where(hit, 1.0, 0.0)
        multi_g.append(m)
    multi = jnp.concatenate(multi_g, axis=1)
    before = (lax.broadcasted_iota(I32, (tm, tm), 0) < lax.broadcasted_iota(I32, (tm, tm), 1))
    cum = _dot(multi.astype(BF16), jnp.where(before, 1.0, 0.0).astype(BF16))
    pad = jnp.zeros((SUBLANES - TOP_K, LANES), F32)
    for g, (picks, raw) in enumerate(routed):
        lanes = slice(g * LANES, (g + 1) * LANES)
        total = raw[0]
        for r in raw[1:]:
            total = total + r
        scale = ROUTE_SCALE / total
        cum_g = cum[:, lanes]
        eidx_ref[:, lanes] = jnp.concatenate([ei for _, ei in picks] + [pad.astype(I32)], axis=0)
        wsel_ref[:, lanes] = jnp.concatenate([r * scale for r in raw] + [pad], axis=0)
        rank_ref[:, lanes] = jnp.concatenate(
            [jnp.sum(jnp.where(hit, cum_g, 0.0), axis=0, keepdims=True) for hit, _ in picks] + [pad],
            axis=0).astype(I32)
    tile_lane = lax.broadcasted_iota(I32, (N_EXPERTS, LANES), 1)
    cnt_ref[...] = jnp.where(tile_lane == i, jnp.sum(multi, axis=1, keepdims=True), cnt_ref[...])


def _post_mixer(x_ctx, x_lat, mix_c, ya, yb, yc, w_out_bf16, mod, ln_g, ln_b, wr_hi_t, wr_lo_t, b_router_col):
    n_ctx = T_CTX // TM_TOK
    ctx_map = lambda i: (jnp.minimum(i, n_ctx - 1), 0)
    lat_map = lambda i: (jnp.maximum(i - n_ctx, 0), 0)
    row_map = lambda i: (i, 0)
    const = lambda i: (0, 0)
    tok_map = lambda i: (0, i)
    return pl.pallas_call(
        _post_mixer_kernel,
        out_shape=(
            jax.ShapeDtypeStruct((T_ALL, D_MODEL), F32),
            jax.ShapeDtypeStruct((T_ALL, D_MODEL), BF16),
            jax.ShapeDtypeStruct((SUBLANES, T_ALL), I32),
            jax.ShapeDtypeStruct((SUBLANES, T_ALL), F32),
            jax.ShapeDtypeStruct((SUBLANES, T_ALL), I32),
            jax.ShapeDtypeStruct((N_EXPERTS, LANES), F32),
        ),
        grid=(N_TOK_TILES,),
        in_specs=[
            pl.BlockSpec((TM_TOK, D_MODEL), ctx_map),
            pl.BlockSpec((TM_TOK, D_MODEL), lat_map),
            pl.BlockSpec((TM_TOK, D_MODEL), ctx_map),
            pl.BlockSpec((TM_TOK, W_A), lat_map),
            pl.BlockSpec((TM_TOK, W_B), lat_map),
            pl.BlockSpec((TM_TOK, W_C), lat_map),
            pl.BlockSpec((D_MODEL, D_MODEL), const),
            pl.BlockSpec((N_COND, 6 * D_MODEL), const),
            pl.BlockSpec((1, D_MODEL), const),
            pl.BlockSpec((1, D_MODEL), const),
            pl.BlockSpec((N_EXPERTS, D_MODEL), const),
            pl.BlockSpec((N_EXPERTS, D_MODEL), const),
            pl.BlockSpec((N_EXPERTS, 1), const),
        ],
        out_specs=(
            pl.BlockSpec((TM_TOK, D_MODEL), row_map),
            pl.BlockSpec((TM_TOK, D_MODEL), row_map),
            pl.BlockSpec((SUBLANES, TM_TOK), tok_map),
            pl.BlockSpec((SUBLANES, TM_TOK), tok_map),
            pl.BlockSpec((SUBLANES, TM_TOK), tok_map),
            pl.BlockSpec((N_EXPERTS, LANES), const),
        ),
        compiler_params=pltpu.CompilerParams(
            dimension_semantics=("arbitrary",), vmem_limit_bytes=VMEM_LIMIT),
        name="post_mixer",
    )(x_ctx, x_lat, mix_c, ya, yb, yc, w_out_bf16, mod, ln_g, ln_b, wr_hi_t, wr_lo_t, b_router_col)


def _plan_kernel(eidx_ref, rank_ref, nmat_ref, lslot_ref, unit_ref, gend_ref):
    i = pl.program_id(0)
    units = jnp.floor((nmat_ref[...] + (UNIT - 1.0)) * (1.0 / UNIT))
    units_bf = units.astype(BF16)
    earlier_e = (lax.broadcasted_iota(I32, (N_EXPERTS, N_EXPERTS), 1)
                 < lax.broadcasted_iota(I32, (N_EXPERTS, N_EXPERTS), 0))
    tri_e = jnp.where(earlier_e, 1.0, 0.0).astype(BF16)
    earlier_t = (lax.broadcasted_iota(I32, (LANES, LANES), 0) < lax.broadcasted_iota(I32, (LANES, LANES), 1))
    tri_t = jnp.where(earlier_t, 1.0, 0.0).astype(BF16)
    local_off = _dot(tri_e, units_bf)
    tile_off = _dot(units_bf, tri_t)
    per_expert = jnp.sum(units, axis=1, keepdims=True)
    blocks = jnp.floor((per_expert + (UNITS_PER_BLOCK - 1.0)) * (1.0 / UNITS_PER_BLOCK))
    blocks_l = jnp.broadcast_to(blocks, (N_EXPERTS, LANES))
    start_blk = _dot(tri_e, blocks_l.astype(BF16))
    end_blk = start_blk + blocks_l
    gend_ref[...] = (end_blk * BM).astype(I32)

    this_tile = lax.broadcasted_iota(I32, (N_EXPERTS, LANES), 1) == i

    def column(a):
        return jnp.sum(jnp.where(this_tile, a, 0.0), axis=1, keepdims=True)

    lo, n_u = column(local_off), column(units)
    base_unit = start_blk[:, 0:1] * UNITS_PER_BLOCK + column(tile_off) - lo
    u = lax.broadcasted_iota(I32, (N_EXPERTS, MAX_UNITS), 1).astype(F32)
    inside = jnp.where(u >= lo, jnp.where(u < lo + n_u, 1.0, 0.0), 0.0)
    dst_unit = jnp.sum(inside * (base_unit + u), axis=0, keepdims=True)
    used = jnp.sum(inside, axis=0, keepdims=True) > 0.5
    spare = (SPARE_UNIT0 + (i % 2) * MAX_UNITS).astype(F32) + u[0:1, :]
    unit_ref[0] = jnp.where(used, dst_unit, spare).astype(I32)

    io_e = lax.broadcasted_iota(I32, (N_EXPERTS, TM_TOK), 0)
    rows = []
    for k in range(TOP_K):
        hit = io_e == eidx_ref[k:k + 1, :]
        seg = jnp.sum(jnp.where(hit, lo * UNIT, 0.0), axis=0, keepdims=True)
        rows.append(seg.astype(I32) + rank_ref[k:k + 1, :])
    rows.append(jnp.full((SUBLANES - TOP_K, TM_TOK), -1, I32))
    lslot_ref[...] = jnp.concatenate(rows, axis=0)


def _slot_plan(eidx, rank, nmat):
    tok_map = lambda i: (0, i)
    const = lambda i: (0, 0)
    lslot, unit_tab, gend = pl.pallas_call(
        _plan_kernel,
        out_shape=(
            jax.ShapeDtypeStruct((SUBLANES, T_ALL), I32),
            jax.ShapeDtypeStruct((N_TOK_TILES, 1, MAX_UNITS), I32),
            jax.ShapeDtypeStruct((N_EXPERTS, LANES), I32),
        ),
        grid=(N_TOK_TILES,),
        in_specs=[
            pl.BlockSpec((SUBLANES, TM_TOK), tok_map),
            pl.BlockSpec((SUBLANES, TM_TOK), tok_map),
            pl.BlockSpec((N_EXPERTS, LANES), const),
        ],
        out_specs=(
            pl.BlockSpec((SUBLANES, TM_TOK), tok_map),
            pl.BlockSpec((1, 1, MAX_UNITS), lambda i: (i, 0, 0)),
            pl.BlockSpec((N_EXPERTS, LANES), const),
        ),
        compiler_params=pltpu.CompilerParams(dimension_semantics=("arbitrary",)),
        name="slot_plan",
    )(eidx, rank, nmat)
    return lslot, unit_tab.reshape(N_TOK_TILES * MAX_UNITS), gend[:, 0]


PACK_W = D_MODEL // 2
HI_HALF = -65536


def _pack_pairs(x):
    lo = lax.bitcast_convert_type(x[:, 0:PACK_W], I32)
    hi = lax.bitcast_convert_type(x[:, PACK_W:D_MODEL], I32)
    return lax.shift_right_logical(lo, 16) | (hi & HI_HALF)


def _unpack_pairs(u):
    lo = lax.bitcast_convert_type(lax.shift_left(u, 16), F32).astype(BF16)
    hi = lax.bitcast_convert_type(u & HI_HALF, F32).astype(BF16)
    return lo, hi


def _unit_rows(unit):
    row = unit * UNIT
    return pl.ds(row if isinstance(unit, int) else pl.multiple_of(row, UNIT), UNIT)


def _unit_copy(src, src_unit, dst, dst_unit, sem):
    return pltpu.make_async_copy(src.at[_unit_rows(src_unit)], dst.at[_unit_rows(dst_unit)], sem)


def _dispatch_kernel(gend_ref, tab_ref, h2_ref, lslot_ref, xs_hbm, zero_ref, local_ref, sem_zero, sem_rows):
    i = pl.program_id(0)
    buf = i % 2

    def drain(b):
        pltpu.make_async_copy(local_ref.at[b], xs_hbm.at[pl.ds(0, LOCAL_ROWS)], sem_rows.at[b]).wait()

    def has_rows(e):
        return gend_ref[e] > jnp.where(e == 0, 0, gend_ref[jnp.maximum(e - 1, 0)])

    def zero_copy(e):
        return pltpu.make_async_copy(
            zero_ref, xs_hbm.at[pl.ds(pl.multiple_of(gend_ref[e] - BM, BM), BM)], sem_zero)

    @pl.when(i == 0)
    def _():
        zero_ref[...] = jnp.zeros_like(zero_ref)

        def start(e, c):
            @pl.when(has_rows(e))
            def _():
                zero_copy(e).start()
            return c

        def wait(e, c):
            @pl.when(has_rows(e))
            def _():
                zero_copy(e).wait()
            return c

        def tail_copy(blk):
            return pltpu.make_async_copy(
                zero_ref, xs_hbm.at[pl.ds(pl.multiple_of(blk * BM, BM), BM)], sem_zero)

        def start_tail(blk, c):
            tail_copy(blk).start()
            return c

        def wait_tail(blk, c):
            tail_copy(blk).wait()
            return c

        n_used = gend_ref[N_EXPERTS - 1] // BM
        lax.fori_loop(0, N_EXPERTS, start, 0)
        lax.fori_loop(n_used, N_BLOCKS_ALL, start_tail, 0)
        lax.fori_loop(0, N_EXPERTS, wait, 0)
        lax.fori_loop(n_used, N_BLOCKS_ALL, wait_tail, 0)

    @pl.when(i >= 2)
    def _():
        drain(buf)

    h2 = h2_ref[...]
    units_per_chunk = PERM_CHUNK // UNIT
    local = local_ref.at[buf]
    for c in range(LOCAL_ROWS // PERM_CHUNK):
        slot = c * PERM_CHUNK + lax.broadcasted_iota(I32, (PERM_CHUNK, TM_TOK), 0)
        p = jnp.zeros((PERM_CHUNK, TM_TOK), F32)
        for k in range(TOP_K):
            p = jnp.where(slot == lslot_ref[k:k + 1, :], 1.0, p)
        local[c * PERM_CHUNK:(c + 1) * PERM_CHUNK, :] = _pack_pairs(_dot(p.astype(BF16), h2))
        for u in range(c * units_per_chunk, (c + 1) * units_per_chunk):
            _unit_copy(local, u, xs_hbm, tab_ref[i * MAX_UNITS + u], sem_rows.at[buf]).start()

    @pl.when(i == N_TOK_TILES - 1)
    def _():
        drain(1 - buf)
        drain(buf)


def _dispatch(h2, lslot, unit_tab, gend):
    return pl.pallas_call(
        _dispatch_kernel,
        out_shape=jax.ShapeDtypeStruct((N_SLOTS, PACK_W), I32),
        grid_spec=pltpu.PrefetchScalarGridSpec(
            num_scalar_prefetch=2,
            grid=(N_TOK_TILES,),
            in_specs=[
                pl.BlockSpec((TM_TOK, D_MODEL), lambda i, ge, tab: (i, 0)),
                pl.BlockSpec((SUBLANES, TM_TOK), lambda i, ge, tab: (0, i)),
            ],
            out_specs=pl.BlockSpec(memory_space=pl.ANY),
            scratch_shapes=[
                pltpu.VMEM((BM, PACK_W), I32),
                pltpu.VMEM((2, LOCAL_ROWS, PACK_W), I32),
                pltpu.SemaphoreType.DMA,
                pltpu.SemaphoreType.DMA((2,)),
            ],
        ),
        compiler_params=pltpu.CompilerParams(
            dimension_semantics=("arbitrary",), vmem_limit_bytes=VMEM_LIMIT),
        name="moe_dispatch",
    )(gend, unit_tab, h2, lslot)


def _expert_kernel(gend_ref, xs_hbm, wgu_ref, wdown_ref, ys_hbm, xbuf, ybuf, wgu_bf, wdown_bf, sem_in, sem_out):
    e = pl.program_id(0)
    first = jnp.where(e == 0, 0, gend_ref[jnp.maximum(e - 1, 0)]) // BM
    n_blk = gend_ref[e] // BM - first

    def rows_of(blk):
        return pl.ds(pl.multiple_of(blk * BM, BM), BM)

    def in_copy(j, b):
        return pltpu.make_async_copy(xs_hbm.at[rows_of(first + j)], xbuf.at[b], sem_in.at[b])

    def out_copy(j, b):
        return pltpu.make_async_copy(ybuf.at[b], ys_hbm.at[rows_of(first + j)], sem_out.at[b])

    @pl.when(jnp.logical_and(e == 0, n_blk > 0))
    def _():
        in_copy(0, 0).start()

    for a in range(1, X_AHEAD):
        @pl.when(n_blk > a)
        def _(a=a):
            in_copy(a, a).start()

    @pl.when(n_blk > 0)
    def _():
        wgu_bf[...] = wgu_ref[0, 0].astype(BF16)
        wdown_bf[...] = wdown_ref[0, 0].astype(BF16)

    def block(j, carry):
        b = j % 2

        @pl.when(j + X_AHEAD < n_blk)
        def _():
            in_copy(j + X_AHEAD, (j + X_AHEAD) % X_BUFS).start()

        in_copy(j, j % X_BUFS).wait()

        @pl.when(j >= 2)
        def _():
            out_copy(j - 2, b).wait()

        x_lo, x_hi = _unpack_pairs(xbuf[j % X_BUFS])
        gu = _dot(x_lo, wgu_bf[0:PACK_W, :]) + _dot(x_hi, wgu_bf[PACK_W:D_MODEL, :])
        act = _silu(gu[:, 0:D_EXPERT]) * gu[:, D_EXPERT:2 * D_EXPERT]
        y = _dot(act.astype(BF16), wdown_bf[...])
        ybuf[b] = _pack_pairs(y.astype(BF16).astype(F32))
        out_copy(j, b).start()
        return carry

    lax.fori_loop(0, n_blk, block, 0)

    nxt = jnp.minimum(e + 1, N_EXPERTS - 1)
    next_blocks = gend_ref[nxt] // BM - gend_ref[e] // BM

    @pl.when(jnp.logical_and(e + 1 < N_EXPERTS, next_blocks > 0))
    def _():
        pltpu.make_async_copy(xs_hbm.at[rows_of(gend_ref[e] // BM)], xbuf.at[0], sem_in.at[0]).start()

    @pl.when(n_blk >= 2)
    def _():
        out_copy(n_blk - 2, n_blk % 2).wait()

    @pl.when(n_blk >= 1)
    def _():
        out_copy(n_blk - 1, (n_blk - 1) % 2).wait()

    @pl.when(e == N_EXPERTS - 1)
    def _():
        ybuf[0] = jnp.zeros((BM, PACK_W), I32)
        n_used = gend_ref[N_EXPERTS - 1] // BM

        def tail_copy(blk):
            return pltpu.make_async_copy(
                ybuf.at[0], ys_hbm.at[pl.ds(pl.multiple_of(blk * BM, BM), BM)], sem_out.at[0])

        def start(blk, c):
            tail_copy(blk).start()
            return c

        def wait(blk, c):
            tail_copy(blk).wait()
            return c

        lax.fori_loop(n_used, N_BLOCKS_ALL, start, 0)
        lax.fori_loop(n_used, N_BLOCKS_ALL, wait, 0)


def _experts(xs, w_gu, w_down, layer, gend):
    return pl.pallas_call(
        _expert_kernel,
        out_shape=jax.ShapeDtypeStruct((N_SLOTS, PACK_W), I32),
        grid_spec=pltpu.PrefetchScalarGridSpec(
            num_scalar_prefetch=1,
            grid=(N_EXPERTS,),
            in_specs=[
                pl.BlockSpec(memory_space=pl.ANY),
                pl.BlockSpec((1, 1, D_MODEL, 2 * D_EXPERT), lambda e, ge: (layer, e, 0, 0)),
                pl.BlockSpec((1, 1, D_EXPERT, D_MODEL), lambda e, ge: (layer, e, 0, 0)),
            ],
            out_specs=pl.BlockSpec(memory_space=pl.ANY),
            scratch_shapes=[
                pltpu.VMEM((X_BUFS, BM, PACK_W), I32),
                pltpu.VMEM((2, BM, PACK_W), I32),
                pltpu.VMEM((D_MODEL, 2 * D_EXPERT), BF16),
                pltpu.VMEM((D_EXPERT, D_MODEL), BF16),
                pltpu.SemaphoreType.DMA((X_BUFS,)),
                pltpu.SemaphoreType.DMA((2,)),
            ],
        ),
        compiler_params=pltpu.CompilerParams(
            dimension_semantics=("arbitrary",), vmem_limit_bytes=VMEM_LIMIT),
        name="moe_experts",
    )(gend, xs, w_gu, w_down)


def _combine_kernel(tab_ref, x1_ref, h2_ref, lslot_ref, wsel_ref, ys_hbm, wsgu_ref, wsdown_ref, mod_ref,
                    g_ref, b_ref, outc_ref, outl_ref, local_ref, sel_ref, ylo_ref, yhi_ref, sem_rows):
    i = pl.program_id(0)
    tm = TM_TOK
    buf = i % 2

    def fetch_unit(tile, b, u):
        _unit_copy(ys_hbm, tab_ref[tile * MAX_UNITS + u], local_ref.at[b], u, sem_rows.at[b]).start()

    def drain(b):
        pltpu.make_async_copy(ys_hbm.at[pl.ds(0, LOCAL_ROWS)], local_ref.at[b], sem_rows.at[b]).wait()

    @pl.when(i == 0)
    def _():
        def body(u, c):
            fetch_unit(0, 0, u)
            return c

        lax.fori_loop(0, MAX_UNITS, body, 0, unroll=8)

    sgu = _dot(h2_ref[...], wsgu_ref[...])
    act = _silu(sgu[:, 0:D_SHARED]) * sgu[:, D_SHARED:2 * D_SHARED]
    f = _dot(act.astype(BF16), wsdown_ref[...])

    nxt = jnp.minimum(i + 1, N_TOK_TILES - 1)
    n_groups = tm // SEL_ROWS
    units_per_group = MAX_UNITS // n_groups
    slot = lax.broadcasted_iota(I32, (SEL_ROWS, LOCAL_ROWS), 1)
    for g in range(n_groups):
        rows = slice(g * SEL_ROWS, (g + 1) * SEL_ROWS)
        sel = jnp.zeros((SEL_ROWS, LOCAL_ROWS), F32)
        for k in range(TOP_K):
            sel = jnp.where(slot == lslot_ref[rows, k:k + 1], wsel_ref[rows, k:k + 1], sel)
        sel_ref[rows, :] = sel.astype(BF16)
        for u in range(g * units_per_group, (g + 1) * units_per_group):
            fetch_unit(nxt, 1 - buf, u)

    local = local_ref.at[buf]
    drain(buf)
    for c in range(LOCAL_ROWS // PERM_CHUNK):
        rows = slice(c * PERM_CHUNK, (c + 1) * PERM_CHUNK)
        ylo_ref[rows, :], yhi_ref[rows, :] = _unpack_pairs(local[rows, :])
    sel = sel_ref[...]
    f = f + jnp.concatenate([_dot(sel, ylo_ref[...]), _dot(sel, yhi_ref[...])], axis=-1)
    ci = _cond_row(i, tm)
    gate2 = mod_ref[pl.ds(ci, 1), 5 * D_MODEL:6 * D_MODEL]
    out = _layer_norm(ALPHA * x1_ref[...] + gate2 * f, g_ref[...], b_ref[...])

    @pl.when(i < T_CTX // tm)
    def _():
        outc_ref[...] = out

    @pl.when(i >= T_CTX // tm)
    def _():
        outl_ref[...] = out

    @pl.when(i == N_TOK_TILES - 1)
    def _():
        drain(1 - buf)


def _combine(x1, h2, lslot_rows, wsel_rows, unit_tab, ys, w_sgu_bf16, w_sdown_bf16, mod, ln_g, ln_b):
    n_ctx = T_CTX // TM_TOK
    row_map = lambda i, tab: (i, 0)
    const = lambda i, tab: (0, 0)
    return pl.pallas_call(
        _combine_kernel,
        out_shape=(jax.ShapeDtypeStruct((T_CTX, D_MODEL), F32),
                   jax.ShapeDtypeStruct((T_LAT, D_MODEL), F32)),
        grid_spec=pltpu.PrefetchScalarGridSpec(
            num_scalar_prefetch=1,
            grid=(N_TOK_TILES,),
            in_specs=[
                pl.BlockSpec((TM_TOK, D_MODEL), row_map),
                pl.BlockSpec((TM_TOK, D_MODEL), row_map),
                pl.BlockSpec((TM_TOK, SUBLANES), row_map),
                pl.BlockSpec((TM_TOK, SUBLANES), row_map),
                pl.BlockSpec(memory_space=pl.ANY),
                pl.BlockSpec((D_MODEL, 2 * D_SHARED), const),
                pl.BlockSpec((D_SHARED, D_MODEL), const),
                pl.BlockSpec((N_COND, 6 * D_MODEL), const),
                pl.BlockSpec((1, D_MODEL), const),
                pl.BlockSpec((1, D_MODEL), const),
            ],
            out_specs=(pl.BlockSpec((TM_TOK, D_MODEL), lambda i, tab: (jnp.minimum(i, n_ctx - 1), 0)),
                       pl.BlockSpec((TM_TOK, D_MODEL), lambda i, tab: (jnp.maximum(i - n_ctx, 0), 0))),
            scratch_shapes=[
                pltpu.VMEM((2, LOCAL_ROWS, PACK_W), I32),
                pltpu.VMEM((TM_TOK, LOCAL_ROWS), BF16),
                pltpu.VMEM((LOCAL_ROWS, PACK_W), BF16),
                pltpu.VMEM((LOCAL_ROWS, PACK_W), BF16),
                pltpu.SemaphoreType.DMA((2,)),
            ],
        ),
        compiler_params=pltpu.CompilerParams(
            dimension_semantics=("arbitrary",), vmem_limit_bytes=VMEM_LIMIT),
        name="moe_combine",
    )(unit_tab, x1, h2, lslot_rows, wsel_rows, ys, w_sgu_bf16, w_sdown_bf16, mod, ln_g, ln_b)


def _lane_rows(v, width):
    return jnp.broadcast_to(v.astype(F32)[:, None, None], (v.shape[0], 1, width))


def kernel(x_prompt, x_sample, state_ret_fwd, state_ret_bwd, cache_win_k, cache_win_v, cache_na_k, cache_na_v, c, c_ctx, w_in, w_out, ret_decay_fwd, ret_decay_bwd, ret_gn_g, ret_gn_b, win_sink, na_rpb, w_mod, b_mod, ln1_g, ln1_b, ln2_g, ln2_b, w_router, b_router, w_expert_gu, w_expert_down, w_shared_gu, w_shared_down):
    cond = jnp.concatenate(
        [c_ctx[None, :], c, jnp.zeros((N_COND - 1 - DEC_BATCH, D_MODEL), F32)], axis=0)
    mod_all = _modulation(cond, w_mod, b_mod)
    cos_t, sin_t = _rope_tables()

    x_ctx = x_prompt.reshape(T_CTX, D_MODEL)
    x_lat = x_sample.reshape(T_LAT, D_MODEL)
    sf_l, sb_l, caches = [], [], ()
    for l in range(DEPTH):
        mod = mod_all[l]
        pa, pb, pc, *caches = _in_projection(x_ctx, x_lat, mod, w_in[l].astype(BF16), tuple(caches))
        decf_s, decb_s = _lane_rows(ret_decay_fwd[l], SEQ), _lane_rows(ret_decay_bwd[l], SEQ)
        gng, gnb = ret_gn_g[l][None, :], ret_gn_b[l][None, :]
        mix_c, st_f, st_b = _ctx_mixers(pa, pb, pc, win_sink[l], decf_s, decb_s, gng, gnb)
        sf_l.append(st_f)
        sb_l.append(st_b)
        ya = _lat_retention(pa, state_ret_fwd, state_ret_bwd, l,
                            _lane_rows(ret_decay_fwd[l], RET_CHUNK), _lane_rows(ret_decay_bwd[l], RET_CHUNK),
                            gng, gnb)
        yb = _lat_window_attn(pb, cache_win_k, cache_win_v, l, win_sink[l], cos_t, sin_t)
        yc = _lat_na_attn(pc, cache_na_k, cache_na_v, l, _na_maskbias(na_rpb[l]))

        wr_t = w_router[l].T
        wr_hi = wr_t.astype(BF16)
        wr_lo = (wr_t - wr_hi.astype(F32)).astype(BF16)
        x1, h2, eidx, wsel, rank, counts = _post_mixer(
            x_ctx, x_lat, mix_c, ya, yb, yc, w_out[l].astype(BF16), mod, ln1_g[l][None, :],
            ln1_b[l][None, :], wr_hi, wr_lo, b_router[l][:, None])
        lslot, unit_tab, gend = _slot_plan(eidx, rank, counts)
        xs = _dispatch(h2, lslot, unit_tab, gend)
        ys = _experts(xs, w_expert_gu, w_expert_down, l, gend)
        x_ctx, x_lat = _combine(x1, h2, lslot.T, wsel.T, unit_tab, ys, w_shared_gu[l].astype(BF16),
                                w_shared_down[l].astype(BF16), mod, ln2_g[l][None, :], ln2_b[l][None, :])

    y_prompt = x_ctx.reshape(BATCH, SEQ, D_MODEL)
    y_sample = x_lat.reshape(DEC_BATCH, DEC_SEQ, D_MODEL)
    new_sf = jnp.stack(sf_l, axis=1)
    new_sb = jnp.stack(sb_l, axis=1)

    return (y_prompt, y_sample, new_sf, new_sb, *caches)
```

```python
import functools

import numpy as np
import jax
import jax.numpy as jnp
from jax import lax
from jax.experimental import pallas as pl
from jax.experimental.pallas import tpu as pltpu

F32 = jnp.float32
BF16 = jnp.bfloat16
I32 = jnp.int32

D_MODEL = 1024
BATCH = 32
SEQ = 256
DEPTH = 2
DEC_BATCH = 4
DEC_SEQ = 2048
PAST_LEN = 256
GRID_W = 64
HEAD_DIM = 64
ATTN_SCALE = HEAD_DIM ** -0.5
H_A = 4
W_A = H_A * HEAD_DIM
GN_EPS = 1e-5
H_B = 6
KV_B = 2
W_B = H_B * HEAD_DIM
KV_W_B = KV_B * HEAD_DIM
WINDOW = 128
ROPE_BASE = 10000.0
H_C = 6
W_C = H_C * HEAD_DIM
NA_ROWS = 8
NA_COLS = 16
IN_WIDTH = 4 * W_A + W_B + 2 * KV_W_B + 3 * W_C
N_EXPERTS = 64
TOP_K = 6
N_GROUPS = 8
GROUP_SIZE = N_EXPERTS // N_GROUPS
TOPK_GROUPS = 4
D_EXPERT = 256
D_SHARED = 256
ROUTE_SCALE = 2.5
ALPHA = (2 * DEPTH) ** 0.25
LN_EPS = 1e-5
NEG_INF = -1e30
PICKED = -3e38

T_CTX = BATCH * SEQ
T_LAT = DEC_BATCH * DEC_SEQ
T_ALL = T_CTX + T_LAT
N_COND = 8

PA_W = 4 * W_A
PB_W = W_B + 2 * KV_W_B
PC_W = 3 * W_C

LANES = 128
SUBLANES = 8
VMEM_LIMIT = 56 * 1024 * 1024

TM_PROJ = 512
TM_TOK = 256
RET_CHUNK = 256
WIN_QB = 128
NA_TILE_ROWS = 4
NA_KEY_ROWS = 11
BM = 512
X_AHEAD = 2
X_BUFS = X_AHEAD + 1
UNIT = SUBLANES
UNITS_PER_BLOCK = BM // UNIT
PERM_CHUNK = 256
SEL_ROWS = 16
N_TOK_TILES = T_ALL // TM_TOK
LOCAL_ROWS = -(-(TM_TOK * TOP_K + N_EXPERTS * (UNIT - 1)) // PERM_CHUNK) * PERM_CHUNK
MAX_UNITS = LOCAL_ROWS // UNIT
N_ASSIGN = T_ALL * TOP_K
N_BLOCKS = -(-(N_ASSIGN + N_TOK_TILES * N_EXPERTS * (UNIT - 1) + N_EXPERTS * (BM - 1)) // BM)
SPARE_BLOCKS = 2 * -(-LOCAL_ROWS // BM)
SPARE_UNIT0 = N_BLOCKS * UNITS_PER_BLOCK
N_BLOCKS_ALL = N_BLOCKS + SPARE_BLOCKS
N_SLOTS = N_BLOCKS_ALL * BM


def _dot(a, b):
    return jnp.dot(a, b, preferred_element_type=F32)


def _dot_nt(a, b):
    return lax.dot_general(a, b, (((1,), (1,)), ((), ())), preferred_element_type=F32)


def _silu(x):
    return x * jax.nn.sigmoid(x)


def _log_sigmoid(x):
    return jnp.minimum(x, 0.0) - jnp.log(1.0 + jnp.exp(-jnp.abs(x)))


def _cond_row(tile, tile_rows):
    n_ctx = T_CTX // tile_rows
    per_lat = DEC_SEQ // tile_rows
    return jnp.where(tile < n_ctx, 0, 1 + (tile - n_ctx) // per_lat)


def _layer_norm(x, g, b):
    mu = jnp.mean(x, -1, keepdims=True)
    xc = x - mu
    var = jnp.mean(xc * xc, -1, keepdims=True)
    return xc * lax.rsqrt(var + LN_EPS) * g + b


MOD_TN = 1536


def _mod_kernel(cond_ref, w_ref, b_ref, o_ref):
    s = _silu(cond_ref[...])
    s_hi = s.astype(BF16)
    s_lo = (s - s_hi.astype(F32)).astype(BF16)
    w = w_ref[0]
    w_hi = w.astype(BF16)
    w_lo = (w - w_hi.astype(F32)).astype(BF16)
    o_ref[0] = _dot(s_hi, w_hi) + _dot(s_lo, w_hi) + _dot(s_hi, w_lo) + b_ref[0]


def _modulation(cond, w_mod, b_mod):
    n_out = 6 * D_MODEL
    return pl.pallas_call(
        _mod_kernel,
        out_shape=jax.ShapeDtypeStruct((DEPTH, N_COND, n_out), F32),
        grid=(DEPTH, n_out // MOD_TN),
        in_specs=[
            pl.BlockSpec((N_COND, D_MODEL), lambda l, j: (0, 0)),
            pl.BlockSpec((1, D_MODEL, MOD_TN), lambda l, j: (l, 0, j)),
            pl.BlockSpec((1, 1, MOD_TN), lambda l, j: (l, 0, j)),
        ],
        out_specs=pl.BlockSpec((1, N_COND, MOD_TN), lambda l, j: (l, 0, j)),
        compiler_params=pltpu.CompilerParams(
            dimension_semantics=("arbitrary", "arbitrary"), vmem_limit_bytes=VMEM_LIMIT),
        name="modulation",
    )(cond, w_mod, b_mod.reshape(DEPTH, 1, n_out))


SEQ_PER_PROJ = TM_PROJ // SEQ


def _inproj_kernel(n_prev, xc_ref, xl_ref, mod_ref, w_ref, *refs):
    prev_refs = refs[:4] if n_prev else ()
    pa_ref, pb_ref, pc_ref, wk_ref, wv_ref, nk_ref, nv_ref = refs[len(prev_refs):]
    i = pl.program_id(0)
    ci = _cond_row(i, TM_PROJ)
    sh = mod_ref[pl.ds(ci, 1), 0:D_MODEL]
    sc = mod_ref[pl.ds(ci, 1), D_MODEL:2 * D_MODEL]
    x = jnp.where(i < T_CTX // TM_PROJ, xc_ref[...], xl_ref[...])
    h = x * (1.0 + sc) + sh
    p = _dot(h.astype(BF16), w_ref[...])
    pa_ref[...] = p[:, 0:PA_W].astype(BF16)
    pb_ref[...] = p[:, PA_W:PA_W + PB_W].astype(BF16)
    pc_ref[...] = p[:, PA_W + PB_W:IN_WIDTH].astype(BF16)

    @pl.when(i < T_CTX // TM_PROJ)
    def _():
        targets = ((wk_ref, PA_W + W_B, KV_B), (wv_ref, PA_W + W_B + KV_W_B, KV_B),
                   (nk_ref, PA_W + PB_W + W_C, H_C), (nv_ref, PA_W + PB_W + 2 * W_C, H_C))
        for j, (ref, col0, n_heads) in enumerate(targets):
            if n_prev:
                ref[:, 0:n_prev] = prev_refs[j][...]
            for s in range(SEQ_PER_PROJ):
                for hh in range(n_heads):
                    ref[s, n_prev, hh] = p[s * SEQ:(s + 1) * SEQ,
                                           col0 + hh * HEAD_DIM:col0 + (hh + 1) * HEAD_DIM]


def _in_projection(x_ctx, x_lat, mod, w_in_bf16, earlier):
    n_ctx_tiles = T_CTX // TM_PROJ
    n_prev = earlier[0].shape[1] if earlier else 0

    def cache_spec(n_layers, n_heads):
        return pl.BlockSpec((SEQ_PER_PROJ, n_layers, n_heads, SEQ, HEAD_DIM),
                            lambda i: (jnp.minimum(i, n_ctx_tiles - 1), 0, 0, 0, 0))

    cache_heads = (KV_B, KV_B, H_C, H_C)
    return pl.pallas_call(
        functools.partial(_inproj_kernel, n_prev),
        out_shape=(
            jax.ShapeDtypeStruct((T_ALL, PA_W), BF16),
            jax.ShapeDtypeStruct((T_ALL, PB_W), BF16),
            jax.ShapeDtypeStruct((T_ALL, PC_W), BF16),
        ) + tuple(jax.ShapeDtypeStruct((BATCH, n_prev + 1, nh, SEQ, HEAD_DIM), F32) for nh in cache_heads),
        grid=(T_ALL // TM_PROJ,),
        in_specs=[
            pl.BlockSpec((TM_PROJ, D_MODEL), lambda i: (jnp.minimum(i, n_ctx_tiles - 1), 0)),
            pl.BlockSpec((TM_PROJ, D_MODEL), lambda i: (jnp.maximum(i - n_ctx_tiles, 0), 0)),
            pl.BlockSpec((N_COND, 6 * D_MODEL), lambda i: (0, 0)),
            pl.BlockSpec((D_MODEL, IN_WIDTH), lambda i: (0, 0)),
        ] + [cache_spec(n_prev, nh) for nh in cache_heads if n_prev],
        out_specs=(
            pl.BlockSpec((TM_PROJ, PA_W), lambda i: (i, 0)),
            pl.BlockSpec((TM_PROJ, PB_W), lambda i: (i, 0)),
            pl.BlockSpec((TM_PROJ, PC_W), lambda i: (i, 0)),
        ) + tuple(cache_spec(n_prev + 1, nh) for nh in cache_heads),
        compiler_params=pltpu.CompilerParams(
            dimension_semantics=("arbitrary",), vmem_limit_bytes=VMEM_LIMIT),
        name="in_projection",
    )(x_ctx, x_lat, mod, w_in_bf16, *earlier)


def _decay_matrix(lg_f, lg_b, n):
    row = lax.broadcasted_iota(I32, (n, n), 0)
    col = lax.broadcasted_iota(I32, (n, n), 1)
    diff = (row - col).astype(F32)
    fwd = jnp.where(diff >= 0, jnp.exp(lg_f * jnp.maximum(diff, 0.0)), 0.0)
    bwd = jnp.where(diff <= 0, jnp.exp(lg_b * jnp.maximum(-diff, 0.0)), 0.0)
    return (fwd + bwd) * ATTN_SCALE


def _retention_readout(o, gate, g, b):
    mu = jnp.mean(o, -1, keepdims=True)
    oc = o - mu
    var = jnp.mean(oc * oc, -1, keepdims=True)
    on = oc * lax.rsqrt(var + GN_EPS) * g + b
    return on * _silu(gate.astype(F32))


def _softmax_attend(s, v, extra_logit=None):
    m = jnp.max(s, -1, keepdims=True)
    if extra_logit is not None:
        m = jnp.maximum(m, extra_logit)
    p = jnp.exp(s - m)
    den = jnp.sum(p, -1, keepdims=True)
    if extra_logit is not None:
        den = den + jnp.exp(extra_logit - m)
    return _dot(p.astype(BF16), v) / den


def _ctx_mixer_kernel(sink_ref, pa_ref, pb_ref, pc_ref, decf_ref, decb_ref, gng_ref, gnb_ref,
                      mix_ref, sf_ref, sb_ref, dmat_ref, zf_ref, zb_ref):
    n = SEQ
    hd = HEAD_DIM

    @pl.when(pl.program_id(0) == 0)
    def _():
        pos = lax.broadcasted_iota(I32, (n, hd), 0).astype(F32)
        for h in range(H_A):
            lg_f = _log_sigmoid(decf_ref[h])
            lg_b = _log_sigmoid(decb_ref[h])
            dmat_ref[h] = _decay_matrix(lg_f, lg_b, n)
            zf_ref[h] = jnp.exp(lg_f[:, 0:hd] * (n - 1.0 - pos)) * ATTN_SCALE
            zb_ref[h] = jnp.exp(lg_b[:, 0:hd] * pos) * ATTN_SCALE

    for h in range(H_A):
        c0 = h * hd
        q = pa_ref[:, c0:c0 + hd]
        k = pa_ref[:, W_A + c0:W_A + c0 + hd]
        v = pa_ref[:, 2 * W_A + c0:2 * W_A + c0 + hd]
        gate = pa_ref[:, 3 * W_A + c0:3 * W_A + c0 + hd]
        o = _dot((_dot_nt(q, k) * dmat_ref[h]).astype(BF16), v)
        kf = k.astype(F32)
        sf_ref[0, h] = _dot((kf * zf_ref[h]).T.astype(BF16), v)
        sb_ref[0, h] = _dot((kf * zb_ref[h]).T.astype(BF16), v)
        y = _retention_readout(o, gate, gng_ref[:, c0:c0 + hd], gnb_ref[:, c0:c0 + hd])
        mix_ref[:, c0:c0 + hd] = y.astype(BF16)
    for j in range(KV_B):
        k = pb_ref[:, W_B + j * hd:W_B + (j + 1) * hd]
        v = pb_ref[:, W_B + KV_W_B + j * hd:W_B + KV_W_B + (j + 1) * hd]
        for g in range(H_B // KV_B):
            hh = j * (H_B // KV_B) + g
            q = pb_ref[:, hh * hd:(hh + 1) * hd]
            o = _softmax_attend(_dot_nt(q, k) * ATTN_SCALE, v, sink_ref[hh])
            mix_ref[:, W_A + hh * hd:W_A + (hh + 1) * hd] = o.astype(BF16)
    for hh in range(H_C):
        q = pc_ref[:, hh * hd:(hh + 1) * hd]
        k = pc_ref[:, W_C + hh * hd:W_C + (hh + 1) * hd]
        v = pc_ref[:, 2 * W_C + hh * hd:2 * W_C + (hh + 1) * hd]
        o = _softmax_attend(_dot_nt(q, k) * ATTN_SCALE, v)
        mix_ref[:, W_A + W_B + hh * hd:W_A + W_B + (hh + 1) * hd] = o.astype(BF16)


def _ctx_mixers(pa, pb, pc, sink, decf, decb, gng, gnb):
    return pl.pallas_call(
        _ctx_mixer_kernel,
        out_shape=(
            jax.ShapeDtypeStruct((T_CTX, D_MODEL), BF16),
            jax.ShapeDtypeStruct((BATCH, H_A, HEAD_DIM, HEAD_DIM), F32),
            jax.ShapeDtypeStruct((BATCH, H_A, HEAD_DIM, HEAD_DIM), F32),
        ),
        grid=(BATCH,),
        in_specs=[
            pl.BlockSpec(memory_space=pltpu.SMEM),
            pl.BlockSpec((SEQ, PA_W), lambda b: (b, 0)),
            pl.BlockSpec((SEQ, PB_W), lambda b: (b, 0)),
            pl.BlockSpec((SEQ, PC_W), lambda b: (b, 0)),
            pl.BlockSpec((H_A, 1, SEQ), lambda b: (0, 0, 0)),
            pl.BlockSpec((H_A, 1, SEQ), lambda b: (0, 0, 0)),
            pl.BlockSpec((1, W_A), lambda b: (0, 0)),
            pl.BlockSpec((1, W_A), lambda b: (0, 0)),
        ],
        out_specs=(
            pl.BlockSpec((SEQ, D_MODEL), lambda b: (b, 0)),
            pl.BlockSpec((1, H_A, HEAD_DIM, HEAD_DIM), lambda b: (b, 0, 0, 0)),
            pl.BlockSpec((1, H_A, HEAD_DIM, HEAD_DIM), lambda b: (b, 0, 0, 0)),
        ),
        scratch_shapes=[
            pltpu.VMEM((H_A, SEQ, SEQ), F32),
            pltpu.VMEM((H_A, SEQ, HEAD_DIM), F32),
            pltpu.VMEM((H_A, SEQ, HEAD_DIM), F32),
        ],
        compiler_params=pltpu.CompilerParams(
            dimension_semantics=("arbitrary",), vmem_limit_bytes=VMEM_LIMIT),
        name="ctx_mixers",
    )(sink, pa, pb, pc, decf, decb, gng, gnb)


def _lat_ret_kernel(pa_ref, stf_ref, stb_ref, decf_ref, decb_ref, gng_ref, gnb_ref, ya_ref, acc_ref):
    c = RET_CHUNK
    hd = HEAD_DIM
    n_chunks = DEC_SEQ // c
    pos = lax.broadcasted_iota(I32, (c, hd), 0).astype(F32)
    for h in range(H_A):
        c0 = h * hd
        lg_f = _log_sigmoid(decf_ref[h])
        lg_b = _log_sigmoid(decb_ref[h])
        dmat = _decay_matrix(lg_f, lg_b, c)
        lf = lg_f[:, 0:hd]
        lb = lg_b[:, 0:hd]
        zf = jnp.exp(lf * (c - 1.0 - pos)) * ATTN_SCALE
        zb = jnp.exp(lb * pos) * ATTN_SCALE
        xf = jnp.exp(lf * (pos + 1.0))
        xb = jnp.exp(lb * (c - pos))
        gcf = jnp.exp(lf * float(c))
        gcb = jnp.exp(lb * float(c))
        g = gng_ref[:, c0:c0 + hd]
        b = gnb_ref[:, c0:c0 + hd]

        def load(i, off):
            rows = pl.ds(pl.multiple_of(i * c, c), c)
            return pa_ref[rows, off + c0:off + c0 + hd]

        def fwd(i, s):
            q, k, v = load(i, 0), load(i, W_A), load(i, 2 * W_A)
            o = _dot((_dot_nt(q, k) * dmat).astype(BF16), v)
            o = o + _dot((q.astype(F32) * xf).astype(BF16), s.astype(BF16))
            acc_ref[pl.ds(pl.multiple_of(i * c, c), c), c0:c0 + hd] = o
            return gcf * s + _dot((k.astype(F32) * zf).T.astype(BF16), v)

        lax.fori_loop(0, n_chunks, fwd, stf_ref[0, 0, h])

        def bwd(j, s):
            i = n_chunks - 1 - j
            rows = pl.ds(pl.multiple_of(i * c, c), c)
            q, k, v = load(i, 0), load(i, W_A), load(i, 2 * W_A)
            o = acc_ref[rows, c0:c0 + hd] + _dot((q.astype(F32) * xb).astype(BF16), s.astype(BF16))
            y = _retention_readout(o, load(i, 3 * W_A), g, b)
            ya_ref[rows, c0:c0 + hd] = y.astype(BF16)
            return gcb * s + _dot((k.astype(F32) * zb).T.astype(BF16), v)

        lax.fori_loop(0, n_chunks, bwd, stb_ref[0, 0, h])


def _lat_retention(pa, st_f, st_b, layer, decf, decb, gng, gnb):
    lat0 = T_CTX // DEC_SEQ
    st_spec = pl.BlockSpec((1, 1, H_A, HEAD_DIM, HEAD_DIM), lambda b: (b, layer, 0, 0, 0))
    return pl.pallas_call(
        _lat_ret_kernel,
        out_shape=jax.ShapeDtypeStruct((T_LAT, W_A), BF16),
        grid=(DEC_BATCH,),
        in_specs=[
            pl.BlockSpec((DEC_SEQ, PA_W), lambda b: (lat0 + b, 0)),
            st_spec, st_spec,
            pl.BlockSpec((H_A, 1, RET_CHUNK), lambda b: (0, 0, 0)),
            pl.BlockSpec((H_A, 1, RET_CHUNK), lambda b: (0, 0, 0)),
            pl.BlockSpec((1, W_A), lambda b: (0, 0)),
            pl.BlockSpec((1, W_A), lambda b: (0, 0)),
        ],
        out_specs=pl.BlockSpec((DEC_SEQ, W_A), lambda b: (b, 0)),
        scratch_shapes=[pltpu.VMEM((DEC_SEQ, W_A), F32)],
        compiler_params=pltpu.CompilerParams(
            dimension_semantics=("arbitrary",), vmem_limit_bytes=VMEM_LIMIT),
        name="lat_retention",
    )(pa, st_f, st_b, decf, decb, gng, gnb)


def _swap_halves_matrix(width):
    r = lax.broadcasted_iota(I32, (width, width), 0)
    c = lax.broadcasted_iota(I32, (width, width), 1)
    return jnp.where((r ^ (HEAD_DIM // 2)) == c, 1.0, 0.0).astype(BF16)


def _rope(x, cos, sin_signed, swap):
    return x.astype(F32) * cos + _dot(x, swap) * sin_signed


def _lat_win_kernel(sink_ref, pq_ref, pseq_ref, kctx_ref, vctx_ref, cos_ref, sin_ref, yb_ref, krope_ref):
    n = pl.program_id(1)
    hd = HEAD_DIM
    qb = WIN_QB
    n_blk = DEC_SEQ // qb
    group = H_B // KV_B
    swap = _swap_halves_matrix(LANES)

    @pl.when(n == 0)
    def _():
        k = pseq_ref[:, W_B:W_B + KV_W_B]
        krope_ref[...] = _rope(k, cos_ref[...], sin_ref[...], swap).astype(BF16)

    q_rows = pl.ds(pl.multiple_of(n * qb, qb), qb)
    cos_q = cos_ref[q_rows, :]
    sin_q = sin_ref[q_rows, :]
    qr = [_rope(pq_ref[:, p * LANES:(p + 1) * LANES], cos_q, sin_q, swap).astype(BF16)
          for p in range(W_B // LANES)]

    ws = jnp.clip(n - 1, 0, n_blk - 3) * qb
    k_rows = pl.ds(pl.multiple_of(ws, qb), 3 * qb)
    q_pos = n * qb + lax.broadcasted_iota(I32, (group * qb, 3 * qb), 0) % qb
    k_pos = ws + lax.broadcasted_iota(I32, (group * qb, 3 * qb), 1)
    valid = jnp.abs(k_pos - q_pos) <= WINDOW
    head_of_row = lax.broadcasted_iota(I32, (group * qb, 1), 0) // qb
    for j in range(KV_B):
        heads = [j * group + g for g in range(group)]
        qs = jnp.concatenate(
            [qr[hh // 2][:, (hh % 2) * hd:(hh % 2 + 1) * hd] for hh in heads], axis=0)
        kw = krope_ref[k_rows, j * hd:(j + 1) * hd]
        vw = pseq_ref[k_rows, W_B + KV_W_B + j * hd:W_B + KV_W_B + (j + 1) * hd]
        kc = kctx_ref[0, 0, j].astype(BF16)
        vc = vctx_ref[0, 0, j].astype(BF16)
        s_loc = jnp.where(valid, _dot_nt(qs, kw) * ATTN_SCALE, NEG_INF)
        s_ctx = _dot_nt(qs, kc) * ATTN_SCALE
        sink = jnp.zeros((group * qb, 1), F32)
        for g, hh in enumerate(heads):
            sink = jnp.where(head_of_row == g, sink_ref[hh], sink)
        m = jnp.maximum(jnp.maximum(jnp.max(s_loc, -1, keepdims=True),
                                    jnp.max(s_ctx, -1, keepdims=True)), sink)
        p_loc = jnp.exp(s_loc - m)
        p_ctx = jnp.exp(s_ctx - m)
        den = (jnp.sum(p_loc, -1, keepdims=True) + jnp.sum(p_ctx, -1, keepdims=True)
               + jnp.exp(sink - m))
        o = (_dot(p_loc.astype(BF16), vw) + _dot(p_ctx.astype(BF16), vc)) / den
        for g, hh in enumerate(heads):
            yb_ref[:, hh * hd:(hh + 1) * hd] = o[g * qb:(g + 1) * qb].astype(BF16)


def _lat_window_attn(pb, cache_k, cache_v, layer, sink, cos_t, sin_t):
    n_blk = DEC_SEQ // WIN_QB
    lat_blk0 = T_CTX // WIN_QB
    lat_seq0 = T_CTX // DEC_SEQ
    ctx_spec = pl.BlockSpec((1, 1, KV_B, PAST_LEN, HEAD_DIM), lambda b, n: (b, layer, 0, 0, 0))
    return pl.pallas_call(
        _lat_win_kernel,
        out_shape=jax.ShapeDtypeStruct((T_LAT, W_B), BF16),
        grid=(DEC_BATCH, n_blk),
        in_specs=[
            pl.BlockSpec(memory_space=pltpu.SMEM),
            pl.BlockSpec((WIN_QB, PB_W), lambda b, n: (lat_blk0 + b * n_blk + n, 0)),
            pl.BlockSpec((DEC_SEQ, PB_W), lambda b, n: (lat_seq0 + b, 0)),
            ctx_spec, ctx_spec,
            pl.BlockSpec((DEC_SEQ, LANES), lambda b, n: (0, 0)),
            pl.BlockSpec((DEC_SEQ, LANES), lambda b, n: (0, 0)),
        ],
        out_specs=pl.BlockSpec((WIN_QB, W_B), lambda b, n: (b * n_blk + n, 0)),
        scratch_shapes=[pltpu.VMEM((DEC_SEQ, KV_W_B), BF16)],
        compiler_params=pltpu.CompilerParams(
            dimension_semantics=("arbitrary", "arbitrary"), vmem_limit_bytes=VMEM_LIMIT),
        name="lat_window_attn",
    )(sink, pb, pb, cache_k, cache_v, cos_t, sin_t)


NA_Q = NA_TILE_ROWS * GRID_W
NA_K = NA_KEY_ROWS * GRID_W
NA_TILES = DEC_SEQ // NA_Q
LAT_ROWS = DEC_SEQ // GRID_W


def _na_window_start(tile):
    return jnp.clip(tile * NA_TILE_ROWS - NA_ROWS // 2, 0, LAT_ROWS - NA_KEY_ROWS)


def _lat_na_kernel(pq_ref, pseq_ref, kctx_ref, vctx_ref, bias_ref, yc_ref):
    t = pl.program_id(1)
    hd = HEAD_DIM
    k_rows = pl.ds(pl.multiple_of(_na_window_start(t) * GRID_W, GRID_W), NA_K)
    for hh in range(H_C):
        q = pq_ref[:, hh * hd:(hh + 1) * hd]
        kw = pseq_ref[k_rows, W_C + hh * hd:W_C + (hh + 1) * hd]
        vw = pseq_ref[k_rows, 2 * W_C + hh * hd:2 * W_C + (hh + 1) * hd]
        kc = kctx_ref[0, 0, hh].astype(BF16)
        vc = vctx_ref[0, 0, hh].astype(BF16)
        s_loc = _dot_nt(q, kw) * ATTN_SCALE + bias_ref[0, hh]
        s_ctx = _dot_nt(q, kc) * ATTN_SCALE
        m = jnp.maximum(jnp.max(s_loc, -1, keepdims=True), jnp.max(s_ctx, -1, keepdims=True))
        p_loc = jnp.exp(s_loc - m)
        p_ctx = jnp.exp(s_ctx - m)
        den = jnp.sum(p_loc, -1, keepdims=True) + jnp.sum(p_ctx, -1, keepdims=True)
        o = (_dot(p_loc.astype(BF16), vw) + _dot(p_ctx.astype(BF16), vc)) / den
        yc_ref[:, hh * hd:(hh + 1) * hd] = o.astype(BF16)


def _na_tile_type(t):
    return jnp.where(t == 0, 0, jnp.where(t == NA_TILES - 1, 2, 1))


def _lat_na_attn(pc, cache_k, cache_v, layer, maskbias):
    lat_tile0 = T_CTX // NA_Q
    lat_seq0 = T_CTX // DEC_SEQ
    ctx_spec = pl.BlockSpec((1, 1, H_C, PAST_LEN, HEAD_DIM), lambda b, t: (b, layer, 0, 0, 0))
    return pl.pallas_call(
        _lat_na_kernel,
        out_shape=jax.ShapeDtypeStruct((T_LAT, W_C), BF16),
        grid=(DEC_BATCH, NA_TILES),
        in_specs=[
            pl.BlockSpec((NA_Q, PC_W), lambda b, t: (lat_tile0 + b * NA_TILES + t, 0)),
            pl.BlockSpec((DEC_SEQ, PC_W), lambda b, t: (lat_seq0 + b, 0)),
            ctx_spec, ctx_spec,
            pl.BlockSpec((1, H_C, NA_Q, NA_K), lambda b, t: (_na_tile_type(t), 0, 0, 0)),
        ],
        out_specs=pl.BlockSpec((NA_Q, W_C), lambda b, t: (b * NA_TILES + t, 0)),
        compiler_params=pltpu.CompilerParams(
            dimension_semantics=("arbitrary", "arbitrary"), vmem_limit_bytes=VMEM_LIMIT),
        name="lat_na_attn",
    )(pc, pc, cache_k, cache_v, maskbias)


def _na_block_index():
    out = np.zeros((3, NA_TILE_ROWS, NA_KEY_ROWS), np.int32)
    for ty, tile in enumerate((0, 1, NA_TILES - 1)):
        r = tile * NA_TILE_ROWS
        ws = int(np.clip(r - NA_ROWS // 2, 0, LAT_ROWS - NA_KEY_ROWS))
        for qq in range(NA_TILE_ROWS):
            qr = r + qq
            r0 = int(np.clip(qr - NA_ROWS // 2, 0, LAT_ROWS - NA_ROWS))
            for kk in range(NA_KEY_ROWS):
                kr = ws + kk
                out[ty, qq, kk] = kr - qr + NA_ROWS - 1 if r0 <= kr < r0 + NA_ROWS else 2 * NA_ROWS - 1
    return out


def _na_maskbias(rpb):
    qc = np.arange(GRID_W)[:, None]
    kc = np.arange(GRID_W)[None, :]
    c0 = np.clip(qc - NA_COLS // 2, 0, GRID_W - NA_COLS)
    col_ok = (kc >= c0) & (kc < c0 + NA_COLS)
    ci = np.clip(kc - qc + NA_COLS - 1, 0, 2 * NA_COLS - 2)
    onehot = (ci[None] == np.arange(2 * NA_COLS - 1)[:, None, None]).astype(np.float32)
    cols = jnp.einsum("hab,bqk->haqk", rpb, jnp.asarray(onehot), precision=lax.Precision.HIGHEST)
    cols = jnp.where(jnp.asarray(col_ok)[None, None], cols, NEG_INF)
    cols = jnp.concatenate([cols, jnp.full((H_C, 1, GRID_W, GRID_W), NEG_INF, F32)], axis=1)
    block_index = _na_block_index()

    def assemble(cols_ref, out_ref):
        for ty in range(3):
            for qq in range(NA_TILE_ROWS):
                for kk in range(NA_KEY_ROWS):
                    out_ref[ty, 0, qq * GRID_W:(qq + 1) * GRID_W, kk * GRID_W:(kk + 1) * GRID_W] = (
                        cols_ref[0, int(block_index[ty, qq, kk])])

    return pl.pallas_call(
        assemble,
        out_shape=jax.ShapeDtypeStruct((3, H_C, NA_Q, NA_K), F32),
        grid=(H_C,),
        in_specs=[pl.BlockSpec((1, 2 * NA_ROWS, GRID_W, GRID_W), lambda h: (h, 0, 0, 0))],
        out_specs=pl.BlockSpec((3, 1, NA_Q, NA_K), lambda h: (0, h, 0, 0)),
        compiler_params=pltpu.CompilerParams(dimension_semantics=("arbitrary",)),
        name="na_bias_assemble",
    )(cols)


def _rope_tables():
    t = np.arange(DEC_SEQ)
    n_freq = HEAD_DIM // 4
    inv = (ROPE_BASE ** (-np.arange(n_freq, dtype=np.float32) / n_freq)).astype(np.float32)
    row = (t // GRID_W).astype(np.float32)[:, None] * inv
    col = (t % GRID_W).astype(np.float32)[:, None] * inv
    ang = np.concatenate([row, col], -1)
    cos, sin = np.cos(ang), np.sin(ang)
    cos_h = np.concatenate([cos, cos], -1)
    sin_h = np.concatenate([-sin, sin], -1)
    reps = LANES // HEAD_DIM
    return (jnp.asarray(np.tile(cos_h, (1, reps)), F32), jnp.asarray(np.tile(sin_h, (1, reps)), F32))


def _first_index_of(mask, iota, sentinel):
    return jnp.min(jnp.where(mask, iota, sentinel), axis=0, keepdims=True)


def _route(logits, b_col):
    n = logits.shape[1]
    scores = jax.nn.sigmoid(logits)
    sel = scores + b_col
    io_g = lax.broadcasted_iota(I32, (GROUP_SIZE, n), 0)
    gs_rows = []
    for g in range(N_GROUPS):
        s = sel[g * GROUP_SIZE:(g + 1) * GROUP_SIZE]
        m1 = jnp.max(s, axis=0, keepdims=True)
        i1 = _first_index_of(s == m1, io_g, GROUP_SIZE)
        m2 = jnp.max(jnp.where(io_g == i1, PICKED, s), axis=0, keepdims=True)
        gs_rows.append(m1 + m2)
    gs = jnp.concatenate(gs_rows, axis=0)
    io_n = lax.broadcasted_iota(I32, (N_GROUPS, n), 0)
    gsel = jnp.zeros((N_GROUPS, n), F32)
    for _ in range(TOPK_GROUPS):
        mg = jnp.max(gs, axis=0, keepdims=True)
        gi = _first_index_of(gs == mg, io_n, N_GROUPS)
        hit = io_n == gi
        gsel = jnp.where(hit, 1.0, gsel)
        gs = jnp.where(hit, PICKED, gs)
    cand = jnp.concatenate(
        [jnp.where(gsel[g:g + 1] > 0.5, sel[g * GROUP_SIZE:(g + 1) * GROUP_SIZE], NEG_INF)
         for g in range(N_GROUPS)], axis=0)
    io_e = lax.broadcasted_iota(I32, (N_EXPERTS, n), 0)
    picks, raw = [], []
    for _ in range(TOP_K):
        mv = jnp.max(cand, axis=0, keepdims=True)
        ei = _first_index_of(cand == mv, io_e, N_EXPERTS)
        hit = io_e == ei
        picks.append((hit, ei))
        raw.append(jnp.sum(jnp.where(hit, scores, 0.0), axis=0, keepdims=True))
        cand = jnp.where(hit, PICKED, cand)
    return picks, raw


def _post_mixer_kernel(xc_ref, xl_ref, mixc_ref, ya_ref, yb_ref, yc_ref, wout_ref, mod_ref, g_ref, b_ref,
                       wr_ref, br_ref,
                       x1_ref, h2_ref, eidx_ref, wsel_ref, rank_ref, cnt_ref):
    i = pl.program_id(0)
    tm = TM_TOK

    @pl.when(i == 0)
    def _():
        cnt_ref[...] = jnp.zeros_like(cnt_ref)

    ci = _cond_row(i, tm)
    gate1 = mod_ref[pl.ds(ci, 1), 2 * D_MODEL:3 * D_MODEL]
    sh2 = mod_ref[pl.ds(ci, 1), 3 * D_MODEL:4 * D_MODEL]
    sc2 = mod_ref[pl.ds(ci, 1), 4 * D_MODEL:5 * D_MODEL]
    mix_lat = jnp.concatenate([ya_ref[...], yb_ref[...], yc_ref[...]], axis=-1)
    mix = jnp.where(i < T_CTX // tm, mixc_ref[...], mix_lat)
    y = _dot(mix, wout_ref[...])
    x = jnp.where(i < T_CTX // tm, xc_ref[...], xl_ref[...])
    x1 = _layer_norm(ALPHA * x + gate1 * y, g_ref[...], b_ref[...])
    x1_ref[...] = x1
    h2 = x1 * (1.0 + sc2) + sh2
    h_hi = h2.astype(BF16)
    h2_ref[...] = h_hi
    h_lo = (h2 - h_hi.astype(F32)).astype(BF16)
    both = (_dot(h_hi, wr_ref[...]) + _dot(h_lo, wr_ref[...])).T
    logits = both[0:N_EXPERTS] + both[N_EXPERTS:2 * N_EXPERTS]
    routed = [_route(logits[:, g * LANES:(g + 1) * LANES], br_ref[...]) for g in range(tm // LANES)]
    multi_g = []
    for picks, _ in routed:
        m = jnp.zeros((N_EXPERTS, LANES), F32)
        for hit, _ in picks:
            m = m + jnp.where(hit, 1.0, 0.0)
        multi_g.append(m)
    multi = jnp.concatenate(multi_g, axis=1)
    before = (lax.broadcasted_iota(I32, (tm, tm), 0) < lax.broadcasted_iota(I32, (tm, tm), 1))
    cum = _dot(multi.astype(BF16), jnp.where(before, 1.0, 0.0).astype(BF16))
    pad = jnp.zeros((SUBLANES - TOP_K, LANES), F32)
    for g, (picks, raw) in enumerate(routed):
        lanes = slice(g * LANES, (g + 1) * LANES)
        total = raw[0]
        for r in raw[1:]:
            total = total + r
        scale = ROUTE_SCALE / total
        cum_g = cum[:, lanes]
        eidx_ref[:, lanes] = jnp.concatenate([ei for _, ei in picks] + [pad.astype(I32)], axis=0)
        wsel_ref[:, lanes] = jnp.concatenate([r * scale for r in raw] + [pad], axis=0)
        rank_ref[:, lanes] = jnp.concatenate(
            [jnp.sum(jnp.where(hit, cum_g, 0.0), axis=0, keepdims=True) for hit, _ in picks] + [pad],
            axis=0).astype(I32)
    tile_lane = lax.broadcasted_iota(I32, (N_EXPERTS, LANES), 1)
    cnt_ref[...] = jnp.where(tile_lane == i, jnp.sum(multi, axis=1, keepdims=True), cnt_ref[...])


def _post_mixer(x_ctx, x_lat, mix_c, ya, yb, yc, w_out_bf16, mod, ln_g, ln_b, wr_split, b_router_col):
    n_ctx = T_CTX // TM_TOK
    ctx_map = lambda i: (jnp.minimum(i, n_ctx - 1), 0)
    lat_map = lambda i: (jnp.maximum(i - n_ctx, 0), 0)
    row_map = lambda i: (i, 0)
    const = lambda i: (0, 0)
    tok_map = lambda i: (0, i)
    return pl.pallas_call(
        _post_mixer_kernel,
        out_shape=(
            jax.ShapeDtypeStruct((T_ALL, D_MODEL), F32),
            jax.ShapeDtypeStruct((T_ALL, D_MODEL), BF16),
            jax.ShapeDtypeStruct((SUBLANES, T_ALL), I32),
            jax.ShapeDtypeStruct((SUBLANES, T_ALL), F32),
            jax.ShapeDtypeStruct((SUBLANES, T_ALL), I32),
            jax.ShapeDtypeStruct((N_EXPERTS, LANES), F32),
        ),
        grid=(N_TOK_TILES,),
        in_specs=[
            pl.BlockSpec((TM_TOK, D_MODEL), ctx_map),
            pl.BlockSpec((TM_TOK, D_MODEL), lat_map),
            pl.BlockSpec((TM_TOK, D_MODEL), ctx_map),
            pl.BlockSpec((TM_TOK, W_A), lat_map),
            pl.BlockSpec((TM_TOK, W_B), lat_map),
            pl.BlockSpec((TM_TOK, W_C), lat_map),
            pl.BlockSpec((D_MODEL, D_MODEL), const),
            pl.BlockSpec((N_COND, 6 * D_MODEL), const),
            pl.BlockSpec((1, D_MODEL), const),
            pl.BlockSpec((1, D_MODEL), const),
            pl.BlockSpec((D_MODEL, 2 * N_EXPERTS), const),
            pl.BlockSpec((N_EXPERTS, 1), const),
        ],
        out_specs=(
            pl.BlockSpec((TM_TOK, D_MODEL), row_map),
            pl.BlockSpec((TM_TOK, D_MODEL), row_map),
            pl.BlockSpec((SUBLANES, TM_TOK), tok_map),
            pl.BlockSpec((SUBLANES, TM_TOK), tok_map),
            pl.BlockSpec((SUBLANES, TM_TOK), tok_map),
            pl.BlockSpec((N_EXPERTS, LANES), const),
        ),
        compiler_params=pltpu.CompilerParams(
            dimension_semantics=("arbitrary",), vmem_limit_bytes=VMEM_LIMIT),
        name="post_mixer",
    )(x_ctx, x_lat, mix_c, ya, yb, yc, w_out_bf16, mod, ln_g, ln_b, wr_split, b_router_col)


PLAN_TILES = 4


def _plan_kernel(eidx_ref, rank_ref, nmat_ref, lslot_ref, unit_ref, gend_ref):
    step = pl.program_id(0)
    units = jnp.floor((nmat_ref[...] + (UNIT - 1.0)) * (1.0 / UNIT))
    units_bf = units.astype(BF16)
    earlier_e = (lax.broadcasted_iota(I32, (N_EXPERTS, N_EXPERTS), 1)
                 < lax.broadcasted_iota(I32, (N_EXPERTS, N_EXPERTS), 0))
    tri_e = jnp.where(earlier_e, 1.0, 0.0).astype(BF16)
    earlier_t = (lax.broadcasted_iota(I32, (LANES, LANES), 0) < lax.broadcasted_iota(I32, (LANES, LANES), 1))
    tri_t = jnp.where(earlier_t, 1.0, 0.0).astype(BF16)
    local_off = _dot(tri_e, units_bf)
    tile_off = _dot(units_bf, tri_t)
    per_expert = jnp.sum(units, axis=1, keepdims=True)
    blocks = jnp.floor((per_expert + (UNITS_PER_BLOCK - 1.0)) * (1.0 / UNITS_PER_BLOCK))
    blocks_l = jnp.broadcast_to(blocks, (N_EXPERTS, LANES))
    start_blk = _dot(tri_e, blocks_l.astype(BF16))
    end_blk = start_blk + blocks_l
    gend_ref[...] = (end_blk * BM).astype(I32)

    tile_lane = lax.broadcasted_iota(I32, (N_EXPERTS, LANES), 1)
    u = lax.broadcasted_iota(I32, (N_EXPERTS, MAX_UNITS), 1).astype(F32)
    io_e = lax.broadcasted_iota(I32, (N_EXPERTS, TM_TOK), 0)
    for s in range(PLAN_TILES):
        i = step * PLAN_TILES + s
        this_tile = tile_lane == i

        def column(a):
            return jnp.sum(jnp.where(this_tile, a, 0.0), axis=1, keepdims=True)

        lo, n_u = column(local_off), column(units)
        base_unit = start_blk[:, 0:1] * UNITS_PER_BLOCK + column(tile_off) - lo
        inside = jnp.where(u >= lo, jnp.where(u < lo + n_u, 1.0, 0.0), 0.0)
        dst_unit = jnp.sum(inside * (base_unit + u), axis=0, keepdims=True)
        used = jnp.sum(inside, axis=0, keepdims=True) > 0.5
        spare = (SPARE_UNIT0 + (i % 2) * MAX_UNITS).astype(F32) + u[0:1, :]
        unit_ref[s] = jnp.where(used, dst_unit, spare).astype(I32)

        toks = slice(s * TM_TOK, (s + 1) * TM_TOK)
        rows = []
        for k in range(TOP_K):
            hit = io_e == eidx_ref[k:k + 1, toks]
            seg = jnp.sum(jnp.where(hit, lo * UNIT, 0.0), axis=0, keepdims=True)
            rows.append(seg.astype(I32) + rank_ref[k:k + 1, toks])
        rows.append(jnp.full((SUBLANES - TOP_K, TM_TOK), -1, I32))
        lslot_ref[:, toks] = jnp.concatenate(rows, axis=0)


def _slot_plan(eidx, rank, nmat):
    tok_map = lambda i: (0, i)
    const = lambda i: (0, 0)
    lslot, unit_tab, gend = pl.pallas_call(
        _plan_kernel,
        out_shape=(
            jax.ShapeDtypeStruct((SUBLANES, T_ALL), I32),
            jax.ShapeDtypeStruct((N_TOK_TILES, 1, MAX_UNITS), I32),
            jax.ShapeDtypeStruct((N_EXPERTS, LANES), I32),
        ),
        grid=(N_TOK_TILES // PLAN_TILES,),
        in_specs=[
            pl.BlockSpec((SUBLANES, PLAN_TILES * TM_TOK), tok_map),
            pl.BlockSpec((SUBLANES, PLAN_TILES * TM_TOK), tok_map),
            pl.BlockSpec((N_EXPERTS, LANES), const),
        ],
        out_specs=(
            pl.BlockSpec((SUBLANES, PLAN_TILES * TM_TOK), tok_map),
            pl.BlockSpec((PLAN_TILES, 1, MAX_UNITS), lambda i: (i, 0, 0)),
            pl.BlockSpec((N_EXPERTS, LANES), const),
        ),
        compiler_params=pltpu.CompilerParams(dimension_semantics=("arbitrary",)),
        name="slot_plan",
    )(eidx, rank, nmat)
    return lslot, unit_tab.reshape(N_TOK_TILES * MAX_UNITS), gend[:, 0]


PACK_W = D_MODEL // 2
HI_HALF = -65536


def _pack_pairs(x):
    lo = lax.bitcast_convert_type(x[:, 0:PACK_W], I32)
    hi = lax.bitcast_convert_type(x[:, PACK_W:D_MODEL], I32)
    return lax.shift_right_logical(lo, 16) | (hi & HI_HALF)


def _unpack_pairs(u):
    lo = lax.bitcast_convert_type(lax.shift_left(u, 16), F32).astype(BF16)
    hi = lax.bitcast_convert_type(u & HI_HALF, F32).astype(BF16)
    return lo, hi


def _unit_rows(unit):
    row = unit * UNIT
    return pl.ds(row if isinstance(unit, int) else pl.multiple_of(row, UNIT), UNIT)


def _unit_copy(src, src_unit, dst, dst_unit, sem):
    return pltpu.make_async_copy(src.at[_unit_rows(src_unit)], dst.at[_unit_rows(dst_unit)], sem)


def _dispatch_kernel(gend_ref, tab_ref, h2_ref, lslot_ref, xs_hbm, zero_ref, local_ref, sem_zero, sem_rows):
    i = pl.program_id(0)
    buf = i % 2

    def drain(b):
        pltpu.make_async_copy(local_ref.at[b], xs_hbm.at[pl.ds(0, LOCAL_ROWS)], sem_rows.at[b]).wait()

    def has_rows(e):
        return gend_ref[e] > jnp.where(e == 0, 0, gend_ref[jnp.maximum(e - 1, 0)])

    def zero_copy(e):
        return pltpu.make_async_copy(
            zero_ref, xs_hbm.at[pl.ds(pl.multiple_of(gend_ref[e] - BM, BM), BM)], sem_zero)

    @pl.when(i == 0)
    def _():
        zero_ref[...] = jnp.zeros_like(zero_ref)

        def start(e, c):
            @pl.when(has_rows(e))
            def _():
                zero_copy(e).start()
            return c

        def wait(e, c):
            @pl.when(has_rows(e))
            def _():
                zero_copy(e).wait()
            return c

        def tail_copy(blk):
            return pltpu.make_async_copy(
                zero_ref, xs_hbm.at[pl.ds(pl.multiple_of(blk * BM, BM), BM)], sem_zero)

        def start_tail(blk, c):
            tail_copy(blk).start()
            return c

        def wait_tail(blk, c):
            tail_copy(blk).wait()
            return c

        n_used = gend_ref[N_EXPERTS - 1] // BM
        lax.fori_loop(0, N_EXPERTS, start, 0)
        lax.fori_loop(n_used, N_BLOCKS_ALL, start_tail, 0)
        lax.fori_loop(0, N_EXPERTS, wait, 0)
        lax.fori_loop(n_used, N_BLOCKS_ALL, wait_tail, 0)

    @pl.when(i >= 2)
    def _():
        drain(buf)

    h2 = h2_ref[...]
    units_per_chunk = PERM_CHUNK // UNIT
    local = local_ref.at[buf]
    for c in range(LOCAL_ROWS // PERM_CHUNK):
        slot = c * PERM_CHUNK + lax.broadcasted_iota(I32, (PERM_CHUNK, TM_TOK), 0)
        p = jnp.zeros((PERM_CHUNK, TM_TOK), F32)
        for k in range(TOP_K):
            p = jnp.where(slot == lslot_ref[k:k + 1, :], 1.0, p)
        local[c * PERM_CHUNK:(c + 1) * PERM_CHUNK, :] = _pack_pairs(_dot(p.astype(BF16), h2))
        for u in range(c * units_per_chunk, (c + 1) * units_per_chunk):
            _unit_copy(local, u, xs_hbm, tab_ref[i * MAX_UNITS + u], sem_rows.at[buf]).start()

    @pl.when(i == N_TOK_TILES - 1)
    def _():
        drain(1 - buf)
        drain(buf)


def _dispatch(h2, lslot, unit_tab, gend):
    return pl.pallas_call(
        _dispatch_kernel,
        out_shape=jax.ShapeDtypeStruct((N_SLOTS, PACK_W), I32),
        grid_spec=pltpu.PrefetchScalarGridSpec(
            num_scalar_prefetch=2,
            grid=(N_TOK_TILES,),
            in_specs=[
                pl.BlockSpec((TM_TOK, D_MODEL), lambda i, ge, tab: (i, 0)),
                pl.BlockSpec((SUBLANES, TM_TOK), lambda i, ge, tab: (0, i)),
            ],
            out_specs=pl.BlockSpec(memory_space=pl.ANY),
            scratch_shapes=[
                pltpu.VMEM((BM, PACK_W), I32),
                pltpu.VMEM((2, LOCAL_ROWS, PACK_W), I32),
                pltpu.SemaphoreType.DMA,
                pltpu.SemaphoreType.DMA((2,)),
            ],
        ),
        compiler_params=pltpu.CompilerParams(
            dimension_semantics=("arbitrary",), vmem_limit_bytes=VMEM_LIMIT),
        name="moe_dispatch",
    )(gend, unit_tab, h2, lslot)


def _expert_kernel(gend_ref, xs_hbm, wgu_ref, wdown_ref, ys_hbm, xbuf, ybuf, wgu_bf, wdown_bf, sem_in, sem_out):
    e = pl.program_id(0)
    first = jnp.where(e == 0, 0, gend_ref[jnp.maximum(e - 1, 0)]) // BM
    n_blk = gend_ref[e] // BM - first

    def rows_of(blk):
        return pl.ds(pl.multiple_of(blk * BM, BM), BM)

    def in_copy(j, b):
        return pltpu.make_async_copy(xs_hbm.at[rows_of(first + j)], xbuf.at[b], sem_in.at[b])

    def out_copy(j, b):
        return pltpu.make_async_copy(ybuf.at[b], ys_hbm.at[rows_of(first + j)], sem_out.at[b])

    @pl.when(jnp.logical_and(e == 0, n_blk > 0))
    def _():
        in_copy(0, 0).start()

    for a in range(1, X_AHEAD):
        @pl.when(n_blk > a)
        def _(a=a):
            in_copy(a, a).start()

    @pl.when(n_blk > 0)
    def _():
        wgu_bf[...] = wgu_ref[0, 0].astype(BF16)
        wdown_bf[...] = wdown_ref[0, 0].astype(BF16)

    def block(j, carry):
        b = j % 2

        @pl.when(j + X_AHEAD < n_blk)
        def _():
            in_copy(j + X_AHEAD, (j + X_AHEAD) % X_BUFS).start()

        in_copy(j, j % X_BUFS).wait()

        @pl.when(j >= 2)
        def _():
            out_copy(j - 2, b).wait()

        x_lo, x_hi = _unpack_pairs(xbuf[j % X_BUFS])
        gu = _dot(x_lo, wgu_bf[0:PACK_W, :]) + _dot(x_hi, wgu_bf[PACK_W:D_MODEL, :])
        act = _silu(gu[:, 0:D_EXPERT]) * gu[:, D_EXPERT:2 * D_EXPERT]
        y = _dot(act.astype(BF16), wdown_bf[...])
        ybuf[b] = _pack_pairs(y.astype(BF16).astype(F32))
        out_copy(j, b).start()
        return carry

    lax.fori_loop(0, n_blk, block, 0)

    nxt = jnp.minimum(e + 1, N_EXPERTS - 1)
    next_blocks = gend_ref[nxt] // BM - gend_ref[e] // BM

    @pl.when(jnp.logical_and(e + 1 < N_EXPERTS, next_blocks > 0))
    def _():
        pltpu.make_async_copy(xs_hbm.at[rows_of(gend_ref[e] // BM)], xbuf.at[0], sem_in.at[0]).start()

    @pl.when(n_blk >= 2)
    def _():
        out_copy(n_blk - 2, n_blk % 2).wait()

    @pl.when(n_blk >= 1)
    def _():
        out_copy(n_blk - 1, (n_blk - 1) % 2).wait()

    @pl.when(e == N_EXPERTS - 1)
    def _():
        ybuf[0] = jnp.zeros((BM, PACK_W), I32)
        n_used = gend_ref[N_EXPERTS - 1] // BM

        def tail_copy(blk):
            return pltpu.make_async_copy(
                ybuf.at[0], ys_hbm.at[pl.ds(pl.multiple_of(blk * BM, BM), BM)], sem_out.at[0])

        def start(blk, c):
            tail_copy(blk).start()
            return c

        def wait(blk, c):
            tail_copy(blk).wait()
            return c

        lax.fori_loop(n_used, N_BLOCKS_ALL, start, 0)
        lax.fori_loop(n_used, N_BLOCKS_ALL, wait, 0)


def _experts(xs, w_gu, w_down, layer, gend):
    return pl.pallas_call(
        _expert_kernel,
        out_shape=jax.ShapeDtypeStruct((N_SLOTS, PACK_W), I32),
        grid_spec=pltpu.PrefetchScalarGridSpec(
            num_scalar_prefetch=1,
            grid=(N_EXPERTS,),
            in_specs=[
                pl.BlockSpec(memory_space=pl.ANY),
                pl.BlockSpec((1, 1, D_MODEL, 2 * D_EXPERT), lambda e, ge: (layer, e, 0, 0)),
                pl.BlockSpec((1, 1, D_EXPERT, D_MODEL), lambda e, ge: (layer, e, 0, 0)),
            ],
            out_specs=pl.BlockSpec(memory_space=pl.ANY),
            scratch_shapes=[
                pltpu.VMEM((X_BUFS, BM, PACK_W), I32),
                pltpu.VMEM((2, BM, PACK_W), I32),
                pltpu.VMEM((D_MODEL, 2 * D_EXPERT), BF16),
                pltpu.VMEM((D_EXPERT, D_MODEL), BF16),
                pltpu.SemaphoreType.DMA((X_BUFS,)),
                pltpu.SemaphoreType.DMA((2,)),
            ],
        ),
        compiler_params=pltpu.CompilerParams(
            dimension_semantics=("arbitrary",), vmem_limit_bytes=VMEM_LIMIT),
        name="moe_experts",
    )(gend, xs, w_gu, w_down)


def _combine_kernel(tab_ref, x1_ref, h2_ref, lslot_ref, wsel_ref, ys_hbm, wsgu_ref, wsdown_ref, mod_ref,
                    g_ref, b_ref, outc_ref, outl_ref, local_ref, sel_ref, ylo_ref, yhi_ref, sem_rows):
    i = pl.program_id(0)
    tm = TM_TOK
    buf = i % 2

    def fetch_unit(tile, b, u):
        _unit_copy(ys_hbm, tab_ref[tile * MAX_UNITS + u], local_ref.at[b], u, sem_rows.at[b]).start()

    def drain(b):
        pltpu.make_async_copy(ys_hbm.at[pl.ds(0, LOCAL_ROWS)], local_ref.at[b], sem_rows.at[b]).wait()

    @pl.when(i == 0)
    def _():
        def body(u, c):
            fetch_unit(0, 0, u)
            return c

        lax.fori_loop(0, MAX_UNITS, body, 0, unroll=8)

    sgu = _dot(h2_ref[...], wsgu_ref[...])
    act = _silu(sgu[:, 0:D_SHARED]) * sgu[:, D_SHARED:2 * D_SHARED]
    f = _dot(act.astype(BF16), wsdown_ref[...])

    nxt = jnp.minimum(i + 1, N_TOK_TILES - 1)
    n_groups = tm // SEL_ROWS
    units_per_group = MAX_UNITS // n_groups
    slot = lax.broadcasted_iota(I32, (SEL_ROWS, LOCAL_ROWS), 1)
    for g in range(n_groups):
        rows = slice(g * SEL_ROWS, (g + 1) * SEL_ROWS)
        sel = jnp.zeros((SEL_ROWS, LOCAL_ROWS), F32)
        for k in range(TOP_K):
            sel = jnp.where(slot == lslot_ref[rows, k:k + 1], wsel_ref[rows, k:k + 1], sel)
        sel_ref[rows, :] = sel.astype(BF16)
        for u in range(g * units_per_group, (g + 1) * units_per_group):
            fetch_unit(nxt, 1 - buf, u)

    local = local_ref.at[buf]
    drain(buf)
    for c in range(LOCAL_ROWS // PERM_CHUNK):
        rows = slice(c * PERM_CHUNK, (c + 1) * PERM_CHUNK)
        ylo_ref[rows, :], yhi_ref[rows, :] = _unpack_pairs(local[rows, :])
    sel = sel_ref[...]
    f = f + jnp.concatenate([_dot(sel, ylo_ref[...]), _dot(sel, yhi_ref[...])], axis=-1)
    ci = _cond_row(i, tm)
    gate2 = mod_ref[pl.ds(ci, 1), 5 * D_MODEL:6 * D_MODEL]
    out = _layer_norm(ALPHA * x1_ref[...] + gate2 * f, g_ref[...], b_ref[...])

    @pl.when(i < T_CTX // tm)
    def _():
        outc_ref[...] = out

    @pl.when(i >= T_CTX // tm)
    def _():
        outl_ref[...] = out

    @pl.when(i == N_TOK_TILES - 1)
    def _():
        drain(1 - buf)


def _combine(x1, h2, lslot_rows, wsel_rows, unit_tab, ys, w_sgu_bf16, w_sdown_bf16, mod, ln_g, ln_b):
    n_ctx = T_CTX // TM_TOK
    row_map = lambda i, tab: (i, 0)
    const = lambda i, tab: (0, 0)
    return pl.pallas_call(
        _combine_kernel,
        out_shape=(jax.ShapeDtypeStruct((T_CTX, D_MODEL), F32),
                   jax.ShapeDtypeStruct((T_LAT, D_MODEL), F32)),
        grid_spec=pltpu.PrefetchScalarGridSpec(
            num_scalar_prefetch=1,
            grid=(N_TOK_TILES,),
            in_specs=[
                pl.BlockSpec((TM_TOK, D_MODEL), row_map),
                pl.BlockSpec((TM_TOK, D_MODEL), row_map),
                pl.BlockSpec((TM_TOK, SUBLANES), row_map),
                pl.BlockSpec((TM_TOK, SUBLANES), row_map),
                pl.BlockSpec(memory_space=pl.ANY),
                pl.BlockSpec((D_MODEL, 2 * D_SHARED), const),
                pl.BlockSpec((D_SHARED, D_MODEL), const),
                pl.BlockSpec((N_COND, 6 * D_MODEL), const),
                pl.BlockSpec((1, D_MODEL), const),
                pl.BlockSpec((1, D_MODEL), const),
            ],
            out_specs=(pl.BlockSpec((TM_TOK, D_MODEL), lambda i, tab: (jnp.minimum(i, n_ctx - 1), 0)),
                       pl.BlockSpec((TM_TOK, D_MODEL), lambda i, tab: (jnp.maximum(i - n_ctx, 0), 0))),
            scratch_shapes=[
                pltpu.VMEM((2, LOCAL_ROWS, PACK_W), I32),
                pltpu.VMEM((TM_TOK, LOCAL_ROWS), BF16),
                pltpu.VMEM((LOCAL_ROWS, PACK_W), BF16),
                pltpu.VMEM((LOCAL_ROWS, PACK_W), BF16),
                pltpu.SemaphoreType.DMA((2,)),
            ],
        ),
        compiler_params=pltpu.CompilerParams(
            dimension_semantics=("arbitrary",), vmem_limit_bytes=VMEM_LIMIT),
        name="moe_combine",
    )(unit_tab, x1, h2, lslot_rows, wsel_rows, ys, w_sgu_bf16, w_sdown_bf16, mod, ln_g, ln_b)


def _lane_rows(v, width):
    return jnp.broadcast_to(v.astype(F32)[:, None, None], (v.shape[0], 1, width))


def kernel(x_prompt, x_sample, state_ret_fwd, state_ret_bwd, cache_win_k, cache_win_v, cache_na_k, cache_na_v, c, c_ctx, w_in, w_out, ret_decay_fwd, ret_decay_bwd, ret_gn_g, ret_gn_b, win_sink, na_rpb, w_mod, b_mod, ln1_g, ln1_b, ln2_g, ln2_b, w_router, b_router, w_expert_gu, w_expert_down, w_shared_gu, w_shared_down):
    cond = jnp.concatenate(
        [c_ctx[None, :], c, jnp.zeros((N_COND - 1 - DEC_BATCH, D_MODEL), F32)], axis=0)
    mod_all = _modulation(cond, w_mod, b_mod)
    cos_t, sin_t = _rope_tables()

    x_ctx = x_prompt.reshape(T_CTX, D_MODEL)
    x_lat = x_sample.reshape(T_LAT, D_MODEL)
    sf_l, sb_l, caches = [], [], ()
    for l in range(DEPTH):
        mod = mod_all[l]
        pa, pb, pc, *caches = _in_projection(x_ctx, x_lat, mod, w_in[l].astype(BF16), tuple(caches))
        decf_s, decb_s = _lane_rows(ret_decay_fwd[l], SEQ), _lane_rows(ret_decay_bwd[l], SEQ)
        gng, gnb = ret_gn_g[l][None, :], ret_gn_b[l][None, :]
        mix_c, st_f, st_b = _ctx_mixers(pa, pb, pc, win_sink[l], decf_s, decb_s, gng, gnb)
        sf_l.append(st_f)
        sb_l.append(st_b)
        ya = _lat_retention(pa, state_ret_fwd, state_ret_bwd, l,
                            _lane_rows(ret_decay_fwd[l], RET_CHUNK), _lane_rows(ret_decay_bwd[l], RET_CHUNK),
                            gng, gnb)
        yb = _lat_window_attn(pb, cache_win_k, cache_win_v, l, win_sink[l], cos_t, sin_t)
        yc = _lat_na_attn(pc, cache_na_k, cache_na_v, l, _na_maskbias(na_rpb[l]))

        wr_hi = w_router[l].astype(BF16)
        wr_lo = (w_router[l] - wr_hi.astype(F32)).astype(BF16)
        x1, h2, eidx, wsel, rank, counts = _post_mixer(
            x_ctx, x_lat, mix_c, ya, yb, yc, w_out[l].astype(BF16), mod, ln1_g[l][None, :],
            ln1_b[l][None, :], jnp.concatenate([wr_hi, wr_lo], axis=1), b_router[l][:, None])
        lslot, unit_tab, gend = _slot_plan(eidx, rank, counts)
        xs = _dispatch(h2, lslot, unit_tab, gend)
        ys = _experts(xs, w_expert_gu, w_expert_down, l, gend)
        x_ctx, x_lat = _combine(x1, h2, lslot.T, wsel.T, unit_tab, ys, w_shared_gu[l].astype(BF16),
                                w_shared_down[l].astype(BF16), mod, ln2_g[l][None, :], ln2_b[l][None, :])

    y_prompt = x_ctx.reshape(BATCH, SEQ, D_MODEL)
    y_sample = x_lat.reshape(DEC_BATCH, DEC_SEQ, D_MODEL)
    new_sf = jnp.stack(sf_l, axis=1)
    new_sb = jnp.stack(sb_l, axis=1)

    return (y_prompt, y_sample, new_sf, new_sb, *caches)
```

```python
import functools

import numpy as np
import jax
import jax.numpy as jnp
from jax import lax
from jax.experimental import pallas as pl
from jax.experimental.pallas import tpu as pltpu

F32 = jnp.float32
BF16 = jnp.bfloat16
I32 = jnp.int32

D_MODEL = 1024
BATCH = 32
SEQ = 256
DEPTH = 2
DEC_BATCH = 4
DEC_SEQ = 2048
PAST_LEN = 256
GRID_W = 64
HEAD_DIM = 64
ATTN_SCALE = HEAD_DIM ** -0.5
H_A = 4
W_A = H_A * HEAD_DIM
GN_EPS = 1e-5
H_B = 6
KV_B = 2
W_B = H_B * HEAD_DIM
KV_W_B = KV_B * HEAD_DIM
WINDOW = 128
ROPE_BASE = 10000.0
H_C = 6
W_C = H_C * HEAD_DIM
NA_ROWS = 8
NA_COLS = 16
IN_WIDTH = 4 * W_A + W_B + 2 * KV_W_B + 3 * W_C
N_EXPERTS = 64
TOP_K = 6
N_GROUPS = 8
GROUP_SIZE = N_EXPERTS // N_GROUPS
TOPK_GROUPS = 4
D_EXPERT = 256
D_SHARED = 256
ROUTE_SCALE = 2.5
ALPHA = (2 * DEPTH) ** 0.25
LN_EPS = 1e-5
NEG_INF = -1e30
PICKED = -3e38

T_CTX = BATCH * SEQ
T_LAT = DEC_BATCH * DEC_SEQ
T_ALL = T_CTX + T_LAT
N_COND = 8

PA_W = 4 * W_A
PB_W = W_B + 2 * KV_W_B
PC_W = 3 * W_C

LANES = 128
SUBLANES = 8
VMEM_LIMIT = 56 * 1024 * 1024

TM_PROJ = 512
TM_TOK = 256
RET_CHUNK = 256
WIN_QB = 128
NA_TILE_ROWS = 4
NA_KEY_ROWS = 11
BM = 512
X_AHEAD = 3
X_BUFS = X_AHEAD + 1
UNIT = SUBLANES
UNITS_PER_BLOCK = BM // UNIT
PERM_CHUNK = 256
SEL_ROWS = 16
N_TOK_TILES = T_ALL // TM_TOK
LOCAL_ROWS = -(-(TM_TOK * TOP_K + N_EXPERTS * (UNIT - 1)) // PERM_CHUNK) * PERM_CHUNK
MAX_UNITS = LOCAL_ROWS // UNIT
N_ASSIGN = T_ALL * TOP_K
N_BLOCKS = -(-(N_ASSIGN + N_TOK_TILES * N_EXPERTS * (UNIT - 1) + N_EXPERTS * (BM - 1)) // BM)
SPARE_BLOCKS = 2 * -(-LOCAL_ROWS // BM)
SPARE_UNIT0 = N_BLOCKS * UNITS_PER_BLOCK
N_BLOCKS_ALL = N_BLOCKS + SPARE_BLOCKS
N_SLOTS = N_BLOCKS_ALL * BM


def _dot(a, b):
    return jnp.dot(a, b, preferred_element_type=F32)


def _dot_nt(a, b):
    return lax.dot_general(a, b, (((1,), (1,)), ((), ())), preferred_element_type=F32)


def _silu(x):
    return x * jax.nn.sigmoid(x)


def _log_sigmoid(x):
    return jnp.minimum(x, 0.0) - jnp.log(1.0 + jnp.exp(-jnp.abs(x)))


def _cond_row(tile, tile_rows):
    n_ctx = T_CTX // tile_rows
    per_lat = DEC_SEQ // tile_rows
    return jnp.where(tile < n_ctx, 0, 1 + (tile - n_ctx) // per_lat)


def _layer_norm(x, g, b):
    mu = jnp.mean(x, -1, keepdims=True)
    xc = x - mu
    var = jnp.mean(xc * xc, -1, keepdims=True)
    return xc * lax.rsqrt(var + LN_EPS) * g + b


MOD_TN = 1536


def _mod_kernel(cond_ref, w_ref, b_ref, o_ref):
    s = _silu(cond_ref[...])
    s_hi = s.astype(BF16)
    s_lo = (s - s_hi.astype(F32)).astype(BF16)
    w = w_ref[0]
    w_hi = w.astype(BF16)
    w_lo = (w - w_hi.astype(F32)).astype(BF16)
    o_ref[0] = _dot(s_hi, w_hi) + _dot(s_lo, w_hi) + _dot(s_hi, w_lo) + b_ref[0]


def _modulation(cond, w_mod, b_mod):
    n_out = 6 * D_MODEL
    return pl.pallas_call(
        _mod_kernel,
        out_shape=jax.ShapeDtypeStruct((DEPTH, N_COND, n_out), F32),
        grid=(DEPTH, n_out // MOD_TN),
        in_specs=[
            pl.BlockSpec((N_COND, D_MODEL), lambda l, j: (0, 0)),
            pl.BlockSpec((1, D_MODEL, MOD_TN), lambda l, j: (l, 0, j)),
            pl.BlockSpec((1, 1, MOD_TN), lambda l, j: (l, 0, j)),
        ],
        out_specs=pl.BlockSpec((1, N_COND, MOD_TN), lambda l, j: (l, 0, j)),
        compiler_params=pltpu.CompilerParams(
            dimension_semantics=("arbitrary", "arbitrary"), vmem_limit_bytes=VMEM_LIMIT),
        name="modulation",
    )(cond, w_mod, b_mod.reshape(DEPTH, 1, n_out))


SEQ_PER_PROJ = TM_PROJ // SEQ


def _inproj_kernel(n_prev, xc_ref, xl_ref, mod_ref, w_ref, *refs):
    prev_refs = refs[:4] if n_prev else ()
    pa_ref, pb_ref, pc_ref, wk_ref, wv_ref, nk_ref, nv_ref = refs[len(prev_refs):]
    i = pl.program_id(0)
    ci = _cond_row(i, TM_PROJ)
    sh = mod_ref[pl.ds(ci, 1), 0:D_MODEL]
    sc = mod_ref[pl.ds(ci, 1), D_MODEL:2 * D_MODEL]
    x = jnp.where(i < T_CTX // TM_PROJ, xc_ref[...], xl_ref[...])
    h = x * (1.0 + sc) + sh
    p = _dot(h.astype(BF16), w_ref[...])
    pa_ref[...] = p[:, 0:PA_W].astype(BF16)
    pb_ref[...] = p[:, PA_W:PA_W + PB_W].astype(BF16)
    pc_ref[...] = p[:, PA_W + PB_W:IN_WIDTH].astype(BF16)

    @pl.when(i < T_CTX // TM_PROJ)
    def _():
        targets = ((wk_ref, PA_W + W_B, KV_B), (wv_ref, PA_W + W_B + KV_W_B, KV_B),
                   (nk_ref, PA_W + PB_W + W_C, H_C), (nv_ref, PA_W + PB_W + 2 * W_C, H_C))
        for j, (ref, col0, n_heads) in enumerate(targets):
            if n_prev:
                ref[:, 0:n_prev] = prev_refs[j][...]
            for s in range(SEQ_PER_PROJ):
                for hh in range(n_heads):
                    ref[s, n_prev, hh] = p[s * SEQ:(s + 1) * SEQ,
                                           col0 + hh * HEAD_DIM:col0 + (hh + 1) * HEAD_DIM]


def _in_projection(x_ctx, x_lat, mod, w_in_bf16, earlier):
    n_ctx_tiles = T_CTX // TM_PROJ
    n_prev = earlier[0].shape[1] if earlier else 0

    def cache_spec(n_layers, n_heads):
        return pl.BlockSpec((SEQ_PER_PROJ, n_layers, n_heads, SEQ, HEAD_DIM),
                            lambda i: (jnp.minimum(i, n_ctx_tiles - 1), 0, 0, 0, 0))

    cache_heads = (KV_B, KV_B, H_C, H_C)
    return pl.pallas_call(
        functools.partial(_inproj_kernel, n_prev),
        out_shape=(
            jax.ShapeDtypeStruct((T_ALL, PA_W), BF16),
            jax.ShapeDtypeStruct((T_ALL, PB_W), BF16),
            jax.ShapeDtypeStruct((T_ALL, PC_W), BF16),
        ) + tuple(jax.ShapeDtypeStruct((BATCH, n_prev + 1, nh, SEQ, HEAD_DIM), F32) for nh in cache_heads),
        grid=(T_ALL // TM_PROJ,),
        in_specs=[
            pl.BlockSpec((TM_PROJ, D_MODEL), lambda i: (jnp.minimum(i, n_ctx_tiles - 1), 0)),
            pl.BlockSpec((TM_PROJ, D_MODEL), lambda i: (jnp.maximum(i - n_ctx_tiles, 0), 0)),
            pl.BlockSpec((N_COND, 6 * D_MODEL), lambda i: (0, 0)),
            pl.BlockSpec((D_MODEL, IN_WIDTH), lambda i: (0, 0)),
        ] + [cache_spec(n_prev, nh) for nh in cache_heads if n_prev],
        out_specs=(
            pl.BlockSpec((TM_PROJ, PA_W), lambda i: (i, 0)),
            pl.BlockSpec((TM_PROJ, PB_W), lambda i: (i, 0)),
            pl.BlockSpec((TM_PROJ, PC_W), lambda i: (i, 0)),
        ) + tuple(cache_spec(n_prev + 1, nh) for nh in cache_heads),
        compiler_params=pltpu.CompilerParams(
            dimension_semantics=("arbitrary",), vmem_limit_bytes=VMEM_LIMIT),
        name="in_projection",
    )(x_ctx, x_lat, mod, w_in_bf16, *earlier)


def _decay_matrix(lg_f, lg_b, n):
    row = lax.broadcasted_iota(I32, (n, n), 0)
    col = lax.broadcasted_iota(I32, (n, n), 1)
    diff = (row - col).astype(F32)
    fwd = jnp.where(diff >= 0, jnp.exp(lg_f * jnp.maximum(diff, 0.0)), 0.0)
    bwd = jnp.where(diff <= 0, jnp.exp(lg_b * jnp.maximum(-diff, 0.0)), 0.0)
    return (fwd + bwd) * ATTN_SCALE


def _retention_readout(o, gate, g, b):
    mu = jnp.mean(o, -1, keepdims=True)
    oc = o - mu
    var = jnp.mean(oc * oc, -1, keepdims=True)
    on = oc * lax.rsqrt(var + GN_EPS) * g + b
    return on * _silu(gate.astype(F32))


def _stacked_softmax_attend(scores, values, extra_logit=None):
    rows = scores[0].shape[0]
    s = jnp.concatenate(scores, axis=0)
    m = jnp.max(s, -1, keepdims=True)
    if extra_logit is not None:
        m = jnp.maximum(m, extra_logit)
    p = jnp.exp(s - m)
    den = jnp.sum(p, -1, keepdims=True)
    if extra_logit is not None:
        den = den + jnp.exp(extra_logit - m)
    p = p.astype(BF16)
    return [_dot(p[h * rows:(h + 1) * rows], v) / den[h * rows:(h + 1) * rows]
            for h, v in enumerate(values)]


def _lane_xor_matrix(width, distance):
    r = lax.broadcasted_iota(I32, (width, width), 0)
    c = lax.broadcasted_iota(I32, (width, width), 1)
    return jnp.where((r ^ distance) == c, 1.0, 0.0).astype(BF16)


def _ctx_mixer_kernel(sink_ref, pa_ref, pb_ref, pc_ref, decf_ref, decb_ref, gng_ref, gnb_ref,
                      mix_ref, sf_ref, sb_ref, dmat_ref, zf_ref, zb_ref):
    n = SEQ
    hd = HEAD_DIM

    @pl.when(pl.program_id(0) == 0)
    def _():
        pos = lax.broadcasted_iota(I32, (n, hd), 0).astype(F32)
        zf, zb = [], []
        for h in range(H_A):
            lg_f = _log_sigmoid(decf_ref[h])
            lg_b = _log_sigmoid(decb_ref[h])
            dmat_ref[h] = _decay_matrix(lg_f, lg_b, n)
            zf.append(jnp.exp(lg_f[:, 0:hd] * (n - 1.0 - pos)) * ATTN_SCALE)
            zb.append(jnp.exp(lg_b[:, 0:hd] * pos) * ATTN_SCALE)
        for p in range(H_A // 2):
            zf_ref[p] = jnp.concatenate(zf[2 * p:2 * p + 2], axis=1)
            zb_ref[p] = jnp.concatenate(zb[2 * p:2 * p + 2], axis=1)

    low = lax.broadcasted_iota(I32, (n, LANES), 1) < hd

    def own_half(x, h):
        zero = jnp.zeros_like(x)
        return jnp.where(low, x, zero) if h % 2 == 0 else jnp.where(low, zero, x)

    def merge(first, second):
        return jnp.where(low, first, second)

    def pair_cols(ref, base, p):
        return ref[:, base + p * LANES:base + (p + 1) * LANES]

    def head_mean(x):
        first = jnp.sum(jnp.where(low, x, 0.0), -1, keepdims=True)
        second = jnp.sum(jnp.where(low, 0.0, x), -1, keepdims=True)
        return merge(first, second) * (1.0 / hd)

    for p in range(H_A // 2):
        q_pair, k_pair = pair_cols(pa_ref, 0, p), pair_cols(pa_ref, W_A, p)
        v_pair, gate_pair = pair_cols(pa_ref, 2 * W_A, p), pair_cols(pa_ref, 3 * W_A, p)
        outs = []
        for h in (2 * p, 2 * p + 1):
            a = _dot_nt(own_half(q_pair, h), k_pair)
            outs.append(_dot((a * dmat_ref[h]).astype(BF16), v_pair))
        o = merge(outs[0], outs[1])
        kf = k_pair.astype(F32)
        for st_ref, z_ref in ((sf_ref, zf_ref), (sb_ref, zb_ref)):
            st = _dot((kf * z_ref[p]).T.astype(BF16), v_pair)
            st_ref[0, 2 * p] = st[0:hd, 0:hd]
            st_ref[0, 2 * p + 1] = st[hd:2 * hd, hd:2 * hd]
        mu = head_mean(o)
        oc = o - mu
        var = head_mean(oc * oc)
        on = oc * lax.rsqrt(var + GN_EPS) * pair_cols(gng_ref, 0, p) + pair_cols(gnb_ref, 0, p)
        mix_ref[:, p * LANES:(p + 1) * LANES] = (on * _silu(gate_pair.astype(F32))).astype(BF16)

    group = H_B // KV_B
    swap = _lane_xor_matrix(LANES, hd)
    kv_k = pb_ref[:, W_B:W_B + KV_W_B]
    kv_v = pb_ref[:, W_B + KV_W_B:W_B + 2 * KV_W_B]
    scores = []
    for hh in range(H_B):
        q = own_half(pair_cols(pb_ref, 0, hh // 2), hh)
        if hh % 2 != hh // group:
            q = _dot(q, swap).astype(BF16)
        scores.append(_dot_nt(q, kv_k) * ATTN_SCALE)
    sinks = jnp.concatenate([jnp.full((n, 1), sink_ref[hh], F32) for hh in range(H_B)], axis=0)
    outs = []
    for hh, o in enumerate(_stacked_softmax_attend(scores, [kv_v] * H_B, sinks)):
        o = o.astype(BF16)
        outs.append(_dot(o, swap).astype(BF16) if hh % 2 != hh // group else o)
    for p in range(H_B // 2):
        mix_ref[:, W_A + p * LANES:W_A + (p + 1) * LANES] = merge(outs[2 * p], outs[2 * p + 1])

    scores, values = [], []
    for hh in range(H_C):
        q = own_half(pair_cols(pc_ref, 0, hh // 2), hh)
        scores.append(_dot_nt(q, pair_cols(pc_ref, W_C, hh // 2)) * ATTN_SCALE)
        values.append(pair_cols(pc_ref, 2 * W_C, hh // 2))
    outs = _stacked_softmax_attend(scores, values)
    for p in range(H_C // 2):
        mix_ref[:, W_A + W_B + p * LANES:W_A + W_B + (p + 1) * LANES] = merge(
            outs[2 * p], outs[2 * p + 1]).astype(BF16)


def _ctx_mixers(pa, pb, pc, sink, decf, decb, gng, gnb):
    return pl.pallas_call(
        _ctx_mixer_kernel,
        out_shape=(
            jax.ShapeDtypeStruct((T_CTX, D_MODEL), BF16),
            jax.ShapeDtypeStruct((BATCH, H_A, HEAD_DIM, HEAD_DIM), F32),
            jax.ShapeDtypeStruct((BATCH, H_A, HEAD_DIM, HEAD_DIM), F32),
        ),
        grid=(BATCH,),
        in_specs=[
            pl.BlockSpec(memory_space=pltpu.SMEM),
            pl.BlockSpec((SEQ, PA_W), lambda b: (b, 0)),
            pl.BlockSpec((SEQ, PB_W), lambda b: (b, 0)),
            pl.BlockSpec((SEQ, PC_W), lambda b: (b, 0)),
            pl.BlockSpec((H_A, 1, SEQ), lambda b: (0, 0, 0)),
            pl.BlockSpec((H_A, 1, SEQ), lambda b: (0, 0, 0)),
            pl.BlockSpec((1, W_A), lambda b: (0, 0)),
            pl.BlockSpec((1, W_A), lambda b: (0, 0)),
        ],
        out_specs=(
            pl.BlockSpec((SEQ, D_MODEL), lambda b: (b, 0)),
            pl.BlockSpec((1, H_A, HEAD_DIM, HEAD_DIM), lambda b: (b, 0, 0, 0)),
            pl.BlockSpec((1, H_A, HEAD_DIM, HEAD_DIM), lambda b: (b, 0, 0, 0)),
        ),
        scratch_shapes=[
            pltpu.VMEM((H_A, SEQ, SEQ), F32),
            pltpu.VMEM((H_A // 2, SEQ, LANES), F32),
            pltpu.VMEM((H_A // 2, SEQ, LANES), F32),
        ],
        compiler_params=pltpu.CompilerParams(
            dimension_semantics=("arbitrary",), vmem_limit_bytes=VMEM_LIMIT),
        name="ctx_mixers",
    )(sink, pa, pb, pc, decf, decb, gng, gnb)


def _lat_ret_kernel(pa_ref, stf_ref, stb_ref, decf_ref, decb_ref, gng_ref, gnb_ref, ya_ref, acc_ref):
    c = RET_CHUNK
    hd = HEAD_DIM
    n_chunks = DEC_SEQ // c
    pos = lax.broadcasted_iota(I32, (c, hd), 0).astype(F32)
    for h in range(H_A):
        c0 = h * hd
        lg_f = _log_sigmoid(decf_ref[h])
        lg_b = _log_sigmoid(decb_ref[h])
        dmat = _decay_matrix(lg_f, lg_b, c)
        lf = lg_f[:, 0:hd]
        lb = lg_b[:, 0:hd]
        zf = jnp.exp(lf * (c - 1.0 - pos)) * ATTN_SCALE
        zb = jnp.exp(lb * pos) * ATTN_SCALE
        xf = jnp.exp(lf * (pos + 1.0))
        xb = jnp.exp(lb * (c - pos))
        gcf = jnp.exp(lf * float(c))
        gcb = jnp.exp(lb * float(c))
        g = gng_ref[:, c0:c0 + hd]
        b = gnb_ref[:, c0:c0 + hd]

        def load(i, off):
            rows = pl.ds(pl.multiple_of(i * c, c), c)
            return pa_ref[rows, off + c0:off + c0 + hd]

        def fwd(i, s):
            q, k, v = load(i, 0), load(i, W_A), load(i, 2 * W_A)
            o = _dot((_dot_nt(q, k) * dmat).astype(BF16), v)
            o = o + _dot((q.astype(F32) * xf).astype(BF16), s.astype(BF16))
            acc_ref[pl.ds(pl.multiple_of(i * c, c), c), c0:c0 + hd] = o
            return gcf * s + _dot((k.astype(F32) * zf).T.astype(BF16), v)

        lax.fori_loop(0, n_chunks, fwd, stf_ref[0, 0, h])

        def bwd(j, s):
            i = n_chunks - 1 - j
            rows = pl.ds(pl.multiple_of(i * c, c), c)
            q, k, v = load(i, 0), load(i, W_A), load(i, 2 * W_A)
            o = acc_ref[rows, c0:c0 + hd] + _dot((q.astype(F32) * xb).astype(BF16), s.astype(BF16))
            y = _retention_readout(o, load(i, 3 * W_A), g, b)
            ya_ref[rows, c0:c0 + hd] = y.astype(BF16)
            return gcb * s + _dot((k.astype(F32) * zb).T.astype(BF16), v)

        lax.fori_loop(0, n_chunks, bwd, stb_ref[0, 0, h])


def _lat_retention(pa, st_f, st_b, layer, decf, decb, gng, gnb):
    lat0 = T_CTX // DEC_SEQ
    st_spec = pl.BlockSpec((1, 1, H_A, HEAD_DIM, HEAD_DIM), lambda b: (b, layer, 0, 0, 0))
    return pl.pallas_call(
        _lat_ret_kernel,
        out_shape=jax.ShapeDtypeStruct((T_LAT, W_A), BF16),
        grid=(DEC_BATCH,),
        in_specs=[
            pl.BlockSpec((DEC_SEQ, PA_W), lambda b: (lat0 + b, 0)),
            st_spec, st_spec,
            pl.BlockSpec((H_A, 1, RET_CHUNK), lambda b: (0, 0, 0)),
            pl.BlockSpec((H_A, 1, RET_CHUNK), lambda b: (0, 0, 0)),
            pl.BlockSpec((1, W_A), lambda b: (0, 0)),
            pl.BlockSpec((1, W_A), lambda b: (0, 0)),
        ],
        out_specs=pl.BlockSpec((DEC_SEQ, W_A), lambda b: (b, 0)),
        scratch_shapes=[pltpu.VMEM((DEC_SEQ, W_A), F32)],
        compiler_params=pltpu.CompilerParams(
            dimension_semantics=("arbitrary",), vmem_limit_bytes=VMEM_LIMIT),
        name="lat_retention",
    )(pa, st_f, st_b, decf, decb, gng, gnb)


def _rope(x, cos, sin_signed, swap):
    return x.astype(F32) * cos + _dot(x, swap) * sin_signed


def _lat_win_kernel(sink_ref, pq_ref, pseq_ref, kctx_ref, vctx_ref, cos_ref, sin_ref, yb_ref, krope_ref):
    n = pl.program_id(1)
    hd = HEAD_DIM
    qb = WIN_QB
    n_blk = DEC_SEQ // qb
    group = H_B // KV_B
    swap = _lane_xor_matrix(LANES, HEAD_DIM // 2)

    @pl.when(n == 0)
    def _():
        k = pseq_ref[:, W_B:W_B + KV_W_B]
        krope_ref[...] = _rope(k, cos_ref[...], sin_ref[...], swap).astype(BF16)

    q_rows = pl.ds(pl.multiple_of(n * qb, qb), qb)
    cos_q = cos_ref[q_rows, :]
    sin_q = sin_ref[q_rows, :]
    qr = [_rope(pq_ref[:, p * LANES:(p + 1) * LANES], cos_q, sin_q, swap).astype(BF16)
          for p in range(W_B // LANES)]

    ws = jnp.clip(n - 1, 0, n_blk - 3) * qb
    k_rows = pl.ds(pl.multiple_of(ws, qb), 3 * qb)
    q_pos = n * qb + lax.broadcasted_iota(I32, (group * qb, 3 * qb), 0) % qb
    k_pos = ws + lax.broadcasted_iota(I32, (group * qb, 3 * qb), 1)
    valid = jnp.abs(k_pos - q_pos) <= WINDOW
    head_of_row = lax.broadcasted_iota(I32, (group * qb, 1), 0) // qb
    for j in range(KV_B):
        heads = [j * group + g for g in range(group)]
        qs = jnp.concatenate(
            [qr[hh // 2][:, (hh % 2) * hd:(hh % 2 + 1) * hd] for hh in heads], axis=0)
        kw = krope_ref[k_rows, j * hd:(j + 1) * hd]
        vw = pseq_ref[k_rows, W_B + KV_W_B + j * hd:W_B + KV_W_B + (j + 1) * hd]
        kc = kctx_ref[0, 0, j].astype(BF16)
        vc = vctx_ref[0, 0, j].astype(BF16)
        s_loc = jnp.where(valid, _dot_nt(qs, kw) * ATTN_SCALE, NEG_INF)
        s_ctx = _dot_nt(qs, kc) * ATTN_SCALE
        sink = jnp.zeros((group * qb, 1), F32)
        for g, hh in enumerate(heads):
            sink = jnp.where(head_of_row == g, sink_ref[hh], sink)
        m = jnp.maximum(jnp.maximum(jnp.max(s_loc, -1, keepdims=True),
                                    jnp.max(s_ctx, -1, keepdims=True)), sink)
        p_loc = jnp.exp(s_loc - m)
        p_ctx = jnp.exp(s_ctx - m)
        den = (jnp.sum(p_loc, -1, keepdims=True) + jnp.sum(p_ctx, -1, keepdims=True)
               + jnp.exp(sink - m))
        o = (_dot(p_loc.astype(BF16), vw) + _dot(p_ctx.astype(BF16), vc)) / den
        for g, hh in enumerate(heads):
            yb_ref[:, hh * hd:(hh + 1) * hd] = o[g * qb:(g + 1) * qb].astype(BF16)


def _lat_window_attn(pb, cache_k, cache_v, layer, sink, cos_t, sin_t):
    n_blk = DEC_SEQ // WIN_QB
    lat_blk0 = T_CTX // WIN_QB
    lat_seq0 = T_CTX // DEC_SEQ
    ctx_spec = pl.BlockSpec((1, 1, KV_B, PAST_LEN, HEAD_DIM), lambda b, n: (b, layer, 0, 0, 0))
    return pl.pallas_call(
        _lat_win_kernel,
        out_shape=jax.ShapeDtypeStruct((T_LAT, W_B), BF16),
        grid=(DEC_BATCH, n_blk),
        in_specs=[
            pl.BlockSpec(memory_space=pltpu.SMEM),
            pl.BlockSpec((WIN_QB, PB_W), lambda b, n: (lat_blk0 + b * n_blk + n, 0)),
            pl.BlockSpec((DEC_SEQ, PB_W), lambda b, n: (lat_seq0 + b, 0)),
            ctx_spec, ctx_spec,
            pl.BlockSpec((DEC_SEQ, LANES), lambda b, n: (0, 0)),
            pl.BlockSpec((DEC_SEQ, LANES), lambda b, n: (0, 0)),
        ],
        out_specs=pl.BlockSpec((WIN_QB, W_B), lambda b, n: (b * n_blk + n, 0)),
        scratch_shapes=[pltpu.VMEM((DEC_SEQ, KV_W_B), BF16)],
        compiler_params=pltpu.CompilerParams(
            dimension_semantics=("arbitrary", "arbitrary"), vmem_limit_bytes=VMEM_LIMIT),
        name="lat_window_attn",
    )(sink, pb, pb, cache_k, cache_v, cos_t, sin_t)


NA_Q = NA_TILE_ROWS * GRID_W
NA_K = NA_KEY_ROWS * GRID_W
NA_TILES = DEC_SEQ // NA_Q
LAT_ROWS = DEC_SEQ // GRID_W


def _na_window_start(tile):
    return jnp.clip(tile * NA_TILE_ROWS - NA_ROWS // 2, 0, LAT_ROWS - NA_KEY_ROWS)


def _lat_na_kernel(pq_ref, pseq_ref, kctx_ref, vctx_ref, bias_ref, yc_ref):
    t = pl.program_id(1)
    hd = HEAD_DIM
    k_rows = pl.ds(pl.multiple_of(_na_window_start(t) * GRID_W, GRID_W), NA_K)
    for hh in range(H_C):
        q = pq_ref[:, hh * hd:(hh + 1) * hd]
        kw = pseq_ref[k_rows, W_C + hh * hd:W_C + (hh + 1) * hd]
        vw = pseq_ref[k_rows, 2 * W_C + hh * hd:2 * W_C + (hh + 1) * hd]
        kc = kctx_ref[0, 0, hh].astype(BF16)
        vc = vctx_ref[0, 0, hh].astype(BF16)
        s_loc = _dot_nt(q, kw) * ATTN_SCALE + bias_ref[0, hh]
        s_ctx = _dot_nt(q, kc) * ATTN_SCALE
        m = jnp.maximum(jnp.max(s_loc, -1, keepdims=True), jnp.max(s_ctx, -1, keepdims=True))
        p_loc = jnp.exp(s_loc - m)
        p_ctx = jnp.exp(s_ctx - m)
        den = jnp.sum(p_loc, -1, keepdims=True) + jnp.sum(p_ctx, -1, keepdims=True)
        o = (_dot(p_loc.astype(BF16), vw) + _dot(p_ctx.astype(BF16), vc)) / den
        yc_ref[:, hh * hd:(hh + 1) * hd] = o.astype(BF16)


def _na_tile_type(t):
    return jnp.where(t == 0, 0, jnp.where(t == NA_TILES - 1, 2, 1))


def _lat_na_attn(pc, cache_k, cache_v, layer, maskbias):
    lat_tile0 = T_CTX // NA_Q
    lat_seq0 = T_CTX // DEC_SEQ
    ctx_spec = pl.BlockSpec((1, 1, H_C, PAST_LEN, HEAD_DIM), lambda b, t: (b, layer, 0, 0, 0))
    return pl.pallas_call(
        _lat_na_kernel,
        out_shape=jax.ShapeDtypeStruct((T_LAT, W_C), BF16),
        grid=(DEC_BATCH, NA_TILES),
        in_specs=[
            pl.BlockSpec((NA_Q, PC_W), lambda b, t: (lat_tile0 + b * NA_TILES + t, 0)),
            pl.BlockSpec((DEC_SEQ, PC_W), lambda b, t: (lat_seq0 + b, 0)),
            ctx_spec, ctx_spec,
            pl.BlockSpec((1, H_C, NA_Q, NA_K), lambda b, t: (_na_tile_type(t), 0, 0, 0)),
        ],
        out_specs=pl.BlockSpec((NA_Q, W_C), lambda b, t: (b * NA_TILES + t, 0)),
        compiler_params=pltpu.CompilerParams(
            dimension_semantics=("arbitrary", "arbitrary"), vmem_limit_bytes=VMEM_LIMIT),
        name="lat_na_attn",
    )(pc, pc, cache_k, cache_v, maskbias)


def _na_block_index():
    out = np.zeros((3, NA_TILE_ROWS, NA_KEY_ROWS), np.int32)
    for ty, tile in enumerate((0, 1, NA_TILES - 1)):
        r = tile * NA_TILE_ROWS
        ws = int(np.clip(r - NA_ROWS // 2, 0, LAT_ROWS - NA_KEY_ROWS))
        for qq in range(NA_TILE_ROWS):
            qr = r + qq
            r0 = int(np.clip(qr - NA_ROWS // 2, 0, LAT_ROWS - NA_ROWS))
            for kk in range(NA_KEY_ROWS):
                kr = ws + kk
                out[ty, qq, kk] = kr - qr + NA_ROWS - 1 if r0 <= kr < r0 + NA_ROWS else 2 * NA_ROWS - 1
    return out


def _na_maskbias(rpb):
    qc = np.arange(GRID_W)[:, None]
    kc = np.arange(GRID_W)[None, :]
    c0 = np.clip(qc - NA_COLS // 2, 0, GRID_W - NA_COLS)
    col_ok = (kc >= c0) & (kc < c0 + NA_COLS)
    ci = np.clip(kc - qc + NA_COLS - 1, 0, 2 * NA_COLS - 2)
    onehot = (ci[None] == np.arange(2 * NA_COLS - 1)[:, None, None]).astype(np.float32)
    cols = jnp.einsum("hab,bqk->haqk", rpb, jnp.asarray(onehot), precision=lax.Precision.HIGHEST)
    cols = jnp.where(jnp.asarray(col_ok)[None, None], cols, NEG_INF)
    cols = jnp.concatenate([cols, jnp.full((H_C, 1, GRID_W, GRID_W), NEG_INF, F32)], axis=1)
    block_index = _na_block_index()

    def assemble(cols_ref, out_ref):
        for ty in range(3):
            for qq in range(NA_TILE_ROWS):
                for kk in range(NA_KEY_ROWS):
                    out_ref[ty, 0, qq * GRID_W:(qq + 1) * GRID_W, kk * GRID_W:(kk + 1) * GRID_W] = (
                        cols_ref[0, int(block_index[ty, qq, kk])])

    return pl.pallas_call(
        assemble,
        out_shape=jax.ShapeDtypeStruct((3, H_C, NA_Q, NA_K), F32),
        grid=(H_C,),
        in_specs=[pl.BlockSpec((1, 2 * NA_ROWS, GRID_W, GRID_W), lambda h: (h, 0, 0, 0))],
        out_specs=pl.BlockSpec((3, 1, NA_Q, NA_K), lambda h: (0, h, 0, 0)),
        compiler_params=pltpu.CompilerParams(dimension_semantics=("arbitrary",)),
        name="na_bias_assemble",
    )(cols)


def _rope_tables():
    t = np.arange(DEC_SEQ)
    n_freq = HEAD_DIM // 4
    inv = (ROPE_BASE ** (-np.arange(n_freq, dtype=np.float32) / n_freq)).astype(np.float32)
    row = (t // GRID_W).astype(np.float32)[:, None] * inv
    col = (t % GRID_W).astype(np.float32)[:, None] * inv
    ang = np.concatenate([row, col], -1)
    cos, sin = np.cos(ang), np.sin(ang)
    cos_h = np.concatenate([cos, cos], -1)
    sin_h = np.concatenate([-sin, sin], -1)
    reps = LANES // HEAD_DIM
    return (jnp.asarray(np.tile(cos_h, (1, reps)), F32), jnp.asarray(np.tile(sin_h, (1, reps)), F32))


def _first_index_of(mask, iota, sentinel):
    return jnp.min(jnp.where(mask, iota, sentinel), axis=0, keepdims=True)


def _route(logits, b_col):
    n = logits.shape[1]
    scores = jax.nn.sigmoid(logits)
    sel = scores + b_col
    io_g = lax.broadcasted_iota(I32, (GROUP_SIZE, n), 0)
    gs_rows = []
    for g in range(N_GROUPS):
        s = sel[g * GROUP_SIZE:(g + 1) * GROUP_SIZE]
        m1 = jnp.max(s, axis=0, keepdims=True)
        i1 = _first_index_of(s == m1, io_g, GROUP_SIZE)
        m2 = jnp.max(jnp.where(io_g == i1, PICKED, s), axis=0, keepdims=True)
        gs_rows.append(m1 + m2)
    gs = jnp.concatenate(gs_rows, axis=0)
    io_n = lax.broadcasted_iota(I32, (N_GROUPS, n), 0)
    gsel = jnp.zeros((N_GROUPS, n), F32)
    for _ in range(TOPK_GROUPS):
        mg = jnp.max(gs, axis=0, keepdims=True)
        gi = _first_index_of(gs == mg, io_n, N_GROUPS)
        hit = io_n == gi
        gsel = jnp.where(hit, 1.0, gsel)
        gs = jnp.where(hit, PICKED, gs)
    cand = jnp.concatenate(
        [jnp.where(gsel[g:g + 1] > 0.5, sel[g * GROUP_SIZE:(g + 1) * GROUP_SIZE], NEG_INF)
         for g in range(N_GROUPS)], axis=0)
    io_e = lax.broadcasted_iota(I32, (N_EXPERTS, n), 0)
    picks, raw = [], []
    for _ in range(TOP_K):
        mv = jnp.max(cand, axis=0, keepdims=True)
        ei = _first_index_of(cand == mv, io_e, N_EXPERTS)
        hit = io_e == ei
        picks.append((hit, ei))
        raw.append(jnp.sum(jnp.where(hit, scores, 0.0), axis=0, keepdims=True))
        cand = jnp.where(hit, PICKED, cand)
    return picks, raw


def _post_mixer_kernel(xc_ref, xl_ref, mixc_ref, ya_ref, yb_ref, yc_ref, wout_ref, mod_ref, g_ref, b_ref,
                       wr_ref, br_ref,
                       x1_ref, h2_ref, eidx_ref, wsel_ref, rank_ref, cnt_ref):
    i = pl.program_id(0)
    tm = TM_TOK

    @pl.when(i == 0)
    def _():
        cnt_ref[...] = jnp.zeros_like(cnt_ref)

    ci = _cond_row(i, tm)
    gate1 = mod_ref[pl.ds(ci, 1), 2 * D_MODEL:3 * D_MODEL]
    sh2 = mod_ref[pl.ds(ci, 1), 3 * D_MODEL:4 * D_MODEL]
    sc2 = mod_ref[pl.ds(ci, 1), 4 * D_MODEL:5 * D_MODEL]
    mix_lat = jnp.concatenate([ya_ref[...], yb_ref[...], yc_ref[...]], axis=-1)
    mix = jnp.where(i < T_CTX // tm, mixc_ref[...], mix_lat)
    y = _dot(mix, wout_ref[...])
    x = jnp.where(i < T_CTX // tm, xc_ref[...], xl_ref[...])
    x1 = _layer_norm(ALPHA * x + gate1 * y, g_ref[...], b_ref[...])
    x1_ref[...] = x1
    h2 = x1 * (1.0 + sc2) + sh2
    h_hi = h2.astype(BF16)
    h2_ref[...] = h_hi
    h_lo = (h2 - h_hi.astype(F32)).astype(BF16)
    both = (_dot(h_hi, wr_ref[...]) + _dot(h_lo, wr_ref[...])).T
    logits = both[0:N_EXPERTS] + both[N_EXPERTS:2 * N_EXPERTS]
    routed = [_route(logits[:, g * LANES:(g + 1) * LANES], br_ref[...]) for g in range(tm // LANES)]
    multi_g = []
    for picks, _ in routed:
        m = jnp.zeros((N_EXPERTS, LANES), F32)
        for hit, _ in picks:
            m = m + jnp.where(hit, 1.0, 0.0)
        multi_g.append(m)
    multi = jnp.concatenate(multi_g, axis=1)
    before = (lax.broadcasted_iota(I32, (tm, tm), 0) < lax.broadcasted_iota(I32, (tm, tm), 1))
    cum = _dot(multi.astype(BF16), jnp.where(before, 1.0, 0.0).astype(BF16))
    pad = jnp.zeros((SUBLANES - TOP_K, LANES), F32)
    for g, (picks, raw) in enumerate(routed):
        lanes = slice(g * LANES, (g + 1) * LANES)
        total = raw[0]
        for r in raw[1:]:
            total = total + r
        scale = ROUTE_SCALE / total
        cum_g = cum[:, lanes]
        eidx_ref[:, lanes] = jnp.concatenate([ei for _, ei in picks] + [pad.astype(I32)], axis=0)
        wsel_ref[:, lanes] = jnp.concatenate([r * scale for r in raw] + [pad], axis=0)
        rank_ref[:, lanes] = jnp.concatenate(
            [jnp.sum(jnp.where(hit, cum_g, 0.0), axis=0, keepdims=True) for hit, _ in picks] + [pad],
            axis=0).astype(I32)
    tile_lane = lax.broadcasted_iota(I32, (N_EXPERTS, LANES), 1)
    cnt_ref[...] = jnp.where(tile_lane == i, jnp.sum(multi, axis=1, keepdims=True), cnt_ref[...])


def _post_mixer(x_ctx, x_lat, mix_c, ya, yb, yc, w_out_bf16, mod, ln_g, ln_b, wr_split, b_router_col):
    n_ctx = T_CTX // TM_TOK
    ctx_map = lambda i: (jnp.minimum(i, n_ctx - 1), 0)
    lat_map = lambda i: (jnp.maximum(i - n_ctx, 0), 0)
    row_map = lambda i: (i, 0)
    const = lambda i: (0, 0)
    tok_map = lambda i: (0, i)
    return pl.pallas_call(
        _post_mixer_kernel,
        out_shape=(
            jax.ShapeDtypeStruct((T_ALL, D_MODEL), F32),
            jax.ShapeDtypeStruct((T_ALL, D_MODEL), BF16),
            jax.ShapeDtypeStruct((SUBLANES, T_ALL), I32),
            jax.ShapeDtypeStruct((SUBLANES, T_ALL), F32),
            jax.ShapeDtypeStruct((SUBLANES, T_ALL), I32),
            jax.ShapeDtypeStruct((N_EXPERTS, LANES), F32),
        ),
        grid=(N_TOK_TILES,),
        in_specs=[
            pl.BlockSpec((TM_TOK, D_MODEL), ctx_map),
            pl.BlockSpec((TM_TOK, D_MODEL), lat_map),
            pl.BlockSpec((TM_TOK, D_MODEL), ctx_map),
            pl.BlockSpec((TM_TOK, W_A), lat_map),
            pl.BlockSpec((TM_TOK, W_B), lat_map),
            pl.BlockSpec((TM_TOK, W_C), lat_map),
            pl.BlockSpec((D_MODEL, D_MODEL), const),
            pl.BlockSpec((N_COND, 6 * D_MODEL), const),
            pl.BlockSpec((1, D_MODEL), const),
            pl.BlockSpec((1, D_MODEL), const),
            pl.BlockSpec((D_MODEL, 2 * N_EXPERTS), const),
            pl.BlockSpec((N_EXPERTS, 1), const),
        ],
        out_specs=(
            pl.BlockSpec((TM_TOK, D_MODEL), row_map),
            pl.BlockSpec((TM_TOK, D_MODEL), row_map),
            pl.BlockSpec((SUBLANES, TM_TOK), tok_map),
            pl.BlockSpec((SUBLANES, TM_TOK), tok_map),
            pl.BlockSpec((SUBLANES, TM_TOK), tok_map),
            pl.BlockSpec((N_EXPERTS, LANES), const),
        ),
        compiler_params=pltpu.CompilerParams(
            dimension_semantics=("arbitrary",), vmem_limit_bytes=VMEM_LIMIT),
        name="post_mixer",
    )(x_ctx, x_lat, mix_c, ya, yb, yc, w_out_bf16, mod, ln_g, ln_b, wr_split, b_router_col)


PLAN_TILES = 4


def _plan_kernel(eidx_ref, rank_ref, nmat_ref, lslot_ref, unit_ref, gend_ref):
    step = pl.program_id(0)
    units = jnp.floor((nmat_ref[...] + (UNIT - 1.0)) * (1.0 / UNIT))
    units_bf = units.astype(BF16)
    earlier_e = (lax.broadcasted_iota(I32, (N_EXPERTS, N_EXPERTS), 1)
                 < lax.broadcasted_iota(I32, (N_EXPERTS, N_EXPERTS), 0))
    tri_e = jnp.where(earlier_e, 1.0, 0.0).astype(BF16)
    earlier_t = (lax.broadcasted_iota(I32, (LANES, LANES), 0) < lax.broadcasted_iota(I32, (LANES, LANES), 1))
    tri_t = jnp.where(earlier_t, 1.0, 0.0).astype(BF16)
    local_off = _dot(tri_e, units_bf)
    tile_off = _dot(units_bf, tri_t)
    per_expert = jnp.sum(units, axis=1, keepdims=True)
    blocks = jnp.floor((per_expert + (UNITS_PER_BLOCK - 1.0)) * (1.0 / UNITS_PER_BLOCK))
    blocks_l = jnp.broadcast_to(blocks, (N_EXPERTS, LANES))
    start_blk = _dot(tri_e, blocks_l.astype(BF16))
    end_blk = start_blk + blocks_l
    gend_ref[...] = (end_blk * BM).astype(I32)

    tile_lane = lax.broadcasted_iota(I32, (N_EXPERTS, LANES), 1)
    u = lax.broadcasted_iota(I32, (N_EXPERTS, MAX_UNITS), 1).astype(F32)
    io_e = lax.broadcasted_iota(I32, (N_EXPERTS, TM_TOK), 0)
    for s in range(PLAN_TILES):
        i = step * PLAN_TILES + s
        this_tile = tile_lane == i

        def column(a):
            return jnp.sum(jnp.where(this_tile, a, 0.0), axis=1, keepdims=True)

        lo, n_u = column(local_off), column(units)
        base_unit = start_blk[:, 0:1] * UNITS_PER_BLOCK + column(tile_off) - lo
        inside = jnp.where(u >= lo, jnp.where(u < lo + n_u, 1.0, 0.0), 0.0)
        dst_unit = jnp.sum(inside * (base_unit + u), axis=0, keepdims=True)
        used = jnp.sum(inside, axis=0, keepdims=True) > 0.5
        spare = (SPARE_UNIT0 + (i % 2) * MAX_UNITS).astype(F32) + u[0:1, :]
        unit_ref[s] = jnp.where(used, dst_unit, spare).astype(I32)

        toks = slice(s * TM_TOK, (s + 1) * TM_TOK)
        rows = []
        for k in range(TOP_K):
            hit = io_e == eidx_ref[k:k + 1, toks]
            seg = jnp.sum(jnp.where(hit, lo * UNIT, 0.0), axis=0, keepdims=True)
            rows.append(seg.astype(I32) + rank_ref[k:k + 1, toks])
        rows.append(jnp.full((SUBLANES - TOP_K, TM_TOK), -1, I32))
        lslot_ref[:, toks] = jnp.concatenate(rows, axis=0)


def _slot_plan(eidx, rank, nmat):
    tok_map = lambda i: (0, i)
    const = lambda i: (0, 0)
    lslot, unit_tab, gend = pl.pallas_call(
        _plan_kernel,
        out_shape=(
            jax.ShapeDtypeStruct((SUBLANES, T_ALL), I32),
            jax.ShapeDtypeStruct((N_TOK_TILES, 1, MAX_UNITS), I32),
            jax.ShapeDtypeStruct((N_EXPERTS, LANES), I32),
        ),
        grid=(N_TOK_TILES // PLAN_TILES,),
        in_specs=[
            pl.BlockSpec((SUBLANES, PLAN_TILES * TM_TOK), tok_map),
            pl.BlockSpec((SUBLANES, PLAN_TILES * TM_TOK), tok_map),
            pl.BlockSpec((N_EXPERTS, LANES), const),
        ],
        out_specs=(
            pl.BlockSpec((SUBLANES, PLAN_TILES * TM_TOK), tok_map),
            pl.BlockSpec((PLAN_TILES, 1, MAX_UNITS), lambda i: (i, 0, 0)),
            pl.BlockSpec((N_EXPERTS, LANES), const),
        ),
        compiler_params=pltpu.CompilerParams(dimension_semantics=("arbitrary",)),
        name="slot_plan",
    )(eidx, rank, nmat)
    return lslot, unit_tab.reshape(N_TOK_TILES * MAX_UNITS), gend[:, 0]


PACK_W = D_MODEL // 2
HI_HALF = -65536


def _pack_pairs(x):
    lo = lax.bitcast_convert_type(x[:, 0:PACK_W], I32)
    hi = lax.bitcast_convert_type(x[:, PACK_W:D_MODEL], I32)
    return lax.shift_right_logical(lo, 16) | (hi & HI_HALF)


def _unpack_pairs(u):
    lo = lax.bitcast_convert_type(lax.shift_left(u, 16), F32).astype(BF16)
    hi = lax.bitcast_convert_type(u & HI_HALF, F32).astype(BF16)
    return lo, hi


def _unit_rows(unit):
    row = unit * UNIT
    return pl.ds(row if isinstance(unit, int) else pl.multiple_of(row, UNIT), UNIT)


def _unit_copy(src, src_unit, dst, dst_unit, sem):
    return pltpu.make_async_copy(src.at[_unit_rows(src_unit)], dst.at[_unit_rows(dst_unit)], sem)


def _dispatch_kernel(gend_ref, tab_ref, h2_ref, lslot_ref, xs_hbm, zero_ref, local_ref, sem_zero, sem_rows):
    i = pl.program_id(0)
    buf = i % 2

    def drain(b):
        pltpu.make_async_copy(local_ref.at[b], xs_hbm.at[pl.ds(0, LOCAL_ROWS)], sem_rows.at[b]).wait()

    def has_rows(e):
        return gend_ref[e] > jnp.where(e == 0, 0, gend_ref[jnp.maximum(e - 1, 0)])

    def zero_copy(e):
        return pltpu.make_async_copy(
            zero_ref, xs_hbm.at[pl.ds(pl.multiple_of(gend_ref[e] - BM, BM), BM)], sem_zero)

    @pl.when(i == 0)
    def _():
        zero_ref[...] = jnp.zeros_like(zero_ref)

        def start(e, c):
            @pl.when(has_rows(e))
            def _():
                zero_copy(e).start()
            return c

        def wait(e, c):
            @pl.when(has_rows(e))
            def _():
                zero_copy(e).wait()
            return c

        def tail_copy(blk):
            return pltpu.make_async_copy(
                zero_ref, xs_hbm.at[pl.ds(pl.multiple_of(blk * BM, BM), BM)], sem_zero)

        def start_tail(blk, c):
            tail_copy(blk).start()
            return c

        def wait_tail(blk, c):
            tail_copy(blk).wait()
            return c

        n_used = gend_ref[N_EXPERTS - 1] // BM
        lax.fori_loop(0, N_EXPERTS, start, 0)
        lax.fori_loop(n_used, N_BLOCKS_ALL, start_tail, 0)
        lax.fori_loop(0, N_EXPERTS, wait, 0)
        lax.fori_loop(n_used, N_BLOCKS_ALL, wait_tail, 0)

    @pl.when(i >= 2)
    def _():
        drain(buf)

    h2 = h2_ref[...]
    units_per_chunk = PERM_CHUNK // UNIT
    local = local_ref.at[buf]
    for c in range(LOCAL_ROWS // PERM_CHUNK):
        slot = c * PERM_CHUNK + lax.broadcasted_iota(I32, (PERM_CHUNK, TM_TOK), 0)
        p = jnp.zeros((PERM_CHUNK, TM_TOK), F32)
        for k in range(TOP_K):
            p = jnp.where(slot == lslot_ref[k:k + 1, :], 1.0, p)
        local[c * PERM_CHUNK:(c + 1) * PERM_CHUNK, :] = _pack_pairs(_dot(p.astype(BF16), h2))
        for u in range(c * units_per_chunk, (c + 1) * units_per_chunk):
            _unit_copy(local, u, xs_hbm, tab_ref[i * MAX_UNITS + u], sem_rows.at[buf]).start()

    @pl.when(i == N_TOK_TILES - 1)
    def _():
        drain(1 - buf)
        drain(buf)


def _dispatch(h2, lslot, unit_tab, gend):
    return pl.pallas_call(
        _dispatch_kernel,
        out_shape=jax.ShapeDtypeStruct((N_SLOTS, PACK_W), I32),
        grid_spec=pltpu.PrefetchScalarGridSpec(
            num_scalar_prefetch=2,
            grid=(N_TOK_TILES,),
            in_specs=[
                pl.BlockSpec((TM_TOK, D_MODEL), lambda i, ge, tab: (i, 0)),
                pl.BlockSpec((SUBLANES, TM_TOK), lambda i, ge, tab: (0, i)),
            ],
            out_specs=pl.BlockSpec(memory_space=pl.ANY),
            scratch_shapes=[
                pltpu.VMEM((BM, PACK_W), I32),
                pltpu.VMEM((2, LOCAL_ROWS, PACK_W), I32),
                pltpu.SemaphoreType.DMA,
                pltpu.SemaphoreType.DMA((2,)),
            ],
        ),
        compiler_params=pltpu.CompilerParams(
            dimension_semantics=("arbitrary",), vmem_limit_bytes=VMEM_LIMIT),
        name="moe_dispatch",
    )(gend, unit_tab, h2, lslot)


def _expert_kernel(gend_ref, xs_hbm, wgu_ref, wdown_ref, ys_hbm, xbuf, ybuf, wgu_bf, wdown_bf, sem_in, sem_out):
    e = pl.program_id(0)
    first = jnp.where(e == 0, 0, gend_ref[jnp.maximum(e - 1, 0)]) // BM
    n_blk = gend_ref[e] // BM - first

    def rows_of(blk):
        return pl.ds(pl.multiple_of(blk * BM, BM), BM)

    def in_copy(j, b):
        return pltpu.make_async_copy(xs_hbm.at[rows_of(first + j)], xbuf.at[b], sem_in.at[b])

    def out_copy(j, b):
        return pltpu.make_async_copy(ybuf.at[b], ys_hbm.at[rows_of(first + j)], sem_out.at[b])

    @pl.when(jnp.logical_and(e == 0, n_blk > 0))
    def _():
        in_copy(0, 0).start()

    for a in range(1, X_AHEAD):
        @pl.when(n_blk > a)
        def _(a=a):
            in_copy(a, a).start()

    @pl.when(n_blk > 0)
    def _():
        wgu_bf[...] = wgu_ref[0, 0].astype(BF16)
        wdown_bf[...] = wdown_ref[0, 0].astype(BF16)

    def block(j, carry):
        b = j % 2

        @pl.when(j + X_AHEAD < n_blk)
        def _():
            in_copy(j + X_AHEAD, (j + X_AHEAD) % X_BUFS).start()

        in_copy(j, j % X_BUFS).wait()

        @pl.when(j >= 2)
        def _():
            out_copy(j - 2, b).wait()

        x_lo, x_hi = _unpack_pairs(xbuf[j % X_BUFS])
        gu = _dot(x_lo, wgu_bf[0:PACK_W, :]) + _dot(x_hi, wgu_bf[PACK_W:D_MODEL, :])
        act = _silu(gu[:, 0:D_EXPERT]) * gu[:, D_EXPERT:2 * D_EXPERT]
        y = _dot(act.astype(BF16), wdown_bf[...])
        ybuf[b] = _pack_pairs(y.astype(BF16).astype(F32))
        out_copy(j, b).start()
        return carry

    lax.fori_loop(0, n_blk, block, 0)

    nxt = jnp.minimum(e + 1, N_EXPERTS - 1)
    next_blocks = gend_ref[nxt] // BM - gend_ref[e] // BM

    @pl.when(jnp.logical_and(e + 1 < N_EXPERTS, next_blocks > 0))
    def _():
        pltpu.make_async_copy(xs_hbm.at[rows_of(gend_ref[e] // BM)], xbuf.at[0], sem_in.at[0]).start()

    @pl.when(n_blk >= 2)
    def _():
        out_copy(n_blk - 2, n_blk % 2).wait()

    @pl.when(n_blk >= 1)
    def _():
        out_copy(n_blk - 1, (n_blk - 1) % 2).wait()

    @pl.when(e == N_EXPERTS - 1)
    def _():
        ybuf[0] = jnp.zeros((BM, PACK_W), I32)
        n_used = gend_ref[N_EXPERTS - 1] // BM

        def tail_copy(blk):
            return pltpu.make_async_copy(
                ybuf.at[0], ys_hbm.at[pl.ds(pl.multiple_of(blk * BM, BM), BM)], sem_out.at[0])

        def start(blk, c):
            tail_copy(blk).start()
            return c

        def wait(blk, c):
            tail_copy(blk).wait()
            return c

        lax.fori_loop(n_used, N_BLOCKS_ALL, start, 0)
        lax.fori_loop(n_used, N_BLOCKS_ALL, wait, 0)


def _experts(xs, w_gu, w_down, layer, gend):
    return pl.pallas_call(
        _expert_kernel,
        out_shape=jax.ShapeDtypeStruct((N_SLOTS, PACK_W), I32),
        grid_spec=pltpu.PrefetchScalarGridSpec(
            num_scalar_prefetch=1,
            grid=(N_EXPERTS,),
            in_specs=[
                pl.BlockSpec(memory_space=pl.ANY),
                pl.BlockSpec((1, 1, D_MODEL, 2 * D_EXPERT), lambda e, ge: (layer, e, 0, 0)),
                pl.BlockSpec((1, 1, D_EXPERT, D_MODEL), lambda e, ge: (layer, e, 0, 0)),
            ],
            out_specs=pl.BlockSpec(memory_space=pl.ANY),
            scratch_shapes=[
                pltpu.VMEM((X_BUFS, BM, PACK_W), I32),
                pltpu.VMEM((2, BM, PACK_W), I32),
                pltpu.VMEM((D_MODEL, 2 * D_EXPERT), BF16),
                pltpu.VMEM((D_EXPERT, D_MODEL), BF16),
                pltpu.SemaphoreType.DMA((X_BUFS,)),
                pltpu.SemaphoreType.DMA((2,)),
            ],
        ),
        compiler_params=pltpu.CompilerParams(
            dimension_semantics=("arbitrary",), vmem_limit_bytes=VMEM_LIMIT),
        name="moe_experts",
    )(gend, xs, w_gu, w_down)


def _combine_kernel(tab_ref, x1_ref, h2_ref, lslot_ref, wsel_ref, ys_hbm, wsgu_ref, wsdown_ref, mod_ref,
                    g_ref, b_ref, outc_ref, outl_ref, local_ref, sel_ref, ylo_ref, yhi_ref, sem_rows):
    i = pl.program_id(0)
    tm = TM_TOK
    buf = i % 2

    def fetch_unit(tile, b, u):
        _unit_copy(ys_hbm, tab_ref[tile * MAX_UNITS + u], local_ref.at[b], u, sem_rows.at[b]).start()

    def drain(b):
        pltpu.make_async_copy(ys_hbm.at[pl.ds(0, LOCAL_ROWS)], local_ref.at[b], sem_rows.at[b]).wait()

    @pl.when(i == 0)
    def _():
        def body(u, c):
            fetch_unit(0, 0, u)
            return c

        lax.fori_loop(0, MAX_UNITS, body, 0, unroll=8)

    sgu = _dot(h2_ref[...], wsgu_ref[...])
    act = _silu(sgu[:, 0:D_SHARED]) * sgu[:, D_SHARED:2 * D_SHARED]
    f = _dot(act.astype(BF16), wsdown_ref[...])

    nxt = jnp.minimum(i + 1, N_TOK_TILES - 1)
    n_groups = tm // SEL_ROWS
    units_per_group = MAX_UNITS // n_groups
    slot = lax.broadcasted_iota(I32, (SEL_ROWS, LOCAL_ROWS), 1)
    for g in range(n_groups):
        rows = slice(g * SEL_ROWS, (g + 1) * SEL_ROWS)
        sel = jnp.zeros((SEL_ROWS, LOCAL_ROWS), F32)
        for k in range(TOP_K):
            sel = jnp.where(slot == lslot_ref[rows, k:k + 1], wsel_ref[rows, k:k + 1], sel)
        sel_ref[rows, :] = sel.astype(BF16)
        for u in range(g * units_per_group, (g + 1) * units_per_group):
            fetch_unit(nxt, 1 - buf, u)

    local = local_ref.at[buf]
    drain(buf)
    for c in range(LOCAL_ROWS // PERM_CHUNK):
        rows = slice(c * PERM_CHUNK, (c + 1) * PERM_CHUNK)
        ylo_ref[rows, :], yhi_ref[rows, :] = _unpack_pairs(local[rows, :])
    sel = sel_ref[...]
    f = f + jnp.concatenate([_dot(sel, ylo_ref[...]), _dot(sel, yhi_ref[...])], axis=-1)
    ci = _cond_row(i, tm)
    gate2 = mod_ref[pl.ds(ci, 1), 5 * D_MODEL:6 * D_MODEL]
    out = _layer_norm(ALPHA * x1_ref[...] + gate2 * f, g_ref[...], b_ref[...])

    @pl.when(i < T_CTX // tm)
    def _():
        outc_ref[...] = out

    @pl.when(i >= T_CTX // tm)
    def _():
        outl_ref[...] = out

    @pl.when(i == N_TOK_TILES - 1)
    def _():
        drain(1 - buf)


def _combine(x1, h2, lslot_rows, wsel_rows, unit_tab, ys, w_sgu_bf16, w_sdown_bf16, mod, ln_g, ln_b):
    n_ctx = T_CTX // TM_TOK
    row_map = lambda i, tab: (i, 0)
    const = lambda i, tab: (0, 0)
    return pl.pallas_call(
        _combine_kernel,
        out_shape=(jax.ShapeDtypeStruct((T_CTX, D_MODEL), F32),
                   jax.ShapeDtypeStruct((T_LAT, D_MODEL), F32)),
        grid_spec=pltpu.PrefetchScalarGridSpec(
            num_scalar_prefetch=1,
            grid=(N_TOK_TILES,),
            in_specs=[
                pl.BlockSpec((TM_TOK, D_MODEL), row_map),
                pl.BlockSpec((TM_TOK, D_MODEL), row_map),
                pl.BlockSpec((TM_TOK, SUBLANES), row_map),
                pl.BlockSpec((TM_TOK, SUBLANES), row_map),
                pl.BlockSpec(memory_space=pl.ANY),
                pl.BlockSpec((D_MODEL, 2 * D_SHARED), const),
                pl.BlockSpec((D_SHARED, D_MODEL), const),
                pl.BlockSpec((N_COND, 6 * D_MODEL), const),
                pl.BlockSpec((1, D_MODEL), const),
                pl.BlockSpec((1, D_MODEL), const),
            ],
            out_specs=(pl.BlockSpec((TM_TOK, D_MODEL), lambda i, tab: (jnp.minimum(i, n_ctx - 1), 0)),
                       pl.BlockSpec((TM_TOK, D_MODEL), lambda i, tab: (jnp.maximum(i - n_ctx, 0), 0))),
            scratch_shapes=[
                pltpu.VMEM((2, LOCAL_ROWS, PACK_W), I32),
                pltpu.VMEM((TM_TOK, LOCAL_ROWS), BF16),
                pltpu.VMEM((LOCAL_ROWS, PACK_W), BF16),
                pltpu.VMEM((LOCAL_ROWS, PACK_W), BF16),
                pltpu.SemaphoreType.DMA((2,)),
            ],
        ),
        compiler_params=pltpu.CompilerParams(
            dimension_semantics=("arbitrary",), vmem_limit_bytes=VMEM_LIMIT),
        name="moe_combine",
    )(unit_tab, x1, h2, lslot_rows, wsel_rows, ys, w_sgu_bf16, w_sdown_bf16, mod, ln_g, ln_b)


def _lane_rows(v, width):
    return jnp.broadcast_to(v.astype(F32)[:, None, None], (v.shape[0], 1, width))


def kernel(x_prompt, x_sample, state_ret_fwd, state_ret_bwd, cache_win_k, cache_win_v, cache_na_k, cache_na_v, c, c_ctx, w_in, w_out, ret_decay_fwd, ret_decay_bwd, ret_gn_g, ret_gn_b, win_sink, na_rpb, w_mod, b_mod, ln1_g, ln1_b, ln2_g, ln2_b, w_router, b_router, w_expert_gu, w_expert_down, w_shared_gu, w_shared_down):
    cond = jnp.concatenate(
        [c_ctx[None, :], c, jnp.zeros((N_COND - 1 - DEC_BATCH, D_MODEL), F32)], axis=0)
    mod_all = _modulation(cond, w_mod, b_mod)
    cos_t, sin_t = _rope_tables()

    x_ctx = x_prompt.reshape(T_CTX, D_MODEL)
    x_lat = x_sample.reshape(T_LAT, D_MODEL)
    sf_l, sb_l, caches = [], [], ()
    for l in range(DEPTH):
        mod = mod_all[l]
        pa, pb, pc, *caches = _in_projection(x_ctx, x_lat, mod, w_in[l].astype(BF16), tuple(caches))
        decf_s, decb_s = _lane_rows(ret_decay_fwd[l], SEQ), _lane_rows(ret_decay_bwd[l], SEQ)
        gng, gnb = ret_gn_g[l][None, :], ret_gn_b[l][None, :]
        mix_c, st_f, st_b = _ctx_mixers(pa, pb, pc, win_sink[l], decf_s, decb_s, gng, gnb)
        sf_l.append(st_f)
        sb_l.append(st_b)
        ya = _lat_retention(pa, state_ret_fwd, state_ret_bwd, l,
                            _lane_rows(ret_decay_fwd[l], RET_CHUNK), _lane_rows(ret_decay_bwd[l], RET_CHUNK),
                            gng, gnb)
        yb = _lat_window_attn(pb, cache_win_k, cache_win_v, l, win_sink[l], cos_t, sin_t)
        yc = _lat_na_attn(pc, cache_na_k, cache_na_v, l, _na_maskbias(na_rpb[l]))

        wr_hi = w_router[l].astype(BF16)
        wr_lo = (w_router[l] - wr_hi.astype(F32)).astype(BF16)
        x1, h2, eidx, wsel, rank, counts = _post_mixer(
            x_ctx, x_lat, mix_c, ya, yb, yc, w_out[l].astype(BF16), mod, ln1_g[l][None, :],
            ln1_b[l][None, :], jnp.concatenate([wr_hi, wr_lo], axis=1), b_router[l][:, None])
        lslot, unit_tab, gend = _slot_plan(eidx, rank, counts)
        xs = _dispatch(h2, lslot, unit_tab, gend)
        ys = _experts(xs, w_expert_gu, w_expert_down, l, gend)
        x_ctx, x_lat = _combine(x1, h2, lslot.T, wsel.T, unit_tab, ys, w_shared_gu[l].astype(BF16),
                                w_shared_down[l].astype(BF16), mod, ln2_g[l][None, :], ln2_b[l][None, :])

    y_prompt = x_ctx.reshape(BATCH, SEQ, D_MODEL)
    y_sample = x_lat.reshape(DEC_BATCH, DEC_SEQ, D_MODEL)
    new_sf = jnp.stack(sf_l, axis=1)
    new_sb = jnp.stack(sb_l, axis=1)

    return (y_prompt, y_sample, new_sf, new_sb, *caches)
```

```python
import functools

import numpy as np
import jax
import jax.numpy as jnp
from jax import lax
from jax.experimental import pallas as pl
from jax.experimental.pallas import tpu as pltpu

F32 = jnp.float32
BF16 = jnp.bfloat16
I32 = jnp.int32

D_MODEL = 1024
BATCH = 32
SEQ = 256
DEPTH = 2
DEC_BATCH = 4
DEC_SEQ = 2048
PAST_LEN = 256
GRID_W = 64
HEAD_DIM = 64
ATTN_SCALE = HEAD_DIM ** -0.5
H_A = 4
W_A = H_A * HEAD_DIM
GN_EPS = 1e-5
H_B = 6
KV_B = 2
W_B = H_B * HEAD_DIM
KV_W_B = KV_B * HEAD_DIM
WINDOW = 128
ROPE_BASE = 10000.0
H_C = 6
W_C = H_C * HEAD_DIM
NA_ROWS = 8
NA_COLS = 16
IN_WIDTH = 4 * W_A + W_B + 2 * KV_W_B + 3 * W_C
N_EXPERTS = 64
TOP_K = 6
N_GROUPS = 8
GROUP_SIZE = N_EXPERTS // N_GROUPS
TOPK_GROUPS = 4
D_EXPERT = 256
D_SHARED = 256
ROUTE_SCALE = 2.5
ALPHA = (2 * DEPTH) ** 0.25
LN_EPS = 1e-5
NEG_INF = -1e30
PICKED = -3e38

T_CTX = BATCH * SEQ
T_LAT = DEC_BATCH * DEC_SEQ
T_ALL = T_CTX + T_LAT
N_COND = 8

PA_W = 4 * W_A
PB_W = W_B + 2 * KV_W_B
PC_W = 3 * W_C

LANES = 128
SUBLANES = 8
VMEM_LIMIT = 56 * 1024 * 1024

TM_PROJ = 512
TM_TOK = 256
RET_CHUNK = 256
WIN_QB = 128
NA_TILE_ROWS = 4
NA_KEY_ROWS = 11
BM = 512
X_AHEAD = 3
X_BUFS = X_AHEAD + 1
UNIT = SUBLANES
UNITS_PER_BLOCK = BM // UNIT
PERM_CHUNK = 256
SEL_ROWS = 16
N_TOK_TILES = T_ALL // TM_TOK
LOCAL_ROWS = -(-(TM_TOK * TOP_K + N_EXPERTS * (UNIT - 1)) // PERM_CHUNK) * PERM_CHUNK
MAX_UNITS = LOCAL_ROWS // UNIT
N_ASSIGN = T_ALL * TOP_K
N_BLOCKS = -(-(N_ASSIGN + N_TOK_TILES * N_EXPERTS * (UNIT - 1) + N_EXPERTS * (BM - 1)) // BM)
SPARE_BLOCKS = 2 * -(-LOCAL_ROWS // BM)
SPARE_UNIT0 = N_BLOCKS * UNITS_PER_BLOCK
N_BLOCKS_ALL = N_BLOCKS + SPARE_BLOCKS
N_SLOTS = N_BLOCKS_ALL * BM


def _dot(a, b):
    return jnp.dot(a, b, preferred_element_type=F32)


def _dot_nt(a, b):
    return lax.dot_general(a, b, (((1,), (1,)), ((), ())), preferred_element_type=F32)


def _silu(x):
    return x * jax.nn.sigmoid(x)


def _log_sigmoid(x):
    return jnp.minimum(x, 0.0) - jnp.log(1.0 + jnp.exp(-jnp.abs(x)))


def _cond_row(tile, tile_rows):
    n_ctx = T_CTX // tile_rows
    per_lat = DEC_SEQ // tile_rows
    return jnp.where(tile < n_ctx, 0, 1 + (tile - n_ctx) // per_lat)


def _layer_norm(x, g, b):
    mu = jnp.mean(x, -1, keepdims=True)
    xc = x - mu
    var = jnp.mean(xc * xc, -1, keepdims=True)
    return xc * lax.rsqrt(var + LN_EPS) * g + b


MOD_TN = 1536


def _mod_kernel(cond_ref, w_ref, b_ref, o_ref):
    s = _silu(cond_ref[...])
    s_hi = s.astype(BF16)
    s_lo = (s - s_hi.astype(F32)).astype(BF16)
    w = w_ref[0]
    w_hi = w.astype(BF16)
    w_lo = (w - w_hi.astype(F32)).astype(BF16)
    o_ref[0] = _dot(s_hi, w_hi) + _dot(s_lo, w_hi) + _dot(s_hi, w_lo) + b_ref[0]


def _modulation(cond, w_mod, b_mod):
    n_out = 6 * D_MODEL
    return pl.pallas_call(
        _mod_kernel,
        out_shape=jax.ShapeDtypeStruct((DEPTH, N_COND, n_out), F32),
        grid=(DEPTH, n_out // MOD_TN),
        in_specs=[
            pl.BlockSpec((N_COND, D_MODEL), lambda l, j: (0, 0)),
            pl.BlockSpec((1, D_MODEL, MOD_TN), lambda l, j: (l, 0, j)),
            pl.BlockSpec((1, 1, MOD_TN), lambda l, j: (l, 0, j)),
        ],
        out_specs=pl.BlockSpec((1, N_COND, MOD_TN), lambda l, j: (l, 0, j)),
        compiler_params=pltpu.CompilerParams(
            dimension_semantics=("arbitrary", "arbitrary"), vmem_limit_bytes=VMEM_LIMIT),
        name="modulation",
    )(cond, w_mod, b_mod.reshape(DEPTH, 1, n_out))


SEQ_PER_PROJ = TM_PROJ // SEQ


def _inproj_kernel(n_prev, xc_ref, xl_ref, mod_ref, w_ref, *refs):
    prev_refs = refs[:4] if n_prev else ()
    pa_ref, pb_ref, pc_ref, wk_ref, wv_ref, nk_ref, nv_ref = refs[len(prev_refs):]
    i = pl.program_id(0)
    ci = _cond_row(i, TM_PROJ)
    sh = mod_ref[pl.ds(ci, 1), 0:D_MODEL]
    sc = mod_ref[pl.ds(ci, 1), D_MODEL:2 * D_MODEL]
    x = jnp.where(i < T_CTX // TM_PROJ, xc_ref[...], xl_ref[...])
    h = x * (1.0 + sc) + sh
    p = _dot(h.astype(BF16), w_ref[...])
    pa_ref[...] = p[:, 0:PA_W].astype(BF16)
    pb_ref[...] = p[:, PA_W:PA_W + PB_W].astype(BF16)
    pc_ref[...] = p[:, PA_W + PB_W:IN_WIDTH].astype(BF16)

    @pl.when(i < T_CTX // TM_PROJ)
    def _():
        targets = ((wk_ref, PA_W + W_B, KV_B), (wv_ref, PA_W + W_B + KV_W_B, KV_B),
                   (nk_ref, PA_W + PB_W + W_C, H_C), (nv_ref, PA_W + PB_W + 2 * W_C, H_C))
        for j, (ref, col0, n_heads) in enumerate(targets):
            if n_prev:
                ref[:, 0:n_prev] = prev_refs[j][...]
            for s in range(SEQ_PER_PROJ):
                for hh in range(n_heads):
                    ref[s, n_prev, hh] = p[s * SEQ:(s + 1) * SEQ,
                                           col0 + hh * HEAD_DIM:col0 + (hh + 1) * HEAD_DIM]


def _in_projection(x_ctx, x_lat, mod, w_in_bf16, earlier):
    n_ctx_tiles = T_CTX // TM_PROJ
    n_prev = earlier[0].shape[1] if earlier else 0

    def cache_spec(n_layers, n_heads):
        return pl.BlockSpec((SEQ_PER_PROJ, n_layers, n_heads, SEQ, HEAD_DIM),
                            lambda i: (jnp.minimum(i, n_ctx_tiles - 1), 0, 0, 0, 0))

    cache_heads = (KV_B, KV_B, H_C, H_C)
    return pl.pallas_call(
        functools.partial(_inproj_kernel, n_prev),
        out_shape=(
            jax.ShapeDtypeStruct((T_ALL, PA_W), BF16),
            jax.ShapeDtypeStruct((T_ALL, PB_W), BF16),
            jax.ShapeDtypeStruct((T_ALL, PC_W), BF16),
        ) + tuple(jax.ShapeDtypeStruct((BATCH, n_prev + 1, nh, SEQ, HEAD_DIM), F32) for nh in cache_heads),
        grid=(T_ALL // TM_PROJ,),
        in_specs=[
            pl.BlockSpec((TM_PROJ, D_MODEL), lambda i: (jnp.minimum(i, n_ctx_tiles - 1), 0)),
            pl.BlockSpec((TM_PROJ, D_MODEL), lambda i: (jnp.maximum(i - n_ctx_tiles, 0), 0)),
            pl.BlockSpec((N_COND, 6 * D_MODEL), lambda i: (0, 0)),
            pl.BlockSpec((D_MODEL, IN_WIDTH), lambda i: (0, 0)),
        ] + [cache_spec(n_prev, nh) for nh in cache_heads if n_prev],
        out_specs=(
            pl.BlockSpec((TM_PROJ, PA_W), lambda i: (i, 0)),
            pl.BlockSpec((TM_PROJ, PB_W), lambda i: (i, 0)),
            pl.BlockSpec((TM_PROJ, PC_W), lambda i: (i, 0)),
        ) + tuple(cache_spec(n_prev + 1, nh) for nh in cache_heads),
        compiler_params=pltpu.CompilerParams(
            dimension_semantics=("arbitrary",), vmem_limit_bytes=VMEM_LIMIT),
        name="in_projection",
    )(x_ctx, x_lat, mod, w_in_bf16, *earlier)


def _decay_matrix(lg_f, lg_b, n):
    row = lax.broadcasted_iota(I32, (n, n), 0)
    col = lax.broadcasted_iota(I32, (n, n), 1)
    diff = (row - col).astype(F32)
    fwd = jnp.where(diff >= 0, jnp.exp(lg_f * jnp.maximum(diff, 0.0)), 0.0)
    bwd = jnp.where(diff <= 0, jnp.exp(lg_b * jnp.maximum(-diff, 0.0)), 0.0)
    return (fwd + bwd) * ATTN_SCALE


def _retention_readout(o, gate, g, b):
    mu = jnp.mean(o, -1, keepdims=True)
    oc = o - mu
    var = jnp.mean(oc * oc, -1, keepdims=True)
    on = oc * lax.rsqrt(var + GN_EPS) * g + b
    return on * _silu(gate.astype(F32))


def _stacked_softmax_attend(scores, values, extra_logit=None):
    rows = scores[0].shape[0]
    s = jnp.concatenate(scores, axis=0)
    m = jnp.max(s, -1, keepdims=True)
    if extra_logit is not None:
        m = jnp.maximum(m, extra_logit)
    p = jnp.exp(s - m)
    den = jnp.sum(p, -1, keepdims=True)
    if extra_logit is not None:
        den = den + jnp.exp(extra_logit - m)
    p = p.astype(BF16)
    return [_dot(p[h * rows:(h + 1) * rows], v) / den[h * rows:(h + 1) * rows]
            for h, v in enumerate(values)]


def _lane_xor_matrix(width, distance):
    r = lax.broadcasted_iota(I32, (width, width), 0)
    c = lax.broadcasted_iota(I32, (width, width), 1)
    return jnp.where((r ^ distance) == c, 1.0, 0.0).astype(BF16)


def _ctx_mixer_kernel(sink_ref, pa_ref, pb_ref, pc_ref, decf_ref, decb_ref, gng_ref, gnb_ref,
                      mix_ref, sf_ref, sb_ref, dmat_ref, zf_ref, zb_ref):
    n = SEQ
    hd = HEAD_DIM

    @pl.when(pl.program_id(0) == 0)
    def _():
        pos = lax.broadcasted_iota(I32, (n, hd), 0).astype(F32)
        zf, zb = [], []
        for h in range(H_A):
            lg_f = _log_sigmoid(decf_ref[h])
            lg_b = _log_sigmoid(decb_ref[h])
            dmat_ref[h] = _decay_matrix(lg_f, lg_b, n)
            zf.append(jnp.exp(lg_f[:, 0:hd] * (n - 1.0 - pos)) * ATTN_SCALE)
            zb.append(jnp.exp(lg_b[:, 0:hd] * pos) * ATTN_SCALE)
        for p in range(H_A // 2):
            zf_ref[p] = jnp.concatenate(zf[2 * p:2 * p + 2], axis=1)
            zb_ref[p] = jnp.concatenate(zb[2 * p:2 * p + 2], axis=1)

    low = lax.broadcasted_iota(I32, (n, LANES), 1) < hd

    def own_half(x, h):
        zero = jnp.zeros_like(x)
        return jnp.where(low, x, zero) if h % 2 == 0 else jnp.where(low, zero, x)

    def merge(first, second):
        return jnp.where(low, first, second)

    def pair_cols(ref, base, p):
        return ref[:, base + p * LANES:base + (p + 1) * LANES]

    def head_mean(x):
        first = jnp.sum(jnp.where(low, x, 0.0), -1, keepdims=True)
        second = jnp.sum(jnp.where(low, 0.0, x), -1, keepdims=True)
        return merge(first, second) * (1.0 / hd)

    for p in range(H_A // 2):
        q_pair, k_pair = pair_cols(pa_ref, 0, p), pair_cols(pa_ref, W_A, p)
        v_pair, gate_pair = pair_cols(pa_ref, 2 * W_A, p), pair_cols(pa_ref, 3 * W_A, p)
        outs = []
        for h in (2 * p, 2 * p + 1):
            a = _dot_nt(own_half(q_pair, h), k_pair)
            outs.append(_dot((a * dmat_ref[h]).astype(BF16), v_pair))
        o = merge(outs[0], outs[1])
        kf = k_pair.astype(F32)
        for st_ref, z_ref in ((sf_ref, zf_ref), (sb_ref, zb_ref)):
            st = _dot((kf * z_ref[p]).T.astype(BF16), v_pair)
            st_ref[0, 2 * p] = st[0:hd, 0:hd]
            st_ref[0, 2 * p + 1] = st[hd:2 * hd, hd:2 * hd]
        mu = head_mean(o)
        oc = o - mu
        var = head_mean(oc * oc)
        on = oc * lax.rsqrt(var + GN_EPS) * pair_cols(gng_ref, 0, p) + pair_cols(gnb_ref, 0, p)
        mix_ref[:, p * LANES:(p + 1) * LANES] = (on * _silu(gate_pair.astype(F32))).astype(BF16)

    group = H_B // KV_B
    swap = _lane_xor_matrix(LANES, hd)
    kv_k = pb_ref[:, W_B:W_B + KV_W_B]
    kv_v = pb_ref[:, W_B + KV_W_B:W_B + 2 * KV_W_B]
    scores = []
    for hh in range(H_B):
        q = own_half(pair_cols(pb_ref, 0, hh // 2), hh)
        if hh % 2 != hh // group:
            q = _dot(q, swap).astype(BF16)
        scores.append(_dot_nt(q, kv_k) * ATTN_SCALE)
    sinks = jnp.concatenate([jnp.full((n, 1), sink_ref[hh], F32) for hh in range(H_B)], axis=0)
    outs = []
    for hh, o in enumerate(_stacked_softmax_attend(scores, [kv_v] * H_B, sinks)):
        o = o.astype(BF16)
        outs.append(_dot(o, swap).astype(BF16) if hh % 2 != hh // group else o)
    for p in range(H_B // 2):
        mix_ref[:, W_A + p * LANES:W_A + (p + 1) * LANES] = merge(outs[2 * p], outs[2 * p + 1])

    scores, values = [], []
    for hh in range(H_C):
        q = own_half(pair_cols(pc_ref, 0, hh // 2), hh)
        scores.append(_dot_nt(q, pair_cols(pc_ref, W_C, hh // 2)) * ATTN_SCALE)
        values.append(pair_cols(pc_ref, 2 * W_C, hh // 2))
    outs = _stacked_softmax_attend(scores, values)
    for p in range(H_C // 2):
        mix_ref[:, W_A + W_B + p * LANES:W_A + W_B + (p + 1) * LANES] = merge(
            outs[2 * p], outs[2 * p + 1]).astype(BF16)


def _ctx_mixers(pa, pb, pc, sink, decf, decb, gng, gnb):
    return pl.pallas_call(
        _ctx_mixer_kernel,
        out_shape=(
            jax.ShapeDtypeStruct((T_CTX, D_MODEL), BF16),
            jax.ShapeDtypeStruct((BATCH, H_A, HEAD_DIM, HEAD_DIM), F32),
            jax.ShapeDtypeStruct((BATCH, H_A, HEAD_DIM, HEAD_DIM), F32),
        ),
        grid=(BATCH,),
        in_specs=[
            pl.BlockSpec(memory_space=pltpu.SMEM),
            pl.BlockSpec((SEQ, PA_W), lambda b: (b, 0)),
            pl.BlockSpec((SEQ, PB_W), lambda b: (b, 0)),
            pl.BlockSpec((SEQ, PC_W), lambda b: (b, 0)),
            pl.BlockSpec((H_A, 1, SEQ), lambda b: (0, 0, 0)),
            pl.BlockSpec((H_A, 1, SEQ), lambda b: (0, 0, 0)),
            pl.BlockSpec((1, W_A), lambda b: (0, 0)),
            pl.BlockSpec((1, W_A), lambda b: (0, 0)),
        ],
        out_specs=(
            pl.BlockSpec((SEQ, D_MODEL), lambda b: (b, 0)),
            pl.BlockSpec((1, H_A, HEAD_DIM, HEAD_DIM), lambda b: (b, 0, 0, 0)),
            pl.BlockSpec((1, H_A, HEAD_DIM, HEAD_DIM), lambda b: (b, 0, 0, 0)),
        ),
        scratch_shapes=[
            pltpu.VMEM((H_A, SEQ, SEQ), F32),
            pltpu.VMEM((H_A // 2, SEQ, LANES), F32),
            pltpu.VMEM((H_A // 2, SEQ, LANES), F32),
        ],
        compiler_params=pltpu.CompilerParams(
            dimension_semantics=("arbitrary",), vmem_limit_bytes=VMEM_LIMIT),
        name="ctx_mixers",
    )(sink, pa, pb, pc, decf, decb, gng, gnb)


def _lat_ret_kernel(pa_ref, stf_ref, stb_ref, decf_ref, decb_ref, gng_ref, gnb_ref, ya_ref, acc_ref):
    c = RET_CHUNK
    hd = HEAD_DIM
    n_chunks = DEC_SEQ // c
    pos = lax.broadcasted_iota(I32, (c, hd), 0).astype(F32)
    for h in range(H_A):
        c0 = h * hd
        lg_f = _log_sigmoid(decf_ref[h])
        lg_b = _log_sigmoid(decb_ref[h])
        dmat = _decay_matrix(lg_f, lg_b, c)
        lf = lg_f[:, 0:hd]
        lb = lg_b[:, 0:hd]
        zf = jnp.exp(lf * (c - 1.0 - pos)) * ATTN_SCALE
        zb = jnp.exp(lb * pos) * ATTN_SCALE
        xf = jnp.exp(lf * (pos + 1.0))
        xb = jnp.exp(lb * (c - pos))
        gcf = jnp.exp(lf * float(c))
        gcb = jnp.exp(lb * float(c))
        g = gng_ref[:, c0:c0 + hd]
        b = gnb_ref[:, c0:c0 + hd]

        def load(i, off):
            rows = pl.ds(pl.multiple_of(i * c, c), c)
            return pa_ref[rows, off + c0:off + c0 + hd]

        def fwd(i, s):
            q, k, v = load(i, 0), load(i, W_A), load(i, 2 * W_A)
            o = _dot((_dot_nt(q, k) * dmat).astype(BF16), v)
            o = o + _dot((q.astype(F32) * xf).astype(BF16), s.astype(BF16))
            acc_ref[pl.ds(pl.multiple_of(i * c, c), c), c0:c0 + hd] = o
            return gcf * s + _dot((k.astype(F32) * zf).T.astype(BF16), v)

        lax.fori_loop(0, n_chunks, fwd, stf_ref[0, 0, h])

        def bwd(j, s):
            i = n_chunks - 1 - j
            rows = pl.ds(pl.multiple_of(i * c, c), c)
            q, k, v = load(i, 0), load(i, W_A), load(i, 2 * W_A)
            o = acc_ref[rows, c0:c0 + hd] + _dot((q.astype(F32) * xb).astype(BF16), s.astype(BF16))
            y = _retention_readout(o, load(i, 3 * W_A), g, b)
            ya_ref[rows, c0:c0 + hd] = y.astype(BF16)
            return gcb * s + _dot((k.astype(F32) * zb).T.astype(BF16), v)

        lax.fori_loop(0, n_chunks, bwd, stb_ref[0, 0, h])


def _lat_retention(pa, st_f, st_b, layer, decf, decb, gng, gnb):
    lat0 = T_CTX // DEC_SEQ
    st_spec = pl.BlockSpec((1, 1, H_A, HEAD_DIM, HEAD_DIM), lambda b: (b, layer, 0, 0, 0))
    return pl.pallas_call(
        _lat_ret_kernel,
        out_shape=jax.ShapeDtypeStruct((T_LAT, W_A), BF16),
        grid=(DEC_BATCH,),
        in_specs=[
            pl.BlockSpec((DEC_SEQ, PA_W), lambda b: (lat0 + b, 0)),
            st_spec, st_spec,
            pl.BlockSpec((H_A, 1, RET_CHUNK), lambda b: (0, 0, 0)),
            pl.BlockSpec((H_A, 1, RET_CHUNK), lambda b: (0, 0, 0)),
            pl.BlockSpec((1, W_A), lambda b: (0, 0)),
            pl.BlockSpec((1, W_A), lambda b: (0, 0)),
        ],
        out_specs=pl.BlockSpec((DEC_SEQ, W_A), lambda b: (b, 0)),
        scratch_shapes=[pltpu.VMEM((DEC_SEQ, W_A), F32)],
        compiler_params=pltpu.CompilerParams(
            dimension_semantics=("arbitrary",), vmem_limit_bytes=VMEM_LIMIT),
        name="lat_retention",
    )(pa, st_f, st_b, decf, decb, gng, gnb)


def _rope(x, cos, sin_signed, swap):
    return x.astype(F32) * cos + _dot(x, swap) * sin_signed


def _lat_win_kernel(sink_ref, pq_ref, pseq_ref, kctx_ref, vctx_ref, cos_ref, sin_ref, yb_ref, krope_ref):
    n = pl.program_id(1)
    hd = HEAD_DIM
    qb = WIN_QB
    n_blk = DEC_SEQ // qb
    group = H_B // KV_B
    swap = _lane_xor_matrix(LANES, HEAD_DIM // 2)

    @pl.when(n == 0)
    def _():
        k = pseq_ref[:, W_B:W_B + KV_W_B]
        krope_ref[...] = _rope(k, cos_ref[...], sin_ref[...], swap).astype(BF16)

    q_rows = pl.ds(pl.multiple_of(n * qb, qb), qb)
    cos_q = cos_ref[q_rows, :]
    sin_q = sin_ref[q_rows, :]
    qr = [_rope(pq_ref[:, p * LANES:(p + 1) * LANES], cos_q, sin_q, swap).astype(BF16)
          for p in range(W_B // LANES)]

    ws = jnp.clip(n - 1, 0, n_blk - 3) * qb
    k_rows = pl.ds(pl.multiple_of(ws, qb), 3 * qb)
    q_pos = n * qb + lax.broadcasted_iota(I32, (group * qb, 3 * qb), 0) % qb
    k_pos = ws + lax.broadcasted_iota(I32, (group * qb, 3 * qb), 1)
    valid = jnp.abs(k_pos - q_pos) <= WINDOW
    head_of_row = lax.broadcasted_iota(I32, (group * qb, 1), 0) // qb
    for j in range(KV_B):
        heads = [j * group + g for g in range(group)]
        qs = jnp.concatenate(
            [qr[hh // 2][:, (hh % 2) * hd:(hh % 2 + 1) * hd] for hh in heads], axis=0)
        kw = krope_ref[k_rows, j * hd:(j + 1) * hd]
        vw = pseq_ref[k_rows, W_B + KV_W_B + j * hd:W_B + KV_W_B + (j + 1) * hd]
        kc = kctx_ref[0, 0, j].astype(BF16)
        vc = vctx_ref[0, 0, j].astype(BF16)
        s_loc = jnp.where(valid, _dot_nt(qs, kw) * ATTN_SCALE, NEG_INF)
        s_ctx = _dot_nt(qs, kc) * ATTN_SCALE
        sink = jnp.zeros((group * qb, 1), F32)
        for g, hh in enumerate(heads):
            sink = jnp.where(head_of_row == g, sink_ref[hh], sink)
        m = jnp.maximum(jnp.maximum(jnp.max(s_loc, -1, keepdims=True),
                                    jnp.max(s_ctx, -1, keepdims=True)), sink)
        p_loc = jnp.exp(s_loc - m)
        p_ctx = jnp.exp(s_ctx - m)
        den = (jnp.sum(p_loc, -1, keepdims=True) + jnp.sum(p_ctx, -1, keepdims=True)
               + jnp.exp(sink - m))
        o = (_dot(p_loc.astype(BF16), vw) + _dot(p_ctx.astype(BF16), vc)) / den
        for g, hh in enumerate(heads):
            yb_ref[:, hh * hd:(hh + 1) * hd] = o[g * qb:(g + 1) * qb].astype(BF16)


def _lat_window_attn(pb, cache_k, cache_v, layer, sink, cos_t, sin_t):
    n_blk = DEC_SEQ // WIN_QB
    lat_blk0 = T_CTX // WIN_QB
    lat_seq0 = T_CTX // DEC_SEQ
    ctx_spec = pl.BlockSpec((1, 1, KV_B, PAST_LEN, HEAD_DIM), lambda b, n: (b, layer, 0, 0, 0))
    return pl.pallas_call(
        _lat_win_kernel,
        out_shape=jax.ShapeDtypeStruct((T_LAT, W_B), BF16),
        grid=(DEC_BATCH, n_blk),
        in_specs=[
            pl.BlockSpec(memory_space=pltpu.SMEM),
            pl.BlockSpec((WIN_QB, PB_W), lambda b, n: (lat_blk0 + b * n_blk + n, 0)),
            pl.BlockSpec((DEC_SEQ, PB_W), lambda b, n: (lat_seq0 + b, 0)),
            ctx_spec, ctx_spec,
            pl.BlockSpec((DEC_SEQ, LANES), lambda b, n: (0, 0)),
            pl.BlockSpec((DEC_SEQ, LANES), lambda b, n: (0, 0)),
        ],
        out_specs=pl.BlockSpec((WIN_QB, W_B), lambda b, n: (b * n_blk + n, 0)),
        scratch_shapes=[pltpu.VMEM((DEC_SEQ, KV_W_B), BF16)],
        compiler_params=pltpu.CompilerParams(
            dimension_semantics=("arbitrary", "arbitrary"), vmem_limit_bytes=VMEM_LIMIT),
        name="lat_window_attn",
    )(sink, pb, pb, cache_k, cache_v, cos_t, sin_t)


NA_Q = NA_TILE_ROWS * GRID_W
NA_K = NA_KEY_ROWS * GRID_W
NA_TILES = DEC_SEQ // NA_Q
LAT_ROWS = DEC_SEQ // GRID_W


def _na_window_start(tile):
    return jnp.clip(tile * NA_TILE_ROWS - NA_ROWS // 2, 0, LAT_ROWS - NA_KEY_ROWS)


def _lat_na_kernel(pq_ref, pseq_ref, kctx_ref, vctx_ref, bias_ref, yc_ref):
    t = pl.program_id(1)
    hd = HEAD_DIM
    k_rows = pl.ds(pl.multiple_of(_na_window_start(t) * GRID_W, GRID_W), NA_K)
    for hh in range(H_C):
        q = pq_ref[:, hh * hd:(hh + 1) * hd]
        kw = pseq_ref[k_rows, W_C + hh * hd:W_C + (hh + 1) * hd]
        vw = pseq_ref[k_rows, 2 * W_C + hh * hd:2 * W_C + (hh + 1) * hd]
        kc = kctx_ref[0, 0, hh].astype(BF16)
        vc = vctx_ref[0, 0, hh].astype(BF16)
        s_loc = _dot_nt(q, kw) * ATTN_SCALE + bias_ref[0, hh]
        s_ctx = _dot_nt(q, kc) * ATTN_SCALE
        m = jnp.maximum(jnp.max(s_loc, -1, keepdims=True), jnp.max(s_ctx, -1, keepdims=True))
        p_loc = jnp.exp(s_loc - m)
        p_ctx = jnp.exp(s_ctx - m)
        den = jnp.sum(p_loc, -1, keepdims=True) + jnp.sum(p_ctx, -1, keepdims=True)
        o = (_dot(p_loc.astype(BF16), vw) + _dot(p_ctx.astype(BF16), vc)) / den
        yc_ref[:, hh * hd:(hh + 1) * hd] = o.astype(BF16)


def _na_tile_type(t):
    return jnp.where(t == 0, 0, jnp.where(t == NA_TILES - 1, 2, 1))


def _lat_na_attn(pc, cache_k, cache_v, layer, maskbias):
    lat_tile0 = T_CTX // NA_Q
    lat_seq0 = T_CTX // DEC_SEQ
    ctx_spec = pl.BlockSpec((1, 1, H_C, PAST_LEN, HEAD_DIM), lambda b, t: (b, layer, 0, 0, 0))
    return pl.pallas_call(
        _lat_na_kernel,
        out_shape=jax.ShapeDtypeStruct((T_LAT, W_C), BF16),
        grid=(DEC_BATCH, NA_TILES),
        in_specs=[
            pl.BlockSpec((NA_Q, PC_W), lambda b, t: (lat_tile0 + b * NA_TILES + t, 0)),
            pl.BlockSpec((DEC_SEQ, PC_W), lambda b, t: (lat_seq0 + b, 0)),
            ctx_spec, ctx_spec,
            pl.BlockSpec((1, H_C, NA_Q, NA_K), lambda b, t: (_na_tile_type(t), 0, 0, 0)),
        ],
        out_specs=pl.BlockSpec((NA_Q, W_C), lambda b, t: (b * NA_TILES + t, 0)),
        compiler_params=pltpu.CompilerParams(
            dimension_semantics=("arbitrary", "arbitrary"), vmem_limit_bytes=VMEM_LIMIT),
        name="lat_na_attn",
    )(pc, pc, cache_k, cache_v, maskbias)


def _na_block_index():
    out = np.zeros((3, NA_TILE_ROWS, NA_KEY_ROWS), np.int32)
    for ty, tile in enumerate((0, 1, NA_TILES - 1)):
        r = tile * NA_TILE_ROWS
        ws = int(np.clip(r - NA_ROWS // 2, 0, LAT_ROWS - NA_KEY_ROWS))
        for qq in range(NA_TILE_ROWS):
            qr = r + qq
            r0 = int(np.clip(qr - NA_ROWS // 2, 0, LAT_ROWS - NA_ROWS))
            for kk in range(NA_KEY_ROWS):
                kr = ws + kk
                out[ty, qq, kk] = kr - qr + NA_ROWS - 1 if r0 <= kr < r0 + NA_ROWS else 2 * NA_ROWS - 1
    return out


def _na_maskbias(rpb):
    qc = np.arange(GRID_W)[:, None]
    kc = np.arange(GRID_W)[None, :]
    c0 = np.clip(qc - NA_COLS // 2, 0, GRID_W - NA_COLS)
    col_ok = (kc >= c0) & (kc < c0 + NA_COLS)
    ci = np.clip(kc - qc + NA_COLS - 1, 0, 2 * NA_COLS - 2)
    onehot = (ci[None] == np.arange(2 * NA_COLS - 1)[:, None, None]).astype(np.float32)
    cols = jnp.einsum("hab,bqk->haqk", rpb, jnp.asarray(onehot), precision=lax.Precision.HIGHEST)
    cols = jnp.where(jnp.asarray(col_ok)[None, None], cols, NEG_INF)
    cols = jnp.concatenate([cols, jnp.full((H_C, 1, GRID_W, GRID_W), NEG_INF, F32)], axis=1)
    block_index = _na_block_index()

    def assemble(cols_ref, out_ref):
        for ty in range(3):
            for qq in range(NA_TILE_ROWS):
                for kk in range(NA_KEY_ROWS):
                    out_ref[ty, 0, qq * GRID_W:(qq + 1) * GRID_W, kk * GRID_W:(kk + 1) * GRID_W] = (
                        cols_ref[0, int(block_index[ty, qq, kk])])

    return pl.pallas_call(
        assemble,
        out_shape=jax.ShapeDtypeStruct((3, H_C, NA_Q, NA_K), F32),
        grid=(H_C,),
        in_specs=[pl.BlockSpec((1, 2 * NA_ROWS, GRID_W, GRID_W), lambda h: (h, 0, 0, 0))],
        out_specs=pl.BlockSpec((3, 1, NA_Q, NA_K), lambda h: (0, h, 0, 0)),
        compiler_params=pltpu.CompilerParams(dimension_semantics=("arbitrary",)),
        name="na_bias_assemble",
    )(cols)


def _rope_tables():
    t = np.arange(DEC_SEQ)
    n_freq = HEAD_DIM // 4
    inv = (ROPE_BASE ** (-np.arange(n_freq, dtype=np.float32) / n_freq)).astype(np.float32)
    row = (t // GRID_W).astype(np.float32)[:, None] * inv
    col = (t % GRID_W).astype(np.float32)[:, None] * inv
    ang = np.concatenate([row, col], -1)
    cos, sin = np.cos(ang), np.sin(ang)
    cos_h = np.concatenate([cos, cos], -1)
    sin_h = np.concatenate([-sin, sin], -1)
    reps = LANES // HEAD_DIM
    return (jnp.asarray(np.tile(cos_h, (1, reps)), F32), jnp.asarray(np.tile(sin_h, (1, reps)), F32))


def _first_index_of(mask, iota, sentinel):
    return jnp.min(jnp.where(mask, iota, sentinel), axis=0, keepdims=True)


def _route(logits, b_col):
    n = logits.shape[1]
    scores = jax.nn.sigmoid(logits)
    sel = scores + b_col
    io_g = lax.broadcasted_iota(I32, (GROUP_SIZE, n), 0)
    gs_rows = []
    for g in range(N_GROUPS):
        s = sel[g * GROUP_SIZE:(g + 1) * GROUP_SIZE]
        m1 = jnp.max(s, axis=0, keepdims=True)
        i1 = _first_index_of(s == m1, io_g, GROUP_SIZE)
        m2 = jnp.max(jnp.where(io_g == i1, PICKED, s), axis=0, keepdims=True)
        gs_rows.append(m1 + m2)
    gs = jnp.concatenate(gs_rows, axis=0)
    io_n = lax.broadcasted_iota(I32, (N_GROUPS, n), 0)
    gsel = jnp.zeros((N_GROUPS, n), F32)
    for _ in range(TOPK_GROUPS):
        mg = jnp.max(gs, axis=0, keepdims=True)
        gi = _first_index_of(gs == mg, io_n, N_GROUPS)
        hit = io_n == gi
        gsel = jnp.where(hit, 1.0, gsel)
        gs = jnp.where(hit, PICKED, gs)
    cand = jnp.concatenate(
        [jnp.where(gsel[g:g + 1] > 0.5, sel[g * GROUP_SIZE:(g + 1) * GROUP_SIZE], NEG_INF)
         for g in range(N_GROUPS)], axis=0)
    io_e = lax.broadcasted_iota(I32, (N_EXPERTS, n), 0)
    picks, raw = [], []
    for _ in range(TOP_K):
        mv = jnp.max(cand, axis=0, keepdims=True)
        ei = _first_index_of(cand == mv, io_e, N_EXPERTS)
        hit = io_e == ei
        picks.append((hit, ei))
        raw.append(jnp.sum(jnp.where(hit, scores, 0.0), axis=0, keepdims=True))
        cand = jnp.where(hit, PICKED, cand)
    return picks, raw


def _post_mixer_kernel(xc_ref, xl_ref, mixc_ref, ya_ref, yb_ref, yc_ref, wout_ref, mod_ref, g_ref, b_ref,
                       wr_ref, br_ref,
                       x1_ref, h2_ref, eidx_ref, wsel_ref, rank_ref, cnt_ref):
    i = pl.program_id(0)
    tm = TM_TOK

    @pl.when(i == 0)
    def _():
        cnt_ref[...] = jnp.zeros_like(cnt_ref)

    ci = _cond_row(i, tm)
    gate1 = mod_ref[pl.ds(ci, 1), 2 * D_MODEL:3 * D_MODEL]
    sh2 = mod_ref[pl.ds(ci, 1), 3 * D_MODEL:4 * D_MODEL]
    sc2 = mod_ref[pl.ds(ci, 1), 4 * D_MODEL:5 * D_MODEL]
    mix_lat = jnp.concatenate([ya_ref[...], yb_ref[...], yc_ref[...]], axis=-1)
    mix = jnp.where(i < T_CTX // tm, mixc_ref[...], mix_lat)
    y = _dot(mix, wout_ref[...])
    x = jnp.where(i < T_CTX // tm, xc_ref[...], xl_ref[...])
    x1 = _layer_norm(ALPHA * x + gate1 * y, g_ref[...], b_ref[...])
    x1_ref[...] = x1
    h2 = x1 * (1.0 + sc2) + sh2
    h_hi = h2.astype(BF16)
    h2_ref[...] = h_hi
    h_lo = (h2 - h_hi.astype(F32)).astype(BF16)
    both = (_dot(h_hi, wr_ref[...]) + _dot(h_lo, wr_ref[...])).T
    logits = both[0:N_EXPERTS] + both[N_EXPERTS:2 * N_EXPERTS]
    routed = [_route(logits[:, g * LANES:(g + 1) * LANES], br_ref[...]) for g in range(tm // LANES)]
    multi_g = []
    for picks, _ in routed:
        m = jnp.zeros((N_EXPERTS, LANES), F32)
        for hit, _ in picks:
            m = m + jnp.where(hit, 1.0, 0.0)
        multi_g.append(m)
    multi = jnp.concatenate(multi_g, axis=1)
    before = (lax.broadcasted_iota(I32, (tm, tm), 0) < lax.broadcasted_iota(I32, (tm, tm), 1))
    cum = _dot(multi.astype(BF16), jnp.where(before, 1.0, 0.0).astype(BF16))
    pad = jnp.zeros((SUBLANES - TOP_K, LANES), F32)
    for g, (picks, raw) in enumerate(routed):
        lanes = slice(g * LANES, (g + 1) * LANES)
        total = raw[0]
        for r in raw[1:]:
            total = total + r
        scale = ROUTE_SCALE / total
        cum_g = cum[:, lanes]
        eidx_ref[:, lanes] = jnp.concatenate([ei for _, ei in picks] + [pad.astype(I32)], axis=0)
        wsel_ref[:, lanes] = jnp.concatenate([r * scale for r in raw] + [pad], axis=0)
        rank_ref[:, lanes] = jnp.concatenate(
            [jnp.sum(jnp.where(hit, cum_g, 0.0), axis=0, keepdims=True) for hit, _ in picks] + [pad],
            axis=0).astype(I32)
    tile_lane = lax.broadcasted_iota(I32, (N_EXPERTS, LANES), 1)
    cnt_ref[...] = jnp.where(tile_lane == i, jnp.sum(multi, axis=1, keepdims=True), cnt_ref[...])


def _post_mixer(x_ctx, x_lat, mix_c, ya, yb, yc, w_out_bf16, mod, ln_g, ln_b, wr_split, b_router_col):
    n_ctx = T_CTX // TM_TOK
    ctx_map = lambda i: (jnp.minimum(i, n_ctx - 1), 0)
    lat_map = lambda i: (jnp.maximum(i - n_ctx, 0), 0)
    row_map = lambda i: (i, 0)
    const = lambda i: (0, 0)
    tok_map = lambda i: (0, i)
    return pl.pallas_call(
        _post_mixer_kernel,
        out_shape=(
            jax.ShapeDtypeStruct((T_ALL, D_MODEL), F32),
            jax.ShapeDtypeStruct((T_ALL, D_MODEL), BF16),
            jax.ShapeDtypeStruct((SUBLANES, T_ALL), I32),
            jax.ShapeDtypeStruct((SUBLANES, T_ALL), F32),
            jax.ShapeDtypeStruct((SUBLANES, T_ALL), I32),
            jax.ShapeDtypeStruct((N_EXPERTS, LANES), F32),
        ),
        grid=(N_TOK_TILES,),
        in_specs=[
            pl.BlockSpec((TM_TOK, D_MODEL), ctx_map),
            pl.BlockSpec((TM_TOK, D_MODEL), lat_map),
            pl.BlockSpec((TM_TOK, D_MODEL), ctx_map),
            pl.BlockSpec((TM_TOK, W_A), lat_map),
            pl.BlockSpec((TM_TOK, W_B), lat_map),
            pl.BlockSpec((TM_TOK, W_C), lat_map),
            pl.BlockSpec((D_MODEL, D_MODEL), const),
            pl.BlockSpec((N_COND, 6 * D_MODEL), const),
            pl.BlockSpec((1, D_MODEL), const),
            pl.BlockSpec((1, D_MODEL), const),
            pl.BlockSpec((D_MODEL, 2 * N_EXPERTS), const),
            pl.BlockSpec((N_EXPERTS, 1), const),
        ],
        out_specs=(
            pl.BlockSpec((TM_TOK, D_MODEL), row_map),
            pl.BlockSpec((TM_TOK, D_MODEL), row_map),
            pl.BlockSpec((SUBLANES, TM_TOK), tok_map),
            pl.BlockSpec((SUBLANES, TM_TOK), tok_map),
            pl.BlockSpec((SUBLANES, TM_TOK), tok_map),
            pl.BlockSpec((N_EXPERTS, LANES), const),
        ),
        compiler_params=pltpu.CompilerParams(
            dimension_semantics=("arbitrary",), vmem_limit_bytes=VMEM_LIMIT),
        name="post_mixer",
    )(x_ctx, x_lat, mix_c, ya, yb, yc, w_out_bf16, mod, ln_g, ln_b, wr_split, b_router_col)


PLAN_TILES = 4


def _plan_kernel(eidx_ref, rank_ref, nmat_ref, lslot_ref, unit_ref, gend_ref):
    step = pl.program_id(0)
    units = jnp.floor((nmat_ref[...] + (UNIT - 1.0)) * (1.0 / UNIT))
    units_bf = units.astype(BF16)
    earlier_e = (lax.broadcasted_iota(I32, (N_EXPERTS, N_EXPERTS), 1)
                 < lax.broadcasted_iota(I32, (N_EXPERTS, N_EXPERTS), 0))
    tri_e = jnp.where(earlier_e, 1.0, 0.0).astype(BF16)
    earlier_t = (lax.broadcasted_iota(I32, (LANES, LANES), 0) < lax.broadcasted_iota(I32, (LANES, LANES), 1))
    tri_t = jnp.where(earlier_t, 1.0, 0.0).astype(BF16)
    local_off = _dot(tri_e, units_bf)
    tile_off = _dot(units_bf, tri_t)
    per_expert = jnp.sum(units, axis=1, keepdims=True)
    blocks = jnp.floor((per_expert + (UNITS_PER_BLOCK - 1.0)) * (1.0 / UNITS_PER_BLOCK))
    blocks_l = jnp.broadcast_to(blocks, (N_EXPERTS, LANES))
    start_blk = _dot(tri_e, blocks_l.astype(BF16))
    end_blk = start_blk + blocks_l
    gend_ref[...] = (end_blk * BM).astype(I32)

    tile_lane = lax.broadcasted_iota(I32, (N_EXPERTS, LANES), 1)
    u = lax.broadcasted_iota(I32, (N_EXPERTS, MAX_UNITS), 1).astype(F32)
    io_e = lax.broadcasted_iota(I32, (N_EXPERTS, TM_TOK), 0)
    for s in range(PLAN_TILES):
        i = step * PLAN_TILES + s
        this_tile = tile_lane == i

        def column(a):
            return jnp.sum(jnp.where(this_tile, a, 0.0), axis=1, keepdims=True)

        lo, n_u = column(local_off), column(units)
        base_unit = start_blk[:, 0:1] * UNITS_PER_BLOCK + column(tile_off) - lo
        inside = jnp.where(u >= lo, jnp.where(u < lo + n_u, 1.0, 0.0), 0.0)
        dst_unit = jnp.sum(inside * (base_unit + u), axis=0, keepdims=True)
        used = jnp.sum(inside, axis=0, keepdims=True) > 0.5
        spare = (SPARE_UNIT0 + (i % 2) * MAX_UNITS).astype(F32) + u[0:1, :]
        unit_ref[s] = jnp.where(used, dst_unit, spare).astype(I32)

        toks = slice(s * TM_TOK, (s + 1) * TM_TOK)
        rows = []
        for k in range(TOP_K):
            hit = io_e == eidx_ref[k:k + 1, toks]
            seg = jnp.sum(jnp.where(hit, lo * UNIT, 0.0), axis=0, keepdims=True)
            rows.append(seg.astype(I32) + rank_ref[k:k + 1, toks])
        rows.append(jnp.full((SUBLANES - TOP_K, TM_TOK), -1, I32))
        lslot_ref[:, toks] = jnp.concatenate(rows, axis=0)


def _slot_plan(eidx, rank, nmat):
    tok_map = lambda i: (0, i)
    const = lambda i: (0, 0)
    lslot, unit_tab, gend = pl.pallas_call(
        _plan_kernel,
        out_shape=(
            jax.ShapeDtypeStruct((SUBLANES, T_ALL), I32),
            jax.ShapeDtypeStruct((N_TOK_TILES, 1, MAX_UNITS), I32),
            jax.ShapeDtypeStruct((N_EXPERTS, LANES), I32),
        ),
        grid=(N_TOK_TILES // PLAN_TILES,),
        in_specs=[
            pl.BlockSpec((SUBLANES, PLAN_TILES * TM_TOK), tok_map),
            pl.BlockSpec((SUBLANES, PLAN_TILES * TM_TOK), tok_map),
            pl.BlockSpec((N_EXPERTS, LANES), const),
        ],
        out_specs=(
            pl.BlockSpec((SUBLANES, PLAN_TILES * TM_TOK), tok_map),
            pl.BlockSpec((PLAN_TILES, 1, MAX_UNITS), lambda i: (i, 0, 0)),
            pl.BlockSpec((N_EXPERTS, LANES), const),
        ),
        compiler_params=pltpu.CompilerParams(dimension_semantics=("arbitrary",)),
        name="slot_plan",
    )(eidx, rank, nmat)
    return lslot, unit_tab.reshape(N_TOK_TILES * MAX_UNITS), gend[:, 0]


PACK_W = D_MODEL // 2
HI_HALF = -65536


def _pack_pairs(x):
    lo = lax.bitcast_convert_type(x[:, 0:PACK_W], I32)
    hi = lax.bitcast_convert_type(x[:, PACK_W:D_MODEL], I32)
    return lax.shift_right_logical(lo, 16) | (hi & HI_HALF)


def _unpack_pairs(u):
    lo = lax.bitcast_convert_type(lax.shift_left(u, 16), F32).astype(BF16)
    hi = lax.bitcast_convert_type(u & HI_HALF, F32).astype(BF16)
    return lo, hi


def _unit_rows(unit):
    row = unit * UNIT
    return pl.ds(row if isinstance(unit, int) else pl.multiple_of(row, UNIT), UNIT)


def _unit_copy(src, src_unit, dst, dst_unit, sem):
    return pltpu.make_async_copy(src.at[_unit_rows(src_unit)], dst.at[_unit_rows(dst_unit)], sem)


def _dispatch_kernel(gend_ref, tab_ref, h2_ref, lslot_ref, xs_hbm, zero_ref, local_ref, sem_zero, sem_rows):
    i = pl.program_id(0)
    buf = i % 2

    def drain(b):
        pltpu.make_async_copy(local_ref.at[b], xs_hbm.at[pl.ds(0, LOCAL_ROWS)], sem_rows.at[b]).wait()

    def has_rows(e):
        return gend_ref[e] > jnp.where(e == 0, 0, gend_ref[jnp.maximum(e - 1, 0)])

    def zero_copy(e):
        return pltpu.make_async_copy(
            zero_ref, xs_hbm.at[pl.ds(pl.multiple_of(gend_ref[e] - BM, BM), BM)], sem_zero)

    @pl.when(i == 0)
    def _():
        zero_ref[...] = jnp.zeros_like(zero_ref)

        def start(e, c):
            @pl.when(has_rows(e))
            def _():
                zero_copy(e).start()
            return c

        def wait(e, c):
            @pl.when(has_rows(e))
            def _():
                zero_copy(e).wait()
            return c

        def tail_copy(blk):
            return pltpu.make_async_copy(
                zero_ref, xs_hbm.at[pl.ds(pl.multiple_of(blk * BM, BM), BM)], sem_zero)

        def start_tail(blk, c):
            tail_copy(blk).start()
            return c

        def wait_tail(blk, c):
            tail_copy(blk).wait()
            return c

        n_used = gend_ref[N_EXPERTS - 1] // BM
        lax.fori_loop(0, N_EXPERTS, start, 0)
        lax.fori_loop(n_used, N_BLOCKS_ALL, start_tail, 0)
        lax.fori_loop(0, N_EXPERTS, wait, 0)
        lax.fori_loop(n_used, N_BLOCKS_ALL, wait_tail, 0)

    @pl.when(i >= 2)
    def _():
        drain(buf)

    h2 = h2_ref[...]
    units_per_chunk = PERM_CHUNK // UNIT
    local = local_ref.at[buf]
    for c in range(LOCAL_ROWS // PERM_CHUNK):
        slot = c * PERM_CHUNK + lax.broadcasted_iota(I32, (PERM_CHUNK, TM_TOK), 0)
        p = jnp.zeros((PERM_CHUNK, TM_TOK), F32)
        for k in range(TOP_K):
            p = jnp.where(slot == lslot_ref[k:k + 1, :], 1.0, p)
        local[c * PERM_CHUNK:(c + 1) * PERM_CHUNK, :] = _pack_pairs(_dot(p.astype(BF16), h2))
        for u in range(c * units_per_chunk, (c + 1) * units_per_chunk):
            _unit_copy(local, u, xs_hbm, tab_ref[i * MAX_UNITS + u], sem_rows.at[buf]).start()

    @pl.when(i == N_TOK_TILES - 1)
    def _():
        drain(1 - buf)
        drain(buf)


def _dispatch(h2, lslot, unit_tab, gend):
    return pl.pallas_call(
        _dispatch_kernel,
        out_shape=jax.ShapeDtypeStruct((N_SLOTS, PACK_W), I32),
        grid_spec=pltpu.PrefetchScalarGridSpec(
            num_scalar_prefetch=2,
            grid=(N_TOK_TILES,),
            in_specs=[
                pl.BlockSpec((TM_TOK, D_MODEL), lambda i, ge, tab: (i, 0)),
                pl.BlockSpec((SUBLANES, TM_TOK), lambda i, ge, tab: (0, i)),
            ],
            out_specs=pl.BlockSpec(memory_space=pl.ANY),
            scratch_shapes=[
                pltpu.VMEM((BM, PACK_W), I32),
                pltpu.VMEM((2, LOCAL_ROWS, PACK_W), I32),
                pltpu.SemaphoreType.DMA,
                pltpu.SemaphoreType.DMA((2,)),
            ],
        ),
        compiler_params=pltpu.CompilerParams(
            dimension_semantics=("arbitrary",), vmem_limit_bytes=VMEM_LIMIT),
        name="moe_dispatch",
    )(gend, unit_tab, h2, lslot)


def _expert_kernel(gend_ref, xs_hbm, wgu_ref, wdown_ref, ys_hbm, xbuf, ybuf, wgu_bf, wdown_bf, sem_in, sem_out):
    e = pl.program_id(0)
    first = jnp.where(e == 0, 0, gend_ref[jnp.maximum(e - 1, 0)]) // BM
    last = gend_ref[e] // BM
    n_used = gend_ref[N_EXPERTS - 1] // BM

    def rows_of(blk):
        return pl.ds(pl.multiple_of(blk * BM, BM), BM)

    def in_copy(blk):
        return pltpu.make_async_copy(xs_hbm.at[rows_of(blk)], xbuf.at[blk % X_BUFS], sem_in.at[blk % X_BUFS])

    def out_copy(blk):
        return pltpu.make_async_copy(ybuf.at[blk % 2], ys_hbm.at[rows_of(blk)], sem_out.at[blk % 2])

    @pl.when(e == 0)
    def _():
        for a in range(X_AHEAD):
            @pl.when(a < n_used)
            def _(a=a):
                in_copy(a).start()

    @pl.when(last > first)
    def _():
        wgu_bf[...] = wgu_ref[0, 0].astype(BF16)
        wdown_bf[...] = wdown_ref[0, 0].astype(BF16)

    def block(blk, carry):
        @pl.when(blk + X_AHEAD < n_used)
        def _():
            in_copy(blk + X_AHEAD).start()

        in_copy(blk).wait()

        @pl.when(blk >= 2)
        def _():
            out_copy(blk - 2).wait()

        x_lo, x_hi = _unpack_pairs(xbuf[blk % X_BUFS])
        gu = _dot(x_lo, wgu_bf[0:PACK_W, :]) + _dot(x_hi, wgu_bf[PACK_W:D_MODEL, :])
        act = _silu(gu[:, 0:D_EXPERT]) * gu[:, D_EXPERT:2 * D_EXPERT]
        y = _dot(act.astype(BF16), wdown_bf[...])
        ybuf[blk % 2] = _pack_pairs(y.astype(BF16).astype(F32))
        out_copy(blk).start()
        return carry

    lax.fori_loop(first, last, block, 0)

    @pl.when(e == N_EXPERTS - 1)
    def _():
        @pl.when(n_used >= 2)
        def _():
            out_copy(n_used - 2).wait()

        @pl.when(n_used >= 1)
        def _():
            out_copy(n_used - 1).wait()

        ybuf[0] = jnp.zeros((BM, PACK_W), I32)

        def tail_copy(blk):
            return pltpu.make_async_copy(ybuf.at[0], ys_hbm.at[rows_of(blk)], sem_out.at[0])

        def start(blk, c):
            tail_copy(blk).start()
            return c

        def wait(blk, c):
            tail_copy(blk).wait()
            return c

        lax.fori_loop(n_used, N_BLOCKS_ALL, start, 0)
        lax.fori_loop(n_used, N_BLOCKS_ALL, wait, 0)


def _experts(xs, w_gu, w_down, layer, gend):
    return pl.pallas_call(
        _expert_kernel,
        out_shape=jax.ShapeDtypeStruct((N_SLOTS, PACK_W), I32),
        grid_spec=pltpu.PrefetchScalarGridSpec(
            num_scalar_prefetch=1,
            grid=(N_EXPERTS,),
            in_specs=[
                pl.BlockSpec(memory_space=pl.ANY),
                pl.BlockSpec((1, 1, D_MODEL, 2 * D_EXPERT), lambda e, ge: (layer, e, 0, 0)),
                pl.BlockSpec((1, 1, D_EXPERT, D_MODEL), lambda e, ge: (layer, e, 0, 0)),
            ],
            out_specs=pl.BlockSpec(memory_space=pl.ANY),
            scratch_shapes=[
                pltpu.VMEM((X_BUFS, BM, PACK_W), I32),
                pltpu.VMEM((2, BM, PACK_W), I32),
                pltpu.VMEM((D_MODEL, 2 * D_EXPERT), BF16),
                pltpu.VMEM((D_EXPERT, D_MODEL), BF16),
                pltpu.SemaphoreType.DMA((X_BUFS,)),
                pltpu.SemaphoreType.DMA((2,)),
            ],
        ),
        compiler_params=pltpu.CompilerParams(
            dimension_semantics=("arbitrary",), vmem_limit_bytes=VMEM_LIMIT),
        name="moe_experts",
    )(gend, xs, w_gu, w_down)


def _combine_kernel(tab_ref, x1_ref, h2_ref, lslot_ref, wsel_ref, ys_hbm, wsgu_ref, wsdown_ref, mod_ref,
                    g_ref, b_ref, outc_ref, outl_ref, local_ref, sel_ref, ylo_ref, yhi_ref, sem_rows):
    i = pl.program_id(0)
    tm = TM_TOK
    buf = i % 2

    def fetch_unit(tile, b, u):
        _unit_copy(ys_hbm, tab_ref[tile * MAX_UNITS + u], local_ref.at[b], u, sem_rows.at[b]).start()

    def drain(b):
        pltpu.make_async_copy(ys_hbm.at[pl.ds(0, LOCAL_ROWS)], local_ref.at[b], sem_rows.at[b]).wait()

    @pl.when(i == 0)
    def _():
        def body(u, c):
            fetch_unit(0, 0, u)
            return c

        lax.fori_loop(0, MAX_UNITS, body, 0, unroll=8)

    sgu = _dot(h2_ref[...], wsgu_ref[...])
    act = _silu(sgu[:, 0:D_SHARED]) * sgu[:, D_SHARED:2 * D_SHARED]
    f = _dot(act.astype(BF16), wsdown_ref[...])

    nxt = jnp.minimum(i + 1, N_TOK_TILES - 1)
    n_groups = tm // SEL_ROWS
    units_per_group = MAX_UNITS // n_groups
    slot = lax.broadcasted_iota(I32, (SEL_ROWS, LOCAL_ROWS), 1)
    for g in range(n_groups):
        rows = slice(g * SEL_ROWS, (g + 1) * SEL_ROWS)
        sel = jnp.zeros((SEL_ROWS, LOCAL_ROWS), F32)
        for k in range(TOP_K):
            sel = jnp.where(slot == lslot_ref[rows, k:k + 1], wsel_ref[rows, k:k + 1], sel)
        sel_ref[rows, :] = sel.astype(BF16)
        for u in range(g * units_per_group, (g + 1) * units_per_group):
            fetch_unit(nxt, 1 - buf, u)

    local = local_ref.at[buf]
    drain(buf)
    for c in range(LOCAL_ROWS // PERM_CHUNK):
        rows = slice(c * PERM_CHUNK, (c + 1) * PERM_CHUNK)
        ylo_ref[rows, :], yhi_ref[rows, :] = _unpack_pairs(local[rows, :])
    sel = sel_ref[...]
    f = f + jnp.concatenate([_dot(sel, ylo_ref[...]), _dot(sel, yhi_ref[...])], axis=-1)
    ci = _cond_row(i, tm)
    gate2 = mod_ref[pl.ds(ci, 1), 5 * D_MODEL:6 * D_MODEL]
    out = _layer_norm(ALPHA * x1_ref[...] + gate2 * f, g_ref[...], b_ref[...])

    @pl.when(i < T_CTX // tm)
    def _():
        outc_ref[...] = out

    @pl.when(i >= T_CTX // tm)
    def _():
        outl_ref[...] = out

    @pl.when(i == N_TOK_TILES - 1)
    def _():
        drain(1 - buf)


def _combine(x1, h2, lslot_rows, wsel_rows, unit_tab, ys, w_sgu_bf16, w_sdown_bf16, mod, ln_g, ln_b):
    n_ctx = T_CTX // TM_TOK
    row_map = lambda i, tab: (i, 0)
    const = lambda i, tab: (0, 0)
    return pl.pallas_call(
        _combine_kernel,
        out_shape=(jax.ShapeDtypeStruct((T_CTX, D_MODEL), F32),
                   jax.ShapeDtypeStruct((T_LAT, D_MODEL), F32)),
        grid_spec=pltpu.PrefetchScalarGridSpec(
            num_scalar_prefetch=1,
            grid=(N_TOK_TILES,),
            in_specs=[
                pl.BlockSpec((TM_TOK, D_MODEL), row_map),
                pl.BlockSpec((TM_TOK, D_MODEL), row_map),
                pl.BlockSpec((TM_TOK, SUBLANES), row_map),
                pl.BlockSpec((TM_TOK, SUBLANES), row_map),
                pl.BlockSpec(memory_space=pl.ANY),
                pl.BlockSpec((D_MODEL, 2 * D_SHARED), const),
                pl.BlockSpec((D_SHARED, D_MODEL), const),
                pl.BlockSpec((N_COND, 6 * D_MODEL), const),
                pl.BlockSpec((1, D_MODEL), const),
                pl.BlockSpec((1, D_MODEL), const),
            ],
            out_specs=(pl.BlockSpec((TM_TOK, D_MODEL), lambda i, tab: (jnp.minimum(i, n_ctx - 1), 0)),
                       pl.BlockSpec((TM_TOK, D_MODEL), lambda i, tab: (jnp.maximum(i - n_ctx, 0), 0))),
            scratch_shapes=[
                pltpu.VMEM((2, LOCAL_ROWS, PACK_W), I32),
                pltpu.VMEM((TM_TOK, LOCAL_ROWS), BF16),
                pltpu.VMEM((LOCAL_ROWS, PACK_W), BF16),
                pltpu.VMEM((LOCAL_ROWS, PACK_W), BF16),
                pltpu.SemaphoreType.DMA((2,)),
            ],
        ),
        compiler_params=pltpu.CompilerParams(
            dimension_semantics=("arbitrary",), vmem_limit_bytes=VMEM_LIMIT),
        name="moe_combine",
    )(unit_tab, x1, h2, lslot_rows, wsel_rows, ys, w_sgu_bf16, w_sdown_bf16, mod, ln_g, ln_b)


def _lane_rows(v, width):
    return jnp.broadcast_to(v.astype(F32)[:, None, None], (v.shape[0], 1, width))


def kernel(x_prompt, x_sample, state_ret_fwd, state_ret_bwd, cache_win_k, cache_win_v, cache_na_k, cache_na_v, c, c_ctx, w_in, w_out, ret_decay_fwd, ret_decay_bwd, ret_gn_g, ret_gn_b, win_sink, na_rpb, w_mod, b_mod, ln1_g, ln1_b, ln2_g, ln2_b, w_router, b_router, w_expert_gu, w_expert_down, w_shared_gu, w_shared_down):
    cond = jnp.concatenate(
        [c_ctx[None, :], c, jnp.zeros((N_COND - 1 - DEC_BATCH, D_MODEL), F32)], axis=0)
    mod_all = _modulation(cond, w_mod, b_mod)
    cos_t, sin_t = _rope_tables()

    x_ctx = x_prompt.reshape(T_CTX, D_MODEL)
    x_lat = x_sample.reshape(T_LAT, D_MODEL)
    sf_l, sb_l, caches = [], [], ()
    for l in range(DEPTH):
        mod = mod_all[l]
        pa, pb, pc, *caches = _in_projection(x_ctx, x_lat, mod, w_in[l].astype(BF16), tuple(caches))
        decf_s, decb_s = _lane_rows(ret_decay_fwd[l], SEQ), _lane_rows(ret_decay_bwd[l], SEQ)
        gng, gnb = ret_gn_g[l][None, :], ret_gn_b[l][None, :]
        mix_c, st_f, st_b = _ctx_mixers(pa, pb, pc, win_sink[l], decf_s, decb_s, gng, gnb)
        sf_l.append(st_f)
        sb_l.append(st_b)
        ya = _lat_retention(pa, state_ret_fwd, state_ret_bwd, l,
                            _lane_rows(ret_decay_fwd[l], RET_CHUNK), _lane_rows(ret_decay_bwd[l], RET_CHUNK),
                            gng, gnb)
        yb = _lat_window_attn(pb, cache_win_k, cache_win_v, l, win_sink[l], cos_t, sin_t)
        yc = _lat_na_attn(pc, cache_na_k, cache_na_v, l, _na_maskbias(na_rpb[l]))

        wr_hi = w_router[l].astype(BF16)
        wr_lo = (w_router[l] - wr_hi.astype(F32)).astype(BF16)
        x1, h2, eidx, wsel, rank, counts = _post_mixer(
            x_ctx, x_lat, mix_c, ya, yb, yc, w_out[l].astype(BF16), mod, ln1_g[l][None, :],
            ln1_b[l][None, :], jnp.concatenate([wr_hi, wr_lo], axis=1), b_router[l][:, None])
        lslot, unit_tab, gend = _slot_plan(eidx, rank, counts)
        xs = _dispatch(h2, lslot, unit_tab, gend)
        ys = _experts(xs, w_expert_gu, w_expert_down, l, gend)
        x_ctx, x_lat = _combine(x1, h2, lslot.T, wsel.T, unit_tab, ys, w_shared_gu[l].astype(BF16),
                                w_shared_down[l].astype(BF16), mod, ln2_g[l][None, :], ln2_b[l][None, :])

    y_prompt = x_ctx.reshape(BATCH, SEQ, D_MODEL)
    y_sample = x_lat.reshape(DEC_BATCH, DEC_SEQ, D_MODEL)
    new_sf = jnp.stack(sf_l, axis=1)
    new_sb = jnp.stack(sb_l, axis=1)

    return (y_prompt, y_sample, new_sf, new_sb, *caches)
```

```python
import functools

import numpy as np
import jax
import jax.numpy as jnp
from jax import lax
from jax.experimental import pallas as pl
from jax.experimental.pallas import tpu as pltpu

F32 = jnp.float32
BF16 = jnp.bfloat16
I32 = jnp.int32

D_MODEL = 1024
BATCH = 32
SEQ = 256
DEPTH = 2
DEC_BATCH = 4
DEC_SEQ = 2048
PAST_LEN = 256
GRID_W = 64
HEAD_DIM = 64
ATTN_SCALE = HEAD_DIM ** -0.5
H_A = 4
W_A = H_A * HEAD_DIM
GN_EPS = 1e-5
H_B = 6
KV_B = 2
W_B = H_B * HEAD_DIM
KV_W_B = KV_B * HEAD_DIM
WINDOW = 128
ROPE_BASE = 10000.0
H_C = 6
W_C = H_C * HEAD_DIM
NA_ROWS = 8
NA_COLS = 16
IN_WIDTH = 4 * W_A + W_B + 2 * KV_W_B + 3 * W_C
N_EXPERTS = 64
TOP_K = 6
N_GROUPS = 8
GROUP_SIZE = N_EXPERTS // N_GROUPS
TOPK_GROUPS = 4
D_EXPERT = 256
D_SHARED = 256
ROUTE_SCALE = 2.5
ALPHA = (2 * DEPTH) ** 0.25
LN_EPS = 1e-5
NEG_INF = -1e30
PICKED = -3e38

T_CTX = BATCH * SEQ
T_LAT = DEC_BATCH * DEC_SEQ
T_ALL = T_CTX + T_LAT
N_COND = 8

PA_W = 4 * W_A
PB_W = W_B + 2 * KV_W_B
PC_W = 3 * W_C

LANES = 128
SUBLANES = 8
VMEM_LIMIT = 56 * 1024 * 1024

TM_PROJ = 512
TM_TOK = 256
RET_CHUNK = 256
WIN_QB = 256
WIN_BAND = WIN_QB + 2 * WINDOW
NA_TILE_ROWS = 4
NA_KEY_ROWS = 11
BM = 512
X_AHEAD = 3
X_BUFS = X_AHEAD + 1
UNIT = SUBLANES
UNITS_PER_BLOCK = BM // UNIT
PERM_CHUNK = 256
SEL_ROWS = 16
N_TOK_TILES = T_ALL // TM_TOK
LOCAL_ROWS = -(-(TM_TOK * TOP_K + N_EXPERTS * (UNIT - 1)) // PERM_CHUNK) * PERM_CHUNK
MAX_UNITS = LOCAL_ROWS // UNIT
N_ASSIGN = T_ALL * TOP_K
N_BLOCKS = -(-(N_ASSIGN + N_TOK_TILES * N_EXPERTS * (UNIT - 1) + N_EXPERTS * (BM - 1)) // BM)
SPARE_BLOCKS = 2 * -(-LOCAL_ROWS // BM)
SPARE_UNIT0 = N_BLOCKS * UNITS_PER_BLOCK
N_BLOCKS_ALL = N_BLOCKS + SPARE_BLOCKS
N_SLOTS = N_BLOCKS_ALL * BM


def _dot(a, b):
    return jnp.dot(a, b, preferred_element_type=F32)


def _dot_nt(a, b):
    return lax.dot_general(a, b, (((1,), (1,)), ((), ())), preferred_element_type=F32)


def _silu(x):
    return x * jax.nn.sigmoid(x)


def _log_sigmoid(x):
    return jnp.minimum(x, 0.0) - jnp.log(1.0 + jnp.exp(-jnp.abs(x)))


def _cond_row(tile, tile_rows):
    n_ctx = T_CTX // tile_rows
    per_lat = DEC_SEQ // tile_rows
    return jnp.where(tile < n_ctx, 0, 1 + (tile - n_ctx) // per_lat)


def _layer_norm(x, g, b):
    mu = jnp.mean(x, -1, keepdims=True)
    xc = x - mu
    var = jnp.mean(xc * xc, -1, keepdims=True)
    return xc * lax.rsqrt(var + LN_EPS) * g + b


MOD_TN = 1536


def _mod_kernel(cond_ref, w_ref, b_ref, o_ref):
    s = _silu(cond_ref[...])
    s_hi = s.astype(BF16)
    s_lo = (s - s_hi.astype(F32)).astype(BF16)
    w = w_ref[0]
    w_hi = w.astype(BF16)
    w_lo = (w - w_hi.astype(F32)).astype(BF16)
    o_ref[0] = _dot(s_hi, w_hi) + _dot(s_lo, w_hi) + _dot(s_hi, w_lo) + b_ref[0]


def _modulation(cond, w_mod, b_mod):
    n_out = 6 * D_MODEL
    return pl.pallas_call(
        _mod_kernel,
        out_shape=jax.ShapeDtypeStruct((DEPTH, N_COND, n_out), F32),
        grid=(DEPTH, n_out // MOD_TN),
        in_specs=[
            pl.BlockSpec((N_COND, D_MODEL), lambda l, j: (0, 0)),
            pl.BlockSpec((1, D_MODEL, MOD_TN), lambda l, j: (l, 0, j)),
            pl.BlockSpec((1, 1, MOD_TN), lambda l, j: (l, 0, j)),
        ],
        out_specs=pl.BlockSpec((1, N_COND, MOD_TN), lambda l, j: (l, 0, j)),
        compiler_params=pltpu.CompilerParams(
            dimension_semantics=("arbitrary", "arbitrary"), vmem_limit_bytes=VMEM_LIMIT),
        name="modulation",
    )(cond, w_mod, b_mod.reshape(DEPTH, 1, n_out))


SEQ_PER_PROJ = TM_PROJ // SEQ


def _inproj_kernel(n_prev, xc_ref, xl_ref, mod_ref, w_ref, *refs):
    prev_refs = refs[:4] if n_prev else ()
    pa_ref, pb_ref, pc_ref, wk_ref, wv_ref, nk_ref, nv_ref = refs[len(prev_refs):]
    i = pl.program_id(0)
    ci = _cond_row(i, TM_PROJ)
    sh = mod_ref[pl.ds(ci, 1), 0:D_MODEL]
    sc = mod_ref[pl.ds(ci, 1), D_MODEL:2 * D_MODEL]
    x = jnp.where(i < T_CTX // TM_PROJ, xc_ref[...], xl_ref[...])
    h = x * (1.0 + sc) + sh
    p = _dot(h.astype(BF16), w_ref[...])
    pa_ref[...] = p[:, 0:PA_W].astype(BF16)
    pb_ref[...] = p[:, PA_W:PA_W + PB_W].astype(BF16)
    pc_ref[...] = p[:, PA_W + PB_W:IN_WIDTH].astype(BF16)

    @pl.when(i < T_CTX // TM_PROJ)
    def _():
        targets = ((wk_ref, PA_W + W_B, KV_B), (wv_ref, PA_W + W_B + KV_W_B, KV_B),
                   (nk_ref, PA_W + PB_W + W_C, H_C), (nv_ref, PA_W + PB_W + 2 * W_C, H_C))
        for j, (ref, col0, n_heads) in enumerate(targets):
            if n_prev:
                ref[:, 0:n_prev] = prev_refs[j][...]
            for s in range(SEQ_PER_PROJ):
                for hh in range(n_heads):
                    ref[s, n_prev, hh] = p[s * SEQ:(s + 1) * SEQ,
                                           col0 + hh * HEAD_DIM:col0 + (hh + 1) * HEAD_DIM]


def _in_projection(x_ctx, x_lat, mod, w_in_bf16, earlier):
    n_ctx_tiles = T_CTX // TM_PROJ
    n_prev = earlier[0].shape[1] if earlier else 0

    def cache_spec(n_layers, n_heads):
        return pl.BlockSpec((SEQ_PER_PROJ, n_layers, n_heads, SEQ, HEAD_DIM),
                            lambda i: (jnp.minimum(i, n_ctx_tiles - 1), 0, 0, 0, 0))

    cache_heads = (KV_B, KV_B, H_C, H_C)
    return pl.pallas_call(
        functools.partial(_inproj_kernel, n_prev),
        out_shape=(
            jax.ShapeDtypeStruct((T_ALL, PA_W), BF16),
            jax.ShapeDtypeStruct((T_ALL, PB_W), BF16),
            jax.ShapeDtypeStruct((T_ALL, PC_W), BF16),
        ) + tuple(jax.ShapeDtypeStruct((BATCH, n_prev + 1, nh, SEQ, HEAD_DIM), F32) for nh in cache_heads),
        grid=(T_ALL // TM_PROJ,),
        in_specs=[
            pl.BlockSpec((TM_PROJ, D_MODEL), lambda i: (jnp.minimum(i, n_ctx_tiles - 1), 0)),
            pl.BlockSpec((TM_PROJ, D_MODEL), lambda i: (jnp.maximum(i - n_ctx_tiles, 0), 0)),
            pl.BlockSpec((N_COND, 6 * D_MODEL), lambda i: (0, 0)),
            pl.BlockSpec((D_MODEL, IN_WIDTH), lambda i: (0, 0)),
        ] + [cache_spec(n_prev, nh) for nh in cache_heads if n_prev],
        out_specs=(
            pl.BlockSpec((TM_PROJ, PA_W), lambda i: (i, 0)),
            pl.BlockSpec((TM_PROJ, PB_W), lambda i: (i, 0)),
            pl.BlockSpec((TM_PROJ, PC_W), lambda i: (i, 0)),
        ) + tuple(cache_spec(n_prev + 1, nh) for nh in cache_heads),
        compiler_params=pltpu.CompilerParams(
            dimension_semantics=("arbitrary",), vmem_limit_bytes=VMEM_LIMIT),
        name="in_projection",
    )(x_ctx, x_lat, mod, w_in_bf16, *earlier)


def _decay_matrix(lg_f, lg_b, n):
    row = lax.broadcasted_iota(I32, (n, n), 0)
    col = lax.broadcasted_iota(I32, (n, n), 1)
    diff = (row - col).astype(F32)
    fwd = jnp.where(diff >= 0, jnp.exp(lg_f * jnp.maximum(diff, 0.0)), 0.0)
    bwd = jnp.where(diff <= 0, jnp.exp(lg_b * jnp.maximum(-diff, 0.0)), 0.0)
    return (fwd + bwd) * ATTN_SCALE


def _retention_readout(o, gate, g, b):
    mu = jnp.mean(o, -1, keepdims=True)
    oc = o - mu
    var = jnp.mean(oc * oc, -1, keepdims=True)
    on = oc * lax.rsqrt(var + GN_EPS) * g + b
    return on * _silu(gate.astype(F32))


def _stacked_softmax_attend(scores, values, extra_logit=None):
    rows = scores[0].shape[0]
    s = jnp.concatenate(scores, axis=0)
    m = jnp.max(s, -1, keepdims=True)
    if extra_logit is not None:
        m = jnp.maximum(m, extra_logit)
    p = jnp.exp(s - m)
    den = jnp.sum(p, -1, keepdims=True)
    if extra_logit is not None:
        den = den + jnp.exp(extra_logit - m)
    p = p.astype(BF16)
    return [_dot(p[h * rows:(h + 1) * rows], v) / den[h * rows:(h + 1) * rows]
            for h, v in enumerate(values)]


def _lane_xor_matrix(width, distance):
    r = lax.broadcasted_iota(I32, (width, width), 0)
    c = lax.broadcasted_iota(I32, (width, width), 1)
    return jnp.where((r ^ distance) == c, 1.0, 0.0).astype(BF16)


def _ctx_mixer_kernel(sink_ref, pa_ref, pb_ref, pc_ref, decf_ref, decb_ref, gng_ref, gnb_ref,
                      mix_ref, sf_ref, sb_ref, dmat_ref, zf_ref, zb_ref):
    n = SEQ
    hd = HEAD_DIM

    @pl.when(pl.program_id(0) == 0)
    def _():
        pos = lax.broadcasted_iota(I32, (n, hd), 0).astype(F32)
        zf, zb = [], []
        for h in range(H_A):
            lg_f = _log_sigmoid(decf_ref[h])
            lg_b = _log_sigmoid(decb_ref[h])
            dmat_ref[h] = _decay_matrix(lg_f, lg_b, n)
            zf.append(jnp.exp(lg_f[:, 0:hd] * (n - 1.0 - pos)) * ATTN_SCALE)
            zb.append(jnp.exp(lg_b[:, 0:hd] * pos) * ATTN_SCALE)
        for p in range(H_A // 2):
            zf_ref[p] = jnp.concatenate(zf[2 * p:2 * p + 2], axis=1)
            zb_ref[p] = jnp.concatenate(zb[2 * p:2 * p + 2], axis=1)

    low = lax.broadcasted_iota(I32, (n, LANES), 1) < hd

    def own_half(x, h):
        zero = jnp.zeros_like(x)
        return jnp.where(low, x, zero) if h % 2 == 0 else jnp.where(low, zero, x)

    def merge(first, second):
        return jnp.where(low, first, second)

    def pair_cols(ref, base, p):
        return ref[:, base + p * LANES:base + (p + 1) * LANES]

    def head_mean(x):
        first = jnp.sum(jnp.where(low, x, 0.0), -1, keepdims=True)
        second = jnp.sum(jnp.where(low, 0.0, x), -1, keepdims=True)
        return merge(first, second) * (1.0 / hd)

    for p in range(H_A // 2):
        q_pair, k_pair = pair_cols(pa_ref, 0, p), pair_cols(pa_ref, W_A, p)
        v_pair, gate_pair = pair_cols(pa_ref, 2 * W_A, p), pair_cols(pa_ref, 3 * W_A, p)
        outs = []
        for h in (2 * p, 2 * p + 1):
            a = _dot_nt(own_half(q_pair, h), k_pair)
            outs.append(_dot((a * dmat_ref[h]).astype(BF16), v_pair))
        o = merge(outs[0], outs[1])
        kf = k_pair.astype(F32)
        for st_ref, z_ref in ((sf_ref, zf_ref), (sb_ref, zb_ref)):
            st = _dot((kf * z_ref[p]).T.astype(BF16), v_pair)
            st_ref[0, 2 * p] = st[0:hd, 0:hd]
            st_ref[0, 2 * p + 1] = st[hd:2 * hd, hd:2 * hd]
        mu = head_mean(o)
        oc = o - mu
        var = head_mean(oc * oc)
        on = oc * lax.rsqrt(var + GN_EPS) * pair_cols(gng_ref, 0, p) + pair_cols(gnb_ref, 0, p)
        mix_ref[:, p * LANES:(p + 1) * LANES] = (on * _silu(gate_pair.astype(F32))).astype(BF16)

    group = H_B // KV_B
    swap = _lane_xor_matrix(LANES, hd)
    kv_k = pb_ref[:, W_B:W_B + KV_W_B]
    kv_v = pb_ref[:, W_B + KV_W_B:W_B + 2 * KV_W_B]
    scores = []
    for hh in range(H_B):
        q = own_half(pair_cols(pb_ref, 0, hh // 2), hh)
        if hh % 2 != hh // group:
            q = _dot(q, swap).astype(BF16)
        scores.append(_dot_nt(q, kv_k) * ATTN_SCALE)
    sinks = jnp.concatenate([jnp.full((n, 1), sink_ref[hh], F32) for hh in range(H_B)], axis=0)
    outs = []
    for hh, o in enumerate(_stacked_softmax_attend(scores, [kv_v] * H_B, sinks)):
        o = o.astype(BF16)
        outs.append(_dot(o, swap).astype(BF16) if hh % 2 != hh // group else o)
    for p in range(H_B // 2):
        mix_ref[:, W_A + p * LANES:W_A + (p + 1) * LANES] = merge(outs[2 * p], outs[2 * p + 1])

    scores, values = [], []
    for hh in range(H_C):
        q = own_half(pair_cols(pc_ref, 0, hh // 2), hh)
        scores.append(_dot_nt(q, pair_cols(pc_ref, W_C, hh // 2)) * ATTN_SCALE)
        values.append(pair_cols(pc_ref, 2 * W_C, hh // 2))
    outs = _stacked_softmax_attend(scores, values)
    for p in range(H_C // 2):
        mix_ref[:, W_A + W_B + p * LANES:W_A + W_B + (p + 1) * LANES] = merge(
            outs[2 * p], outs[2 * p + 1]).astype(BF16)


def _ctx_mixers(pa, pb, pc, sink, decf, decb, gng, gnb):
    return pl.pallas_call(
        _ctx_mixer_kernel,
        out_shape=(
            jax.ShapeDtypeStruct((T_CTX, D_MODEL), BF16),
            jax.ShapeDtypeStruct((BATCH, H_A, HEAD_DIM, HEAD_DIM), F32),
            jax.ShapeDtypeStruct((BATCH, H_A, HEAD_DIM, HEAD_DIM), F32),
        ),
        grid=(BATCH,),
        in_specs=[
            pl.BlockSpec(memory_space=pltpu.SMEM),
            pl.BlockSpec((SEQ, PA_W), lambda b: (b, 0)),
            pl.BlockSpec((SEQ, PB_W), lambda b: (b, 0)),
            pl.BlockSpec((SEQ, PC_W), lambda b: (b, 0)),
            pl.BlockSpec((H_A, 1, SEQ), lambda b: (0, 0, 0)),
            pl.BlockSpec((H_A, 1, SEQ), lambda b: (0, 0, 0)),
            pl.BlockSpec((1, W_A), lambda b: (0, 0)),
            pl.BlockSpec((1, W_A), lambda b: (0, 0)),
        ],
        out_specs=(
            pl.BlockSpec((SEQ, D_MODEL), lambda b: (b, 0)),
            pl.BlockSpec((1, H_A, HEAD_DIM, HEAD_DIM), lambda b: (b, 0, 0, 0)),
            pl.BlockSpec((1, H_A, HEAD_DIM, HEAD_DIM), lambda b: (b, 0, 0, 0)),
        ),
        scratch_shapes=[
            pltpu.VMEM((H_A, SEQ, SEQ), F32),
            pltpu.VMEM((H_A // 2, SEQ, LANES), F32),
            pltpu.VMEM((H_A // 2, SEQ, LANES), F32),
        ],
        compiler_params=pltpu.CompilerParams(
            dimension_semantics=("arbitrary",), vmem_limit_bytes=VMEM_LIMIT),
        name="ctx_mixers",
    )(sink, pa, pb, pc, decf, decb, gng, gnb)


def _lat_ret_kernel(pa_ref, stf_ref, stb_ref, decf_ref, decb_ref, gng_ref, gnb_ref, ya_ref, acc_ref):
    c = RET_CHUNK
    hd = HEAD_DIM
    n_chunks = DEC_SEQ // c
    pos = lax.broadcasted_iota(I32, (c, hd), 0).astype(F32)
    for h in range(H_A):
        c0 = h * hd
        lg_f = _log_sigmoid(decf_ref[h])
        lg_b = _log_sigmoid(decb_ref[h])
        dmat = _decay_matrix(lg_f, lg_b, c)
        lf = lg_f[:, 0:hd]
        lb = lg_b[:, 0:hd]
        zf = jnp.exp(lf * (c - 1.0 - pos)) * ATTN_SCALE
        zb = jnp.exp(lb * pos) * ATTN_SCALE
        xf = jnp.exp(lf * (pos + 1.0))
        xb = jnp.exp(lb * (c - pos))
        gcf = jnp.exp(lf * float(c))
        gcb = jnp.exp(lb * float(c))
        g = gng_ref[:, c0:c0 + hd]
        b = gnb_ref[:, c0:c0 + hd]

        def load(i, off):
            rows = pl.ds(pl.multiple_of(i * c, c), c)
            return pa_ref[rows, off + c0:off + c0 + hd]

        def fwd(i, s):
            q, k, v = load(i, 0), load(i, W_A), load(i, 2 * W_A)
            o = _dot((_dot_nt(q, k) * dmat).astype(BF16), v)
            o = o + _dot((q.astype(F32) * xf).astype(BF16), s.astype(BF16))
            acc_ref[pl.ds(pl.multiple_of(i * c, c), c), c0:c0 + hd] = o
            return gcf * s + _dot((k.astype(F32) * zf).T.astype(BF16), v)

        lax.fori_loop(0, n_chunks, fwd, stf_ref[0, 0, h])

        def bwd(j, s):
            i = n_chunks - 1 - j
            rows = pl.ds(pl.multiple_of(i * c, c), c)
            q, k, v = load(i, 0), load(i, W_A), load(i, 2 * W_A)
            o = acc_ref[rows, c0:c0 + hd] + _dot((q.astype(F32) * xb).astype(BF16), s.astype(BF16))
            y = _retention_readout(o, load(i, 3 * W_A), g, b)
            ya_ref[rows, c0:c0 + hd] = y.astype(BF16)
            return gcb * s + _dot((k.astype(F32) * zb).T.astype(BF16), v)

        lax.fori_loop(0, n_chunks, bwd, stb_ref[0, 0, h])


def _lat_retention(pa, st_f, st_b, layer, decf, decb, gng, gnb):
    lat0 = T_CTX // DEC_SEQ
    st_spec = pl.BlockSpec((1, 1, H_A, HEAD_DIM, HEAD_DIM), lambda b: (b, layer, 0, 0, 0))
    return pl.pallas_call(
        _lat_ret_kernel,
        out_shape=jax.ShapeDtypeStruct((T_LAT, W_A), BF16),
        grid=(DEC_BATCH,),
        in_specs=[
            pl.BlockSpec((DEC_SEQ, PA_W), lambda b: (lat0 + b, 0)),
            st_spec, st_spec,
            pl.BlockSpec((H_A, 1, RET_CHUNK), lambda b: (0, 0, 0)),
            pl.BlockSpec((H_A, 1, RET_CHUNK), lambda b: (0, 0, 0)),
            pl.BlockSpec((1, W_A), lambda b: (0, 0)),
            pl.BlockSpec((1, W_A), lambda b: (0, 0)),
        ],
        out_specs=pl.BlockSpec((DEC_SEQ, W_A), lambda b: (b, 0)),
        scratch_shapes=[pltpu.VMEM((DEC_SEQ, W_A), F32)],
        compiler_params=pltpu.CompilerParams(
            dimension_semantics=("arbitrary",), vmem_limit_bytes=VMEM_LIMIT),
        name="lat_retention",
    )(pa, st_f, st_b, decf, decb, gng, gnb)


def _rope(x, cos, sin_signed, swap):
    return x.astype(F32) * cos + _dot(x, swap) * sin_signed


def _lat_win_kernel(sink_ref, pq_ref, pseq_ref, kctx_ref, vctx_ref, cos_ref, sin_ref, yb_ref,
                    krope_ref, kc_ref, vc_ref):
    n = pl.program_id(1)
    hd = HEAD_DIM
    qb = WIN_QB
    band = WIN_BAND
    group = H_B // KV_B
    rot = _lane_xor_matrix(LANES, hd // 2)
    swap = _lane_xor_matrix(LANES, hd)

    @pl.when(n == 0)
    def _():
        k = pseq_ref[:, W_B:W_B + KV_W_B]
        krope_ref[...] = _rope(k, cos_ref[...], sin_ref[...], rot).astype(BF16)
        kc_ref[...] = jnp.concatenate([kctx_ref[0, 0, j] for j in range(KV_B)], axis=1).astype(BF16)
        vc_ref[...] = jnp.concatenate([vctx_ref[0, 0, j] for j in range(KV_B)], axis=1).astype(BF16)

    q_rows = pl.ds(pl.multiple_of(n * qb, qb), qb)
    cos_q = cos_ref[q_rows, :]
    sin_q = sin_ref[q_rows, :]
    low = lax.broadcasted_iota(I32, (qb, LANES), 1) < hd
    q_heads = []
    for p in range(H_B // 2):
        q_pair = _rope(pq_ref[:, p * LANES:(p + 1) * LANES], cos_q, sin_q, rot)
        for hh in (2 * p, 2 * p + 1):
            q = jnp.where(low, q_pair, 0.0) if hh % 2 == 0 else jnp.where(low, 0.0, q_pair)
            q = q.astype(BF16)
            q_heads.append(_dot(q, swap).astype(BF16) if hh % 2 != hh // group else q)

    ws = jnp.clip(n * qb - WINDOW, 0, DEC_SEQ - band)
    k_rows = pl.ds(pl.multiple_of(ws, WINDOW), band)
    q_pos = n * qb + lax.broadcasted_iota(I32, (group * qb, band), 0) % qb
    k_pos = ws + lax.broadcasted_iota(I32, (group * qb, band), 1)
    valid = jnp.abs(k_pos - q_pos) <= WINDOW
    head_of_row = lax.broadcasted_iota(I32, (group * qb, 1), 0) // qb
    kw = krope_ref[k_rows, :]
    vw = pseq_ref[k_rows, W_B + KV_W_B:W_B + 2 * KV_W_B]
    outs = []
    for j in range(KV_B):
        heads = [j * group + g for g in range(group)]
        qs = jnp.concatenate([q_heads[hh] for hh in heads], axis=0)
        s_loc = jnp.where(valid, _dot_nt(qs, kw) * ATTN_SCALE, NEG_INF)
        s_ctx = _dot_nt(qs, kc_ref[...]) * ATTN_SCALE
        sink = jnp.zeros((group * qb, 1), F32)
        for g, hh in enumerate(heads):
            sink = jnp.where(head_of_row == g, sink_ref[hh], sink)
        m = jnp.maximum(jnp.maximum(jnp.max(s_loc, -1, keepdims=True),
                                    jnp.max(s_ctx, -1, keepdims=True)), sink)
        p_loc = jnp.exp(s_loc - m)
        p_ctx = jnp.exp(s_ctx - m)
        den = (jnp.sum(p_loc, -1, keepdims=True) + jnp.sum(p_ctx, -1, keepdims=True)
               + jnp.exp(sink - m))
        o = ((_dot(p_loc.astype(BF16), vw) + _dot(p_ctx.astype(BF16), vc_ref[...])) / den).astype(BF16)
        for g, hh in enumerate(heads):
            o_h = o[g * qb:(g + 1) * qb]
            outs.append(_dot(o_h, swap).astype(BF16) if hh % 2 != j else o_h)
    for p in range(H_B // 2):
        yb_ref[:, p * LANES:(p + 1) * LANES] = jnp.where(low, outs[2 * p], outs[2 * p + 1])


def _lat_window_attn(pb, cache_k, cache_v, layer, sink, cos_t, sin_t):
    n_blk = DEC_SEQ // WIN_QB
    lat_blk0 = T_CTX // WIN_QB
    lat_seq0 = T_CTX // DEC_SEQ
    ctx_spec = pl.BlockSpec((1, 1, KV_B, PAST_LEN, HEAD_DIM), lambda b, n: (b, layer, 0, 0, 0))
    return pl.pallas_call(
        _lat_win_kernel,
        out_shape=jax.ShapeDtypeStruct((T_LAT, W_B), BF16),
        grid=(DEC_BATCH, n_blk),
        in_specs=[
            pl.BlockSpec(memory_space=pltpu.SMEM),
            pl.BlockSpec((WIN_QB, PB_W), lambda b, n: (lat_blk0 + b * n_blk + n, 0)),
            pl.BlockSpec((DEC_SEQ, PB_W), lambda b, n: (lat_seq0 + b, 0)),
            ctx_spec, ctx_spec,
            pl.BlockSpec((DEC_SEQ, LANES), lambda b, n: (0, 0)),
            pl.BlockSpec((DEC_SEQ, LANES), lambda b, n: (0, 0)),
        ],
        out_specs=pl.BlockSpec((WIN_QB, W_B), lambda b, n: (b * n_blk + n, 0)),
        scratch_shapes=[
            pltpu.VMEM((DEC_SEQ, KV_W_B), BF16),
            pltpu.VMEM((PAST_LEN, KV_W_B), BF16),
            pltpu.VMEM((PAST_LEN, KV_W_B), BF16),
        ],
        compiler_params=pltpu.CompilerParams(
            dimension_semantics=("arbitrary", "arbitrary"), vmem_limit_bytes=VMEM_LIMIT),
        name="lat_window_attn",
    )(sink, pb, pb, cache_k, cache_v, cos_t, sin_t)


NA_Q = NA_TILE_ROWS * GRID_W
NA_K = NA_KEY_ROWS * GRID_W
NA_TILES = DEC_SEQ // NA_Q
LAT_ROWS = DEC_SEQ // GRID_W


def _na_window_start(tile):
    return jnp.clip(tile * NA_TILE_ROWS - NA_ROWS // 2, 0, LAT_ROWS - NA_KEY_ROWS)


def _lat_na_kernel(pq_ref, pseq_ref, kctx_ref, vctx_ref, bias_ref, yc_ref):
    t = pl.program_id(1)
    hd = HEAD_DIM
    k_rows = pl.ds(pl.multiple_of(_na_window_start(t) * GRID_W, GRID_W), NA_K)
    for hh in range(H_C):
        q = pq_ref[:, hh * hd:(hh + 1) * hd]
        kw = pseq_ref[k_rows, W_C + hh * hd:W_C + (hh + 1) * hd]
        vw = pseq_ref[k_rows, 2 * W_C + hh * hd:2 * W_C + (hh + 1) * hd]
        kc = kctx_ref[0, 0, hh].astype(BF16)
        vc = vctx_ref[0, 0, hh].astype(BF16)
        s_loc = _dot_nt(q, kw) * ATTN_SCALE + bias_ref[0, hh]
        s_ctx = _dot_nt(q, kc) * ATTN_SCALE
        m = jnp.maximum(jnp.max(s_loc, -1, keepdims=True), jnp.max(s_ctx, -1, keepdims=True))
        p_loc = jnp.exp(s_loc - m)
        p_ctx = jnp.exp(s_ctx - m)
        den = jnp.sum(p_loc, -1, keepdims=True) + jnp.sum(p_ctx, -1, keepdims=True)
        o = (_dot(p_loc.astype(BF16), vw) + _dot(p_ctx.astype(BF16), vc)) / den
        yc_ref[:, hh * hd:(hh + 1) * hd] = o.astype(BF16)


def _na_tile_type(t):
    return jnp.where(t == 0, 0, jnp.where(t == NA_TILES - 1, 2, 1))


def _lat_na_attn(pc, cache_k, cache_v, layer, maskbias):
    lat_tile0 = T_CTX // NA_Q
    lat_seq0 = T_CTX // DEC_SEQ
    ctx_spec = pl.BlockSpec((1, 1, H_C, PAST_LEN, HEAD_DIM), lambda b, t: (b, layer, 0, 0, 0))
    return pl.pallas_call(
        _lat_na_kernel,
        out_shape=jax.ShapeDtypeStruct((T_LAT, W_C), BF16),
        grid=(DEC_BATCH, NA_TILES),
        in_specs=[
            pl.BlockSpec((NA_Q, PC_W), lambda b, t: (lat_tile0 + b * NA_TILES + t, 0)),
            pl.BlockSpec((DEC_SEQ, PC_W), lambda b, t: (lat_seq0 + b, 0)),
            ctx_spec, ctx_spec,
            pl.BlockSpec((1, H_C, NA_Q, NA_K), lambda b, t: (_na_tile_type(t), 0, 0, 0)),
        ],
        out_specs=pl.BlockSpec((NA_Q, W_C), lambda b, t: (b * NA_TILES + t, 0)),
        compiler_params=pltpu.CompilerParams(
            dimension_semantics=("arbitrary", "arbitrary"), vmem_limit_bytes=VMEM_LIMIT),
        name="lat_na_attn",
    )(pc, pc, cache_k, cache_v, maskbias)


def _na_block_index():
    out = np.zeros((3, NA_TILE_ROWS, NA_KEY_ROWS), np.int32)
    for ty, tile in enumerate((0, 1, NA_TILES - 1)):
        r = tile * NA_TILE_ROWS
        ws = int(np.clip(r - NA_ROWS // 2, 0, LAT_ROWS - NA_KEY_ROWS))
        for qq in range(NA_TILE_ROWS):
            qr = r + qq
            r0 = int(np.clip(qr - NA_ROWS // 2, 0, LAT_ROWS - NA_ROWS))
            for kk in range(NA_KEY_ROWS):
                kr = ws + kk
                out[ty, qq, kk] = kr - qr + NA_ROWS - 1 if r0 <= kr < r0 + NA_ROWS else 2 * NA_ROWS - 1
    return out


def _na_maskbias(rpb):
    qc = np.arange(GRID_W)[:, None]
    kc = np.arange(GRID_W)[None, :]
    c0 = np.clip(qc - NA_COLS // 2, 0, GRID_W - NA_COLS)
    col_ok = (kc >= c0) & (kc < c0 + NA_COLS)
    ci = np.clip(kc - qc + NA_COLS - 1, 0, 2 * NA_COLS - 2)
    onehot = (ci[None] == np.arange(2 * NA_COLS - 1)[:, None, None]).astype(np.float32)
    cols = jnp.einsum("hab,bqk->haqk", rpb, jnp.asarray(onehot), precision=lax.Precision.HIGHEST)
    cols = jnp.where(jnp.asarray(col_ok)[None, None], cols, NEG_INF)
    cols = jnp.concatenate([cols, jnp.full((H_C, 1, GRID_W, GRID_W), NEG_INF, F32)], axis=1)
    block_index = _na_block_index()

    def assemble(cols_ref, out_ref):
        for ty in range(3):
            for qq in range(NA_TILE_ROWS):
                for kk in range(NA_KEY_ROWS):
                    out_ref[ty, 0, qq * GRID_W:(qq + 1) * GRID_W, kk * GRID_W:(kk + 1) * GRID_W] = (
                        cols_ref[0, int(block_index[ty, qq, kk])])

    return pl.pallas_call(
        assemble,
        out_shape=jax.ShapeDtypeStruct((3, H_C, NA_Q, NA_K), F32),
        grid=(H_C,),
        in_specs=[pl.BlockSpec((1, 2 * NA_ROWS, GRID_W, GRID_W), lambda h: (h, 0, 0, 0))],
        out_specs=pl.BlockSpec((3, 1, NA_Q, NA_K), lambda h: (0, h, 0, 0)),
        compiler_params=pltpu.CompilerParams(dimension_semantics=("arbitrary",)),
        name="na_bias_assemble",
    )(cols)


def _rope_tables():
    t = np.arange(DEC_SEQ)
    n_freq = HEAD_DIM // 4
    inv = (ROPE_BASE ** (-np.arange(n_freq, dtype=np.float32) / n_freq)).astype(np.float32)
    row = (t // GRID_W).astype(np.float32)[:, None] * inv
    col = (t % GRID_W).astype(np.float32)[:, None] * inv
    ang = np.concatenate([row, col], -1)
    cos, sin = np.cos(ang), np.sin(ang)
    cos_h = np.concatenate([cos, cos], -1)
    sin_h = np.concatenate([-sin, sin], -1)
    reps = LANES // HEAD_DIM
    return (jnp.asarray(np.tile(cos_h, (1, reps)), F32), jnp.asarray(np.tile(sin_h, (1, reps)), F32))


def _first_index_of(mask, iota, sentinel):
    return jnp.min(jnp.where(mask, iota, sentinel), axis=0, keepdims=True)


def _route(logits, b_col):
    n = logits.shape[1]
    scores = jax.nn.sigmoid(logits)
    sel = scores + b_col
    io_g = lax.broadcasted_iota(I32, (GROUP_SIZE, n), 0)
    gs_rows = []
    for g in range(N_GROUPS):
        s = sel[g * GROUP_SIZE:(g + 1) * GROUP_SIZE]
        m1 = jnp.max(s, axis=0, keepdims=True)
        i1 = _first_index_of(s == m1, io_g, GROUP_SIZE)
        m2 = jnp.max(jnp.where(io_g == i1, PICKED, s), axis=0, keepdims=True)
        gs_rows.append(m1 + m2)
    gs = jnp.concatenate(gs_rows, axis=0)
    io_n = lax.broadcasted_iota(I32, (N_GROUPS, n), 0)
    gsel = jnp.zeros((N_GROUPS, n), F32)
    for _ in range(TOPK_GROUPS):
        mg = jnp.max(gs, axis=0, keepdims=True)
        gi = _first_index_of(gs == mg, io_n, N_GROUPS)
        hit = io_n == gi
        gsel = jnp.where(hit, 1.0, gsel)
        gs = jnp.where(hit, PICKED, gs)
    cand = jnp.concatenate(
        [jnp.where(gsel[g:g + 1] > 0.5, sel[g * GROUP_SIZE:(g + 1) * GROUP_SIZE], NEG_INF)
         for g in range(N_GROUPS)], axis=0)
    io_e = lax.broadcasted_iota(I32, (N_EXPERTS, n), 0)
    picks, raw = [], []
    for _ in range(TOP_K):
        mv = jnp.max(cand, axis=0, keepdims=True)
        ei = _first_index_of(cand == mv, io_e, N_EXPERTS)
        hit = io_e == ei
        picks.append((hit, ei))
        raw.append(jnp.sum(jnp.where(hit, scores, 0.0), axis=0, keepdims=True))
        cand = jnp.where(hit, PICKED, cand)
    return picks, raw


def _post_mixer_kernel(xc_ref, xl_ref, mixc_ref, ya_ref, yb_ref, yc_ref, wout_ref, mod_ref, g_ref, b_ref,
                       wr_ref, br_ref,
                       x1_ref, h2_ref, eidx_ref, wsel_ref, rank_ref, cnt_ref):
    i = pl.program_id(0)
    tm = TM_TOK

    @pl.when(i == 0)
    def _():
        cnt_ref[...] = jnp.zeros_like(cnt_ref)

    ci = _cond_row(i, tm)
    gate1 = mod_ref[pl.ds(ci, 1), 2 * D_MODEL:3 * D_MODEL]
    sh2 = mod_ref[pl.ds(ci, 1), 3 * D_MODEL:4 * D_MODEL]
    sc2 = mod_ref[pl.ds(ci, 1), 4 * D_MODEL:5 * D_MODEL]
    mix_lat = jnp.concatenate([ya_ref[...], yb_ref[...], yc_ref[...]], axis=-1)
    mix = jnp.where(i < T_CTX // tm, mixc_ref[...], mix_lat)
    y = _dot(mix, wout_ref[...])
    x = jnp.where(i < T_CTX // tm, xc_ref[...], xl_ref[...])
    x1 = _layer_norm(ALPHA * x + gate1 * y, g_ref[...], b_ref[...])
    x1_ref[...] = x1
    h2 = x1 * (1.0 + sc2) + sh2
    h_hi = h2.astype(BF16)
    h2_ref[...] = h_hi
    h_lo = (h2 - h_hi.astype(F32)).astype(BF16)
    both = (_dot(h_hi, wr_ref[...]) + _dot(h_lo, wr_ref[...])).T
    logits = both[0:N_EXPERTS] + both[N_EXPERTS:2 * N_EXPERTS]
    routed = [_route(logits[:, g * LANES:(g + 1) * LANES], br_ref[...]) for g in range(tm // LANES)]
    multi_g = []
    for picks, _ in routed:
        m = jnp.zeros((N_EXPERTS, LANES), F32)
        for hit, _ in picks:
            m = m + jnp.where(hit, 1.0, 0.0)
        multi_g.append(m)
    multi = jnp.concatenate(multi_g, axis=1)
    before = (lax.broadcasted_iota(I32, (tm, tm), 0) < lax.broadcasted_iota(I32, (tm, tm), 1))
    cum = _dot(multi.astype(BF16), jnp.where(before, 1.0, 0.0).astype(BF16))
    pad = jnp.zeros((SUBLANES - TOP_K, LANES), F32)
    for g, (picks, raw) in enumerate(routed):
        lanes = slice(g * LANES, (g + 1) * LANES)
        total = raw[0]
        for r in raw[1:]:
            total = total + r
        scale = ROUTE_SCALE / total
        cum_g = cum[:, lanes]
        eidx_ref[:, lanes] = jnp.concatenate([ei for _, ei in picks] + [pad.astype(I32)], axis=0)
        wsel_ref[:, lanes] = jnp.concatenate([r * scale for r in raw] + [pad], axis=0)
        rank_ref[:, lanes] = jnp.concatenate(
            [jnp.sum(jnp.where(hit, cum_g, 0.0), axis=0, keepdims=True) for hit, _ in picks] + [pad],
            axis=0).astype(I32)
    tile_lane = lax.broadcasted_iota(I32, (N_EXPERTS, LANES), 1)
    cnt_ref[...] = jnp.where(tile_lane == i, jnp.sum(multi, axis=1, keepdims=True), cnt_ref[...])


def _post_mixer(x_ctx, x_lat, mix_c, ya, yb, yc, w_out_bf16, mod, ln_g, ln_b, wr_split, b_router_col):
    n_ctx = T_CTX // TM_TOK
    ctx_map = lambda i: (jnp.minimum(i, n_ctx - 1), 0)
    lat_map = lambda i: (jnp.maximum(i - n_ctx, 0), 0)
    row_map = lambda i: (i, 0)
    const = lambda i: (0, 0)
    tok_map = lambda i: (0, i)
    return pl.pallas_call(
        _post_mixer_kernel,
        out_shape=(
            jax.ShapeDtypeStruct((T_ALL, D_MODEL), F32),
            jax.ShapeDtypeStruct((T_ALL, D_MODEL), BF16),
            jax.ShapeDtypeStruct((SUBLANES, T_ALL), I32),
            jax.ShapeDtypeStruct((SUBLANES, T_ALL), F32),
            jax.ShapeDtypeStruct((SUBLANES, T_ALL), I32),
            jax.ShapeDtypeStruct((N_EXPERTS, LANES), F32),
        ),
        grid=(N_TOK_TILES,),
        in_specs=[
            pl.BlockSpec((TM_TOK, D_MODEL), ctx_map),
            pl.BlockSpec((TM_TOK, D_MODEL), lat_map),
            pl.BlockSpec((TM_TOK, D_MODEL), ctx_map),
            pl.BlockSpec((TM_TOK, W_A), lat_map),
            pl.BlockSpec((TM_TOK, W_B), lat_map),
            pl.BlockSpec((TM_TOK, W_C), lat_map),
            pl.BlockSpec((D_MODEL, D_MODEL), const),
            pl.BlockSpec((N_COND, 6 * D_MODEL), const),
            pl.BlockSpec((1, D_MODEL), const),
            pl.BlockSpec((1, D_MODEL), const),
            pl.BlockSpec((D_MODEL, 2 * N_EXPERTS), const),
            pl.BlockSpec((N_EXPERTS, 1), const),
        ],
        out_specs=(
            pl.BlockSpec((TM_TOK, D_MODEL), row_map),
            pl.BlockSpec((TM_TOK, D_MODEL), row_map),
            pl.BlockSpec((SUBLANES, TM_TOK), tok_map),
            pl.BlockSpec((SUBLANES, TM_TOK), tok_map),
            pl.BlockSpec((SUBLANES, TM_TOK), tok_map),
            pl.BlockSpec((N_EXPERTS, LANES), const),
        ),
        compiler_params=pltpu.CompilerParams(
            dimension_semantics=("arbitrary",), vmem_limit_bytes=VMEM_LIMIT),
        name="post_mixer",
    )(x_ctx, x_lat, mix_c, ya, yb, yc, w_out_bf16, mod, ln_g, ln_b, wr_split, b_router_col)


PLAN_TILES = 4


def _plan_kernel(eidx_ref, rank_ref, nmat_ref, lslot_ref, unit_ref, gend_ref):
    step = pl.program_id(0)
    units = jnp.floor((nmat_ref[...] + (UNIT - 1.0)) * (1.0 / UNIT))
    units_bf = units.astype(BF16)
    earlier_e = (lax.broadcasted_iota(I32, (N_EXPERTS, N_EXPERTS), 1)
                 < lax.broadcasted_iota(I32, (N_EXPERTS, N_EXPERTS), 0))
    tri_e = jnp.where(earlier_e, 1.0, 0.0).astype(BF16)
    earlier_t = (lax.broadcasted_iota(I32, (LANES, LANES), 0) < lax.broadcasted_iota(I32, (LANES, LANES), 1))
    tri_t = jnp.where(earlier_t, 1.0, 0.0).astype(BF16)
    local_off = _dot(tri_e, units_bf)
    tile_off = _dot(units_bf, tri_t)
    per_expert = jnp.sum(units, axis=1, keepdims=True)
    blocks = jnp.floor((per_expert + (UNITS_PER_BLOCK - 1.0)) * (1.0 / UNITS_PER_BLOCK))
    blocks_l = jnp.broadcast_to(blocks, (N_EXPERTS, LANES))
    start_blk = _dot(tri_e, blocks_l.astype(BF16))
    end_blk = start_blk + blocks_l
    gend_ref[...] = (end_blk * BM).astype(I32)

    tile_lane = lax.broadcasted_iota(I32, (N_EXPERTS, LANES), 1)
    u = lax.broadcasted_iota(I32, (N_EXPERTS, MAX_UNITS), 1).astype(F32)
    io_e = lax.broadcasted_iota(I32, (N_EXPERTS, TM_TOK), 0)
    for s in range(PLAN_TILES):
        i = step * PLAN_TILES + s
        this_tile = tile_lane == i

        def column(a):
            return jnp.sum(jnp.where(this_tile, a, 0.0), axis=1, keepdims=True)

        lo, n_u = column(local_off), column(units)
        base_unit = start_blk[:, 0:1] * UNITS_PER_BLOCK + column(tile_off) - lo
        inside = jnp.where(u >= lo, jnp.where(u < lo + n_u, 1.0, 0.0), 0.0)
        dst_unit = jnp.sum(inside * (base_unit + u), axis=0, keepdims=True)
        used = jnp.sum(inside, axis=0, keepdims=True) > 0.5
        spare = (SPARE_UNIT0 + (i % 2) * MAX_UNITS).astype(F32) + u[0:1, :]
        unit_ref[s] = jnp.where(used, dst_unit, spare).astype(I32)

        toks = slice(s * TM_TOK, (s + 1) * TM_TOK)
        rows = []
        for k in range(TOP_K):
            hit = io_e == eidx_ref[k:k + 1, toks]
            seg = jnp.sum(jnp.where(hit, lo * UNIT, 0.0), axis=0, keepdims=True)
            rows.append(seg.astype(I32) + rank_ref[k:k + 1, toks])
        rows.append(jnp.full((SUBLANES - TOP_K, TM_TOK), -1, I32))
        lslot_ref[:, toks] = jnp.concatenate(rows, axis=0)


def _slot_plan(eidx, rank, nmat):
    tok_map = lambda i: (0, i)
    const = lambda i: (0, 0)
    lslot, unit_tab, gend = pl.pallas_call(
        _plan_kernel,
        out_shape=(
            jax.ShapeDtypeStruct((SUBLANES, T_ALL), I32),
            jax.ShapeDtypeStruct((N_TOK_TILES, 1, MAX_UNITS), I32),
            jax.ShapeDtypeStruct((N_EXPERTS, LANES), I32),
        ),
        grid=(N_TOK_TILES // PLAN_TILES,),
        in_specs=[
            pl.BlockSpec((SUBLANES, PLAN_TILES * TM_TOK), tok_map),
            pl.BlockSpec((SUBLANES, PLAN_TILES * TM_TOK), tok_map),
            pl.BlockSpec((N_EXPERTS, LANES), const),
        ],
        out_specs=(
            pl.BlockSpec((SUBLANES, PLAN_TILES * TM_TOK), tok_map),
            pl.BlockSpec((PLAN_TILES, 1, MAX_UNITS), lambda i: (i, 0, 0)),
            pl.BlockSpec((N_EXPERTS, LANES), const),
        ),
        compiler_params=pltpu.CompilerParams(dimension_semantics=("arbitrary",)),
        name="slot_plan",
    )(eidx, rank, nmat)
    return lslot, unit_tab.reshape(N_TOK_TILES * MAX_UNITS), gend[:, 0]


PACK_W = D_MODEL // 2
HI_HALF = -65536


def _pack_pairs(x):
    lo = lax.bitcast_convert_type(x[:, 0:PACK_W], I32)
    hi = lax.bitcast_convert_type(x[:, PACK_W:D_MODEL], I32)
    return lax.shift_right_logical(lo, 16) | (hi & HI_HALF)


def _unpack_pairs(u):
    lo = lax.bitcast_convert_type(lax.shift_left(u, 16), F32).astype(BF16)
    hi = lax.bitcast_convert_type(u & HI_HALF, F32).astype(BF16)
    return lo, hi


def _unit_rows(unit):
    row = unit * UNIT
    return pl.ds(row if isinstance(unit, int) else pl.multiple_of(row, UNIT), UNIT)


def _unit_copy(src, src_unit, dst, dst_unit, sem):
    return pltpu.make_async_copy(src.at[_unit_rows(src_unit)], dst.at[_unit_rows(dst_unit)], sem)


def _dispatch_kernel(gend_ref, tab_ref, h2_ref, lslot_ref, xs_hbm, zero_ref, local_ref, sem_zero, sem_rows):
    i = pl.program_id(0)
    buf = i % 2

    def drain(b):
        pltpu.make_async_copy(local_ref.at[b], xs_hbm.at[pl.ds(0, LOCAL_ROWS)], sem_rows.at[b]).wait()

    def has_rows(e):
        return gend_ref[e] > jnp.where(e == 0, 0, gend_ref[jnp.maximum(e - 1, 0)])

    def zero_copy(e):
        return pltpu.make_async_copy(
            zero_ref, xs_hbm.at[pl.ds(pl.multiple_of(gend_ref[e] - BM, BM), BM)], sem_zero)

    @pl.when(i == 0)
    def _():
        zero_ref[...] = jnp.zeros_like(zero_ref)

        def start(e, c):
            @pl.when(has_rows(e))
            def _():
                zero_copy(e).start()
            return c

        def wait(e, c):
            @pl.when(has_rows(e))
            def _():
                zero_copy(e).wait()
            return c

        def tail_copy(blk):
            return pltpu.make_async_copy(
                zero_ref, xs_hbm.at[pl.ds(pl.multiple_of(blk * BM, BM), BM)], sem_zero)

        def start_tail(blk, c):
            tail_copy(blk).start()
            return c

        def wait_tail(blk, c):
            tail_copy(blk).wait()
            return c

        n_used = gend_ref[N_EXPERTS - 1] // BM
        lax.fori_loop(0, N_EXPERTS, start, 0)
        lax.fori_loop(n_used, N_BLOCKS_ALL, start_tail, 0)
        lax.fori_loop(0, N_EXPERTS, wait, 0)
        lax.fori_loop(n_used, N_BLOCKS_ALL, wait_tail, 0)

    @pl.when(i >= 2)
    def _():
        drain(buf)

    h2 = h2_ref[...]
    units_per_chunk = PERM_CHUNK // UNIT
    local = local_ref.at[buf]
    for c in range(LOCAL_ROWS // PERM_CHUNK):
        slot = c * PERM_CHUNK + lax.broadcasted_iota(I32, (PERM_CHUNK, TM_TOK), 0)
        p = jnp.zeros((PERM_CHUNK, TM_TOK), F32)
        for k in range(TOP_K):
            p = jnp.where(slot == lslot_ref[k:k + 1, :], 1.0, p)
        local[c * PERM_CHUNK:(c + 1) * PERM_CHUNK, :] = _pack_pairs(_dot(p.astype(BF16), h2))
        for u in range(c * units_per_chunk, (c + 1) * units_per_chunk):
            _unit_copy(local, u, xs_hbm, tab_ref[i * MAX_UNITS + u], sem_rows.at[buf]).start()

    @pl.when(i == N_TOK_TILES - 1)
    def _():
        drain(1 - buf)
        drain(buf)


def _dispatch(h2, lslot, unit_tab, gend):
    return pl.pallas_call(
        _dispatch_kernel,
        out_shape=jax.ShapeDtypeStruct((N_SLOTS, PACK_W), I32),
        grid_spec=pltpu.PrefetchScalarGridSpec(
            num_scalar_prefetch=2,
            grid=(N_TOK_TILES,),
            in_specs=[
                pl.BlockSpec((TM_TOK, D_MODEL), lambda i, ge, tab: (i, 0)),
                pl.BlockSpec((SUBLANES, TM_TOK), lambda i, ge, tab: (0, i)),
            ],
            out_specs=pl.BlockSpec(memory_space=pl.ANY),
            scratch_shapes=[
                pltpu.VMEM((BM, PACK_W), I32),
                pltpu.VMEM((2, LOCAL_ROWS, PACK_W), I32),
                pltpu.SemaphoreType.DMA,
                pltpu.SemaphoreType.DMA((2,)),
            ],
        ),
        compiler_params=pltpu.CompilerParams(
            dimension_semantics=("arbitrary",), vmem_limit_bytes=VMEM_LIMIT),
        name="moe_dispatch",
    )(gend, unit_tab, h2, lslot)


def _expert_kernel(gend_ref, xs_hbm, wgu_ref, wdown_ref, ys_hbm, xbuf, ybuf, wgu_bf, wdown_bf, sem_in, sem_out):
    e = pl.program_id(0)
    first = jnp.where(e == 0, 0, gend_ref[jnp.maximum(e - 1, 0)]) // BM
    last = gend_ref[e] // BM
    n_used = gend_ref[N_EXPERTS - 1] // BM

    def rows_of(blk):
        return pl.ds(pl.multiple_of(blk * BM, BM), BM)

    def in_copy(blk):
        return pltpu.make_async_copy(xs_hbm.at[rows_of(blk)], xbuf.at[blk % X_BUFS], sem_in.at[blk % X_BUFS])

    def out_copy(blk):
        return pltpu.make_async_copy(ybuf.at[blk % 2], ys_hbm.at[rows_of(blk)], sem_out.at[blk % 2])

    @pl.when(e == 0)
    def _():
        for a in range(X_AHEAD):
            @pl.when(a < n_used)
            def _(a=a):
                in_copy(a).start()

    @pl.when(last > first)
    def _():
        wgu_bf[...] = wgu_ref[0, 0].astype(BF16)
        wdown_bf[...] = wdown_ref[0, 0].astype(BF16)

    def block(blk, carry):
        @pl.when(blk + X_AHEAD < n_used)
        def _():
            in_copy(blk + X_AHEAD).start()

        in_copy(blk).wait()

        @pl.when(blk >= 2)
        def _():
            out_copy(blk - 2).wait()

        x_lo, x_hi = _unpack_pairs(xbuf[blk % X_BUFS])
        gu = _dot(x_lo, wgu_bf[0:PACK_W, :]) + _dot(x_hi, wgu_bf[PACK_W:D_MODEL, :])
        act = _silu(gu[:, 0:D_EXPERT]) * gu[:, D_EXPERT:2 * D_EXPERT]
        y = _dot(act.astype(BF16), wdown_bf[...])
        ybuf[blk % 2] = _pack_pairs(y.astype(BF16).astype(F32))
        out_copy(blk).start()
        return carry

    lax.fori_loop(first, last, block, 0)

    @pl.when(e == N_EXPERTS - 1)
    def _():
        @pl.when(n_used >= 2)
        def _():
            out_copy(n_used - 2).wait()

        @pl.when(n_used >= 1)
        def _():
            out_copy(n_used - 1).wait()

        ybuf[0] = jnp.zeros((BM, PACK_W), I32)

        def tail_copy(blk):
            return pltpu.make_async_copy(ybuf.at[0], ys_hbm.at[rows_of(blk)], sem_out.at[0])

        def start(blk, c):
            tail_copy(blk).start()
            return c

        def wait(blk, c):
            tail_copy(blk).wait()
            return c

        lax.fori_loop(n_used, N_BLOCKS_ALL, start, 0)
        lax.fori_loop(n_used, N_BLOCKS_ALL, wait, 0)


def _experts(xs, w_gu, w_down, layer, gend):
    return pl.pallas_call(
        _expert_kernel,
        out_shape=jax.ShapeDtypeStruct((N_SLOTS, PACK_W), I32),
        grid_spec=pltpu.PrefetchScalarGridSpec(
            num_scalar_prefetch=1,
            grid=(N_EXPERTS,),
            in_specs=[
                pl.BlockSpec(memory_space=pl.ANY),
                pl.BlockSpec((1, 1, D_MODEL, 2 * D_EXPERT), lambda e, ge: (layer, e, 0, 0)),
                pl.BlockSpec((1, 1, D_EXPERT, D_MODEL), lambda e, ge: (layer, e, 0, 0)),
            ],
            out_specs=pl.BlockSpec(memory_space=pl.ANY),
            scratch_shapes=[
                pltpu.VMEM((X_BUFS, BM, PACK_W), I32),
                pltpu.VMEM((2, BM, PACK_W), I32),
                pltpu.VMEM((D_MODEL, 2 * D_EXPERT), BF16),
                pltpu.VMEM((D_EXPERT, D_MODEL), BF16),
                pltpu.SemaphoreType.DMA((X_BUFS,)),
                pltpu.SemaphoreType.DMA((2,)),
            ],
        ),
        compiler_params=pltpu.CompilerParams(
            dimension_semantics=("arbitrary",), vmem_limit_bytes=VMEM_LIMIT),
        name="moe_experts",
    )(gend, xs, w_gu, w_down)


def _combine_kernel(tab_ref, x1_ref, h2_ref, lslot_ref, wsel_ref, ys_hbm, wsgu_ref, wsdown_ref, mod_ref,
                    g_ref, b_ref, outc_ref, outl_ref, local_ref, sel_ref, ylo_ref, yhi_ref, sem_rows):
    i = pl.program_id(0)
    tm = TM_TOK
    buf = i % 2

    def fetch_unit(tile, b, u):
        _unit_copy(ys_hbm, tab_ref[tile * MAX_UNITS + u], local_ref.at[b], u, sem_rows.at[b]).start()

    def drain(b):
        pltpu.make_async_copy(ys_hbm.at[pl.ds(0, LOCAL_ROWS)], local_ref.at[b], sem_rows.at[b]).wait()

    @pl.when(i == 0)
    def _():
        def body(u, c):
            fetch_unit(0, 0, u)
            return c

        lax.fori_loop(0, MAX_UNITS, body, 0, unroll=8)

    sgu = _dot(h2_ref[...], wsgu_ref[...])
    act = _silu(sgu[:, 0:D_SHARED]) * sgu[:, D_SHARED:2 * D_SHARED]
    f = _dot(act.astype(BF16), wsdown_ref[...])

    nxt = jnp.minimum(i + 1, N_TOK_TILES - 1)
    n_groups = tm // SEL_ROWS
    units_per_group = MAX_UNITS // n_groups
    slot = lax.broadcasted_iota(I32, (SEL_ROWS, LOCAL_ROWS), 1)
    for g in range(n_groups):
        rows = slice(g * SEL_ROWS, (g + 1) * SEL_ROWS)
        sel = jnp.zeros((SEL_ROWS, LOCAL_ROWS), F32)
        for k in range(TOP_K):
            sel = jnp.where(slot == lslot_ref[rows, k:k + 1], wsel_ref[rows, k:k + 1], sel)
        sel_ref[rows, :] = sel.astype(BF16)
        for u in range(g * units_per_group, (g + 1) * units_per_group):
            fetch_unit(nxt, 1 - buf, u)

    local = local_ref.at[buf]
    drain(buf)
    for c in range(LOCAL_ROWS // PERM_CHUNK):
        rows = slice(c * PERM_CHUNK, (c + 1) * PERM_CHUNK)
        ylo_ref[rows, :], yhi_ref[rows, :] = _unpack_pairs(local[rows, :])
    sel = sel_ref[...]
    f = f + jnp.concatenate([_dot(sel, ylo_ref[...]), _dot(sel, yhi_ref[...])], axis=-1)
    ci = _cond_row(i, tm)
    gate2 = mod_ref[pl.ds(ci, 1), 5 * D_MODEL:6 * D_MODEL]
    out = _layer_norm(ALPHA * x1_ref[...] + gate2 * f, g_ref[...], b_ref[...])

    @pl.when(i < T_CTX // tm)
    def _():
        outc_ref[...] = out

    @pl.when(i >= T_CTX // tm)
    def _():
        outl_ref[...] = out

    @pl.when(i == N_TOK_TILES - 1)
    def _():
        drain(1 - buf)


def _combine(x1, h2, lslot_rows, wsel_rows, unit_tab, ys, w_sgu_bf16, w_sdown_bf16, mod, ln_g, ln_b):
    n_ctx = T_CTX // TM_TOK
    row_map = lambda i, tab: (i, 0)
    const = lambda i, tab: (0, 0)
    return pl.pallas_call(
        _combine_kernel,
        out_shape=(jax.ShapeDtypeStruct((T_CTX, D_MODEL), F32),
                   jax.ShapeDtypeStruct((T_LAT, D_MODEL), F32)),
        grid_spec=pltpu.PrefetchScalarGridSpec(
            num_scalar_prefetch=1,
            grid=(N_TOK_TILES,),
            in_specs=[
                pl.BlockSpec((TM_TOK, D_MODEL), row_map),
                pl.BlockSpec((TM_TOK, D_MODEL), row_map),
                pl.BlockSpec((TM_TOK, SUBLANES), row_map),
                pl.BlockSpec((TM_TOK, SUBLANES), row_map),
                pl.BlockSpec(memory_space=pl.ANY),
                pl.BlockSpec((D_MODEL, 2 * D_SHARED), const),
                pl.BlockSpec((D_SHARED, D_MODEL), const),
                pl.BlockSpec((N_COND, 6 * D_MODEL), const),
                pl.BlockSpec((1, D_MODEL), const),
                pl.BlockSpec((1, D_MODEL), const),
            ],
            out_specs=(pl.BlockSpec((TM_TOK, D_MODEL), lambda i, tab: (jnp.minimum(i, n_ctx - 1), 0)),
                       pl.BlockSpec((TM_TOK, D_MODEL), lambda i, tab: (jnp.maximum(i - n_ctx, 0), 0))),
            scratch_shapes=[
                pltpu.VMEM((2, LOCAL_ROWS, PACK_W), I32),
                pltpu.VMEM((TM_TOK, LOCAL_ROWS), BF16),
                pltpu.VMEM((LOCAL_ROWS, PACK_W), BF16),
                pltpu.VMEM((LOCAL_ROWS, PACK_W), BF16),
                pltpu.SemaphoreType.DMA((2,)),
            ],
        ),
        compiler_params=pltpu.CompilerParams(
            dimension_semantics=("arbitrary",), vmem_limit_bytes=VMEM_LIMIT),
        name="moe_combine",
    )(unit_tab, x1, h2, lslot_rows, wsel_rows, ys, w_sgu_bf16, w_sdown_bf16, mod, ln_g, ln_b)


def _lane_rows(v, width):
    return jnp.broadcast_to(v.astype(F32)[:, None, None], (v.shape[0], 1, width))


def kernel(x_prompt, x_sample, state_ret_fwd, state_ret_bwd, cache_win_k, cache_win_v, cache_na_k, cache_na_v, c, c_ctx, w_in, w_out, ret_decay_fwd, ret_decay_bwd, ret_gn_g, ret_gn_b, win_sink, na_rpb, w_mod, b_mod, ln1_g, ln1_b, ln2_g, ln2_b, w_router, b_router, w_expert_gu, w_expert_down, w_shared_gu, w_shared_down):
    cond = jnp.concatenate(
        [c_ctx[None, :], c, jnp.zeros((N_COND - 1 - DEC_BATCH, D_MODEL), F32)], axis=0)
    mod_all = _modulation(cond, w_mod, b_mod)
    cos_t, sin_t = _rope_tables()

    x_ctx = x_prompt.reshape(T_CTX, D_MODEL)
    x_lat = x_sample.reshape(T_LAT, D_MODEL)
    sf_l, sb_l, caches = [], [], ()
    for l in range(DEPTH):
        mod = mod_all[l]
        pa, pb, pc, *caches = _in_projection(x_ctx, x_lat, mod, w_in[l].astype(BF16), tuple(caches))
        decf_s, decb_s = _lane_rows(ret_decay_fwd[l], SEQ), _lane_rows(ret_decay_bwd[l], SEQ)
        gng, gnb = ret_gn_g[l][None, :], ret_gn_b[l][None, :]
        mix_c, st_f, st_b = _ctx_mixers(pa, pb, pc, win_sink[l], decf_s, decb_s, gng, gnb)
        sf_l.append(st_f)
        sb_l.append(st_b)
        ya = _lat_retention(pa, state_ret_fwd, state_ret_bwd, l,
                            _lane_rows(ret_decay_fwd[l], RET_CHUNK), _lane_rows(ret_decay_bwd[l], RET_CHUNK),
                            gng, gnb)
        yb = _lat_window_attn(pb, cache_win_k, cache_win_v, l, win_sink[l], cos_t, sin_t)
        yc = _lat_na_attn(pc, cache_na_k, cache_na_v, l, _na_maskbias(na_rpb[l]))

        wr_hi = w_router[l].astype(BF16)
        wr_lo = (w_router[l] - wr_hi.astype(F32)).astype(BF16)
        x1, h2, eidx, wsel, rank, counts = _post_mixer(
            x_ctx, x_lat, mix_c, ya, yb, yc, w_out[l].astype(BF16), mod, ln1_g[l][None, :],
            ln1_b[l][None, :], jnp.concatenate([wr_hi, wr_lo], axis=1), b_router[l][:, None])
        lslot, unit_tab, gend = _slot_plan(eidx, rank, counts)
        xs = _dispatch(h2, lslot, unit_tab, gend)
        ys = _experts(xs, w_expert_gu, w_expert_down, l, gend)
        x_ctx, x_lat = _combine(x1, h2, lslot.T, wsel.T, unit_tab, ys, w_shared_gu[l].astype(BF16),
                                w_shared_down[l].astype(BF16), mod, ln2_g[l][None, :], ln2_b[l][None, :])

    y_prompt = x_ctx.reshape(BATCH, SEQ, D_MODEL)
    y_sample = x_lat.reshape(DEC_BATCH, DEC_SEQ, D_MODEL)
    new_sf = jnp.stack(sf_l, axis=1)
    new_sb = jnp.stack(sb_l, axis=1)

    return (y_prompt, y_sample, new_sf, new_sb, *caches)
```

```python
import functools

import numpy as np
import jax
import jax.numpy as jnp
from jax import lax
from jax.experimental import pallas as pl
from jax.experimental.pallas import tpu as pltpu

F32 = jnp.float32
BF16 = jnp.bfloat16
I32 = jnp.int32

D_MODEL = 1024
BATCH = 32
SEQ = 256
DEPTH = 2
DEC_BATCH = 4
DEC_SEQ = 2048
PAST_LEN = 256
GRID_W = 64
HEAD_DIM = 64
ATTN_SCALE = HEAD_DIM ** -0.5
H_A = 4
W_A = H_A * HEAD_DIM
GN_EPS = 1e-5
H_B = 6
KV_B = 2
W_B = H_B * HEAD_DIM
KV_W_B = KV_B * HEAD_DIM
WINDOW = 128
ROPE_BASE = 10000.0
H_C = 6
W_C = H_C * HEAD_DIM
NA_ROWS = 8
NA_COLS = 16
IN_WIDTH = 4 * W_A + W_B + 2 * KV_W_B + 3 * W_C
N_EXPERTS = 64
TOP_K = 6
N_GROUPS = 8
GROUP_SIZE = N_EXPERTS // N_GROUPS
TOPK_GROUPS = 4
D_EXPERT = 256
D_SHARED = 256
ROUTE_SCALE = 2.5
ALPHA = (2 * DEPTH) ** 0.25
LN_EPS = 1e-5
NEG_INF = -1e30
PICKED = -3e38

T_CTX = BATCH * SEQ
T_LAT = DEC_BATCH * DEC_SEQ
T_ALL = T_CTX + T_LAT
N_COND = 8

PA_W = 4 * W_A
PB_W = W_B + 2 * KV_W_B
PC_W = 3 * W_C

LANES = 128
SUBLANES = 8
VMEM_LIMIT = 56 * 1024 * 1024

TM_PROJ = 512
TM_TOK = 256
RET_CHUNK = 256
WIN_QB = 256
WIN_BAND = WIN_QB + 2 * WINDOW
NA_TILE_ROWS = 4
NA_KEY_ROWS = 11
BM = 512
X_AHEAD = 3
X_BUFS = X_AHEAD + 1
UNIT = SUBLANES
UNITS_PER_BLOCK = BM // UNIT
PERM_CHUNK = 256
SEL_ROWS = 16
N_TOK_TILES = T_ALL // TM_TOK
LOCAL_ROWS = -(-(TM_TOK * TOP_K + N_EXPERTS * (UNIT - 1)) // PERM_CHUNK) * PERM_CHUNK
MAX_UNITS = LOCAL_ROWS // UNIT
N_ASSIGN = T_ALL * TOP_K
N_BLOCKS = -(-(N_ASSIGN + N_TOK_TILES * N_EXPERTS * (UNIT - 1) + N_EXPERTS * (BM - 1)) // BM)
SPARE_BLOCKS = 2 * -(-LOCAL_ROWS // BM)
SPARE_UNIT0 = N_BLOCKS * UNITS_PER_BLOCK
N_BLOCKS_ALL = N_BLOCKS + SPARE_BLOCKS
N_SLOTS = N_BLOCKS_ALL * BM


def _dot(a, b):
    return jnp.dot(a, b, preferred_element_type=F32)


def _dot_nt(a, b):
    return lax.dot_general(a, b, (((1,), (1,)), ((), ())), preferred_element_type=F32)


def _silu(x):
    return x * jax.nn.sigmoid(x)


def _log_sigmoid(x):
    return jnp.minimum(x, 0.0) - jnp.log(1.0 + jnp.exp(-jnp.abs(x)))


def _cond_row(tile, tile_rows):
    n_ctx = T_CTX // tile_rows
    per_lat = DEC_SEQ // tile_rows
    return jnp.where(tile < n_ctx, 0, 1 + (tile - n_ctx) // per_lat)


def _layer_norm(x, g, b):
    mu = jnp.mean(x, -1, keepdims=True)
    xc = x - mu
    var = jnp.mean(xc * xc, -1, keepdims=True)
    return xc * lax.rsqrt(var + LN_EPS) * g + b


MOD_TN = 1536


def _mod_kernel(cond_ref, w_ref, b_ref, o_ref):
    s = _silu(cond_ref[...])
    s_hi = s.astype(BF16)
    s_lo = (s - s_hi.astype(F32)).astype(BF16)
    w = w_ref[0]
    w_hi = w.astype(BF16)
    w_lo = (w - w_hi.astype(F32)).astype(BF16)
    o_ref[0] = _dot(s_hi, w_hi) + _dot(s_lo, w_hi) + _dot(s_hi, w_lo) + b_ref[0]


def _modulation(cond, w_mod, b_mod):
    n_out = 6 * D_MODEL
    return pl.pallas_call(
        _mod_kernel,
        out_shape=jax.ShapeDtypeStruct((DEPTH, N_COND, n_out), F32),
        grid=(DEPTH, n_out // MOD_TN),
        in_specs=[
            pl.BlockSpec((N_COND, D_MODEL), lambda l, j: (0, 0)),
            pl.BlockSpec((1, D_MODEL, MOD_TN), lambda l, j: (l, 0, j)),
            pl.BlockSpec((1, 1, MOD_TN), lambda l, j: (l, 0, j)),
        ],
        out_specs=pl.BlockSpec((1, N_COND, MOD_TN), lambda l, j: (l, 0, j)),
        compiler_params=pltpu.CompilerParams(
            dimension_semantics=("arbitrary", "arbitrary"), vmem_limit_bytes=VMEM_LIMIT),
        name="modulation",
    )(cond, w_mod, b_mod.reshape(DEPTH, 1, n_out))


SEQ_PER_PROJ = TM_PROJ // SEQ


def _inproj_kernel(n_prev, xc_ref, xl_ref, mod_ref, w_ref, *refs):
    prev_refs = refs[:4] if n_prev else ()
    pa_ref, pb_ref, pc_ref, wk_ref, wv_ref, nk_ref, nv_ref = refs[len(prev_refs):]
    i = pl.program_id(0)
    ci = _cond_row(i, TM_PROJ)
    sh = mod_ref[pl.ds(ci, 1), 0:D_MODEL]
    sc = mod_ref[pl.ds(ci, 1), D_MODEL:2 * D_MODEL]
    x = jnp.where(i < T_CTX // TM_PROJ, xc_ref[...], xl_ref[...])
    h = x * (1.0 + sc) + sh
    p = _dot(h.astype(BF16), w_ref[...])
    pa_ref[...] = p[:, 0:PA_W].astype(BF16)
    pb_ref[...] = p[:, PA_W:PA_W + PB_W].astype(BF16)
    pc_ref[...] = p[:, PA_W + PB_W:IN_WIDTH].astype(BF16)

    @pl.when(i < T_CTX // TM_PROJ)
    def _():
        targets = ((wk_ref, PA_W + W_B, KV_B), (wv_ref, PA_W + W_B + KV_W_B, KV_B),
                   (nk_ref, PA_W + PB_W + W_C, H_C), (nv_ref, PA_W + PB_W + 2 * W_C, H_C))
        for j, (ref, col0, n_heads) in enumerate(targets):
            if n_prev:
                ref[:, 0:n_prev] = prev_refs[j][...]
            for s in range(SEQ_PER_PROJ):
                for hh in range(n_heads):
                    ref[s, n_prev, hh] = p[s * SEQ:(s + 1) * SEQ,
                                           col0 + hh * HEAD_DIM:col0 + (hh + 1) * HEAD_DIM]


def _in_projection(x_ctx, x_lat, mod, w_in_bf16, earlier):
    n_ctx_tiles = T_CTX // TM_PROJ
    n_prev = earlier[0].shape[1] if earlier else 0

    def cache_spec(n_layers, n_heads):
        return pl.BlockSpec((SEQ_PER_PROJ, n_layers, n_heads, SEQ, HEAD_DIM),
                            lambda i: (jnp.minimum(i, n_ctx_tiles - 1), 0, 0, 0, 0))

    cache_heads = (KV_B, KV_B, H_C, H_C)
    return pl.pallas_call(
        functools.partial(_inproj_kernel, n_prev),
        out_shape=(
            jax.ShapeDtypeStruct((T_ALL, PA_W), BF16),
            jax.ShapeDtypeStruct((T_ALL, PB_W), BF16),
            jax.ShapeDtypeStruct((T_ALL, PC_W), BF16),
        ) + tuple(jax.ShapeDtypeStruct((BATCH, n_prev + 1, nh, SEQ, HEAD_DIM), F32) for nh in cache_heads),
        grid=(T_ALL // TM_PROJ,),
        in_specs=[
            pl.BlockSpec((TM_PROJ, D_MODEL), lambda i: (jnp.minimum(i, n_ctx_tiles - 1), 0)),
            pl.BlockSpec((TM_PROJ, D_MODEL), lambda i: (jnp.maximum(i - n_ctx_tiles, 0), 0)),
            pl.BlockSpec((N_COND, 6 * D_MODEL), lambda i: (0, 0)),
            pl.BlockSpec((D_MODEL, IN_WIDTH), lambda i: (0, 0)),
        ] + [cache_spec(n_prev, nh) for nh in cache_heads if n_prev],
        out_specs=(
            pl.BlockSpec((TM_PROJ, PA_W), lambda i: (i, 0)),
            pl.BlockSpec((TM_PROJ, PB_W), lambda i: (i, 0)),
            pl.BlockSpec((TM_PROJ, PC_W), lambda i: (i, 0)),
        ) + tuple(cache_spec(n_prev + 1, nh) for nh in cache_heads),
        compiler_params=pltpu.CompilerParams(
            dimension_semantics=("arbitrary",), vmem_limit_bytes=VMEM_LIMIT),
        name="in_projection",
    )(x_ctx, x_lat, mod, w_in_bf16, *earlier)


def _decay_matrix(lg_f, lg_b, n):
    row = lax.broadcasted_iota(I32, (n, n), 0)
    col = lax.broadcasted_iota(I32, (n, n), 1)
    diff = (row - col).astype(F32)
    fwd = jnp.where(diff >= 0, jnp.exp(lg_f * jnp.maximum(diff, 0.0)), 0.0)
    bwd = jnp.where(diff <= 0, jnp.exp(lg_b * jnp.maximum(-diff, 0.0)), 0.0)
    return (fwd + bwd) * ATTN_SCALE


def _stacked_softmax_attend(scores, values, extra_logit=None):
    rows = scores[0].shape[0]
    s = jnp.concatenate(scores, axis=0)
    m = jnp.max(s, -1, keepdims=True)
    if extra_logit is not None:
        m = jnp.maximum(m, extra_logit)
    p = jnp.exp(s - m)
    den = jnp.sum(p, -1, keepdims=True)
    if extra_logit is not None:
        den = den + jnp.exp(extra_logit - m)
    p = p.astype(BF16)
    return [_dot(p[h * rows:(h + 1) * rows], v) / den[h * rows:(h + 1) * rows]
            for h, v in enumerate(values)]


def _lane_xor_matrix(width, distance):
    r = lax.broadcasted_iota(I32, (width, width), 0)
    c = lax.broadcasted_iota(I32, (width, width), 1)
    return jnp.where((r ^ distance) == c, 1.0, 0.0).astype(BF16)


def _ctx_mixer_kernel(sink_ref, pa_ref, pb_ref, pc_ref, decf_ref, decb_ref, gng_ref, gnb_ref,
                      mix_ref, sf_ref, sb_ref, dmat_ref, zf_ref, zb_ref):
    n = SEQ
    hd = HEAD_DIM

    @pl.when(pl.program_id(0) == 0)
    def _():
        pos = lax.broadcasted_iota(I32, (n, hd), 0).astype(F32)
        zf, zb = [], []
        for h in range(H_A):
            lg_f = _log_sigmoid(decf_ref[h])
            lg_b = _log_sigmoid(decb_ref[h])
            dmat_ref[h] = _decay_matrix(lg_f, lg_b, n)
            zf.append(jnp.exp(lg_f[:, 0:hd] * (n - 1.0 - pos)) * ATTN_SCALE)
            zb.append(jnp.exp(lg_b[:, 0:hd] * pos) * ATTN_SCALE)
        for p in range(H_A // 2):
            zf_ref[p] = jnp.concatenate(zf[2 * p:2 * p + 2], axis=1)
            zb_ref[p] = jnp.concatenate(zb[2 * p:2 * p + 2], axis=1)

    low = lax.broadcasted_iota(I32, (n, LANES), 1) < hd

    def own_half(x, h):
        zero = jnp.zeros_like(x)
        return jnp.where(low, x, zero) if h % 2 == 0 else jnp.where(low, zero, x)

    def merge(first, second):
        return jnp.where(low, first, second)

    def pair_cols(ref, base, p):
        return ref[:, base + p * LANES:base + (p + 1) * LANES]

    def head_mean(x):
        first = jnp.sum(jnp.where(low, x, 0.0), -1, keepdims=True)
        second = jnp.sum(jnp.where(low, 0.0, x), -1, keepdims=True)
        return merge(first, second) * (1.0 / hd)

    for p in range(H_A // 2):
        q_pair, k_pair = pair_cols(pa_ref, 0, p), pair_cols(pa_ref, W_A, p)
        v_pair, gate_pair = pair_cols(pa_ref, 2 * W_A, p), pair_cols(pa_ref, 3 * W_A, p)
        outs = []
        for h in (2 * p, 2 * p + 1):
            a = _dot_nt(own_half(q_pair, h), k_pair)
            outs.append(_dot((a * dmat_ref[h]).astype(BF16), v_pair))
        o = merge(outs[0], outs[1])
        kf = k_pair.astype(F32)
        for st_ref, z_ref in ((sf_ref, zf_ref), (sb_ref, zb_ref)):
            st = _dot((kf * z_ref[p]).T.astype(BF16), v_pair)
            st_ref[0, 2 * p] = st[0:hd, 0:hd]
            st_ref[0, 2 * p + 1] = st[hd:2 * hd, hd:2 * hd]
        mu = head_mean(o)
        oc = o - mu
        var = head_mean(oc * oc)
        on = oc * lax.rsqrt(var + GN_EPS) * pair_cols(gng_ref, 0, p) + pair_cols(gnb_ref, 0, p)
        mix_ref[:, p * LANES:(p + 1) * LANES] = (on * _silu(gate_pair.astype(F32))).astype(BF16)

    group = H_B // KV_B
    swap = _lane_xor_matrix(LANES, hd)
    kv_k = pb_ref[:, W_B:W_B + KV_W_B]
    kv_v = pb_ref[:, W_B + KV_W_B:W_B + 2 * KV_W_B]
    scores = []
    for hh in range(H_B):
        q = own_half(pair_cols(pb_ref, 0, hh // 2), hh) * ATTN_SCALE
        if hh % 2 != hh // group:
            q = _dot(q, swap).astype(BF16)
        scores.append(_dot_nt(q, kv_k))
    sinks = jnp.concatenate([jnp.full((n, 1), sink_ref[hh], F32) for hh in range(H_B)], axis=0)
    outs = []
    for hh, o in enumerate(_stacked_softmax_attend(scores, [kv_v] * H_B, sinks)):
        o = o.astype(BF16)
        outs.append(_dot(o, swap).astype(BF16) if hh % 2 != hh // group else o)
    for p in range(H_B // 2):
        mix_ref[:, W_A + p * LANES:W_A + (p + 1) * LANES] = merge(outs[2 * p], outs[2 * p + 1])

    scores, values = [], []
    for hh in range(H_C):
        q = own_half(pair_cols(pc_ref, 0, hh // 2), hh) * ATTN_SCALE
        scores.append(_dot_nt(q, pair_cols(pc_ref, W_C, hh // 2)))
        values.append(pair_cols(pc_ref, 2 * W_C, hh // 2))
    outs = _stacked_softmax_attend(scores, values)
    for p in range(H_C // 2):
        mix_ref[:, W_A + W_B + p * LANES:W_A + W_B + (p + 1) * LANES] = merge(
            outs[2 * p], outs[2 * p + 1]).astype(BF16)


def _ctx_mixers(pa, pb, pc, sink, decf, decb, gng, gnb):
    return pl.pallas_call(
        _ctx_mixer_kernel,
        out_shape=(
            jax.ShapeDtypeStruct((T_CTX, D_MODEL), BF16),
            jax.ShapeDtypeStruct((BATCH, H_A, HEAD_DIM, HEAD_DIM), F32),
            jax.ShapeDtypeStruct((BATCH, H_A, HEAD_DIM, HEAD_DIM), F32),
        ),
        grid=(BATCH,),
        in_specs=[
            pl.BlockSpec(memory_space=pltpu.SMEM),
            pl.BlockSpec((SEQ, PA_W), lambda b: (b, 0)),
            pl.BlockSpec((SEQ, PB_W), lambda b: (b, 0)),
            pl.BlockSpec((SEQ, PC_W), lambda b: (b, 0)),
            pl.BlockSpec((H_A, 1, SEQ), lambda b: (0, 0, 0)),
            pl.BlockSpec((H_A, 1, SEQ), lambda b: (0, 0, 0)),
            pl.BlockSpec((1, W_A), lambda b: (0, 0)),
            pl.BlockSpec((1, W_A), lambda b: (0, 0)),
        ],
        out_specs=(
            pl.BlockSpec((SEQ, D_MODEL), lambda b: (b, 0)),
            pl.BlockSpec((1, H_A, HEAD_DIM, HEAD_DIM), lambda b: (b, 0, 0, 0)),
            pl.BlockSpec((1, H_A, HEAD_DIM, HEAD_DIM), lambda b: (b, 0, 0, 0)),
        ),
        scratch_shapes=[
            pltpu.VMEM((H_A, SEQ, SEQ), F32),
            pltpu.VMEM((H_A // 2, SEQ, LANES), F32),
            pltpu.VMEM((H_A // 2, SEQ, LANES), F32),
        ],
        compiler_params=pltpu.CompilerParams(
            dimension_semantics=("arbitrary",), vmem_limit_bytes=VMEM_LIMIT),
        name="ctx_mixers",
    )(sink, pa, pb, pc, decf, decb, gng, gnb)


def _lat_ret_kernel(pa_ref, stf_ref, stb_ref, decf_ref, decb_ref, gng_ref, gnb_ref, ya_ref):
    c = RET_CHUNK
    hd = HEAD_DIM
    n_chunks = DEC_SEQ // c
    pos = lax.broadcasted_iota(I32, (c, LANES), 0).astype(F32)
    low = lax.broadcasted_iota(I32, (c, LANES), 1) < hd
    same_head = (lax.broadcasted_iota(I32, (LANES, LANES), 0) // hd
                 == lax.broadcasted_iota(I32, (LANES, LANES), 1) // hd)
    zero_blk = jnp.zeros((hd, hd), F32)

    def block_diag(a, b):
        return jnp.concatenate([jnp.concatenate([a, zero_blk], axis=1),
                                jnp.concatenate([zero_blk, b], axis=1)], axis=0)

    def head_mean(x):
        first = jnp.sum(jnp.where(low, x, 0.0), -1, keepdims=True)
        second = jnp.sum(jnp.where(low, 0.0, x), -1, keepdims=True)
        return jnp.where(low, first, second) * (1.0 / hd)

    for p in range(H_A // 2):
        lg_f = [_log_sigmoid(decf_ref[h]) for h in (2 * p, 2 * p + 1)]
        lg_b = [_log_sigmoid(decb_ref[h]) for h in (2 * p, 2 * p + 1)]
        dmat = [_decay_matrix(lg_f[t], lg_b[t], c) for t in range(2)]
        lf = jnp.where(low[0:1], lg_f[0][:, 0:LANES], lg_f[1][:, 0:LANES])
        lb = jnp.where(low[0:1], lg_b[0][:, 0:LANES], lg_b[1][:, 0:LANES])
        zf = jnp.exp(lf * (c - 1.0 - pos)) * ATTN_SCALE
        zb = jnp.exp(lb * pos) * ATTN_SCALE
        xf = jnp.exp(lf * (pos + 1.0))
        xb = jnp.exp(lb * (c - pos))
        gcf = jnp.exp(lf * float(c))
        gcb = jnp.exp(lb * float(c))
        pair = slice(p * LANES, (p + 1) * LANES)

        def chunk(i, base):
            return pa_ref[i * c:(i + 1) * c, base + p * LANES:base + (p + 1) * LANES]

        kv_f, kv_b = [], []
        for i in range(n_chunks):
            kf = chunk(i, W_A).astype(F32)
            v = chunk(i, 2 * W_A)
            kv_f.append(jnp.where(same_head, _dot((kf * zf).T.astype(BF16), v), 0.0))
            kv_b.append(jnp.where(same_head, _dot((kf * zb).T.astype(BF16), v), 0.0))

        s = block_diag(stf_ref[0, 0, 2 * p], stf_ref[0, 0, 2 * p + 1])
        seen_f = []
        for i in range(n_chunks):
            seen_f.append(s)
            s = gcf * s + kv_f[i]
        s = block_diag(stb_ref[0, 0, 2 * p], stb_ref[0, 0, 2 * p + 1])
        seen_b = [None] * n_chunks
        for i in reversed(range(n_chunks)):
            seen_b[i] = s
            s = gcb * s + kv_b[i]

        for i in range(n_chunks):
            q_pair, k_pair, v_pair = chunk(i, 0), chunk(i, W_A), chunk(i, 2 * W_A)
            inner = []
            for t in range(2):
                zero = jnp.zeros_like(q_pair)
                q = jnp.where(low, q_pair, zero) if t == 0 else jnp.where(low, zero, q_pair)
                inner.append(_dot((_dot_nt(q, k_pair) * dmat[t]).astype(BF16), v_pair))
            qf = q_pair.astype(F32)
            lhs = jnp.concatenate([(qf * xf).astype(BF16), (qf * xb).astype(BF16)], axis=1)
            rhs = jnp.concatenate([seen_f[i], seen_b[i]], axis=0).astype(BF16)
            o = jnp.where(low, inner[0], inner[1]) + _dot(lhs, rhs)
            mu = head_mean(o)
            oc = o - mu
            var = head_mean(oc * oc)
            on = oc * lax.rsqrt(var + GN_EPS) * gng_ref[:, pair] + gnb_ref[:, pair]
            y = on * _silu(chunk(i, 3 * W_A).astype(F32))
            ya_ref[i * c:(i + 1) * c, pair] = y.astype(BF16)


def _lat_retention(pa, st_f, st_b, layer, decf, decb, gng, gnb):
    lat0 = T_CTX // DEC_SEQ
    st_spec = pl.BlockSpec((1, 1, H_A, HEAD_DIM, HEAD_DIM), lambda b: (b, layer, 0, 0, 0))
    return pl.pallas_call(
        _lat_ret_kernel,
        out_shape=jax.ShapeDtypeStruct((T_LAT, W_A), BF16),
        grid=(DEC_BATCH,),
        in_specs=[
            pl.BlockSpec((DEC_SEQ, PA_W), lambda b: (lat0 + b, 0)),
            st_spec, st_spec,
            pl.BlockSpec((H_A, 1, RET_CHUNK), lambda b: (0, 0, 0)),
            pl.BlockSpec((H_A, 1, RET_CHUNK), lambda b: (0, 0, 0)),
            pl.BlockSpec((1, W_A), lambda b: (0, 0)),
            pl.BlockSpec((1, W_A), lambda b: (0, 0)),
        ],
        out_specs=pl.BlockSpec((DEC_SEQ, W_A), lambda b: (b, 0)),
        compiler_params=pltpu.CompilerParams(
            dimension_semantics=("arbitrary",), vmem_limit_bytes=VMEM_LIMIT),
        name="lat_retention",
    )(pa, st_f, st_b, decf, decb, gng, gnb)


def _rope(x, cos, sin_signed, swap):
    return x.astype(F32) * cos + _dot(x, swap) * sin_signed


def _lat_win_kernel(sink_ref, pq_ref, pseq_ref, kctx_ref, vctx_ref, cos_ref, sin_ref, yb_ref,
                    krope_ref, kc_ref, vc_ref):
    n = pl.program_id(1)
    hd = HEAD_DIM
    qb = WIN_QB
    band = WIN_BAND
    group = H_B // KV_B
    rot = _lane_xor_matrix(LANES, hd // 2)
    swap = _lane_xor_matrix(LANES, hd)

    @pl.when(n == 0)
    def _():
        k = pseq_ref[:, W_B:W_B + KV_W_B]
        krope_ref[...] = _rope(k, cos_ref[...], sin_ref[...], rot).astype(BF16)
        kc_ref[...] = jnp.concatenate([kctx_ref[0, 0, j] for j in range(KV_B)], axis=1).astype(BF16)
        vc_ref[...] = jnp.concatenate([vctx_ref[0, 0, j] for j in range(KV_B)], axis=1).astype(BF16)

    q_rows = pl.ds(pl.multiple_of(n * qb, qb), qb)
    cos_q = cos_ref[q_rows, :]
    sin_q = sin_ref[q_rows, :]
    low = lax.broadcasted_iota(I32, (qb, LANES), 1) < hd
    q_heads = []
    for p in range(H_B // 2):
        q_pair = _rope(pq_ref[:, p * LANES:(p + 1) * LANES], cos_q, sin_q, rot) * ATTN_SCALE
        for hh in (2 * p, 2 * p + 1):
            q = jnp.where(low, q_pair, 0.0) if hh % 2 == 0 else jnp.where(low, 0.0, q_pair)
            q = q.astype(BF16)
            q_heads.append(_dot(q, swap).astype(BF16) if hh % 2 != hh // group else q)

    ws = jnp.clip(n * qb - WINDOW, 0, DEC_SEQ - band)
    k_rows = pl.ds(pl.multiple_of(ws, WINDOW), band)
    q_pos = n * qb + lax.broadcasted_iota(I32, (group * qb, band), 0) % qb
    k_pos = ws + lax.broadcasted_iota(I32, (group * qb, band), 1)
    valid = jnp.abs(k_pos - q_pos) <= WINDOW
    head_of_row = lax.broadcasted_iota(I32, (group * qb, 1), 0) // qb
    kw = krope_ref[k_rows, :]
    vw = pseq_ref[k_rows, W_B + KV_W_B:W_B + 2 * KV_W_B]
    outs = []
    for j in range(KV_B):
        heads = [j * group + g for g in range(group)]
        qs = jnp.concatenate([q_heads[hh] for hh in heads], axis=0)
        s_loc = jnp.where(valid, _dot_nt(qs, kw), NEG_INF)
        s_ctx = _dot_nt(qs, kc_ref[...])
        sink = jnp.zeros((group * qb, 1), F32)
        for g, hh in enumerate(heads):
            sink = jnp.where(head_of_row == g, sink_ref[hh], sink)
        m = jnp.maximum(jnp.maximum(jnp.max(s_loc, -1, keepdims=True),
                                    jnp.max(s_ctx, -1, keepdims=True)), sink)
        p_loc = jnp.exp(s_loc - m)
        p_ctx = jnp.exp(s_ctx - m)
        den = (jnp.sum(p_loc, -1, keepdims=True) + jnp.sum(p_ctx, -1, keepdims=True)
               + jnp.exp(sink - m))
        o = ((_dot(p_loc.astype(BF16), vw) + _dot(p_ctx.astype(BF16), vc_ref[...])) / den).astype(BF16)
        for g, hh in enumerate(heads):
            o_h = o[g * qb:(g + 1) * qb]
            outs.append(_dot(o_h, swap).astype(BF16) if hh % 2 != j else o_h)
    for p in range(H_B // 2):
        yb_ref[:, p * LANES:(p + 1) * LANES] = jnp.where(low, outs[2 * p], outs[2 * p + 1])


def _lat_window_attn(pb, cache_k, cache_v, layer, sink, cos_t, sin_t):
    n_blk = DEC_SEQ // WIN_QB
    lat_blk0 = T_CTX // WIN_QB
    lat_seq0 = T_CTX // DEC_SEQ
    ctx_spec = pl.BlockSpec((1, 1, KV_B, PAST_LEN, HEAD_DIM), lambda b, n: (b, layer, 0, 0, 0))
    return pl.pallas_call(
        _lat_win_kernel,
        out_shape=jax.ShapeDtypeStruct((T_LAT, W_B), BF16),
        grid=(DEC_BATCH, n_blk),
        in_specs=[
            pl.BlockSpec(memory_space=pltpu.SMEM),
            pl.BlockSpec((WIN_QB, PB_W), lambda b, n: (lat_blk0 + b * n_blk + n, 0)),
            pl.BlockSpec((DEC_SEQ, PB_W), lambda b, n: (lat_seq0 + b, 0)),
            ctx_spec, ctx_spec,
            pl.BlockSpec((DEC_SEQ, LANES), lambda b, n: (0, 0)),
            pl.BlockSpec((DEC_SEQ, LANES), lambda b, n: (0, 0)),
        ],
        out_specs=pl.BlockSpec((WIN_QB, W_B), lambda b, n: (b * n_blk + n, 0)),
        scratch_shapes=[
            pltpu.VMEM((DEC_SEQ, KV_W_B), BF16),
            pltpu.VMEM((PAST_LEN, KV_W_B), BF16),
            pltpu.VMEM((PAST_LEN, KV_W_B), BF16),
        ],
        compiler_params=pltpu.CompilerParams(
            dimension_semantics=("arbitrary", "arbitrary"), vmem_limit_bytes=VMEM_LIMIT),
        name="lat_window_attn",
    )(sink, pb, pb, cache_k, cache_v, cos_t, sin_t)


NA_Q = NA_TILE_ROWS * GRID_W
NA_K = NA_KEY_ROWS * GRID_W
NA_TILES = DEC_SEQ // NA_Q
LAT_ROWS = DEC_SEQ // GRID_W


def _na_window_start(tile):
    return jnp.clip(tile * NA_TILE_ROWS - NA_ROWS // 2, 0, LAT_ROWS - NA_KEY_ROWS)


def _lat_na_kernel(pq_ref, pseq_ref, kctx_ref, vctx_ref, bias_ref, yc_ref):
    t = pl.program_id(1)
    hd = HEAD_DIM
    k_rows = pl.ds(pl.multiple_of(_na_window_start(t) * GRID_W, GRID_W), NA_K)
    for hh in range(H_C):
        q = pq_ref[:, hh * hd:(hh + 1) * hd] * ATTN_SCALE
        kw = pseq_ref[k_rows, W_C + hh * hd:W_C + (hh + 1) * hd]
        vw = pseq_ref[k_rows, 2 * W_C + hh * hd:2 * W_C + (hh + 1) * hd]
        kc = kctx_ref[0, 0, hh].astype(BF16)
        vc = vctx_ref[0, 0, hh].astype(BF16)
        s_loc = _dot_nt(q, kw) + bias_ref[0, hh]
        s_ctx = _dot_nt(q, kc)
        m = jnp.maximum(jnp.max(s_loc, -1, keepdims=True), jnp.max(s_ctx, -1, keepdims=True))
        p_loc = jnp.exp(s_loc - m)
        p_ctx = jnp.exp(s_ctx - m)
        den = jnp.sum(p_loc, -1, keepdims=True) + jnp.sum(p_ctx, -1, keepdims=True)
        o = (_dot(p_loc.astype(BF16), vw) + _dot(p_ctx.astype(BF16), vc)) / den
        yc_ref[:, hh * hd:(hh + 1) * hd] = o.astype(BF16)


def _na_tile_type(t):
    return jnp.where(t == 0, 0, jnp.where(t == NA_TILES - 1, 2, 1))


def _lat_na_attn(pc, cache_k, cache_v, layer, maskbias):
    lat_tile0 = T_CTX // NA_Q
    lat_seq0 = T_CTX // DEC_SEQ
    ctx_spec = pl.BlockSpec((1, 1, H_C, PAST_LEN, HEAD_DIM), lambda b, t: (b, layer, 0, 0, 0))
    return pl.pallas_call(
        _lat_na_kernel,
        out_shape=jax.ShapeDtypeStruct((T_LAT, W_C), BF16),
        grid=(DEC_BATCH, NA_TILES),
        in_specs=[
            pl.BlockSpec((NA_Q, PC_W), lambda b, t: (lat_tile0 + b * NA_TILES + t, 0)),
            pl.BlockSpec((DEC_SEQ, PC_W), lambda b, t: (lat_seq0 + b, 0)),
            ctx_spec, ctx_spec,
            pl.BlockSpec((1, H_C, NA_Q, NA_K), lambda b, t: (_na_tile_type(t), 0, 0, 0)),
        ],
        out_specs=pl.BlockSpec((NA_Q, W_C), lambda b, t: (b * NA_TILES + t, 0)),
        compiler_params=pltpu.CompilerParams(
            dimension_semantics=("arbitrary", "arbitrary"), vmem_limit_bytes=VMEM_LIMIT),
        name="lat_na_attn",
    )(pc, pc, cache_k, cache_v, maskbias)


def _na_block_index():
    out = np.zeros((3, NA_TILE_ROWS, NA_KEY_ROWS), np.int32)
    for ty, tile in enumerate((0, 1, NA_TILES - 1)):
        r = tile * NA_TILE_ROWS
        ws = int(np.clip(r - NA_ROWS // 2, 0, LAT_ROWS - NA_KEY_ROWS))
        for qq in range(NA_TILE_ROWS):
            qr = r + qq
            r0 = int(np.clip(qr - NA_ROWS // 2, 0, LAT_ROWS - NA_ROWS))
            for kk in range(NA_KEY_ROWS):
                kr = ws + kk
                out[ty, qq, kk] = kr - qr + NA_ROWS - 1 if r0 <= kr < r0 + NA_ROWS else 2 * NA_ROWS - 1
    return out


def _na_maskbias(rpb):
    qc = np.arange(GRID_W)[:, None]
    kc = np.arange(GRID_W)[None, :]
    c0 = np.clip(qc - NA_COLS // 2, 0, GRID_W - NA_COLS)
    col_ok = (kc >= c0) & (kc < c0 + NA_COLS)
    ci = np.clip(kc - qc + NA_COLS - 1, 0, 2 * NA_COLS - 2)
    onehot = (ci[None] == np.arange(2 * NA_COLS - 1)[:, None, None]).astype(np.float32)
    cols = jnp.einsum("hab,bqk->haqk", rpb, jnp.asarray(onehot), precision=lax.Precision.HIGHEST)
    cols = jnp.where(jnp.asarray(col_ok)[None, None], cols, NEG_INF)
    cols = jnp.concatenate([cols, jnp.full((H_C, 1, GRID_W, GRID_W), NEG_INF, F32)], axis=1)
    block_index = _na_block_index()

    def assemble(cols_ref, out_ref):
        for ty in range(3):
            for qq in range(NA_TILE_ROWS):
                for kk in range(NA_KEY_ROWS):
                    out_ref[ty, 0, qq * GRID_W:(qq + 1) * GRID_W, kk * GRID_W:(kk + 1) * GRID_W] = (
                        cols_ref[0, int(block_index[ty, qq, kk])])

    return pl.pallas_call(
        assemble,
        out_shape=jax.ShapeDtypeStruct((3, H_C, NA_Q, NA_K), F32),
        grid=(H_C,),
        in_specs=[pl.BlockSpec((1, 2 * NA_ROWS, GRID_W, GRID_W), lambda h: (h, 0, 0, 0))],
        out_specs=pl.BlockSpec((3, 1, NA_Q, NA_K), lambda h: (0, h, 0, 0)),
        compiler_params=pltpu.CompilerParams(dimension_semantics=("arbitrary",)),
        name="na_bias_assemble",
    )(cols)


def _rope_tables():
    t = np.arange(DEC_SEQ)
    n_freq = HEAD_DIM // 4
    inv = (ROPE_BASE ** (-np.arange(n_freq, dtype=np.float32) / n_freq)).astype(np.float32)
    row = (t // GRID_W).astype(np.float32)[:, None] * inv
    col = (t % GRID_W).astype(np.float32)[:, None] * inv
    ang = np.concatenate([row, col], -1)
    cos, sin = np.cos(ang), np.sin(ang)
    cos_h = np.concatenate([cos, cos], -1)
    sin_h = np.concatenate([-sin, sin], -1)
    reps = LANES // HEAD_DIM
    return (jnp.asarray(np.tile(cos_h, (1, reps)), F32), jnp.asarray(np.tile(sin_h, (1, reps)), F32))


def _first_index_of(mask, iota, sentinel):
    return jnp.min(jnp.where(mask, iota, sentinel), axis=0, keepdims=True)


def _route(logits, b_col):
    n = logits.shape[1]
    scores = jax.nn.sigmoid(logits)
    sel = scores + b_col
    io_g = lax.broadcasted_iota(I32, (GROUP_SIZE, n), 0)
    gs_rows = []
    for g in range(N_GROUPS):
        s = sel[g * GROUP_SIZE:(g + 1) * GROUP_SIZE]
        m1 = jnp.max(s, axis=0, keepdims=True)
        i1 = _first_index_of(s == m1, io_g, GROUP_SIZE)
        m2 = jnp.max(jnp.where(io_g == i1, PICKED, s), axis=0, keepdims=True)
        gs_rows.append(m1 + m2)
    gs = jnp.concatenate(gs_rows, axis=0)
    io_n = lax.broadcasted_iota(I32, (N_GROUPS, n), 0)
    gsel = jnp.zeros((N_GROUPS, n), F32)
    for _ in range(TOPK_GROUPS):
        mg = jnp.max(gs, axis=0, keepdims=True)
        gi = _first_index_of(gs == mg, io_n, N_GROUPS)
        hit = io_n == gi
        gsel = jnp.where(hit, 1.0, gsel)
        gs = jnp.where(hit, PICKED, gs)
    cand = jnp.concatenate(
        [jnp.where(gsel[g:g + 1] > 0.5, sel[g * GROUP_SIZE:(g + 1) * GROUP_SIZE], NEG_INF)
         for g in range(N_GROUPS)], axis=0)
    io_e = lax.broadcasted_iota(I32, (N_EXPERTS, n), 0)
    picks, raw = [], []
    for _ in range(TOP_K):
        mv = jnp.max(cand, axis=0, keepdims=True)
        ei = _first_index_of(cand == mv, io_e, N_EXPERTS)
        hit = io_e == ei
        picks.append((hit, ei))
        raw.append(jnp.sum(jnp.where(hit, scores, 0.0), axis=0, keepdims=True))
        cand = jnp.where(hit, PICKED, cand)
    return picks, raw


def _post_mixer_kernel(xc_ref, xl_ref, mixc_ref, ya_ref, yb_ref, yc_ref, wout_ref, mod_ref, g_ref, b_ref,
                       wr_ref, br_ref,
                       x1_ref, h2_ref, eidx_ref, wsel_ref, rank_ref, cnt_ref):
    i = pl.program_id(0)
    tm = TM_TOK

    @pl.when(i == 0)
    def _():
        cnt_ref[...] = jnp.zeros_like(cnt_ref)

    ci = _cond_row(i, tm)
    gate1 = mod_ref[pl.ds(ci, 1), 2 * D_MODEL:3 * D_MODEL]
    sh2 = mod_ref[pl.ds(ci, 1), 3 * D_MODEL:4 * D_MODEL]
    sc2 = mod_ref[pl.ds(ci, 1), 4 * D_MODEL:5 * D_MODEL]
    mix_lat = jnp.concatenate([ya_ref[...], yb_ref[...], yc_ref[...]], axis=-1)
    mix = jnp.where(i < T_CTX // tm, mixc_ref[...], mix_lat)
    y = _dot(mix, wout_ref[...])
    x = jnp.where(i < T_CTX // tm, xc_ref[...], xl_ref[...])
    x1 = _layer_norm(ALPHA * x + gate1 * y, g_ref[...], b_ref[...])
    x1_ref[...] = x1
    h2 = x1 * (1.0 + sc2) + sh2
    h_hi = h2.astype(BF16)
    h2_ref[...] = h_hi
    h_lo = (h2 - h_hi.astype(F32)).astype(BF16)
    both = (_dot(h_hi, wr_ref[...]) + _dot(h_lo, wr_ref[...])).T
    logits = both[0:N_EXPERTS] + both[N_EXPERTS:2 * N_EXPERTS]
    routed = [_route(logits[:, g * LANES:(g + 1) * LANES], br_ref[...]) for g in range(tm // LANES)]
    multi_g = []
    for picks, _ in routed:
        m = jnp.zeros((N_EXPERTS, LANES), F32)
        for hit, _ in picks:
            m = m + jnp.where(hit, 1.0, 0.0)
        multi_g.append(m)
    multi = jnp.concatenate(multi_g, axis=1)
    before = (lax.broadcasted_iota(I32, (tm, tm), 0) < lax.broadcasted_iota(I32, (tm, tm), 1))
    cum = _dot(multi.astype(BF16), jnp.where(before, 1.0, 0.0).astype(BF16))
    pad = jnp.zeros((SUBLANES - TOP_K, LANES), F32)
    for g, (picks, raw) in enumerate(routed):
        lanes = slice(g * LANES, (g + 1) * LANES)
        total = raw[0]
        for r in raw[1:]:
            total = total + r
        scale = ROUTE_SCALE / total
        cum_g = cum[:, lanes]
        eidx_ref[:, lanes] = jnp.concatenate([ei for _, ei in picks] + [pad.astype(I32)], axis=0)
        wsel_ref[:, lanes] = jnp.concatenate([r * scale for r in raw] + [pad], axis=0)
        rank_ref[:, lanes] = jnp.concatenate(
            [jnp.sum(jnp.where(hit, cum_g, 0.0), axis=0, keepdims=True) for hit, _ in picks] + [pad],
            axis=0).astype(I32)
    tile_lane = lax.broadcasted_iota(I32, (N_EXPERTS, LANES), 1)
    cnt_ref[...] = jnp.where(tile_lane == i, jnp.sum(multi, axis=1, keepdims=True), cnt_ref[...])


def _post_mixer(x_ctx, x_lat, mix_c, ya, yb, yc, w_out_bf16, mod, ln_g, ln_b, wr_split, b_router_col):
    n_ctx = T_CTX // TM_TOK
    ctx_map = lambda i: (jnp.minimum(i, n_ctx - 1), 0)
    lat_map = lambda i: (jnp.maximum(i - n_ctx, 0), 0)
    row_map = lambda i: (i, 0)
    const = lambda i: (0, 0)
    tok_map = lambda i: (0, i)
    return pl.pallas_call(
        _post_mixer_kernel,
        out_shape=(
            jax.ShapeDtypeStruct((T_ALL, D_MODEL), F32),
            jax.ShapeDtypeStruct((T_ALL, D_MODEL), BF16),
            jax.ShapeDtypeStruct((SUBLANES, T_ALL), I32),
            jax.ShapeDtypeStruct((SUBLANES, T_ALL), F32),
            jax.ShapeDtypeStruct((SUBLANES, T_ALL), I32),
            jax.ShapeDtypeStruct((N_EXPERTS, LANES), F32),
        ),
        grid=(N_TOK_TILES,),
        in_specs=[
            pl.BlockSpec((TM_TOK, D_MODEL), ctx_map),
            pl.BlockSpec((TM_TOK, D_MODEL), lat_map),
            pl.BlockSpec((TM_TOK, D_MODEL), ctx_map),
            pl.BlockSpec((TM_TOK, W_A), lat_map),
            pl.BlockSpec((TM_TOK, W_B), lat_map),
            pl.BlockSpec((TM_TOK, W_C), lat_map),
            pl.BlockSpec((D_MODEL, D_MODEL), const),
            pl.BlockSpec((N_COND, 6 * D_MODEL), const),
            pl.BlockSpec((1, D_MODEL), const),
            pl.BlockSpec((1, D_MODEL), const),
            pl.BlockSpec((D_MODEL, 2 * N_EXPERTS), const),
            pl.BlockSpec((N_EXPERTS, 1), const),
        ],
        out_specs=(
            pl.BlockSpec((TM_TOK, D_MODEL), row_map),
            pl.BlockSpec((TM_TOK, D_MODEL), row_map),
            pl.BlockSpec((SUBLANES, TM_TOK), tok_map),
            pl.BlockSpec((SUBLANES, TM_TOK), tok_map),
            pl.BlockSpec((SUBLANES, TM_TOK), tok_map),
            pl.BlockSpec((N_EXPERTS, LANES), const),
        ),
        compiler_params=pltpu.CompilerParams(
            dimension_semantics=("arbitrary",), vmem_limit_bytes=VMEM_LIMIT),
        name="post_mixer",
    )(x_ctx, x_lat, mix_c, ya, yb, yc, w_out_bf16, mod, ln_g, ln_b, wr_split, b_router_col)


PLAN_TILES = 4


def _plan_kernel(eidx_ref, rank_ref, nmat_ref, lslot_ref, unit_ref, gend_ref):
    step = pl.program_id(0)
    units = jnp.floor((nmat_ref[...] + (UNIT - 1.0)) * (1.0 / UNIT))
    units_bf = units.astype(BF16)
    earlier_e = (lax.broadcasted_iota(I32, (N_EXPERTS, N_EXPERTS), 1)
                 < lax.broadcasted_iota(I32, (N_EXPERTS, N_EXPERTS), 0))
    tri_e = jnp.where(earlier_e, 1.0, 0.0).astype(BF16)
    earlier_t = (lax.broadcasted_iota(I32, (LANES, LANES), 0) < lax.broadcasted_iota(I32, (LANES, LANES), 1))
    tri_t = jnp.where(earlier_t, 1.0, 0.0).astype(BF16)
    local_off = _dot(tri_e, units_bf)
    tile_off = _dot(units_bf, tri_t)
    per_expert = jnp.sum(units, axis=1, keepdims=True)
    blocks = jnp.floor((per_expert + (UNITS_PER_BLOCK - 1.0)) * (1.0 / UNITS_PER_BLOCK))
    blocks_l = jnp.broadcast_to(blocks, (N_EXPERTS, LANES))
    start_blk = _dot(tri_e, blocks_l.astype(BF16))
    end_blk = start_blk + blocks_l
    gend_ref[...] = (end_blk * BM).astype(I32)

    tile_lane = lax.broadcasted_iota(I32, (N_EXPERTS, LANES), 1)
    u = lax.broadcasted_iota(I32, (N_EXPERTS, MAX_UNITS), 1).astype(F32)
    io_e = lax.broadcasted_iota(I32, (N_EXPERTS, TM_TOK), 0)
    for s in range(PLAN_TILES):
        i = step * PLAN_TILES + s
        this_tile = tile_lane == i

        def column(a):
            return jnp.sum(jnp.where(this_tile, a, 0.0), axis=1, keepdims=True)

        lo, n_u = column(local_off), column(units)
        base_unit = start_blk[:, 0:1] * UNITS_PER_BLOCK + column(tile_off) - lo
        inside = jnp.where(u >= lo, jnp.where(u < lo + n_u, 1.0, 0.0), 0.0)
        dst_unit = jnp.sum(inside * (base_unit + u), axis=0, keepdims=True)
        used = jnp.sum(inside, axis=0, keepdims=True) > 0.5
        spare = (SPARE_UNIT0 + (i % 2) * MAX_UNITS).astype(F32) + u[0:1, :]
        unit_ref[s] = jnp.where(used, dst_unit, spare).astype(I32)

        toks = slice(s * TM_TOK, (s + 1) * TM_TOK)
        rows = []
        for k in range(TOP_K):
            hit = io_e == eidx_ref[k:k + 1, toks]
            seg = jnp.sum(jnp.where(hit, lo * UNIT, 0.0), axis=0, keepdims=True)
            rows.append(seg.astype(I32) + rank_ref[k:k + 1, toks])
        rows.append(jnp.full((SUBLANES - TOP_K, TM_TOK), -1, I32))
        lslot_ref[:, toks] = jnp.concatenate(rows, axis=0)


def _slot_plan(eidx, rank, nmat):
    tok_map = lambda i: (0, i)
    const = lambda i: (0, 0)
    lslot, unit_tab, gend = pl.pallas_call(
        _plan_kernel,
        out_shape=(
            jax.ShapeDtypeStruct((SUBLANES, T_ALL), I32),
            jax.ShapeDtypeStruct((N_TOK_TILES, 1, MAX_UNITS), I32),
            jax.ShapeDtypeStruct((N_EXPERTS, LANES), I32),
        ),
        grid=(N_TOK_TILES // PLAN_TILES,),
        in_specs=[
            pl.BlockSpec((SUBLANES, PLAN_TILES * TM_TOK), tok_map),
            pl.BlockSpec((SUBLANES, PLAN_TILES * TM_TOK), tok_map),
            pl.BlockSpec((N_EXPERTS, LANES), const),
        ],
        out_specs=(
            pl.BlockSpec((SUBLANES, PLAN_TILES * TM_TOK), tok_map),
            pl.BlockSpec((PLAN_TILES, 1, MAX_UNITS), lambda i: (i, 0, 0)),
            pl.BlockSpec((N_EXPERTS, LANES), const),
        ),
        compiler_params=pltpu.CompilerParams(dimension_semantics=("arbitrary",)),
        name="slot_plan",
    )(eidx, rank, nmat)
    return lslot, unit_tab.reshape(N_TOK_TILES * MAX_UNITS), gend[:, 0]


PACK_W = D_MODEL // 2
HI_HALF = -65536


def _pack_pairs(x):
    lo = lax.bitcast_convert_type(x[:, 0:PACK_W], I32)
    hi = lax.bitcast_convert_type(x[:, PACK_W:D_MODEL], I32)
    return lax.shift_right_logical(lo, 16) | (hi & HI_HALF)


def _unpack_pairs(u):
    lo = lax.bitcast_convert_type(lax.shift_left(u, 16), F32).astype(BF16)
    hi = lax.bitcast_convert_type(u & HI_HALF, F32).astype(BF16)
    return lo, hi


def _unit_rows(unit):
    row = unit * UNIT
    return pl.ds(row if isinstance(unit, int) else pl.multiple_of(row, UNIT), UNIT)


def _unit_copy(src, src_unit, dst, dst_unit, sem):
    return pltpu.make_async_copy(src.at[_unit_rows(src_unit)], dst.at[_unit_rows(dst_unit)], sem)


def _dispatch_kernel(gend_ref, tab_ref, h2_ref, lslot_ref, xs_hbm, zero_ref, local_ref, sem_zero, sem_rows):
    i = pl.program_id(0)
    buf = i % 2

    def drain(b):
        pltpu.make_async_copy(local_ref.at[b], xs_hbm.at[pl.ds(0, LOCAL_ROWS)], sem_rows.at[b]).wait()

    def has_rows(e):
        return gend_ref[e] > jnp.where(e == 0, 0, gend_ref[jnp.maximum(e - 1, 0)])

    def zero_copy(e):
        return pltpu.make_async_copy(
            zero_ref, xs_hbm.at[pl.ds(pl.multiple_of(gend_ref[e] - BM, BM), BM)], sem_zero)

    @pl.when(i == 0)
    def _():
        zero_ref[...] = jnp.zeros_like(zero_ref)

        def start(e, c):
            @pl.when(has_rows(e))
            def _():
                zero_copy(e).start()
            return c

        def wait(e, c):
            @pl.when(has_rows(e))
            def _():
                zero_copy(e).wait()
            return c

        def tail_copy(blk):
            return pltpu.make_async_copy(
                zero_ref, xs_hbm.at[pl.ds(pl.multiple_of(blk * BM, BM), BM)], sem_zero)

        def start_tail(blk, c):
            tail_copy(blk).start()
            return c

        def wait_tail(blk, c):
            tail_copy(blk).wait()
            return c

        n_used = gend_ref[N_EXPERTS - 1] // BM
        lax.fori_loop(0, N_EXPERTS, start, 0)
        lax.fori_loop(n_used, N_BLOCKS_ALL, start_tail, 0)
        lax.fori_loop(0, N_EXPERTS, wait, 0)
        lax.fori_loop(n_used, N_BLOCKS_ALL, wait_tail, 0)

    @pl.when(i >= 2)
    def _():
        drain(buf)

    h2 = h2_ref[...]
    units_per_chunk = PERM_CHUNK // UNIT
    local = local_ref.at[buf]
    for c in range(LOCAL_ROWS // PERM_CHUNK):
        slot = c * PERM_CHUNK + lax.broadcasted_iota(I32, (PERM_CHUNK, TM_TOK), 0)
        p = jnp.zeros((PERM_CHUNK, TM_TOK), F32)
        for k in range(TOP_K):
            p = jnp.where(slot == lslot_ref[k:k + 1, :], 1.0, p)
        local[c * PERM_CHUNK:(c + 1) * PERM_CHUNK, :] = _pack_pairs(_dot(p.astype(BF16), h2))
        for u in range(c * units_per_chunk, (c + 1) * units_per_chunk):
            _unit_copy(local, u, xs_hbm, tab_ref[i * MAX_UNITS + u], sem_rows.at[buf]).start()

    @pl.when(i == N_TOK_TILES - 1)
    def _():
        drain(1 - buf)
        drain(buf)


def _dispatch(h2, lslot, unit_tab, gend):
    return pl.pallas_call(
        _dispatch_kernel,
        out_shape=jax.ShapeDtypeStruct((N_SLOTS, PACK_W), I32),
        grid_spec=pltpu.PrefetchScalarGridSpec(
            num_scalar_prefetch=2,
            grid=(N_TOK_TILES,),
            in_specs=[
                pl.BlockSpec((TM_TOK, D_MODEL), lambda i, ge, tab: (i, 0)),
                pl.BlockSpec((SUBLANES, TM_TOK), lambda i, ge, tab: (0, i)),
            ],
            out_specs=pl.BlockSpec(memory_space=pl.ANY),
            scratch_shapes=[
                pltpu.VMEM((BM, PACK_W), I32),
                pltpu.VMEM((2, LOCAL_ROWS, PACK_W), I32),
                pltpu.SemaphoreType.DMA,
                pltpu.SemaphoreType.DMA((2,)),
            ],
        ),
        compiler_params=pltpu.CompilerParams(
            dimension_semantics=("arbitrary",), vmem_limit_bytes=VMEM_LIMIT),
        name="moe_dispatch",
    )(gend, unit_tab, h2, lslot)


def _expert_kernel(gend_ref, xs_hbm, wgu_ref, wdown_ref, ys_hbm, xbuf, ybuf, wgu_bf, wdown_bf, sem_in, sem_out):
    e = pl.program_id(0)
    first = jnp.where(e == 0, 0, gend_ref[jnp.maximum(e - 1, 0)]) // BM
    last = gend_ref[e] // BM
    n_used = gend_ref[N_EXPERTS - 1] // BM

    def rows_of(blk):
        return pl.ds(pl.multiple_of(blk * BM, BM), BM)

    def in_copy(blk):
        return pltpu.make_async_copy(xs_hbm.at[rows_of(blk)], xbuf.at[blk % X_BUFS], sem_in.at[blk % X_BUFS])

    def out_copy(blk):
        return pltpu.make_async_copy(ybuf.at[blk % 2], ys_hbm.at[rows_of(blk)], sem_out.at[blk % 2])

    @pl.when(e == 0)
    def _():
        for a in range(X_AHEAD):
            @pl.when(a < n_used)
            def _(a=a):
                in_copy(a).start()

    @pl.when(last > first)
    def _():
        wgu_bf[...] = wgu_ref[0, 0].astype(BF16)
        wdown_bf[...] = wdown_ref[0, 0].astype(BF16)

    def block(blk, carry):
        @pl.when(blk + X_AHEAD < n_used)
        def _():
            in_copy(blk + X_AHEAD).start()

        in_copy(blk).wait()

        @pl.when(blk >= 2)
        def _():
            out_copy(blk - 2).wait()

        x_lo, x_hi = _unpack_pairs(xbuf[blk % X_BUFS])
        gu = _dot(x_lo, wgu_bf[0:PACK_W, :]) + _dot(x_hi, wgu_bf[PACK_W:D_MODEL, :])
        act = _silu(gu[:, 0:D_EXPERT]) * gu[:, D_EXPERT:2 * D_EXPERT]
        y = _dot(act.astype(BF16), wdown_bf[...])
        ybuf[blk % 2] = _pack_pairs(y.astype(BF16).astype(F32))
        out_copy(blk).start()
        return carry

    lax.fori_loop(first, last, block, 0)

    @pl.when(e == N_EXPERTS - 1)
    def _():
        @pl.when(n_used >= 2)
        def _():
            out_copy(n_used - 2).wait()

        @pl.when(n_used >= 1)
        def _():
            out_copy(n_used - 1).wait()

        ybuf[0] = jnp.zeros((BM, PACK_W), I32)

        def tail_copy(blk):
            return pltpu.make_async_copy(ybuf.at[0], ys_hbm.at[rows_of(blk)], sem_out.at[0])

        def start(blk, c):
            tail_copy(blk).start()
            return c

        def wait(blk, c):
            tail_copy(blk).wait()
            return c

        lax.fori_loop(n_used, N_BLOCKS_ALL, start, 0)
        lax.fori_loop(n_used, N_BLOCKS_ALL, wait, 0)


def _experts(xs, w_gu, w_down, layer, gend):
    return pl.pallas_call(
        _expert_kernel,
        out_shape=jax.ShapeDtypeStruct((N_SLOTS, PACK_W), I32),
        grid_spec=pltpu.PrefetchScalarGridSpec(
            num_scalar_prefetch=1,
            grid=(N_EXPERTS,),
            in_specs=[
                pl.BlockSpec(memory_space=pl.ANY),
                pl.BlockSpec((1, 1, D_MODEL, 2 * D_EXPERT), lambda e, ge: (layer, e, 0, 0)),
                pl.BlockSpec((1, 1, D_EXPERT, D_MODEL), lambda e, ge: (layer, e, 0, 0)),
            ],
            out_specs=pl.BlockSpec(memory_space=pl.ANY),
            scratch_shapes=[
                pltpu.VMEM((X_BUFS, BM, PACK_W), I32),
                pltpu.VMEM((2, BM, PACK_W), I32),
                pltpu.VMEM((D_MODEL, 2 * D_EXPERT), BF16),
                pltpu.VMEM((D_EXPERT, D_MODEL), BF16),
                pltpu.SemaphoreType.DMA((X_BUFS,)),
                pltpu.SemaphoreType.DMA((2,)),
            ],
        ),
        compiler_params=pltpu.CompilerParams(
            dimension_semantics=("arbitrary",), vmem_limit_bytes=VMEM_LIMIT),
        name="moe_experts",
    )(gend, xs, w_gu, w_down)


def _combine_kernel(tab_ref, x1_ref, h2_ref, lslot_ref, wsel_ref, ys_hbm, wsgu_ref, wsdown_ref, mod_ref,
                    g_ref, b_ref, outc_ref, outl_ref, local_ref, sel_ref, ylo_ref, yhi_ref, sem_rows):
    i = pl.program_id(0)
    tm = TM_TOK
    buf = i % 2

    def fetch_unit(tile, b, u):
        _unit_copy(ys_hbm, tab_ref[tile * MAX_UNITS + u], local_ref.at[b], u, sem_rows.at[b]).start()

    def drain(b):
        pltpu.make_async_copy(ys_hbm.at[pl.ds(0, LOCAL_ROWS)], local_ref.at[b], sem_rows.at[b]).wait()

    @pl.when(i == 0)
    def _():
        def body(u, c):
            fetch_unit(0, 0, u)
            return c

        lax.fori_loop(0, MAX_UNITS, body, 0, unroll=8)

    sgu = _dot(h2_ref[...], wsgu_ref[...])
    act = _silu(sgu[:, 0:D_SHARED]) * sgu[:, D_SHARED:2 * D_SHARED]
    f = _dot(act.astype(BF16), wsdown_ref[...])

    nxt = jnp.minimum(i + 1, N_TOK_TILES - 1)
    n_groups = tm // SEL_ROWS
    units_per_group = MAX_UNITS // n_groups
    slot = lax.broadcasted_iota(I32, (SEL_ROWS, LOCAL_ROWS), 1)
    for g in range(n_groups):
        rows = slice(g * SEL_ROWS, (g + 1) * SEL_ROWS)
        sel = jnp.zeros((SEL_ROWS, LOCAL_ROWS), F32)
        for k in range(TOP_K):
            sel = jnp.where(slot == lslot_ref[rows, k:k + 1], wsel_ref[rows, k:k + 1], sel)
        sel_ref[rows, :] = sel.astype(BF16)
        for u in range(g * units_per_group, (g + 1) * units_per_group):
            fetch_unit(nxt, 1 - buf, u)

    local = local_ref.at[buf]
    drain(buf)
    for c in range(LOCAL_ROWS // PERM_CHUNK):
        rows = slice(c * PERM_CHUNK, (c + 1) * PERM_CHUNK)
        ylo_ref[rows, :], yhi_ref[rows, :] = _unpack_pairs(local[rows, :])
    sel = sel_ref[...]
    f = f + jnp.concatenate([_dot(sel, ylo_ref[...]), _dot(sel, yhi_ref[...])], axis=-1)
    ci = _cond_row(i, tm)
    gate2 = mod_ref[pl.ds(ci, 1), 5 * D_MODEL:6 * D_MODEL]
    out = _layer_norm(ALPHA * x1_ref[...] + gate2 * f, g_ref[...], b_ref[...])

    @pl.when(i < T_CTX // tm)
    def _():
        outc_ref[...] = out

    @pl.when(i >= T_CTX // tm)
    def _():
        outl_ref[...] = out

    @pl.when(i == N_TOK_TILES - 1)
    def _():
        drain(1 - buf)


def _combine(x1, h2, lslot_rows, wsel_rows, unit_tab, ys, w_sgu_bf16, w_sdown_bf16, mod, ln_g, ln_b):
    n_ctx = T_CTX // TM_TOK
    row_map = lambda i, tab: (i, 0)
    const = lambda i, tab: (0, 0)
    return pl.pallas_call(
        _combine_kernel,
        out_shape=(jax.ShapeDtypeStruct((T_CTX, D_MODEL), F32),
                   jax.ShapeDtypeStruct((T_LAT, D_MODEL), F32)),
        grid_spec=pltpu.PrefetchScalarGridSpec(
            num_scalar_prefetch=1,
            grid=(N_TOK_TILES,),
            in_specs=[
                pl.BlockSpec((TM_TOK, D_MODEL), row_map),
                pl.BlockSpec((TM_TOK, D_MODEL), row_map),
                pl.BlockSpec((TM_TOK, SUBLANES), row_map),
                pl.BlockSpec((TM_TOK, SUBLANES), row_map),
                pl.BlockSpec(memory_space=pl.ANY),
                pl.BlockSpec((D_MODEL, 2 * D_SHARED), const),
                pl.BlockSpec((D_SHARED, D_MODEL), const),
                pl.BlockSpec((N_COND, 6 * D_MODEL), const),
                pl.BlockSpec((1, D_MODEL), const),
                pl.BlockSpec((1, D_MODEL), const),
            ],
            out_specs=(pl.BlockSpec((TM_TOK, D_MODEL), lambda i, tab: (jnp.minimum(i, n_ctx - 1), 0)),
                       pl.BlockSpec((TM_TOK, D_MODEL), lambda i, tab: (jnp.maximum(i - n_ctx, 0), 0))),
            scratch_shapes=[
                pltpu.VMEM((2, LOCAL_ROWS, PACK_W), I32),
                pltpu.VMEM((TM_TOK, LOCAL_ROWS), BF16),
                pltpu.VMEM((LOCAL_ROWS, PACK_W), BF16),
                pltpu.VMEM((LOCAL_ROWS, PACK_W), BF16),
                pltpu.SemaphoreType.DMA((2,)),
            ],
        ),
        compiler_params=pltpu.CompilerParams(
            dimension_semantics=("arbitrary",), vmem_limit_bytes=VMEM_LIMIT),
        name="moe_combine",
    )(unit_tab, x1, h2, lslot_rows, wsel_rows, ys, w_sgu_bf16, w_sdown_bf16, mod, ln_g, ln_b)


def _lane_rows(v, width):
    return jnp.broadcast_to(v.astype(F32)[:, None, None], (v.shape[0], 1, width))


def kernel(x_prompt, x_sample, state_ret_fwd, state_ret_bwd, cache_win_k, cache_win_v, cache_na_k, cache_na_v, c, c_ctx, w_in, w_out, ret_decay_fwd, ret_decay_bwd, ret_gn_g, ret_gn_b, win_sink, na_rpb, w_mod, b_mod, ln1_g, ln1_b, ln2_g, ln2_b, w_router, b_router, w_expert_gu, w_expert_down, w_shared_gu, w_shared_down):
    cond = jnp.concatenate(
        [c_ctx[None, :], c, jnp.zeros((N_COND - 1 - DEC_BATCH, D_MODEL), F32)], axis=0)
    mod_all = _modulation(cond, w_mod, b_mod)
    cos_t, sin_t = _rope_tables()

    x_ctx = x_prompt.reshape(T_CTX, D_MODEL)
    x_lat = x_sample.reshape(T_LAT, D_MODEL)
    sf_l, sb_l, caches = [], [], ()
    for l in range(DEPTH):
        mod = mod_all[l]
        pa, pb, pc, *caches = _in_projection(x_ctx, x_lat, mod, w_in[l].astype(BF16), tuple(caches))
        decf_s, decb_s = _lane_rows(ret_decay_fwd[l], SEQ), _lane_rows(ret_decay_bwd[l], SEQ)
        gng, gnb = ret_gn_g[l][None, :], ret_gn_b[l][None, :]
        mix_c, st_f, st_b = _ctx_mixers(pa, pb, pc, win_sink[l], decf_s, decb_s, gng, gnb)
        sf_l.append(st_f)
        sb_l.append(st_b)
        ya = _lat_retention(pa, state_ret_fwd, state_ret_bwd, l,
                            _lane_rows(ret_decay_fwd[l], RET_CHUNK), _lane_rows(ret_decay_bwd[l], RET_CHUNK),
                            gng, gnb)
        yb = _lat_window_attn(pb, cache_win_k, cache_win_v, l, win_sink[l], cos_t, sin_t)
        yc = _lat_na_attn(pc, cache_na_k, cache_na_v, l, _na_maskbias(na_rpb[l]))

        wr_hi = w_router[l].astype(BF16)
        wr_lo = (w_router[l] - wr_hi.astype(F32)).astype(BF16)
        x1, h2, eidx, wsel, rank, counts = _post_mixer(
            x_ctx, x_lat, mix_c, ya, yb, yc, w_out[l].astype(BF16), mod, ln1_g[l][None, :],
            ln1_b[l][None, :], jnp.concatenate([wr_hi, wr_lo], axis=1), b_router[l][:, None])
        lslot, unit_tab, gend = _slot_plan(eidx, rank, counts)
        xs = _dispatch(h2, lslot, unit_tab, gend)
        ys = _experts(xs, w_expert_gu, w_expert_down, l, gend)
        x_ctx, x_lat = _combine(x1, h2, lslot.T, wsel.T, unit_tab, ys, w_shared_gu[l].astype(BF16),
                                w_shared_down[l].astype(BF16), mod, ln2_g[l][None, :], ln2_b[l][None, :])

    y_prompt = x_ctx.reshape(BATCH, SEQ, D_MODEL)
    y_sample = x_lat.reshape(DEC_BATCH, DEC_SEQ, D_MODEL)
    new_sf = jnp.stack(sf_l, axis=1)
    new_sb = jnp.stack(sb_l, axis=1)

    return (y_prompt, y_sample, new_sf, new_sb, *caches)
```

```python
import functools

import numpy as np
import jax
import jax.numpy as jnp
from jax import lax
from jax.experimental import pallas as pl
from jax.experimental.pallas import tpu as pltpu

F32 = jnp.float32
BF16 = jnp.bfloat16
I32 = jnp.int32

D_MODEL = 1024
BATCH = 32
SEQ = 256
DEPTH = 2
DEC_BATCH = 4
DEC_SEQ = 2048
PAST_LEN = 256
GRID_W = 64
HEAD_DIM = 64
ATTN_SCALE = HEAD_DIM ** -0.5
H_A = 4
W_A = H_A * HEAD_DIM
GN_EPS = 1e-5
H_B = 6
KV_B = 2
W_B = H_B * HEAD_DIM
KV_W_B = KV_B * HEAD_DIM
WINDOW = 128
ROPE_BASE = 10000.0
H_C = 6
W_C = H_C * HEAD_DIM
NA_ROWS = 8
NA_COLS = 16
IN_WIDTH = 4 * W_A + W_B + 2 * KV_W_B + 3 * W_C
N_EXPERTS = 64
TOP_K = 6
N_GROUPS = 8
GROUP_SIZE = N_EXPERTS // N_GROUPS
TOPK_GROUPS = 4
D_EXPERT = 256
D_SHARED = 256
ROUTE_SCALE = 2.5
ALPHA = (2 * DEPTH) ** 0.25
LN_EPS = 1e-5
NEG_INF = -1e30
PICKED = -3e38

T_CTX = BATCH * SEQ
T_LAT = DEC_BATCH * DEC_SEQ
T_ALL = T_CTX + T_LAT
N_COND = 8

PA_W = 4 * W_A
PB_W = W_B + 2 * KV_W_B
PC_W = 3 * W_C

LANES = 128
SUBLANES = 8
VMEM_LIMIT = 56 * 1024 * 1024

TM_PROJ = 512
TM_TOK = 256
RET_CHUNK = 256
CTX_STACK = 3
NA_STACK = 6
WIN_QB = 256
WIN_BAND = WIN_QB + 2 * WINDOW
NA_TILE_ROWS = 4
NA_KEY_ROWS = 11
BM = 512
X_AHEAD = 3
X_BUFS = X_AHEAD + 1
UNIT = SUBLANES
UNITS_PER_BLOCK = BM // UNIT
PERM_CHUNK = 256
SEL_ROWS = 16
N_TOK_TILES = T_ALL // TM_TOK
LOCAL_ROWS = -(-(TM_TOK * TOP_K + N_EXPERTS * (UNIT - 1)) // PERM_CHUNK) * PERM_CHUNK
MAX_UNITS = LOCAL_ROWS // UNIT
N_ASSIGN = T_ALL * TOP_K
N_BLOCKS = -(-(N_ASSIGN + N_TOK_TILES * N_EXPERTS * (UNIT - 1) + N_EXPERTS * (BM - 1)) // BM)
SPARE_BLOCKS = 2 * -(-LOCAL_ROWS // BM)
SPARE_UNIT0 = N_BLOCKS * UNITS_PER_BLOCK
N_BLOCKS_ALL = N_BLOCKS + SPARE_BLOCKS
N_SLOTS = N_BLOCKS_ALL * BM


def _dot(a, b):
    return jnp.dot(a, b, preferred_element_type=F32)


def _dot_nt(a, b):
    return lax.dot_general(a, b, (((1,), (1,)), ((), ())), preferred_element_type=F32)


def _silu(x):
    return x * jax.nn.sigmoid(x)


def _log_sigmoid(x):
    return jnp.minimum(x, 0.0) - jnp.log(1.0 + jnp.exp(-jnp.abs(x)))


def _cond_row(tile, tile_rows):
    n_ctx = T_CTX // tile_rows
    per_lat = DEC_SEQ // tile_rows
    return jnp.where(tile < n_ctx, 0, 1 + (tile - n_ctx) // per_lat)


def _layer_norm(x, g, b):
    mu = jnp.mean(x, -1, keepdims=True)
    xc = x - mu
    var = jnp.mean(xc * xc, -1, keepdims=True)
    return xc * lax.rsqrt(var + LN_EPS) * g + b


MOD_TN = 1536


def _mod_kernel(cond_ref, w_ref, b_ref, o_ref):
    s = _silu(cond_ref[...])
    s_hi = s.astype(BF16)
    s_lo = (s - s_hi.astype(F32)).astype(BF16)
    w = w_ref[0]
    w_hi = w.astype(BF16)
    w_lo = (w - w_hi.astype(F32)).astype(BF16)
    o_ref[0] = _dot(s_hi, w_hi) + _dot(s_lo, w_hi) + _dot(s_hi, w_lo) + b_ref[0]


def _modulation(cond, w_mod, b_mod):
    n_out = 6 * D_MODEL
    return pl.pallas_call(
        _mod_kernel,
        out_shape=jax.ShapeDtypeStruct((DEPTH, N_COND, n_out), F32),
        grid=(DEPTH, n_out // MOD_TN),
        in_specs=[
            pl.BlockSpec((N_COND, D_MODEL), lambda l, j: (0, 0)),
            pl.BlockSpec((1, D_MODEL, MOD_TN), lambda l, j: (l, 0, j)),
            pl.BlockSpec((1, 1, MOD_TN), lambda l, j: (l, 0, j)),
        ],
        out_specs=pl.BlockSpec((1, N_COND, MOD_TN), lambda l, j: (l, 0, j)),
        compiler_params=pltpu.CompilerParams(
            dimension_semantics=("arbitrary", "arbitrary"), vmem_limit_bytes=VMEM_LIMIT),
        name="modulation",
    )(cond, w_mod, b_mod.reshape(DEPTH, 1, n_out))


SEQ_PER_PROJ = TM_PROJ // SEQ


def _inproj_kernel(n_prev, xc_ref, xl_ref, mod_ref, w_ref, *refs):
    prev_refs = refs[:4] if n_prev else ()
    pa_ref, pb_ref, pc_ref, wk_ref, wv_ref, nk_ref, nv_ref = refs[len(prev_refs):]
    i = pl.program_id(0)
    ci = _cond_row(i, TM_PROJ)
    sh = mod_ref[pl.ds(ci, 1), 0:D_MODEL]
    sc = mod_ref[pl.ds(ci, 1), D_MODEL:2 * D_MODEL]
    x = jnp.where(i < T_CTX // TM_PROJ, xc_ref[...], xl_ref[...])
    h = x * (1.0 + sc) + sh
    p = _dot(h.astype(BF16), w_ref[...])
    pa_ref[...] = p[:, 0:PA_W].astype(BF16)
    pb_ref[...] = p[:, PA_W:PA_W + PB_W].astype(BF16)
    pc_ref[...] = p[:, PA_W + PB_W:IN_WIDTH].astype(BF16)

    @pl.when(i < T_CTX // TM_PROJ)
    def _():
        targets = ((wk_ref, PA_W + W_B, KV_B), (wv_ref, PA_W + W_B + KV_W_B, KV_B),
                   (nk_ref, PA_W + PB_W + W_C, H_C), (nv_ref, PA_W + PB_W + 2 * W_C, H_C))
        for j, (ref, col0, n_heads) in enumerate(targets):
            if n_prev:
                ref[:, 0:n_prev] = prev_refs[j][...]
            for s in range(SEQ_PER_PROJ):
                for hh in range(n_heads):
                    ref[s, n_prev, hh] = p[s * SEQ:(s + 1) * SEQ,
                                           col0 + hh * HEAD_DIM:col0 + (hh + 1) * HEAD_DIM]


def _in_projection(x_ctx, x_lat, mod, w_in_bf16, earlier):
    n_ctx_tiles = T_CTX // TM_PROJ
    n_prev = earlier[0].shape[1] if earlier else 0

    def cache_spec(n_layers, n_heads):
        return pl.BlockSpec((SEQ_PER_PROJ, n_layers, n_heads, SEQ, HEAD_DIM),
                            lambda i: (jnp.minimum(i, n_ctx_tiles - 1), 0, 0, 0, 0))

    cache_heads = (KV_B, KV_B, H_C, H_C)
    return pl.pallas_call(
        functools.partial(_inproj_kernel, n_prev),
        out_shape=(
            jax.ShapeDtypeStruct((T_ALL, PA_W), BF16),
            jax.ShapeDtypeStruct((T_ALL, PB_W), BF16),
            jax.ShapeDtypeStruct((T_ALL, PC_W), BF16),
        ) + tuple(jax.ShapeDtypeStruct((BATCH, n_prev + 1, nh, SEQ, HEAD_DIM), F32) for nh in cache_heads),
        grid=(T_ALL // TM_PROJ,),
        in_specs=[
            pl.BlockSpec((TM_PROJ, D_MODEL), lambda i: (jnp.minimum(i, n_ctx_tiles - 1), 0)),
            pl.BlockSpec((TM_PROJ, D_MODEL), lambda i: (jnp.maximum(i - n_ctx_tiles, 0), 0)),
            pl.BlockSpec((N_COND, 6 * D_MODEL), lambda i: (0, 0)),
            pl.BlockSpec((D_MODEL, IN_WIDTH), lambda i: (0, 0)),
        ] + [cache_spec(n_prev, nh) for nh in cache_heads if n_prev],
        out_specs=(
            pl.BlockSpec((TM_PROJ, PA_W), lambda i: (i, 0)),
            pl.BlockSpec((TM_PROJ, PB_W), lambda i: (i, 0)),
            pl.BlockSpec((TM_PROJ, PC_W), lambda i: (i, 0)),
        ) + tuple(cache_spec(n_prev + 1, nh) for nh in cache_heads),
        compiler_params=pltpu.CompilerParams(
            dimension_semantics=("arbitrary",), vmem_limit_bytes=VMEM_LIMIT),
        name="in_projection",
    )(x_ctx, x_lat, mod, w_in_bf16, *earlier)


def _decay_matrix(lg_f, lg_b, n):
    row = lax.broadcasted_iota(I32, (n, n), 0)
    col = lax.broadcasted_iota(I32, (n, n), 1)
    diff = (row - col).astype(F32)
    fwd = jnp.where(diff >= 0, jnp.exp(lg_f * jnp.maximum(diff, 0.0)), 0.0)
    bwd = jnp.where(diff <= 0, jnp.exp(lg_b * jnp.maximum(-diff, 0.0)), 0.0)
    return (fwd + bwd) * ATTN_SCALE


def _stacked_softmax_attend(scores, values, extra_logit=None):
    rows = scores[0].shape[0]
    s = jnp.concatenate(scores, axis=0)
    m = jnp.max(s, -1, keepdims=True)
    if extra_logit is not None:
        m = jnp.maximum(m, extra_logit)
    p = jnp.exp(s - m)
    den = jnp.sum(p, -1, keepdims=True)
    if extra_logit is not None:
        den = den + jnp.exp(extra_logit - m)
    p = p.astype(BF16)
    return [_dot(p[h * rows:(h + 1) * rows], v) / den[h * rows:(h + 1) * rows]
            for h, v in enumerate(values)]


def _lane_xor_matrix(width, distance):
    r = lax.broadcasted_iota(I32, (width, width), 0)
    c = lax.broadcasted_iota(I32, (width, width), 1)
    return jnp.where((r ^ distance) == c, 1.0, 0.0).astype(BF16)


def _ctx_mixer_kernel(sink_ref, pa_ref, pb_ref, pc_ref, decf_ref, decb_ref, gng_ref, gnb_ref,
                      mix_ref, sf_ref, sb_ref, dmat_ref, zf_ref, zb_ref):
    n = SEQ
    hd = HEAD_DIM

    @pl.when(pl.program_id(0) == 0)
    def _():
        pos = lax.broadcasted_iota(I32, (n, hd), 0).astype(F32)
        zf, zb = [], []
        for h in range(H_A):
            lg_f = _log_sigmoid(decf_ref[h])
            lg_b = _log_sigmoid(decb_ref[h])
            dmat_ref[h] = _decay_matrix(lg_f, lg_b, n)
            zf.append(jnp.exp(lg_f[:, 0:hd] * (n - 1.0 - pos)) * ATTN_SCALE)
            zb.append(jnp.exp(lg_b[:, 0:hd] * pos) * ATTN_SCALE)
        for p in range(H_A // 2):
            zf_ref[p] = jnp.concatenate(zf[2 * p:2 * p + 2], axis=1)
            zb_ref[p] = jnp.concatenate(zb[2 * p:2 * p + 2], axis=1)

    low = lax.broadcasted_iota(I32, (n, LANES), 1) < hd

    def own_half(x, h):
        zero = jnp.zeros_like(x)
        return jnp.where(low, x, zero) if h % 2 == 0 else jnp.where(low, zero, x)

    def merge(first, second):
        return jnp.where(low, first, second)

    def pair_cols(ref, base, p):
        return ref[:, base + p * LANES:base + (p + 1) * LANES]

    def head_mean(x):
        first = jnp.sum(jnp.where(low, x, 0.0), -1, keepdims=True)
        second = jnp.sum(jnp.where(low, 0.0, x), -1, keepdims=True)
        return merge(first, second) * (1.0 / hd)

    for p in range(H_A // 2):
        q_pair, k_pair = pair_cols(pa_ref, 0, p), pair_cols(pa_ref, W_A, p)
        v_pair, gate_pair = pair_cols(pa_ref, 2 * W_A, p), pair_cols(pa_ref, 3 * W_A, p)
        outs = []
        for h in (2 * p, 2 * p + 1):
            a = _dot_nt(own_half(q_pair, h), k_pair)
            outs.append(_dot((a * dmat_ref[h]).astype(BF16), v_pair))
        o = merge(outs[0], outs[1])
        kf = k_pair.astype(F32)
        for st_ref, z_ref in ((sf_ref, zf_ref), (sb_ref, zb_ref)):
            st = _dot((kf * z_ref[p]).T.astype(BF16), v_pair)
            st_ref[0, 2 * p] = st[0:hd, 0:hd]
            st_ref[0, 2 * p + 1] = st[hd:2 * hd, hd:2 * hd]
        mu = head_mean(o)
        oc = o - mu
        var = head_mean(oc * oc)
        on = oc * lax.rsqrt(var + GN_EPS) * pair_cols(gng_ref, 0, p) + pair_cols(gnb_ref, 0, p)
        mix_ref[:, p * LANES:(p + 1) * LANES] = (on * _silu(gate_pair.astype(F32))).astype(BF16)

    group = H_B // KV_B
    swap = _lane_xor_matrix(LANES, hd)
    kv_k = pb_ref[:, W_B:W_B + KV_W_B]
    kv_v = pb_ref[:, W_B + KV_W_B:W_B + 2 * KV_W_B]
    scores = []
    for hh in range(H_B):
        q = own_half(pair_cols(pb_ref, 0, hh // 2), hh) * ATTN_SCALE
        if hh % 2 != hh // group:
            q = _dot(q, swap).astype(BF16)
        scores.append(_dot_nt(q, kv_k))
    sinks = jnp.concatenate([jnp.full((n, 1), sink_ref[hh], F32) for hh in range(H_B)], axis=0)
    outs = []
    stacked = []
    for g0 in range(0, H_B, CTX_STACK):
        stacked += _stacked_softmax_attend(scores[g0:g0 + CTX_STACK], [kv_v] * CTX_STACK,
                                           sinks[g0 * n:(g0 + CTX_STACK) * n])
    for hh, o in enumerate(stacked):
        o = o.astype(BF16)
        outs.append(_dot(o, swap).astype(BF16) if hh % 2 != hh // group else o)
    for p in range(H_B // 2):
        mix_ref[:, W_A + p * LANES:W_A + (p + 1) * LANES] = merge(outs[2 * p], outs[2 * p + 1])

    scores, values = [], []
    for hh in range(H_C):
        q = own_half(pair_cols(pc_ref, 0, hh // 2), hh) * ATTN_SCALE
        scores.append(_dot_nt(q, pair_cols(pc_ref, W_C, hh // 2)))
        values.append(pair_cols(pc_ref, 2 * W_C, hh // 2))
    outs = []
    for g0 in range(0, H_C, CTX_STACK):
        outs += _stacked_softmax_attend(scores[g0:g0 + CTX_STACK], values[g0:g0 + CTX_STACK])
    for p in range(H_C // 2):
        mix_ref[:, W_A + W_B + p * LANES:W_A + W_B + (p + 1) * LANES] = merge(
            outs[2 * p], outs[2 * p + 1]).astype(BF16)


def _ctx_mixers(pa, pb, pc, sink, decf, decb, gng, gnb):
    return pl.pallas_call(
        _ctx_mixer_kernel,
        out_shape=(
            jax.ShapeDtypeStruct((T_CTX, D_MODEL), BF16),
            jax.ShapeDtypeStruct((BATCH, H_A, HEAD_DIM, HEAD_DIM), F32),
            jax.ShapeDtypeStruct((BATCH, H_A, HEAD_DIM, HEAD_DIM), F32),
        ),
        grid=(BATCH,),
        in_specs=[
            pl.BlockSpec(memory_space=pltpu.SMEM),
            pl.BlockSpec((SEQ, PA_W), lambda b: (b, 0)),
            pl.BlockSpec((SEQ, PB_W), lambda b: (b, 0)),
            pl.BlockSpec((SEQ, PC_W), lambda b: (b, 0)),
            pl.BlockSpec((H_A, 1, SEQ), lambda b: (0, 0, 0)),
            pl.BlockSpec((H_A, 1, SEQ), lambda b: (0, 0, 0)),
            pl.BlockSpec((1, W_A), lambda b: (0, 0)),
            pl.BlockSpec((1, W_A), lambda b: (0, 0)),
        ],
        out_specs=(
            pl.BlockSpec((SEQ, D_MODEL), lambda b: (b, 0)),
            pl.BlockSpec((1, H_A, HEAD_DIM, HEAD_DIM), lambda b: (b, 0, 0, 0)),
            pl.BlockSpec((1, H_A, HEAD_DIM, HEAD_DIM), lambda b: (b, 0, 0, 0)),
        ),
        scratch_shapes=[
            pltpu.VMEM((H_A, SEQ, SEQ), F32),
            pltpu.VMEM((H_A // 2, SEQ, LANES), F32),
            pltpu.VMEM((H_A // 2, SEQ, LANES), F32),
        ],
        compiler_params=pltpu.CompilerParams(
            dimension_semantics=("arbitrary",), vmem_limit_bytes=VMEM_LIMIT),
        name="ctx_mixers",
    )(sink, pa, pb, pc, decf, decb, gng, gnb)


def _lat_ret_kernel(pa_ref, stf_ref, stb_ref, decf_ref, decb_ref, gng_ref, gnb_ref, ya_ref):
    c = RET_CHUNK
    hd = HEAD_DIM
    n_chunks = DEC_SEQ // c
    pos = lax.broadcasted_iota(I32, (c, LANES), 0).astype(F32)
    low = lax.broadcasted_iota(I32, (c, LANES), 1) < hd
    same_head = (lax.broadcasted_iota(I32, (LANES, LANES), 0) // hd
                 == lax.broadcasted_iota(I32, (LANES, LANES), 1) // hd)
    zero_blk = jnp.zeros((hd, hd), F32)

    def block_diag(a, b):
        return jnp.concatenate([jnp.concatenate([a, zero_blk], axis=1),
                                jnp.concatenate([zero_blk, b], axis=1)], axis=0)

    def head_mean(x):
        first = jnp.sum(jnp.where(low, x, 0.0), -1, keepdims=True)
        second = jnp.sum(jnp.where(low, 0.0, x), -1, keepdims=True)
        return jnp.where(low, first, second) * (1.0 / hd)

    for p in range(H_A // 2):
        lg_f = [_log_sigmoid(decf_ref[h]) for h in (2 * p, 2 * p + 1)]
        lg_b = [_log_sigmoid(decb_ref[h]) for h in (2 * p, 2 * p + 1)]
        dmat = [_decay_matrix(lg_f[t], lg_b[t], c) for t in range(2)]
        lf = jnp.where(low[0:1], lg_f[0][:, 0:LANES], lg_f[1][:, 0:LANES])
        lb = jnp.where(low[0:1], lg_b[0][:, 0:LANES], lg_b[1][:, 0:LANES])
        zf = jnp.exp(lf * (c - 1.0 - pos)) * ATTN_SCALE
        zb = jnp.exp(lb * pos) * ATTN_SCALE
        xf = jnp.exp(lf * (pos + 1.0))
        xb = jnp.exp(lb * (c - pos))
        gcf = jnp.exp(lf * float(c))
        gcb = jnp.exp(lb * float(c))
        pair = slice(p * LANES, (p + 1) * LANES)

        def chunk(i, base):
            return pa_ref[i * c:(i + 1) * c, base + p * LANES:base + (p + 1) * LANES]

        kv_f, kv_b = [], []
        for i in range(n_chunks):
            kf = chunk(i, W_A).astype(F32)
            v = chunk(i, 2 * W_A)
            kv_f.append(jnp.where(same_head, _dot((kf * zf).T.astype(BF16), v), 0.0))
            kv_b.append(jnp.where(same_head, _dot((kf * zb).T.astype(BF16), v), 0.0))

        s = block_diag(stf_ref[0, 0, 2 * p], stf_ref[0, 0, 2 * p + 1])
        seen_f = []
        for i in range(n_chunks):
            seen_f.append(s)
            s = gcf * s + kv_f[i]
        s = block_diag(stb_ref[0, 0, 2 * p], stb_ref[0, 0, 2 * p + 1])
        seen_b = [None] * n_chunks
        for i in reversed(range(n_chunks)):
            seen_b[i] = s
            s = gcb * s + kv_b[i]

        for i in range(n_chunks):
            q_pair, k_pair, v_pair = chunk(i, 0), chunk(i, W_A), chunk(i, 2 * W_A)
            inner = []
            for t in range(2):
                zero = jnp.zeros_like(q_pair)
                q = jnp.where(low, q_pair, zero) if t == 0 else jnp.where(low, zero, q_pair)
                inner.append(_dot((_dot_nt(q, k_pair) * dmat[t]).astype(BF16), v_pair))
            qf = q_pair.astype(F32)
            lhs = jnp.concatenate([(qf * xf).astype(BF16), (qf * xb).astype(BF16)], axis=1)
            rhs = jnp.concatenate([seen_f[i], seen_b[i]], axis=0).astype(BF16)
            o = jnp.where(low, inner[0], inner[1]) + _dot(lhs, rhs)
            mu = head_mean(o)
            oc = o - mu
            var = head_mean(oc * oc)
            on = oc * lax.rsqrt(var + GN_EPS) * gng_ref[:, pair] + gnb_ref[:, pair]
            y = on * _silu(chunk(i, 3 * W_A).astype(F32))
            ya_ref[i * c:(i + 1) * c, pair] = y.astype(BF16)


def _lat_retention(pa, st_f, st_b, layer, decf, decb, gng, gnb):
    lat0 = T_CTX // DEC_SEQ
    st_spec = pl.BlockSpec((1, 1, H_A, HEAD_DIM, HEAD_DIM), lambda b: (b, layer, 0, 0, 0))
    return pl.pallas_call(
        _lat_ret_kernel,
        out_shape=jax.ShapeDtypeStruct((T_LAT, W_A), BF16),
        grid=(DEC_BATCH,),
        in_specs=[
            pl.BlockSpec((DEC_SEQ, PA_W), lambda b: (lat0 + b, 0)),
            st_spec, st_spec,
            pl.BlockSpec((H_A, 1, RET_CHUNK), lambda b: (0, 0, 0)),
            pl.BlockSpec((H_A, 1, RET_CHUNK), lambda b: (0, 0, 0)),
            pl.BlockSpec((1, W_A), lambda b: (0, 0)),
            pl.BlockSpec((1, W_A), lambda b: (0, 0)),
        ],
        out_specs=pl.BlockSpec((DEC_SEQ, W_A), lambda b: (b, 0)),
        compiler_params=pltpu.CompilerParams(
            dimension_semantics=("arbitrary",), vmem_limit_bytes=VMEM_LIMIT),
        name="lat_retention",
    )(pa, st_f, st_b, decf, decb, gng, gnb)


def _rope(x, cos, sin_signed, swap):
    return x.astype(F32) * cos + _dot(x, swap) * sin_signed


def _lat_win_kernel(sink_ref, pq_ref, pseq_ref, kctx_ref, vctx_ref, cos_ref, sin_ref, yb_ref,
                    krope_ref, kc_ref, vc_ref):
    n = pl.program_id(1)
    hd = HEAD_DIM
    qb = WIN_QB
    band = WIN_BAND
    group = H_B // KV_B
    rot = _lane_xor_matrix(LANES, hd // 2)
    swap = _lane_xor_matrix(LANES, hd)

    @pl.when(n == 0)
    def _():
        k = pseq_ref[:, W_B:W_B + KV_W_B]
        krope_ref[...] = _rope(k, cos_ref[...], sin_ref[...], rot).astype(BF16)
        kc_ref[...] = jnp.concatenate([kctx_ref[0, 0, j] for j in range(KV_B)], axis=1).astype(BF16)
        vc_ref[...] = jnp.concatenate([vctx_ref[0, 0, j] for j in range(KV_B)], axis=1).astype(BF16)

    q_rows = pl.ds(pl.multiple_of(n * qb, qb), qb)
    cos_q = cos_ref[q_rows, :]
    sin_q = sin_ref[q_rows, :]
    low = lax.broadcasted_iota(I32, (qb, LANES), 1) < hd
    q_heads = []
    for p in range(H_B // 2):
        q_pair = _rope(pq_ref[:, p * LANES:(p + 1) * LANES], cos_q, sin_q, rot) * ATTN_SCALE
        for hh in (2 * p, 2 * p + 1):
            q = jnp.where(low, q_pair, 0.0) if hh % 2 == 0 else jnp.where(low, 0.0, q_pair)
            q = q.astype(BF16)
            q_heads.append(_dot(q, swap).astype(BF16) if hh % 2 != hh // group else q)

    ws = jnp.clip(n * qb - WINDOW, 0, DEC_SEQ - band)
    k_rows = pl.ds(pl.multiple_of(ws, WINDOW), band)
    q_pos = n * qb + lax.broadcasted_iota(I32, (group * qb, band), 0) % qb
    k_pos = ws + lax.broadcasted_iota(I32, (group * qb, band), 1)
    valid = jnp.abs(k_pos - q_pos) <= WINDOW
    head_of_row = lax.broadcasted_iota(I32, (group * qb, 1), 0) // qb
    kw = krope_ref[k_rows, :]
    vw = pseq_ref[k_rows, W_B + KV_W_B:W_B + 2 * KV_W_B]
    outs = []
    for j in range(KV_B):
        heads = [j * group + g for g in range(group)]
        qs = jnp.concatenate([q_heads[hh] for hh in heads], axis=0)
        s_loc = jnp.where(valid, _dot_nt(qs, kw), NEG_INF)
        s_ctx = _dot_nt(qs, kc_ref[...])
        sink = jnp.zeros((group * qb, 1), F32)
        for g, hh in enumerate(heads):
            sink = jnp.where(head_of_row == g, sink_ref[hh], sink)
        m = jnp.maximum(jnp.maximum(jnp.max(s_loc, -1, keepdims=True),
                                    jnp.max(s_ctx, -1, keepdims=True)), sink)
        p_loc = jnp.exp(s_loc - m)
        p_ctx = jnp.exp(s_ctx - m)
        den = (jnp.sum(p_loc, -1, keepdims=True) + jnp.sum(p_ctx, -1, keepdims=True)
               + jnp.exp(sink - m))
        o = ((_dot(p_loc.astype(BF16), vw) + _dot(p_ctx.astype(BF16), vc_ref[...])) / den).astype(BF16)
        for g, hh in enumerate(heads):
            o_h = o[g * qb:(g + 1) * qb]
            outs.append(_dot(o_h, swap).astype(BF16) if hh % 2 != j else o_h)
    for p in range(H_B // 2):
        yb_ref[:, p * LANES:(p + 1) * LANES] = jnp.where(low, outs[2 * p], outs[2 * p + 1])


def _lat_window_attn(pb, cache_k, cache_v, layer, sink, cos_t, sin_t):
    n_blk = DEC_SEQ // WIN_QB
    lat_blk0 = T_CTX // WIN_QB
    lat_seq0 = T_CTX // DEC_SEQ
    ctx_spec = pl.BlockSpec((1, 1, KV_B, PAST_LEN, HEAD_DIM), lambda b, n: (b, layer, 0, 0, 0))
    return pl.pallas_call(
        _lat_win_kernel,
        out_shape=jax.ShapeDtypeStruct((T_LAT, W_B), BF16),
        grid=(DEC_BATCH, n_blk),
        in_specs=[
            pl.BlockSpec(memory_space=pltpu.SMEM),
            pl.BlockSpec((WIN_QB, PB_W), lambda b, n: (lat_blk0 + b * n_blk + n, 0)),
            pl.BlockSpec((DEC_SEQ, PB_W), lambda b, n: (lat_seq0 + b, 0)),
            ctx_spec, ctx_spec,
            pl.BlockSpec((DEC_SEQ, LANES), lambda b, n: (0, 0)),
            pl.BlockSpec((DEC_SEQ, LANES), lambda b, n: (0, 0)),
        ],
        out_specs=pl.BlockSpec((WIN_QB, W_B), lambda b, n: (b * n_blk + n, 0)),
        scratch_shapes=[
            pltpu.VMEM((DEC_SEQ, KV_W_B), BF16),
            pltpu.VMEM((PAST_LEN, KV_W_B), BF16),
            pltpu.VMEM((PAST_LEN, KV_W_B), BF16),
        ],
        compiler_params=pltpu.CompilerParams(
            dimension_semantics=("arbitrary", "arbitrary"), vmem_limit_bytes=VMEM_LIMIT),
        name="lat_window_attn",
    )(sink, pb, pb, cache_k, cache_v, cos_t, sin_t)


NA_Q = NA_TILE_ROWS * GRID_W
NA_K = NA_KEY_ROWS * GRID_W
NA_TILES = DEC_SEQ // NA_Q
LAT_ROWS = DEC_SEQ // GRID_W


def _na_window_start(tile):
    return jnp.clip(tile * NA_TILE_ROWS - NA_ROWS // 2, 0, LAT_ROWS - NA_KEY_ROWS)


def _lat_na_kernel(pq_ref, pseq_ref, kctx_ref, vctx_ref, bias_ref, yc_ref, kc_ref, vc_ref):
    t = pl.program_id(1)
    hd = HEAD_DIM
    nq = NA_Q

    @pl.when(t == 0)
    def _():
        for p in range(H_C // 2):
            kc_ref[p] = jnp.concatenate([kctx_ref[0, 0, 2 * p], kctx_ref[0, 0, 2 * p + 1]], axis=1).astype(BF16)
            vc_ref[p] = jnp.concatenate([vctx_ref[0, 0, 2 * p], vctx_ref[0, 0, 2 * p + 1]], axis=1).astype(BF16)

    k_rows = pl.ds(pl.multiple_of(_na_window_start(t) * GRID_W, GRID_W), NA_K)
    low = lax.broadcasted_iota(I32, (nq, LANES), 1) < hd
    loc, ctx = [], []
    for hh in range(H_C):
        p = hh // 2
        q_pair = pq_ref[:, p * LANES:(p + 1) * LANES] * ATTN_SCALE
        zero = jnp.zeros_like(q_pair)
        q = jnp.where(low, q_pair, zero) if hh % 2 == 0 else jnp.where(low, zero, q_pair)
        loc.append(_dot_nt(q, pseq_ref[k_rows, W_C + p * LANES:W_C + (p + 1) * LANES]) + bias_ref[0, hh])
        ctx.append(_dot_nt(q, kc_ref[p]))
    outs = []
    for g0 in range(0, H_C, NA_STACK):
        s_loc = jnp.concatenate(loc[g0:g0 + NA_STACK], axis=0)
        s_ctx = jnp.concatenate(ctx[g0:g0 + NA_STACK], axis=0)
        m = jnp.maximum(jnp.max(s_loc, -1, keepdims=True), jnp.max(s_ctx, -1, keepdims=True))
        p_loc = jnp.exp(s_loc - m)
        p_ctx = jnp.exp(s_ctx - m)
        den = jnp.sum(p_loc, -1, keepdims=True) + jnp.sum(p_ctx, -1, keepdims=True)
        p_loc = p_loc.astype(BF16)
        p_ctx = p_ctx.astype(BF16)
        for hh in range(g0, g0 + NA_STACK):
            p = hh // 2
            rows = slice((hh - g0) * nq, (hh - g0 + 1) * nq)
            vw = pseq_ref[k_rows, 2 * W_C + p * LANES:2 * W_C + (p + 1) * LANES]
            outs.append((_dot(p_loc[rows], vw) + _dot(p_ctx[rows], vc_ref[p])) / den[rows])
    for p in range(H_C // 2):
        yc_ref[:, p * LANES:(p + 1) * LANES] = jnp.where(low, outs[2 * p], outs[2 * p + 1]).astype(BF16)


def _na_tile_type(t):
    return jnp.where(t == 0, 0, jnp.where(t == NA_TILES - 1, 2, 1))


def _lat_na_attn(pc, cache_k, cache_v, layer, maskbias):
    lat_tile0 = T_CTX // NA_Q
    lat_seq0 = T_CTX // DEC_SEQ
    ctx_spec = pl.BlockSpec((1, 1, H_C, PAST_LEN, HEAD_DIM), lambda b, t: (b, layer, 0, 0, 0))
    return pl.pallas_call(
        _lat_na_kernel,
        out_shape=jax.ShapeDtypeStruct((T_LAT, W_C), BF16),
        grid=(DEC_BATCH, NA_TILES),
        in_specs=[
            pl.BlockSpec((NA_Q, PC_W), lambda b, t: (lat_tile0 + b * NA_TILES + t, 0)),
            pl.BlockSpec((DEC_SEQ, PC_W), lambda b, t: (lat_seq0 + b, 0)),
            ctx_spec, ctx_spec,
            pl.BlockSpec((1, H_C, NA_Q, NA_K), lambda b, t: (_na_tile_type(t), 0, 0, 0)),
        ],
        out_specs=pl.BlockSpec((NA_Q, W_C), lambda b, t: (b * NA_TILES + t, 0)),
        scratch_shapes=[
            pltpu.VMEM((H_C // 2, PAST_LEN, LANES), BF16),
            pltpu.VMEM((H_C // 2, PAST_LEN, LANES), BF16),
        ],
        compiler_params=pltpu.CompilerParams(
            dimension_semantics=("arbitrary", "arbitrary"), vmem_limit_bytes=VMEM_LIMIT),
        name="lat_na_attn",
    )(pc, pc, cache_k, cache_v, maskbias)


def _na_block_index():
    out = np.zeros((3, NA_TILE_ROWS, NA_KEY_ROWS), np.int32)
    for ty, tile in enumerate((0, 1, NA_TILES - 1)):
        r = tile * NA_TILE_ROWS
        ws = int(np.clip(r - NA_ROWS // 2, 0, LAT_ROWS - NA_KEY_ROWS))
        for qq in range(NA_TILE_ROWS):
            qr = r + qq
            r0 = int(np.clip(qr - NA_ROWS // 2, 0, LAT_ROWS - NA_ROWS))
            for kk in range(NA_KEY_ROWS):
                kr = ws + kk
                out[ty, qq, kk] = kr - qr + NA_ROWS - 1 if r0 <= kr < r0 + NA_ROWS else 2 * NA_ROWS - 1
    return out


def _na_maskbias(rpb):
    qc = np.arange(GRID_W)[:, None]
    kc = np.arange(GRID_W)[None, :]
    c0 = np.clip(qc - NA_COLS // 2, 0, GRID_W - NA_COLS)
    col_ok = (kc >= c0) & (kc < c0 + NA_COLS)
    ci = np.clip(kc - qc + NA_COLS - 1, 0, 2 * NA_COLS - 2)
    onehot = (ci[None] == np.arange(2 * NA_COLS - 1)[:, None, None]).astype(np.float32)
    cols = jnp.einsum("hab,bqk->haqk", rpb, jnp.asarray(onehot), precision=lax.Precision.HIGHEST)
    cols = jnp.where(jnp.asarray(col_ok)[None, None], cols, NEG_INF)
    cols = jnp.concatenate([cols, jnp.full((H_C, 1, GRID_W, GRID_W), NEG_INF, F32)], axis=1)
    block_index = _na_block_index()

    def assemble(cols_ref, out_ref):
        for ty in range(3):
            for qq in range(NA_TILE_ROWS):
                for kk in range(NA_KEY_ROWS):
                    out_ref[ty, 0, qq * GRID_W:(qq + 1) * GRID_W, kk * GRID_W:(kk + 1) * GRID_W] = (
                        cols_ref[0, int(block_index[ty, qq, kk])])

    return pl.pallas_call(
        assemble,
        out_shape=jax.ShapeDtypeStruct((3, H_C, NA_Q, NA_K), F32),
        grid=(H_C,),
        in_specs=[pl.BlockSpec((1, 2 * NA_ROWS, GRID_W, GRID_W), lambda h: (h, 0, 0, 0))],
        out_specs=pl.BlockSpec((3, 1, NA_Q, NA_K), lambda h: (0, h, 0, 0)),
        compiler_params=pltpu.CompilerParams(dimension_semantics=("arbitrary",)),
        name="na_bias_assemble",
    )(cols)


def _rope_tables():
    t = np.arange(DEC_SEQ)
    n_freq = HEAD_DIM // 4
    inv = (ROPE_BASE ** (-np.arange(n_freq, dtype=np.float32) / n_freq)).astype(np.float32)
    row = (t // GRID_W).astype(np.float32)[:, None] * inv
    col = (t % GRID_W).astype(np.float32)[:, None] * inv
    ang = np.concatenate([row, col], -1)
    cos, sin = np.cos(ang), np.sin(ang)
    cos_h = np.concatenate([cos, cos], -1)
    sin_h = np.concatenate([-sin, sin], -1)
    reps = LANES // HEAD_DIM
    return (jnp.asarray(np.tile(cos_h, (1, reps)), F32), jnp.asarray(np.tile(sin_h, (1, reps)), F32))


def _first_index_of(mask, iota, sentinel):
    return jnp.min(jnp.where(mask, iota, sentinel), axis=0, keepdims=True)


def _route(logits, b_col):
    n = logits.shape[1]
    scores = jax.nn.sigmoid(logits)
    sel = scores + b_col
    io_g = lax.broadcasted_iota(I32, (GROUP_SIZE, n), 0)
    gs_rows = []
    for g in range(N_GROUPS):
        s = sel[g * GROUP_SIZE:(g + 1) * GROUP_SIZE]
        m1 = jnp.max(s, axis=0, keepdims=True)
        i1 = _first_index_of(s == m1, io_g, GROUP_SIZE)
        m2 = jnp.max(jnp.where(io_g == i1, PICKED, s), axis=0, keepdims=True)
        gs_rows.append(m1 + m2)
    gs = jnp.concatenate(gs_rows, axis=0)
    io_n = lax.broadcasted_iota(I32, (N_GROUPS, n), 0)
    gsel = jnp.zeros((N_GROUPS, n), F32)
    for _ in range(TOPK_GROUPS):
        mg = jnp.max(gs, axis=0, keepdims=True)
        gi = _first_index_of(gs == mg, io_n, N_GROUPS)
        hit = io_n == gi
        gsel = jnp.where(hit, 1.0, gsel)
        gs = jnp.where(hit, PICKED, gs)
    cand = jnp.concatenate(
        [jnp.where(gsel[g:g + 1] > 0.5, sel[g * GROUP_SIZE:(g + 1) * GROUP_SIZE], NEG_INF)
         for g in range(N_GROUPS)], axis=0)
    io_e = lax.broadcasted_iota(I32, (N_EXPERTS, n), 0)
    picks, raw = [], []
    for _ in range(TOP_K):
        mv = jnp.max(cand, axis=0, keepdims=True)
        ei = _first_index_of(cand == mv, io_e, N_EXPERTS)
        hit = io_e == ei
        picks.append((hit, ei))
        raw.append(jnp.sum(jnp.where(hit, scores, 0.0), axis=0, keepdims=True))
        cand = jnp.where(hit, PICKED, cand)
    return picks, raw


def _post_mixer_kernel(xc_ref, xl_ref, mixc_ref, ya_ref, yb_ref, yc_ref, wout_ref, mod_ref, g_ref, b_ref,
                       wr_ref, br_ref,
                       x1_ref, h2_ref, eidx_ref, wsel_ref, rank_ref, cnt_ref):
    i = pl.program_id(0)
    tm = TM_TOK

    @pl.when(i == 0)
    def _():
        cnt_ref[...] = jnp.zeros_like(cnt_ref)

    ci = _cond_row(i, tm)
    gate1 = mod_ref[pl.ds(ci, 1), 2 * D_MODEL:3 * D_MODEL]
    sh2 = mod_ref[pl.ds(ci, 1), 3 * D_MODEL:4 * D_MODEL]
    sc2 = mod_ref[pl.ds(ci, 1), 4 * D_MODEL:5 * D_MODEL]
    mix_lat = jnp.concatenate([ya_ref[...], yb_ref[...], yc_ref[...]], axis=-1)
    mix = jnp.where(i < T_CTX // tm, mixc_ref[...], mix_lat)
    y = _dot(mix, wout_ref[...])
    x = jnp.where(i < T_CTX // tm, xc_ref[...], xl_ref[...])
    x1 = _layer_norm(ALPHA * x + gate1 * y, g_ref[...], b_ref[...])
    x1_ref[...] = x1
    h2 = x1 * (1.0 + sc2) + sh2
    h_hi = h2.astype(BF16)
    h2_ref[...] = h_hi
    h_lo = (h2 - h_hi.astype(F32)).astype(BF16)
    both = (_dot(h_hi, wr_ref[...]) + _dot(h_lo, wr_ref[...])).T
    logits = both[0:N_EXPERTS] + both[N_EXPERTS:2 * N_EXPERTS]
    routed = [_route(logits[:, g * LANES:(g + 1) * LANES], br_ref[...]) for g in range(tm // LANES)]
    multi_g = []
    for picks, _ in routed:
        m = jnp.zeros((N_EXPERTS, LANES), F32)
        for hit, _ in picks:
            m = m + jnp.where(hit, 1.0, 0.0)
        multi_g.append(m)
    multi = jnp.concatenate(multi_g, axis=1)
    before = (lax.broadcasted_iota(I32, (tm, tm), 0) < lax.broadcasted_iota(I32, (tm, tm), 1))
    cum = _dot(multi.astype(BF16), jnp.where(before, 1.0, 0.0).astype(BF16))
    pad = jnp.zeros((SUBLANES - TOP_K, LANES), F32)
    for g, (picks, raw) in enumerate(routed):
        lanes = slice(g * LANES, (g + 1) * LANES)
        total = raw[0]
        for r in raw[1:]:
            total = total + r
        scale = ROUTE_SCALE / total
        cum_g = cum[:, lanes]
        eidx_ref[:, lanes] = jnp.concatenate([ei for _, ei in picks] + [pad.astype(I32)], axis=0)
        wsel_ref[:, lanes] = jnp.concatenate([r * scale for r in raw] + [pad], axis=0)
        rank_ref[:, lanes] = jnp.concatenate(
            [jnp.sum(jnp.where(hit, cum_g, 0.0), axis=0, keepdims=True) for hit, _ in picks] + [pad],
            axis=0).astype(I32)
    tile_lane = lax.broadcasted_iota(I32, (N_EXPERTS, LANES), 1)
    cnt_ref[...] = jnp.where(tile_lane == i, jnp.sum(multi, axis=1, keepdims=True), cnt_ref[...])


def _post_mixer(x_ctx, x_lat, mix_c, ya, yb, yc, w_out_bf16, mod, ln_g, ln_b, wr_split, b_router_col):
    n_ctx = T_CTX // TM_TOK
    ctx_map = lambda i: (jnp.minimum(i, n_ctx - 1), 0)
    lat_map = lambda i: (jnp.maximum(i - n_ctx, 0), 0)
    row_map = lambda i: (i, 0)
    const = lambda i: (0, 0)
    tok_map = lambda i: (0, i)
    return pl.pallas_call(
        _post_mixer_kernel,
        out_shape=(
            jax.ShapeDtypeStruct((T_ALL, D_MODEL), F32),
            jax.ShapeDtypeStruct((T_ALL, D_MODEL), BF16),
            jax.ShapeDtypeStruct((SUBLANES, T_ALL), I32),
            jax.ShapeDtypeStruct((SUBLANES, T_ALL), F32),
            jax.ShapeDtypeStruct((SUBLANES, T_ALL), I32),
            jax.ShapeDtypeStruct((N_EXPERTS, LANES), F32),
        ),
        grid=(N_TOK_TILES,),
        in_specs=[
            pl.BlockSpec((TM_TOK, D_MODEL), ctx_map),
            pl.BlockSpec((TM_TOK, D_MODEL), lat_map),
            pl.BlockSpec((TM_TOK, D_MODEL), ctx_map),
            pl.BlockSpec((TM_TOK, W_A), lat_map),
            pl.BlockSpec((TM_TOK, W_B), lat_map),
            pl.BlockSpec((TM_TOK, W_C), lat_map),
            pl.BlockSpec((D_MODEL, D_MODEL), const),
            pl.BlockSpec((N_COND, 6 * D_MODEL), const),
            pl.BlockSpec((1, D_MODEL), const),
            pl.BlockSpec((1, D_MODEL), const),
            pl.BlockSpec((D_MODEL, 2 * N_EXPERTS), const),
            pl.BlockSpec((N_EXPERTS, 1), const),
        ],
        out_specs=(
            pl.BlockSpec((TM_TOK, D_MODEL), row_map),
            pl.BlockSpec((TM_TOK, D_MODEL), row_map),
            pl.BlockSpec((SUBLANES, TM_TOK), tok_map),
            pl.BlockSpec((SUBLANES, TM_TOK), tok_map),
            pl.BlockSpec((SUBLANES, TM_TOK), tok_map),
            pl.BlockSpec((N_EXPERTS, LANES), const),
        ),
        compiler_params=pltpu.CompilerParams(
            dimension_semantics=("arbitrary",), vmem_limit_bytes=VMEM_LIMIT),
        name="post_mixer",
    )(x_ctx, x_lat, mix_c, ya, yb, yc, w_out_bf16, mod, ln_g, ln_b, wr_split, b_router_col)


PLAN_TILES = 4


def _plan_kernel(eidx_ref, rank_ref, nmat_ref, lslot_ref, unit_ref, gend_ref):
    step = pl.program_id(0)
    units = jnp.floor((nmat_ref[...] + (UNIT - 1.0)) * (1.0 / UNIT))
    units_bf = units.astype(BF16)
    earlier_e = (lax.broadcasted_iota(I32, (N_EXPERTS, N_EXPERTS), 1)
                 < lax.broadcasted_iota(I32, (N_EXPERTS, N_EXPERTS), 0))
    tri_e = jnp.where(earlier_e, 1.0, 0.0).astype(BF16)
    earlier_t = (lax.broadcasted_iota(I32, (LANES, LANES), 0) < lax.broadcasted_iota(I32, (LANES, LANES), 1))
    tri_t = jnp.where(earlier_t, 1.0, 0.0).astype(BF16)
    local_off = _dot(tri_e, units_bf)
    tile_off = _dot(units_bf, tri_t)
    per_expert = jnp.sum(units, axis=1, keepdims=True)
    blocks = jnp.floor((per_expert + (UNITS_PER_BLOCK - 1.0)) * (1.0 / UNITS_PER_BLOCK))
    blocks_l = jnp.broadcast_to(blocks, (N_EXPERTS, LANES))
    start_blk = _dot(tri_e, blocks_l.astype(BF16))
    end_blk = start_blk + blocks_l
    gend_ref[...] = (end_blk * BM).astype(I32)

    tile_lane = lax.broadcasted_iota(I32, (N_EXPERTS, LANES), 1)
    u = lax.broadcasted_iota(I32, (N_EXPERTS, MAX_UNITS), 1).astype(F32)
    io_e = lax.broadcasted_iota(I32, (N_EXPERTS, TM_TOK), 0)
    for s in range(PLAN_TILES):
        i = step * PLAN_TILES + s
        this_tile = tile_lane == i

        def column(a):
            return jnp.sum(jnp.where(this_tile, a, 0.0), axis=1, keepdims=True)

        lo, n_u = column(local_off), column(units)
        base_unit = start_blk[:, 0:1] * UNITS_PER_BLOCK + column(tile_off) - lo
        inside = jnp.where(u >= lo, jnp.where(u < lo + n_u, 1.0, 0.0), 0.0)
        dst_unit = jnp.sum(inside * (base_unit + u), axis=0, keepdims=True)
        used = jnp.sum(inside, axis=0, keepdims=True) > 0.5
        spare = (SPARE_UNIT0 + (i % 2) * MAX_UNITS).astype(F32) + u[0:1, :]
        unit_ref[s] = jnp.where(used, dst_unit, spare).astype(I32)

        toks = slice(s * TM_TOK, (s + 1) * TM_TOK)
        rows = []
        for k in range(TOP_K):
            hit = io_e == eidx_ref[k:k + 1, toks]
            seg = jnp.sum(jnp.where(hit, lo * UNIT, 0.0), axis=0, keepdims=True)
            rows.append(seg.astype(I32) + rank_ref[k:k + 1, toks])
        rows.append(jnp.full((SUBLANES - TOP_K, TM_TOK), -1, I32))
        lslot_ref[:, toks] = jnp.concatenate(rows, axis=0)


def _slot_plan(eidx, rank, nmat):
    tok_map = lambda i: (0, i)
    const = lambda i: (0, 0)
    lslot, unit_tab, gend = pl.pallas_call(
        _plan_kernel,
        out_shape=(
            jax.ShapeDtypeStruct((SUBLANES, T_ALL), I32),
            jax.ShapeDtypeStruct((N_TOK_TILES, 1, MAX_UNITS), I32),
            jax.ShapeDtypeStruct((N_EXPERTS, LANES), I32),
        ),
        grid=(N_TOK_TILES // PLAN_TILES,),
        in_specs=[
            pl.BlockSpec((SUBLANES, PLAN_TILES * TM_TOK), tok_map),
            pl.BlockSpec((SUBLANES, PLAN_TILES * TM_TOK), tok_map),
            pl.BlockSpec((N_EXPERTS, LANES), const),
        ],
        out_specs=(
            pl.BlockSpec((SUBLANES, PLAN_TILES * TM_TOK), tok_map),
            pl.BlockSpec((PLAN_TILES, 1, MAX_UNITS), lambda i: (i, 0, 0)),
            pl.BlockSpec((N_EXPERTS, LANES), const),
        ),
        compiler_params=pltpu.CompilerParams(dimension_semantics=("arbitrary",)),
        name="slot_plan",
    )(eidx, rank, nmat)
    return lslot, unit_tab.reshape(N_TOK_TILES * MAX_UNITS), gend[:, 0]


PACK_W = D_MODEL // 2
HI_HALF = -65536


def _pack_pairs(x):
    lo = lax.bitcast_convert_type(x[:, 0:PACK_W], I32)
    hi = lax.bitcast_convert_type(x[:, PACK_W:D_MODEL], I32)
    return lax.shift_right_logical(lo, 16) | (hi & HI_HALF)


def _unpack_pairs(u):
    lo = lax.bitcast_convert_type(lax.shift_left(u, 16), F32).astype(BF16)
    hi = lax.bitcast_convert_type(u & HI_HALF, F32).astype(BF16)
    return lo, hi


def _unit_rows(unit):
    row = unit * UNIT
    return pl.ds(row if isinstance(unit, int) else pl.multiple_of(row, UNIT), UNIT)


def _unit_copy(src, src_unit, dst, dst_unit, sem):
    return pltpu.make_async_copy(src.at[_unit_rows(src_unit)], dst.at[_unit_rows(dst_unit)], sem)


def _dispatch_kernel(gend_ref, tab_ref, h2_ref, lslot_ref, xs_hbm, zero_ref, local_ref, sem_zero, sem_rows):
    i = pl.program_id(0)
    buf = i % 2

    def drain(b):
        pltpu.make_async_copy(local_ref.at[b], xs_hbm.at[pl.ds(0, LOCAL_ROWS)], sem_rows.at[b]).wait()

    def has_rows(e):
        return gend_ref[e] > jnp.where(e == 0, 0, gend_ref[jnp.maximum(e - 1, 0)])

    def zero_copy(e):
        return pltpu.make_async_copy(
            zero_ref, xs_hbm.at[pl.ds(pl.multiple_of(gend_ref[e] - BM, BM), BM)], sem_zero)

    @pl.when(i == 0)
    def _():
        zero_ref[...] = jnp.zeros_like(zero_ref)

        def start(e, c):
            @pl.when(has_rows(e))
            def _():
                zero_copy(e).start()
            return c

        def wait(e, c):
            @pl.when(has_rows(e))
            def _():
                zero_copy(e).wait()
            return c

        def tail_copy(blk):
            return pltpu.make_async_copy(
                zero_ref, xs_hbm.at[pl.ds(pl.multiple_of(blk * BM, BM), BM)], sem_zero)

        def start_tail(blk, c):
            tail_copy(blk).start()
            return c

        def wait_tail(blk, c):
            tail_copy(blk).wait()
            return c

        n_used = gend_ref[N_EXPERTS - 1] // BM
        lax.fori_loop(0, N_EXPERTS, start, 0)
        lax.fori_loop(n_used, N_BLOCKS_ALL, start_tail, 0)
        lax.fori_loop(0, N_EXPERTS, wait, 0)
        lax.fori_loop(n_used, N_BLOCKS_ALL, wait_tail, 0)

    @pl.when(i >= 2)
    def _():
        drain(buf)

    h2 = h2_ref[...]
    units_per_chunk = PERM_CHUNK // UNIT
    local = local_ref.at[buf]
    for c in range(LOCAL_ROWS // PERM_CHUNK):
        slot = c * PERM_CHUNK + lax.broadcasted_iota(I32, (PERM_CHUNK, TM_TOK), 0)
        p = jnp.zeros((PERM_CHUNK, TM_TOK), F32)
        for k in range(TOP_K):
            p = jnp.where(slot == lslot_ref[k:k + 1, :], 1.0, p)
        local[c * PERM_CHUNK:(c + 1) * PERM_CHUNK, :] = _pack_pairs(_dot(p.astype(BF16), h2))
        for u in range(c * units_per_chunk, (c + 1) * units_per_chunk):
            _unit_copy(local, u, xs_hbm, tab_ref[i * MAX_UNITS + u], sem_rows.at[buf]).start()

    @pl.when(i == N_TOK_TILES - 1)
    def _():
        drain(1 - buf)
        drain(buf)


def _dispatch(h2, lslot, unit_tab, gend):
    return pl.pallas_call(
        _dispatch_kernel,
        out_shape=jax.ShapeDtypeStruct((N_SLOTS, PACK_W), I32),
        grid_spec=pltpu.PrefetchScalarGridSpec(
            num_scalar_prefetch=2,
            grid=(N_TOK_TILES,),
            in_specs=[
                pl.BlockSpec((TM_TOK, D_MODEL), lambda i, ge, tab: (i, 0)),
                pl.BlockSpec((SUBLANES, TM_TOK), lambda i, ge, tab: (0, i)),
            ],
            out_specs=pl.BlockSpec(memory_space=pl.ANY),
            scratch_shapes=[
                pltpu.VMEM((BM, PACK_W), I32),
                pltpu.VMEM((2, LOCAL_ROWS, PACK_W), I32),
                pltpu.SemaphoreType.DMA,
                pltpu.SemaphoreType.DMA((2,)),
            ],
        ),
        compiler_params=pltpu.CompilerParams(
            dimension_semantics=("arbitrary",), vmem_limit_bytes=VMEM_LIMIT),
        name="moe_dispatch",
    )(gend, unit_tab, h2, lslot)


def _expert_kernel(gend_ref, xs_hbm, wgu_ref, wdown_ref, ys_hbm, xbuf, ybuf, wgu_bf, wdown_bf, sem_in, sem_out):
    e = pl.program_id(0)
    first = jnp.where(e == 0, 0, gend_ref[jnp.maximum(e - 1, 0)]) // BM
    last = gend_ref[e] // BM
    n_used = gend_ref[N_EXPERTS - 1] // BM

    def rows_of(blk):
        return pl.ds(pl.multiple_of(blk * BM, BM), BM)

    def in_copy(blk):
        return pltpu.make_async_copy(xs_hbm.at[rows_of(blk)], xbuf.at[blk % X_BUFS], sem_in.at[blk % X_BUFS])

    def out_copy(blk):
        return pltpu.make_async_copy(ybuf.at[blk % 2], ys_hbm.at[rows_of(blk)], sem_out.at[blk % 2])

    @pl.when(e == 0)
    def _():
        for a in range(X_AHEAD):
            @pl.when(a < n_used)
            def _(a=a):
                in_copy(a).start()

    @pl.when(last > first)
    def _():
        wgu_bf[...] = wgu_ref[0, 0].astype(BF16)
        wdown_bf[...] = wdown_ref[0, 0].astype(BF16)

    def block(blk, carry):
        @pl.when(blk + X_AHEAD < n_used)
        def _():
            in_copy(blk + X_AHEAD).start()

        in_copy(blk).wait()

        @pl.when(blk >= 2)
        def _():
            out_copy(blk - 2).wait()

        x_lo, x_hi = _unpack_pairs(xbuf[blk % X_BUFS])
        gu = _dot(x_lo, wgu_bf[0:PACK_W, :]) + _dot(x_hi, wgu_bf[PACK_W:D_MODEL, :])
        act = _silu(gu[:, 0:D_EXPERT]) * gu[:, D_EXPERT:2 * D_EXPERT]
        y = _dot(act.astype(BF16), wdown_bf[...])
        ybuf[blk % 2] = _pack_pairs(y.astype(BF16).astype(F32))
        out_copy(blk).start()
        return carry

    lax.fori_loop(first, last, block, 0)

    @pl.when(e == N_EXPERTS - 1)
    def _():
        @pl.when(n_used >= 2)
        def _():
            out_copy(n_used - 2).wait()

        @pl.when(n_used >= 1)
        def _():
            out_copy(n_used - 1).wait()

        ybuf[0] = jnp.zeros((BM, PACK_W), I32)

        def tail_copy(blk):
            return pltpu.make_async_copy(ybuf.at[0], ys_hbm.at[rows_of(blk)], sem_out.at[0])

        def start(blk, c):
            tail_copy(blk).start()
            return c

        def wait(blk, c):
            tail_copy(blk).wait()
            return c

        lax.fori_loop(n_used, N_BLOCKS_ALL, start, 0)
        lax.fori_loop(n_used, N_BLOCKS_ALL, wait, 0)


def _experts(xs, w_gu, w_down, layer, gend):
    return pl.pallas_call(
        _expert_kernel,
        out_shape=jax.ShapeDtypeStruct((N_SLOTS, PACK_W), I32),
        grid_spec=pltpu.PrefetchScalarGridSpec(
            num_scalar_prefetch=1,
            grid=(N_EXPERTS,),
            in_specs=[
                pl.BlockSpec(memory_space=pl.ANY),
                pl.BlockSpec((1, 1, D_MODEL, 2 * D_EXPERT), lambda e, ge: (layer, e, 0, 0)),
                pl.BlockSpec((1, 1, D_EXPERT, D_MODEL), lambda e, ge: (layer, e, 0, 0)),
            ],
            out_specs=pl.BlockSpec(memory_space=pl.ANY),
            scratch_shapes=[
                pltpu.VMEM((X_BUFS, BM, PACK_W), I32),
                pltpu.VMEM((2, BM, PACK_W), I32),
                pltpu.VMEM((D_MODEL, 2 * D_EXPERT), BF16),
                pltpu.VMEM((D_EXPERT, D_MODEL), BF16),
                pltpu.SemaphoreType.DMA((X_BUFS,)),
                pltpu.SemaphoreType.DMA((2,)),
            ],
        ),
        compiler_params=pltpu.CompilerParams(
            dimension_semantics=("arbitrary",), vmem_limit_bytes=VMEM_LIMIT),
        name="moe_experts",
    )(gend, xs, w_gu, w_down)


def _combine_kernel(tab_ref, x1_ref, h2_ref, lslot_ref, wsel_ref, ys_hbm, wsgu_ref, wsdown_ref, mod_ref,
                    g_ref, b_ref, outc_ref, outl_ref, local_ref, sel_ref, ylo_ref, yhi_ref, sem_rows):
    i = pl.program_id(0)
    tm = TM_TOK
    buf = i % 2

    def fetch_unit(tile, b, u):
        _unit_copy(ys_hbm, tab_ref[tile * MAX_UNITS + u], local_ref.at[b], u, sem_rows.at[b]).start()

    def drain(b):
        pltpu.make_async_copy(ys_hbm.at[pl.ds(0, LOCAL_ROWS)], local_ref.at[b], sem_rows.at[b]).wait()

    @pl.when(i == 0)
    def _():
        def body(u, c):
            fetch_unit(0, 0, u)
            return c

        lax.fori_loop(0, MAX_UNITS, body, 0, unroll=8)

    sgu = _dot(h2_ref[...], wsgu_ref[...])
    act = _silu(sgu[:, 0:D_SHARED]) * sgu[:, D_SHARED:2 * D_SHARED]
    f = _dot(act.astype(BF16), wsdown_ref[...])

    nxt = jnp.minimum(i + 1, N_TOK_TILES - 1)
    n_groups = tm // SEL_ROWS
    units_per_group = MAX_UNITS // n_groups
    slot = lax.broadcasted_iota(I32, (SEL_ROWS, LOCAL_ROWS), 1)
    for g in range(n_groups):
        rows = slice(g * SEL_ROWS, (g + 1) * SEL_ROWS)
        sel = jnp.zeros((SEL_ROWS, LOCAL_ROWS), F32)
        for k in range(TOP_K):
            sel = jnp.where(slot == lslot_ref[rows, k:k + 1], wsel_ref[rows, k:k + 1], sel)
        sel_ref[rows, :] = sel.astype(BF16)
        for u in range(g * units_per_group, (g + 1) * units_per_group):
            fetch_unit(nxt, 1 - buf, u)

    local = local_ref.at[buf]
    drain(buf)
    for c in range(LOCAL_ROWS // PERM_CHUNK):
        rows = slice(c * PERM_CHUNK, (c + 1) * PERM_CHUNK)
        ylo_ref[rows, :], yhi_ref[rows, :] = _unpack_pairs(local[rows, :])
    sel = sel_ref[...]
    f = f + jnp.concatenate([_dot(sel, ylo_ref[...]), _dot(sel, yhi_ref[...])], axis=-1)
    ci = _cond_row(i, tm)
    gate2 = mod_ref[pl.ds(ci, 1), 5 * D_MODEL:6 * D_MODEL]
    out = _layer_norm(ALPHA * x1_ref[...] + gate2 * f, g_ref[...], b_ref[...])

    @pl.when(i < T_CTX // tm)
    def _():
        outc_ref[...] = out

    @pl.when(i >= T_CTX // tm)
    def _():
        outl_ref[...] = out

    @pl.when(i == N_TOK_TILES - 1)
    def _():
        drain(1 - buf)


def _combine(x1, h2, lslot_rows, wsel_rows, unit_tab, ys, w_sgu_bf16, w_sdown_bf16, mod, ln_g, ln_b):
    n_ctx = T_CTX // TM_TOK
    row_map = lambda i, tab: (i, 0)
    const = lambda i, tab: (0, 0)
    return pl.pallas_call(
        _combine_kernel,
        out_shape=(jax.ShapeDtypeStruct((T_CTX, D_MODEL), F32),
                   jax.ShapeDtypeStruct((T_LAT, D_MODEL), F32)),
        grid_spec=pltpu.PrefetchScalarGridSpec(
            num_scalar_prefetch=1,
            grid=(N_TOK_TILES,),
            in_specs=[
                pl.BlockSpec((TM_TOK, D_MODEL), row_map),
                pl.BlockSpec((TM_TOK, D_MODEL), row_map),
                pl.BlockSpec((TM_TOK, SUBLANES), row_map),
                pl.BlockSpec((TM_TOK, SUBLANES), row_map),
                pl.BlockSpec(memory_space=pl.ANY),
                pl.BlockSpec((D_MODEL, 2 * D_SHARED), const),
                pl.BlockSpec((D_SHARED, D_MODEL), const),
                pl.BlockSpec((N_COND, 6 * D_MODEL), const),
                pl.BlockSpec((1, D_MODEL), const),
                pl.BlockSpec((1, D_MODEL), const),
            ],
            out_specs=(pl.BlockSpec((TM_TOK, D_MODEL), lambda i, tab: (jnp.minimum(i, n_ctx - 1), 0)),
                       pl.BlockSpec((TM_TOK, D_MODEL), lambda i, tab: (jnp.maximum(i - n_ctx, 0), 0))),
            scratch_shapes=[
                pltpu.VMEM((2, LOCAL_ROWS, PACK_W), I32),
                pltpu.VMEM((TM_TOK, LOCAL_ROWS), BF16),
                pltpu.VMEM((LOCAL_ROWS, PACK_W), BF16),
                pltpu.VMEM((LOCAL_ROWS, PACK_W), BF16),
                pltpu.SemaphoreType.DMA((2,)),
            ],
        ),
        compiler_params=pltpu.CompilerParams(
            dimension_semantics=("arbitrary",), vmem_limit_bytes=VMEM_LIMIT),
        name="moe_combine",
    )(unit_tab, x1, h2, lslot_rows, wsel_rows, ys, w_sgu_bf16, w_sdown_bf16, mod, ln_g, ln_b)


def _lane_rows(v, width):
    return jnp.broadcast_to(v.astype(F32)[:, None, None], (v.shape[0], 1, width))


def kernel(x_prompt, x_sample, state_ret_fwd, state_ret_bwd, cache_win_k, cache_win_v, cache_na_k, cache_na_v, c, c_ctx, w_in, w_out, ret_decay_fwd, ret_decay_bwd, ret_gn_g, ret_gn_b, win_sink, na_rpb, w_mod, b_mod, ln1_g, ln1_b, ln2_g, ln2_b, w_router, b_router, w_expert_gu, w_expert_down, w_shared_gu, w_shared_down):
    cond = jnp.concatenate(
        [c_ctx[None, :], c, jnp.zeros((N_COND - 1 - DEC_BATCH, D_MODEL), F32)], axis=0)
    mod_all = _modulation(cond, w_mod, b_mod)
    cos_t, sin_t = _rope_tables()

    x_ctx = x_prompt.reshape(T_CTX, D_MODEL)
    x_lat = x_sample.reshape(T_LAT, D_MODEL)
    sf_l, sb_l, caches = [], [], ()
    for l in range(DEPTH):
        mod = mod_all[l]
        pa, pb, pc, *caches = _in_projection(x_ctx, x_lat, mod, w_in[l].astype(BF16), tuple(caches))
        decf_s, decb_s = _lane_rows(ret_decay_fwd[l], SEQ), _lane_rows(ret_decay_bwd[l], SEQ)
        gng, gnb = ret_gn_g[l][None, :], ret_gn_b[l][None, :]
        mix_c, st_f, st_b = _ctx_mixers(pa, pb, pc, win_sink[l], decf_s, decb_s, gng, gnb)
        sf_l.append(st_f)
        sb_l.append(st_b)
        ya = _lat_retention(pa, state_ret_fwd, state_ret_bwd, l,
                            _lane_rows(ret_decay_fwd[l], RET_CHUNK), _lane_rows(ret_decay_bwd[l], RET_CHUNK),
                            gng, gnb)
        yb = _lat_window_attn(pb, cache_win_k, cache_win_v, l, win_sink[l], cos_t, sin_t)
        yc = _lat_na_attn(pc, cache_na_k, cache_na_v, l, _na_maskbias(na_rpb[l]))

        wr_hi = w_router[l].astype(BF16)
        wr_lo = (w_router[l] - wr_hi.astype(F32)).astype(BF16)
        x1, h2, eidx, wsel, rank, counts = _post_mixer(
            x_ctx, x_lat, mix_c, ya, yb, yc, w_out[l].astype(BF16), mod, ln1_g[l][None, :],
            ln1_b[l][None, :], jnp.concatenate([wr_hi, wr_lo], axis=1), b_router[l][:, None])
        lslot, unit_tab, gend = _slot_plan(eidx, rank, counts)
        xs = _dispatch(h2, lslot, unit_tab, gend)
        ys = _experts(xs, w_expert_gu, w_expert_down, l, gend)
        x_ctx, x_lat = _combine(x1, h2, lslot.T, wsel.T, unit_tab, ys, w_shared_gu[l].astype(BF16),
                                w_shared_down[l].astype(BF16), mod, ln2_g[l][None, :], ln2_b[l][None, :])

    y_prompt = x_ctx.reshape(BATCH, SEQ, D_MODEL)
    y_sample = x_lat.reshape(DEC_BATCH, DEC_SEQ, D_MODEL)
    new_sf = jnp.stack(sf_l, axis=1)
    new_sb = jnp.stack(sb_l, axis=1)

    return (y_prompt, y_sample, new_sf, new_sb, *caches)
```

```python
import functools

import numpy as np
import jax
import jax.numpy as jnp
from jax import lax
from jax.experimental import pallas as pl
from jax.experimental.pallas import tpu as pltpu

F32 = jnp.float32
BF16 = jnp.bfloat16
I32 = jnp.int32

D_MODEL = 1024
BATCH = 32
SEQ = 256
DEPTH = 2
DEC_BATCH = 4
DEC_SEQ = 2048
PAST_LEN = 256
GRID_W = 64
HEAD_DIM = 64
ATTN_SCALE = HEAD_DIM ** -0.5
H_A = 4
W_A = H_A * HEAD_DIM
GN_EPS = 1e-5
H_B = 6
KV_B = 2
W_B = H_B * HEAD_DIM
KV_W_B = KV_B * HEAD_DIM
WINDOW = 128
ROPE_BASE = 10000.0
H_C = 6
W_C = H_C * HEAD_DIM
NA_ROWS = 8
NA_COLS = 16
IN_WIDTH = 4 * W_A + W_B + 2 * KV_W_B + 3 * W_C
N_EXPERTS = 64
TOP_K = 6
N_GROUPS = 8
GROUP_SIZE = N_EXPERTS // N_GROUPS
TOPK_GROUPS = 4
D_EXPERT = 256
D_SHARED = 256
ROUTE_SCALE = 2.5
ALPHA = (2 * DEPTH) ** 0.25
LN_EPS = 1e-5
NEG_INF = -1e30
PICKED = -3e38

T_CTX = BATCH * SEQ
T_LAT = DEC_BATCH * DEC_SEQ
T_ALL = T_CTX + T_LAT
N_COND = 8

PA_W = 4 * W_A
PB_W = W_B + 2 * KV_W_B
PC_W = 3 * W_C

LANES = 128
SUBLANES = 8
VMEM_LIMIT = 56 * 1024 * 1024

TM_PROJ = 512
TM_TOK = 256
RET_CHUNK = 256
POST_TILES = 4
CTX_STACK = 3
NA_STACK = 6
WIN_QB = 256
WIN_BAND = WIN_QB + 2 * WINDOW
NA_TILE_ROWS = 4
NA_KEY_ROWS = 11
BM = 512
X_AHEAD = 3
X_BUFS = X_AHEAD + 1
UNIT = SUBLANES
UNITS_PER_BLOCK = BM // UNIT
PERM_CHUNK = 256
SEL_ROWS = 16
N_TOK_TILES = T_ALL // TM_TOK
LOCAL_ROWS = -(-(TM_TOK * TOP_K + N_EXPERTS * (UNIT - 1)) // PERM_CHUNK) * PERM_CHUNK
MAX_UNITS = LOCAL_ROWS // UNIT
N_ASSIGN = T_ALL * TOP_K
N_BLOCKS = -(-(N_ASSIGN + N_TOK_TILES * N_EXPERTS * (UNIT - 1) + N_EXPERTS * (BM - 1)) // BM)
SPARE_BLOCKS = 2 * -(-LOCAL_ROWS // BM)
SPARE_UNIT0 = N_BLOCKS * UNITS_PER_BLOCK
N_BLOCKS_ALL = N_BLOCKS + SPARE_BLOCKS
N_SLOTS = N_BLOCKS_ALL * BM


def _dot(a, b):
    return jnp.dot(a, b, preferred_element_type=F32)


def _dot_nt(a, b):
    return lax.dot_general(a, b, (((1,), (1,)), ((), ())), preferred_element_type=F32)


def _silu(x):
    return x * jax.nn.sigmoid(x)


def _log_sigmoid(x):
    return jnp.minimum(x, 0.0) - jnp.log(1.0 + jnp.exp(-jnp.abs(x)))


def _cond_row(tile, tile_rows):
    n_ctx = T_CTX // tile_rows
    per_lat = DEC_SEQ // tile_rows
    return jnp.where(tile < n_ctx, 0, 1 + (tile - n_ctx) // per_lat)


def _layer_norm(x, g, b):
    mu = jnp.mean(x, -1, keepdims=True)
    xc = x - mu
    var = jnp.mean(xc * xc, -1, keepdims=True)
    return xc * lax.rsqrt(var + LN_EPS) * g + b


MOD_TN = 1536


def _mod_kernel(cond_ref, w_ref, b_ref, o_ref):
    s = _silu(cond_ref[...])
    s_hi = s.astype(BF16)
    s_lo = (s - s_hi.astype(F32)).astype(BF16)
    w = w_ref[0]
    w_hi = w.astype(BF16)
    w_lo = (w - w_hi.astype(F32)).astype(BF16)
    o_ref[0] = _dot(s_hi, w_hi) + _dot(s_lo, w_hi) + _dot(s_hi, w_lo) + b_ref[0]


def _modulation(cond, w_mod, b_mod):
    n_out = 6 * D_MODEL
    return pl.pallas_call(
        _mod_kernel,
        out_shape=jax.ShapeDtypeStruct((DEPTH, N_COND, n_out), F32),
        grid=(DEPTH, n_out // MOD_TN),
        in_specs=[
            pl.BlockSpec((N_COND, D_MODEL), lambda l, j: (0, 0)),
            pl.BlockSpec((1, D_MODEL, MOD_TN), lambda l, j: (l, 0, j)),
            pl.BlockSpec((1, 1, MOD_TN), lambda l, j: (l, 0, j)),
        ],
        out_specs=pl.BlockSpec((1, N_COND, MOD_TN), lambda l, j: (l, 0, j)),
        compiler_params=pltpu.CompilerParams(
            dimension_semantics=("arbitrary", "arbitrary"), vmem_limit_bytes=VMEM_LIMIT),
        name="modulation",
    )(cond, w_mod, b_mod.reshape(DEPTH, 1, n_out))


SEQ_PER_PROJ = TM_PROJ // SEQ


def _inproj_kernel(n_prev, xc_ref, xl_ref, mod_ref, w_ref, *refs):
    prev_refs = refs[:4] if n_prev else ()
    pa_ref, pb_ref, pc_ref, wk_ref, wv_ref, nk_ref, nv_ref = refs[len(prev_refs):]
    i = pl.program_id(0)
    ci = _cond_row(i, TM_PROJ)
    sh = mod_ref[pl.ds(ci, 1), 0:D_MODEL]
    sc = mod_ref[pl.ds(ci, 1), D_MODEL:2 * D_MODEL]
    x = jnp.where(i < T_CTX // TM_PROJ, xc_ref[...], xl_ref[...])
    h = x * (1.0 + sc) + sh
    p = _dot(h.astype(BF16), w_ref[...])
    pa_ref[...] = p[:, 0:PA_W].astype(BF16)
    pb_ref[...] = p[:, PA_W:PA_W + PB_W].astype(BF16)
    pc_ref[...] = p[:, PA_W + PB_W:IN_WIDTH].astype(BF16)

    @pl.when(i < T_CTX // TM_PROJ)
    def _():
        targets = ((wk_ref, PA_W + W_B, KV_B), (wv_ref, PA_W + W_B + KV_W_B, KV_B),
                   (nk_ref, PA_W + PB_W + W_C, H_C), (nv_ref, PA_W + PB_W + 2 * W_C, H_C))
        for j, (ref, col0, n_heads) in enumerate(targets):
            if n_prev:
                ref[:, 0:n_prev] = prev_refs[j][...]
            for s in range(SEQ_PER_PROJ):
                for hh in range(n_heads):
                    ref[s, n_prev, hh] = p[s * SEQ:(s + 1) * SEQ,
                                           col0 + hh * HEAD_DIM:col0 + (hh + 1) * HEAD_DIM]


def _in_projection(x_ctx, x_lat, mod, w_in_bf16, earlier):
    n_ctx_tiles = T_CTX // TM_PROJ
    n_prev = earlier[0].shape[1] if earlier else 0

    def cache_spec(n_layers, n_heads):
        return pl.BlockSpec((SEQ_PER_PROJ, n_layers, n_heads, SEQ, HEAD_DIM),
                            lambda i: (jnp.minimum(i, n_ctx_tiles - 1), 0, 0, 0, 0))

    cache_heads = (KV_B, KV_B, H_C, H_C)
    return pl.pallas_call(
        functools.partial(_inproj_kernel, n_prev),
        out_shape=(
            jax.ShapeDtypeStruct((T_ALL, PA_W), BF16),
            jax.ShapeDtypeStruct((T_ALL, PB_W), BF16),
            jax.ShapeDtypeStruct((T_ALL, PC_W), BF16),
        ) + tuple(jax.ShapeDtypeStruct((BATCH, n_prev + 1, nh, SEQ, HEAD_DIM), F32) for nh in cache_heads),
        grid=(T_ALL // TM_PROJ,),
        in_specs=[
            pl.BlockSpec((TM_PROJ, D_MODEL), lambda i: (jnp.minimum(i, n_ctx_tiles - 1), 0)),
            pl.BlockSpec((TM_PROJ, D_MODEL), lambda i: (jnp.maximum(i - n_ctx_tiles, 0), 0)),
            pl.BlockSpec((N_COND, 6 * D_MODEL), lambda i: (0, 0)),
            pl.BlockSpec((D_MODEL, IN_WIDTH), lambda i: (0, 0)),
        ] + [cache_spec(n_prev, nh) for nh in cache_heads if n_prev],
        out_specs=(
            pl.BlockSpec((TM_PROJ, PA_W), lambda i: (i, 0)),
            pl.BlockSpec((TM_PROJ, PB_W), lambda i: (i, 0)),
            pl.BlockSpec((TM_PROJ, PC_W), lambda i: (i, 0)),
        ) + tuple(cache_spec(n_prev + 1, nh) for nh in cache_heads),
        compiler_params=pltpu.CompilerParams(
            dimension_semantics=("arbitrary",), vmem_limit_bytes=VMEM_LIMIT),
        name="in_projection",
    )(x_ctx, x_lat, mod, w_in_bf16, *earlier)


def _decay_matrix(lg_f, lg_b, n):
    row = lax.broadcasted_iota(I32, (n, n), 0)
    col = lax.broadcasted_iota(I32, (n, n), 1)
    diff = (row - col).astype(F32)
    fwd = jnp.where(diff >= 0, jnp.exp(lg_f * jnp.maximum(diff, 0.0)), 0.0)
    bwd = jnp.where(diff <= 0, jnp.exp(lg_b * jnp.maximum(-diff, 0.0)), 0.0)
    return (fwd + bwd) * ATTN_SCALE


def _stacked_softmax_attend(scores, values, extra_logit=None):
    rows = scores[0].shape[0]
    s = jnp.concatenate(scores, axis=0)
    m = jnp.max(s, -1, keepdims=True)
    if extra_logit is not None:
        m = jnp.maximum(m, extra_logit)
    p = jnp.exp(s - m)
    den = jnp.sum(p, -1, keepdims=True)
    if extra_logit is not None:
        den = den + jnp.exp(extra_logit - m)
    p = p.astype(BF16)
    return [_dot(p[h * rows:(h + 1) * rows], v) / den[h * rows:(h + 1) * rows]
            for h, v in enumerate(values)]


def _lane_xor_matrix(width, distance):
    r = lax.broadcasted_iota(I32, (width, width), 0)
    c = lax.broadcasted_iota(I32, (width, width), 1)
    return jnp.where((r ^ distance) == c, 1.0, 0.0).astype(BF16)


def _ctx_mixer_kernel(sink_ref, pa_ref, pb_ref, pc_ref, decf_ref, decb_ref, gng_ref, gnb_ref,
                      mix_ref, sf_ref, sb_ref, dmat_ref, zf_ref, zb_ref):
    n = SEQ
    hd = HEAD_DIM

    @pl.when(pl.program_id(0) == 0)
    def _():
        pos = lax.broadcasted_iota(I32, (n, hd), 0).astype(F32)
        zf, zb = [], []
        for h in range(H_A):
            lg_f = _log_sigmoid(decf_ref[h])
            lg_b = _log_sigmoid(decb_ref[h])
            dmat_ref[h] = _decay_matrix(lg_f, lg_b, n)
            zf.append(jnp.exp(lg_f[:, 0:hd] * (n - 1.0 - pos)) * ATTN_SCALE)
            zb.append(jnp.exp(lg_b[:, 0:hd] * pos) * ATTN_SCALE)
        for p in range(H_A // 2):
            zf_ref[p] = jnp.concatenate(zf[2 * p:2 * p + 2], axis=1)
            zb_ref[p] = jnp.concatenate(zb[2 * p:2 * p + 2], axis=1)

    low = lax.broadcasted_iota(I32, (n, LANES), 1) < hd

    def own_half(x, h):
        zero = jnp.zeros_like(x)
        return jnp.where(low, x, zero) if h % 2 == 0 else jnp.where(low, zero, x)

    def merge(first, second):
        return jnp.where(low, first, second)

    def pair_cols(ref, base, p):
        return ref[:, base + p * LANES:base + (p + 1) * LANES]

    def head_mean(x):
        first = jnp.sum(jnp.where(low, x, 0.0), -1, keepdims=True)
        second = jnp.sum(jnp.where(low, 0.0, x), -1, keepdims=True)
        return merge(first, second) * (1.0 / hd)

    for p in range(H_A // 2):
        q_pair, k_pair = pair_cols(pa_ref, 0, p), pair_cols(pa_ref, W_A, p)
        v_pair, gate_pair = pair_cols(pa_ref, 2 * W_A, p), pair_cols(pa_ref, 3 * W_A, p)
        outs = []
        for h in (2 * p, 2 * p + 1):
            a = _dot_nt(own_half(q_pair, h), k_pair)
            outs.append(_dot((a * dmat_ref[h]).astype(BF16), v_pair))
        o = merge(outs[0], outs[1])
        kf = k_pair.astype(F32)
        for st_ref, z_ref in ((sf_ref, zf_ref), (sb_ref, zb_ref)):
            st = _dot((kf * z_ref[p]).T.astype(BF16), v_pair)
            st_ref[0, 2 * p] = st[0:hd, 0:hd]
            st_ref[0, 2 * p + 1] = st[hd:2 * hd, hd:2 * hd]
        mu = head_mean(o)
        oc = o - mu
        var = head_mean(oc * oc)
        on = oc * lax.rsqrt(var + GN_EPS) * pair_cols(gng_ref, 0, p) + pair_cols(gnb_ref, 0, p)
        mix_ref[:, p * LANES:(p + 1) * LANES] = (on * _silu(gate_pair.astype(F32))).astype(BF16)

    group = H_B // KV_B
    swap = _lane_xor_matrix(LANES, hd)
    kv_k = pb_ref[:, W_B:W_B + KV_W_B]
    kv_v = pb_ref[:, W_B + KV_W_B:W_B + 2 * KV_W_B]
    scores = []
    for hh in range(H_B):
        q = own_half(pair_cols(pb_ref, 0, hh // 2), hh) * ATTN_SCALE
        if hh % 2 != hh // group:
            q = _dot(q, swap).astype(BF16)
        scores.append(_dot_nt(q, kv_k))
    sinks = jnp.concatenate([jnp.full((n, 1), sink_ref[hh], F32) for hh in range(H_B)], axis=0)
    outs = []
    stacked = []
    for g0 in range(0, H_B, CTX_STACK):
        stacked += _stacked_softmax_attend(scores[g0:g0 + CTX_STACK], [kv_v] * CTX_STACK,
                                           sinks[g0 * n:(g0 + CTX_STACK) * n])
    for hh, o in enumerate(stacked):
        o = o.astype(BF16)
        outs.append(_dot(o, swap).astype(BF16) if hh % 2 != hh // group else o)
    for p in range(H_B // 2):
        mix_ref[:, W_A + p * LANES:W_A + (p + 1) * LANES] = merge(outs[2 * p], outs[2 * p + 1])

    scores, values = [], []
    for hh in range(H_C):
        q = own_half(pair_cols(pc_ref, 0, hh // 2), hh) * ATTN_SCALE
        scores.append(_dot_nt(q, pair_cols(pc_ref, W_C, hh // 2)))
        values.append(pair_cols(pc_ref, 2 * W_C, hh // 2))
    outs = []
    for g0 in range(0, H_C, CTX_STACK):
        outs += _stacked_softmax_attend(scores[g0:g0 + CTX_STACK], values[g0:g0 + CTX_STACK])
    for p in range(H_C // 2):
        mix_ref[:, W_A + W_B + p * LANES:W_A + W_B + (p + 1) * LANES] = merge(
            outs[2 * p], outs[2 * p + 1]).astype(BF16)


def _ctx_mixers(pa, pb, pc, sink, decf, decb, gng, gnb):
    return pl.pallas_call(
        _ctx_mixer_kernel,
        out_shape=(
            jax.ShapeDtypeStruct((T_CTX, D_MODEL), BF16),
            jax.ShapeDtypeStruct((BATCH, H_A, HEAD_DIM, HEAD_DIM), F32),
            jax.ShapeDtypeStruct((BATCH, H_A, HEAD_DIM, HEAD_DIM), F32),
        ),
        grid=(BATCH,),
        in_specs=[
            pl.BlockSpec(memory_space=pltpu.SMEM),
            pl.BlockSpec((SEQ, PA_W), lambda b: (b, 0)),
            pl.BlockSpec((SEQ, PB_W), lambda b: (b, 0)),
            pl.BlockSpec((SEQ, PC_W), lambda b: (b, 0)),
            pl.BlockSpec((H_A, 1, SEQ), lambda b: (0, 0, 0)),
            pl.BlockSpec((H_A, 1, SEQ), lambda b: (0, 0, 0)),
            pl.BlockSpec((1, W_A), lambda b: (0, 0)),
            pl.BlockSpec((1, W_A), lambda b: (0, 0)),
        ],
        out_specs=(
            pl.BlockSpec((SEQ, D_MODEL), lambda b: (b, 0)),
            pl.BlockSpec((1, H_A, HEAD_DIM, HEAD_DIM), lambda b: (b, 0, 0, 0)),
            pl.BlockSpec((1, H_A, HEAD_DIM, HEAD_DIM), lambda b: (b, 0, 0, 0)),
        ),
        scratch_shapes=[
            pltpu.VMEM((H_A, SEQ, SEQ), F32),
            pltpu.VMEM((H_A // 2, SEQ, LANES), F32),
            pltpu.VMEM((H_A // 2, SEQ, LANES), F32),
        ],
        compiler_params=pltpu.CompilerParams(
            dimension_semantics=("arbitrary",), vmem_limit_bytes=VMEM_LIMIT),
        name="ctx_mixers",
    )(sink, pa, pb, pc, decf, decb, gng, gnb)


def _lat_ret_kernel(pa_ref, stf_ref, stb_ref, decf_ref, decb_ref, gng_ref, gnb_ref, ya_ref):
    c = RET_CHUNK
    hd = HEAD_DIM
    n_chunks = DEC_SEQ // c
    pos = lax.broadcasted_iota(I32, (c, LANES), 0).astype(F32)
    low = lax.broadcasted_iota(I32, (c, LANES), 1) < hd
    same_head = (lax.broadcasted_iota(I32, (LANES, LANES), 0) // hd
                 == lax.broadcasted_iota(I32, (LANES, LANES), 1) // hd)
    zero_blk = jnp.zeros((hd, hd), F32)

    def block_diag(a, b):
        return jnp.concatenate([jnp.concatenate([a, zero_blk], axis=1),
                                jnp.concatenate([zero_blk, b], axis=1)], axis=0)

    def head_mean(x):
        first = jnp.sum(jnp.where(low, x, 0.0), -1, keepdims=True)
        second = jnp.sum(jnp.where(low, 0.0, x), -1, keepdims=True)
        return jnp.where(low, first, second) * (1.0 / hd)

    for p in range(H_A // 2):
        lg_f = [_log_sigmoid(decf_ref[h]) for h in (2 * p, 2 * p + 1)]
        lg_b = [_log_sigmoid(decb_ref[h]) for h in (2 * p, 2 * p + 1)]
        dmat = [_decay_matrix(lg_f[t], lg_b[t], c) for t in range(2)]
        lf = jnp.where(low[0:1], lg_f[0][:, 0:LANES], lg_f[1][:, 0:LANES])
        lb = jnp.where(low[0:1], lg_b[0][:, 0:LANES], lg_b[1][:, 0:LANES])
        zf = jnp.exp(lf * (c - 1.0 - pos)) * ATTN_SCALE
        zb = jnp.exp(lb * pos) * ATTN_SCALE
        xf = jnp.exp(lf * (pos + 1.0))
        xb = jnp.exp(lb * (c - pos))
        gcf = jnp.exp(lf * float(c))
        gcb = jnp.exp(lb * float(c))
        pair = slice(p * LANES, (p + 1) * LANES)

        def chunk(i, base):
            return pa_ref[i * c:(i + 1) * c, base + p * LANES:base + (p + 1) * LANES]

        kv_f, kv_b = [], []
        for i in range(n_chunks):
            kf = chunk(i, W_A).astype(F32)
            v = chunk(i, 2 * W_A)
            kv_f.append(jnp.where(same_head, _dot((kf * zf).T.astype(BF16), v), 0.0))
            kv_b.append(jnp.where(same_head, _dot((kf * zb).T.astype(BF16), v), 0.0))

        s = block_diag(stf_ref[0, 0, 2 * p], stf_ref[0, 0, 2 * p + 1])
        seen_f = []
        for i in range(n_chunks):
            seen_f.append(s)
            s = gcf * s + kv_f[i]
        s = block_diag(stb_ref[0, 0, 2 * p], stb_ref[0, 0, 2 * p + 1])
        seen_b = [None] * n_chunks
        for i in reversed(range(n_chunks)):
            seen_b[i] = s
            s = gcb * s + kv_b[i]

        for i in range(n_chunks):
            q_pair, k_pair, v_pair = chunk(i, 0), chunk(i, W_A), chunk(i, 2 * W_A)
            inner = []
            for t in range(2):
                zero = jnp.zeros_like(q_pair)
                q = jnp.where(low, q_pair, zero) if t == 0 else jnp.where(low, zero, q_pair)
                inner.append(_dot((_dot_nt(q, k_pair) * dmat[t]).astype(BF16), v_pair))
            qf = q_pair.astype(F32)
            lhs = jnp.concatenate([(qf * xf).astype(BF16), (qf * xb).astype(BF16)], axis=1)
            rhs = jnp.concatenate([seen_f[i], seen_b[i]], axis=0).astype(BF16)
            o = jnp.where(low, inner[0], inner[1]) + _dot(lhs, rhs)
            mu = head_mean(o)
            oc = o - mu
            var = head_mean(oc * oc)
            on = oc * lax.rsqrt(var + GN_EPS) * gng_ref[:, pair] + gnb_ref[:, pair]
            y = on * _silu(chunk(i, 3 * W_A).astype(F32))
            ya_ref[i * c:(i + 1) * c, pair] = y.astype(BF16)


def _lat_retention(pa, st_f, st_b, layer, decf, decb, gng, gnb):
    lat0 = T_CTX // DEC_SEQ
    st_spec = pl.BlockSpec((1, 1, H_A, HEAD_DIM, HEAD_DIM), lambda b: (b, layer, 0, 0, 0))
    return pl.pallas_call(
        _lat_ret_kernel,
        out_shape=jax.ShapeDtypeStruct((T_LAT, W_A), BF16),
        grid=(DEC_BATCH,),
        in_specs=[
            pl.BlockSpec((DEC_SEQ, PA_W), lambda b: (lat0 + b, 0)),
            st_spec, st_spec,
            pl.BlockSpec((H_A, 1, RET_CHUNK), lambda b: (0, 0, 0)),
            pl.BlockSpec((H_A, 1, RET_CHUNK), lambda b: (0, 0, 0)),
            pl.BlockSpec((1, W_A), lambda b: (0, 0)),
            pl.BlockSpec((1, W_A), lambda b: (0, 0)),
        ],
        out_specs=pl.BlockSpec((DEC_SEQ, W_A), lambda b: (b, 0)),
        compiler_params=pltpu.CompilerParams(
            dimension_semantics=("arbitrary",), vmem_limit_bytes=VMEM_LIMIT),
        name="lat_retention",
    )(pa, st_f, st_b, decf, decb, gng, gnb)


def _rope(x, cos, sin_signed, swap):
    return x.astype(F32) * cos + _dot(x, swap) * sin_signed


def _lat_win_kernel(sink_ref, pq_ref, pseq_ref, kctx_ref, vctx_ref, cos_ref, sin_ref, yb_ref,
                    krope_ref, kc_ref, vc_ref, mask_ref):
    n = pl.program_id(1)
    hd = HEAD_DIM
    qb = WIN_QB
    band = WIN_BAND
    group = H_B // KV_B
    rot = _lane_xor_matrix(LANES, hd // 2)
    swap = _lane_xor_matrix(LANES, hd)

    @pl.when(n == 0)
    def _():
        k = pseq_ref[:, W_B:W_B + KV_W_B]
        krope_ref[...] = _rope(k, cos_ref[...], sin_ref[...], rot).astype(BF16)
        kc_ref[...] = jnp.concatenate([kctx_ref[0, 0, j] for j in range(KV_B)], axis=1).astype(BF16)
        vc_ref[...] = jnp.concatenate([vctx_ref[0, 0, j] for j in range(KV_B)], axis=1).astype(BF16)
        q_in_blk = lax.broadcasted_iota(I32, (group * qb, band), 0) % qb
        k_in_band = lax.broadcasted_iota(I32, (group * qb, band), 1)
        for ty, lead in enumerate((0, WINDOW, band - qb)):
            mask_ref[ty] = jnp.where(jnp.abs(k_in_band - lead - q_in_blk) <= WINDOW, 0.0, NEG_INF)

    q_rows = pl.ds(pl.multiple_of(n * qb, qb), qb)
    cos_q = cos_ref[q_rows, :]
    sin_q = sin_ref[q_rows, :]
    low = lax.broadcasted_iota(I32, (qb, LANES), 1) < hd
    q_heads = []
    for p in range(H_B // 2):
        q_pair = _rope(pq_ref[:, p * LANES:(p + 1) * LANES], cos_q, sin_q, rot) * ATTN_SCALE
        for hh in (2 * p, 2 * p + 1):
            q = jnp.where(low, q_pair, 0.0) if hh % 2 == 0 else jnp.where(low, 0.0, q_pair)
            q = q.astype(BF16)
            q_heads.append(_dot(q, swap).astype(BF16) if hh % 2 != hh // group else q)

    ws = jnp.clip(n * qb - WINDOW, 0, DEC_SEQ - band)
    k_rows = pl.ds(pl.multiple_of(ws, WINDOW), band)
    n_blk = DEC_SEQ // qb
    band_bias = mask_ref[jnp.where(n == 0, 0, jnp.where(n == n_blk - 1, 2, 1))]
    head_of_row = lax.broadcasted_iota(I32, (group * qb, 1), 0) // qb
    kw = krope_ref[k_rows, :]
    vw = pseq_ref[k_rows, W_B + KV_W_B:W_B + 2 * KV_W_B]
    outs = []
    for j in range(KV_B):
        heads = [j * group + g for g in range(group)]
        qs = jnp.concatenate([q_heads[hh] for hh in heads], axis=0)
        s_loc = _dot_nt(qs, kw) + band_bias
        s_ctx = _dot_nt(qs, kc_ref[...])
        sink = jnp.zeros((group * qb, 1), F32)
        for g, hh in enumerate(heads):
            sink = jnp.where(head_of_row == g, sink_ref[hh], sink)
        m = jnp.maximum(jnp.maximum(jnp.max(s_loc, -1, keepdims=True),
                                    jnp.max(s_ctx, -1, keepdims=True)), sink)
        p_loc = jnp.exp(s_loc - m)
        p_ctx = jnp.exp(s_ctx - m)
        den = (jnp.sum(p_loc, -1, keepdims=True) + jnp.sum(p_ctx, -1, keepdims=True)
               + jnp.exp(sink - m))
        o = ((_dot(p_loc.astype(BF16), vw) + _dot(p_ctx.astype(BF16), vc_ref[...])) / den).astype(BF16)
        for g, hh in enumerate(heads):
            o_h = o[g * qb:(g + 1) * qb]
            outs.append(_dot(o_h, swap).astype(BF16) if hh % 2 != j else o_h)
    for p in range(H_B // 2):
        yb_ref[:, p * LANES:(p + 1) * LANES] = jnp.where(low, outs[2 * p], outs[2 * p + 1])


def _lat_window_attn(pb, cache_k, cache_v, layer, sink, cos_t, sin_t):
    n_blk = DEC_SEQ // WIN_QB
    lat_blk0 = T_CTX // WIN_QB
    lat_seq0 = T_CTX // DEC_SEQ
    ctx_spec = pl.BlockSpec((1, 1, KV_B, PAST_LEN, HEAD_DIM), lambda b, n: (b, layer, 0, 0, 0))
    return pl.pallas_call(
        _lat_win_kernel,
        out_shape=jax.ShapeDtypeStruct((T_LAT, W_B), BF16),
        grid=(DEC_BATCH, n_blk),
        in_specs=[
            pl.BlockSpec(memory_space=pltpu.SMEM),
            pl.BlockSpec((WIN_QB, PB_W), lambda b, n: (lat_blk0 + b * n_blk + n, 0)),
            pl.BlockSpec((DEC_SEQ, PB_W), lambda b, n: (lat_seq0 + b, 0)),
            ctx_spec, ctx_spec,
            pl.BlockSpec((DEC_SEQ, LANES), lambda b, n: (0, 0)),
            pl.BlockSpec((DEC_SEQ, LANES), lambda b, n: (0, 0)),
        ],
        out_specs=pl.BlockSpec((WIN_QB, W_B), lambda b, n: (b * n_blk + n, 0)),
        scratch_shapes=[
            pltpu.VMEM((DEC_SEQ, KV_W_B), BF16),
            pltpu.VMEM((PAST_LEN, KV_W_B), BF16),
            pltpu.VMEM((PAST_LEN, KV_W_B), BF16),
            pltpu.VMEM((3, (H_B // KV_B) * WIN_QB, WIN_BAND), F32),
        ],
        compiler_params=pltpu.CompilerParams(
            dimension_semantics=("arbitrary", "arbitrary"), vmem_limit_bytes=VMEM_LIMIT),
        name="lat_window_attn",
    )(sink, pb, pb, cache_k, cache_v, cos_t, sin_t)


NA_Q = NA_TILE_ROWS * GRID_W
NA_K = NA_KEY_ROWS * GRID_W
NA_TILES = DEC_SEQ // NA_Q
LAT_ROWS = DEC_SEQ // GRID_W


def _na_window_start(tile):
    return jnp.clip(tile * NA_TILE_ROWS - NA_ROWS // 2, 0, LAT_ROWS - NA_KEY_ROWS)


def _lat_na_kernel(pq_ref, pseq_ref, kctx_ref, vctx_ref, bias_ref, yc_ref, kc_ref, vc_ref):
    t = pl.program_id(1)
    hd = HEAD_DIM
    nq = NA_Q

    @pl.when(t == 0)
    def _():
        for p in range(H_C // 2):
            kc_ref[p] = jnp.concatenate([kctx_ref[0, 0, 2 * p], kctx_ref[0, 0, 2 * p + 1]], axis=1).astype(BF16)
            vc_ref[p] = jnp.concatenate([vctx_ref[0, 0, 2 * p], vctx_ref[0, 0, 2 * p + 1]], axis=1).astype(BF16)

    k_rows = pl.ds(pl.multiple_of(_na_window_start(t) * GRID_W, GRID_W), NA_K)
    low = lax.broadcasted_iota(I32, (nq, LANES), 1) < hd
    loc, ctx = [], []
    for hh in range(H_C):
        p = hh // 2
        q_pair = pq_ref[:, p * LANES:(p + 1) * LANES] * ATTN_SCALE
        zero = jnp.zeros_like(q_pair)
        q = jnp.where(low, q_pair, zero) if hh % 2 == 0 else jnp.where(low, zero, q_pair)
        loc.append(_dot_nt(q, pseq_ref[k_rows, W_C + p * LANES:W_C + (p + 1) * LANES]) + bias_ref[0, hh])
        ctx.append(_dot_nt(q, kc_ref[p]))
    outs = []
    for g0 in range(0, H_C, NA_STACK):
        s_loc = jnp.concatenate(loc[g0:g0 + NA_STACK], axis=0)
        s_ctx = jnp.concatenate(ctx[g0:g0 + NA_STACK], axis=0)
        m = jnp.maximum(jnp.max(s_loc, -1, keepdims=True), jnp.max(s_ctx, -1, keepdims=True))
        p_loc = jnp.exp(s_loc - m)
        p_ctx = jnp.exp(s_ctx - m)
        den = jnp.sum(p_loc, -1, keepdims=True) + jnp.sum(p_ctx, -1, keepdims=True)
        p_loc = p_loc.astype(BF16)
        p_ctx = p_ctx.astype(BF16)
        for hh in range(g0, g0 + NA_STACK):
            p = hh // 2
            rows = slice((hh - g0) * nq, (hh - g0 + 1) * nq)
            vw = pseq_ref[k_rows, 2 * W_C + p * LANES:2 * W_C + (p + 1) * LANES]
            outs.append((_dot(p_loc[rows], vw) + _dot(p_ctx[rows], vc_ref[p])) / den[rows])
    for p in range(H_C // 2):
        yc_ref[:, p * LANES:(p + 1) * LANES] = jnp.where(low, outs[2 * p], outs[2 * p + 1]).astype(BF16)


def _na_tile_type(t):
    return jnp.where(t == 0, 0, jnp.where(t == NA_TILES - 1, 2, 1))


def _lat_na_attn(pc, cache_k, cache_v, layer, maskbias):
    lat_tile0 = T_CTX // NA_Q
    lat_seq0 = T_CTX // DEC_SEQ
    ctx_spec = pl.BlockSpec((1, 1, H_C, PAST_LEN, HEAD_DIM), lambda b, t: (b, layer, 0, 0, 0))
    return pl.pallas_call(
        _lat_na_kernel,
        out_shape=jax.ShapeDtypeStruct((T_LAT, W_C), BF16),
        grid=(DEC_BATCH, NA_TILES),
        in_specs=[
            pl.BlockSpec((NA_Q, PC_W), lambda b, t: (lat_tile0 + b * NA_TILES + t, 0)),
            pl.BlockSpec((DEC_SEQ, PC_W), lambda b, t: (lat_seq0 + b, 0)),
            ctx_spec, ctx_spec,
            pl.BlockSpec((1, H_C, NA_Q, NA_K), lambda b, t: (_na_tile_type(t), 0, 0, 0)),
        ],
        out_specs=pl.BlockSpec((NA_Q, W_C), lambda b, t: (b * NA_TILES + t, 0)),
        scratch_shapes=[
            pltpu.VMEM((H_C // 2, PAST_LEN, LANES), BF16),
            pltpu.VMEM((H_C // 2, PAST_LEN, LANES), BF16),
        ],
        compiler_params=pltpu.CompilerParams(
            dimension_semantics=("arbitrary", "arbitrary"), vmem_limit_bytes=VMEM_LIMIT),
        name="lat_na_attn",
    )(pc, pc, cache_k, cache_v, maskbias)


def _na_block_index():
    out = np.zeros((3, NA_TILE_ROWS, NA_KEY_ROWS), np.int32)
    for ty, tile in enumerate((0, 1, NA_TILES - 1)):
        r = tile * NA_TILE_ROWS
        ws = int(np.clip(r - NA_ROWS // 2, 0, LAT_ROWS - NA_KEY_ROWS))
        for qq in range(NA_TILE_ROWS):
            qr = r + qq
            r0 = int(np.clip(qr - NA_ROWS // 2, 0, LAT_ROWS - NA_ROWS))
            for kk in range(NA_KEY_ROWS):
                kr = ws + kk
                out[ty, qq, kk] = kr - qr + NA_ROWS - 1 if r0 <= kr < r0 + NA_ROWS else 2 * NA_ROWS - 1
    return out


def _na_maskbias(rpb):
    qc = np.arange(GRID_W)[:, None]
    kc = np.arange(GRID_W)[None, :]
    c0 = np.clip(qc - NA_COLS // 2, 0, GRID_W - NA_COLS)
    col_ok = (kc >= c0) & (kc < c0 + NA_COLS)
    ci = np.clip(kc - qc + NA_COLS - 1, 0, 2 * NA_COLS - 2)
    onehot = (ci[None] == np.arange(2 * NA_COLS - 1)[:, None, None]).astype(np.float32)
    cols = jnp.einsum("hab,bqk->haqk", rpb, jnp.asarray(onehot), precision=lax.Precision.HIGHEST)
    cols = jnp.where(jnp.asarray(col_ok)[None, None], cols, NEG_INF)
    cols = jnp.concatenate([cols, jnp.full((H_C, 1, GRID_W, GRID_W), NEG_INF, F32)], axis=1)
    block_index = _na_block_index()

    def assemble(cols_ref, out_ref):
        for ty in range(3):
            for qq in range(NA_TILE_ROWS):
                for kk in range(NA_KEY_ROWS):
                    out_ref[ty, 0, qq * GRID_W:(qq + 1) * GRID_W, kk * GRID_W:(kk + 1) * GRID_W] = (
                        cols_ref[0, int(block_index[ty, qq, kk])])

    return pl.pallas_call(
        assemble,
        out_shape=jax.ShapeDtypeStruct((3, H_C, NA_Q, NA_K), F32),
        grid=(H_C,),
        in_specs=[pl.BlockSpec((1, 2 * NA_ROWS, GRID_W, GRID_W), lambda h: (h, 0, 0, 0))],
        out_specs=pl.BlockSpec((3, 1, NA_Q, NA_K), lambda h: (0, h, 0, 0)),
        compiler_params=pltpu.CompilerParams(dimension_semantics=("arbitrary",)),
        name="na_bias_assemble",
    )(cols)


def _rope_tables():
    t = np.arange(DEC_SEQ)
    n_freq = HEAD_DIM // 4
    inv = (ROPE_BASE ** (-np.arange(n_freq, dtype=np.float32) / n_freq)).astype(np.float32)
    row = (t // GRID_W).astype(np.float32)[:, None] * inv
    col = (t % GRID_W).astype(np.float32)[:, None] * inv
    ang = np.concatenate([row, col], -1)
    cos, sin = np.cos(ang), np.sin(ang)
    cos_h = np.concatenate([cos, cos], -1)
    sin_h = np.concatenate([-sin, sin], -1)
    reps = LANES // HEAD_DIM
    return (jnp.asarray(np.tile(cos_h, (1, reps)), F32), jnp.asarray(np.tile(sin_h, (1, reps)), F32))


def _first_index_of(mask, iota, sentinel):
    return jnp.min(jnp.where(mask, iota, sentinel), axis=0, keepdims=True)


def _route(logits, b_col):
    n = logits.shape[1]
    scores = jax.nn.sigmoid(logits)
    sel = scores + b_col
    io_g = lax.broadcasted_iota(I32, (GROUP_SIZE, n), 0)
    gs_rows = []
    for g in range(N_GROUPS):
        s = sel[g * GROUP_SIZE:(g + 1) * GROUP_SIZE]
        m1 = jnp.max(s, axis=0, keepdims=True)
        i1 = _first_index_of(s == m1, io_g, GROUP_SIZE)
        m2 = jnp.max(jnp.where(io_g == i1, PICKED, s), axis=0, keepdims=True)
        gs_rows.append(m1 + m2)
    gs = jnp.concatenate(gs_rows, axis=0)
    io_n = lax.broadcasted_iota(I32, (N_GROUPS, n), 0)
    gsel = jnp.zeros((N_GROUPS, n), F32)
    for _ in range(TOPK_GROUPS):
        mg = jnp.max(gs, axis=0, keepdims=True)
        gi = _first_index_of(gs == mg, io_n, N_GROUPS)
        hit = io_n == gi
        gsel = jnp.where(hit, 1.0, gsel)
        gs = jnp.where(hit, PICKED, gs)
    cand = jnp.concatenate(
        [jnp.where(gsel[g:g + 1] > 0.5, sel[g * GROUP_SIZE:(g + 1) * GROUP_SIZE], NEG_INF)
         for g in range(N_GROUPS)], axis=0)
    io_e = lax.broadcasted_iota(I32, (N_EXPERTS, n), 0)
    picks, raw = [], []
    for _ in range(TOP_K):
        mv = jnp.max(cand, axis=0, keepdims=True)
        ei = _first_index_of(cand == mv, io_e, N_EXPERTS)
        hit = io_e == ei
        picks.append((hit, ei))
        raw.append(jnp.sum(jnp.where(hit, scores, 0.0), axis=0, keepdims=True))
        cand = jnp.where(hit, PICKED, cand)
    return picks, raw


def _post_mixer_kernel(xc_ref, xl_ref, mixc_ref, ya_ref, yb_ref, yc_ref, wout_ref, mod_ref, g_ref, b_ref,
                       wr_ref, br_ref,
                       x1_ref, h2_ref, eidx_ref, wsel_ref, rank_ref, cnt_ref):
    step = pl.program_id(0)
    tm = TM_TOK
    is_ctx = step < T_CTX // (POST_TILES * tm)

    @pl.when(step == 0)
    def _():
        cnt_ref[...] = jnp.zeros_like(cnt_ref)

    def project(s):
        rows = slice(s * tm, (s + 1) * tm)
        ci = _cond_row(step * POST_TILES + s, tm)
        gate1 = mod_ref[pl.ds(ci, 1), 2 * D_MODEL:3 * D_MODEL]
        sh2 = mod_ref[pl.ds(ci, 1), 3 * D_MODEL:4 * D_MODEL]
        sc2 = mod_ref[pl.ds(ci, 1), 4 * D_MODEL:5 * D_MODEL]
        mix_lat = jnp.concatenate([ya_ref[rows, :], yb_ref[rows, :], yc_ref[rows, :]], axis=-1)
        mix = jnp.where(is_ctx, mixc_ref[rows, :], mix_lat)
        y = _dot(mix, wout_ref[...])
        x = jnp.where(is_ctx, xc_ref[rows, :], xl_ref[rows, :])
        x1 = _layer_norm(ALPHA * x + gate1 * y, g_ref[...], b_ref[...])
        x1_ref[rows, :] = x1
        h2 = x1 * (1.0 + sc2) + sh2
        h_hi = h2.astype(BF16)
        h2_ref[rows, :] = h_hi
        h_lo = (h2 - h_hi.astype(F32)).astype(BF16)
        both = (_dot(h_hi, wr_ref[...]) + _dot(h_lo, wr_ref[...])).T
        return both[0:N_EXPERTS] + both[N_EXPERTS:2 * N_EXPERTS]

    def route(s, logits):
        toks = s * tm
        routed = [_route(logits[:, g * LANES:(g + 1) * LANES], br_ref[...]) for g in range(tm // LANES)]
        multi_g = []
        for picks, _ in routed:
            m = jnp.zeros((N_EXPERTS, LANES), F32)
            for hit, _ in picks:
                m = m + jnp.where(hit, 1.0, 0.0)
            multi_g.append(m)
        multi = jnp.concatenate(multi_g, axis=1)
        before = (lax.broadcasted_iota(I32, (tm, tm), 0) < lax.broadcasted_iota(I32, (tm, tm), 1))
        cum = _dot(multi.astype(BF16), jnp.where(before, 1.0, 0.0).astype(BF16))
        pad = jnp.zeros((SUBLANES - TOP_K, LANES), F32)
        for g, (picks, raw) in enumerate(routed):
            lanes = slice(toks + g * LANES, toks + (g + 1) * LANES)
            total = raw[0]
            for r in raw[1:]:
                total = total + r
            scale = ROUTE_SCALE / total
            cum_g = cum[:, g * LANES:(g + 1) * LANES]
            eidx_ref[:, lanes] = jnp.concatenate([ei for _, ei in picks] + [pad.astype(I32)], axis=0)
            wsel_ref[:, lanes] = jnp.concatenate([r * scale for r in raw] + [pad], axis=0)
            rank_ref[:, lanes] = jnp.concatenate(
                [jnp.sum(jnp.where(hit, cum_g, 0.0), axis=0, keepdims=True) for hit, _ in picks] + [pad],
                axis=0).astype(I32)
        tile_lane = lax.broadcasted_iota(I32, (N_EXPERTS, LANES), 1)
        cnt_ref[...] = jnp.where(tile_lane == step * POST_TILES + s,
                                 jnp.sum(multi, axis=1, keepdims=True), cnt_ref[...])

    logits = [project(s) for s in range(POST_TILES)]
    for s in range(POST_TILES):
        route(s, logits[s])


def _post_mixer(x_ctx, x_lat, mix_c, ya, yb, yc, w_out_bf16, mod, ln_g, ln_b, wr_split, b_router_col):
    tm = POST_TILES * TM_TOK
    n_ctx = T_CTX // tm
    ctx_map = lambda i: (jnp.minimum(i, n_ctx - 1), 0)
    lat_map = lambda i: (jnp.maximum(i - n_ctx, 0), 0)
    row_map = lambda i: (i, 0)
    const = lambda i: (0, 0)
    tok_map = lambda i: (0, i)
    return pl.pallas_call(
        _post_mixer_kernel,
        out_shape=(
            jax.ShapeDtypeStruct((T_ALL, D_MODEL), F32),
            jax.ShapeDtypeStruct((T_ALL, D_MODEL), BF16),
            jax.ShapeDtypeStruct((SUBLANES, T_ALL), I32),
            jax.ShapeDtypeStruct((SUBLANES, T_ALL), F32),
            jax.ShapeDtypeStruct((SUBLANES, T_ALL), I32),
            jax.ShapeDtypeStruct((N_EXPERTS, LANES), F32),
        ),
        grid=(T_ALL // tm,),
        in_specs=[
            pl.BlockSpec((tm, D_MODEL), ctx_map),
            pl.BlockSpec((tm, D_MODEL), lat_map),
            pl.BlockSpec((tm, D_MODEL), ctx_map),
            pl.BlockSpec((tm, W_A), lat_map),
            pl.BlockSpec((tm, W_B), lat_map),
            pl.BlockSpec((tm, W_C), lat_map),
            pl.BlockSpec((D_MODEL, D_MODEL), const),
            pl.BlockSpec((N_COND, 6 * D_MODEL), const),
            pl.BlockSpec((1, D_MODEL), const),
            pl.BlockSpec((1, D_MODEL), const),
            pl.BlockSpec((D_MODEL, 2 * N_EXPERTS), const),
            pl.BlockSpec((N_EXPERTS, 1), const),
        ],
        out_specs=(
            pl.BlockSpec((tm, D_MODEL), row_map),
            pl.BlockSpec((tm, D_MODEL), row_map),
            pl.BlockSpec((SUBLANES, tm), tok_map),
            pl.BlockSpec((SUBLANES, tm), tok_map),
            pl.BlockSpec((SUBLANES, tm), tok_map),
            pl.BlockSpec((N_EXPERTS, LANES), const),
        ),
        compiler_params=pltpu.CompilerParams(
            dimension_semantics=("arbitrary",), vmem_limit_bytes=VMEM_LIMIT),
        name="post_mixer",
    )(x_ctx, x_lat, mix_c, ya, yb, yc, w_out_bf16, mod, ln_g, ln_b, wr_split, b_router_col)


PLAN_TILES = 4


def _plan_kernel(eidx_ref, rank_ref, nmat_ref, lslot_ref, unit_ref, gend_ref):
    step = pl.program_id(0)
    units = jnp.floor((nmat_ref[...] + (UNIT - 1.0)) * (1.0 / UNIT))
    units_bf = units.astype(BF16)
    earlier_e = (lax.broadcasted_iota(I32, (N_EXPERTS, N_EXPERTS), 1)
                 < lax.broadcasted_iota(I32, (N_EXPERTS, N_EXPERTS), 0))
    tri_e = jnp.where(earlier_e, 1.0, 0.0).astype(BF16)
    earlier_t = (lax.broadcasted_iota(I32, (LANES, LANES), 0) < lax.broadcasted_iota(I32, (LANES, LANES), 1))
    tri_t = jnp.where(earlier_t, 1.0, 0.0).astype(BF16)
    local_off = _dot(tri_e, units_bf)
    tile_off = _dot(units_bf, tri_t)
    per_expert = jnp.sum(units, axis=1, keepdims=True)
    blocks = jnp.floor((per_expert + (UNITS_PER_BLOCK - 1.0)) * (1.0 / UNITS_PER_BLOCK))
    blocks_l = jnp.broadcast_to(blocks, (N_EXPERTS, LANES))
    start_blk = _dot(tri_e, blocks_l.astype(BF16))
    end_blk = start_blk + blocks_l
    gend_ref[...] = (end_blk * BM).astype(I32)

    tile_lane = lax.broadcasted_iota(I32, (N_EXPERTS, LANES), 1)
    u = lax.broadcasted_iota(I32, (N_EXPERTS, MAX_UNITS), 1).astype(F32)
    io_e = lax.broadcasted_iota(I32, (N_EXPERTS, TM_TOK), 0)
    for s in range(PLAN_TILES):
        i = step * PLAN_TILES + s
        this_tile = tile_lane == i

        def column(a):
            return jnp.sum(jnp.where(this_tile, a, 0.0), axis=1, keepdims=True)

        lo, n_u = column(local_off), column(units)
        base_unit = start_blk[:, 0:1] * UNITS_PER_BLOCK + column(tile_off) - lo
        inside = jnp.where(u >= lo, jnp.where(u < lo + n_u, 1.0, 0.0), 0.0)
        dst_unit = jnp.sum(inside * (base_unit + u), axis=0, keepdims=True)
        used = jnp.sum(inside, axis=0, keepdims=True) > 0.5
        spare = (SPARE_UNIT0 + (i % 2) * MAX_UNITS).astype(F32) + u[0:1, :]
        unit_ref[s] = jnp.where(used, dst_unit, spare).astype(I32)

        toks = slice(s * TM_TOK, (s + 1) * TM_TOK)
        rows = []
        for k in range(TOP_K):
            hit = io_e == eidx_ref[k:k + 1, toks]
            seg = jnp.sum(jnp.where(hit, lo * UNIT, 0.0), axis=0, keepdims=True)
            rows.append(seg.astype(I32) + rank_ref[k:k + 1, toks])
        rows.append(jnp.full((SUBLANES - TOP_K, TM_TOK), -1, I32))
        lslot_ref[:, toks] = jnp.concatenate(rows, axis=0)


def _slot_plan(eidx, rank, nmat):
    tok_map = lambda i: (0, i)
    const = lambda i: (0, 0)
    lslot, unit_tab, gend = pl.pallas_call(
        _plan_kernel,
        out_shape=(
            jax.ShapeDtypeStruct((SUBLANES, T_ALL), I32),
            jax.ShapeDtypeStruct((N_TOK_TILES, 1, MAX_UNITS), I32),
            jax.ShapeDtypeStruct((N_EXPERTS, LANES), I32),
        ),
        grid=(N_TOK_TILES // PLAN_TILES,),
        in_specs=[
            pl.BlockSpec((SUBLANES, PLAN_TILES * TM_TOK), tok_map),
            pl.BlockSpec((SUBLANES, PLAN_TILES * TM_TOK), tok_map),
            pl.BlockSpec((N_EXPERTS, LANES), const),
        ],
        out_specs=(
            pl.BlockSpec((SUBLANES, PLAN_TILES * TM_TOK), tok_map),
            pl.BlockSpec((PLAN_TILES, 1, MAX_UNITS), lambda i: (i, 0, 0)),
            pl.BlockSpec((N_EXPERTS, LANES), const),
        ),
        compiler_params=pltpu.CompilerParams(dimension_semantics=("arbitrary",)),
        name="slot_plan",
    )(eidx, rank, nmat)
    return lslot, unit_tab.reshape(N_TOK_TILES * MAX_UNITS), gend[:, 0]


PACK_W = D_MODEL // 2
HI_HALF = -65536


def _pack_pairs(x):
    lo = lax.bitcast_convert_type(x[:, 0:PACK_W], I32)
    hi = lax.bitcast_convert_type(x[:, PACK_W:D_MODEL], I32)
    return lax.shift_right_logical(lo, 16) | (hi & HI_HALF)


def _unpack_pairs(u):
    lo = lax.bitcast_convert_type(lax.shift_left(u, 16), F32).astype(BF16)
    hi = lax.bitcast_convert_type(u & HI_HALF, F32).astype(BF16)
    return lo, hi


def _unit_rows(unit):
    row = unit * UNIT
    return pl.ds(row if isinstance(unit, int) else pl.multiple_of(row, UNIT), UNIT)


def _unit_copy(src, src_unit, dst, dst_unit, sem):
    return pltpu.make_async_copy(src.at[_unit_rows(src_unit)], dst.at[_unit_rows(dst_unit)], sem)


def _dispatch_kernel(gend_ref, tab_ref, h2_ref, lslot_ref, xs_hbm, zero_ref, local_ref, sem_zero, sem_rows):
    i = pl.program_id(0)
    buf = i % 2

    def drain(b):
        pltpu.make_async_copy(local_ref.at[b], xs_hbm.at[pl.ds(0, LOCAL_ROWS)], sem_rows.at[b]).wait()

    def has_rows(e):
        return gend_ref[e] > jnp.where(e == 0, 0, gend_ref[jnp.maximum(e - 1, 0)])

    def zero_copy(e):
        return pltpu.make_async_copy(
            zero_ref, xs_hbm.at[pl.ds(pl.multiple_of(gend_ref[e] - BM, BM), BM)], sem_zero)

    @pl.when(i == 0)
    def _():
        zero_ref[...] = jnp.zeros_like(zero_ref)

        def start(e, c):
            @pl.when(has_rows(e))
            def _():
                zero_copy(e).start()
            return c

        def wait(e, c):
            @pl.when(has_rows(e))
            def _():
                zero_copy(e).wait()
            return c

        def tail_copy(blk):
            return pltpu.make_async_copy(
                zero_ref, xs_hbm.at[pl.ds(pl.multiple_of(blk * BM, BM), BM)], sem_zero)

        def start_tail(blk, c):
            tail_copy(blk).start()
            return c

        def wait_tail(blk, c):
            tail_copy(blk).wait()
            return c

        n_used = gend_ref[N_EXPERTS - 1] // BM
        lax.fori_loop(0, N_EXPERTS, start, 0)
        lax.fori_loop(n_used, N_BLOCKS_ALL, start_tail, 0)
        lax.fori_loop(0, N_EXPERTS, wait, 0)
        lax.fori_loop(n_used, N_BLOCKS_ALL, wait_tail, 0)

    @pl.when(i >= 2)
    def _():
        drain(buf)

    h2 = h2_ref[...]
    units_per_chunk = PERM_CHUNK // UNIT
    local = local_ref.at[buf]
    for c in range(LOCAL_ROWS // PERM_CHUNK):
        slot = c * PERM_CHUNK + lax.broadcasted_iota(I32, (PERM_CHUNK, TM_TOK), 0)
        p = jnp.zeros((PERM_CHUNK, TM_TOK), F32)
        for k in range(TOP_K):
            p = jnp.where(slot == lslot_ref[k:k + 1, :], 1.0, p)
        local[c * PERM_CHUNK:(c + 1) * PERM_CHUNK, :] = _pack_pairs(_dot(p.astype(BF16), h2))
        for u in range(c * units_per_chunk, (c + 1) * units_per_chunk):
            _unit_copy(local, u, xs_hbm, tab_ref[i * MAX_UNITS + u], sem_rows.at[buf]).start()

    @pl.when(i == N_TOK_TILES - 1)
    def _():
        drain(1 - buf)
        drain(buf)


def _dispatch(h2, lslot, unit_tab, gend):
    return pl.pallas_call(
        _dispatch_kernel,
        out_shape=jax.ShapeDtypeStruct((N_SLOTS, PACK_W), I32),
        grid_spec=pltpu.PrefetchScalarGridSpec(
            num_scalar_prefetch=2,
            grid=(N_TOK_TILES,),
            in_specs=[
                pl.BlockSpec((TM_TOK, D_MODEL), lambda i, ge, tab: (i, 0)),
                pl.BlockSpec((SUBLANES, TM_TOK), lambda i, ge, tab: (0, i)),
            ],
            out_specs=pl.BlockSpec(memory_space=pl.ANY),
            scratch_shapes=[
                pltpu.VMEM((BM, PACK_W), I32),
                pltpu.VMEM((2, LOCAL_ROWS, PACK_W), I32),
                pltpu.SemaphoreType.DMA,
                pltpu.SemaphoreType.DMA((2,)),
            ],
        ),
        compiler_params=pltpu.CompilerParams(
            dimension_semantics=("arbitrary",), vmem_limit_bytes=VMEM_LIMIT),
        name="moe_dispatch",
    )(gend, unit_tab, h2, lslot)


def _expert_kernel(gend_ref, xs_hbm, wgu_ref, wdown_ref, ys_hbm, xbuf, ybuf, wgu_bf, wdown_bf, sem_in, sem_out):
    e = pl.program_id(0)
    first = jnp.where(e == 0, 0, gend_ref[jnp.maximum(e - 1, 0)]) // BM
    last = gend_ref[e] // BM
    n_used = gend_ref[N_EXPERTS - 1] // BM

    def rows_of(blk):
        return pl.ds(pl.multiple_of(blk * BM, BM), BM)

    def in_copy(blk):
        return pltpu.make_async_copy(xs_hbm.at[rows_of(blk)], xbuf.at[blk % X_BUFS], sem_in.at[blk % X_BUFS])

    def out_copy(blk):
        return pltpu.make_async_copy(ybuf.at[blk % 2], ys_hbm.at[rows_of(blk)], sem_out.at[blk % 2])

    @pl.when(e == 0)
    def _():
        for a in range(X_AHEAD):
            @pl.when(a < n_used)
            def _(a=a):
                in_copy(a).start()

    @pl.when(last > first)
    def _():
        wgu_bf[...] = wgu_ref[0, 0].astype(BF16)
        wdown_bf[...] = wdown_ref[0, 0].astype(BF16)

    def block(blk, carry):
        @pl.when(blk + X_AHEAD < n_used)
        def _():
            in_copy(blk + X_AHEAD).start()

        in_copy(blk).wait()

        @pl.when(blk >= 2)
        def _():
            out_copy(blk - 2).wait()

        x_lo, x_hi = _unpack_pairs(xbuf[blk % X_BUFS])
        gu = _dot(x_lo, wgu_bf[0:PACK_W, :]) + _dot(x_hi, wgu_bf[PACK_W:D_MODEL, :])
        act = _silu(gu[:, 0:D_EXPERT]) * gu[:, D_EXPERT:2 * D_EXPERT]
        y = _dot(act.astype(BF16), wdown_bf[...])
        ybuf[blk % 2] = _pack_pairs(y.astype(BF16).astype(F32))
        out_copy(blk).start()
        return carry

    lax.fori_loop(first, last, block, 0)

    @pl.when(e == N_EXPERTS - 1)
    def _():
        @pl.when(n_used >= 2)
        def _():
            out_copy(n_used - 2).wait()

        @pl.when(n_used >= 1)
        def _():
            out_copy(n_used - 1).wait()

        ybuf[0] = jnp.zeros((BM, PACK_W), I32)

        def tail_copy(blk):
            return pltpu.make_async_copy(ybuf.at[0], ys_hbm.at[rows_of(blk)], sem_out.at[0])

        def start(blk, c):
            tail_copy(blk).start()
            return c

        def wait(blk, c):
            tail_copy(blk).wait()
            return c

        lax.fori_loop(n_used, N_BLOCKS_ALL, start, 0)
        lax.fori_loop(n_used, N_BLOCKS_ALL, wait, 0)


def _experts(xs, w_gu, w_down, layer, gend):
    return pl.pallas_call(
        _expert_kernel,
        out_shape=jax.ShapeDtypeStruct((N_SLOTS, PACK_W), I32),
        grid_spec=pltpu.PrefetchScalarGridSpec(
            num_scalar_prefetch=1,
            grid=(N_EXPERTS,),
            in_specs=[
                pl.BlockSpec(memory_space=pl.ANY),
                pl.BlockSpec((1, 1, D_MODEL, 2 * D_EXPERT), lambda e, ge: (layer, e, 0, 0)),
                pl.BlockSpec((1, 1, D_EXPERT, D_MODEL), lambda e, ge: (layer, e, 0, 0)),
            ],
            out_specs=pl.BlockSpec(memory_space=pl.ANY),
            scratch_shapes=[
                pltpu.VMEM((X_BUFS, BM, PACK_W), I32),
                pltpu.VMEM((2, BM, PACK_W), I32),
                pltpu.VMEM((D_MODEL, 2 * D_EXPERT), BF16),
                pltpu.VMEM((D_EXPERT, D_MODEL), BF16),
                pltpu.SemaphoreType.DMA((X_BUFS,)),
                pltpu.SemaphoreType.DMA((2,)),
            ],
        ),
        compiler_params=pltpu.CompilerParams(
            dimension_semantics=("arbitrary",), vmem_limit_bytes=VMEM_LIMIT),
        name="moe_experts",
    )(gend, xs, w_gu, w_down)


def _combine_kernel(tab_ref, x1_ref, h2_ref, lslot_ref, wsel_ref, ys_hbm, wsgu_ref, wsdown_ref, mod_ref,
                    g_ref, b_ref, outc_ref, outl_ref, local_ref, sel_ref, ylo_ref, yhi_ref, sem_rows):
    i = pl.program_id(0)
    tm = TM_TOK
    buf = i % 2

    def fetch_unit(tile, b, u):
        _unit_copy(ys_hbm, tab_ref[tile * MAX_UNITS + u], local_ref.at[b], u, sem_rows.at[b]).start()

    def drain(b):
        pltpu.make_async_copy(ys_hbm.at[pl.ds(0, LOCAL_ROWS)], local_ref.at[b], sem_rows.at[b]).wait()

    @pl.when(i == 0)
    def _():
        def body(u, c):
            fetch_unit(0, 0, u)
            return c

        lax.fori_loop(0, MAX_UNITS, body, 0, unroll=8)

    sgu = _dot(h2_ref[...], wsgu_ref[...])
    act = _silu(sgu[:, 0:D_SHARED]) * sgu[:, D_SHARED:2 * D_SHARED]
    f = _dot(act.astype(BF16), wsdown_ref[...])

    nxt = jnp.minimum(i + 1, N_TOK_TILES - 1)
    n_groups = tm // SEL_ROWS
    units_per_group = MAX_UNITS // n_groups
    slot = lax.broadcasted_iota(I32, (SEL_ROWS, LOCAL_ROWS), 1)
    for g in range(n_groups):
        rows = slice(g * SEL_ROWS, (g + 1) * SEL_ROWS)
        sel = jnp.zeros((SEL_ROWS, LOCAL_ROWS), F32)
        for k in range(TOP_K):
            sel = jnp.where(slot == lslot_ref[rows, k:k + 1], wsel_ref[rows, k:k + 1], sel)
        sel_ref[rows, :] = sel.astype(BF16)
        for u in range(g * units_per_group, (g + 1) * units_per_group):
            fetch_unit(nxt, 1 - buf, u)

    local = local_ref.at[buf]
    drain(buf)
    for c in range(LOCAL_ROWS // PERM_CHUNK):
        rows = slice(c * PERM_CHUNK, (c + 1) * PERM_CHUNK)
        ylo_ref[rows, :], yhi_ref[rows, :] = _unpack_pairs(local[rows, :])
    sel = sel_ref[...]
    f = f + jnp.concatenate([_dot(sel, ylo_ref[...]), _dot(sel, yhi_ref[...])], axis=-1)
    ci = _cond_row(i, tm)
    gate2 = mod_ref[pl.ds(ci, 1), 5 * D_MODEL:6 * D_MODEL]
    out = _layer_norm(ALPHA * x1_ref[...] + gate2 * f, g_ref[...], b_ref[...])

    @pl.when(i < T_CTX // tm)
    def _():
        outc_ref[...] = out

    @pl.when(i >= T_CTX // tm)
    def _():
        outl_ref[...] = out

    @pl.when(i == N_TOK_TILES - 1)
    def _():
        drain(1 - buf)


def _combine(x1, h2, lslot_rows, wsel_rows, unit_tab, ys, w_sgu_bf16, w_sdown_bf16, mod, ln_g, ln_b):
    n_ctx = T_CTX // TM_TOK
    row_map = lambda i, tab: (i, 0)
    const = lambda i, tab: (0, 0)
    return pl.pallas_call(
        _combine_kernel,
        out_shape=(jax.ShapeDtypeStruct((T_CTX, D_MODEL), F32),
                   jax.ShapeDtypeStruct((T_LAT, D_MODEL), F32)),
        grid_spec=pltpu.PrefetchScalarGridSpec(
            num_scalar_prefetch=1,
            grid=(N_TOK_TILES,),
            in_specs=[
                pl.BlockSpec((TM_TOK, D_MODEL), row_map),
                pl.BlockSpec((TM_TOK, D_MODEL), row_map),
                pl.BlockSpec((TM_TOK, SUBLANES), row_map),
                pl.BlockSpec((TM_TOK, SUBLANES), row_map),
                pl.BlockSpec(memory_space=pl.ANY),
                pl.BlockSpec((D_MODEL, 2 * D_SHARED), const),
                pl.BlockSpec((D_SHARED, D_MODEL), const),
                pl.BlockSpec((N_COND, 6 * D_MODEL), const),
                pl.BlockSpec((1, D_MODEL), const),
                pl.BlockSpec((1, D_MODEL), const),
            ],
            out_specs=(pl.BlockSpec((TM_TOK, D_MODEL), lambda i, tab: (jnp.minimum(i, n_ctx - 1), 0)),
                       pl.BlockSpec((TM_TOK, D_MODEL), lambda i, tab: (jnp.maximum(i - n_ctx, 0), 0))),
            scratch_shapes=[
                pltpu.VMEM((2, LOCAL_ROWS, PACK_W), I32),
                pltpu.VMEM((TM_TOK, LOCAL_ROWS), BF16),
                pltpu.VMEM((LOCAL_ROWS, PACK_W), BF16),
                pltpu.VMEM((LOCAL_ROWS, PACK_W), BF16),
                pltpu.SemaphoreType.DMA((2,)),
            ],
        ),
        compiler_params=pltpu.CompilerParams(
            dimension_semantics=("arbitrary",), vmem_limit_bytes=VMEM_LIMIT),
        name="moe_combine",
    )(unit_tab, x1, h2, lslot_rows, wsel_rows, ys, w_sgu_bf16, w_sdown_bf16, mod, ln_g, ln_b)


def _lane_rows(v, width):
    return jnp.broadcast_to(v.astype(F32)[:, None, None], (v.shape[0], 1, width))


def kernel(x_prompt, x_sample, state_ret_fwd, state_ret_bwd, cache_win_k, cache_win_v, cache_na_k, cache_na_v, c, c_ctx, w_in, w_out, ret_decay_fwd, ret_decay_bwd, ret_gn_g, ret_gn_b, win_sink, na_rpb, w_mod, b_mod, ln1_g, ln1_b, ln2_g, ln2_b, w_router, b_router, w_expert_gu, w_expert_down, w_shared_gu, w_shared_down):
    cond = jnp.concatenate(
        [c_ctx[None, :], c, jnp.zeros((N_COND - 1 - DEC_BATCH, D_MODEL), F32)], axis=0)
    mod_all = _modulation(cond, w_mod, b_mod)
    cos_t, sin_t = _rope_tables()

    x_ctx = x_prompt.reshape(T_CTX, D_MODEL)
    x_lat = x_sample.reshape(T_LAT, D_MODEL)
    sf_l, sb_l, caches = [], [], ()
    for l in range(DEPTH):
        mod = mod_all[l]
        pa, pb, pc, *caches = _in_projection(x_ctx, x_lat, mod, w_in[l].astype(BF16), tuple(caches))
        decf_s, decb_s = _lane_rows(ret_decay_fwd[l], SEQ), _lane_rows(ret_decay_bwd[l], SEQ)
        gng, gnb = ret_gn_g[l][None, :], ret_gn_b[l][None, :]
        mix_c, st_f, st_b = _ctx_mixers(pa, pb, pc, win_sink[l], decf_s, decb_s, gng, gnb)
        sf_l.append(st_f)
        sb_l.append(st_b)
        ya = _lat_retention(pa, state_ret_fwd, state_ret_bwd, l,
                            _lane_rows(ret_decay_fwd[l], RET_CHUNK), _lane_rows(ret_decay_bwd[l], RET_CHUNK),
                            gng, gnb)
        yb = _lat_window_attn(pb, cache_win_k, cache_win_v, l, win_sink[l], cos_t, sin_t)
        yc = _lat_na_attn(pc, cache_na_k, cache_na_v, l, _na_maskbias(na_rpb[l]))

        wr_hi = w_router[l].astype(BF16)
        wr_lo = (w_router[l] - wr_hi.astype(F32)).astype(BF16)
        x1, h2, eidx, wsel, rank, counts = _post_mixer(
            x_ctx, x_lat, mix_c, ya, yb, yc, w_out[l].astype(BF16), mod, ln1_g[l][None, :],
            ln1_b[l][None, :], jnp.concatenate([wr_hi, wr_lo], axis=1), b_router[l][:, None])
        lslot, unit_tab, gend = _slot_plan(eidx, rank, counts)
        xs = _dispatch(h2, lslot, unit_tab, gend)
        ys = _experts(xs, w_expert_gu, w_expert_down, l, gend)
        x_ctx, x_lat = _combine(x1, h2, lslot.T, wsel.T, unit_tab, ys, w_shared_gu[l].astype(BF16),
                                w_shared_down[l].astype(BF16), mod, ln2_g[l][None, :], ln2_b[l][None, :])

    y_prompt = x_ctx.reshape(BATCH, SEQ, D_MODEL)
    y_sample = x_lat.reshape(DEC_BATCH, DEC_SEQ, D_MODEL)
    new_sf = jnp.stack(sf_l, axis=1)
    new_sb = jnp.stack(sb_l, axis=1)

    return (y_prompt, y_sample, new_sf, new_sb, *caches)
```

```python
import functools

import numpy as np
import jax
import jax.numpy as jnp
from jax import lax
from jax.experimental import pallas as pl
from jax.experimental.pallas import tpu as pltpu

F32 = jnp.float32
BF16 = jnp.bfloat16
I32 = jnp.int32

D_MODEL = 1024
BATCH = 32
SEQ = 256
DEPTH = 2
DEC_BATCH = 4
DEC_SEQ = 2048
PAST_LEN = 256
GRID_W = 64
HEAD_DIM = 64
ATTN_SCALE = HEAD_DIM ** -0.5
H_A = 4
W_A = H_A * HEAD_DIM
GN_EPS = 1e-5
H_B = 6
KV_B = 2
W_B = H_B * HEAD_DIM
KV_W_B = KV_B * HEAD_DIM
WINDOW = 128
ROPE_BASE = 10000.0
H_C = 6
W_C = H_C * HEAD_DIM
NA_ROWS = 8
NA_COLS = 16
IN_WIDTH = 4 * W_A + W_B + 2 * KV_W_B + 3 * W_C
N_EXPERTS = 64
TOP_K = 6
N_GROUPS = 8
GROUP_SIZE = N_EXPERTS // N_GROUPS
TOPK_GROUPS = 4
D_EXPERT = 256
D_SHARED = 256
ROUTE_SCALE = 2.5
ALPHA = (2 * DEPTH) ** 0.25
LN_EPS = 1e-5
NEG_INF = -1e30
PICKED = -3e38

T_CTX = BATCH * SEQ
T_LAT = DEC_BATCH * DEC_SEQ
T_ALL = T_CTX + T_LAT
N_COND = 8

PA_W = 4 * W_A
PB_W = W_B + 2 * KV_W_B
PC_W = 3 * W_C

LANES = 128
SUBLANES = 8
VMEM_LIMIT = 56 * 1024 * 1024

TM_PROJ = 512
TM_TOK = 256
RET_CHUNK = 256
POST_TILES = 4
CTX_STACK = 3
NA_STACK = 6
WIN_QB = 256
WIN_BAND = WIN_QB + 2 * WINDOW
NA_TILE_ROWS = 4
NA_KEY_ROWS = 11
BM = 512
X_AHEAD = 3
X_BUFS = X_AHEAD + 1
UNIT = SUBLANES
UNITS_PER_BLOCK = BM // UNIT
PERM_CHUNK = 256
SEL_ROWS = 16
N_TOK_TILES = T_ALL // TM_TOK
LOCAL_ROWS = -(-(TM_TOK * TOP_K + N_EXPERTS * (UNIT - 1)) // PERM_CHUNK) * PERM_CHUNK
MAX_UNITS = LOCAL_ROWS // UNIT
N_ASSIGN = T_ALL * TOP_K
N_BLOCKS = -(-(N_ASSIGN + N_TOK_TILES * N_EXPERTS * (UNIT - 1) + N_EXPERTS * (BM - 1)) // BM)
SPARE_BLOCKS = 2 * -(-LOCAL_ROWS // BM)
SPARE_UNIT0 = N_BLOCKS * UNITS_PER_BLOCK
N_BLOCKS_ALL = N_BLOCKS + SPARE_BLOCKS
N_SLOTS = N_BLOCKS_ALL * BM


def _dot(a, b):
    return jnp.dot(a, b, preferred_element_type=F32)


def _dot_nt(a, b):
    return lax.dot_general(a, b, (((1,), (1,)), ((), ())), preferred_element_type=F32)


def _silu(x):
    return x * jax.nn.sigmoid(x)


def _log_sigmoid(x):
    return jnp.minimum(x, 0.0) - jnp.log(1.0 + jnp.exp(-jnp.abs(x)))


def _cond_row(tile, tile_rows):
    n_ctx = T_CTX // tile_rows
    per_lat = DEC_SEQ // tile_rows
    return jnp.where(tile < n_ctx, 0, 1 + (tile - n_ctx) // per_lat)


def _layer_norm(x, g, b):
    mu = jnp.mean(x, -1, keepdims=True)
    xc = x - mu
    var = jnp.mean(xc * xc, -1, keepdims=True)
    return xc * lax.rsqrt(var + LN_EPS) * g + b


MOD_TN = 1536


def _mod_kernel(cond_ref, w_ref, b_ref, o_ref):
    s = _silu(cond_ref[...])
    s_hi = s.astype(BF16)
    s_lo = (s - s_hi.astype(F32)).astype(BF16)
    w = w_ref[0]
    w_hi = w.astype(BF16)
    w_lo = (w - w_hi.astype(F32)).astype(BF16)
    o_ref[0] = _dot(s_hi, w_hi) + _dot(s_lo, w_hi) + _dot(s_hi, w_lo) + b_ref[0]


def _modulation(cond, w_mod, b_mod):
    n_out = 6 * D_MODEL
    return pl.pallas_call(
        _mod_kernel,
        out_shape=jax.ShapeDtypeStruct((DEPTH, N_COND, n_out), F32),
        grid=(DEPTH, n_out // MOD_TN),
        in_specs=[
            pl.BlockSpec((N_COND, D_MODEL), lambda l, j: (0, 0)),
            pl.BlockSpec((1, D_MODEL, MOD_TN), lambda l, j: (l, 0, j)),
            pl.BlockSpec((1, 1, MOD_TN), lambda l, j: (l, 0, j)),
        ],
        out_specs=pl.BlockSpec((1, N_COND, MOD_TN), lambda l, j: (l, 0, j)),
        compiler_params=pltpu.CompilerParams(
            dimension_semantics=("arbitrary", "arbitrary"), vmem_limit_bytes=VMEM_LIMIT),
        name="modulation",
    )(cond, w_mod, b_mod.reshape(DEPTH, 1, n_out))


SEQ_PER_PROJ = TM_PROJ // SEQ


def _inproj_kernel(n_prev, xc_ref, xl_ref, mod_ref, w_ref, *refs):
    prev_refs = refs[:4] if n_prev else ()
    pa_ref, pb_ref, pc_ref, wk_ref, wv_ref, nk_ref, nv_ref = refs[len(prev_refs):]
    i = pl.program_id(0)
    is_ctx = i < T_CTX // TM_PROJ
    targets = ((wk_ref, PA_W + W_B, KV_B), (wv_ref, PA_W + W_B + KV_W_B, KV_B),
               (nk_ref, PA_W + PB_W + W_C, H_C), (nv_ref, PA_W + PB_W + 2 * W_C, H_C))

    def project(x_ref, ci, with_cache):
        sh = mod_ref[pl.ds(ci, 1), 0:D_MODEL]
        sc = mod_ref[pl.ds(ci, 1), D_MODEL:2 * D_MODEL]
        for s in range(SEQ_PER_PROJ):
            rows = slice(s * SEQ, (s + 1) * SEQ)
            h = x_ref[rows, :] * (1.0 + sc) + sh
            p = _dot(h.astype(BF16), w_ref[...])
            pa_ref[rows, :] = p[:, 0:PA_W].astype(BF16)
            pb_ref[rows, :] = p[:, PA_W:PA_W + PB_W].astype(BF16)
            pc_ref[rows, :] = p[:, PA_W + PB_W:IN_WIDTH].astype(BF16)
            if with_cache:
                for j, (ref, col0, n_heads) in enumerate(targets):
                    if n_prev:
                        ref[s, 0:n_prev] = prev_refs[j][s]
                    for hh in range(n_heads):
                        ref[s, n_prev, hh] = p[:, col0 + hh * HEAD_DIM:col0 + (hh + 1) * HEAD_DIM]

    @pl.when(is_ctx)
    def _():
        project(xc_ref, 0, True)

    @pl.when(jnp.logical_not(is_ctx))
    def _():
        project(xl_ref, _cond_row(i, TM_PROJ), False)


def _in_projection(x_ctx, x_lat, mod, w_in_bf16, earlier):
    n_ctx_tiles = T_CTX // TM_PROJ
    n_prev = earlier[0].shape[1] if earlier else 0

    def cache_spec(n_layers, n_heads):
        return pl.BlockSpec((SEQ_PER_PROJ, n_layers, n_heads, SEQ, HEAD_DIM),
                            lambda i: (jnp.minimum(i, n_ctx_tiles - 1), 0, 0, 0, 0))

    cache_heads = (KV_B, KV_B, H_C, H_C)
    return pl.pallas_call(
        functools.partial(_inproj_kernel, n_prev),
        out_shape=(
            jax.ShapeDtypeStruct((T_ALL, PA_W), BF16),
            jax.ShapeDtypeStruct((T_ALL, PB_W), BF16),
            jax.ShapeDtypeStruct((T_ALL, PC_W), BF16),
        ) + tuple(jax.ShapeDtypeStruct((BATCH, n_prev + 1, nh, SEQ, HEAD_DIM), F32) for nh in cache_heads),
        grid=(T_ALL // TM_PROJ,),
        in_specs=[
            pl.BlockSpec((TM_PROJ, D_MODEL), lambda i: (jnp.minimum(i, n_ctx_tiles - 1), 0)),
            pl.BlockSpec((TM_PROJ, D_MODEL), lambda i: (jnp.maximum(i - n_ctx_tiles, 0), 0)),
            pl.BlockSpec((N_COND, 6 * D_MODEL), lambda i: (0, 0)),
            pl.BlockSpec((D_MODEL, IN_WIDTH), lambda i: (0, 0)),
        ] + [cache_spec(n_prev, nh) for nh in cache_heads if n_prev],
        out_specs=(
            pl.BlockSpec((TM_PROJ, PA_W), lambda i: (i, 0)),
            pl.BlockSpec((TM_PROJ, PB_W), lambda i: (i, 0)),
            pl.BlockSpec((TM_PROJ, PC_W), lambda i: (i, 0)),
        ) + tuple(cache_spec(n_prev + 1, nh) for nh in cache_heads),
        compiler_params=pltpu.CompilerParams(
            dimension_semantics=("arbitrary",), vmem_limit_bytes=VMEM_LIMIT),
        name="in_projection",
    )(x_ctx, x_lat, mod, w_in_bf16, *earlier)


def _decay_matrix(lg_f, lg_b, n):
    row = lax.broadcasted_iota(I32, (n, n), 0)
    col = lax.broadcasted_iota(I32, (n, n), 1)
    diff = (row - col).astype(F32)
    fwd = jnp.where(diff >= 0, jnp.exp(lg_f * jnp.maximum(diff, 0.0)), 0.0)
    bwd = jnp.where(diff <= 0, jnp.exp(lg_b * jnp.maximum(-diff, 0.0)), 0.0)
    return (fwd + bwd) * ATTN_SCALE


def _stacked_softmax_attend(scores, values, extra_logit=None):
    rows = scores[0].shape[0]
    s = jnp.concatenate(scores, axis=0)
    m = jnp.max(s, -1, keepdims=True)
    if extra_logit is not None:
        m = jnp.maximum(m, extra_logit)
    p = jnp.exp(s - m)
    den = jnp.sum(p, -1, keepdims=True)
    if extra_logit is not None:
        den = den + jnp.exp(extra_logit - m)
    p = p.astype(BF16)
    return [_dot(p[h * rows:(h + 1) * rows], v) / den[h * rows:(h + 1) * rows]
            for h, v in enumerate(values)]


def _lane_xor_matrix(width, distance):
    r = lax.broadcasted_iota(I32, (width, width), 0)
    c = lax.broadcasted_iota(I32, (width, width), 1)
    return jnp.where((r ^ distance) == c, 1.0, 0.0).astype(BF16)


def _ctx_mixer_kernel(sink_ref, pa_ref, pb_ref, pc_ref, decf_ref, decb_ref, gng_ref, gnb_ref,
                      mix_ref, sf_ref, sb_ref, dmat_ref, zf_ref, zb_ref):
    n = SEQ
    hd = HEAD_DIM

    @pl.when(pl.program_id(0) == 0)
    def _():
        pos = lax.broadcasted_iota(I32, (n, hd), 0).astype(F32)
        zf, zb = [], []
        for h in range(H_A):
            lg_f = _log_sigmoid(decf_ref[h])
            lg_b = _log_sigmoid(decb_ref[h])
            dmat_ref[h] = _decay_matrix(lg_f, lg_b, n)
            zf.append(jnp.exp(lg_f[:, 0:hd] * (n - 1.0 - pos)) * ATTN_SCALE)
            zb.append(jnp.exp(lg_b[:, 0:hd] * pos) * ATTN_SCALE)
        for p in range(H_A // 2):
            zf_ref[p] = jnp.concatenate(zf[2 * p:2 * p + 2], axis=1)
            zb_ref[p] = jnp.concatenate(zb[2 * p:2 * p + 2], axis=1)

    low = lax.broadcasted_iota(I32, (n, LANES), 1) < hd

    def own_half(x, h):
        zero = jnp.zeros_like(x)
        return jnp.where(low, x, zero) if h % 2 == 0 else jnp.where(low, zero, x)

    def merge(first, second):
        return jnp.where(low, first, second)

    def pair_cols(ref, base, p):
        return ref[:, base + p * LANES:base + (p + 1) * LANES]

    def head_mean(x):
        first = jnp.sum(jnp.where(low, x, 0.0), -1, keepdims=True)
        second = jnp.sum(jnp.where(low, 0.0, x), -1, keepdims=True)
        return merge(first, second) * (1.0 / hd)

    for p in range(H_A // 2):
        q_pair, k_pair = pair_cols(pa_ref, 0, p), pair_cols(pa_ref, W_A, p)
        v_pair, gate_pair = pair_cols(pa_ref, 2 * W_A, p), pair_cols(pa_ref, 3 * W_A, p)
        outs = []
        for h in (2 * p, 2 * p + 1):
            a = _dot_nt(own_half(q_pair, h), k_pair)
            outs.append(_dot((a * dmat_ref[h]).astype(BF16), v_pair))
        o = merge(outs[0], outs[1])
        kf = k_pair.astype(F32)
        for st_ref, z_ref in ((sf_ref, zf_ref), (sb_ref, zb_ref)):
            st = _dot((kf * z_ref[p]).T.astype(BF16), v_pair)
            st_ref[0, 2 * p] = st[0:hd, 0:hd]
            st_ref[0, 2 * p + 1] = st[hd:2 * hd, hd:2 * hd]
        mu = head_mean(o)
        oc = o - mu
        var = head_mean(oc * oc)
        on = oc * lax.rsqrt(var + GN_EPS) * pair_cols(gng_ref, 0, p) + pair_cols(gnb_ref, 0, p)
        mix_ref[:, p * LANES:(p + 1) * LANES] = (on * _silu(gate_pair.astype(F32))).astype(BF16)

    group = H_B // KV_B
    swap = _lane_xor_matrix(LANES, hd)
    kv_k = pb_ref[:, W_B:W_B + KV_W_B]
    kv_v = pb_ref[:, W_B + KV_W_B:W_B + 2 * KV_W_B]
    scores = []
    for hh in range(H_B):
        q = own_half(pair_cols(pb_ref, 0, hh // 2), hh) * ATTN_SCALE
        if hh % 2 != hh // group:
            q = _dot(q, swap).astype(BF16)
        scores.append(_dot_nt(q, kv_k))
    sinks = jnp.concatenate([jnp.full((n, 1), sink_ref[hh], F32) for hh in range(H_B)], axis=0)
    outs = []
    stacked = []
    for g0 in range(0, H_B, CTX_STACK):
        stacked += _stacked_softmax_attend(scores[g0:g0 + CTX_STACK], [kv_v] * CTX_STACK,
                                           sinks[g0 * n:(g0 + CTX_STACK) * n])
    for hh, o in enumerate(stacked):
        o = o.astype(BF16)
        outs.append(_dot(o, swap).astype(BF16) if hh % 2 != hh // group else o)
    for p in range(H_B // 2):
        mix_ref[:, W_A + p * LANES:W_A + (p + 1) * LANES] = merge(outs[2 * p], outs[2 * p + 1])

    scores, values = [], []
    for hh in range(H_C):
        q = own_half(pair_cols(pc_ref, 0, hh // 2), hh) * ATTN_SCALE
        scores.append(_dot_nt(q, pair_cols(pc_ref, W_C, hh // 2)))
        values.append(pair_cols(pc_ref, 2 * W_C, hh // 2))
    outs = []
    for g0 in range(0, H_C, CTX_STACK):
        outs += _stacked_softmax_attend(scores[g0:g0 + CTX_STACK], values[g0:g0 + CTX_STACK])
    for p in range(H_C // 2):
        mix_ref[:, W_A + W_B + p * LANES:W_A + W_B + (p + 1) * LANES] = merge(
            outs[2 * p], outs[2 * p + 1]).astype(BF16)


def _ctx_mixers(pa, pb, pc, sink, decf, decb, gng, gnb):
    return pl.pallas_call(
        _ctx_mixer_kernel,
        out_shape=(
            jax.ShapeDtypeStruct((T_CTX, D_MODEL), BF16),
            jax.ShapeDtypeStruct((BATCH, H_A, HEAD_DIM, HEAD_DIM), F32),
            jax.ShapeDtypeStruct((BATCH, H_A, HEAD_DIM, HEAD_DIM), F32),
        ),
        grid=(BATCH,),
        in_specs=[
            pl.BlockSpec(memory_space=pltpu.SMEM),
            pl.BlockSpec((SEQ, PA_W), lambda b: (b, 0)),
            pl.BlockSpec((SEQ, PB_W), lambda b: (b, 0)),
            pl.BlockSpec((SEQ, PC_W), lambda b: (b, 0)),
            pl.BlockSpec((H_A, 1, SEQ), lambda b: (0, 0, 0)),
            pl.BlockSpec((H_A, 1, SEQ), lambda b: (0, 0, 0)),
            pl.BlockSpec((1, W_A), lambda b: (0, 0)),
            pl.BlockSpec((1, W_A), lambda b: (0, 0)),
        ],
        out_specs=(
            pl.BlockSpec((SEQ, D_MODEL), lambda b: (b, 0)),
            pl.BlockSpec((1, H_A, HEAD_DIM, HEAD_DIM), lambda b: (b, 0, 0, 0)),
            pl.BlockSpec((1, H_A, HEAD_DIM, HEAD_DIM), lambda b: (b, 0, 0, 0)),
        ),
        scratch_shapes=[
            pltpu.VMEM((H_A, SEQ, SEQ), F32),
            pltpu.VMEM((H_A // 2, SEQ, LANES), F32),
            pltpu.VMEM((H_A // 2, SEQ, LANES), F32),
        ],
        compiler_params=pltpu.CompilerParams(
            dimension_semantics=("arbitrary",), vmem_limit_bytes=VMEM_LIMIT),
        name="ctx_mixers",
    )(sink, pa, pb, pc, decf, decb, gng, gnb)


def _lat_ret_kernel(pa_ref, stf_ref, stb_ref, decf_ref, decb_ref, gng_ref, gnb_ref, ya_ref):
    c = RET_CHUNK
    hd = HEAD_DIM
    n_chunks = DEC_SEQ // c
    pos = lax.broadcasted_iota(I32, (c, LANES), 0).astype(F32)
    low = lax.broadcasted_iota(I32, (c, LANES), 1) < hd
    same_head = (lax.broadcasted_iota(I32, (LANES, LANES), 0) // hd
                 == lax.broadcasted_iota(I32, (LANES, LANES), 1) // hd)
    zero_blk = jnp.zeros((hd, hd), F32)

    def block_diag(a, b):
        return jnp.concatenate([jnp.concatenate([a, zero_blk], axis=1),
                                jnp.concatenate([zero_blk, b], axis=1)], axis=0)

    def head_mean(x):
        first = jnp.sum(jnp.where(low, x, 0.0), -1, keepdims=True)
        second = jnp.sum(jnp.where(low, 0.0, x), -1, keepdims=True)
        return jnp.where(low, first, second) * (1.0 / hd)

    for p in range(H_A // 2):
        lg_f = [_log_sigmoid(decf_ref[h]) for h in (2 * p, 2 * p + 1)]
        lg_b = [_log_sigmoid(decb_ref[h]) for h in (2 * p, 2 * p + 1)]
        dmat = [_decay_matrix(lg_f[t], lg_b[t], c) for t in range(2)]
        lf = jnp.where(low[0:1], lg_f[0][:, 0:LANES], lg_f[1][:, 0:LANES])
        lb = jnp.where(low[0:1], lg_b[0][:, 0:LANES], lg_b[1][:, 0:LANES])
        zf = jnp.exp(lf * (c - 1.0 - pos)) * ATTN_SCALE
        zb = jnp.exp(lb * pos) * ATTN_SCALE
        xf = jnp.exp(lf * (pos + 1.0))
        xb = jnp.exp(lb * (c - pos))
        gcf = jnp.exp(lf * float(c))
        gcb = jnp.exp(lb * float(c))
        pair = slice(p * LANES, (p + 1) * LANES)

        def chunk(i, base):
            return pa_ref[i * c:(i + 1) * c, base + p * LANES:base + (p + 1) * LANES]

        kv_f, kv_b = [], []
        for i in range(n_chunks):
            kf = chunk(i, W_A).astype(F32)
            v = chunk(i, 2 * W_A)
            kv_f.append(jnp.where(same_head, _dot((kf * zf).T.astype(BF16), v), 0.0))
            kv_b.append(jnp.where(same_head, _dot((kf * zb).T.astype(BF16), v), 0.0))

        s = block_diag(stf_ref[0, 0, 2 * p], stf_ref[0, 0, 2 * p + 1])
        seen_f = []
        for i in range(n_chunks):
            seen_f.append(s)
            s = gcf * s + kv_f[i]
        s = block_diag(stb_ref[0, 0, 2 * p], stb_ref[0, 0, 2 * p + 1])
        seen_b = [None] * n_chunks
        for i in reversed(range(n_chunks)):
            seen_b[i] = s
            s = gcb * s + kv_b[i]

        for i in range(n_chunks):
            q_pair, k_pair, v_pair = chunk(i, 0), chunk(i, W_A), chunk(i, 2 * W_A)
            inner = []
            for t in range(2):
                zero = jnp.zeros_like(q_pair)
                q = jnp.where(low, q_pair, zero) if t == 0 else jnp.where(low, zero, q_pair)
                inner.append(_dot((_dot_nt(q, k_pair) * dmat[t]).astype(BF16), v_pair))
            qf = q_pair.astype(F32)
            lhs = jnp.concatenate([(qf * xf).astype(BF16), (qf * xb).astype(BF16)], axis=1)
            rhs = jnp.concatenate([seen_f[i], seen_b[i]], axis=0).astype(BF16)
            o = jnp.where(low, inner[0], inner[1]) + _dot(lhs, rhs)
            mu = head_mean(o)
            oc = o - mu
            var = head_mean(oc * oc)
            on = oc * lax.rsqrt(var + GN_EPS) * gng_ref[:, pair] + gnb_ref[:, pair]
            y = on * _silu(chunk(i, 3 * W_A).astype(F32))
            ya_ref[i * c:(i + 1) * c, pair] = y.astype(BF16)


def _lat_retention(pa, st_f, st_b, layer, decf, decb, gng, gnb):
    lat0 = T_CTX // DEC_SEQ
    st_spec = pl.BlockSpec((1, 1, H_A, HEAD_DIM, HEAD_DIM), lambda b: (b, layer, 0, 0, 0))
    return pl.pallas_call(
        _lat_ret_kernel,
        out_shape=jax.ShapeDtypeStruct((T_LAT, W_A), BF16),
        grid=(DEC_BATCH,),
        in_specs=[
            pl.BlockSpec((DEC_SEQ, PA_W), lambda b: (lat0 + b, 0)),
            st_spec, st_spec,
            pl.BlockSpec((H_A, 1, RET_CHUNK), lambda b: (0, 0, 0)),
            pl.BlockSpec((H_A, 1, RET_CHUNK), lambda b: (0, 0, 0)),
            pl.BlockSpec((1, W_A), lambda b: (0, 0)),
            pl.BlockSpec((1, W_A), lambda b: (0, 0)),
        ],
        out_specs=pl.BlockSpec((DEC_SEQ, W_A), lambda b: (b, 0)),
        compiler_params=pltpu.CompilerParams(
            dimension_semantics=("arbitrary",), vmem_limit_bytes=VMEM_LIMIT),
        name="lat_retention",
    )(pa, st_f, st_b, decf, decb, gng, gnb)


def _rope(x, cos, sin_signed, swap):
    return x.astype(F32) * cos + _dot(x, swap) * sin_signed


def _lat_win_kernel(sink_ref, pq_ref, pseq_ref, kctx_ref, vctx_ref, cos_ref, sin_ref, yb_ref,
                    krope_ref, kc_ref, vc_ref, mask_ref):
    n = pl.program_id(1)
    hd = HEAD_DIM
    qb = WIN_QB
    band = WIN_BAND
    group = H_B // KV_B
    rot = _lane_xor_matrix(LANES, hd // 2)
    swap = _lane_xor_matrix(LANES, hd)

    @pl.when(n == 0)
    def _():
        k = pseq_ref[:, W_B:W_B + KV_W_B]
        krope_ref[...] = _rope(k, cos_ref[...], sin_ref[...], rot).astype(BF16)
        kc_ref[...] = jnp.concatenate([kctx_ref[0, 0, j] for j in range(KV_B)], axis=1).astype(BF16)
        vc_ref[...] = jnp.concatenate([vctx_ref[0, 0, j] for j in range(KV_B)], axis=1).astype(BF16)
        q_in_blk = lax.broadcasted_iota(I32, (group * qb, band), 0) % qb
        k_in_band = lax.broadcasted_iota(I32, (group * qb, band), 1)
        for ty, lead in enumerate((0, WINDOW, band - qb)):
            mask_ref[ty] = jnp.where(jnp.abs(k_in_band - lead - q_in_blk) <= WINDOW, 0.0, NEG_INF)

    q_rows = pl.ds(pl.multiple_of(n * qb, qb), qb)
    cos_q = cos_ref[q_rows, :]
    sin_q = sin_ref[q_rows, :]
    low = lax.broadcasted_iota(I32, (qb, LANES), 1) < hd
    q_heads = []
    for p in range(H_B // 2):
        q_pair = _rope(pq_ref[:, p * LANES:(p + 1) * LANES], cos_q, sin_q, rot) * ATTN_SCALE
        for hh in (2 * p, 2 * p + 1):
            q = jnp.where(low, q_pair, 0.0) if hh % 2 == 0 else jnp.where(low, 0.0, q_pair)
            q = q.astype(BF16)
            q_heads.append(_dot(q, swap).astype(BF16) if hh % 2 != hh // group else q)

    ws = jnp.clip(n * qb - WINDOW, 0, DEC_SEQ - band)
    k_rows = pl.ds(pl.multiple_of(ws, WINDOW), band)
    n_blk = DEC_SEQ // qb
    band_bias = mask_ref[jnp.where(n == 0, 0, jnp.where(n == n_blk - 1, 2, 1))]
    head_of_row = lax.broadcasted_iota(I32, (group * qb, 1), 0) // qb
    kw = krope_ref[k_rows, :]
    vw = pseq_ref[k_rows, W_B + KV_W_B:W_B + 2 * KV_W_B]
    outs = []
    for j in range(KV_B):
        heads = [j * group + g for g in range(group)]
        qs = jnp.concatenate([q_heads[hh] for hh in heads], axis=0)
        s_loc = _dot_nt(qs, kw) + band_bias
        s_ctx = _dot_nt(qs, kc_ref[...])
        sink = jnp.zeros((group * qb, 1), F32)
        for g, hh in enumerate(heads):
            sink = jnp.where(head_of_row == g, sink_ref[hh], sink)
        m = jnp.maximum(jnp.maximum(jnp.max(s_loc, -1, keepdims=True),
                                    jnp.max(s_ctx, -1, keepdims=True)), sink)
        p_loc = jnp.exp(s_loc - m)
        p_ctx = jnp.exp(s_ctx - m)
        den = (jnp.sum(p_loc, -1, keepdims=True) + jnp.sum(p_ctx, -1, keepdims=True)
               + jnp.exp(sink - m))
        o = ((_dot(p_loc.astype(BF16), vw) + _dot(p_ctx.astype(BF16), vc_ref[...])) / den).astype(BF16)
        for g, hh in enumerate(heads):
            o_h = o[g * qb:(g + 1) * qb]
            outs.append(_dot(o_h, swap).astype(BF16) if hh % 2 != j else o_h)
    for p in range(H_B // 2):
        yb_ref[:, p * LANES:(p + 1) * LANES] = jnp.where(low, outs[2 * p], outs[2 * p + 1])


def _lat_window_attn(pb, cache_k, cache_v, layer, sink, cos_t, sin_t):
    n_blk = DEC_SEQ // WIN_QB
    lat_blk0 = T_CTX // WIN_QB
    lat_seq0 = T_CTX // DEC_SEQ
    ctx_spec = pl.BlockSpec((1, 1, KV_B, PAST_LEN, HEAD_DIM), lambda b, n: (b, layer, 0, 0, 0))
    return pl.pallas_call(
        _lat_win_kernel,
        out_shape=jax.ShapeDtypeStruct((T_LAT, W_B), BF16),
        grid=(DEC_BATCH, n_blk),
        in_specs=[
            pl.BlockSpec(memory_space=pltpu.SMEM),
            pl.BlockSpec((WIN_QB, PB_W), lambda b, n: (lat_blk0 + b * n_blk + n, 0)),
            pl.BlockSpec((DEC_SEQ, PB_W), lambda b, n: (lat_seq0 + b, 0)),
            ctx_spec, ctx_spec,
            pl.BlockSpec((DEC_SEQ, LANES), lambda b, n: (0, 0)),
            pl.BlockSpec((DEC_SEQ, LANES), lambda b, n: (0, 0)),
        ],
        out_specs=pl.BlockSpec((WIN_QB, W_B), lambda b, n: (b * n_blk + n, 0)),
        scratch_shapes=[
            pltpu.VMEM((DEC_SEQ, KV_W_B), BF16),
            pltpu.VMEM((PAST_LEN, KV_W_B), BF16),
            pltpu.VMEM((PAST_LEN, KV_W_B), BF16),
            pltpu.VMEM((3, (H_B // KV_B) * WIN_QB, WIN_BAND), F32),
        ],
        compiler_params=pltpu.CompilerParams(
            dimension_semantics=("arbitrary", "arbitrary"), vmem_limit_bytes=VMEM_LIMIT),
        name="lat_window_attn",
    )(sink, pb, pb, cache_k, cache_v, cos_t, sin_t)


NA_Q = NA_TILE_ROWS * GRID_W
NA_K = NA_KEY_ROWS * GRID_W
NA_TILES = DEC_SEQ // NA_Q
LAT_ROWS = DEC_SEQ // GRID_W


def _na_window_start(tile):
    return jnp.clip(tile * NA_TILE_ROWS - NA_ROWS // 2, 0, LAT_ROWS - NA_KEY_ROWS)


def _lat_na_kernel(pq_ref, pseq_ref, kctx_ref, vctx_ref, bias_ref, yc_ref, kc_ref, vc_ref):
    t = pl.program_id(1)
    hd = HEAD_DIM
    nq = NA_Q

    @pl.when(t == 0)
    def _():
        for p in range(H_C // 2):
            kc_ref[p] = jnp.concatenate([kctx_ref[0, 0, 2 * p], kctx_ref[0, 0, 2 * p + 1]], axis=1).astype(BF16)
            vc_ref[p] = jnp.concatenate([vctx_ref[0, 0, 2 * p], vctx_ref[0, 0, 2 * p + 1]], axis=1).astype(BF16)

    k_rows = pl.ds(pl.multiple_of(_na_window_start(t) * GRID_W, GRID_W), NA_K)
    low = lax.broadcasted_iota(I32, (nq, LANES), 1) < hd
    loc, ctx = [], []
    for hh in range(H_C):
        p = hh // 2
        q_pair = pq_ref[:, p * LANES:(p + 1) * LANES] * ATTN_SCALE
        zero = jnp.zeros_like(q_pair)
        q = jnp.where(low, q_pair, zero) if hh % 2 == 0 else jnp.where(low, zero, q_pair)
        loc.append(_dot_nt(q, pseq_ref[k_rows, W_C + p * LANES:W_C + (p + 1) * LANES]) + bias_ref[0, hh])
        ctx.append(_dot_nt(q, kc_ref[p]))
    outs = []
    for g0 in range(0, H_C, NA_STACK):
        s_loc = jnp.concatenate(loc[g0:g0 + NA_STACK], axis=0)
        s_ctx = jnp.concatenate(ctx[g0:g0 + NA_STACK], axis=0)
        m = jnp.maximum(jnp.max(s_loc, -1, keepdims=True), jnp.max(s_ctx, -1, keepdims=True))
        p_loc = jnp.exp(s_loc - m)
        p_ctx = jnp.exp(s_ctx - m)
        den = jnp.sum(p_loc, -1, keepdims=True) + jnp.sum(p_ctx, -1, keepdims=True)
        p_loc = p_loc.astype(BF16)
        p_ctx = p_ctx.astype(BF16)
        for hh in range(g0, g0 + NA_STACK):
            p = hh // 2
            rows = slice((hh - g0) * nq, (hh - g0 + 1) * nq)
            vw = pseq_ref[k_rows, 2 * W_C + p * LANES:2 * W_C + (p + 1) * LANES]
            outs.append((_dot(p_loc[rows], vw) + _dot(p_ctx[rows], vc_ref[p])) / den[rows])
    for p in range(H_C // 2):
        yc_ref[:, p * LANES:(p + 1) * LANES] = jnp.where(low, outs[2 * p], outs[2 * p + 1]).astype(BF16)


def _na_tile_type(t):
    return jnp.where(t == 0, 0, jnp.where(t == NA_TILES - 1, 2, 1))


def _lat_na_attn(pc, cache_k, cache_v, layer, maskbias):
    lat_tile0 = T_CTX // NA_Q
    lat_seq0 = T_CTX // DEC_SEQ
    ctx_spec = pl.BlockSpec((1, 1, H_C, PAST_LEN, HEAD_DIM), lambda b, t: (b, layer, 0, 0, 0))
    return pl.pallas_call(
        _lat_na_kernel,
        out_shape=jax.ShapeDtypeStruct((T_LAT, W_C), BF16),
        grid=(DEC_BATCH, NA_TILES),
        in_specs=[
            pl.BlockSpec((NA_Q, PC_W), lambda b, t: (lat_tile0 + b * NA_TILES + t, 0)),
            pl.BlockSpec((DEC_SEQ, PC_W), lambda b, t: (lat_seq0 + b, 0)),
            ctx_spec, ctx_spec,
            pl.BlockSpec((1, H_C, NA_Q, NA_K), lambda b, t: (_na_tile_type(t), 0, 0, 0)),
        ],
        out_specs=pl.BlockSpec((NA_Q, W_C), lambda b, t: (b * NA_TILES + t, 0)),
        scratch_shapes=[
            pltpu.VMEM((H_C // 2, PAST_LEN, LANES), BF16),
            pltpu.VMEM((H_C // 2, PAST_LEN, LANES), BF16),
        ],
        compiler_params=pltpu.CompilerParams(
            dimension_semantics=("arbitrary", "arbitrary"), vmem_limit_bytes=VMEM_LIMIT),
        name="lat_na_attn",
    )(pc, pc, cache_k, cache_v, maskbias)


def _na_block_index():
    out = np.zeros((3, NA_TILE_ROWS, NA_KEY_ROWS), np.int32)
    for ty, tile in enumerate((0, 1, NA_TILES - 1)):
        r = tile * NA_TILE_ROWS
        ws = int(np.clip(r - NA_ROWS // 2, 0, LAT_ROWS - NA_KEY_ROWS))
        for qq in range(NA_TILE_ROWS):
            qr = r + qq
            r0 = int(np.clip(qr - NA_ROWS // 2, 0, LAT_ROWS - NA_ROWS))
            for kk in range(NA_KEY_ROWS):
                kr = ws + kk
                out[ty, qq, kk] = kr - qr + NA_ROWS - 1 if r0 <= kr < r0 + NA_ROWS else 2 * NA_ROWS - 1
    return out


def _na_maskbias(rpb):
    qc = np.arange(GRID_W)[:, None]
    kc = np.arange(GRID_W)[None, :]
    c0 = np.clip(qc - NA_COLS // 2, 0, GRID_W - NA_COLS)
    col_ok = (kc >= c0) & (kc < c0 + NA_COLS)
    ci = np.clip(kc - qc + NA_COLS - 1, 0, 2 * NA_COLS - 2)
    onehot = (ci[None] == np.arange(2 * NA_COLS - 1)[:, None, None]).astype(np.float32)
    cols = jnp.einsum("hab,bqk->haqk", rpb, jnp.asarray(onehot), precision=lax.Precision.HIGHEST)
    cols = jnp.where(jnp.asarray(col_ok)[None, None], cols, NEG_INF)
    cols = jnp.concatenate([cols, jnp.full((H_C, 1, GRID_W, GRID_W), NEG_INF, F32)], axis=1)
    block_index = _na_block_index()

    def assemble(cols_ref, out_ref):
        for ty in range(3):
            for qq in range(NA_TILE_ROWS):
                for kk in range(NA_KEY_ROWS):
                    out_ref[ty, 0, qq * GRID_W:(qq + 1) * GRID_W, kk * GRID_W:(kk + 1) * GRID_W] = (
                        cols_ref[0, int(block_index[ty, qq, kk])])

    return pl.pallas_call(
        assemble,
        out_shape=jax.ShapeDtypeStruct((3, H_C, NA_Q, NA_K), F32),
        grid=(H_C,),
        in_specs=[pl.BlockSpec((1, 2 * NA_ROWS, GRID_W, GRID_W), lambda h: (h, 0, 0, 0))],
        out_specs=pl.BlockSpec((3, 1, NA_Q, NA_K), lambda h: (0, h, 0, 0)),
        compiler_params=pltpu.CompilerParams(dimension_semantics=("arbitrary",)),
        name="na_bias_assemble",
    )(cols)


def _rope_tables():
    t = np.arange(DEC_SEQ)
    n_freq = HEAD_DIM // 4
    inv = (ROPE_BASE ** (-np.arange(n_freq, dtype=np.float32) / n_freq)).astype(np.float32)
    row = (t // GRID_W).astype(np.float32)[:, None] * inv
    col = (t % GRID_W).astype(np.float32)[:, None] * inv
    ang = np.concatenate([row, col], -1)
    cos, sin = np.cos(ang), np.sin(ang)
    cos_h = np.concatenate([cos, cos], -1)
    sin_h = np.concatenate([-sin, sin], -1)
    reps = LANES // HEAD_DIM
    return (jnp.asarray(np.tile(cos_h, (1, reps)), F32), jnp.asarray(np.tile(sin_h, (1, reps)), F32))


def _first_index_of(mask, iota, sentinel):
    return jnp.min(jnp.where(mask, iota, sentinel), axis=0, keepdims=True)


def _route(logits, b_col):
    n = logits.shape[1]
    scores = jax.nn.sigmoid(logits)
    sel = scores + b_col
    io_g = lax.broadcasted_iota(I32, (GROUP_SIZE, n), 0)
    gs_rows = []
    for g in range(N_GROUPS):
        s = sel[g * GROUP_SIZE:(g + 1) * GROUP_SIZE]
        m1 = jnp.max(s, axis=0, keepdims=True)
        i1 = _first_index_of(s == m1, io_g, GROUP_SIZE)
        m2 = jnp.max(jnp.where(io_g == i1, PICKED, s), axis=0, keepdims=True)
        gs_rows.append(m1 + m2)
    gs = jnp.concatenate(gs_rows, axis=0)
    io_n = lax.broadcasted_iota(I32, (N_GROUPS, n), 0)
    gsel = jnp.zeros((N_GROUPS, n), F32)
    for _ in range(TOPK_GROUPS):
        mg = jnp.max(gs, axis=0, keepdims=True)
        gi = _first_index_of(gs == mg, io_n, N_GROUPS)
        hit = io_n == gi
        gsel = jnp.where(hit, 1.0, gsel)
        gs = jnp.where(hit, PICKED, gs)
    cand = jnp.concatenate(
        [jnp.where(gsel[g:g + 1] > 0.5, sel[g * GROUP_SIZE:(g + 1) * GROUP_SIZE], NEG_INF)
         for g in range(N_GROUPS)], axis=0)
    io_e = lax.broadcasted_iota(I32, (N_EXPERTS, n), 0)
    picks, raw = [], []
    for _ in range(TOP_K):
        mv = jnp.max(cand, axis=0, keepdims=True)
        ei = _first_index_of(cand == mv, io_e, N_EXPERTS)
        hit = io_e == ei
        picks.append((hit, ei))
        raw.append(jnp.sum(jnp.where(hit, scores, 0.0), axis=0, keepdims=True))
        cand = jnp.where(hit, PICKED, cand)
    return picks, raw


def _post_mixer_kernel(xc_ref, xl_ref, mixc_ref, ya_ref, yb_ref, yc_ref, wout_ref, mod_ref, g_ref, b_ref,
                       wr_ref, br_ref,
                       x1_ref, h2_ref, eidx_ref, wsel_ref, rank_ref, cnt_ref):
    step = pl.program_id(0)
    tm = TM_TOK
    is_ctx = step < T_CTX // (POST_TILES * tm)

    @pl.when(step == 0)
    def _():
        cnt_ref[...] = jnp.zeros_like(cnt_ref)

    def project(s):
        rows = slice(s * tm, (s + 1) * tm)
        ci = _cond_row(step * POST_TILES + s, tm)
        gate1 = mod_ref[pl.ds(ci, 1), 2 * D_MODEL:3 * D_MODEL]
        sh2 = mod_ref[pl.ds(ci, 1), 3 * D_MODEL:4 * D_MODEL]
        sc2 = mod_ref[pl.ds(ci, 1), 4 * D_MODEL:5 * D_MODEL]
        mix_lat = jnp.concatenate([ya_ref[rows, :], yb_ref[rows, :], yc_ref[rows, :]], axis=-1)
        mix = jnp.where(is_ctx, mixc_ref[rows, :], mix_lat)
        y = _dot(mix, wout_ref[...])
        x = jnp.where(is_ctx, xc_ref[rows, :], xl_ref[rows, :])
        x1 = _layer_norm(ALPHA * x + gate1 * y, g_ref[...], b_ref[...])
        x1_ref[rows, :] = x1
        h2 = x1 * (1.0 + sc2) + sh2
        h_hi = h2.astype(BF16)
        h2_ref[rows, :] = h_hi
        h_lo = (h2 - h_hi.astype(F32)).astype(BF16)
        both = (_dot(h_hi, wr_ref[...]) + _dot(h_lo, wr_ref[...])).T
        return both[0:N_EXPERTS] + both[N_EXPERTS:2 * N_EXPERTS]

    def route(s, logits):
        toks = s * tm
        routed = [_route(logits[:, g * LANES:(g + 1) * LANES], br_ref[...]) for g in range(tm // LANES)]
        multi_g = []
        for picks, _ in routed:
            m = jnp.zeros((N_EXPERTS, LANES), F32)
            for hit, _ in picks:
                m = m + jnp.where(hit, 1.0, 0.0)
            multi_g.append(m)
        multi = jnp.concatenate(multi_g, axis=1)
        before = (lax.broadcasted_iota(I32, (tm, tm), 0) < lax.broadcasted_iota(I32, (tm, tm), 1))
        cum = _dot(multi.astype(BF16), jnp.where(before, 1.0, 0.0).astype(BF16))
        pad = jnp.zeros((SUBLANES - TOP_K, LANES), F32)
        for g, (picks, raw) in enumerate(routed):
            lanes = slice(toks + g * LANES, toks + (g + 1) * LANES)
            total = raw[0]
            for r in raw[1:]:
                total = total + r
            scale = ROUTE_SCALE / total
            cum_g = cum[:, g * LANES:(g + 1) * LANES]
            eidx_ref[:, lanes] = jnp.concatenate([ei for _, ei in picks] + [pad.astype(I32)], axis=0)
            wsel_ref[:, lanes] = jnp.concatenate([r * scale for r in raw] + [pad], axis=0)
            rank_ref[:, lanes] = jnp.concatenate(
                [jnp.sum(jnp.where(hit, cum_g, 0.0), axis=0, keepdims=True) for hit, _ in picks] + [pad],
                axis=0).astype(I32)
        tile_lane = lax.broadcasted_iota(I32, (N_EXPERTS, LANES), 1)
        cnt_ref[...] = jnp.where(tile_lane == step * POST_TILES + s,
                                 jnp.sum(multi, axis=1, keepdims=True), cnt_ref[...])

    logits = [project(s) for s in range(POST_TILES)]
    for s in range(POST_TILES):
        route(s, logits[s])


def _post_mixer(x_ctx, x_lat, mix_c, ya, yb, yc, w_out_bf16, mod, ln_g, ln_b, wr_split, b_router_col):
    tm = POST_TILES * TM_TOK
    n_ctx = T_CTX // tm
    ctx_map = lambda i: (jnp.minimum(i, n_ctx - 1), 0)
    lat_map = lambda i: (jnp.maximum(i - n_ctx, 0), 0)
    row_map = lambda i: (i, 0)
    const = lambda i: (0, 0)
    tok_map = lambda i: (0, i)
    return pl.pallas_call(
        _post_mixer_kernel,
        out_shape=(
            jax.ShapeDtypeStruct((T_ALL, D_MODEL), F32),
            jax.ShapeDtypeStruct((T_ALL, D_MODEL), BF16),
            jax.ShapeDtypeStruct((SUBLANES, T_ALL), I32),
            jax.ShapeDtypeStruct((SUBLANES, T_ALL), F32),
            jax.ShapeDtypeStruct((SUBLANES, T_ALL), I32),
            jax.ShapeDtypeStruct((N_EXPERTS, LANES), F32),
        ),
        grid=(T_ALL // tm,),
        in_specs=[
            pl.BlockSpec((tm, D_MODEL), ctx_map),
            pl.BlockSpec((tm, D_MODEL), lat_map),
            pl.BlockSpec((tm, D_MODEL), ctx_map),
            pl.BlockSpec((tm, W_A), lat_map),
            pl.BlockSpec((tm, W_B), lat_map),
            pl.BlockSpec((tm, W_C), lat_map),
            pl.BlockSpec((D_MODEL, D_MODEL), const),
            pl.BlockSpec((N_COND, 6 * D_MODEL), const),
            pl.BlockSpec((1, D_MODEL), const),
            pl.BlockSpec((1, D_MODEL), const),
            pl.BlockSpec((D_MODEL, 2 * N_EXPERTS), const),
            pl.BlockSpec((N_EXPERTS, 1), const),
        ],
        out_specs=(
            pl.BlockSpec((tm, D_MODEL), row_map),
            pl.BlockSpec((tm, D_MODEL), row_map),
            pl.BlockSpec((SUBLANES, tm), tok_map),
            pl.BlockSpec((SUBLANES, tm), tok_map),
            pl.BlockSpec((SUBLANES, tm), tok_map),
            pl.BlockSpec((N_EXPERTS, LANES), const),
        ),
        compiler_params=pltpu.CompilerParams(
            dimension_semantics=("arbitrary",), vmem_limit_bytes=VMEM_LIMIT),
        name="post_mixer",
    )(x_ctx, x_lat, mix_c, ya, yb, yc, w_out_bf16, mod, ln_g, ln_b, wr_split, b_router_col)


PLAN_TILES = 4


def _plan_kernel(eidx_ref, rank_ref, nmat_ref, lslot_ref, unit_ref, gend_ref):
    step = pl.program_id(0)
    units = jnp.floor((nmat_ref[...] + (UNIT - 1.0)) * (1.0 / UNIT))
    units_bf = units.astype(BF16)
    earlier_e = (lax.broadcasted_iota(I32, (N_EXPERTS, N_EXPERTS), 1)
                 < lax.broadcasted_iota(I32, (N_EXPERTS, N_EXPERTS), 0))
    tri_e = jnp.where(earlier_e, 1.0, 0.0).astype(BF16)
    earlier_t = (lax.broadcasted_iota(I32, (LANES, LANES), 0) < lax.broadcasted_iota(I32, (LANES, LANES), 1))
    tri_t = jnp.where(earlier_t, 1.0, 0.0).astype(BF16)
    local_off = _dot(tri_e, units_bf)
    tile_off = _dot(units_bf, tri_t)
    per_expert = jnp.sum(units, axis=1, keepdims=True)
    blocks = jnp.floor((per_expert + (UNITS_PER_BLOCK - 1.0)) * (1.0 / UNITS_PER_BLOCK))
    blocks_l = jnp.broadcast_to(blocks, (N_EXPERTS, LANES))
    start_blk = _dot(tri_e, blocks_l.astype(BF16))
    end_blk = start_blk + blocks_l
    gend_ref[...] = (end_blk * BM).astype(I32)

    tile_lane = lax.broadcasted_iota(I32, (N_EXPERTS, LANES), 1)
    u = lax.broadcasted_iota(I32, (N_EXPERTS, MAX_UNITS), 1).astype(F32)
    io_e = lax.broadcasted_iota(I32, (N_EXPERTS, TM_TOK), 0)
    for s in range(PLAN_TILES):
        i = step * PLAN_TILES + s
        this_tile = tile_lane == i

        def column(a):
            return jnp.sum(jnp.where(this_tile, a, 0.0), axis=1, keepdims=True)

        lo, n_u = column(local_off), column(units)
        base_unit = start_blk[:, 0:1] * UNITS_PER_BLOCK + column(tile_off) - lo
        inside = jnp.where(u >= lo, jnp.where(u < lo + n_u, 1.0, 0.0), 0.0)
        dst_unit = jnp.sum(inside * (base_unit + u), axis=0, keepdims=True)
        used = jnp.sum(inside, axis=0, keepdims=True) > 0.5
        spare = (SPARE_UNIT0 + (i % 2) * MAX_UNITS).astype(F32) + u[0:1, :]
        unit_ref[s] = jnp.where(used, dst_unit, spare).astype(I32)

        toks = slice(s * TM_TOK, (s + 1) * TM_TOK)
        rows = []
        for k in range(TOP_K):
            hit = io_e == eidx_ref[k:k + 1, toks]
            seg = jnp.sum(jnp.where(hit, lo * UNIT, 0.0), axis=0, keepdims=True)
            rows.append(seg.astype(I32) + rank_ref[k:k + 1, toks])
        rows.append(jnp.full((SUBLANES - TOP_K, TM_TOK), -1, I32))
        lslot_ref[:, toks] = jnp.concatenate(rows, axis=0)


def _slot_plan(eidx, rank, nmat):
    tok_map = lambda i: (0, i)
    const = lambda i: (0, 0)
    lslot, unit_tab, gend = pl.pallas_call(
        _plan_kernel,
        out_shape=(
            jax.ShapeDtypeStruct((SUBLANES, T_ALL), I32),
            jax.ShapeDtypeStruct((N_TOK_TILES, 1, MAX_UNITS), I32),
            jax.ShapeDtypeStruct((N_EXPERTS, LANES), I32),
        ),
        grid=(N_TOK_TILES // PLAN_TILES,),
        in_specs=[
            pl.BlockSpec((SUBLANES, PLAN_TILES * TM_TOK), tok_map),
            pl.BlockSpec((SUBLANES, PLAN_TILES * TM_TOK), tok_map),
            pl.BlockSpec((N_EXPERTS, LANES), const),
        ],
        out_specs=(
            pl.BlockSpec((SUBLANES, PLAN_TILES * TM_TOK), tok_map),
            pl.BlockSpec((PLAN_TILES, 1, MAX_UNITS), lambda i: (i, 0, 0)),
            pl.BlockSpec((N_EXPERTS, LANES), const),
        ),
        compiler_params=pltpu.CompilerParams(dimension_semantics=("arbitrary",)),
        name="slot_plan",
    )(eidx, rank, nmat)
    return lslot, unit_tab.reshape(N_TOK_TILES * MAX_UNITS), gend[:, 0]


PACK_W = D_MODEL // 2
HI_HALF = -65536


def _pack_pairs(x):
    lo = lax.bitcast_convert_type(x[:, 0:PACK_W], I32)
    hi = lax.bitcast_convert_type(x[:, PACK_W:D_MODEL], I32)
    return lax.shift_right_logical(lo, 16) | (hi & HI_HALF)


def _unpack_pairs(u):
    lo = lax.bitcast_convert_type(lax.shift_left(u, 16), F32).astype(BF16)
    hi = lax.bitcast_convert_type(u & HI_HALF, F32).astype(BF16)
    return lo, hi


def _unit_rows(unit):
    row = unit * UNIT
    return pl.ds(row if isinstance(unit, int) else pl.multiple_of(row, UNIT), UNIT)


def _unit_copy(src, src_unit, dst, dst_unit, sem):
    return pltpu.make_async_copy(src.at[_unit_rows(src_unit)], dst.at[_unit_rows(dst_unit)], sem)


def _dispatch_kernel(gend_ref, tab_ref, h2_ref, lslot_ref, xs_hbm, zero_ref, local_ref, sem_zero, sem_rows):
    i = pl.program_id(0)
    buf = i % 2

    def drain(b):
        pltpu.make_async_copy(local_ref.at[b], xs_hbm.at[pl.ds(0, LOCAL_ROWS)], sem_rows.at[b]).wait()

    def has_rows(e):
        return gend_ref[e] > jnp.where(e == 0, 0, gend_ref[jnp.maximum(e - 1, 0)])

    def zero_copy(e):
        return pltpu.make_async_copy(
            zero_ref, xs_hbm.at[pl.ds(pl.multiple_of(gend_ref[e] - BM, BM), BM)], sem_zero)

    @pl.when(i == 0)
    def _():
        zero_ref[...] = jnp.zeros_like(zero_ref)

        def start(e, c):
            @pl.when(has_rows(e))
            def _():
                zero_copy(e).start()
            return c

        def wait(e, c):
            @pl.when(has_rows(e))
            def _():
                zero_copy(e).wait()
            return c

        def tail_copy(blk):
            return pltpu.make_async_copy(
                zero_ref, xs_hbm.at[pl.ds(pl.multiple_of(blk * BM, BM), BM)], sem_zero)

        def start_tail(blk, c):
            tail_copy(blk).start()
            return c

        def wait_tail(blk, c):
            tail_copy(blk).wait()
            return c

        n_used = gend_ref[N_EXPERTS - 1] // BM
        lax.fori_loop(0, N_EXPERTS, start, 0)
        lax.fori_loop(n_used, N_BLOCKS_ALL, start_tail, 0)
        lax.fori_loop(0, N_EXPERTS, wait, 0)
        lax.fori_loop(n_used, N_BLOCKS_ALL, wait_tail, 0)

    @pl.when(i >= 2)
    def _():
        drain(buf)

    h2 = h2_ref[...]
    units_per_chunk = PERM_CHUNK // UNIT
    local = local_ref.at[buf]
    for c in range(LOCAL_ROWS // PERM_CHUNK):
        slot = c * PERM_CHUNK + lax.broadcasted_iota(I32, (PERM_CHUNK, TM_TOK), 0)
        p = jnp.zeros((PERM_CHUNK, TM_TOK), F32)
        for k in range(TOP_K):
            p = jnp.where(slot == lslot_ref[k:k + 1, :], 1.0, p)
        local[c * PERM_CHUNK:(c + 1) * PERM_CHUNK, :] = _pack_pairs(_dot(p.astype(BF16), h2))
        for u in range(c * units_per_chunk, (c + 1) * units_per_chunk):
            _unit_copy(local, u, xs_hbm, tab_ref[i * MAX_UNITS + u], sem_rows.at[buf]).start()

    @pl.when(i == N_TOK_TILES - 1)
    def _():
        drain(1 - buf)
        drain(buf)


def _dispatch(h2, lslot, unit_tab, gend):
    return pl.pallas_call(
        _dispatch_kernel,
        out_shape=jax.ShapeDtypeStruct((N_SLOTS, PACK_W), I32),
        grid_spec=pltpu.PrefetchScalarGridSpec(
            num_scalar_prefetch=2,
            grid=(N_TOK_TILES,),
            in_specs=[
                pl.BlockSpec((TM_TOK, D_MODEL), lambda i, ge, tab: (i, 0)),
                pl.BlockSpec((SUBLANES, TM_TOK), lambda i, ge, tab: (0, i)),
            ],
            out_specs=pl.BlockSpec(memory_space=pl.ANY),
            scratch_shapes=[
                pltpu.VMEM((BM, PACK_W), I32),
                pltpu.VMEM((2, LOCAL_ROWS, PACK_W), I32),
                pltpu.SemaphoreType.DMA,
                pltpu.SemaphoreType.DMA((2,)),
            ],
        ),
        compiler_params=pltpu.CompilerParams(
            dimension_semantics=("arbitrary",), vmem_limit_bytes=VMEM_LIMIT),
        name="moe_dispatch",
    )(gend, unit_tab, h2, lslot)


def _expert_kernel(gend_ref, xs_hbm, wgu_ref, wdown_ref, ys_hbm, xbuf, ybuf, wgu_bf, wdown_bf, sem_in, sem_out):
    e = pl.program_id(0)
    first = jnp.where(e == 0, 0, gend_ref[jnp.maximum(e - 1, 0)]) // BM
    last = gend_ref[e] // BM
    n_used = gend_ref[N_EXPERTS - 1] // BM

    def rows_of(blk):
        return pl.ds(pl.multiple_of(blk * BM, BM), BM)

    def in_copy(blk):
        return pltpu.make_async_copy(xs_hbm.at[rows_of(blk)], xbuf.at[blk % X_BUFS], sem_in.at[blk % X_BUFS])

    def out_copy(blk):
        return pltpu.make_async_copy(ybuf.at[blk % 2], ys_hbm.at[rows_of(blk)], sem_out.at[blk % 2])

    @pl.when(e == 0)
    def _():
        for a in range(X_AHEAD):
            @pl.when(a < n_used)
            def _(a=a):
                in_copy(a).start()

    @pl.when(last > first)
    def _():
        wgu_bf[...] = wgu_ref[0, 0].astype(BF16)
        wdown_bf[...] = wdown_ref[0, 0].astype(BF16)

    def block(blk, carry):
        @pl.when(blk + X_AHEAD < n_used)
        def _():
            in_copy(blk + X_AHEAD).start()

        in_copy(blk).wait()

        @pl.when(blk >= 2)
        def _():
            out_copy(blk - 2).wait()

        x_lo, x_hi = _unpack_pairs(xbuf[blk % X_BUFS])
        gu = _dot(x_lo, wgu_bf[0:PACK_W, :]) + _dot(x_hi, wgu_bf[PACK_W:D_MODEL, :])
        act = _silu(gu[:, 0:D_EXPERT]) * gu[:, D_EXPERT:2 * D_EXPERT]
        y = _dot(act.astype(BF16), wdown_bf[...])
        ybuf[blk % 2] = _pack_pairs(y.astype(BF16).astype(F32))
        out_copy(blk).start()
        return carry

    lax.fori_loop(first, last, block, 0)

    @pl.when(e == N_EXPERTS - 1)
    def _():
        @pl.when(n_used >= 2)
        def _():
            out_copy(n_used - 2).wait()

        @pl.when(n_used >= 1)
        def _():
            out_copy(n_used - 1).wait()

        ybuf[0] = jnp.zeros((BM, PACK_W), I32)

        def tail_copy(blk):
            return pltpu.make_async_copy(ybuf.at[0], ys_hbm.at[rows_of(blk)], sem_out.at[0])

        def start(blk, c):
            tail_copy(blk).start()
            return c

        def wait(blk, c):
            tail_copy(blk).wait()
            return c

        lax.fori_loop(n_used, N_BLOCKS_ALL, start, 0)
        lax.fori_loop(n_used, N_BLOCKS_ALL, wait, 0)


def _experts(xs, w_gu, w_down, layer, gend):
    return pl.pallas_call(
        _expert_kernel,
        out_shape=jax.ShapeDtypeStruct((N_SLOTS, PACK_W), I32),
        grid_spec=pltpu.PrefetchScalarGridSpec(
            num_scalar_prefetch=1,
            grid=(N_EXPERTS,),
            in_specs=[
                pl.BlockSpec(memory_space=pl.ANY),
                pl.BlockSpec((1, 1, D_MODEL, 2 * D_EXPERT), lambda e, ge: (layer, e, 0, 0)),
                pl.BlockSpec((1, 1, D_EXPERT, D_MODEL), lambda e, ge: (layer, e, 0, 0)),
            ],
            out_specs=pl.BlockSpec(memory_space=pl.ANY),
            scratch_shapes=[
                pltpu.VMEM((X_BUFS, BM, PACK_W), I32),
                pltpu.VMEM((2, BM, PACK_W), I32),
                pltpu.VMEM((D_MODEL, 2 * D_EXPERT), BF16),
                pltpu.VMEM((D_EXPERT, D_MODEL), BF16),
                pltpu.SemaphoreType.DMA((X_BUFS,)),
                pltpu.SemaphoreType.DMA((2,)),
            ],
        ),
        compiler_params=pltpu.CompilerParams(
            dimension_semantics=("arbitrary",), vmem_limit_bytes=VMEM_LIMIT),
        name="moe_experts",
    )(gend, xs, w_gu, w_down)


def _combine_kernel(tab_ref, x1_ref, h2_ref, lslot_ref, wsel_ref, ys_hbm, wsgu_ref, wsdown_ref, mod_ref,
                    g_ref, b_ref, outc_ref, outl_ref, local_ref, sel_ref, ylo_ref, yhi_ref, sem_rows):
    i = pl.program_id(0)
    tm = TM_TOK
    buf = i % 2

    def fetch_unit(tile, b, u):
        _unit_copy(ys_hbm, tab_ref[tile * MAX_UNITS + u], local_ref.at[b], u, sem_rows.at[b]).start()

    def drain(b):
        pltpu.make_async_copy(ys_hbm.at[pl.ds(0, LOCAL_ROWS)], local_ref.at[b], sem_rows.at[b]).wait()

    @pl.when(i == 0)
    def _():
        def body(u, c):
            fetch_unit(0, 0, u)
            return c

        lax.fori_loop(0, MAX_UNITS, body, 0, unroll=8)

    sgu = _dot(h2_ref[...], wsgu_ref[...])
    act = _silu(sgu[:, 0:D_SHARED]) * sgu[:, D_SHARED:2 * D_SHARED]
    f = _dot(act.astype(BF16), wsdown_ref[...])

    nxt = jnp.minimum(i + 1, N_TOK_TILES - 1)
    n_groups = tm // SEL_ROWS
    units_per_group = MAX_UNITS // n_groups
    slot = lax.broadcasted_iota(I32, (SEL_ROWS, LOCAL_ROWS), 1)
    for g in range(n_groups):
        rows = slice(g * SEL_ROWS, (g + 1) * SEL_ROWS)
        sel = jnp.zeros((SEL_ROWS, LOCAL_ROWS), F32)
        for k in range(TOP_K):
            sel = jnp.where(slot == lslot_ref[rows, k:k + 1], wsel_ref[rows, k:k + 1], sel)
        sel_ref[rows, :] = sel.astype(BF16)
        for u in range(g * units_per_group, (g + 1) * units_per_group):
            fetch_unit(nxt, 1 - buf, u)

    local = local_ref.at[buf]
    drain(buf)
    for c in range(LOCAL_ROWS // PERM_CHUNK):
        rows = slice(c * PERM_CHUNK, (c + 1) * PERM_CHUNK)
        ylo_ref[rows, :], yhi_ref[rows, :] = _unpack_pairs(local[rows, :])
    sel = sel_ref[...]
    f = f + jnp.concatenate([_dot(sel, ylo_ref[...]), _dot(sel, yhi_ref[...])], axis=-1)
    ci = _cond_row(i, tm)
    gate2 = mod_ref[pl.ds(ci, 1), 5 * D_MODEL:6 * D_MODEL]
    out = _layer_norm(ALPHA * x1_ref[...] + gate2 * f, g_ref[...], b_ref[...])

    @pl.when(i < T_CTX // tm)
    def _():
        outc_ref[...] = out

    @pl.when(i >= T_CTX // tm)
    def _():
        outl_ref[...] = out

    @pl.when(i == N_TOK_TILES - 1)
    def _():
        drain(1 - buf)


def _combine(x1, h2, lslot_rows, wsel_rows, unit_tab, ys, w_sgu_bf16, w_sdown_bf16, mod, ln_g, ln_b):
    n_ctx = T_CTX // TM_TOK
    row_map = lambda i, tab: (i, 0)
    const = lambda i, tab: (0, 0)
    return pl.pallas_call(
        _combine_kernel,
        out_shape=(jax.ShapeDtypeStruct((T_CTX, D_MODEL), F32),
                   jax.ShapeDtypeStruct((T_LAT, D_MODEL), F32)),
        grid_spec=pltpu.PrefetchScalarGridSpec(
            num_scalar_prefetch=1,
            grid=(N_TOK_TILES,),
            in_specs=[
                pl.BlockSpec((TM_TOK, D_MODEL), row_map),
                pl.BlockSpec((TM_TOK, D_MODEL), row_map),
                pl.BlockSpec((TM_TOK, SUBLANES), row_map),
                pl.BlockSpec((TM_TOK, SUBLANES), row_map),
                pl.BlockSpec(memory_space=pl.ANY),
                pl.BlockSpec((D_MODEL, 2 * D_SHARED), const),
                pl.BlockSpec((D_SHARED, D_MODEL), const),
                pl.BlockSpec((N_COND, 6 * D_MODEL), const),
                pl.BlockSpec((1, D_MODEL), const),
                pl.BlockSpec((1, D_MODEL), const),
            ],
            out_specs=(pl.BlockSpec((TM_TOK, D_MODEL), lambda i, tab: (jnp.minimum(i, n_ctx - 1), 0)),
                       pl.BlockSpec((TM_TOK, D_MODEL), lambda i, tab: (jnp.maximum(i - n_ctx, 0), 0))),
            scratch_shapes=[
                pltpu.VMEM((2, LOCAL_ROWS, PACK_W), I32),
                pltpu.VMEM((TM_TOK, LOCAL_ROWS), BF16),
                pltpu.VMEM((LOCAL_ROWS, PACK_W), BF16),
                pltpu.VMEM((LOCAL_ROWS, PACK_W), BF16),
                pltpu.SemaphoreType.DMA((2,)),
            ],
        ),
        compiler_params=pltpu.CompilerParams(
            dimension_semantics=("arbitrary",), vmem_limit_bytes=VMEM_LIMIT),
        name="moe_combine",
    )(unit_tab, x1, h2, lslot_rows, wsel_rows, ys, w_sgu_bf16, w_sdown_bf16, mod, ln_g, ln_b)


def _lane_rows(v, width):
    return jnp.broadcast_to(v.astype(F32)[:, None, None], (v.shape[0], 1, width))


def kernel(x_prompt, x_sample, state_ret_fwd, state_ret_bwd, cache_win_k, cache_win_v, cache_na_k, cache_na_v, c, c_ctx, w_in, w_out, ret_decay_fwd, ret_decay_bwd, ret_gn_g, ret_gn_b, win_sink, na_rpb, w_mod, b_mod, ln1_g, ln1_b, ln2_g, ln2_b, w_router, b_router, w_expert_gu, w_expert_down, w_shared_gu, w_shared_down):
    cond = jnp.concatenate(
        [c_ctx[None, :], c, jnp.zeros((N_COND - 1 - DEC_BATCH, D_MODEL), F32)], axis=0)
    mod_all = _modulation(cond, w_mod, b_mod)
    cos_t, sin_t = _rope_tables()

    x_ctx = x_prompt.reshape(T_CTX, D_MODEL)
    x_lat = x_sample.reshape(T_LAT, D_MODEL)
    sf_l, sb_l, caches = [], [], ()
    for l in range(DEPTH):
        mod = mod_all[l]
        pa, pb, pc, *caches = _in_projection(x_ctx, x_lat, mod, w_in[l].astype(BF16), tuple(caches))
        decf_s, decb_s = _lane_rows(ret_decay_fwd[l], SEQ), _lane_rows(ret_decay_bwd[l], SEQ)
        gng, gnb = ret_gn_g[l][None, :], ret_gn_b[l][None, :]
        mix_c, st_f, st_b = _ctx_mixers(pa, pb, pc, win_sink[l], decf_s, decb_s, gng, gnb)
        sf_l.append(st_f)
        sb_l.append(st_b)
        ya = _lat_retention(pa, state_ret_fwd, state_ret_bwd, l,
                            _lane_rows(ret_decay_fwd[l], RET_CHUNK), _lane_rows(ret_decay_bwd[l], RET_CHUNK),
                            gng, gnb)
        yb = _lat_window_attn(pb, cache_win_k, cache_win_v, l, win_sink[l], cos_t, sin_t)
        yc = _lat_na_attn(pc, cache_na_k, cache_na_v, l, _na_maskbias(na_rpb[l]))

        wr_hi = w_router[l].astype(BF16)
        wr_lo = (w_router[l] - wr_hi.astype(F32)).astype(BF16)
        x1, h2, eidx, wsel, rank, counts = _post_mixer(
            x_ctx, x_lat, mix_c, ya, yb, yc, w_out[l].astype(BF16), mod, ln1_g[l][None, :],
            ln1_b[l][None, :], jnp.concatenate([wr_hi, wr_lo], axis=1), b_router[l][:, None])
        lslot, unit_tab, gend = _slot_plan(eidx, rank, counts)
        xs = _dispatch(h2, lslot, unit_tab, gend)
        ys = _experts(xs, w_expert_gu, w_expert_down, l, gend)
        x_ctx, x_lat = _combine(x1, h2, lslot.T, wsel.T, unit_tab, ys, w_shared_gu[l].astype(BF16),
                                w_shared_down[l].astype(BF16), mod, ln2_g[l][None, :], ln2_b[l][None, :])

    y_prompt = x_ctx.reshape(BATCH, SEQ, D_MODEL)
    y_sample = x_lat.reshape(DEC_BATCH, DEC_SEQ, D_MODEL)
    new_sf = jnp.stack(sf_l, axis=1)
    new_sb = jnp.stack(sb_l, axis=1)

    return (y_prompt, y_sample, new_sf, new_sb, *caches)
```

```python
import functools

import numpy as np
import jax
import jax.numpy as jnp
from jax import lax
from jax.experimental import pallas as pl
from jax.experimental.pallas import tpu as pltpu

F32 = jnp.float32
BF16 = jnp.bfloat16
I32 = jnp.int32

D_MODEL = 1024
BATCH = 32
SEQ = 256
DEPTH = 2
DEC_BATCH = 4
DEC_SEQ = 2048
PAST_LEN = 256
GRID_W = 64
HEAD_DIM = 64
ATTN_SCALE = HEAD_DIM ** -0.5
H_A = 4
W_A = H_A * HEAD_DIM
GN_EPS = 1e-5
H_B = 6
KV_B = 2
W_B = H_B * HEAD_DIM
KV_W_B = KV_B * HEAD_DIM
WINDOW = 128
ROPE_BASE = 10000.0
H_C = 6
W_C = H_C * HEAD_DIM
NA_ROWS = 8
NA_COLS = 16
IN_WIDTH = 4 * W_A + W_B + 2 * KV_W_B + 3 * W_C
N_EXPERTS = 64
TOP_K = 6
N_GROUPS = 8
GROUP_SIZE = N_EXPERTS // N_GROUPS
TOPK_GROUPS = 4
D_EXPERT = 256
D_SHARED = 256
ROUTE_SCALE = 2.5
ALPHA = (2 * DEPTH) ** 0.25
LN_EPS = 1e-5
NEG_INF = -1e30
PICKED = -3e38

T_CTX = BATCH * SEQ
T_LAT = DEC_BATCH * DEC_SEQ
T_ALL = T_CTX + T_LAT
N_COND = 8

PA_W = 4 * W_A
PB_W = W_B + 2 * KV_W_B
PC_W = 3 * W_C

LANES = 128
SUBLANES = 8
VMEM_LIMIT = 56 * 1024 * 1024

TM_PROJ = 512
TM_TOK = 256
RET_CHUNK = 256
POST_TILES = 4
CTX_STACK = 3
NA_STACK = 6
WIN_QB = 256
WIN_BAND = WIN_QB + 2 * WINDOW
NA_TILE_ROWS = 4
NA_KEY_ROWS = 11
BM = 512
X_AHEAD = 3
X_BUFS = X_AHEAD + 1
UNIT = SUBLANES
UNITS_PER_BLOCK = BM // UNIT
PERM_CHUNK = 256
SEL_ROWS = 16
N_TOK_TILES = T_ALL // TM_TOK
LOCAL_ROWS = -(-(TM_TOK * TOP_K + N_EXPERTS * (UNIT - 1)) // PERM_CHUNK) * PERM_CHUNK
MAX_UNITS = LOCAL_ROWS // UNIT
N_ASSIGN = T_ALL * TOP_K
N_BLOCKS = -(-(N_ASSIGN + N_TOK_TILES * N_EXPERTS * (UNIT - 1) + N_EXPERTS * (BM - 1)) // BM)
SPARE_BLOCKS = 2 * -(-LOCAL_ROWS // BM)
SPARE_UNIT0 = N_BLOCKS * UNITS_PER_BLOCK
N_BLOCKS_ALL = N_BLOCKS + SPARE_BLOCKS
N_SLOTS = N_BLOCKS_ALL * BM


def _dot(a, b):
    return jnp.dot(a, b, preferred_element_type=F32)


def _dot_nt(a, b):
    return lax.dot_general(a, b, (((1,), (1,)), ((), ())), preferred_element_type=F32)


def _silu(x):
    return x * jax.nn.sigmoid(x)


def _log_sigmoid(x):
    return jnp.minimum(x, 0.0) - jnp.log(1.0 + jnp.exp(-jnp.abs(x)))


def _cond_row(tile, tile_rows):
    n_ctx = T_CTX // tile_rows
    per_lat = DEC_SEQ // tile_rows
    return jnp.where(tile < n_ctx, 0, 1 + (tile - n_ctx) // per_lat)


def _layer_norm(x, g, b):
    mu = jnp.mean(x, -1, keepdims=True)
    xc = x - mu
    var = jnp.mean(xc * xc, -1, keepdims=True)
    return xc * lax.rsqrt(var + LN_EPS) * g + b


MOD_TN = 1536


def _mod_kernel(cond_ref, w_ref, b_ref, o_ref):
    s = _silu(cond_ref[...])
    s_hi = s.astype(BF16)
    s_lo = (s - s_hi.astype(F32)).astype(BF16)
    w = w_ref[0]
    w_hi = w.astype(BF16)
    w_lo = (w - w_hi.astype(F32)).astype(BF16)
    o_ref[0] = _dot(s_hi, w_hi) + _dot(s_lo, w_hi) + _dot(s_hi, w_lo) + b_ref[0]


def _modulation(cond, w_mod, b_mod):
    n_out = 6 * D_MODEL
    return pl.pallas_call(
        _mod_kernel,
        out_shape=jax.ShapeDtypeStruct((DEPTH, N_COND, n_out), F32),
        grid=(DEPTH, n_out // MOD_TN),
        in_specs=[
            pl.BlockSpec((N_COND, D_MODEL), lambda l, j: (0, 0)),
            pl.BlockSpec((1, D_MODEL, MOD_TN), lambda l, j: (l, 0, j)),
            pl.BlockSpec((1, 1, MOD_TN), lambda l, j: (l, 0, j)),
        ],
        out_specs=pl.BlockSpec((1, N_COND, MOD_TN), lambda l, j: (l, 0, j)),
        compiler_params=pltpu.CompilerParams(
            dimension_semantics=("arbitrary", "arbitrary"), vmem_limit_bytes=VMEM_LIMIT),
        name="modulation",
    )(cond, w_mod, b_mod.reshape(DEPTH, 1, n_out))


SEQ_PER_PROJ = TM_PROJ // SEQ


def _inproj_kernel(n_prev, xc_ref, xl_ref, mod_ref, w_ref, *refs):
    prev_refs = refs[:4] if n_prev else ()
    pa_ref, pb_ref, pc_ref, wk_ref, wv_ref, nk_ref, nv_ref = refs[len(prev_refs):]
    i = pl.program_id(0)
    is_ctx = i < T_CTX // TM_PROJ
    targets = ((wk_ref, PA_W + W_B, KV_B), (wv_ref, PA_W + W_B + KV_W_B, KV_B),
               (nk_ref, PA_W + PB_W + W_C, H_C), (nv_ref, PA_W + PB_W + 2 * W_C, H_C))

    def project(x_ref, ci, with_cache):
        sh = mod_ref[pl.ds(ci, 1), 0:D_MODEL]
        sc = mod_ref[pl.ds(ci, 1), D_MODEL:2 * D_MODEL]
        for s in range(SEQ_PER_PROJ):
            rows = slice(s * SEQ, (s + 1) * SEQ)
            h = x_ref[rows, :] * (1.0 + sc) + sh
            p = _dot(h.astype(BF16), w_ref[...])
            pa_ref[rows, :] = p[:, 0:PA_W].astype(BF16)
            pb_ref[rows, :] = p[:, PA_W:PA_W + PB_W].astype(BF16)
            pc_ref[rows, :] = p[:, PA_W + PB_W:IN_WIDTH].astype(BF16)
            if with_cache:
                for j, (ref, col0, n_heads) in enumerate(targets):
                    if n_prev:
                        ref[s, 0:n_prev] = prev_refs[j][s]
                    for hh in range(n_heads):
                        ref[s, n_prev, hh] = p[:, col0 + hh * HEAD_DIM:col0 + (hh + 1) * HEAD_DIM]

    @pl.when(is_ctx)
    def _():
        project(xc_ref, 0, True)

    @pl.when(jnp.logical_not(is_ctx))
    def _():
        project(xl_ref, _cond_row(i, TM_PROJ), False)


def _in_projection(x_ctx, x_lat, mod, w_in_bf16, earlier):
    n_ctx_tiles = T_CTX // TM_PROJ
    n_prev = earlier[0].shape[1] if earlier else 0

    def cache_spec(n_layers, n_heads):
        return pl.BlockSpec((SEQ_PER_PROJ, n_layers, n_heads, SEQ, HEAD_DIM),
                            lambda i: (jnp.minimum(i, n_ctx_tiles - 1), 0, 0, 0, 0))

    cache_heads = (KV_B, KV_B, H_C, H_C)
    return pl.pallas_call(
        functools.partial(_inproj_kernel, n_prev),
        out_shape=(
            jax.ShapeDtypeStruct((T_ALL, PA_W), BF16),
            jax.ShapeDtypeStruct((T_ALL, PB_W), BF16),
            jax.ShapeDtypeStruct((T_ALL, PC_W), BF16),
        ) + tuple(jax.ShapeDtypeStruct((BATCH, n_prev + 1, nh, SEQ, HEAD_DIM), F32) for nh in cache_heads),
        grid=(T_ALL // TM_PROJ,),
        in_specs=[
            pl.BlockSpec((TM_PROJ, D_MODEL), lambda i: (jnp.minimum(i, n_ctx_tiles - 1), 0)),
            pl.BlockSpec((TM_PROJ, D_MODEL), lambda i: (jnp.maximum(i - n_ctx_tiles, 0), 0)),
            pl.BlockSpec((N_COND, 6 * D_MODEL), lambda i: (0, 0)),
            pl.BlockSpec((D_MODEL, IN_WIDTH), lambda i: (0, 0)),
        ] + [cache_spec(n_prev, nh) for nh in cache_heads if n_prev],
        out_specs=(
            pl.BlockSpec((TM_PROJ, PA_W), lambda i: (i, 0)),
            pl.BlockSpec((TM_PROJ, PB_W), lambda i: (i, 0)),
            pl.BlockSpec((TM_PROJ, PC_W), lambda i: (i, 0)),
        ) + tuple(cache_spec(n_prev + 1, nh) for nh in cache_heads),
        compiler_params=pltpu.CompilerParams(
            dimension_semantics=("arbitrary",), vmem_limit_bytes=VMEM_LIMIT),
        name="in_projection",
    )(x_ctx, x_lat, mod, w_in_bf16, *earlier)


def _decay_matrix(lg_f, lg_b, n):
    row = lax.broadcasted_iota(I32, (n, n), 0)
    col = lax.broadcasted_iota(I32, (n, n), 1)
    diff = (row - col).astype(F32)
    fwd = jnp.where(diff >= 0, jnp.exp(lg_f * jnp.maximum(diff, 0.0)), 0.0)
    bwd = jnp.where(diff <= 0, jnp.exp(lg_b * jnp.maximum(-diff, 0.0)), 0.0)
    return (fwd + bwd) * ATTN_SCALE


def _stacked_softmax_attend(scores, values, extra_logit=None):
    rows = scores[0].shape[0]
    s = jnp.concatenate(scores, axis=0)
    m = jnp.max(s, -1, keepdims=True)
    if extra_logit is not None:
        m = jnp.maximum(m, extra_logit)
    p = jnp.exp(s - m)
    den = jnp.sum(p, -1, keepdims=True)
    if extra_logit is not None:
        den = den + jnp.exp(extra_logit - m)
    p = p.astype(BF16)
    return [_dot(p[h * rows:(h + 1) * rows], v) / den[h * rows:(h + 1) * rows]
            for h, v in enumerate(values)]


def _lane_xor_matrix(width, distance):
    r = lax.broadcasted_iota(I32, (width, width), 0)
    c = lax.broadcasted_iota(I32, (width, width), 1)
    return jnp.where((r ^ distance) == c, 1.0, 0.0).astype(BF16)


def _ctx_mixer_kernel(sink_ref, pa_ref, pb_ref, pc_ref, decf_ref, decb_ref, gng_ref, gnb_ref,
                      mix_ref, sf_ref, sb_ref, dmat_ref, zf_ref, zb_ref):
    n = SEQ
    hd = HEAD_DIM

    @pl.when(pl.program_id(0) == 0)
    def _():
        pos = lax.broadcasted_iota(I32, (n, hd), 0).astype(F32)
        zf, zb = [], []
        for h in range(H_A):
            lg_f = _log_sigmoid(decf_ref[h])
            lg_b = _log_sigmoid(decb_ref[h])
            dmat_ref[h] = _decay_matrix(lg_f, lg_b, n)
            zf.append(jnp.exp(lg_f[:, 0:hd] * (n - 1.0 - pos)) * ATTN_SCALE)
            zb.append(jnp.exp(lg_b[:, 0:hd] * pos) * ATTN_SCALE)
        for p in range(H_A // 2):
            zf_ref[p] = jnp.concatenate(zf[2 * p:2 * p + 2], axis=1)
            zb_ref[p] = jnp.concatenate(zb[2 * p:2 * p + 2], axis=1)

    low = lax.broadcasted_iota(I32, (n, LANES), 1) < hd

    def own_half(x, h):
        zero = jnp.zeros_like(x)
        return jnp.where(low, x, zero) if h % 2 == 0 else jnp.where(low, zero, x)

    def merge(first, second):
        return jnp.where(low, first, second)

    def pair_cols(ref, base, p):
        return ref[:, base + p * LANES:base + (p + 1) * LANES]

    def head_mean(x):
        first = jnp.sum(jnp.where(low, x, 0.0), -1, keepdims=True)
        second = jnp.sum(jnp.where(low, 0.0, x), -1, keepdims=True)
        return merge(first, second) * (1.0 / hd)

    for p in range(H_A // 2):
        q_pair, k_pair = pair_cols(pa_ref, 0, p), pair_cols(pa_ref, W_A, p)
        v_pair, gate_pair = pair_cols(pa_ref, 2 * W_A, p), pair_cols(pa_ref, 3 * W_A, p)
        outs = []
        for h in (2 * p, 2 * p + 1):
            a = _dot_nt(own_half(q_pair, h), k_pair)
            outs.append(_dot((a * dmat_ref[h]).astype(BF16), v_pair))
        o = merge(outs[0], outs[1])
        kf = k_pair.astype(F32)
        for st_ref, z_ref in ((sf_ref, zf_ref), (sb_ref, zb_ref)):
            st = _dot((kf * z_ref[p]).T.astype(BF16), v_pair)
            st_ref[0, 2 * p] = st[0:hd, 0:hd]
            st_ref[0, 2 * p + 1] = st[hd:2 * hd, hd:2 * hd]
        mu = head_mean(o)
        oc = o - mu
        var = head_mean(oc * oc)
        on = oc * lax.rsqrt(var + GN_EPS) * pair_cols(gng_ref, 0, p) + pair_cols(gnb_ref, 0, p)
        mix_ref[:, p * LANES:(p + 1) * LANES] = (on * _silu(gate_pair.astype(F32))).astype(BF16)

    group = H_B // KV_B
    swap = _lane_xor_matrix(LANES, hd)
    kv_k = pb_ref[:, W_B:W_B + KV_W_B]
    kv_v = pb_ref[:, W_B + KV_W_B:W_B + 2 * KV_W_B]
    scores = []
    for hh in range(H_B):
        q = own_half(pair_cols(pb_ref, 0, hh // 2), hh) * ATTN_SCALE
        if hh % 2 != hh // group:
            q = _dot(q, swap).astype(BF16)
        scores.append(_dot_nt(q, kv_k))
    sinks = jnp.concatenate([jnp.full((n, 1), sink_ref[hh], F32) for hh in range(H_B)], axis=0)
    outs = []
    stacked = []
    for g0 in range(0, H_B, CTX_STACK):
        stacked += _stacked_softmax_attend(scores[g0:g0 + CTX_STACK], [kv_v] * CTX_STACK,
                                           sinks[g0 * n:(g0 + CTX_STACK) * n])
    for hh, o in enumerate(stacked):
        o = o.astype(BF16)
        outs.append(_dot(o, swap).astype(BF16) if hh % 2 != hh // group else o)
    for p in range(H_B // 2):
        mix_ref[:, W_A + p * LANES:W_A + (p + 1) * LANES] = merge(outs[2 * p], outs[2 * p + 1])

    scores, values = [], []
    for hh in range(H_C):
        q = own_half(pair_cols(pc_ref, 0, hh // 2), hh) * ATTN_SCALE
        scores.append(_dot_nt(q, pair_cols(pc_ref, W_C, hh // 2)))
        values.append(pair_cols(pc_ref, 2 * W_C, hh // 2))
    outs = []
    for g0 in range(0, H_C, CTX_STACK):
        outs += _stacked_softmax_attend(scores[g0:g0 + CTX_STACK], values[g0:g0 + CTX_STACK])
    for p in range(H_C // 2):
        mix_ref[:, W_A + W_B + p * LANES:W_A + W_B + (p + 1) * LANES] = merge(
            outs[2 * p], outs[2 * p + 1]).astype(BF16)


def _ctx_mixers(pa, pb, pc, sink, decf, decb, gng, gnb):
    return pl.pallas_call(
        _ctx_mixer_kernel,
        out_shape=(
            jax.ShapeDtypeStruct((T_CTX, D_MODEL), BF16),
            jax.ShapeDtypeStruct((BATCH, H_A, HEAD_DIM, HEAD_DIM), F32),
            jax.ShapeDtypeStruct((BATCH, H_A, HEAD_DIM, HEAD_DIM), F32),
        ),
        grid=(BATCH,),
        in_specs=[
            pl.BlockSpec(memory_space=pltpu.SMEM),
            pl.BlockSpec((SEQ, PA_W), lambda b: (b, 0)),
            pl.BlockSpec((SEQ, PB_W), lambda b: (b, 0)),
            pl.BlockSpec((SEQ, PC_W), lambda b: (b, 0)),
            pl.BlockSpec((H_A, 1, SEQ), lambda b: (0, 0, 0)),
            pl.BlockSpec((H_A, 1, SEQ), lambda b: (0, 0, 0)),
            pl.BlockSpec((1, W_A), lambda b: (0, 0)),
            pl.BlockSpec((1, W_A), lambda b: (0, 0)),
        ],
        out_specs=(
            pl.BlockSpec((SEQ, D_MODEL), lambda b: (b, 0)),
            pl.BlockSpec((1, H_A, HEAD_DIM, HEAD_DIM), lambda b: (b, 0, 0, 0)),
            pl.BlockSpec((1, H_A, HEAD_DIM, HEAD_DIM), lambda b: (b, 0, 0, 0)),
        ),
        scratch_shapes=[
            pltpu.VMEM((H_A, SEQ, SEQ), F32),
            pltpu.VMEM((H_A // 2, SEQ, LANES), F32),
            pltpu.VMEM((H_A // 2, SEQ, LANES), F32),
        ],
        compiler_params=pltpu.CompilerParams(
            dimension_semantics=("arbitrary",), vmem_limit_bytes=VMEM_LIMIT),
        name="ctx_mixers",
    )(sink, pa, pb, pc, decf, decb, gng, gnb)


def _lat_ret_kernel(pa_ref, stf_ref, stb_ref, decf_ref, decb_ref, gng_ref, gnb_ref, ya_ref):
    c = RET_CHUNK
    hd = HEAD_DIM
    n_chunks = DEC_SEQ // c
    pos = lax.broadcasted_iota(I32, (c, LANES), 0).astype(F32)
    low = lax.broadcasted_iota(I32, (c, LANES), 1) < hd
    same_head = (lax.broadcasted_iota(I32, (LANES, LANES), 0) // hd
                 == lax.broadcasted_iota(I32, (LANES, LANES), 1) // hd)
    zero_blk = jnp.zeros((hd, hd), F32)

    def block_diag(a, b):
        return jnp.concatenate([jnp.concatenate([a, zero_blk], axis=1),
                                jnp.concatenate([zero_blk, b], axis=1)], axis=0)

    def head_mean(x):
        first = jnp.sum(jnp.where(low, x, 0.0), -1, keepdims=True)
        second = jnp.sum(jnp.where(low, 0.0, x), -1, keepdims=True)
        return jnp.where(low, first, second) * (1.0 / hd)

    for p in range(H_A // 2):
        lg_f = [_log_sigmoid(decf_ref[h]) for h in (2 * p, 2 * p + 1)]
        lg_b = [_log_sigmoid(decb_ref[h]) for h in (2 * p, 2 * p + 1)]
        dmat = [_decay_matrix(lg_f[t], lg_b[t], c) for t in range(2)]
        lf = jnp.where(low[0:1], lg_f[0][:, 0:LANES], lg_f[1][:, 0:LANES])
        lb = jnp.where(low[0:1], lg_b[0][:, 0:LANES], lg_b[1][:, 0:LANES])
        zf = jnp.exp(lf * (c - 1.0 - pos)) * ATTN_SCALE
        zb = jnp.exp(lb * pos) * ATTN_SCALE
        xf = jnp.exp(lf * (pos + 1.0))
        xb = jnp.exp(lb * (c - pos))
        gcf = jnp.exp(lf * float(c))
        gcb = jnp.exp(lb * float(c))
        pair = slice(p * LANES, (p + 1) * LANES)

        def chunk(i, base):
            return pa_ref[i * c:(i + 1) * c, base + p * LANES:base + (p + 1) * LANES]

        kv_f, kv_b = [], []
        for i in range(n_chunks):
            kf = chunk(i, W_A).astype(F32)
            v = chunk(i, 2 * W_A)
            kv_f.append(jnp.where(same_head, _dot((kf * zf).T.astype(BF16), v), 0.0))
            kv_b.append(jnp.where(same_head, _dot((kf * zb).T.astype(BF16), v), 0.0))

        s = block_diag(stf_ref[0, 0, 2 * p], stf_ref[0, 0, 2 * p + 1])
        seen_f = []
        for i in range(n_chunks):
            seen_f.append(s)
            s = gcf * s + kv_f[i]
        s = block_diag(stb_ref[0, 0, 2 * p], stb_ref[0, 0, 2 * p + 1])
        seen_b = [None] * n_chunks
        for i in reversed(range(n_chunks)):
            seen_b[i] = s
            s = gcb * s + kv_b[i]

        for i in range(n_chunks):
            q_pair, k_pair, v_pair = chunk(i, 0), chunk(i, W_A), chunk(i, 2 * W_A)
            inner = []
            for t in range(2):
                zero = jnp.zeros_like(q_pair)
                q = jnp.where(low, q_pair, zero) if t == 0 else jnp.where(low, zero, q_pair)
                inner.append(_dot((_dot_nt(q, k_pair) * dmat[t]).astype(BF16), v_pair))
            qf = q_pair.astype(F32)
            lhs = jnp.concatenate([(qf * xf).astype(BF16), (qf * xb).astype(BF16)], axis=1)
            rhs = jnp.concatenate([seen_f[i], seen_b[i]], axis=0).astype(BF16)
            o = jnp.where(low, inner[0], inner[1]) + _dot(lhs, rhs)
            mu = head_mean(o)
            oc = o - mu
            var = head_mean(oc * oc)
            on = oc * lax.rsqrt(var + GN_EPS) * gng_ref[:, pair] + gnb_ref[:, pair]
            y = on * _silu(chunk(i, 3 * W_A).astype(F32))
            ya_ref[i * c:(i + 1) * c, pair] = y.astype(BF16)


def _lat_retention(pa, st_f, st_b, layer, decf, decb, gng, gnb):
    lat0 = T_CTX // DEC_SEQ
    st_spec = pl.BlockSpec((1, 1, H_A, HEAD_DIM, HEAD_DIM), lambda b: (b, layer, 0, 0, 0))
    return pl.pallas_call(
        _lat_ret_kernel,
        out_shape=jax.ShapeDtypeStruct((T_LAT, W_A), BF16),
        grid=(DEC_BATCH,),
        in_specs=[
            pl.BlockSpec((DEC_SEQ, PA_W), lambda b: (lat0 + b, 0)),
            st_spec, st_spec,
            pl.BlockSpec((H_A, 1, RET_CHUNK), lambda b: (0, 0, 0)),
            pl.BlockSpec((H_A, 1, RET_CHUNK), lambda b: (0, 0, 0)),
            pl.BlockSpec((1, W_A), lambda b: (0, 0)),
            pl.BlockSpec((1, W_A), lambda b: (0, 0)),
        ],
        out_specs=pl.BlockSpec((DEC_SEQ, W_A), lambda b: (b, 0)),
        compiler_params=pltpu.CompilerParams(
            dimension_semantics=("arbitrary",), vmem_limit_bytes=VMEM_LIMIT),
        name="lat_retention",
    )(pa, st_f, st_b, decf, decb, gng, gnb)


def _rope(x, cos, sin_signed, swap):
    return x.astype(F32) * cos + _dot(x, swap) * sin_signed


def _lat_win_kernel(sink_ref, pq_ref, pseq_ref, kctx_ref, vctx_ref, cos_ref, sin_ref, yb_ref,
                    krope_ref, kc_ref, vc_ref, mask_ref):
    n = pl.program_id(1)
    hd = HEAD_DIM
    qb = WIN_QB
    band = WIN_BAND
    group = H_B // KV_B
    rot = _lane_xor_matrix(LANES, hd // 2)
    swap = _lane_xor_matrix(LANES, hd)

    @pl.when(n == 0)
    def _():
        k = pseq_ref[:, W_B:W_B + KV_W_B]
        krope_ref[...] = _rope(k, cos_ref[...], sin_ref[...], rot).astype(BF16)
        kc_ref[...] = jnp.concatenate([kctx_ref[0, 0, j] for j in range(KV_B)], axis=1).astype(BF16)
        vc_ref[...] = jnp.concatenate([vctx_ref[0, 0, j] for j in range(KV_B)], axis=1).astype(BF16)
        q_in_blk = lax.broadcasted_iota(I32, (group * qb, band), 0) % qb
        k_in_band = lax.broadcasted_iota(I32, (group * qb, band), 1)
        for ty, lead in enumerate((0, WINDOW, band - qb)):
            mask_ref[ty] = jnp.where(jnp.abs(k_in_band - lead - q_in_blk) <= WINDOW, 0.0, NEG_INF)

    q_rows = pl.ds(pl.multiple_of(n * qb, qb), qb)
    cos_q = cos_ref[q_rows, :]
    sin_q = sin_ref[q_rows, :]
    low = lax.broadcasted_iota(I32, (qb, LANES), 1) < hd
    q_heads = []
    for p in range(H_B // 2):
        q_pair = _rope(pq_ref[:, p * LANES:(p + 1) * LANES], cos_q, sin_q, rot) * ATTN_SCALE
        for hh in (2 * p, 2 * p + 1):
            q = jnp.where(low, q_pair, 0.0) if hh % 2 == 0 else jnp.where(low, 0.0, q_pair)
            q = q.astype(BF16)
            q_heads.append(_dot(q, swap).astype(BF16) if hh % 2 != hh // group else q)

    ws = jnp.clip(n * qb - WINDOW, 0, DEC_SEQ - band)
    k_rows = pl.ds(pl.multiple_of(ws, WINDOW), band)
    n_blk = DEC_SEQ // qb
    band_bias = mask_ref[jnp.where(n == 0, 0, jnp.where(n == n_blk - 1, 2, 1))]
    head_of_row = lax.broadcasted_iota(I32, (group * qb, 1), 0) // qb
    kw = krope_ref[k_rows, :]
    vw = pseq_ref[k_rows, W_B + KV_W_B:W_B + 2 * KV_W_B]
    outs = []
    for j in range(KV_B):
        heads = [j * group + g for g in range(group)]
        qs = jnp.concatenate([q_heads[hh] for hh in heads], axis=0)
        s_loc = _dot_nt(qs, kw) + band_bias
        s_ctx = _dot_nt(qs, kc_ref[...])
        sink = jnp.zeros((group * qb, 1), F32)
        for g, hh in enumerate(heads):
            sink = jnp.where(head_of_row == g, sink_ref[hh], sink)
        m = jnp.maximum(jnp.maximum(jnp.max(s_loc, -1, keepdims=True),
                                    jnp.max(s_ctx, -1, keepdims=True)), sink)
        p_loc = jnp.exp(s_loc - m)
        p_ctx = jnp.exp(s_ctx - m)
        den = (jnp.sum(p_loc, -1, keepdims=True) + jnp.sum(p_ctx, -1, keepdims=True)
               + jnp.exp(sink - m))
        o = ((_dot(p_loc.astype(BF16), vw) + _dot(p_ctx.astype(BF16), vc_ref[...])) / den).astype(BF16)
        for g, hh in enumerate(heads):
            o_h = o[g * qb:(g + 1) * qb]
            outs.append(_dot(o_h, swap).astype(BF16) if hh % 2 != j else o_h)
    for p in range(H_B // 2):
        yb_ref[:, p * LANES:(p + 1) * LANES] = jnp.where(low, outs[2 * p], outs[2 * p + 1])


def _lat_window_attn(pb, cache_k, cache_v, layer, sink, cos_t, sin_t):
    n_blk = DEC_SEQ // WIN_QB
    lat_blk0 = T_CTX // WIN_QB
    lat_seq0 = T_CTX // DEC_SEQ
    ctx_spec = pl.BlockSpec((1, 1, KV_B, PAST_LEN, HEAD_DIM), lambda b, n: (b, layer, 0, 0, 0))
    return pl.pallas_call(
        _lat_win_kernel,
        out_shape=jax.ShapeDtypeStruct((T_LAT, W_B), BF16),
        grid=(DEC_BATCH, n_blk),
        in_specs=[
            pl.BlockSpec(memory_space=pltpu.SMEM),
            pl.BlockSpec((WIN_QB, PB_W), lambda b, n: (lat_blk0 + b * n_blk + n, 0)),
            pl.BlockSpec((DEC_SEQ, PB_W), lambda b, n: (lat_seq0 + b, 0)),
            ctx_spec, ctx_spec,
            pl.BlockSpec((DEC_SEQ, LANES), lambda b, n: (0, 0)),
            pl.BlockSpec((DEC_SEQ, LANES), lambda b, n: (0, 0)),
        ],
        out_specs=pl.BlockSpec((WIN_QB, W_B), lambda b, n: (b * n_blk + n, 0)),
        scratch_shapes=[
            pltpu.VMEM((DEC_SEQ, KV_W_B), BF16),
            pltpu.VMEM((PAST_LEN, KV_W_B), BF16),
            pltpu.VMEM((PAST_LEN, KV_W_B), BF16),
            pltpu.VMEM((3, (H_B // KV_B) * WIN_QB, WIN_BAND), F32),
        ],
        compiler_params=pltpu.CompilerParams(
            dimension_semantics=("arbitrary", "arbitrary"), vmem_limit_bytes=VMEM_LIMIT),
        name="lat_window_attn",
    )(sink, pb, pb, cache_k, cache_v, cos_t, sin_t)


NA_Q = NA_TILE_ROWS * GRID_W
NA_K = NA_KEY_ROWS * GRID_W
NA_TILES = DEC_SEQ // NA_Q
LAT_ROWS = DEC_SEQ // GRID_W


def _na_window_start(tile):
    return jnp.clip(tile * NA_TILE_ROWS - NA_ROWS // 2, 0, LAT_ROWS - NA_KEY_ROWS)


def _lat_na_kernel(pq_ref, pseq_ref, kctx_ref, vctx_ref, bias_ref, yc_ref, kc_ref, vc_ref):
    t = pl.program_id(1)
    hd = HEAD_DIM
    nq = NA_Q

    @pl.when(t == 0)
    def _():
        for p in range(H_C // 2):
            kc_ref[p] = jnp.concatenate([kctx_ref[0, 0, 2 * p], kctx_ref[0, 0, 2 * p + 1]], axis=1).astype(BF16)
            vc_ref[p] = jnp.concatenate([vctx_ref[0, 0, 2 * p], vctx_ref[0, 0, 2 * p + 1]], axis=1).astype(BF16)

    k_rows = pl.ds(pl.multiple_of(_na_window_start(t) * GRID_W, GRID_W), NA_K)
    low = lax.broadcasted_iota(I32, (nq, LANES), 1) < hd
    loc, ctx = [], []
    for hh in range(H_C):
        p = hh // 2
        q_pair = pq_ref[:, p * LANES:(p + 1) * LANES] * ATTN_SCALE
        zero = jnp.zeros_like(q_pair)
        q = jnp.where(low, q_pair, zero) if hh % 2 == 0 else jnp.where(low, zero, q_pair)
        loc.append(_dot_nt(q, pseq_ref[k_rows, W_C + p * LANES:W_C + (p + 1) * LANES]) + bias_ref[0, hh])
        ctx.append(_dot_nt(q, kc_ref[p]))
    outs = []
    for g0 in range(0, H_C, NA_STACK):
        s_loc = jnp.concatenate(loc[g0:g0 + NA_STACK], axis=0)
        s_ctx = jnp.concatenate(ctx[g0:g0 + NA_STACK], axis=0)
        m = jnp.maximum(jnp.max(s_loc, -1, keepdims=True), jnp.max(s_ctx, -1, keepdims=True))
        p_loc = jnp.exp(s_loc - m)
        p_ctx = jnp.exp(s_ctx - m)
        den = jnp.sum(p_loc, -1, keepdims=True) + jnp.sum(p_ctx, -1, keepdims=True)
        p_loc = p_loc.astype(BF16)
        p_ctx = p_ctx.astype(BF16)
        for hh in range(g0, g0 + NA_STACK):
            p = hh // 2
            rows = slice((hh - g0) * nq, (hh - g0 + 1) * nq)
            vw = pseq_ref[k_rows, 2 * W_C + p * LANES:2 * W_C + (p + 1) * LANES]
            outs.append((_dot(p_loc[rows], vw) + _dot(p_ctx[rows], vc_ref[p])) / den[rows])
    for p in range(H_C // 2):
        yc_ref[:, p * LANES:(p + 1) * LANES] = jnp.where(low, outs[2 * p], outs[2 * p + 1]).astype(BF16)


def _na_tile_type(t):
    return jnp.where(t == 0, 0, jnp.where(t == NA_TILES - 1, 2, 1))


def _lat_na_attn(pc, cache_k, cache_v, layer, maskbias):
    lat_tile0 = T_CTX // NA_Q
    lat_seq0 = T_CTX // DEC_SEQ
    ctx_spec = pl.BlockSpec((1, 1, H_C, PAST_LEN, HEAD_DIM), lambda b, t: (b, layer, 0, 0, 0))
    return pl.pallas_call(
        _lat_na_kernel,
        out_shape=jax.ShapeDtypeStruct((T_LAT, W_C), BF16),
        grid=(DEC_BATCH, NA_TILES),
        in_specs=[
            pl.BlockSpec((NA_Q, PC_W), lambda b, t: (lat_tile0 + b * NA_TILES + t, 0)),
            pl.BlockSpec((DEC_SEQ, PC_W), lambda b, t: (lat_seq0 + b, 0)),
            ctx_spec, ctx_spec,
            pl.BlockSpec((1, H_C, NA_Q, NA_K), lambda b, t: (_na_tile_type(t), 0, 0, 0)),
        ],
        out_specs=pl.BlockSpec((NA_Q, W_C), lambda b, t: (b * NA_TILES + t, 0)),
        scratch_shapes=[
            pltpu.VMEM((H_C // 2, PAST_LEN, LANES), BF16),
            pltpu.VMEM((H_C // 2, PAST_LEN, LANES), BF16),
        ],
        compiler_params=pltpu.CompilerParams(
            dimension_semantics=("arbitrary", "arbitrary"), vmem_limit_bytes=VMEM_LIMIT),
        name="lat_na_attn",
    )(pc, pc, cache_k, cache_v, maskbias)


def _na_block_index():
    out = np.zeros((3, NA_TILE_ROWS, NA_KEY_ROWS), np.int32)
    for ty, tile in enumerate((0, 1, NA_TILES - 1)):
        r = tile * NA_TILE_ROWS
        ws = int(np.clip(r - NA_ROWS // 2, 0, LAT_ROWS - NA_KEY_ROWS))
        for qq in range(NA_TILE_ROWS):
            qr = r + qq
            r0 = int(np.clip(qr - NA_ROWS // 2, 0, LAT_ROWS - NA_ROWS))
            for kk in range(NA_KEY_ROWS):
                kr = ws + kk
                out[ty, qq, kk] = kr - qr + NA_ROWS - 1 if r0 <= kr < r0 + NA_ROWS else 2 * NA_ROWS - 1
    return out


def _na_maskbias(rpb):
    qc = np.arange(GRID_W)[:, None]
    kc = np.arange(GRID_W)[None, :]
    c0 = np.clip(qc - NA_COLS // 2, 0, GRID_W - NA_COLS)
    col_ok = (kc >= c0) & (kc < c0 + NA_COLS)
    ci = np.clip(kc - qc + NA_COLS - 1, 0, 2 * NA_COLS - 2)
    onehot = (ci[None] == np.arange(2 * NA_COLS - 1)[:, None, None]).astype(np.float32)
    cols = jnp.einsum("hab,bqk->haqk", rpb, jnp.asarray(onehot), precision=lax.Precision.HIGHEST)
    cols = jnp.where(jnp.asarray(col_ok)[None, None], cols, NEG_INF)
    cols = jnp.concatenate([cols, jnp.full((H_C, 1, GRID_W, GRID_W), NEG_INF, F32)], axis=1)
    block_index = _na_block_index()

    def assemble(cols_ref, out_ref):
        for ty in range(3):
            for qq in range(NA_TILE_ROWS):
                for kk in range(NA_KEY_ROWS):
                    out_ref[ty, 0, qq * GRID_W:(qq + 1) * GRID_W, kk * GRID_W:(kk + 1) * GRID_W] = (
                        cols_ref[0, int(block_index[ty, qq, kk])])

    return pl.pallas_call(
        assemble,
        out_shape=jax.ShapeDtypeStruct((3, H_C, NA_Q, NA_K), F32),
        grid=(H_C,),
        in_specs=[pl.BlockSpec((1, 2 * NA_ROWS, GRID_W, GRID_W), lambda h: (h, 0, 0, 0))],
        out_specs=pl.BlockSpec((3, 1, NA_Q, NA_K), lambda h: (0, h, 0, 0)),
        compiler_params=pltpu.CompilerParams(dimension_semantics=("arbitrary",)),
        name="na_bias_assemble",
    )(cols)


def _rope_tables():
    t = np.arange(DEC_SEQ)
    n_freq = HEAD_DIM // 4
    inv = (ROPE_BASE ** (-np.arange(n_freq, dtype=np.float32) / n_freq)).astype(np.float32)
    row = (t // GRID_W).astype(np.float32)[:, None] * inv
    col = (t % GRID_W).astype(np.float32)[:, None] * inv
    ang = np.concatenate([row, col], -1)
    cos, sin = np.cos(ang), np.sin(ang)
    cos_h = np.concatenate([cos, cos], -1)
    sin_h = np.concatenate([-sin, sin], -1)
    reps = LANES // HEAD_DIM
    return (jnp.asarray(np.tile(cos_h, (1, reps)), F32), jnp.asarray(np.tile(sin_h, (1, reps)), F32))


def _first_index_of(mask, iota, sentinel):
    return jnp.min(jnp.where(mask, iota, sentinel), axis=0, keepdims=True)


def _route(logits, b_col):
    n = logits.shape[1]
    scores = jax.nn.sigmoid(logits)
    sel = scores + b_col
    io_g = lax.broadcasted_iota(I32, (GROUP_SIZE, n), 0)
    gs_rows = []
    for g in range(N_GROUPS):
        s = sel[g * GROUP_SIZE:(g + 1) * GROUP_SIZE]
        m1 = jnp.max(s, axis=0, keepdims=True)
        i1 = _first_index_of(s == m1, io_g, GROUP_SIZE)
        m2 = jnp.max(jnp.where(io_g == i1, PICKED, s), axis=0, keepdims=True)
        gs_rows.append(m1 + m2)
    gs = jnp.concatenate(gs_rows, axis=0)
    io_n = lax.broadcasted_iota(I32, (N_GROUPS, n), 0)
    gsel = jnp.zeros((N_GROUPS, n), F32)
    for _ in range(TOPK_GROUPS):
        mg = jnp.max(gs, axis=0, keepdims=True)
        gi = _first_index_of(gs == mg, io_n, N_GROUPS)
        hit = io_n == gi
        gsel = jnp.where(hit, 1.0, gsel)
        gs = jnp.where(hit, PICKED, gs)
    cand = jnp.concatenate(
        [jnp.where(gsel[g:g + 1] > 0.5, sel[g * GROUP_SIZE:(g + 1) * GROUP_SIZE], NEG_INF)
         for g in range(N_GROUPS)], axis=0)
    io_e = lax.broadcasted_iota(I32, (N_EXPERTS, n), 0)
    picks, raw = [], []
    for _ in range(TOP_K):
        mv = jnp.max(cand, axis=0, keepdims=True)
        ei = _first_index_of(cand == mv, io_e, N_EXPERTS)
        hit = io_e == ei
        picks.append((hit, ei))
        raw.append(jnp.sum(jnp.where(hit, scores, 0.0), axis=0, keepdims=True))
        cand = jnp.where(hit, PICKED, cand)
    return picks, raw


def _post_mixer_kernel(xc_ref, xl_ref, mixc_ref, ya_ref, yb_ref, yc_ref, wout_ref, mod_ref, g_ref, b_ref,
                       wr_ref, br_ref,
                       x1_ref, h2_ref, eidx_ref, wsel_ref, rank_ref, cnt_ref):
    step = pl.program_id(0)
    tm = TM_TOK
    is_ctx = step < T_CTX // (POST_TILES * tm)

    @pl.when(step == 0)
    def _():
        cnt_ref[...] = jnp.zeros_like(cnt_ref)

    def project(s):
        rows = slice(s * tm, (s + 1) * tm)
        ci = _cond_row(step * POST_TILES + s, tm)
        gate1 = mod_ref[pl.ds(ci, 1), 2 * D_MODEL:3 * D_MODEL]
        sh2 = mod_ref[pl.ds(ci, 1), 3 * D_MODEL:4 * D_MODEL]
        sc2 = mod_ref[pl.ds(ci, 1), 4 * D_MODEL:5 * D_MODEL]
        mix_lat = jnp.concatenate([ya_ref[rows, :], yb_ref[rows, :], yc_ref[rows, :]], axis=-1)
        mix = jnp.where(is_ctx, mixc_ref[rows, :], mix_lat)
        y = _dot(mix, wout_ref[...])
        x = jnp.where(is_ctx, xc_ref[rows, :], xl_ref[rows, :])
        x1 = _layer_norm(ALPHA * x + gate1 * y, g_ref[...], b_ref[...])
        x1_ref[rows, :] = x1
        h2 = x1 * (1.0 + sc2) + sh2
        h_hi = h2.astype(BF16)
        h2_ref[rows, :] = h_hi
        h_lo = (h2 - h_hi.astype(F32)).astype(BF16)
        both = (_dot(h_hi, wr_ref[...]) + _dot(h_lo, wr_ref[...])).T
        return both[0:N_EXPERTS] + both[N_EXPERTS:2 * N_EXPERTS]

    def route(s, logits):
        toks = s * tm
        routed = [_route(logits[:, g * LANES:(g + 1) * LANES], br_ref[...]) for g in range(tm // LANES)]
        multi_g = []
        for picks, _ in routed:
            m = jnp.zeros((N_EXPERTS, LANES), F32)
            for hit, _ in picks:
                m = m + jnp.where(hit, 1.0, 0.0)
            multi_g.append(m)
        multi = jnp.concatenate(multi_g, axis=1)
        before = (lax.broadcasted_iota(I32, (tm, tm), 0) < lax.broadcasted_iota(I32, (tm, tm), 1))
        cum = _dot(multi.astype(BF16), jnp.where(before, 1.0, 0.0).astype(BF16))
        pad = jnp.zeros((SUBLANES - TOP_K, LANES), F32)
        for g, (picks, raw) in enumerate(routed):
            lanes = slice(toks + g * LANES, toks + (g + 1) * LANES)
            total = raw[0]
            for r in raw[1:]:
                total = total + r
            scale = ROUTE_SCALE / total
            cum_g = cum[:, g * LANES:(g + 1) * LANES]
            eidx_ref[:, lanes] = jnp.concatenate([ei for _, ei in picks] + [pad.astype(I32)], axis=0)
            wsel_ref[:, lanes] = jnp.concatenate([r * scale for r in raw] + [pad], axis=0)
            rank_ref[:, lanes] = jnp.concatenate(
                [jnp.sum(jnp.where(hit, cum_g, 0.0), axis=0, keepdims=True) for hit, _ in picks] + [pad],
                axis=0).astype(I32)
        tile_lane = lax.broadcasted_iota(I32, (N_EXPERTS, LANES), 1)
        cnt_ref[...] = jnp.where(tile_lane == step * POST_TILES + s,
                                 jnp.sum(multi, axis=1, keepdims=True), cnt_ref[...])

    logits = [project(s) for s in range(POST_TILES)]
    for s in range(POST_TILES):
        route(s, logits[s])


def _post_mixer(x_ctx, x_lat, mix_c, ya, yb, yc, w_out_bf16, mod, ln_g, ln_b, wr_split, b_router_col):
    tm = POST_TILES * TM_TOK
    n_ctx = T_CTX // tm
    ctx_map = lambda i: (jnp.minimum(i, n_ctx - 1), 0)
    lat_map = lambda i: (jnp.maximum(i - n_ctx, 0), 0)
    row_map = lambda i: (i, 0)
    const = lambda i: (0, 0)
    tok_map = lambda i: (0, i)
    return pl.pallas_call(
        _post_mixer_kernel,
        out_shape=(
            jax.ShapeDtypeStruct((T_ALL, D_MODEL), F32),
            jax.ShapeDtypeStruct((T_ALL, D_MODEL), BF16),
            jax.ShapeDtypeStruct((SUBLANES, T_ALL), I32),
            jax.ShapeDtypeStruct((SUBLANES, T_ALL), F32),
            jax.ShapeDtypeStruct((SUBLANES, T_ALL), I32),
            jax.ShapeDtypeStruct((N_EXPERTS, LANES), F32),
        ),
        grid=(T_ALL // tm,),
        in_specs=[
            pl.BlockSpec((tm, D_MODEL), ctx_map),
            pl.BlockSpec((tm, D_MODEL), lat_map),
            pl.BlockSpec((tm, D_MODEL), ctx_map),
            pl.BlockSpec((tm, W_A), lat_map),
            pl.BlockSpec((tm, W_B), lat_map),
            pl.BlockSpec((tm, W_C), lat_map),
            pl.BlockSpec((D_MODEL, D_MODEL), const),
            pl.BlockSpec((N_COND, 6 * D_MODEL), const),
            pl.BlockSpec((1, D_MODEL), const),
            pl.BlockSpec((1, D_MODEL), const),
            pl.BlockSpec((D_MODEL, 2 * N_EXPERTS), const),
            pl.BlockSpec((N_EXPERTS, 1), const),
        ],
        out_specs=(
            pl.BlockSpec((tm, D_MODEL), row_map),
            pl.BlockSpec((tm, D_MODEL), row_map),
            pl.BlockSpec((SUBLANES, tm), tok_map),
            pl.BlockSpec((SUBLANES, tm), tok_map),
            pl.BlockSpec((SUBLANES, tm), tok_map),
            pl.BlockSpec((N_EXPERTS, LANES), const),
        ),
        compiler_params=pltpu.CompilerParams(
            dimension_semantics=("arbitrary",), vmem_limit_bytes=VMEM_LIMIT),
        name="post_mixer",
    )(x_ctx, x_lat, mix_c, ya, yb, yc, w_out_bf16, mod, ln_g, ln_b, wr_split, b_router_col)


PLAN_TILES = 4


def _plan_kernel(eidx_ref, rank_ref, nmat_ref, lslot_ref, unit_ref, gend_ref):
    step = pl.program_id(0)
    units = jnp.floor((nmat_ref[...] + (UNIT - 1.0)) * (1.0 / UNIT))
    units_bf = units.astype(BF16)
    earlier_e = (lax.broadcasted_iota(I32, (N_EXPERTS, N_EXPERTS), 1)
                 < lax.broadcasted_iota(I32, (N_EXPERTS, N_EXPERTS), 0))
    tri_e = jnp.where(earlier_e, 1.0, 0.0).astype(BF16)
    earlier_t = (lax.broadcasted_iota(I32, (LANES, LANES), 0) < lax.broadcasted_iota(I32, (LANES, LANES), 1))
    tri_t = jnp.where(earlier_t, 1.0, 0.0).astype(BF16)
    local_off = _dot(tri_e, units_bf)
    tile_off = _dot(units_bf, tri_t)
    per_expert = jnp.sum(units, axis=1, keepdims=True)
    blocks = jnp.floor((per_expert + (UNITS_PER_BLOCK - 1.0)) * (1.0 / UNITS_PER_BLOCK))
    blocks_l = jnp.broadcast_to(blocks, (N_EXPERTS, LANES))
    start_blk = _dot(tri_e, blocks_l.astype(BF16))
    end_blk = start_blk + blocks_l
    gend_ref[...] = (end_blk * BM).astype(I32)

    tile_lane = lax.broadcasted_iota(I32, (N_EXPERTS, LANES), 1)
    u = lax.broadcasted_iota(I32, (N_EXPERTS, MAX_UNITS), 1).astype(F32)
    io_e = lax.broadcasted_iota(I32, (N_EXPERTS, TM_TOK), 0)
    for s in range(PLAN_TILES):
        i = step * PLAN_TILES + s
        this_tile = tile_lane == i

        def column(a):
            return jnp.sum(jnp.where(this_tile, a, 0.0), axis=1, keepdims=True)

        lo, n_u = column(local_off), column(units)
        base_unit = start_blk[:, 0:1] * UNITS_PER_BLOCK + column(tile_off) - lo
        inside = jnp.where(u >= lo, jnp.where(u < lo + n_u, 1.0, 0.0), 0.0)
        dst_unit = jnp.sum(inside * (base_unit + u), axis=0, keepdims=True)
        used = jnp.sum(inside, axis=0, keepdims=True) > 0.5
        spare = (SPARE_UNIT0 + (i % 2) * MAX_UNITS).astype(F32) + u[0:1, :]
        unit_ref[s] = jnp.where(used, dst_unit, spare).astype(I32)

        toks = slice(s * TM_TOK, (s + 1) * TM_TOK)
        rows = []
        for k in range(TOP_K):
            hit = io_e == eidx_ref[k:k + 1, toks]
            seg = jnp.sum(jnp.where(hit, lo * UNIT, 0.0), axis=0, keepdims=True)
            rows.append(seg.astype(I32) + rank_ref[k:k + 1, toks])
        rows.append(jnp.full((SUBLANES - TOP_K, TM_TOK), -1, I32))
        lslot_ref[:, toks] = jnp.concatenate(rows, axis=0)


def _slot_plan(eidx, rank, nmat):
    tok_map = lambda i: (0, i)
    const = lambda i: (0, 0)
    lslot, unit_tab, gend = pl.pallas_call(
        _plan_kernel,
        out_shape=(
            jax.ShapeDtypeStruct((SUBLANES, T_ALL), I32),
            jax.ShapeDtypeStruct((N_TOK_TILES, 1, MAX_UNITS), I32),
            jax.ShapeDtypeStruct((N_EXPERTS, LANES), I32),
        ),
        grid=(N_TOK_TILES // PLAN_TILES,),
        in_specs=[
            pl.BlockSpec((SUBLANES, PLAN_TILES * TM_TOK), tok_map),
            pl.BlockSpec((SUBLANES, PLAN_TILES * TM_TOK), tok_map),
            pl.BlockSpec((N_EXPERTS, LANES), const),
        ],
        out_specs=(
            pl.BlockSpec((SUBLANES, PLAN_TILES * TM_TOK), tok_map),
            pl.BlockSpec((PLAN_TILES, 1, MAX_UNITS), lambda i: (i, 0, 0)),
            pl.BlockSpec((N_EXPERTS, LANES), const),
        ),
        compiler_params=pltpu.CompilerParams(dimension_semantics=("arbitrary",)),
        name="slot_plan",
    )(eidx, rank, nmat)
    return lslot, unit_tab.reshape(N_TOK_TILES * MAX_UNITS), gend[:, 0]


PACK_W = D_MODEL // 2
HI_HALF = -65536


def _pack_pairs(x):
    lo = lax.bitcast_convert_type(x[:, 0:PACK_W], I32)
    hi = lax.bitcast_convert_type(x[:, PACK_W:D_MODEL], I32)
    return lax.shift_right_logical(lo, 16) | (hi & HI_HALF)


def _unpack_pairs(u):
    lo = lax.bitcast_convert_type(lax.shift_left(u, 16), F32).astype(BF16)
    hi = lax.bitcast_convert_type(u & HI_HALF, F32).astype(BF16)
    return lo, hi


def _unit_rows(unit):
    row = unit * UNIT
    return pl.ds(row if isinstance(unit, int) else pl.multiple_of(row, UNIT), UNIT)


def _unit_copy(src, src_unit, dst, dst_unit, sem):
    return pltpu.make_async_copy(src.at[_unit_rows(src_unit)], dst.at[_unit_rows(dst_unit)], sem)


def _dispatch_kernel(fresh, gend_ref, tab_ref, h2_ref, lslot_ref, *refs):
    xs_hbm, zero_ref, local_ref, sem_zero, sem_rows = refs if fresh else refs[1:]
    i = pl.program_id(0)
    buf = i % 2

    def drain(b):
        pltpu.make_async_copy(local_ref.at[b], xs_hbm.at[pl.ds(0, LOCAL_ROWS)], sem_rows.at[b]).wait()

    def has_rows(e):
        return gend_ref[e] > jnp.where(e == 0, 0, gend_ref[jnp.maximum(e - 1, 0)])

    def zero_copy(e):
        return pltpu.make_async_copy(
            zero_ref, xs_hbm.at[pl.ds(pl.multiple_of(gend_ref[e] - BM, BM), BM)], sem_zero)

    @pl.when(jnp.logical_and(i == 0, fresh))
    def _():
        zero_ref[...] = jnp.zeros_like(zero_ref)

        def start(e, c):
            @pl.when(has_rows(e))
            def _():
                zero_copy(e).start()
            return c

        def wait(e, c):
            @pl.when(has_rows(e))
            def _():
                zero_copy(e).wait()
            return c

        def tail_copy(blk):
            return pltpu.make_async_copy(
                zero_ref, xs_hbm.at[pl.ds(pl.multiple_of(blk * BM, BM), BM)], sem_zero)

        def start_tail(blk, c):
            tail_copy(blk).start()
            return c

        def wait_tail(blk, c):
            tail_copy(blk).wait()
            return c

        n_used = gend_ref[N_EXPERTS - 1] // BM
        lax.fori_loop(0, N_EXPERTS, start, 0)
        lax.fori_loop(n_used, N_BLOCKS_ALL, start_tail, 0)
        lax.fori_loop(0, N_EXPERTS, wait, 0)
        lax.fori_loop(n_used, N_BLOCKS_ALL, wait_tail, 0)

    @pl.when(i >= 2)
    def _():
        drain(buf)

    h2 = h2_ref[...]
    units_per_chunk = PERM_CHUNK // UNIT
    local = local_ref.at[buf]
    for c in range(LOCAL_ROWS // PERM_CHUNK):
        slot = c * PERM_CHUNK + lax.broadcasted_iota(I32, (PERM_CHUNK, TM_TOK), 0)
        p = jnp.zeros((PERM_CHUNK, TM_TOK), F32)
        for k in range(TOP_K):
            p = jnp.where(slot == lslot_ref[k:k + 1, :], 1.0, p)
        local[c * PERM_CHUNK:(c + 1) * PERM_CHUNK, :] = _pack_pairs(_dot(p.astype(BF16), h2))
        for u in range(c * units_per_chunk, (c + 1) * units_per_chunk):
            _unit_copy(local, u, xs_hbm, tab_ref[i * MAX_UNITS + u], sem_rows.at[buf]).start()

    @pl.when(i == N_TOK_TILES - 1)
    def _():
        drain(1 - buf)
        drain(buf)


def _dispatch(h2, lslot, unit_tab, gend, xs_prev):
    fresh = xs_prev is None
    n_prefetch = 2
    return pl.pallas_call(
        functools.partial(_dispatch_kernel, fresh),
        out_shape=jax.ShapeDtypeStruct((N_SLOTS, PACK_W), I32),
        input_output_aliases={} if fresh else {n_prefetch + 2: 0},
        grid_spec=pltpu.PrefetchScalarGridSpec(
            num_scalar_prefetch=n_prefetch,
            grid=(N_TOK_TILES,),
            in_specs=[
                pl.BlockSpec((TM_TOK, D_MODEL), lambda i, ge, tab: (i, 0)),
                pl.BlockSpec((SUBLANES, TM_TOK), lambda i, ge, tab: (0, i)),
            ] + ([] if fresh else [pl.BlockSpec(memory_space=pl.ANY)]),
            out_specs=pl.BlockSpec(memory_space=pl.ANY),
            scratch_shapes=[
                pltpu.VMEM((BM, PACK_W), I32),
                pltpu.VMEM((2, LOCAL_ROWS, PACK_W), I32),
                pltpu.SemaphoreType.DMA,
                pltpu.SemaphoreType.DMA((2,)),
            ],
        ),
        compiler_params=pltpu.CompilerParams(
            dimension_semantics=("arbitrary",), vmem_limit_bytes=VMEM_LIMIT),
        name="moe_dispatch",
    )(gend, unit_tab, h2, lslot, *(() if fresh else (xs_prev,)))


def _expert_kernel(fresh, gend_ref, xs_hbm, wgu_ref, wdown_ref, *refs):
    ys_hbm, xbuf, ybuf, wgu_bf, wdown_bf, sem_in, sem_out = refs if fresh else refs[1:]
    e = pl.program_id(0)
    first = jnp.where(e == 0, 0, gend_ref[jnp.maximum(e - 1, 0)]) // BM
    last = gend_ref[e] // BM
    n_used = gend_ref[N_EXPERTS - 1] // BM

    def rows_of(blk):
        return pl.ds(pl.multiple_of(blk * BM, BM), BM)

    def in_copy(blk):
        return pltpu.make_async_copy(xs_hbm.at[rows_of(blk)], xbuf.at[blk % X_BUFS], sem_in.at[blk % X_BUFS])

    def out_copy(blk):
        return pltpu.make_async_copy(ybuf.at[blk % 2], ys_hbm.at[rows_of(blk)], sem_out.at[blk % 2])

    @pl.when(e == 0)
    def _():
        for a in range(X_AHEAD):
            @pl.when(a < n_used)
            def _(a=a):
                in_copy(a).start()

    @pl.when(last > first)
    def _():
        wgu_bf[...] = wgu_ref[0, 0].astype(BF16)
        wdown_bf[...] = wdown_ref[0, 0].astype(BF16)

    def block(blk, carry):
        @pl.when(blk + X_AHEAD < n_used)
        def _():
            in_copy(blk + X_AHEAD).start()

        in_copy(blk).wait()

        @pl.when(blk >= 2)
        def _():
            out_copy(blk - 2).wait()

        x_lo, x_hi = _unpack_pairs(xbuf[blk % X_BUFS])
        gu = _dot(x_lo, wgu_bf[0:PACK_W, :]) + _dot(x_hi, wgu_bf[PACK_W:D_MODEL, :])
        act = _silu(gu[:, 0:D_EXPERT]) * gu[:, D_EXPERT:2 * D_EXPERT]
        y = _dot(act.astype(BF16), wdown_bf[...])
        ybuf[blk % 2] = _pack_pairs(y.astype(BF16).astype(F32))
        out_copy(blk).start()
        return carry

    lax.fori_loop(first, last, block, 0)

    @pl.when(e == N_EXPERTS - 1)
    def _():
        @pl.when(n_used >= 2)
        def _():
            out_copy(n_used - 2).wait()

        @pl.when(n_used >= 1)
        def _():
            out_copy(n_used - 1).wait()

        if fresh:
            ybuf[0] = jnp.zeros((BM, PACK_W), I32)

            def tail_copy(blk):
                return pltpu.make_async_copy(ybuf.at[0], ys_hbm.at[rows_of(blk)], sem_out.at[0])

            def start(blk, c):
                tail_copy(blk).start()
                return c

            def wait(blk, c):
                tail_copy(blk).wait()
                return c

            lax.fori_loop(n_used, N_BLOCKS_ALL, start, 0)
            lax.fori_loop(n_used, N_BLOCKS_ALL, wait, 0)


def _experts(xs, w_gu, w_down, layer, gend, ys_prev):
    fresh = ys_prev is None
    n_prefetch = 1
    return pl.pallas_call(
        functools.partial(_expert_kernel, fresh),
        out_shape=jax.ShapeDtypeStruct((N_SLOTS, PACK_W), I32),
        input_output_aliases={} if fresh else {n_prefetch + 3: 0},
        grid_spec=pltpu.PrefetchScalarGridSpec(
            num_scalar_prefetch=n_prefetch,
            grid=(N_EXPERTS,),
            in_specs=[
                pl.BlockSpec(memory_space=pl.ANY),
                pl.BlockSpec((1, 1, D_MODEL, 2 * D_EXPERT), lambda e, ge: (layer, e, 0, 0)),
                pl.BlockSpec((1, 1, D_EXPERT, D_MODEL), lambda e, ge: (layer, e, 0, 0)),
            ] + ([] if fresh else [pl.BlockSpec(memory_space=pl.ANY)]),
            out_specs=pl.BlockSpec(memory_space=pl.ANY),
            scratch_shapes=[
                pltpu.VMEM((X_BUFS, BM, PACK_W), I32),
                pltpu.VMEM((2, BM, PACK_W), I32),
                pltpu.VMEM((D_MODEL, 2 * D_EXPERT), BF16),
                pltpu.VMEM((D_EXPERT, D_MODEL), BF16),
                pltpu.SemaphoreType.DMA((X_BUFS,)),
                pltpu.SemaphoreType.DMA((2,)),
            ],
        ),
        compiler_params=pltpu.CompilerParams(
            dimension_semantics=("arbitrary",), vmem_limit_bytes=VMEM_LIMIT),
        name="moe_experts",
    )(gend, xs, w_gu, w_down, *(() if fresh else (ys_prev,)))


def _combine_kernel(tab_ref, x1_ref, h2_ref, lslot_ref, wsel_ref, ys_hbm, wsgu_ref, wsdown_ref, mod_ref,
                    g_ref, b_ref, outc_ref, outl_ref, local_ref, sel_ref, ylo_ref, yhi_ref, sem_rows):
    i = pl.program_id(0)
    tm = TM_TOK
    buf = i % 2

    def fetch_unit(tile, b, u):
        _unit_copy(ys_hbm, tab_ref[tile * MAX_UNITS + u], local_ref.at[b], u, sem_rows.at[b]).start()

    def drain(b):
        pltpu.make_async_copy(ys_hbm.at[pl.ds(0, LOCAL_ROWS)], local_ref.at[b], sem_rows.at[b]).wait()

    @pl.when(i == 0)
    def _():
        def body(u, c):
            fetch_unit(0, 0, u)
            return c

        lax.fori_loop(0, MAX_UNITS, body, 0, unroll=8)

    sgu = _dot(h2_ref[...], wsgu_ref[...])
    act = _silu(sgu[:, 0:D_SHARED]) * sgu[:, D_SHARED:2 * D_SHARED]
    f = _dot(act.astype(BF16), wsdown_ref[...])

    nxt = jnp.minimum(i + 1, N_TOK_TILES - 1)
    n_groups = tm // SEL_ROWS
    units_per_group = MAX_UNITS // n_groups
    slot = lax.broadcasted_iota(I32, (SEL_ROWS, LOCAL_ROWS), 1)
    for g in range(n_groups):
        rows = slice(g * SEL_ROWS, (g + 1) * SEL_ROWS)
        sel = jnp.zeros((SEL_ROWS, LOCAL_ROWS), F32)
        for k in range(TOP_K):
            sel = jnp.where(slot == lslot_ref[rows, k:k + 1], wsel_ref[rows, k:k + 1], sel)
        sel_ref[rows, :] = sel.astype(BF16)
        for u in range(g * units_per_group, (g + 1) * units_per_group):
            fetch_unit(nxt, 1 - buf, u)

    local = local_ref.at[buf]
    drain(buf)
    for c in range(LOCAL_ROWS // PERM_CHUNK):
        rows = slice(c * PERM_CHUNK, (c + 1) * PERM_CHUNK)
        ylo_ref[rows, :], yhi_ref[rows, :] = _unpack_pairs(local[rows, :])
    sel = sel_ref[...]
    f = f + jnp.concatenate([_dot(sel, ylo_ref[...]), _dot(sel, yhi_ref[...])], axis=-1)
    ci = _cond_row(i, tm)
    gate2 = mod_ref[pl.ds(ci, 1), 5 * D_MODEL:6 * D_MODEL]
    out = _layer_norm(ALPHA * x1_ref[...] + gate2 * f, g_ref[...], b_ref[...])

    @pl.when(i < T_CTX // tm)
    def _():
        outc_ref[...] = out

    @pl.when(i >= T_CTX // tm)
    def _():
        outl_ref[...] = out

    @pl.when(i == N_TOK_TILES - 1)
    def _():
        drain(1 - buf)


def _combine(x1, h2, lslot_rows, wsel_rows, unit_tab, ys, w_sgu_bf16, w_sdown_bf16, mod, ln_g, ln_b):
    n_ctx = T_CTX // TM_TOK
    row_map = lambda i, tab: (i, 0)
    const = lambda i, tab: (0, 0)
    return pl.pallas_call(
        _combine_kernel,
        out_shape=(jax.ShapeDtypeStruct((T_CTX, D_MODEL), F32),
                   jax.ShapeDtypeStruct((T_LAT, D_MODEL), F32)),
        grid_spec=pltpu.PrefetchScalarGridSpec(
            num_scalar_prefetch=1,
            grid=(N_TOK_TILES,),
            in_specs=[
                pl.BlockSpec((TM_TOK, D_MODEL), row_map),
                pl.BlockSpec((TM_TOK, D_MODEL), row_map),
                pl.BlockSpec((TM_TOK, SUBLANES), row_map),
                pl.BlockSpec((TM_TOK, SUBLANES), row_map),
                pl.BlockSpec(memory_space=pl.ANY),
                pl.BlockSpec((D_MODEL, 2 * D_SHARED), const),
                pl.BlockSpec((D_SHARED, D_MODEL), const),
                pl.BlockSpec((N_COND, 6 * D_MODEL), const),
                pl.BlockSpec((1, D_MODEL), const),
                pl.BlockSpec((1, D_MODEL), const),
            ],
            out_specs=(pl.BlockSpec((TM_TOK, D_MODEL), lambda i, tab: (jnp.minimum(i, n_ctx - 1), 0)),
                       pl.BlockSpec((TM_TOK, D_MODEL), lambda i, tab: (jnp.maximum(i - n_ctx, 0), 0))),
            scratch_shapes=[
                pltpu.VMEM((2, LOCAL_ROWS, PACK_W), I32),
                pltpu.VMEM((TM_TOK, LOCAL_ROWS), BF16),
                pltpu.VMEM((LOCAL_ROWS, PACK_W), BF16),
                pltpu.VMEM((LOCAL_ROWS, PACK_W), BF16),
                pltpu.SemaphoreType.DMA((2,)),
            ],
        ),
        compiler_params=pltpu.CompilerParams(
            dimension_semantics=("arbitrary",), vmem_limit_bytes=VMEM_LIMIT),
        name="moe_combine",
    )(unit_tab, x1, h2, lslot_rows, wsel_rows, ys, w_sgu_bf16, w_sdown_bf16, mod, ln_g, ln_b)


def _lane_rows(v, width):
    return jnp.broadcast_to(v.astype(F32)[:, None, None], (v.shape[0], 1, width))


def kernel(x_prompt, x_sample, state_ret_fwd, state_ret_bwd, cache_win_k, cache_win_v, cache_na_k, cache_na_v, c, c_ctx, w_in, w_out, ret_decay_fwd, ret_decay_bwd, ret_gn_g, ret_gn_b, win_sink, na_rpb, w_mod, b_mod, ln1_g, ln1_b, ln2_g, ln2_b, w_router, b_router, w_expert_gu, w_expert_down, w_shared_gu, w_shared_down):
    cond = jnp.concatenate(
        [c_ctx[None, :], c, jnp.zeros((N_COND - 1 - DEC_BATCH, D_MODEL), F32)], axis=0)
    mod_all = _modulation(cond, w_mod, b_mod)
    cos_t, sin_t = _rope_tables()

    x_ctx = x_prompt.reshape(T_CTX, D_MODEL)
    x_lat = x_sample.reshape(T_LAT, D_MODEL)
    sf_l, sb_l, caches = [], [], ()
    xs = ys = None
    for l in range(DEPTH):
        mod = mod_all[l]
        pa, pb, pc, *caches = _in_projection(x_ctx, x_lat, mod, w_in[l].astype(BF16), tuple(caches))
        decf_s, decb_s = _lane_rows(ret_decay_fwd[l], SEQ), _lane_rows(ret_decay_bwd[l], SEQ)
        gng, gnb = ret_gn_g[l][None, :], ret_gn_b[l][None, :]
        mix_c, st_f, st_b = _ctx_mixers(pa, pb, pc, win_sink[l], decf_s, decb_s, gng, gnb)
        sf_l.append(st_f)
        sb_l.append(st_b)
        ya = _lat_retention(pa, state_ret_fwd, state_ret_bwd, l,
                            _lane_rows(ret_decay_fwd[l], RET_CHUNK), _lane_rows(ret_decay_bwd[l], RET_CHUNK),
                            gng, gnb)
        yb = _lat_window_attn(pb, cache_win_k, cache_win_v, l, win_sink[l], cos_t, sin_t)
        yc = _lat_na_attn(pc, cache_na_k, cache_na_v, l, _na_maskbias(na_rpb[l]))

        wr_hi = w_router[l].astype(BF16)
        wr_lo = (w_router[l] - wr_hi.astype(F32)).astype(BF16)
        x1, h2, eidx, wsel, rank, counts = _post_mixer(
            x_ctx, x_lat, mix_c, ya, yb, yc, w_out[l].astype(BF16), mod, ln1_g[l][None, :],
            ln1_b[l][None, :], jnp.concatenate([wr_hi, wr_lo], axis=1), b_router[l][:, None])
        lslot, unit_tab, gend = _slot_plan(eidx, rank, counts)
        xs = _dispatch(h2, lslot, unit_tab, gend, xs)
        ys = _experts(xs, w_expert_gu, w_expert_down, l, gend, ys)
        x_ctx, x_lat = _combine(x1, h2, lslot.T, wsel.T, unit_tab, ys, w_shared_gu[l].astype(BF16),
                                w_shared_down[l].astype(BF16), mod, ln2_g[l][None, :], ln2_b[l][None, :])

    y_prompt = x_ctx.reshape(BATCH, SEQ, D_MODEL)
    y_sample = x_lat.reshape(DEC_BATCH, DEC_SEQ, D_MODEL)
    new_sf = jnp.stack(sf_l, axis=1)
    new_sb = jnp.stack(sb_l, axis=1)

    return (y_prompt, y_sample, new_sf, new_sb, *caches)
```

```python
import functools

import numpy as np
import jax
import jax.numpy as jnp
from jax import lax
from jax.experimental import pallas as pl
from jax.experimental.pallas import tpu as pltpu

F32 = jnp.float32
BF16 = jnp.bfloat16
I32 = jnp.int32

D_MODEL = 1024
BATCH = 32
SEQ = 256
DEPTH = 2
DEC_BATCH = 4
DEC_SEQ = 2048
PAST_LEN = 256
GRID_W = 64
HEAD_DIM = 64
ATTN_SCALE = HEAD_DIM ** -0.5
H_A = 4
W_A = H_A * HEAD_DIM
GN_EPS = 1e-5
H_B = 6
KV_B = 2
W_B = H_B * HEAD_DIM
KV_W_B = KV_B * HEAD_DIM
WINDOW = 128
ROPE_BASE = 10000.0
H_C = 6
W_C = H_C * HEAD_DIM
NA_ROWS = 8
NA_COLS = 16
IN_WIDTH = 4 * W_A + W_B + 2 * KV_W_B + 3 * W_C
N_EXPERTS = 64
TOP_K = 6
N_GROUPS = 8
GROUP_SIZE = N_EXPERTS // N_GROUPS
TOPK_GROUPS = 4
D_EXPERT = 256
D_SHARED = 256
ROUTE_SCALE = 2.5
ALPHA = (2 * DEPTH) ** 0.25
LN_EPS = 1e-5
NEG_INF = -1e30
PICKED = -3e38

T_CTX = BATCH * SEQ
T_LAT = DEC_BATCH * DEC_SEQ
T_ALL = T_CTX + T_LAT
N_COND = 8

PA_W = 4 * W_A
PB_W = W_B + 2 * KV_W_B
PC_W = 3 * W_C

LANES = 128
SUBLANES = 8
VMEM_LIMIT = 56 * 1024 * 1024

TM_PROJ = 512
TM_TOK = 256
RET_CHUNK = 256
POST_TILES = 4
CTX_STACK = 3
NA_STACK = 6
WIN_QB = 256
WIN_BAND = WIN_QB + 2 * WINDOW
NA_TILE_ROWS = 4
NA_KEY_ROWS = 11
BM = 512
X_AHEAD = 3
X_BUFS = X_AHEAD + 1
UNIT = SUBLANES
UNITS_PER_BLOCK = BM // UNIT
PERM_CHUNK = 256
SEL_ROWS = 16
N_TOK_TILES = T_ALL // TM_TOK
LOCAL_ROWS = -(-(TM_TOK * TOP_K + N_EXPERTS * (UNIT - 1)) // PERM_CHUNK) * PERM_CHUNK
MAX_UNITS = LOCAL_ROWS // UNIT
FIXED_UNITS = TM_TOK * TOP_K // UNIT
TAIL_GROUP = 8
N_ASSIGN = T_ALL * TOP_K
N_BLOCKS = -(-(N_ASSIGN + N_TOK_TILES * N_EXPERTS * (UNIT - 1) + N_EXPERTS * (BM - 1)) // BM)
SPARE_BLOCKS = 2 * -(-LOCAL_ROWS // BM)
SPARE_UNIT0 = N_BLOCKS * UNITS_PER_BLOCK
N_BLOCKS_ALL = N_BLOCKS + SPARE_BLOCKS
N_SLOTS = N_BLOCKS_ALL * BM


def _dot(a, b):
    return jnp.dot(a, b, preferred_element_type=F32)


def _dot_nt(a, b):
    return lax.dot_general(a, b, (((1,), (1,)), ((), ())), preferred_element_type=F32)


def _silu(x):
    return x * jax.nn.sigmoid(x)


def _log_sigmoid(x):
    return jnp.minimum(x, 0.0) - jnp.log(1.0 + jnp.exp(-jnp.abs(x)))


def _cond_row(tile, tile_rows):
    n_ctx = T_CTX // tile_rows
    per_lat = DEC_SEQ // tile_rows
    return jnp.where(tile < n_ctx, 0, 1 + (tile - n_ctx) // per_lat)


def _layer_norm(x, g, b):
    mu = jnp.mean(x, -1, keepdims=True)
    xc = x - mu
    var = jnp.mean(xc * xc, -1, keepdims=True)
    return xc * lax.rsqrt(var + LN_EPS) * g + b


MOD_TN = 1536


def _mod_kernel(cond_ref, w_ref, b_ref, o_ref):
    s = _silu(cond_ref[...])
    s_hi = s.astype(BF16)
    s_lo = (s - s_hi.astype(F32)).astype(BF16)
    w = w_ref[0]
    w_hi = w.astype(BF16)
    w_lo = (w - w_hi.astype(F32)).astype(BF16)
    o_ref[0] = _dot(s_hi, w_hi) + _dot(s_lo, w_hi) + _dot(s_hi, w_lo) + b_ref[0]


def _modulation(cond, w_mod, b_mod):
    n_out = 6 * D_MODEL
    return pl.pallas_call(
        _mod_kernel,
        out_shape=jax.ShapeDtypeStruct((DEPTH, N_COND, n_out), F32),
        grid=(DEPTH, n_out // MOD_TN),
        in_specs=[
            pl.BlockSpec((N_COND, D_MODEL), lambda l, j: (0, 0)),
            pl.BlockSpec((1, D_MODEL, MOD_TN), lambda l, j: (l, 0, j)),
            pl.BlockSpec((1, 1, MOD_TN), lambda l, j: (l, 0, j)),
        ],
        out_specs=pl.BlockSpec((1, N_COND, MOD_TN), lambda l, j: (l, 0, j)),
        compiler_params=pltpu.CompilerParams(
            dimension_semantics=("arbitrary", "arbitrary"), vmem_limit_bytes=VMEM_LIMIT),
        name="modulation",
    )(cond, w_mod, b_mod.reshape(DEPTH, 1, n_out))


SEQ_PER_PROJ = TM_PROJ // SEQ


def _inproj_kernel(n_prev, xc_ref, xl_ref, mod_ref, w_ref, *refs):
    prev_refs = refs[:4] if n_prev else ()
    pa_ref, pb_ref, pc_ref, wk_ref, wv_ref, nk_ref, nv_ref = refs[len(prev_refs):]
    i = pl.program_id(0)
    is_ctx = i < T_CTX // TM_PROJ
    targets = ((wk_ref, PA_W + W_B, KV_B), (wv_ref, PA_W + W_B + KV_W_B, KV_B),
               (nk_ref, PA_W + PB_W + W_C, H_C), (nv_ref, PA_W + PB_W + 2 * W_C, H_C))

    def project(x_ref, ci, with_cache):
        sh = mod_ref[pl.ds(ci, 1), 0:D_MODEL]
        sc = mod_ref[pl.ds(ci, 1), D_MODEL:2 * D_MODEL]
        for s in range(SEQ_PER_PROJ):
            rows = slice(s * SEQ, (s + 1) * SEQ)
            h = x_ref[rows, :] * (1.0 + sc) + sh
            p = _dot(h.astype(BF16), w_ref[...])
            pa_ref[rows, :] = p[:, 0:PA_W].astype(BF16)
            pb_ref[rows, :] = p[:, PA_W:PA_W + PB_W].astype(BF16)
            pc_ref[rows, :] = p[:, PA_W + PB_W:IN_WIDTH].astype(BF16)
            if with_cache:
                for j, (ref, col0, n_heads) in enumerate(targets):
                    if n_prev:
                        ref[s, 0:n_prev] = prev_refs[j][s]
                    for hh in range(n_heads):
                        ref[s, n_prev, hh] = p[:, col0 + hh * HEAD_DIM:col0 + (hh + 1) * HEAD_DIM]

    @pl.when(is_ctx)
    def _():
        project(xc_ref, 0, True)

    @pl.when(jnp.logical_not(is_ctx))
    def _():
        project(xl_ref, _cond_row(i, TM_PROJ), False)


def _in_projection(x_ctx, x_lat, mod, w_in_bf16, earlier):
    n_ctx_tiles = T_CTX // TM_PROJ
    n_prev = earlier[0].shape[1] if earlier else 0

    def cache_spec(n_layers, n_heads):
        return pl.BlockSpec((SEQ_PER_PROJ, n_layers, n_heads, SEQ, HEAD_DIM),
                            lambda i: (jnp.minimum(i, n_ctx_tiles - 1), 0, 0, 0, 0))

    cache_heads = (KV_B, KV_B, H_C, H_C)
    return pl.pallas_call(
        functools.partial(_inproj_kernel, n_prev),
        out_shape=(
            jax.ShapeDtypeStruct((T_ALL, PA_W), BF16),
            jax.ShapeDtypeStruct((T_ALL, PB_W), BF16),
            jax.ShapeDtypeStruct((T_ALL, PC_W), BF16),
        ) + tuple(jax.ShapeDtypeStruct((BATCH, n_prev + 1, nh, SEQ, HEAD_DIM), F32) for nh in cache_heads),
        grid=(T_ALL // TM_PROJ,),
        in_specs=[
            pl.BlockSpec((TM_PROJ, D_MODEL), lambda i: (jnp.minimum(i, n_ctx_tiles - 1), 0)),
            pl.BlockSpec((TM_PROJ, D_MODEL), lambda i: (jnp.maximum(i - n_ctx_tiles, 0), 0)),
            pl.BlockSpec((N_COND, 6 * D_MODEL), lambda i: (0, 0)),
            pl.BlockSpec((D_MODEL, IN_WIDTH), lambda i: (0, 0)),
        ] + [cache_spec(n_prev, nh) for nh in cache_heads if n_prev],
        out_specs=(
            pl.BlockSpec((TM_PROJ, PA_W), lambda i: (i, 0)),
            pl.BlockSpec((TM_PROJ, PB_W), lambda i: (i, 0)),
            pl.BlockSpec((TM_PROJ, PC_W), lambda i: (i, 0)),
        ) + tuple(cache_spec(n_prev + 1, nh) for nh in cache_heads),
        compiler_params=pltpu.CompilerParams(
            dimension_semantics=("arbitrary",), vmem_limit_bytes=VMEM_LIMIT),
        name="in_projection",
    )(x_ctx, x_lat, mod, w_in_bf16, *earlier)


def _decay_matrix(lg_f, lg_b, n):
    row = lax.broadcasted_iota(I32, (n, n), 0)
    col = lax.broadcasted_iota(I32, (n, n), 1)
    diff = (row - col).astype(F32)
    fwd = jnp.where(diff >= 0, jnp.exp(lg_f * jnp.maximum(diff, 0.0)), 0.0)
    bwd = jnp.where(diff <= 0, jnp.exp(lg_b * jnp.maximum(-diff, 0.0)), 0.0)
    return (fwd + bwd) * ATTN_SCALE


def _stacked_softmax_attend(scores, values, extra_logit=None):
    rows = scores[0].shape[0]
    s = jnp.concatenate(scores, axis=0)
    m = jnp.max(s, -1, keepdims=True)
    if extra_logit is not None:
        m = jnp.maximum(m, extra_logit)
    p = jnp.exp(s - m)
    den = jnp.sum(p, -1, keepdims=True)
    if extra_logit is not None:
        den = den + jnp.exp(extra_logit - m)
    p = p.astype(BF16)
    return [_dot(p[h * rows:(h + 1) * rows], v) / den[h * rows:(h + 1) * rows]
            for h, v in enumerate(values)]


def _lane_xor_matrix(width, distance):
    r = lax.broadcasted_iota(I32, (width, width), 0)
    c = lax.broadcasted_iota(I32, (width, width), 1)
    return jnp.where((r ^ distance) == c, 1.0, 0.0).astype(BF16)


def _ctx_mixer_kernel(sink_ref, pa_ref, pb_ref, pc_ref, decf_ref, decb_ref, gng_ref, gnb_ref,
                      mix_ref, sf_ref, sb_ref, dmat_ref, zf_ref, zb_ref):
    n = SEQ
    hd = HEAD_DIM

    @pl.when(pl.program_id(0) == 0)
    def _():
        pos = lax.broadcasted_iota(I32, (n, hd), 0).astype(F32)
        zf, zb = [], []
        for h in range(H_A):
            lg_f = _log_sigmoid(decf_ref[h])
            lg_b = _log_sigmoid(decb_ref[h])
            dmat_ref[h] = _decay_matrix(lg_f, lg_b, n)
            zf.append(jnp.exp(lg_f[:, 0:hd] * (n - 1.0 - pos)) * ATTN_SCALE)
            zb.append(jnp.exp(lg_b[:, 0:hd] * pos) * ATTN_SCALE)
        for p in range(H_A // 2):
            zf_ref[p] = jnp.concatenate(zf[2 * p:2 * p + 2], axis=1)
            zb_ref[p] = jnp.concatenate(zb[2 * p:2 * p + 2], axis=1)

    low = lax.broadcasted_iota(I32, (n, LANES), 1) < hd

    def own_half(x, h):
        zero = jnp.zeros_like(x)
        return jnp.where(low, x, zero) if h % 2 == 0 else jnp.where(low, zero, x)

    def merge(first, second):
        return jnp.where(low, first, second)

    def pair_cols(ref, base, p):
        return ref[:, base + p * LANES:base + (p + 1) * LANES]

    def head_mean(x):
        first = jnp.sum(jnp.where(low, x, 0.0), -1, keepdims=True)
        second = jnp.sum(jnp.where(low, 0.0, x), -1, keepdims=True)
        return merge(first, second) * (1.0 / hd)

    for p in range(H_A // 2):
        q_pair, k_pair = pair_cols(pa_ref, 0, p), pair_cols(pa_ref, W_A, p)
        v_pair, gate_pair = pair_cols(pa_ref, 2 * W_A, p), pair_cols(pa_ref, 3 * W_A, p)
        outs = []
        for h in (2 * p, 2 * p + 1):
            a = _dot_nt(own_half(q_pair, h), k_pair)
            outs.append(_dot((a * dmat_ref[h]).astype(BF16), v_pair))
        o = merge(outs[0], outs[1])
        kf = k_pair.astype(F32)
        for st_ref, z_ref in ((sf_ref, zf_ref), (sb_ref, zb_ref)):
            st = _dot((kf * z_ref[p]).T.astype(BF16), v_pair)
            st_ref[0, 2 * p] = st[0:hd, 0:hd]
            st_ref[0, 2 * p + 1] = st[hd:2 * hd, hd:2 * hd]
        mu = head_mean(o)
        oc = o - mu
        var = head_mean(oc * oc)
        on = oc * lax.rsqrt(var + GN_EPS) * pair_cols(gng_ref, 0, p) + pair_cols(gnb_ref, 0, p)
        mix_ref[:, p * LANES:(p + 1) * LANES] = (on * _silu(gate_pair.astype(F32))).astype(BF16)

    group = H_B // KV_B
    swap = _lane_xor_matrix(LANES, hd)
    kv_k = pb_ref[:, W_B:W_B + KV_W_B]
    kv_v = pb_ref[:, W_B + KV_W_B:W_B + 2 * KV_W_B]
    scores = []
    for hh in range(H_B):
        q = own_half(pair_cols(pb_ref, 0, hh // 2), hh) * ATTN_SCALE
        if hh % 2 != hh // group:
            q = _dot(q, swap).astype(BF16)
        scores.append(_dot_nt(q, kv_k))
    sinks = jnp.concatenate([jnp.full((n, 1), sink_ref[hh], F32) for hh in range(H_B)], axis=0)
    outs = []
    stacked = []
    for g0 in range(0, H_B, CTX_STACK):
        stacked += _stacked_softmax_attend(scores[g0:g0 + CTX_STACK], [kv_v] * CTX_STACK,
                                           sinks[g0 * n:(g0 + CTX_STACK) * n])
    for hh, o in enumerate(stacked):
        o = o.astype(BF16)
        outs.append(_dot(o, swap).astype(BF16) if hh % 2 != hh // group else o)
    for p in range(H_B // 2):
        mix_ref[:, W_A + p * LANES:W_A + (p + 1) * LANES] = merge(outs[2 * p], outs[2 * p + 1])

    scores, values = [], []
    for hh in range(H_C):
        q = own_half(pair_cols(pc_ref, 0, hh // 2), hh) * ATTN_SCALE
        scores.append(_dot_nt(q, pair_cols(pc_ref, W_C, hh // 2)))
        values.append(pair_cols(pc_ref, 2 * W_C, hh // 2))
    outs = []
    for g0 in range(0, H_C, CTX_STACK):
        outs += _stacked_softmax_attend(scores[g0:g0 + CTX_STACK], values[g0:g0 + CTX_STACK])
    for p in range(H_C // 2):
        mix_ref[:, W_A + W_B + p * LANES:W_A + W_B + (p + 1) * LANES] = merge(
            outs[2 * p], outs[2 * p + 1]).astype(BF16)


def _ctx_mixers(pa, pb, pc, sink, decf, decb, gng, gnb):
    return pl.pallas_call(
        _ctx_mixer_kernel,
        out_shape=(
            jax.ShapeDtypeStruct((T_CTX, D_MODEL), BF16),
            jax.ShapeDtypeStruct((BATCH, H_A, HEAD_DIM, HEAD_DIM), F32),
            jax.ShapeDtypeStruct((BATCH, H_A, HEAD_DIM, HEAD_DIM), F32),
        ),
        grid=(BATCH,),
        in_specs=[
            pl.BlockSpec(memory_space=pltpu.SMEM),
            pl.BlockSpec((SEQ, PA_W), lambda b: (b, 0)),
            pl.BlockSpec((SEQ, PB_W), lambda b: (b, 0)),
            pl.BlockSpec((SEQ, PC_W), lambda b: (b, 0)),
            pl.BlockSpec((H_A, 1, SEQ), lambda b: (0, 0, 0)),
            pl.BlockSpec((H_A, 1, SEQ), lambda b: (0, 0, 0)),
            pl.BlockSpec((1, W_A), lambda b: (0, 0)),
            pl.BlockSpec((1, W_A), lambda b: (0, 0)),
        ],
        out_specs=(
            pl.BlockSpec((SEQ, D_MODEL), lambda b: (b, 0)),
            pl.BlockSpec((1, H_A, HEAD_DIM, HEAD_DIM), lambda b: (b, 0, 0, 0)),
            pl.BlockSpec((1, H_A, HEAD_DIM, HEAD_DIM), lambda b: (b, 0, 0, 0)),
        ),
        scratch_shapes=[
            pltpu.VMEM((H_A, SEQ, SEQ), F32),
            pltpu.VMEM((H_A // 2, SEQ, LANES), F32),
            pltpu.VMEM((H_A // 2, SEQ, LANES), F32),
        ],
        compiler_params=pltpu.CompilerParams(
            dimension_semantics=("arbitrary",), vmem_limit_bytes=VMEM_LIMIT),
        name="ctx_mixers",
    )(sink, pa, pb, pc, decf, decb, gng, gnb)


def _lat_ret_kernel(pa_ref, stf_ref, stb_ref, decf_ref, decb_ref, gng_ref, gnb_ref, ya_ref):
    c = RET_CHUNK
    hd = HEAD_DIM
    n_chunks = DEC_SEQ // c
    pos = lax.broadcasted_iota(I32, (c, LANES), 0).astype(F32)
    low = lax.broadcasted_iota(I32, (c, LANES), 1) < hd
    same_head = (lax.broadcasted_iota(I32, (LANES, LANES), 0) // hd
                 == lax.broadcasted_iota(I32, (LANES, LANES), 1) // hd)
    zero_blk = jnp.zeros((hd, hd), F32)

    def block_diag(a, b):
        return jnp.concatenate([jnp.concatenate([a, zero_blk], axis=1),
                                jnp.concatenate([zero_blk, b], axis=1)], axis=0)

    def head_mean(x):
        first = jnp.sum(jnp.where(low, x, 0.0), -1, keepdims=True)
        second = jnp.sum(jnp.where(low, 0.0, x), -1, keepdims=True)
        return jnp.where(low, first, second) * (1.0 / hd)

    for p in range(H_A // 2):
        lg_f = [_log_sigmoid(decf_ref[h]) for h in (2 * p, 2 * p + 1)]
        lg_b = [_log_sigmoid(decb_ref[h]) for h in (2 * p, 2 * p + 1)]
        dmat = [_decay_matrix(lg_f[t], lg_b[t], c) for t in range(2)]
        lf = jnp.where(low[0:1], lg_f[0][:, 0:LANES], lg_f[1][:, 0:LANES])
        lb = jnp.where(low[0:1], lg_b[0][:, 0:LANES], lg_b[1][:, 0:LANES])
        zf = jnp.exp(lf * (c - 1.0 - pos)) * ATTN_SCALE
        zb = jnp.exp(lb * pos) * ATTN_SCALE
        xf = jnp.exp(lf * (pos + 1.0))
        xb = jnp.exp(lb * (c - pos))
        gcf = jnp.exp(lf * float(c))
        gcb = jnp.exp(lb * float(c))
        pair = slice(p * LANES, (p + 1) * LANES)

        def chunk(i, base):
            return pa_ref[i * c:(i + 1) * c, base + p * LANES:base + (p + 1) * LANES]

        kv_f, kv_b = [], []
        for i in range(n_chunks):
            kf = chunk(i, W_A).astype(F32)
            v = chunk(i, 2 * W_A)
            kv_f.append(jnp.where(same_head, _dot((kf * zf).T.astype(BF16), v), 0.0))
            kv_b.append(jnp.where(same_head, _dot((kf * zb).T.astype(BF16), v), 0.0))

        s = block_diag(stf_ref[0, 0, 2 * p], stf_ref[0, 0, 2 * p + 1])
        seen_f = []
        for i in range(n_chunks):
            seen_f.append(s)
            s = gcf * s + kv_f[i]
        s = block_diag(stb_ref[0, 0, 2 * p], stb_ref[0, 0, 2 * p + 1])
        seen_b = [None] * n_chunks
        for i in reversed(range(n_chunks)):
            seen_b[i] = s
            s = gcb * s + kv_b[i]

        for i in range(n_chunks):
            q_pair, k_pair, v_pair = chunk(i, 0), chunk(i, W_A), chunk(i, 2 * W_A)
            inner = []
            for t in range(2):
                zero = jnp.zeros_like(q_pair)
                q = jnp.where(low, q_pair, zero) if t == 0 else jnp.where(low, zero, q_pair)
                inner.append(_dot((_dot_nt(q, k_pair) * dmat[t]).astype(BF16), v_pair))
            qf = q_pair.astype(F32)
            lhs = jnp.concatenate([(qf * xf).astype(BF16), (qf * xb).astype(BF16)], axis=1)
            rhs = jnp.concatenate([seen_f[i], seen_b[i]], axis=0).astype(BF16)
            o = jnp.where(low, inner[0], inner[1]) + _dot(lhs, rhs)
            mu = head_mean(o)
            oc = o - mu
            var = head_mean(oc * oc)
            on = oc * lax.rsqrt(var + GN_EPS) * gng_ref[:, pair] + gnb_ref[:, pair]
            y = on * _silu(chunk(i, 3 * W_A).astype(F32))
            ya_ref[i * c:(i + 1) * c, pair] = y.astype(BF16)


def _lat_retention(pa, st_f, st_b, layer, decf, decb, gng, gnb):
    lat0 = T_CTX // DEC_SEQ
    st_spec = pl.BlockSpec((1, 1, H_A, HEAD_DIM, HEAD_DIM), lambda b: (b, layer, 0, 0, 0))
    return pl.pallas_call(
        _lat_ret_kernel,
        out_shape=jax.ShapeDtypeStruct((T_LAT, W_A), BF16),
        grid=(DEC_BATCH,),
        in_specs=[
            pl.BlockSpec((DEC_SEQ, PA_W), lambda b: (lat0 + b, 0)),
            st_spec, st_spec,
            pl.BlockSpec((H_A, 1, RET_CHUNK), lambda b: (0, 0, 0)),
            pl.BlockSpec((H_A, 1, RET_CHUNK), lambda b: (0, 0, 0)),
            pl.BlockSpec((1, W_A), lambda b: (0, 0)),
            pl.BlockSpec((1, W_A), lambda b: (0, 0)),
        ],
        out_specs=pl.BlockSpec((DEC_SEQ, W_A), lambda b: (b, 0)),
        compiler_params=pltpu.CompilerParams(
            dimension_semantics=("arbitrary",), vmem_limit_bytes=VMEM_LIMIT),
        name="lat_retention",
    )(pa, st_f, st_b, decf, decb, gng, gnb)


def _rope(x, cos, sin_signed, swap):
    return x.astype(F32) * cos + _dot(x, swap) * sin_signed


def _lat_win_kernel(sink_ref, pq_ref, pseq_ref, kctx_ref, vctx_ref, cos_ref, sin_ref, yb_ref,
                    krope_ref, kc_ref, vc_ref, mask_ref):
    n = pl.program_id(1)
    hd = HEAD_DIM
    qb = WIN_QB
    band = WIN_BAND
    group = H_B // KV_B
    rot = _lane_xor_matrix(LANES, hd // 2)
    swap = _lane_xor_matrix(LANES, hd)

    @pl.when(n == 0)
    def _():
        k = pseq_ref[:, W_B:W_B + KV_W_B]
        krope_ref[...] = _rope(k, cos_ref[...], sin_ref[...], rot).astype(BF16)
        kc_ref[...] = jnp.concatenate([kctx_ref[0, 0, j] for j in range(KV_B)], axis=1).astype(BF16)
        vc_ref[...] = jnp.concatenate([vctx_ref[0, 0, j] for j in range(KV_B)], axis=1).astype(BF16)
        q_in_blk = lax.broadcasted_iota(I32, (group * qb, band), 0) % qb
        k_in_band = lax.broadcasted_iota(I32, (group * qb, band), 1)
        for ty, lead in enumerate((0, WINDOW, band - qb)):
            mask_ref[ty] = jnp.where(jnp.abs(k_in_band - lead - q_in_blk) <= WINDOW, 0.0, NEG_INF)

    q_rows = pl.ds(pl.multiple_of(n * qb, qb), qb)
    cos_q = cos_ref[q_rows, :]
    sin_q = sin_ref[q_rows, :]
    low = lax.broadcasted_iota(I32, (qb, LANES), 1) < hd
    q_heads = []
    for p in range(H_B // 2):
        q_pair = _rope(pq_ref[:, p * LANES:(p + 1) * LANES], cos_q, sin_q, rot) * ATTN_SCALE
        for hh in (2 * p, 2 * p + 1):
            q = jnp.where(low, q_pair, 0.0) if hh % 2 == 0 else jnp.where(low, 0.0, q_pair)
            q = q.astype(BF16)
            q_heads.append(_dot(q, swap).astype(BF16) if hh % 2 != hh // group else q)

    ws = jnp.clip(n * qb - WINDOW, 0, DEC_SEQ - band)
    k_rows = pl.ds(pl.multiple_of(ws, WINDOW), band)
    n_blk = DEC_SEQ // qb
    band_bias = mask_ref[jnp.where(n == 0, 0, jnp.where(n == n_blk - 1, 2, 1))]
    head_of_row = lax.broadcasted_iota(I32, (group * qb, 1), 0) // qb
    kw = krope_ref[k_rows, :]
    vw = pseq_ref[k_rows, W_B + KV_W_B:W_B + 2 * KV_W_B]
    outs = []
    for j in range(KV_B):
        heads = [j * group + g for g in range(group)]
        qs = jnp.concatenate([q_heads[hh] for hh in heads], axis=0)
        s_loc = _dot_nt(qs, kw) + band_bias
        s_ctx = _dot_nt(qs, kc_ref[...])
        sink = jnp.zeros((group * qb, 1), F32)
        for g, hh in enumerate(heads):
            sink = jnp.where(head_of_row == g, sink_ref[hh], sink)
        m = jnp.maximum(jnp.maximum(jnp.max(s_loc, -1, keepdims=True),
                                    jnp.max(s_ctx, -1, keepdims=True)), sink)
        p_loc = jnp.exp(s_loc - m)
        p_ctx = jnp.exp(s_ctx - m)
        den = (jnp.sum(p_loc, -1, keepdims=True) + jnp.sum(p_ctx, -1, keepdims=True)
               + jnp.exp(sink - m))
        o = ((_dot(p_loc.astype(BF16), vw) + _dot(p_ctx.astype(BF16), vc_ref[...])) / den).astype(BF16)
        for g, hh in enumerate(heads):
            o_h = o[g * qb:(g + 1) * qb]
            outs.append(_dot(o_h, swap).astype(BF16) if hh % 2 != j else o_h)
    for p in range(H_B // 2):
        yb_ref[:, p * LANES:(p + 1) * LANES] = jnp.where(low, outs[2 * p], outs[2 * p + 1])


def _lat_window_attn(pb, cache_k, cache_v, layer, sink, cos_t, sin_t):
    n_blk = DEC_SEQ // WIN_QB
    lat_blk0 = T_CTX // WIN_QB
    lat_seq0 = T_CTX // DEC_SEQ
    ctx_spec = pl.BlockSpec((1, 1, KV_B, PAST_LEN, HEAD_DIM), lambda b, n: (b, layer, 0, 0, 0))
    return pl.pallas_call(
        _lat_win_kernel,
        out_shape=jax.ShapeDtypeStruct((T_LAT, W_B), BF16),
        grid=(DEC_BATCH, n_blk),
        in_specs=[
            pl.BlockSpec(memory_space=pltpu.SMEM),
            pl.BlockSpec((WIN_QB, PB_W), lambda b, n: (lat_blk0 + b * n_blk + n, 0)),
            pl.BlockSpec((DEC_SEQ, PB_W), lambda b, n: (lat_seq0 + b, 0)),
            ctx_spec, ctx_spec,
            pl.BlockSpec((DEC_SEQ, LANES), lambda b, n: (0, 0)),
            pl.BlockSpec((DEC_SEQ, LANES), lambda b, n: (0, 0)),
        ],
        out_specs=pl.BlockSpec((WIN_QB, W_B), lambda b, n: (b * n_blk + n, 0)),
        scratch_shapes=[
            pltpu.VMEM((DEC_SEQ, KV_W_B), BF16),
            pltpu.VMEM((PAST_LEN, KV_W_B), BF16),
            pltpu.VMEM((PAST_LEN, KV_W_B), BF16),
            pltpu.VMEM((3, (H_B // KV_B) * WIN_QB, WIN_BAND), F32),
        ],
        compiler_params=pltpu.CompilerParams(
            dimension_semantics=("arbitrary", "arbitrary"), vmem_limit_bytes=VMEM_LIMIT),
        name="lat_window_attn",
    )(sink, pb, pb, cache_k, cache_v, cos_t, sin_t)


NA_Q = NA_TILE_ROWS * GRID_W
NA_K = NA_KEY_ROWS * GRID_W
NA_TILES = DEC_SEQ // NA_Q
LAT_ROWS = DEC_SEQ // GRID_W


def _na_window_start(tile):
    return jnp.clip(tile * NA_TILE_ROWS - NA_ROWS // 2, 0, LAT_ROWS - NA_KEY_ROWS)


def _lat_na_kernel(pq_ref, pseq_ref, kctx_ref, vctx_ref, bias_ref, yc_ref, kc_ref, vc_ref):
    t = pl.program_id(1)
    hd = HEAD_DIM
    nq = NA_Q

    @pl.when(t == 0)
    def _():
        for p in range(H_C // 2):
            kc_ref[p] = jnp.concatenate([kctx_ref[0, 0, 2 * p], kctx_ref[0, 0, 2 * p + 1]], axis=1).astype(BF16)
            vc_ref[p] = jnp.concatenate([vctx_ref[0, 0, 2 * p], vctx_ref[0, 0, 2 * p + 1]], axis=1).astype(BF16)

    k_rows = pl.ds(pl.multiple_of(_na_window_start(t) * GRID_W, GRID_W), NA_K)
    low = lax.broadcasted_iota(I32, (nq, LANES), 1) < hd
    loc, ctx = [], []
    for hh in range(H_C):
        p = hh // 2
        q_pair = pq_ref[:, p * LANES:(p + 1) * LANES] * ATTN_SCALE
        zero = jnp.zeros_like(q_pair)
        q = jnp.where(low, q_pair, zero) if hh % 2 == 0 else jnp.where(low, zero, q_pair)
        loc.append(_dot_nt(q, pseq_ref[k_rows, W_C + p * LANES:W_C + (p + 1) * LANES]) + bias_ref[0, hh])
        ctx.append(_dot_nt(q, kc_ref[p]))
    outs = []
    for g0 in range(0, H_C, NA_STACK):
        s_loc = jnp.concatenate(loc[g0:g0 + NA_STACK], axis=0)
        s_ctx = jnp.concatenate(ctx[g0:g0 + NA_STACK], axis=0)
        m = jnp.maximum(jnp.max(s_loc, -1, keepdims=True), jnp.max(s_ctx, -1, keepdims=True))
        p_loc = jnp.exp(s_loc - m)
        p_ctx = jnp.exp(s_ctx - m)
        den = jnp.sum(p_loc, -1, keepdims=True) + jnp.sum(p_ctx, -1, keepdims=True)
        p_loc = p_loc.astype(BF16)
        p_ctx = p_ctx.astype(BF16)
        for hh in range(g0, g0 + NA_STACK):
            p = hh // 2
            rows = slice((hh - g0) * nq, (hh - g0 + 1) * nq)
            vw = pseq_ref[k_rows, 2 * W_C + p * LANES:2 * W_C + (p + 1) * LANES]
            outs.append((_dot(p_loc[rows], vw) + _dot(p_ctx[rows], vc_ref[p])) / den[rows])
    for p in range(H_C // 2):
        yc_ref[:, p * LANES:(p + 1) * LANES] = jnp.where(low, outs[2 * p], outs[2 * p + 1]).astype(BF16)


def _na_tile_type(t):
    return jnp.where(t == 0, 0, jnp.where(t == NA_TILES - 1, 2, 1))


def _lat_na_attn(pc, cache_k, cache_v, layer, maskbias):
    lat_tile0 = T_CTX // NA_Q
    lat_seq0 = T_CTX // DEC_SEQ
    ctx_spec = pl.BlockSpec((1, 1, H_C, PAST_LEN, HEAD_DIM), lambda b, t: (b, layer, 0, 0, 0))
    return pl.pallas_call(
        _lat_na_kernel,
        out_shape=jax.ShapeDtypeStruct((T_LAT, W_C), BF16),
        grid=(DEC_BATCH, NA_TILES),
        in_specs=[
            pl.BlockSpec((NA_Q, PC_W), lambda b, t: (lat_tile0 + b * NA_TILES + t, 0)),
            pl.BlockSpec((DEC_SEQ, PC_W), lambda b, t: (lat_seq0 + b, 0)),
            ctx_spec, ctx_spec,
            pl.BlockSpec((1, H_C, NA_Q, NA_K), lambda b, t: (_na_tile_type(t), 0, 0, 0)),
        ],
        out_specs=pl.BlockSpec((NA_Q, W_C), lambda b, t: (b * NA_TILES + t, 0)),
        scratch_shapes=[
            pltpu.VMEM((H_C // 2, PAST_LEN, LANES), BF16),
            pltpu.VMEM((H_C // 2, PAST_LEN, LANES), BF16),
        ],
        compiler_params=pltpu.CompilerParams(
            dimension_semantics=("arbitrary", "arbitrary"), vmem_limit_bytes=VMEM_LIMIT),
        name="lat_na_attn",
    )(pc, pc, cache_k, cache_v, maskbias)


def _na_block_index():
    out = np.zeros((3, NA_TILE_ROWS, NA_KEY_ROWS), np.int32)
    for ty, tile in enumerate((0, 1, NA_TILES - 1)):
        r = tile * NA_TILE_ROWS
        ws = int(np.clip(r - NA_ROWS // 2, 0, LAT_ROWS - NA_KEY_ROWS))
        for qq in range(NA_TILE_ROWS):
            qr = r + qq
            r0 = int(np.clip(qr - NA_ROWS // 2, 0, LAT_ROWS - NA_ROWS))
            for kk in range(NA_KEY_ROWS):
                kr = ws + kk
                out[ty, qq, kk] = kr - qr + NA_ROWS - 1 if r0 <= kr < r0 + NA_ROWS else 2 * NA_ROWS - 1
    return out


def _na_maskbias(rpb):
    qc = np.arange(GRID_W)[:, None]
    kc = np.arange(GRID_W)[None, :]
    c0 = np.clip(qc - NA_COLS // 2, 0, GRID_W - NA_COLS)
    col_ok = (kc >= c0) & (kc < c0 + NA_COLS)
    ci = np.clip(kc - qc + NA_COLS - 1, 0, 2 * NA_COLS - 2)
    onehot = (ci[None] == np.arange(2 * NA_COLS - 1)[:, None, None]).astype(np.float32)
    cols = jnp.einsum("hab,bqk->haqk", rpb, jnp.asarray(onehot), precision=lax.Precision.HIGHEST)
    cols = jnp.where(jnp.asarray(col_ok)[None, None], cols, NEG_INF)
    cols = jnp.concatenate([cols, jnp.full((H_C, 1, GRID_W, GRID_W), NEG_INF, F32)], axis=1)
    block_index = _na_block_index()

    def assemble(cols_ref, out_ref):
        for ty in range(3):
            for qq in range(NA_TILE_ROWS):
                for kk in range(NA_KEY_ROWS):
                    out_ref[ty, 0, qq * GRID_W:(qq + 1) * GRID_W, kk * GRID_W:(kk + 1) * GRID_W] = (
                        cols_ref[0, int(block_index[ty, qq, kk])])

    return pl.pallas_call(
        assemble,
        out_shape=jax.ShapeDtypeStruct((3, H_C, NA_Q, NA_K), F32),
        grid=(H_C,),
        in_specs=[pl.BlockSpec((1, 2 * NA_ROWS, GRID_W, GRID_W), lambda h: (h, 0, 0, 0))],
        out_specs=pl.BlockSpec((3, 1, NA_Q, NA_K), lambda h: (0, h, 0, 0)),
        compiler_params=pltpu.CompilerParams(dimension_semantics=("arbitrary",)),
        name="na_bias_assemble",
    )(cols)


def _rope_tables():
    t = np.arange(DEC_SEQ)
    n_freq = HEAD_DIM // 4
    inv = (ROPE_BASE ** (-np.arange(n_freq, dtype=np.float32) / n_freq)).astype(np.float32)
    row = (t // GRID_W).astype(np.float32)[:, None] * inv
    col = (t % GRID_W).astype(np.float32)[:, None] * inv
    ang = np.concatenate([row, col], -1)
    cos, sin = np.cos(ang), np.sin(ang)
    cos_h = np.concatenate([cos, cos], -1)
    sin_h = np.concatenate([-sin, sin], -1)
    reps = LANES // HEAD_DIM
    return (jnp.asarray(np.tile(cos_h, (1, reps)), F32), jnp.asarray(np.tile(sin_h, (1, reps)), F32))


def _first_index_of(mask, iota, sentinel):
    return jnp.min(jnp.where(mask, iota, sentinel), axis=0, keepdims=True)


def _route(logits, b_col):
    n = logits.shape[1]
    scores = jax.nn.sigmoid(logits)
    sel = scores + b_col
    io_g = lax.broadcasted_iota(I32, (GROUP_SIZE, n), 0)
    gs_rows = []
    for g in range(N_GROUPS):
        s = sel[g * GROUP_SIZE:(g + 1) * GROUP_SIZE]
        m1 = jnp.max(s, axis=0, keepdims=True)
        i1 = _first_index_of(s == m1, io_g, GROUP_SIZE)
        m2 = jnp.max(jnp.where(io_g == i1, PICKED, s), axis=0, keepdims=True)
        gs_rows.append(m1 + m2)
    gs = jnp.concatenate(gs_rows, axis=0)
    io_n = lax.broadcasted_iota(I32, (N_GROUPS, n), 0)
    gsel = jnp.zeros((N_GROUPS, n), F32)
    for _ in range(TOPK_GROUPS):
        mg = jnp.max(gs, axis=0, keepdims=True)
        gi = _first_index_of(gs == mg, io_n, N_GROUPS)
        hit = io_n == gi
        gsel = jnp.where(hit, 1.0, gsel)
        gs = jnp.where(hit, PICKED, gs)
    cand = jnp.concatenate(
        [jnp.where(gsel[g:g + 1] > 0.5, sel[g * GROUP_SIZE:(g + 1) * GROUP_SIZE], NEG_INF)
         for g in range(N_GROUPS)], axis=0)
    io_e = lax.broadcasted_iota(I32, (N_EXPERTS, n), 0)
    picks, raw = [], []
    for _ in range(TOP_K):
        mv = jnp.max(cand, axis=0, keepdims=True)
        ei = _first_index_of(cand == mv, io_e, N_EXPERTS)
        hit = io_e == ei
        picks.append((hit, ei))
        raw.append(jnp.sum(jnp.where(hit, scores, 0.0), axis=0, keepdims=True))
        cand = jnp.where(hit, PICKED, cand)
    return picks, raw


def _post_mixer_kernel(xc_ref, xl_ref, mixc_ref, ya_ref, yb_ref, yc_ref, wout_ref, mod_ref, g_ref, b_ref,
                       wr_ref, br_ref,
                       x1_ref, h2_ref, eidx_ref, wsel_ref, rank_ref, cnt_ref):
    step = pl.program_id(0)
    tm = TM_TOK
    is_ctx = step < T_CTX // (POST_TILES * tm)

    @pl.when(step == 0)
    def _():
        cnt_ref[...] = jnp.zeros_like(cnt_ref)

    def project(s):
        rows = slice(s * tm, (s + 1) * tm)
        ci = _cond_row(step * POST_TILES + s, tm)
        gate1 = mod_ref[pl.ds(ci, 1), 2 * D_MODEL:3 * D_MODEL]
        sh2 = mod_ref[pl.ds(ci, 1), 3 * D_MODEL:4 * D_MODEL]
        sc2 = mod_ref[pl.ds(ci, 1), 4 * D_MODEL:5 * D_MODEL]
        mix_lat = jnp.concatenate([ya_ref[rows, :], yb_ref[rows, :], yc_ref[rows, :]], axis=-1)
        mix = jnp.where(is_ctx, mixc_ref[rows, :], mix_lat)
        y = _dot(mix, wout_ref[...])
        x = jnp.where(is_ctx, xc_ref[rows, :], xl_ref[rows, :])
        x1 = _layer_norm(ALPHA * x + gate1 * y, g_ref[...], b_ref[...])
        x1_ref[rows, :] = x1
        h2 = x1 * (1.0 + sc2) + sh2
        h_hi = h2.astype(BF16)
        h2_ref[rows, :] = h_hi
        h_lo = (h2 - h_hi.astype(F32)).astype(BF16)
        both = (_dot(h_hi, wr_ref[...]) + _dot(h_lo, wr_ref[...])).T
        return both[0:N_EXPERTS] + both[N_EXPERTS:2 * N_EXPERTS]

    def route(s, logits):
        toks = s * tm
        routed = [_route(logits[:, g * LANES:(g + 1) * LANES], br_ref[...]) for g in range(tm // LANES)]
        multi_g = []
        for picks, _ in routed:
            m = jnp.zeros((N_EXPERTS, LANES), F32)
            for hit, _ in picks:
                m = m + jnp.where(hit, 1.0, 0.0)
            multi_g.append(m)
        multi = jnp.concatenate(multi_g, axis=1)
        before = (lax.broadcasted_iota(I32, (tm, tm), 0) < lax.broadcasted_iota(I32, (tm, tm), 1))
        cum = _dot(multi.astype(BF16), jnp.where(before, 1.0, 0.0).astype(BF16))
        pad = jnp.zeros((SUBLANES - TOP_K, LANES), F32)
        for g, (picks, raw) in enumerate(routed):
            lanes = slice(toks + g * LANES, toks + (g + 1) * LANES)
            total = raw[0]
            for r in raw[1:]:
                total = total + r
            scale = ROUTE_SCALE / total
            cum_g = cum[:, g * LANES:(g + 1) * LANES]
            eidx_ref[:, lanes] = jnp.concatenate([ei for _, ei in picks] + [pad.astype(I32)], axis=0)
            wsel_ref[:, lanes] = jnp.concatenate([r * scale for r in raw] + [pad], axis=0)
            rank_ref[:, lanes] = jnp.concatenate(
                [jnp.sum(jnp.where(hit, cum_g, 0.0), axis=0, keepdims=True) for hit, _ in picks] + [pad],
                axis=0).astype(I32)
        tile_lane = lax.broadcasted_iota(I32, (N_EXPERTS, LANES), 1)
        cnt_ref[...] = jnp.where(tile_lane == step * POST_TILES + s,
                                 jnp.sum(multi, axis=1, keepdims=True), cnt_ref[...])

    logits = [project(s) for s in range(POST_TILES)]
    for s in range(POST_TILES):
        route(s, logits[s])


def _post_mixer(x_ctx, x_lat, mix_c, ya, yb, yc, w_out_bf16, mod, ln_g, ln_b, wr_split, b_router_col):
    tm = POST_TILES * TM_TOK
    n_ctx = T_CTX // tm
    ctx_map = lambda i: (jnp.minimum(i, n_ctx - 1), 0)
    lat_map = lambda i: (jnp.maximum(i - n_ctx, 0), 0)
    row_map = lambda i: (i, 0)
    const = lambda i: (0, 0)
    tok_map = lambda i: (0, i)
    return pl.pallas_call(
        _post_mixer_kernel,
        out_shape=(
            jax.ShapeDtypeStruct((T_ALL, D_MODEL), F32),
            jax.ShapeDtypeStruct((T_ALL, D_MODEL), BF16),
            jax.ShapeDtypeStruct((SUBLANES, T_ALL), I32),
            jax.ShapeDtypeStruct((SUBLANES, T_ALL), F32),
            jax.ShapeDtypeStruct((SUBLANES, T_ALL), I32),
            jax.ShapeDtypeStruct((N_EXPERTS, LANES), F32),
        ),
        grid=(T_ALL // tm,),
        in_specs=[
            pl.BlockSpec((tm, D_MODEL), ctx_map),
            pl.BlockSpec((tm, D_MODEL), lat_map),
            pl.BlockSpec((tm, D_MODEL), ctx_map),
            pl.BlockSpec((tm, W_A), lat_map),
            pl.BlockSpec((tm, W_B), lat_map),
            pl.BlockSpec((tm, W_C), lat_map),
            pl.BlockSpec((D_MODEL, D_MODEL), const),
            pl.BlockSpec((N_COND, 6 * D_MODEL), const),
            pl.BlockSpec((1, D_MODEL), const),
            pl.BlockSpec((1, D_MODEL), const),
            pl.BlockSpec((D_MODEL, 2 * N_EXPERTS), const),
            pl.BlockSpec((N_EXPERTS, 1), const),
        ],
        out_specs=(
            pl.BlockSpec((tm, D_MODEL), row_map),
            pl.BlockSpec((tm, D_MODEL), row_map),
            pl.BlockSpec((SUBLANES, tm), tok_map),
            pl.BlockSpec((SUBLANES, tm), tok_map),
            pl.BlockSpec((SUBLANES, tm), tok_map),
            pl.BlockSpec((N_EXPERTS, LANES), const),
        ),
        compiler_params=pltpu.CompilerParams(
            dimension_semantics=("arbitrary",), vmem_limit_bytes=VMEM_LIMIT),
        name="post_mixer",
    )(x_ctx, x_lat, mix_c, ya, yb, yc, w_out_bf16, mod, ln_g, ln_b, wr_split, b_router_col)


PLAN_TILES = 4


def _plan_kernel(eidx_ref, rank_ref, nmat_ref, lslot_ref, unit_ref, gend_ref):
    step = pl.program_id(0)
    units = jnp.floor((nmat_ref[...] + (UNIT - 1.0)) * (1.0 / UNIT))
    units_bf = units.astype(BF16)
    earlier_e = (lax.broadcasted_iota(I32, (N_EXPERTS, N_EXPERTS), 1)
                 < lax.broadcasted_iota(I32, (N_EXPERTS, N_EXPERTS), 0))
    tri_e = jnp.where(earlier_e, 1.0, 0.0).astype(BF16)
    earlier_t = (lax.broadcasted_iota(I32, (LANES, LANES), 0) < lax.broadcasted_iota(I32, (LANES, LANES), 1))
    tri_t = jnp.where(earlier_t, 1.0, 0.0).astype(BF16)
    local_off = _dot(tri_e, units_bf)
    tile_off = _dot(units_bf, tri_t)
    per_expert = jnp.sum(units, axis=1, keepdims=True)
    blocks = jnp.floor((per_expert + (UNITS_PER_BLOCK - 1.0)) * (1.0 / UNITS_PER_BLOCK))
    blocks_l = jnp.broadcast_to(blocks, (N_EXPERTS, LANES))
    start_blk = _dot(tri_e, blocks_l.astype(BF16))
    end_blk = start_blk + blocks_l
    gend_ref[...] = (end_blk * BM).astype(I32)

    tile_lane = lax.broadcasted_iota(I32, (N_EXPERTS, LANES), 1)
    u = lax.broadcasted_iota(I32, (N_EXPERTS, MAX_UNITS), 1).astype(F32)
    io_e = lax.broadcasted_iota(I32, (N_EXPERTS, TM_TOK), 0)
    for s in range(PLAN_TILES):
        i = step * PLAN_TILES + s
        this_tile = tile_lane == i

        def column(a):
            return jnp.sum(jnp.where(this_tile, a, 0.0), axis=1, keepdims=True)

        lo, n_u = column(local_off), column(units)
        base_unit = start_blk[:, 0:1] * UNITS_PER_BLOCK + column(tile_off) - lo
        inside = jnp.where(u >= lo, jnp.where(u < lo + n_u, 1.0, 0.0), 0.0)
        dst_unit = jnp.sum(inside * (base_unit + u), axis=0, keepdims=True)
        used = jnp.sum(inside, axis=0, keepdims=True) > 0.5
        spare0 = (SPARE_UNIT0 + (i % 2) * MAX_UNITS).astype(F32)
        table = jnp.where(used, dst_unit, spare0 + u[0:1, :])
        n_units = jnp.sum(n_u, axis=0, keepdims=True)
        unit_ref[s] = jnp.where(u[0:1, :] == MAX_UNITS - 1.0, spare0 + n_units, table).astype(I32)

        toks = slice(s * TM_TOK, (s + 1) * TM_TOK)
        rows = []
        for k in range(TOP_K):
            hit = io_e == eidx_ref[k:k + 1, toks]
            seg = jnp.sum(jnp.where(hit, lo * UNIT, 0.0), axis=0, keepdims=True)
            rows.append(seg.astype(I32) + rank_ref[k:k + 1, toks])
        rows.append(jnp.full((SUBLANES - TOP_K, TM_TOK), -1, I32))
        lslot_ref[:, toks] = jnp.concatenate(rows, axis=0)


def _slot_plan(eidx, rank, nmat):
    tok_map = lambda i: (0, i)
    const = lambda i: (0, 0)
    lslot, unit_tab, gend = pl.pallas_call(
        _plan_kernel,
        out_shape=(
            jax.ShapeDtypeStruct((SUBLANES, T_ALL), I32),
            jax.ShapeDtypeStruct((N_TOK_TILES, 1, MAX_UNITS), I32),
            jax.ShapeDtypeStruct((N_EXPERTS, LANES), I32),
        ),
        grid=(N_TOK_TILES // PLAN_TILES,),
        in_specs=[
            pl.BlockSpec((SUBLANES, PLAN_TILES * TM_TOK), tok_map),
            pl.BlockSpec((SUBLANES, PLAN_TILES * TM_TOK), tok_map),
            pl.BlockSpec((N_EXPERTS, LANES), const),
        ],
        out_specs=(
            pl.BlockSpec((SUBLANES, PLAN_TILES * TM_TOK), tok_map),
            pl.BlockSpec((PLAN_TILES, 1, MAX_UNITS), lambda i: (i, 0, 0)),
            pl.BlockSpec((N_EXPERTS, LANES), const),
        ),
        compiler_params=pltpu.CompilerParams(dimension_semantics=("arbitrary",)),
        name="slot_plan",
    )(eidx, rank, nmat)
    return lslot, unit_tab.reshape(N_TOK_TILES * MAX_UNITS), gend[:, 0]


PACK_W = D_MODEL // 2
HI_HALF = -65536


def _pack_pairs(x):
    lo = lax.bitcast_convert_type(x[:, 0:PACK_W], I32)
    hi = lax.bitcast_convert_type(x[:, PACK_W:D_MODEL], I32)
    return lax.shift_right_logical(lo, 16) | (hi & HI_HALF)


def _unpack_pairs(u):
    lo = lax.bitcast_convert_type(lax.shift_left(u, 16), F32).astype(BF16)
    hi = lax.bitcast_convert_type(u & HI_HALF, F32).astype(BF16)
    return lo, hi


def _unit_rows(unit):
    row = unit * UNIT
    return pl.ds(row if isinstance(unit, int) else pl.multiple_of(row, UNIT), UNIT)


def _unit_copy(src, src_unit, dst, dst_unit, sem):
    return pltpu.make_async_copy(src.at[_unit_rows(src_unit)], dst.at[_unit_rows(dst_unit)], sem)


def _dispatch_kernel(fresh, gend_ref, tab_ref, h2_ref, lslot_ref, *refs):
    xs_hbm, zero_ref, local_ref, sem_zero, sem_rows = refs if fresh else refs[1:]
    i = pl.program_id(0)
    buf = i % 2

    def tail_groups(tile):
        n_units = tab_ref[tile * MAX_UNITS + MAX_UNITS - 1] - (SPARE_UNIT0 + (tile % 2) * MAX_UNITS)
        return [(g0, g0 < n_units) for g0 in range(FIXED_UNITS, MAX_UNITS, TAIL_GROUP)]

    def drain(b, tile):
        def wait_rows(rows):
            pltpu.make_async_copy(local_ref.at[b, pl.ds(0, rows)], xs_hbm.at[pl.ds(0, rows)],
                                  sem_rows.at[b]).wait()

        wait_rows(FIXED_UNITS * UNIT)
        for _, live in tail_groups(tile):
            @pl.when(live)
            def _():
                wait_rows(TAIL_GROUP * UNIT)

    def has_rows(e):
        return gend_ref[e] > jnp.where(e == 0, 0, gend_ref[jnp.maximum(e - 1, 0)])

    def zero_copy(e):
        return pltpu.make_async_copy(
            zero_ref, xs_hbm.at[pl.ds(pl.multiple_of(gend_ref[e] - BM, BM), BM)], sem_zero)

    @pl.when(jnp.logical_and(i == 0, fresh))
    def _():
        zero_ref[...] = jnp.zeros_like(zero_ref)

        def start(e, c):
            @pl.when(has_rows(e))
            def _():
                zero_copy(e).start()
            return c

        def wait(e, c):
            @pl.when(has_rows(e))
            def _():
                zero_copy(e).wait()
            return c

        def tail_copy(blk):
            return pltpu.make_async_copy(
                zero_ref, xs_hbm.at[pl.ds(pl.multiple_of(blk * BM, BM), BM)], sem_zero)

        def start_tail(blk, c):
            tail_copy(blk).start()
            return c

        def wait_tail(blk, c):
            tail_copy(blk).wait()
            return c

        n_used = gend_ref[N_EXPERTS - 1] // BM
        lax.fori_loop(0, N_EXPERTS, start, 0)
        lax.fori_loop(n_used, N_BLOCKS_ALL, start_tail, 0)
        lax.fori_loop(0, N_EXPERTS, wait, 0)
        lax.fori_loop(n_used, N_BLOCKS_ALL, wait_tail, 0)

    @pl.when(i >= 2)
    def _():
        drain(buf, i - 2)

    h2 = h2_ref[...]
    units_per_chunk = PERM_CHUNK // UNIT
    local = local_ref.at[buf]
    for c in range(LOCAL_ROWS // PERM_CHUNK):
        slot = c * PERM_CHUNK + lax.broadcasted_iota(I32, (PERM_CHUNK, TM_TOK), 0)
        p = jnp.zeros((PERM_CHUNK, TM_TOK), F32)
        for k in range(TOP_K):
            p = jnp.where(slot == lslot_ref[k:k + 1, :], 1.0, p)
        local[c * PERM_CHUNK:(c + 1) * PERM_CHUNK, :] = _pack_pairs(_dot(p.astype(BF16), h2))
        for u in range(c * units_per_chunk, min((c + 1) * units_per_chunk, FIXED_UNITS)):
            _unit_copy(local, u, xs_hbm, tab_ref[i * MAX_UNITS + u], sem_rows.at[buf]).start()

    for g0, live in tail_groups(i):
        @pl.when(live)
        def _(g0=g0):
            for u in range(g0, g0 + TAIL_GROUP):
                _unit_copy(local, u, xs_hbm, tab_ref[i * MAX_UNITS + u], sem_rows.at[buf]).start()

    @pl.when(i == N_TOK_TILES - 1)
    def _():
        drain(1 - buf, i - 1)
        drain(buf, i)


def _dispatch(h2, lslot, unit_tab, gend, xs_prev):
    fresh = xs_prev is None
    n_prefetch = 2
    return pl.pallas_call(
        functools.partial(_dispatch_kernel, fresh),
        out_shape=jax.ShapeDtypeStruct((N_SLOTS, PACK_W), I32),
        input_output_aliases={} if fresh else {n_prefetch + 2: 0},
        grid_spec=pltpu.PrefetchScalarGridSpec(
            num_scalar_prefetch=n_prefetch,
            grid=(N_TOK_TILES,),
            in_specs=[
                pl.BlockSpec((TM_TOK, D_MODEL), lambda i, ge, tab: (i, 0)),
                pl.BlockSpec((SUBLANES, TM_TOK), lambda i, ge, tab: (0, i)),
            ] + ([] if fresh else [pl.BlockSpec(memory_space=pl.ANY)]),
            out_specs=pl.BlockSpec(memory_space=pl.ANY),
            scratch_shapes=[
                pltpu.VMEM((BM, PACK_W), I32),
                pltpu.VMEM((2, LOCAL_ROWS, PACK_W), I32),
                pltpu.SemaphoreType.DMA,
                pltpu.SemaphoreType.DMA((2,)),
            ],
        ),
        compiler_params=pltpu.CompilerParams(
            dimension_semantics=("arbitrary",), vmem_limit_bytes=VMEM_LIMIT),
        name="moe_dispatch",
    )(gend, unit_tab, h2, lslot, *(() if fresh else (xs_prev,)))


def _expert_kernel(fresh, gend_ref, xs_hbm, wgu_ref, wdown_ref, *refs):
    ys_hbm, xbuf, ybuf, wgu_bf, wdown_bf, sem_in, sem_out = refs if fresh else refs[1:]
    e = pl.program_id(0)
    first = jnp.where(e == 0, 0, gend_ref[jnp.maximum(e - 1, 0)]) // BM
    last = gend_ref[e] // BM
    n_used = gend_ref[N_EXPERTS - 1] // BM

    def rows_of(blk):
        return pl.ds(pl.multiple_of(blk * BM, BM), BM)

    def in_copy(blk):
        return pltpu.make_async_copy(xs_hbm.at[rows_of(blk)], xbuf.at[blk % X_BUFS], sem_in.at[blk % X_BUFS])

    def out_copy(blk):
        return pltpu.make_async_copy(ybuf.at[blk % 2], ys_hbm.at[rows_of(blk)], sem_out.at[blk % 2])

    @pl.when(e == 0)
    def _():
        for a in range(X_AHEAD):
            @pl.when(a < n_used)
            def _(a=a):
                in_copy(a).start()

    @pl.when(last > first)
    def _():
        wgu_bf[...] = wgu_ref[0, 0].astype(BF16)
        wdown_bf[...] = wdown_ref[0, 0].astype(BF16)

    def block(blk, carry):
        @pl.when(blk + X_AHEAD < n_used)
        def _():
            in_copy(blk + X_AHEAD).start()

        in_copy(blk).wait()

        @pl.when(blk >= 2)
        def _():
            out_copy(blk - 2).wait()

        x_lo, x_hi = _unpack_pairs(xbuf[blk % X_BUFS])
        gu = _dot(x_lo, wgu_bf[0:PACK_W, :]) + _dot(x_hi, wgu_bf[PACK_W:D_MODEL, :])
        act = _silu(gu[:, 0:D_EXPERT]) * gu[:, D_EXPERT:2 * D_EXPERT]
        y = _dot(act.astype(BF16), wdown_bf[...])
        ybuf[blk % 2] = _pack_pairs(y.astype(BF16).astype(F32))
        out_copy(blk).start()
        return carry

    lax.fori_loop(first, last, block, 0)

    @pl.when(e == N_EXPERTS - 1)
    def _():
        @pl.when(n_used >= 2)
        def _():
            out_copy(n_used - 2).wait()

        @pl.when(n_used >= 1)
        def _():
            out_copy(n_used - 1).wait()

        if fresh:
            ybuf[0] = jnp.zeros((BM, PACK_W), I32)

            def tail_copy(blk):
                return pltpu.make_async_copy(ybuf.at[0], ys_hbm.at[rows_of(blk)], sem_out.at[0])

            def start(blk, c):
                tail_copy(blk).start()
                return c

            def wait(blk, c):
                tail_copy(blk).wait()
                return c

            lax.fori_loop(n_used, N_BLOCKS_ALL, start, 0)
            lax.fori_loop(n_used, N_BLOCKS_ALL, wait, 0)


def _experts(xs, w_gu, w_down, layer, gend, ys_prev):
    fresh = ys_prev is None
    n_prefetch = 1
    return pl.pallas_call(
        functools.partial(_expert_kernel, fresh),
        out_shape=jax.ShapeDtypeStruct((N_SLOTS, PACK_W), I32),
        input_output_aliases={} if fresh else {n_prefetch + 3: 0},
        grid_spec=pltpu.PrefetchScalarGridSpec(
            num_scalar_prefetch=n_prefetch,
            grid=(N_EXPERTS,),
            in_specs=[
                pl.BlockSpec(memory_space=pl.ANY),
                pl.BlockSpec((1, 1, D_MODEL, 2 * D_EXPERT), lambda e, ge: (layer, e, 0, 0)),
                pl.BlockSpec((1, 1, D_EXPERT, D_MODEL), lambda e, ge: (layer, e, 0, 0)),
            ] + ([] if fresh else [pl.BlockSpec(memory_space=pl.ANY)]),
            out_specs=pl.BlockSpec(memory_space=pl.ANY),
            scratch_shapes=[
                pltpu.VMEM((X_BUFS, BM, PACK_W), I32),
                pltpu.VMEM((2, BM, PACK_W), I32),
                pltpu.VMEM((D_MODEL, 2 * D_EXPERT), BF16),
                pltpu.VMEM((D_EXPERT, D_MODEL), BF16),
                pltpu.SemaphoreType.DMA((X_BUFS,)),
                pltpu.SemaphoreType.DMA((2,)),
            ],
        ),
        compiler_params=pltpu.CompilerParams(
            dimension_semantics=("arbitrary",), vmem_limit_bytes=VMEM_LIMIT),
        name="moe_experts",
    )(gend, xs, w_gu, w_down, *(() if fresh else (ys_prev,)))


def _combine_kernel(tab_ref, x1_ref, h2_ref, lslot_ref, wsel_ref, ys_hbm, wsgu_ref, wsdown_ref, mod_ref,
                    g_ref, b_ref, outc_ref, outl_ref, local_ref, sel_ref, ylo_ref, yhi_ref, sem_rows):
    i = pl.program_id(0)
    tm = TM_TOK
    buf = i % 2

    def fetch_unit(tile, b, u):
        _unit_copy(ys_hbm, tab_ref[tile * MAX_UNITS + u], local_ref.at[b], u, sem_rows.at[b]).start()

    def drain(b):
        pltpu.make_async_copy(ys_hbm.at[pl.ds(0, LOCAL_ROWS)], local_ref.at[b], sem_rows.at[b]).wait()

    @pl.when(i == 0)
    def _():
        def body(u, c):
            fetch_unit(0, 0, u)
            return c

        lax.fori_loop(0, MAX_UNITS, body, 0, unroll=8)

    sgu = _dot(h2_ref[...], wsgu_ref[...])
    act = _silu(sgu[:, 0:D_SHARED]) * sgu[:, D_SHARED:2 * D_SHARED]
    f = _dot(act.astype(BF16), wsdown_ref[...])

    nxt = jnp.minimum(i + 1, N_TOK_TILES - 1)
    n_groups = tm // SEL_ROWS
    units_per_group = MAX_UNITS // n_groups
    slot = lax.broadcasted_iota(I32, (SEL_ROWS, LOCAL_ROWS), 1)
    for g in range(n_groups):
        rows = slice(g * SEL_ROWS, (g + 1) * SEL_ROWS)
        sel = jnp.zeros((SEL_ROWS, LOCAL_ROWS), F32)
        for k in range(TOP_K):
            sel = jnp.where(slot == lslot_ref[rows, k:k + 1], wsel_ref[rows, k:k + 1], sel)
        sel_ref[rows, :] = sel.astype(BF16)
        for u in range(g * units_per_group, (g + 1) * units_per_group):
            fetch_unit(nxt, 1 - buf, u)

    local = local_ref.at[buf]
    drain(buf)
    for c in range(LOCAL_ROWS // PERM_CHUNK):
        rows = slice(c * PERM_CHUNK, (c + 1) * PERM_CHUNK)
        ylo_ref[rows, :], yhi_ref[rows, :] = _unpack_pairs(local[rows, :])
    sel = sel_ref[...]
    f = f + jnp.concatenate([_dot(sel, ylo_ref[...]), _dot(sel, yhi_ref[...])], axis=-1)
    ci = _cond_row(i, tm)
    gate2 = mod_ref[pl.ds(ci, 1), 5 * D_MODEL:6 * D_MODEL]
    out = _layer_norm(ALPHA * x1_ref[...] + gate2 * f, g_ref[...], b_ref[...])

    @pl.when(i < T_CTX // tm)
    def _():
        outc_ref[...] = out

    @pl.when(i >= T_CTX // tm)
    def _():
        outl_ref[...] = out

    @pl.when(i == N_TOK_TILES - 1)
    def _():
        drain(1 - buf)


def _combine(x1, h2, lslot_rows, wsel_rows, unit_tab, ys, w_sgu_bf16, w_sdown_bf16, mod, ln_g, ln_b):
    n_ctx = T_CTX // TM_TOK
    row_map = lambda i, tab: (i, 0)
    const = lambda i, tab: (0, 0)
    return pl.pallas_call(
        _combine_kernel,
        out_shape=(jax.ShapeDtypeStruct((T_CTX, D_MODEL), F32),
                   jax.ShapeDtypeStruct((T_LAT, D_MODEL), F32)),
        grid_spec=pltpu.PrefetchScalarGridSpec(
            num_scalar_prefetch=1,
            grid=(N_TOK_TILES,),
            in_specs=[
                pl.BlockSpec((TM_TOK, D_MODEL), row_map),
                pl.BlockSpec((TM_TOK, D_MODEL), row_map),
                pl.BlockSpec((TM_TOK, SUBLANES), row_map),
                pl.BlockSpec((TM_TOK, SUBLANES), row_map),
                pl.BlockSpec(memory_space=pl.ANY),
                pl.BlockSpec((D_MODEL, 2 * D_SHARED), const),
                pl.BlockSpec((D_SHARED, D_MODEL), const),
                pl.BlockSpec((N_COND, 6 * D_MODEL), const),
                pl.BlockSpec((1, D_MODEL), const),
                pl.BlockSpec((1, D_MODEL), const),
            ],
            out_specs=(pl.BlockSpec((TM_TOK, D_MODEL), lambda i, tab: (jnp.minimum(i, n_ctx - 1), 0)),
                       pl.BlockSpec((TM_TOK, D_MODEL), lambda i, tab: (jnp.maximum(i - n_ctx, 0), 0))),
            scratch_shapes=[
                pltpu.VMEM((2, LOCAL_ROWS, PACK_W), I32),
                pltpu.VMEM((TM_TOK, LOCAL_ROWS), BF16),
                pltpu.VMEM((LOCAL_ROWS, PACK_W), BF16),
                pltpu.VMEM((LOCAL_ROWS, PACK_W), BF16),
                pltpu.SemaphoreType.DMA((2,)),
            ],
        ),
        compiler_params=pltpu.CompilerParams(
            dimension_semantics=("arbitrary",), vmem_limit_bytes=VMEM_LIMIT),
        name="moe_combine",
    )(unit_tab, x1, h2, lslot_rows, wsel_rows, ys, w_sgu_bf16, w_sdown_bf16, mod, ln_g, ln_b)


def _lane_rows(v, width):
    return jnp.broadcast_to(v.astype(F32)[:, None, None], (v.shape[0], 1, width))


def kernel(x_prompt, x_sample, state_ret_fwd, state_ret_bwd, cache_win_k, cache_win_v, cache_na_k, cache_na_v, c, c_ctx, w_in, w_out, ret_decay_fwd, ret_decay_bwd, ret_gn_g, ret_gn_b, win_sink, na_rpb, w_mod, b_mod, ln1_g, ln1_b, ln2_g, ln2_b, w_router, b_router, w_expert_gu, w_expert_down, w_shared_gu, w_shared_down):
    cond = jnp.concatenate(
        [c_ctx[None, :], c, jnp.zeros((N_COND - 1 - DEC_BATCH, D_MODEL), F32)], axis=0)
    mod_all = _modulation(cond, w_mod, b_mod)
    cos_t, sin_t = _rope_tables()

    x_ctx = x_prompt.reshape(T_CTX, D_MODEL)
    x_lat = x_sample.reshape(T_LAT, D_MODEL)
    sf_l, sb_l, caches = [], [], ()
    xs = ys = None
    for l in range(DEPTH):
        mod = mod_all[l]
        pa, pb, pc, *caches = _in_projection(x_ctx, x_lat, mod, w_in[l].astype(BF16), tuple(caches))
        decf_s, decb_s = _lane_rows(ret_decay_fwd[l], SEQ), _lane_rows(ret_decay_bwd[l], SEQ)
        gng, gnb = ret_gn_g[l][None, :], ret_gn_b[l][None, :]
        mix_c, st_f, st_b = _ctx_mixers(pa, pb, pc, win_sink[l], decf_s, decb_s, gng, gnb)
        sf_l.append(st_f)
        sb_l.append(st_b)
        ya = _lat_retention(pa, state_ret_fwd, state_ret_bwd, l,
                            _lane_rows(ret_decay_fwd[l], RET_CHUNK), _lane_rows(ret_decay_bwd[l], RET_CHUNK),
                            gng, gnb)
        yb = _lat_window_attn(pb, cache_win_k, cache_win_v, l, win_sink[l], cos_t, sin_t)
        yc = _lat_na_attn(pc, cache_na_k, cache_na_v, l, _na_maskbias(na_rpb[l]))

        wr_hi = w_router[l].astype(BF16)
        wr_lo = (w_router[l] - wr_hi.astype(F32)).astype(BF16)
        x1, h2, eidx, wsel, rank, counts = _post_mixer(
            x_ctx, x_lat, mix_c, ya, yb, yc, w_out[l].astype(BF16), mod, ln1_g[l][None, :],
            ln1_b[l][None, :], jnp.concatenate([wr_hi, wr_lo], axis=1), b_router[l][:, None])
        lslot, unit_tab, gend = _slot_plan(eidx, rank, counts)
        xs = _dispatch(h2, lslot, unit_tab, gend, xs)
        ys = _experts(xs, w_expert_gu, w_expert_down, l, gend, ys)
        x_ctx, x_lat = _combine(x1, h2, lslot.T, wsel.T, unit_tab, ys, w_shared_gu[l].astype(BF16),
                                w_shared_down[l].astype(BF16), mod, ln2_g[l][None, :], ln2_b[l][None, :])

    y_prompt = x_ctx.reshape(BATCH, SEQ, D_MODEL)
    y_sample = x_lat.reshape(DEC_BATCH, DEC_SEQ, D_MODEL)
    new_sf = jnp.stack(sf_l, axis=1)
    new_sb = jnp.stack(sb_l, axis=1)

    return (y_prompt, y_sample, new_sf, new_sb, *caches)
```

```python
import functools

import numpy as np
import jax
import jax.numpy as jnp
from jax import lax
from jax.experimental import pallas as pl
from jax.experimental.pallas import tpu as pltpu

F32 = jnp.float32
BF16 = jnp.bfloat16
I32 = jnp.int32

D_MODEL = 1024
BATCH = 32
SEQ = 256
DEPTH = 2
DEC_BATCH = 4
DEC_SEQ = 2048
PAST_LEN = 256
GRID_W = 64
HEAD_DIM = 64
ATTN_SCALE = HEAD_DIM ** -0.5
H_A = 4
W_A = H_A * HEAD_DIM
GN_EPS = 1e-5
H_B = 6
KV_B = 2
W_B = H_B * HEAD_DIM
KV_W_B = KV_B * HEAD_DIM
WINDOW = 128
ROPE_BASE = 10000.0
H_C = 6
W_C = H_C * HEAD_DIM
NA_ROWS = 8
NA_COLS = 16
IN_WIDTH = 4 * W_A + W_B + 2 * KV_W_B + 3 * W_C
N_EXPERTS = 64
TOP_K = 6
N_GROUPS = 8
GROUP_SIZE = N_EXPERTS // N_GROUPS
TOPK_GROUPS = 4
D_EXPERT = 256
D_SHARED = 256
ROUTE_SCALE = 2.5
ALPHA = (2 * DEPTH) ** 0.25
LN_EPS = 1e-5
NEG_INF = -1e30
PICKED = -3e38

T_CTX = BATCH * SEQ
T_LAT = DEC_BATCH * DEC_SEQ
T_ALL = T_CTX + T_LAT
N_COND = 8

PA_W = 4 * W_A
PB_W = W_B + 2 * KV_W_B
PC_W = 3 * W_C

LANES = 128
SUBLANES = 8
VMEM_LIMIT = 56 * 1024 * 1024

TM_PROJ = 512
TM_TOK = 256
RET_CHUNK = 256
POST_TILES = 4
CTX_SEQS = 2
CTX_STACK = 3
NA_STACK = 6
WIN_QB = 256
WIN_BAND = WIN_QB + 2 * WINDOW
NA_TILE_ROWS = 4
NA_KEY_ROWS = 11
BM = 512
X_AHEAD = 3
X_BUFS = X_AHEAD + 1
UNIT = SUBLANES
UNITS_PER_BLOCK = BM // UNIT
PERM_CHUNK = 256
SEL_ROWS = 16
N_TOK_TILES = T_ALL // TM_TOK
LOCAL_ROWS = -(-(TM_TOK * TOP_K + N_EXPERTS * (UNIT - 1)) // PERM_CHUNK) * PERM_CHUNK
MAX_UNITS = LOCAL_ROWS // UNIT
N_ASSIGN = T_ALL * TOP_K
N_BLOCKS = -(-(N_ASSIGN + N_TOK_TILES * N_EXPERTS * (UNIT - 1) + N_EXPERTS * (BM - 1)) // BM)
SPARE_BLOCKS = 2 * -(-LOCAL_ROWS // BM)
SPARE_UNIT0 = N_BLOCKS * UNITS_PER_BLOCK
N_BLOCKS_ALL = N_BLOCKS + SPARE_BLOCKS
N_SLOTS = N_BLOCKS_ALL * BM


def _dot(a, b):
    return jnp.dot(a, b, preferred_element_type=F32)


def _dot_nt(a, b):
    return lax.dot_general(a, b, (((1,), (1,)), ((), ())), preferred_element_type=F32)


def _silu(x):
    return x * jax.nn.sigmoid(x)


def _log_sigmoid(x):
    return jnp.minimum(x, 0.0) - jnp.log(1.0 + jnp.exp(-jnp.abs(x)))


def _cond_row(tile, tile_rows):
    n_ctx = T_CTX // tile_rows
    per_lat = DEC_SEQ // tile_rows
    return jnp.where(tile < n_ctx, 0, 1 + (tile - n_ctx) // per_lat)


def _layer_norm(x, g, b):
    mu = jnp.mean(x, -1, keepdims=True)
    xc = x - mu
    var = jnp.mean(xc * xc, -1, keepdims=True)
    return xc * lax.rsqrt(var + LN_EPS) * g + b


MOD_TN = 1536


def _mod_kernel(cond_ref, w_ref, b_ref, o_ref):
    s = _silu(cond_ref[...])
    s_hi = s.astype(BF16)
    s_lo = (s - s_hi.astype(F32)).astype(BF16)
    w = w_ref[0]
    w_hi = w.astype(BF16)
    w_lo = (w - w_hi.astype(F32)).astype(BF16)
    o_ref[0] = _dot(s_hi, w_hi) + _dot(s_lo, w_hi) + _dot(s_hi, w_lo) + b_ref[0]


def _modulation(cond, w_mod, b_mod):
    n_out = 6 * D_MODEL
    return pl.pallas_call(
        _mod_kernel,
        out_shape=jax.ShapeDtypeStruct((DEPTH, N_COND, n_out), F32),
        grid=(DEPTH, n_out // MOD_TN),
        in_specs=[
            pl.BlockSpec((N_COND, D_MODEL), lambda l, j: (0, 0)),
            pl.BlockSpec((1, D_MODEL, MOD_TN), lambda l, j: (l, 0, j)),
            pl.BlockSpec((1, 1, MOD_TN), lambda l, j: (l, 0, j)),
        ],
        out_specs=pl.BlockSpec((1, N_COND, MOD_TN), lambda l, j: (l, 0, j)),
        compiler_params=pltpu.CompilerParams(
            dimension_semantics=("arbitrary", "arbitrary"), vmem_limit_bytes=VMEM_LIMIT),
        name="modulation",
    )(cond, w_mod, b_mod.reshape(DEPTH, 1, n_out))


SEQ_PER_PROJ = TM_PROJ // SEQ


def _inproj_kernel(n_prev, xc_ref, xl_ref, mod_ref, w_ref, *refs):
    prev_refs = refs[:4] if n_prev else ()
    pa_ref, pb_ref, pc_ref, wk_ref, wv_ref, nk_ref, nv_ref = refs[len(prev_refs):]
    i = pl.program_id(0)
    is_ctx = i < T_CTX // TM_PROJ
    targets = ((wk_ref, PA_W + W_B, KV_B), (wv_ref, PA_W + W_B + KV_W_B, KV_B),
               (nk_ref, PA_W + PB_W + W_C, H_C), (nv_ref, PA_W + PB_W + 2 * W_C, H_C))

    def project(x_ref, ci, with_cache):
        sh = mod_ref[pl.ds(ci, 1), 0:D_MODEL]
        sc = mod_ref[pl.ds(ci, 1), D_MODEL:2 * D_MODEL]
        for s in range(SEQ_PER_PROJ):
            rows = slice(s * SEQ, (s + 1) * SEQ)
            h = x_ref[rows, :] * (1.0 + sc) + sh
            p = _dot(h.astype(BF16), w_ref[...])
            pa_ref[rows, :] = p[:, 0:PA_W].astype(BF16)
            pb_ref[rows, :] = p[:, PA_W:PA_W + PB_W].astype(BF16)
            pc_ref[rows, :] = p[:, PA_W + PB_W:IN_WIDTH].astype(BF16)
            if with_cache:
                for j, (ref, col0, n_heads) in enumerate(targets):
                    if n_prev:
                        ref[s, 0:n_prev] = prev_refs[j][s]
                    for hh in range(n_heads):
                        ref[s, n_prev, hh] = p[:, col0 + hh * HEAD_DIM:col0 + (hh + 1) * HEAD_DIM]

    @pl.when(is_ctx)
    def _():
        project(xc_ref, 0, True)

    @pl.when(jnp.logical_not(is_ctx))
    def _():
        project(xl_ref, _cond_row(i, TM_PROJ), False)


def _in_projection(x_ctx, x_lat, mod, w_in_bf16, earlier):
    n_ctx_tiles = T_CTX // TM_PROJ
    n_prev = earlier[0].shape[1] if earlier else 0

    def cache_spec(n_layers, n_heads):
        return pl.BlockSpec((SEQ_PER_PROJ, n_layers, n_heads, SEQ, HEAD_DIM),
                            lambda i: (jnp.minimum(i, n_ctx_tiles - 1), 0, 0, 0, 0))

    cache_heads = (KV_B, KV_B, H_C, H_C)
    return pl.pallas_call(
        functools.partial(_inproj_kernel, n_prev),
        out_shape=(
            jax.ShapeDtypeStruct((T_ALL, PA_W), BF16),
            jax.ShapeDtypeStruct((T_ALL, PB_W), BF16),
            jax.ShapeDtypeStruct((T_ALL, PC_W), BF16),
        ) + tuple(jax.ShapeDtypeStruct((BATCH, n_prev + 1, nh, SEQ, HEAD_DIM), F32) for nh in cache_heads),
        grid=(T_ALL // TM_PROJ,),
        in_specs=[
            pl.BlockSpec((TM_PROJ, D_MODEL), lambda i: (jnp.minimum(i, n_ctx_tiles - 1), 0)),
            pl.BlockSpec((TM_PROJ, D_MODEL), lambda i: (jnp.maximum(i - n_ctx_tiles, 0), 0)),
            pl.BlockSpec((N_COND, 6 * D_MODEL), lambda i: (0, 0)),
            pl.BlockSpec((D_MODEL, IN_WIDTH), lambda i: (0, 0)),
        ] + [cache_spec(n_prev, nh) for nh in cache_heads if n_prev],
        out_specs=(
            pl.BlockSpec((TM_PROJ, PA_W), lambda i: (i, 0)),
            pl.BlockSpec((TM_PROJ, PB_W), lambda i: (i, 0)),
            pl.BlockSpec((TM_PROJ, PC_W), lambda i: (i, 0)),
        ) + tuple(cache_spec(n_prev + 1, nh) for nh in cache_heads),
        compiler_params=pltpu.CompilerParams(
            dimension_semantics=("arbitrary",), vmem_limit_bytes=VMEM_LIMIT),
        name="in_projection",
    )(x_ctx, x_lat, mod, w_in_bf16, *earlier)


def _decay_matrix(lg_f, lg_b, n):
    row = lax.broadcasted_iota(I32, (n, n), 0)
    col = lax.broadcasted_iota(I32, (n, n), 1)
    diff = (row - col).astype(F32)
    fwd = jnp.where(diff >= 0, jnp.exp(lg_f * jnp.maximum(diff, 0.0)), 0.0)
    bwd = jnp.where(diff <= 0, jnp.exp(lg_b * jnp.maximum(-diff, 0.0)), 0.0)
    return (fwd + bwd) * ATTN_SCALE


def _stacked_softmax_attend(scores, values, extra_logit=None):
    rows = scores[0].shape[0]
    s = jnp.concatenate(scores, axis=0)
    m = jnp.max(s, -1, keepdims=True)
    if extra_logit is not None:
        m = jnp.maximum(m, extra_logit)
    p = jnp.exp(s - m)
    den = jnp.sum(p, -1, keepdims=True)
    if extra_logit is not None:
        den = den + jnp.exp(extra_logit - m)
    p = p.astype(BF16)
    return [_dot(p[h * rows:(h + 1) * rows], v) / den[h * rows:(h + 1) * rows]
            for h, v in enumerate(values)]


def _lane_xor_matrix(width, distance):
    r = lax.broadcasted_iota(I32, (width, width), 0)
    c = lax.broadcasted_iota(I32, (width, width), 1)
    return jnp.where((r ^ distance) == c, 1.0, 0.0).astype(BF16)


def _ctx_mixer_kernel(sink_ref, pa_ref, pb_ref, pc_ref, decf_ref, decb_ref, gng_ref, gnb_ref,
                      mix_ref, sf_ref, sb_ref, dmat_ref, zf_ref, zb_ref):
    n = SEQ
    hd = HEAD_DIM

    @pl.when(pl.program_id(0) == 0)
    def _():
        pos = lax.broadcasted_iota(I32, (n, hd), 0).astype(F32)
        zf, zb = [], []
        for h in range(H_A):
            lg_f = _log_sigmoid(decf_ref[h])
            lg_b = _log_sigmoid(decb_ref[h])
            dmat_ref[h] = _decay_matrix(lg_f, lg_b, n)
            zf.append(jnp.exp(lg_f[:, 0:hd] * (n - 1.0 - pos)) * ATTN_SCALE)
            zb.append(jnp.exp(lg_b[:, 0:hd] * pos) * ATTN_SCALE)
        for p in range(H_A // 2):
            zf_ref[p] = jnp.concatenate(zf[2 * p:2 * p + 2], axis=1)
            zb_ref[p] = jnp.concatenate(zb[2 * p:2 * p + 2], axis=1)

    low = lax.broadcasted_iota(I32, (n, LANES), 1) < hd

    def own_half(x, h):
        zero = jnp.zeros_like(x)
        return jnp.where(low, x, zero) if h % 2 == 0 else jnp.where(low, zero, x)

    def merge(first, second):
        return jnp.where(low, first, second)

    def head_mean(x):
        first = jnp.sum(jnp.where(low, x, 0.0), -1, keepdims=True)
        second = jnp.sum(jnp.where(low, 0.0, x), -1, keepdims=True)
        return merge(first, second) * (1.0 / hd)

    for s in range(CTX_SEQS):
        rows = slice(s * n, (s + 1) * n)

        def pair_cols(ref, base, p):
            return ref[rows, base + p * LANES:base + (p + 1) * LANES]

        for p in range(H_A // 2):
            q_pair, k_pair = pair_cols(pa_ref, 0, p), pair_cols(pa_ref, W_A, p)
            v_pair, gate_pair = pair_cols(pa_ref, 2 * W_A, p), pair_cols(pa_ref, 3 * W_A, p)
            outs = []
            for h in (2 * p, 2 * p + 1):
                a = _dot_nt(own_half(q_pair, h), k_pair)
                outs.append(_dot((a * dmat_ref[h]).astype(BF16), v_pair))
            o = merge(outs[0], outs[1])
            kf = k_pair.astype(F32)
            for st_ref, z_ref in ((sf_ref, zf_ref), (sb_ref, zb_ref)):
                st = _dot((kf * z_ref[p]).T.astype(BF16), v_pair)
                st_ref[s, 2 * p] = st[0:hd, 0:hd]
                st_ref[s, 2 * p + 1] = st[hd:2 * hd, hd:2 * hd]
            mu = head_mean(o)
            oc = o - mu
            var = head_mean(oc * oc)
            pair = slice(p * LANES, (p + 1) * LANES)
            on = oc * lax.rsqrt(var + GN_EPS) * gng_ref[:, pair] + gnb_ref[:, pair]
            mix_ref[rows, pair] = (on * _silu(gate_pair.astype(F32))).astype(BF16)

        group = H_B // KV_B
        swap = _lane_xor_matrix(LANES, hd)
        kv_k = pb_ref[rows, W_B:W_B + KV_W_B]
        kv_v = pb_ref[rows, W_B + KV_W_B:W_B + 2 * KV_W_B]
        scores = []
        for hh in range(H_B):
            q = own_half(pair_cols(pb_ref, 0, hh // 2), hh) * ATTN_SCALE
            if hh % 2 != hh // group:
                q = _dot(q, swap).astype(BF16)
            scores.append(_dot_nt(q, kv_k))
        sinks = jnp.concatenate([jnp.full((n, 1), sink_ref[hh], F32) for hh in range(H_B)], axis=0)
        outs = []
        stacked = []
        for g0 in range(0, H_B, CTX_STACK):
            stacked += _stacked_softmax_attend(scores[g0:g0 + CTX_STACK], [kv_v] * CTX_STACK,
                                               sinks[g0 * n:(g0 + CTX_STACK) * n])
        for hh, o in enumerate(stacked):
            o = o.astype(BF16)
            outs.append(_dot(o, swap).astype(BF16) if hh % 2 != hh // group else o)
        for p in range(H_B // 2):
            mix_ref[rows, W_A + p * LANES:W_A + (p + 1) * LANES] = merge(outs[2 * p], outs[2 * p + 1])

        scores, values = [], []
        for hh in range(H_C):
            q = own_half(pair_cols(pc_ref, 0, hh // 2), hh) * ATTN_SCALE
            scores.append(_dot_nt(q, pair_cols(pc_ref, W_C, hh // 2)))
            values.append(pair_cols(pc_ref, 2 * W_C, hh // 2))
        outs = []
        for g0 in range(0, H_C, CTX_STACK):
            outs += _stacked_softmax_attend(scores[g0:g0 + CTX_STACK], values[g0:g0 + CTX_STACK])
        for p in range(H_C // 2):
            mix_ref[rows, W_A + W_B + p * LANES:W_A + W_B + (p + 1) * LANES] = merge(
                outs[2 * p], outs[2 * p + 1]).astype(BF16)


def _ctx_mixers(pa, pb, pc, sink, decf, decb, gng, gnb):
    return pl.pallas_call(
        _ctx_mixer_kernel,
        out_shape=(
            jax.ShapeDtypeStruct((T_CTX, D_MODEL), BF16),
            jax.ShapeDtypeStruct((BATCH, H_A, HEAD_DIM, HEAD_DIM), F32),
            jax.ShapeDtypeStruct((BATCH, H_A, HEAD_DIM, HEAD_DIM), F32),
        ),
        grid=(BATCH // CTX_SEQS,),
        in_specs=[
            pl.BlockSpec(memory_space=pltpu.SMEM),
            pl.BlockSpec((CTX_SEQS * SEQ, PA_W), lambda b: (b, 0)),
            pl.BlockSpec((CTX_SEQS * SEQ, PB_W), lambda b: (b, 0)),
            pl.BlockSpec((CTX_SEQS * SEQ, PC_W), lambda b: (b, 0)),
            pl.BlockSpec((H_A, 1, SEQ), lambda b: (0, 0, 0)),
            pl.BlockSpec((H_A, 1, SEQ), lambda b: (0, 0, 0)),
            pl.BlockSpec((1, W_A), lambda b: (0, 0)),
            pl.BlockSpec((1, W_A), lambda b: (0, 0)),
        ],
        out_specs=(
            pl.BlockSpec((CTX_SEQS * SEQ, D_MODEL), lambda b: (b, 0)),
            pl.BlockSpec((CTX_SEQS, H_A, HEAD_DIM, HEAD_DIM), lambda b: (b, 0, 0, 0)),
            pl.BlockSpec((CTX_SEQS, H_A, HEAD_DIM, HEAD_DIM), lambda b: (b, 0, 0, 0)),
        ),
        scratch_shapes=[
            pltpu.VMEM((H_A, SEQ, SEQ), F32),
            pltpu.VMEM((H_A // 2, SEQ, LANES), F32),
            pltpu.VMEM((H_A // 2, SEQ, LANES), F32),
        ],
        compiler_params=pltpu.CompilerParams(
            dimension_semantics=("arbitrary",), vmem_limit_bytes=VMEM_LIMIT),
        name="ctx_mixers",
    )(sink, pa, pb, pc, decf, decb, gng, gnb)


def _lat_ret_kernel(pa_ref, stf_ref, stb_ref, decf_ref, decb_ref, gng_ref, gnb_ref, ya_ref):
    c = RET_CHUNK
    hd = HEAD_DIM
    n_chunks = DEC_SEQ // c
    pos = lax.broadcasted_iota(I32, (c, LANES), 0).astype(F32)
    low = lax.broadcasted_iota(I32, (c, LANES), 1) < hd
    same_head = (lax.broadcasted_iota(I32, (LANES, LANES), 0) // hd
                 == lax.broadcasted_iota(I32, (LANES, LANES), 1) // hd)
    zero_blk = jnp.zeros((hd, hd), F32)

    def block_diag(a, b):
        return jnp.concatenate([jnp.concatenate([a, zero_blk], axis=1),
                                jnp.concatenate([zero_blk, b], axis=1)], axis=0)

    def head_mean(x):
        first = jnp.sum(jnp.where(low, x, 0.0), -1, keepdims=True)
        second = jnp.sum(jnp.where(low, 0.0, x), -1, keepdims=True)
        return jnp.where(low, first, second) * (1.0 / hd)

    for p in range(H_A // 2):
        lg_f = [_log_sigmoid(decf_ref[h]) for h in (2 * p, 2 * p + 1)]
        lg_b = [_log_sigmoid(decb_ref[h]) for h in (2 * p, 2 * p + 1)]
        dmat = [_decay_matrix(lg_f[t], lg_b[t], c) for t in range(2)]
        lf = jnp.where(low[0:1], lg_f[0][:, 0:LANES], lg_f[1][:, 0:LANES])
        lb = jnp.where(low[0:1], lg_b[0][:, 0:LANES], lg_b[1][:, 0:LANES])
        zf = jnp.exp(lf * (c - 1.0 - pos)) * ATTN_SCALE
        zb = jnp.exp(lb * pos) * ATTN_SCALE
        xf = jnp.exp(lf * (pos + 1.0))
        xb = jnp.exp(lb * (c - pos))
        gcf = jnp.exp(lf * float(c))
        gcb = jnp.exp(lb * float(c))
        pair = slice(p * LANES, (p + 1) * LANES)

        def chunk(i, base):
            return pa_ref[i * c:(i + 1) * c, base + p * LANES:base + (p + 1) * LANES]

        kv_f, kv_b = [], []
        for i in range(n_chunks):
            kf = chunk(i, W_A).astype(F32)
            v = chunk(i, 2 * W_A)
            kv_f.append(jnp.where(same_head, _dot((kf * zf).T.astype(BF16), v), 0.0))
            kv_b.append(jnp.where(same_head, _dot((kf * zb).T.astype(BF16), v), 0.0))

        s = block_diag(stf_ref[0, 0, 2 * p], stf_ref[0, 0, 2 * p + 1])
        seen_f = []
        for i in range(n_chunks):
            seen_f.append(s)
            s = gcf * s + kv_f[i]
        s = block_diag(stb_ref[0, 0, 2 * p], stb_ref[0, 0, 2 * p + 1])
        seen_b = [None] * n_chunks
        for i in reversed(range(n_chunks)):
            seen_b[i] = s
            s = gcb * s + kv_b[i]

        for i in range(n_chunks):
            q_pair, k_pair, v_pair = chunk(i, 0), chunk(i, W_A), chunk(i, 2 * W_A)
            inner = []
            for t in range(2):
                zero = jnp.zeros_like(q_pair)
                q = jnp.where(low, q_pair, zero) if t == 0 else jnp.where(low, zero, q_pair)
                inner.append(_dot((_dot_nt(q, k_pair) * dmat[t]).astype(BF16), v_pair))
            qf = q_pair.astype(F32)
            lhs = jnp.concatenate([(qf * xf).astype(BF16), (qf * xb).astype(BF16)], axis=1)
            rhs = jnp.concatenate([seen_f[i], seen_b[i]], axis=0).astype(BF16)
            o = jnp.where(low, inner[0], inner[1]) + _dot(lhs, rhs)
            mu = head_mean(o)
            oc = o - mu
            var = head_mean(oc * oc)
            on = oc * lax.rsqrt(var + GN_EPS) * gng_ref[:, pair] + gnb_ref[:, pair]
            y = on * _silu(chunk(i, 3 * W_A).astype(F32))
            ya_ref[i * c:(i + 1) * c, pair] = y.astype(BF16)


def _lat_retention(pa, st_f, st_b, layer, decf, decb, gng, gnb):
    lat0 = T_CTX // DEC_SEQ
    st_spec = pl.BlockSpec((1, 1, H_A, HEAD_DIM, HEAD_DIM), lambda b: (b, layer, 0, 0, 0))
    return pl.pallas_call(
        _lat_ret_kernel,
        out_shape=jax.ShapeDtypeStruct((T_LAT, W_A), BF16),
        grid=(DEC_BATCH,),
        in_specs=[
            pl.BlockSpec((DEC_SEQ, PA_W), lambda b: (lat0 + b, 0)),
            st_spec, st_spec,
            pl.BlockSpec((H_A, 1, RET_CHUNK), lambda b: (0, 0, 0)),
            pl.BlockSpec((H_A, 1, RET_CHUNK), lambda b: (0, 0, 0)),
            pl.BlockSpec((1, W_A), lambda b: (0, 0)),
            pl.BlockSpec((1, W_A), lambda b: (0, 0)),
        ],
        out_specs=pl.BlockSpec((DEC_SEQ, W_A), lambda b: (b, 0)),
        compiler_params=pltpu.CompilerParams(
            dimension_semantics=("arbitrary",), vmem_limit_bytes=VMEM_LIMIT),
        name="lat_retention",
    )(pa, st_f, st_b, decf, decb, gng, gnb)


def _rope(x, cos, sin_signed, swap):
    return x.astype(F32) * cos + _dot(x, swap) * sin_signed


def _lat_win_kernel(sink_ref, pq_ref, pseq_ref, kctx_ref, vctx_ref, cos_ref, sin_ref, yb_ref,
                    krope_ref, kc_ref, vc_ref, mask_ref):
    n = pl.program_id(1)
    hd = HEAD_DIM
    qb = WIN_QB
    band = WIN_BAND
    group = H_B // KV_B
    rot = _lane_xor_matrix(LANES, hd // 2)
    swap = _lane_xor_matrix(LANES, hd)

    @pl.when(n == 0)
    def _():
        k = pseq_ref[:, W_B:W_B + KV_W_B]
        krope_ref[...] = _rope(k, cos_ref[...], sin_ref[...], rot).astype(BF16)
        kc_ref[...] = jnp.concatenate([kctx_ref[0, 0, j] for j in range(KV_B)], axis=1).astype(BF16)
        vc_ref[...] = jnp.concatenate([vctx_ref[0, 0, j] for j in range(KV_B)], axis=1).astype(BF16)
        q_in_blk = lax.broadcasted_iota(I32, (group * qb, band), 0) % qb
        k_in_band = lax.broadcasted_iota(I32, (group * qb, band), 1)
        for ty, lead in enumerate((0, WINDOW, band - qb)):
            mask_ref[ty] = jnp.where(jnp.abs(k_in_band - lead - q_in_blk) <= WINDOW, 0.0, NEG_INF)

    q_rows = pl.ds(pl.multiple_of(n * qb, qb), qb)
    cos_q = cos_ref[q_rows, :]
    sin_q = sin_ref[q_rows, :]
    low = lax.broadcasted_iota(I32, (qb, LANES), 1) < hd
    q_heads = []
    for p in range(H_B // 2):
        q_pair = _rope(pq_ref[:, p * LANES:(p + 1) * LANES], cos_q, sin_q, rot) * ATTN_SCALE
        for hh in (2 * p, 2 * p + 1):
            q = jnp.where(low, q_pair, 0.0) if hh % 2 == 0 else jnp.where(low, 0.0, q_pair)
            q = q.astype(BF16)
            q_heads.append(_dot(q, swap).astype(BF16) if hh % 2 != hh // group else q)

    ws = jnp.clip(n * qb - WINDOW, 0, DEC_SEQ - band)
    k_rows = pl.ds(pl.multiple_of(ws, WINDOW), band)
    n_blk = DEC_SEQ // qb
    band_bias = mask_ref[jnp.where(n == 0, 0, jnp.where(n == n_blk - 1, 2, 1))]
    head_of_row = lax.broadcasted_iota(I32, (group * qb, 1), 0) // qb
    kw = krope_ref[k_rows, :]
    vw = pseq_ref[k_rows, W_B + KV_W_B:W_B + 2 * KV_W_B]
    outs = []
    for j in range(KV_B):
        heads = [j * group + g for g in range(group)]
        qs = jnp.concatenate([q_heads[hh] for hh in heads], axis=0)
        s_loc = _dot_nt(qs, kw) + band_bias
        s_ctx = _dot_nt(qs, kc_ref[...])
        sink = jnp.zeros((group * qb, 1), F32)
        for g, hh in enumerate(heads):
            sink = jnp.where(head_of_row == g, sink_ref[hh], sink)
        m = jnp.maximum(jnp.maximum(jnp.max(s_loc, -1, keepdims=True),
                                    jnp.max(s_ctx, -1, keepdims=True)), sink)
        p_loc = jnp.exp(s_loc - m)
        p_ctx = jnp.exp(s_ctx - m)
        den = (jnp.sum(p_loc, -1, keepdims=True) + jnp.sum(p_ctx, -1, keepdims=True)
               + jnp.exp(sink - m))
        o = ((_dot(p_loc.astype(BF16), vw) + _dot(p_ctx.astype(BF16), vc_ref[...])) / den).astype(BF16)
        for g, hh in enumerate(heads):
            o_h = o[g * qb:(g + 1) * qb]
            outs.append(_dot(o_h, swap).astype(BF16) if hh % 2 != j else o_h)
    for p in range(H_B // 2):
        yb_ref[:, p * LANES:(p + 1) * LANES] = jnp.where(low, outs[2 * p], outs[2 * p + 1])


def _lat_window_attn(pb, cache_k, cache_v, layer, sink, cos_t, sin_t):
    n_blk = DEC_SEQ // WIN_QB
    lat_blk0 = T_CTX // WIN_QB
    lat_seq0 = T_CTX // DEC_SEQ
    ctx_spec = pl.BlockSpec((1, 1, KV_B, PAST_LEN, HEAD_DIM), lambda b, n: (b, layer, 0, 0, 0))
    return pl.pallas_call(
        _lat_win_kernel,
        out_shape=jax.ShapeDtypeStruct((T_LAT, W_B), BF16),
        grid=(DEC_BATCH, n_blk),
        in_specs=[
            pl.BlockSpec(memory_space=pltpu.SMEM),
            pl.BlockSpec((WIN_QB, PB_W), lambda b, n: (lat_blk0 + b * n_blk + n, 0)),
            pl.BlockSpec((DEC_SEQ, PB_W), lambda b, n: (lat_seq0 + b, 0)),
            ctx_spec, ctx_spec,
            pl.BlockSpec((DEC_SEQ, LANES), lambda b, n: (0, 0)),
            pl.BlockSpec((DEC_SEQ, LANES), lambda b, n: (0, 0)),
        ],
        out_specs=pl.BlockSpec((WIN_QB, W_B), lambda b, n: (b * n_blk + n, 0)),
        scratch_shapes=[
            pltpu.VMEM((DEC_SEQ, KV_W_B), BF16),
            pltpu.VMEM((PAST_LEN, KV_W_B), BF16),
            pltpu.VMEM((PAST_LEN, KV_W_B), BF16),
            pltpu.VMEM((3, (H_B // KV_B) * WIN_QB, WIN_BAND), F32),
        ],
        compiler_params=pltpu.CompilerParams(
            dimension_semantics=("arbitrary", "arbitrary"), vmem_limit_bytes=VMEM_LIMIT),
        name="lat_window_attn",
    )(sink, pb, pb, cache_k, cache_v, cos_t, sin_t)


NA_Q = NA_TILE_ROWS * GRID_W
NA_K = NA_KEY_ROWS * GRID_W
NA_TILES = DEC_SEQ // NA_Q
LAT_ROWS = DEC_SEQ // GRID_W


def _na_window_start(tile):
    return jnp.clip(tile * NA_TILE_ROWS - NA_ROWS // 2, 0, LAT_ROWS - NA_KEY_ROWS)


def _lat_na_kernel(pq_ref, pseq_ref, kctx_ref, vctx_ref, bias_ref, yc_ref, kc_ref, vc_ref):
    t = pl.program_id(1)
    hd = HEAD_DIM
    nq = NA_Q

    @pl.when(t == 0)
    def _():
        for p in range(H_C // 2):
            kc_ref[p] = jnp.concatenate([kctx_ref[0, 0, 2 * p], kctx_ref[0, 0, 2 * p + 1]], axis=1).astype(BF16)
            vc_ref[p] = jnp.concatenate([vctx_ref[0, 0, 2 * p], vctx_ref[0, 0, 2 * p + 1]], axis=1).astype(BF16)

    k_rows = pl.ds(pl.multiple_of(_na_window_start(t) * GRID_W, GRID_W), NA_K)
    low = lax.broadcasted_iota(I32, (nq, LANES), 1) < hd
    loc, ctx = [], []
    for hh in range(H_C):
        p = hh // 2
        q_pair = pq_ref[:, p * LANES:(p + 1) * LANES] * ATTN_SCALE
        zero = jnp.zeros_like(q_pair)
        q = jnp.where(low, q_pair, zero) if hh % 2 == 0 else jnp.where(low, zero, q_pair)
        loc.append(_dot_nt(q, pseq_ref[k_rows, W_C + p * LANES:W_C + (p + 1) * LANES]) + bias_ref[0, hh])
        ctx.append(_dot_nt(q, kc_ref[p]))
    outs = []
    for g0 in range(0, H_C, NA_STACK):
        s_loc = jnp.concatenate(loc[g0:g0 + NA_STACK], axis=0)
        s_ctx = jnp.concatenate(ctx[g0:g0 + NA_STACK], axis=0)
        m = jnp.maximum(jnp.max(s_loc, -1, keepdims=True), jnp.max(s_ctx, -1, keepdims=True))
        p_loc = jnp.exp(s_loc - m)
        p_ctx = jnp.exp(s_ctx - m)
        den = jnp.sum(p_loc, -1, keepdims=True) + jnp.sum(p_ctx, -1, keepdims=True)
        p_loc = p_loc.astype(BF16)
        p_ctx = p_ctx.astype(BF16)
        for hh in range(g0, g0 + NA_STACK):
            p = hh // 2
            rows = slice((hh - g0) * nq, (hh - g0 + 1) * nq)
            vw = pseq_ref[k_rows, 2 * W_C + p * LANES:2 * W_C + (p + 1) * LANES]
            outs.append((_dot(p_loc[rows], vw) + _dot(p_ctx[rows], vc_ref[p])) / den[rows])
    for p in range(H_C // 2):
        yc_ref[:, p * LANES:(p + 1) * LANES] = jnp.where(low, outs[2 * p], outs[2 * p + 1]).astype(BF16)


def _na_tile_type(t):
    return jnp.where(t == 0, 0, jnp.where(t == NA_TILES - 1, 2, 1))


def _lat_na_attn(pc, cache_k, cache_v, layer, maskbias):
    lat_tile0 = T_CTX // NA_Q
    lat_seq0 = T_CTX // DEC_SEQ
    ctx_spec = pl.BlockSpec((1, 1, H_C, PAST_LEN, HEAD_DIM), lambda b, t: (b, layer, 0, 0, 0))
    return pl.pallas_call(
        _lat_na_kernel,
        out_shape=jax.ShapeDtypeStruct((T_LAT, W_C), BF16),
        grid=(DEC_BATCH, NA_TILES),
        in_specs=[
            pl.BlockSpec((NA_Q, PC_W), lambda b, t: (lat_tile0 + b * NA_TILES + t, 0)),
            pl.BlockSpec((DEC_SEQ, PC_W), lambda b, t: (lat_seq0 + b, 0)),
            ctx_spec, ctx_spec,
            pl.BlockSpec((1, H_C, NA_Q, NA_K), lambda b, t: (_na_tile_type(t), 0, 0, 0)),
        ],
        out_specs=pl.BlockSpec((NA_Q, W_C), lambda b, t: (b * NA_TILES + t, 0)),
        scratch_shapes=[
            pltpu.VMEM((H_C // 2, PAST_LEN, LANES), BF16),
            pltpu.VMEM((H_C // 2, PAST_LEN, LANES), BF16),
        ],
        compiler_params=pltpu.CompilerParams(
            dimension_semantics=("arbitrary", "arbitrary"), vmem_limit_bytes=VMEM_LIMIT),
        name="lat_na_attn",
    )(pc, pc, cache_k, cache_v, maskbias)


def _na_block_index():
    out = np.zeros((3, NA_TILE_ROWS, NA_KEY_ROWS), np.int32)
    for ty, tile in enumerate((0, 1, NA_TILES - 1)):
        r = tile * NA_TILE_ROWS
        ws = int(np.clip(r - NA_ROWS // 2, 0, LAT_ROWS - NA_KEY_ROWS))
        for qq in range(NA_TILE_ROWS):
            qr = r + qq
            r0 = int(np.clip(qr - NA_ROWS // 2, 0, LAT_ROWS - NA_ROWS))
            for kk in range(NA_KEY_ROWS):
                kr = ws + kk
                out[ty, qq, kk] = kr - qr + NA_ROWS - 1 if r0 <= kr < r0 + NA_ROWS else 2 * NA_ROWS - 1
    return out


def _na_maskbias(rpb):
    qc = np.arange(GRID_W)[:, None]
    kc = np.arange(GRID_W)[None, :]
    c0 = np.clip(qc - NA_COLS // 2, 0, GRID_W - NA_COLS)
    col_ok = (kc >= c0) & (kc < c0 + NA_COLS)
    ci = np.clip(kc - qc + NA_COLS - 1, 0, 2 * NA_COLS - 2)
    onehot = (ci[None] == np.arange(2 * NA_COLS - 1)[:, None, None]).astype(np.float32)
    cols = jnp.einsum("hab,bqk->haqk", rpb, jnp.asarray(onehot), precision=lax.Precision.HIGHEST)
    cols = jnp.where(jnp.asarray(col_ok)[None, None], cols, NEG_INF)
    cols = jnp.concatenate([cols, jnp.full((H_C, 1, GRID_W, GRID_W), NEG_INF, F32)], axis=1)
    block_index = _na_block_index()

    def assemble(cols_ref, out_ref):
        for ty in range(3):
            for qq in range(NA_TILE_ROWS):
                for kk in range(NA_KEY_ROWS):
                    out_ref[ty, 0, qq * GRID_W:(qq + 1) * GRID_W, kk * GRID_W:(kk + 1) * GRID_W] = (
                        cols_ref[0, int(block_index[ty, qq, kk])])

    return pl.pallas_call(
        assemble,
        out_shape=jax.ShapeDtypeStruct((3, H_C, NA_Q, NA_K), F32),
        grid=(H_C,),
        in_specs=[pl.BlockSpec((1, 2 * NA_ROWS, GRID_W, GRID_W), lambda h: (h, 0, 0, 0))],
        out_specs=pl.BlockSpec((3, 1, NA_Q, NA_K), lambda h: (0, h, 0, 0)),
        compiler_params=pltpu.CompilerParams(dimension_semantics=("arbitrary",)),
        name="na_bias_assemble",
    )(cols)


def _rope_tables():
    t = np.arange(DEC_SEQ)
    n_freq = HEAD_DIM // 4
    inv = (ROPE_BASE ** (-np.arange(n_freq, dtype=np.float32) / n_freq)).astype(np.float32)
    row = (t // GRID_W).astype(np.float32)[:, None] * inv
    col = (t % GRID_W).astype(np.float32)[:, None] * inv
    ang = np.concatenate([row, col], -1)
    cos, sin = np.cos(ang), np.sin(ang)
    cos_h = np.concatenate([cos, cos], -1)
    sin_h = np.concatenate([-sin, sin], -1)
    reps = LANES // HEAD_DIM
    return (jnp.asarray(np.tile(cos_h, (1, reps)), F32), jnp.asarray(np.tile(sin_h, (1, reps)), F32))


def _first_index_of(mask, iota, sentinel):
    return jnp.min(jnp.where(mask, iota, sentinel), axis=0, keepdims=True)


def _route(logits, b_col):
    n = logits.shape[1]
    scores = jax.nn.sigmoid(logits)
    sel = scores + b_col
    io_g = lax.broadcasted_iota(I32, (GROUP_SIZE, n), 0)
    gs_rows = []
    for g in range(N_GROUPS):
        s = sel[g * GROUP_SIZE:(g + 1) * GROUP_SIZE]
        m1 = jnp.max(s, axis=0, keepdims=True)
        i1 = _first_index_of(s == m1, io_g, GROUP_SIZE)
        m2 = jnp.max(jnp.where(io_g == i1, PICKED, s), axis=0, keepdims=True)
        gs_rows.append(m1 + m2)
    gs = jnp.concatenate(gs_rows, axis=0)
    io_n = lax.broadcasted_iota(I32, (N_GROUPS, n), 0)
    gsel = jnp.zeros((N_GROUPS, n), F32)
    for _ in range(TOPK_GROUPS):
        mg = jnp.max(gs, axis=0, keepdims=True)
        gi = _first_index_of(gs == mg, io_n, N_GROUPS)
        hit = io_n == gi
        gsel = jnp.where(hit, 1.0, gsel)
        gs = jnp.where(hit, PICKED, gs)
    cand = jnp.concatenate(
        [jnp.where(gsel[g:g + 1] > 0.5, sel[g * GROUP_SIZE:(g + 1) * GROUP_SIZE], NEG_INF)
         for g in range(N_GROUPS)], axis=0)
    io_e = lax.broadcasted_iota(I32, (N_EXPERTS, n), 0)
    picks, raw = [], []
    for _ in range(TOP_K):
        mv = jnp.max(cand, axis=0, keepdims=True)
        ei = _first_index_of(cand == mv, io_e, N_EXPERTS)
        hit = io_e == ei
        picks.append((hit, ei))
        raw.append(jnp.sum(jnp.where(hit, scores, 0.0), axis=0, keepdims=True))
        cand = jnp.where(hit, PICKED, cand)
    return picks, raw


def _post_mixer_kernel(xc_ref, xl_ref, mixc_ref, ya_ref, yb_ref, yc_ref, wout_ref, mod_ref, g_ref, b_ref,
                       wr_ref, br_ref,
                       x1_ref, h2_ref, eidx_ref, wsel_ref, rank_ref, cnt_ref):
    step = pl.program_id(0)
    tm = TM_TOK
    is_ctx = step < T_CTX // (POST_TILES * tm)

    @pl.when(step == 0)
    def _():
        cnt_ref[...] = jnp.zeros_like(cnt_ref)

    def project(s):
        rows = slice(s * tm, (s + 1) * tm)
        ci = _cond_row(step * POST_TILES + s, tm)
        gate1 = mod_ref[pl.ds(ci, 1), 2 * D_MODEL:3 * D_MODEL]
        sh2 = mod_ref[pl.ds(ci, 1), 3 * D_MODEL:4 * D_MODEL]
        sc2 = mod_ref[pl.ds(ci, 1), 4 * D_MODEL:5 * D_MODEL]
        mix_lat = jnp.concatenate([ya_ref[rows, :], yb_ref[rows, :], yc_ref[rows, :]], axis=-1)
        mix = jnp.where(is_ctx, mixc_ref[rows, :], mix_lat)
        y = _dot(mix, wout_ref[...])
        x = jnp.where(is_ctx, xc_ref[rows, :], xl_ref[rows, :])
        x1 = _layer_norm(ALPHA * x + gate1 * y, g_ref[...], b_ref[...])
        x1_ref[rows, :] = x1
        h2 = x1 * (1.0 + sc2) + sh2
        h_hi = h2.astype(BF16)
        h2_ref[rows, :] = h_hi
        h_lo = (h2 - h_hi.astype(F32)).astype(BF16)
        both = (_dot(h_hi, wr_ref[...]) + _dot(h_lo, wr_ref[...])).T
        return both[0:N_EXPERTS] + both[N_EXPERTS:2 * N_EXPERTS]

    def route(s, logits):
        toks = s * tm
        routed = [_route(logits[:, g * LANES:(g + 1) * LANES], br_ref[...]) for g in range(tm // LANES)]
        multi_g = []
        for picks, _ in routed:
            m = jnp.zeros((N_EXPERTS, LANES), F32)
            for hit, _ in picks:
                m = m + jnp.where(hit, 1.0, 0.0)
            multi_g.append(m)
        multi = jnp.concatenate(multi_g, axis=1)
        before = (lax.broadcasted_iota(I32, (tm, tm), 0) < lax.broadcasted_iota(I32, (tm, tm), 1))
        cum = _dot(multi.astype(BF16), jnp.where(before, 1.0, 0.0).astype(BF16))
        pad = jnp.zeros((SUBLANES - TOP_K, LANES), F32)
        for g, (picks, raw) in enumerate(routed):
            lanes = slice(toks + g * LANES, toks + (g + 1) * LANES)
            total = raw[0]
            for r in raw[1:]:
                total = total + r
            scale = ROUTE_SCALE / total
            cum_g = cum[:, g * LANES:(g + 1) * LANES]
            eidx_ref[:, lanes] = jnp.concatenate([ei for _, ei in picks] + [pad.astype(I32)], axis=0)
            wsel_ref[:, lanes] = jnp.concatenate([r * scale for r in raw] + [pad], axis=0)
            rank_ref[:, lanes] = jnp.concatenate(
                [jnp.sum(jnp.where(hit, cum_g, 0.0), axis=0, keepdims=True) for hit, _ in picks] + [pad],
                axis=0).astype(I32)
        tile_lane = lax.broadcasted_iota(I32, (N_EXPERTS, LANES), 1)
        cnt_ref[...] = jnp.where(tile_lane == step * POST_TILES + s,
                                 jnp.sum(multi, axis=1, keepdims=True), cnt_ref[...])

    logits = [project(s) for s in range(POST_TILES)]
    for s in range(POST_TILES):
        route(s, logits[s])


def _post_mixer(x_ctx, x_lat, mix_c, ya, yb, yc, w_out_bf16, mod, ln_g, ln_b, wr_split, b_router_col):
    tm = POST_TILES * TM_TOK
    n_ctx = T_CTX // tm
    ctx_map = lambda i: (jnp.minimum(i, n_ctx - 1), 0)
    lat_map = lambda i: (jnp.maximum(i - n_ctx, 0), 0)
    row_map = lambda i: (i, 0)
    const = lambda i: (0, 0)
    tok_map = lambda i: (0, i)
    return pl.pallas_call(
        _post_mixer_kernel,
        out_shape=(
            jax.ShapeDtypeStruct((T_ALL, D_MODEL), F32),
            jax.ShapeDtypeStruct((T_ALL, D_MODEL), BF16),
            jax.ShapeDtypeStruct((SUBLANES, T_ALL), I32),
            jax.ShapeDtypeStruct((SUBLANES, T_ALL), F32),
            jax.ShapeDtypeStruct((SUBLANES, T_ALL), I32),
            jax.ShapeDtypeStruct((N_EXPERTS, LANES), F32),
        ),
        grid=(T_ALL // tm,),
        in_specs=[
            pl.BlockSpec((tm, D_MODEL), ctx_map),
            pl.BlockSpec((tm, D_MODEL), lat_map),
            pl.BlockSpec((tm, D_MODEL), ctx_map),
            pl.BlockSpec((tm, W_A), lat_map),
            pl.BlockSpec((tm, W_B), lat_map),
            pl.BlockSpec((tm, W_C), lat_map),
            pl.BlockSpec((D_MODEL, D_MODEL), const),
            pl.BlockSpec((N_COND, 6 * D_MODEL), const),
            pl.BlockSpec((1, D_MODEL), const),
            pl.BlockSpec((1, D_MODEL), const),
            pl.BlockSpec((D_MODEL, 2 * N_EXPERTS), const),
            pl.BlockSpec((N_EXPERTS, 1), const),
        ],
        out_specs=(
            pl.BlockSpec((tm, D_MODEL), row_map),
            pl.BlockSpec((tm, D_MODEL), row_map),
            pl.BlockSpec((SUBLANES, tm), tok_map),
            pl.BlockSpec((SUBLANES, tm), tok_map),
            pl.BlockSpec((SUBLANES, tm), tok_map),
            pl.BlockSpec((N_EXPERTS, LANES), const),
        ),
        compiler_params=pltpu.CompilerParams(
            dimension_semantics=("arbitrary",), vmem_limit_bytes=VMEM_LIMIT),
        name="post_mixer",
    )(x_ctx, x_lat, mix_c, ya, yb, yc, w_out_bf16, mod, ln_g, ln_b, wr_split, b_router_col)


PLAN_TILES = 4


def _plan_kernel(eidx_ref, rank_ref, nmat_ref, lslot_ref, unit_ref, gend_ref):
    step = pl.program_id(0)
    units = jnp.floor((nmat_ref[...] + (UNIT - 1.0)) * (1.0 / UNIT))
    units_bf = units.astype(BF16)
    earlier_e = (lax.broadcasted_iota(I32, (N_EXPERTS, N_EXPERTS), 1)
                 < lax.broadcasted_iota(I32, (N_EXPERTS, N_EXPERTS), 0))
    tri_e = jnp.where(earlier_e, 1.0, 0.0).astype(BF16)
    earlier_t = (lax.broadcasted_iota(I32, (LANES, LANES), 0) < lax.broadcasted_iota(I32, (LANES, LANES), 1))
    tri_t = jnp.where(earlier_t, 1.0, 0.0).astype(BF16)
    local_off = _dot(tri_e, units_bf)
    tile_off = _dot(units_bf, tri_t)
    per_expert = jnp.sum(units, axis=1, keepdims=True)
    blocks = jnp.floor((per_expert + (UNITS_PER_BLOCK - 1.0)) * (1.0 / UNITS_PER_BLOCK))
    blocks_l = jnp.broadcast_to(blocks, (N_EXPERTS, LANES))
    start_blk = _dot(tri_e, blocks_l.astype(BF16))
    end_blk = start_blk + blocks_l
    gend_ref[...] = (end_blk * BM).astype(I32)

    tile_lane = lax.broadcasted_iota(I32, (N_EXPERTS, LANES), 1)
    u = lax.broadcasted_iota(I32, (N_EXPERTS, MAX_UNITS), 1).astype(F32)
    io_e = lax.broadcasted_iota(I32, (N_EXPERTS, TM_TOK), 0)
    for s in range(PLAN_TILES):
        i = step * PLAN_TILES + s
        this_tile = tile_lane == i

        def column(a):
            return jnp.sum(jnp.where(this_tile, a, 0.0), axis=1, keepdims=True)

        lo, n_u = column(local_off), column(units)
        base_unit = start_blk[:, 0:1] * UNITS_PER_BLOCK + column(tile_off) - lo
        inside = jnp.where(u >= lo, jnp.where(u < lo + n_u, 1.0, 0.0), 0.0)
        dst_unit = jnp.sum(inside * (base_unit + u), axis=0, keepdims=True)
        used = jnp.sum(inside, axis=0, keepdims=True) > 0.5
        spare = (SPARE_UNIT0 + (i % 2) * MAX_UNITS).astype(F32) + u[0:1, :]
        unit_ref[s] = jnp.where(used, dst_unit, spare).astype(I32)

        toks = slice(s * TM_TOK, (s + 1) * TM_TOK)
        rows = []
        for k in range(TOP_K):
            hit = io_e == eidx_ref[k:k + 1, toks]
            seg = jnp.sum(jnp.where(hit, lo * UNIT, 0.0), axis=0, keepdims=True)
            rows.append(seg.astype(I32) + rank_ref[k:k + 1, toks])
        rows.append(jnp.full((SUBLANES - TOP_K, TM_TOK), -1, I32))
        lslot_ref[:, toks] = jnp.concatenate(rows, axis=0)


def _slot_plan(eidx, rank, nmat):
    tok_map = lambda i: (0, i)
    const = lambda i: (0, 0)
    lslot, unit_tab, gend = pl.pallas_call(
        _plan_kernel,
        out_shape=(
            jax.ShapeDtypeStruct((SUBLANES, T_ALL), I32),
            jax.ShapeDtypeStruct((N_TOK_TILES, 1, MAX_UNITS), I32),
            jax.ShapeDtypeStruct((N_EXPERTS, LANES), I32),
        ),
        grid=(N_TOK_TILES // PLAN_TILES,),
        in_specs=[
            pl.BlockSpec((SUBLANES, PLAN_TILES * TM_TOK), tok_map),
            pl.BlockSpec((SUBLANES, PLAN_TILES * TM_TOK), tok_map),
            pl.BlockSpec((N_EXPERTS, LANES), const),
        ],
        out_specs=(
            pl.BlockSpec((SUBLANES, PLAN_TILES * TM_TOK), tok_map),
            pl.BlockSpec((PLAN_TILES, 1, MAX_UNITS), lambda i: (i, 0, 0)),
            pl.BlockSpec((N_EXPERTS, LANES), const),
        ),
        compiler_params=pltpu.CompilerParams(dimension_semantics=("arbitrary",)),
        name="slot_plan",
    )(eidx, rank, nmat)
    return lslot, unit_tab.reshape(N_TOK_TILES * MAX_UNITS), gend[:, 0]


PACK_W = D_MODEL // 2
HI_HALF = -65536


def _pack_pairs(x):
    lo = lax.bitcast_convert_type(x[:, 0:PACK_W], I32)
    hi = lax.bitcast_convert_type(x[:, PACK_W:D_MODEL], I32)
    return lax.shift_right_logical(lo, 16) | (hi & HI_HALF)


def _unpack_pairs(u):
    lo = lax.bitcast_convert_type(lax.shift_left(u, 16), F32).astype(BF16)
    hi = lax.bitcast_convert_type(u & HI_HALF, F32).astype(BF16)
    return lo, hi


def _unit_rows(unit):
    row = unit * UNIT
    return pl.ds(row if isinstance(unit, int) else pl.multiple_of(row, UNIT), UNIT)


def _unit_copy(src, src_unit, dst, dst_unit, sem):
    return pltpu.make_async_copy(src.at[_unit_rows(src_unit)], dst.at[_unit_rows(dst_unit)], sem)


def _dispatch_kernel(fresh, gend_ref, tab_ref, h2_ref, lslot_ref, *refs):
    xs_hbm, zero_ref, local_ref, sem_zero, sem_rows = refs if fresh else refs[1:]
    i = pl.program_id(0)
    buf = i % 2

    def drain(b):
        pltpu.make_async_copy(local_ref.at[b], xs_hbm.at[pl.ds(0, LOCAL_ROWS)], sem_rows.at[b]).wait()

    def has_rows(e):
        return gend_ref[e] > jnp.where(e == 0, 0, gend_ref[jnp.maximum(e - 1, 0)])

    def zero_copy(e):
        return pltpu.make_async_copy(
            zero_ref, xs_hbm.at[pl.ds(pl.multiple_of(gend_ref[e] - BM, BM), BM)], sem_zero)

    @pl.when(jnp.logical_and(i == 0, fresh))
    def _():
        zero_ref[...] = jnp.zeros_like(zero_ref)

        def start(e, c):
            @pl.when(has_rows(e))
            def _():
                zero_copy(e).start()
            return c

        def wait(e, c):
            @pl.when(has_rows(e))
            def _():
                zero_copy(e).wait()
            return c

        def tail_copy(blk):
            return pltpu.make_async_copy(
                zero_ref, xs_hbm.at[pl.ds(pl.multiple_of(blk * BM, BM), BM)], sem_zero)

        def start_tail(blk, c):
            tail_copy(blk).start()
            return c

        def wait_tail(blk, c):
            tail_copy(blk).wait()
            return c

        n_used = gend_ref[N_EXPERTS - 1] // BM
        lax.fori_loop(0, N_EXPERTS, start, 0)
        lax.fori_loop(n_used, N_BLOCKS_ALL, start_tail, 0)
        lax.fori_loop(0, N_EXPERTS, wait, 0)
        lax.fori_loop(n_used, N_BLOCKS_ALL, wait_tail, 0)

    @pl.when(i >= 2)
    def _():
        drain(buf)

    h2 = h2_ref[...]
    units_per_chunk = PERM_CHUNK // UNIT
    local = local_ref.at[buf]
    for c in range(LOCAL_ROWS // PERM_CHUNK):
        slot = c * PERM_CHUNK + lax.broadcasted_iota(I32, (PERM_CHUNK, TM_TOK), 0)
        p = jnp.zeros((PERM_CHUNK, TM_TOK), F32)
        for k in range(TOP_K):
            p = jnp.where(slot == lslot_ref[k:k + 1, :], 1.0, p)
        local[c * PERM_CHUNK:(c + 1) * PERM_CHUNK, :] = _pack_pairs(_dot(p.astype(BF16), h2))
        for u in range(c * units_per_chunk, (c + 1) * units_per_chunk):
            _unit_copy(local, u, xs_hbm, tab_ref[i * MAX_UNITS + u], sem_rows.at[buf]).start()

    @pl.when(i == N_TOK_TILES - 1)
    def _():
        drain(1 - buf)
        drain(buf)


def _dispatch(h2, lslot, unit_tab, gend, xs_prev):
    fresh = xs_prev is None
    n_prefetch = 2
    return pl.pallas_call(
        functools.partial(_dispatch_kernel, fresh),
        out_shape=jax.ShapeDtypeStruct((N_SLOTS, PACK_W), I32),
        input_output_aliases={} if fresh else {n_prefetch + 2: 0},
        grid_spec=pltpu.PrefetchScalarGridSpec(
            num_scalar_prefetch=n_prefetch,
            grid=(N_TOK_TILES,),
            in_specs=[
                pl.BlockSpec((TM_TOK, D_MODEL), lambda i, ge, tab: (i, 0)),
                pl.BlockSpec((SUBLANES, TM_TOK), lambda i, ge, tab: (0, i)),
            ] + ([] if fresh else [pl.BlockSpec(memory_space=pl.ANY)]),
            out_specs=pl.BlockSpec(memory_space=pl.ANY),
            scratch_shapes=[
                pltpu.VMEM((BM, PACK_W), I32),
                pltpu.VMEM((2, LOCAL_ROWS, PACK_W), I32),
                pltpu.SemaphoreType.DMA,
                pltpu.SemaphoreType.DMA((2,)),
            ],
        ),
        compiler_params=pltpu.CompilerParams(
            dimension_semantics=("arbitrary",), vmem_limit_bytes=VMEM_LIMIT),
        name="moe_dispatch",
    )(gend, unit_tab, h2, lslot, *(() if fresh else (xs_prev,)))


def _expert_kernel(fresh, gend_ref, xs_hbm, wgu_ref, wdown_ref, *refs):
    ys_hbm, xbuf, ybuf, wgu_bf, wdown_bf, sem_in, sem_out = refs if fresh else refs[1:]
    e = pl.program_id(0)
    first = jnp.where(e == 0, 0, gend_ref[jnp.maximum(e - 1, 0)]) // BM
    last = gend_ref[e] // BM
    n_used = gend_ref[N_EXPERTS - 1] // BM

    def rows_of(blk):
        return pl.ds(pl.multiple_of(blk * BM, BM), BM)

    def in_copy(blk):
        return pltpu.make_async_copy(xs_hbm.at[rows_of(blk)], xbuf.at[blk % X_BUFS], sem_in.at[blk % X_BUFS])

    def out_copy(blk):
        return pltpu.make_async_copy(ybuf.at[blk % 2], ys_hbm.at[rows_of(blk)], sem_out.at[blk % 2])

    @pl.when(e == 0)
    def _():
        for a in range(X_AHEAD):
            @pl.when(a < n_used)
            def _(a=a):
                in_copy(a).start()

    @pl.when(last > first)
    def _():
        wgu_bf[...] = wgu_ref[0, 0].astype(BF16)
        wdown_bf[...] = wdown_ref[0, 0].astype(BF16)

    def block(blk, carry):
        @pl.when(blk + X_AHEAD < n_used)
        def _():
            in_copy(blk + X_AHEAD).start()

        in_copy(blk).wait()

        @pl.when(blk >= 2)
        def _():
            out_copy(blk - 2).wait()

        x_lo, x_hi = _unpack_pairs(xbuf[blk % X_BUFS])
        gu = _dot(x_lo, wgu_bf[0:PACK_W, :]) + _dot(x_hi, wgu_bf[PACK_W:D_MODEL, :])
        act = _silu(gu[:, 0:D_EXPERT]) * gu[:, D_EXPERT:2 * D_EXPERT]
        y = _dot(act.astype(BF16), wdown_bf[...])
        ybuf[blk % 2] = _pack_pairs(y.astype(BF16).astype(F32))
        out_copy(blk).start()
        return carry

    lax.fori_loop(first, last, block, 0)

    @pl.when(e == N_EXPERTS - 1)
    def _():
        @pl.when(n_used >= 2)
        def _():
            out_copy(n_used - 2).wait()

        @pl.when(n_used >= 1)
        def _():
            out_copy(n_used - 1).wait()

        if fresh:
            ybuf[0] = jnp.zeros((BM, PACK_W), I32)

            def tail_copy(blk):
                return pltpu.make_async_copy(ybuf.at[0], ys_hbm.at[rows_of(blk)], sem_out.at[0])

            def start(blk, c):
                tail_copy(blk).start()
                return c

            def wait(blk, c):
                tail_copy(blk).wait()
                return c

            lax.fori_loop(n_used, N_BLOCKS_ALL, start, 0)
            lax.fori_loop(n_used, N_BLOCKS_ALL, wait, 0)


def _experts(xs, w_gu, w_down, layer, gend, ys_prev):
    fresh = ys_prev is None
    n_prefetch = 1
    return pl.pallas_call(
        functools.partial(_expert_kernel, fresh),
        out_shape=jax.ShapeDtypeStruct((N_SLOTS, PACK_W), I32),
        input_output_aliases={} if fresh else {n_prefetch + 3: 0},
        grid_spec=pltpu.PrefetchScalarGridSpec(
            num_scalar_prefetch=n_prefetch,
            grid=(N_EXPERTS,),
            in_specs=[
                pl.BlockSpec(memory_space=pl.ANY),
                pl.BlockSpec((1, 1, D_MODEL, 2 * D_EXPERT), lambda e, ge: (layer, e, 0, 0)),
                pl.BlockSpec((1, 1, D_EXPERT, D_MODEL), lambda e, ge: (layer, e, 0, 0)),
            ] + ([] if fresh else [pl.BlockSpec(memory_space=pl.ANY)]),
            out_specs=pl.BlockSpec(memory_space=pl.ANY),
            scratch_shapes=[
                pltpu.VMEM((X_BUFS, BM, PACK_W), I32),
                pltpu.VMEM((2, BM, PACK_W), I32),
                pltpu.VMEM((D_MODEL, 2 * D_EXPERT), BF16),
                pltpu.VMEM((D_EXPERT, D_MODEL), BF16),
                pltpu.SemaphoreType.DMA((X_BUFS,)),
                pltpu.SemaphoreType.DMA((2,)),
            ],
        ),
        compiler_params=pltpu.CompilerParams(
            dimension_semantics=("arbitrary",), vmem_limit_bytes=VMEM_LIMIT),
        name="moe_experts",
    )(gend, xs, w_gu, w_down, *(() if fresh else (ys_prev,)))


def _combine_kernel(tab_ref, x1_ref, h2_ref, lslot_ref, wsel_ref, ys_hbm, wsgu_ref, wsdown_ref, mod_ref,
                    g_ref, b_ref, outc_ref, outl_ref, local_ref, sel_ref, ylo_ref, yhi_ref, sem_rows):
    i = pl.program_id(0)
    tm = TM_TOK
    buf = i % 2

    def fetch_unit(tile, b, u):
        _unit_copy(ys_hbm, tab_ref[tile * MAX_UNITS + u], local_ref.at[b], u, sem_rows.at[b]).start()

    def drain(b):
        pltpu.make_async_copy(ys_hbm.at[pl.ds(0, LOCAL_ROWS)], local_ref.at[b], sem_rows.at[b]).wait()

    @pl.when(i == 0)
    def _():
        def body(u, c):
            fetch_unit(0, 0, u)
            return c

        lax.fori_loop(0, MAX_UNITS, body, 0, unroll=8)

    sgu = _dot(h2_ref[...], wsgu_ref[...])
    act = _silu(sgu[:, 0:D_SHARED]) * sgu[:, D_SHARED:2 * D_SHARED]
    f = _dot(act.astype(BF16), wsdown_ref[...])

    nxt = jnp.minimum(i + 1, N_TOK_TILES - 1)
    n_groups = tm // SEL_ROWS
    units_per_group = MAX_UNITS // n_groups
    slot = lax.broadcasted_iota(I32, (SEL_ROWS, LOCAL_ROWS), 1)
    for g in range(n_groups):
        rows = slice(g * SEL_ROWS, (g + 1) * SEL_ROWS)
        sel = jnp.zeros((SEL_ROWS, LOCAL_ROWS), F32)
        for k in range(TOP_K):
            sel = jnp.where(slot == lslot_ref[rows, k:k + 1], wsel_ref[rows, k:k + 1], sel)
        sel_ref[rows, :] = sel.astype(BF16)
        for u in range(g * units_per_group, (g + 1) * units_per_group):
            fetch_unit(nxt, 1 - buf, u)

    local = local_ref.at[buf]
    drain(buf)
    for c in range(LOCAL_ROWS // PERM_CHUNK):
        rows = slice(c * PERM_CHUNK, (c + 1) * PERM_CHUNK)
        ylo_ref[rows, :], yhi_ref[rows, :] = _unpack_pairs(local[rows, :])
    sel = sel_ref[...]
    f = f + jnp.concatenate([_dot(sel, ylo_ref[...]), _dot(sel, yhi_ref[...])], axis=-1)
    ci = _cond_row(i, tm)
    gate2 = mod_ref[pl.ds(ci, 1), 5 * D_MODEL:6 * D_MODEL]
    out = _layer_norm(ALPHA * x1_ref[...] + gate2 * f, g_ref[...], b_ref[...])

    @pl.when(i < T_CTX // tm)
    def _():
        outc_ref[...] = out

    @pl.when(i >= T_CTX // tm)
    def _():
        outl_ref[...] = out

    @pl.when(i == N_TOK_TILES - 1)
    def _():
        drain(1 - buf)


def _combine(x1, h2, lslot_rows, wsel_rows, unit_tab, ys, w_sgu_bf16, w_sdown_bf16, mod, ln_g, ln_b):
    n_ctx = T_CTX // TM_TOK
    row_map = lambda i, tab: (i, 0)
    const = lambda i, tab: (0, 0)
    return pl.pallas_call(
        _combine_kernel,
        out_shape=(jax.ShapeDtypeStruct((T_CTX, D_MODEL), F32),
                   jax.ShapeDtypeStruct((T_LAT, D_MODEL), F32)),
        grid_spec=pltpu.PrefetchScalarGridSpec(
            num_scalar_prefetch=1,
            grid=(N_TOK_TILES,),
            in_specs=[
                pl.BlockSpec((TM_TOK, D_MODEL), row_map),
                pl.BlockSpec((TM_TOK, D_MODEL), row_map),
                pl.BlockSpec((TM_TOK, SUBLANES), row_map),
                pl.BlockSpec((TM_TOK, SUBLANES), row_map),
                pl.BlockSpec(memory_space=pl.ANY),
                pl.BlockSpec((D_MODEL, 2 * D_SHARED), const),
                pl.BlockSpec((D_SHARED, D_MODEL), const),
                pl.BlockSpec((N_COND, 6 * D_MODEL), const),
                pl.BlockSpec((1, D_MODEL), const),
                pl.BlockSpec((1, D_MODEL), const),
            ],
            out_specs=(pl.BlockSpec((TM_TOK, D_MODEL), lambda i, tab: (jnp.minimum(i, n_ctx - 1), 0)),
                       pl.BlockSpec((TM_TOK, D_MODEL), lambda i, tab: (jnp.maximum(i - n_ctx, 0), 0))),
            scratch_shapes=[
                pltpu.VMEM((2, LOCAL_ROWS, PACK_W), I32),
                pltpu.VMEM((TM_TOK, LOCAL_ROWS), BF16),
                pltpu.VMEM((LOCAL_ROWS, PACK_W), BF16),
                pltpu.VMEM((LOCAL_ROWS, PACK_W), BF16),
                pltpu.SemaphoreType.DMA((2,)),
            ],
        ),
        compiler_params=pltpu.CompilerParams(
            dimension_semantics=("arbitrary",), vmem_limit_bytes=VMEM_LIMIT),
        name="moe_combine",
    )(unit_tab, x1, h2, lslot_rows, wsel_rows, ys, w_sgu_bf16, w_sdown_bf16, mod, ln_g, ln_b)


def _lane_rows(v, width):
    return jnp.broadcast_to(v.astype(F32)[:, None, None], (v.shape[0], 1, width))


def kernel(x_prompt, x_sample, state_ret_fwd, state_ret_bwd, cache_win_k, cache_win_v, cache_na_k, cache_na_v, c, c_ctx, w_in, w_out, ret_decay_fwd, ret_decay_bwd, ret_gn_g, ret_gn_b, win_sink, na_rpb, w_mod, b_mod, ln1_g, ln1_b, ln2_g, ln2_b, w_router, b_router, w_expert_gu, w_expert_down, w_shared_gu, w_shared_down):
    cond = jnp.concatenate(
        [c_ctx[None, :], c, jnp.zeros((N_COND - 1 - DEC_BATCH, D_MODEL), F32)], axis=0)
    mod_all = _modulation(cond, w_mod, b_mod)
    cos_t, sin_t = _rope_tables()

    x_ctx = x_prompt.reshape(T_CTX, D_MODEL)
    x_lat = x_sample.reshape(T_LAT, D_MODEL)
    sf_l, sb_l, caches = [], [], ()
    xs = ys = None
    for l in range(DEPTH):
        mod = mod_all[l]
        pa, pb, pc, *caches = _in_projection(x_ctx, x_lat, mod, w_in[l].astype(BF16), tuple(caches))
        decf_s, decb_s = _lane_rows(ret_decay_fwd[l], SEQ), _lane_rows(ret_decay_bwd[l], SEQ)
        gng, gnb = ret_gn_g[l][None, :], ret_gn_b[l][None, :]
        mix_c, st_f, st_b = _ctx_mixers(pa, pb, pc, win_sink[l], decf_s, decb_s, gng, gnb)
        sf_l.append(st_f)
        sb_l.append(st_b)
        ya = _lat_retention(pa, state_ret_fwd, state_ret_bwd, l,
                            _lane_rows(ret_decay_fwd[l], RET_CHUNK), _lane_rows(ret_decay_bwd[l], RET_CHUNK),
                            gng, gnb)
        yb = _lat_window_attn(pb, cache_win_k, cache_win_v, l, win_sink[l], cos_t, sin_t)
        yc = _lat_na_attn(pc, cache_na_k, cache_na_v, l, _na_maskbias(na_rpb[l]))

        wr_hi = w_router[l].astype(BF16)
        wr_lo = (w_router[l] - wr_hi.astype(F32)).astype(BF16)
        x1, h2, eidx, wsel, rank, counts = _post_mixer(
            x_ctx, x_lat, mix_c, ya, yb, yc, w_out[l].astype(BF16), mod, ln1_g[l][None, :],
            ln1_b[l][None, :], jnp.concatenate([wr_hi, wr_lo], axis=1), b_router[l][:, None])
        lslot, unit_tab, gend = _slot_plan(eidx, rank, counts)
        xs = _dispatch(h2, lslot, unit_tab, gend, xs)
        ys = _experts(xs, w_expert_gu, w_expert_down, l, gend, ys)
        x_ctx, x_lat = _combine(x1, h2, lslot.T, wsel.T, unit_tab, ys, w_shared_gu[l].astype(BF16),
                                w_shared_down[l].astype(BF16), mod, ln2_g[l][None, :], ln2_b[l][None, :])

    y_prompt = x_ctx.reshape(BATCH, SEQ, D_MODEL)
    y_sample = x_lat.reshape(DEC_BATCH, DEC_SEQ, D_MODEL)
    new_sf = jnp.stack(sf_l, axis=1)
    new_sb = jnp.stack(sb_l, axis=1)

    return (y_prompt, y_sample, new_sf, new_sb, *caches)
```

```python
import functools

import numpy as np
import jax
import jax.numpy as jnp
from jax import lax
from jax.experimental import pallas as pl
from jax.experimental.pallas import tpu as pltpu

F32 = jnp.float32
BF16 = jnp.bfloat16
I32 = jnp.int32

D_MODEL = 1024
BATCH = 32
SEQ = 256
DEPTH = 2
DEC_BATCH = 4
DEC_SEQ = 2048
PAST_LEN = 256
GRID_W = 64
HEAD_DIM = 64
ATTN_SCALE = HEAD_DIM ** -0.5
H_A = 4
W_A = H_A * HEAD_DIM
GN_EPS = 1e-5
H_B = 6
KV_B = 2
W_B = H_B * HEAD_DIM
KV_W_B = KV_B * HEAD_DIM
WINDOW = 128
ROPE_BASE = 10000.0
H_C = 6
W_C = H_C * HEAD_DIM
NA_ROWS = 8
NA_COLS = 16
IN_WIDTH = 4 * W_A + W_B + 2 * KV_W_B + 3 * W_C
N_EXPERTS = 64
TOP_K = 6
N_GROUPS = 8
GROUP_SIZE = N_EXPERTS // N_GROUPS
TOPK_GROUPS = 4
D_EXPERT = 256
D_SHARED = 256
ROUTE_SCALE = 2.5
ALPHA = (2 * DEPTH) ** 0.25
LN_EPS = 1e-5
NEG_INF = -1e30
PICKED = -3e38

T_CTX = BATCH * SEQ
T_LAT = DEC_BATCH * DEC_SEQ
T_ALL = T_CTX + T_LAT
N_COND = 8

PA_W = 4 * W_A
PB_W = W_B + 2 * KV_W_B
PC_W = 3 * W_C

LANES = 128
SUBLANES = 8
VMEM_LIMIT = 56 * 1024 * 1024

TM_PROJ = 512
TM_TOK = 256
RET_CHUNK = 256
POST_TILES = 4
CTX_SEQS = 4
CTX_STACK = 3
NA_STACK = 6
WIN_QB = 256
WIN_BAND = WIN_QB + 2 * WINDOW
NA_TILE_ROWS = 4
NA_KEY_ROWS = 11
BM = 512
X_AHEAD = 3
X_BUFS = X_AHEAD + 1
UNIT = SUBLANES
UNITS_PER_BLOCK = BM // UNIT
PERM_CHUNK = 256
SEL_ROWS = 16
N_TOK_TILES = T_ALL // TM_TOK
LOCAL_ROWS = -(-(TM_TOK * TOP_K + N_EXPERTS * (UNIT - 1)) // PERM_CHUNK) * PERM_CHUNK
MAX_UNITS = LOCAL_ROWS // UNIT
N_ASSIGN = T_ALL * TOP_K
N_BLOCKS = -(-(N_ASSIGN + N_TOK_TILES * N_EXPERTS * (UNIT - 1) + N_EXPERTS * (BM - 1)) // BM)
SPARE_BLOCKS = 2 * -(-LOCAL_ROWS // BM)
SPARE_UNIT0 = N_BLOCKS * UNITS_PER_BLOCK
N_BLOCKS_ALL = N_BLOCKS + SPARE_BLOCKS
N_SLOTS = N_BLOCKS_ALL * BM


def _dot(a, b):
    return jnp.dot(a, b, preferred_element_type=F32)


def _dot_nt(a, b):
    return lax.dot_general(a, b, (((1,), (1,)), ((), ())), preferred_element_type=F32)


def _silu(x):
    return x * jax.nn.sigmoid(x)


def _log_sigmoid(x):
    return jnp.minimum(x, 0.0) - jnp.log(1.0 + jnp.exp(-jnp.abs(x)))


def _cond_row(tile, tile_rows):
    n_ctx = T_CTX // tile_rows
    per_lat = DEC_SEQ // tile_rows
    return jnp.where(tile < n_ctx, 0, 1 + (tile - n_ctx) // per_lat)


def _layer_norm(x, g, b):
    mu = jnp.mean(x, -1, keepdims=True)
    xc = x - mu
    var = jnp.mean(xc * xc, -1, keepdims=True)
    return xc * lax.rsqrt(var + LN_EPS) * g + b


MOD_TN = 1536


def _mod_kernel(cond_ref, w_ref, b_ref, o_ref):
    s = _silu(cond_ref[...])
    s_hi = s.astype(BF16)
    s_lo = (s - s_hi.astype(F32)).astype(BF16)
    w = w_ref[0]
    w_hi = w.astype(BF16)
    w_lo = (w - w_hi.astype(F32)).astype(BF16)
    o_ref[0] = _dot(s_hi, w_hi) + _dot(s_lo, w_hi) + _dot(s_hi, w_lo) + b_ref[0]


def _modulation(cond, w_mod, b_mod):
    n_out = 6 * D_MODEL
    return pl.pallas_call(
        _mod_kernel,
        out_shape=jax.ShapeDtypeStruct((DEPTH, N_COND, n_out), F32),
        grid=(DEPTH, n_out // MOD_TN),
        in_specs=[
            pl.BlockSpec((N_COND, D_MODEL), lambda l, j: (0, 0)),
            pl.BlockSpec((1, D_MODEL, MOD_TN), lambda l, j: (l, 0, j)),
            pl.BlockSpec((1, 1, MOD_TN), lambda l, j: (l, 0, j)),
        ],
        out_specs=pl.BlockSpec((1, N_COND, MOD_TN), lambda l, j: (l, 0, j)),
        compiler_params=pltpu.CompilerParams(
            dimension_semantics=("arbitrary", "arbitrary"), vmem_limit_bytes=VMEM_LIMIT),
        name="modulation",
    )(cond, w_mod, b_mod.reshape(DEPTH, 1, n_out))


SEQ_PER_PROJ = TM_PROJ // SEQ


def _inproj_kernel(n_prev, xc_ref, xl_ref, mod_ref, w_ref, *refs):
    prev_refs = refs[:4] if n_prev else ()
    pa_ref, pb_ref, pc_ref, wk_ref, wv_ref, nk_ref, nv_ref = refs[len(prev_refs):]
    i = pl.program_id(0)
    is_ctx = i < T_CTX // TM_PROJ
    targets = ((wk_ref, PA_W + W_B, KV_B), (wv_ref, PA_W + W_B + KV_W_B, KV_B),
               (nk_ref, PA_W + PB_W + W_C, H_C), (nv_ref, PA_W + PB_W + 2 * W_C, H_C))

    def project(x_ref, ci, with_cache):
        sh = mod_ref[pl.ds(ci, 1), 0:D_MODEL]
        sc = mod_ref[pl.ds(ci, 1), D_MODEL:2 * D_MODEL]
        for s in range(SEQ_PER_PROJ):
            rows = slice(s * SEQ, (s + 1) * SEQ)
            h = x_ref[rows, :] * (1.0 + sc) + sh
            p = _dot(h.astype(BF16), w_ref[...])
            pa_ref[rows, :] = p[:, 0:PA_W].astype(BF16)
            pb_ref[rows, :] = p[:, PA_W:PA_W + PB_W].astype(BF16)
            pc_ref[rows, :] = p[:, PA_W + PB_W:IN_WIDTH].astype(BF16)
            if with_cache:
                for j, (ref, col0, n_heads) in enumerate(targets):
                    if n_prev:
                        ref[s, 0:n_prev] = prev_refs[j][s]
                    for hh in range(n_heads):
                        ref[s, n_prev, hh] = p[:, col0 + hh * HEAD_DIM:col0 + (hh + 1) * HEAD_DIM]

    @pl.when(is_ctx)
    def _():
        project(xc_ref, 0, True)

    @pl.when(jnp.logical_not(is_ctx))
    def _():
        project(xl_ref, _cond_row(i, TM_PROJ), False)


def _in_projection(x_ctx, x_lat, mod, w_in_bf16, earlier):
    n_ctx_tiles = T_CTX // TM_PROJ
    n_prev = earlier[0].shape[1] if earlier else 0

    def cache_spec(n_layers, n_heads):
        return pl.BlockSpec((SEQ_PER_PROJ, n_layers, n_heads, SEQ, HEAD_DIM),
                            lambda i: (jnp.minimum(i, n_ctx_tiles - 1), 0, 0, 0, 0))

    cache_heads = (KV_B, KV_B, H_C, H_C)
    return pl.pallas_call(
        functools.partial(_inproj_kernel, n_prev),
        out_shape=(
            jax.ShapeDtypeStruct((T_ALL, PA_W), BF16),
            jax.ShapeDtypeStruct((T_ALL, PB_W), BF16),
            jax.ShapeDtypeStruct((T_ALL, PC_W), BF16),
        ) + tuple(jax.ShapeDtypeStruct((BATCH, n_prev + 1, nh, SEQ, HEAD_DIM), F32) for nh in cache_heads),
        grid=(T_ALL // TM_PROJ,),
        in_specs=[
            pl.BlockSpec((TM_PROJ, D_MODEL), lambda i: (jnp.minimum(i, n_ctx_tiles - 1), 0)),
            pl.BlockSpec((TM_PROJ, D_MODEL), lambda i: (jnp.maximum(i - n_ctx_tiles, 0), 0)),
            pl.BlockSpec((N_COND, 6 * D_MODEL), lambda i: (0, 0)),
            pl.BlockSpec((D_MODEL, IN_WIDTH), lambda i: (0, 0)),
        ] + [cache_spec(n_prev, nh) for nh in cache_heads if n_prev],
        out_specs=(
            pl.BlockSpec((TM_PROJ, PA_W), lambda i: (i, 0)),
            pl.BlockSpec((TM_PROJ, PB_W), lambda i: (i, 0)),
            pl.BlockSpec((TM_PROJ, PC_W), lambda i: (i, 0)),
        ) + tuple(cache_spec(n_prev + 1, nh) for nh in cache_heads),
        compiler_params=pltpu.CompilerParams(
            dimension_semantics=("arbitrary",), vmem_limit_bytes=VMEM_LIMIT),
        name="in_projection",
    )(x_ctx, x_lat, mod, w_in_bf16, *earlier)


def _decay_matrix(lg_f, lg_b, n):
    row = lax.broadcasted_iota(I32, (n, n), 0)
    col = lax.broadcasted_iota(I32, (n, n), 1)
    diff = (row - col).astype(F32)
    fwd = jnp.where(diff >= 0, jnp.exp(lg_f * jnp.maximum(diff, 0.0)), 0.0)
    bwd = jnp.where(diff <= 0, jnp.exp(lg_b * jnp.maximum(-diff, 0.0)), 0.0)
    return (fwd + bwd) * ATTN_SCALE


def _stacked_softmax_attend(scores, values, extra_logit=None):
    rows = scores[0].shape[0]
    s = jnp.concatenate(scores, axis=0)
    m = jnp.max(s, -1, keepdims=True)
    if extra_logit is not None:
        m = jnp.maximum(m, extra_logit)
    p = jnp.exp(s - m)
    den = jnp.sum(p, -1, keepdims=True)
    if extra_logit is not None:
        den = den + jnp.exp(extra_logit - m)
    p = p.astype(BF16)
    return [_dot(p[h * rows:(h + 1) * rows], v) / den[h * rows:(h + 1) * rows]
            for h, v in enumerate(values)]


def _lane_xor_matrix(width, distance):
    r = lax.broadcasted_iota(I32, (width, width), 0)
    c = lax.broadcasted_iota(I32, (width, width), 1)
    return jnp.where((r ^ distance) == c, 1.0, 0.0).astype(BF16)


def _ctx_mixer_kernel(sink_ref, pa_ref, pb_ref, pc_ref, decf_ref, decb_ref, gng_ref, gnb_ref,
                      mix_ref, sf_ref, sb_ref, dmat_ref, zf_ref, zb_ref):
    n = SEQ
    hd = HEAD_DIM

    @pl.when(pl.program_id(0) == 0)
    def _():
        pos = lax.broadcasted_iota(I32, (n, hd), 0).astype(F32)
        zf, zb = [], []
        for h in range(H_A):
            lg_f = _log_sigmoid(decf_ref[h])
            lg_b = _log_sigmoid(decb_ref[h])
            dmat_ref[h] = _decay_matrix(lg_f, lg_b, n)
            zf.append(jnp.exp(lg_f[:, 0:hd] * (n - 1.0 - pos)) * ATTN_SCALE)
            zb.append(jnp.exp(lg_b[:, 0:hd] * pos) * ATTN_SCALE)
        for p in range(H_A // 2):
            zf_ref[p] = jnp.concatenate(zf[2 * p:2 * p + 2], axis=1)
            zb_ref[p] = jnp.concatenate(zb[2 * p:2 * p + 2], axis=1)

    low = lax.broadcasted_iota(I32, (n, LANES), 1) < hd

    def own_half(x, h):
        zero = jnp.zeros_like(x)
        return jnp.where(low, x, zero) if h % 2 == 0 else jnp.where(low, zero, x)

    def merge(first, second):
        return jnp.where(low, first, second)

    def head_mean(x):
        first = jnp.sum(jnp.where(low, x, 0.0), -1, keepdims=True)
        second = jnp.sum(jnp.where(low, 0.0, x), -1, keepdims=True)
        return merge(first, second) * (1.0 / hd)

    for s in range(CTX_SEQS):
        rows = slice(s * n, (s + 1) * n)

        def pair_cols(ref, base, p):
            return ref[rows, base + p * LANES:base + (p + 1) * LANES]

        for p in range(H_A // 2):
            q_pair, k_pair = pair_cols(pa_ref, 0, p), pair_cols(pa_ref, W_A, p)
            v_pair, gate_pair = pair_cols(pa_ref, 2 * W_A, p), pair_cols(pa_ref, 3 * W_A, p)
            outs = []
            for h in (2 * p, 2 * p + 1):
                a = _dot_nt(own_half(q_pair, h), k_pair)
                outs.append(_dot((a * dmat_ref[h]).astype(BF16), v_pair))
            o = merge(outs[0], outs[1])
            kf = k_pair.astype(F32)
            for st_ref, z_ref in ((sf_ref, zf_ref), (sb_ref, zb_ref)):
                st = _dot((kf * z_ref[p]).T.astype(BF16), v_pair)
                st_ref[s, 2 * p] = st[0:hd, 0:hd]
                st_ref[s, 2 * p + 1] = st[hd:2 * hd, hd:2 * hd]
            mu = head_mean(o)
            oc = o - mu
            var = head_mean(oc * oc)
            pair = slice(p * LANES, (p + 1) * LANES)
            on = oc * lax.rsqrt(var + GN_EPS) * gng_ref[:, pair] + gnb_ref[:, pair]
            mix_ref[rows, pair] = (on * _silu(gate_pair.astype(F32))).astype(BF16)

        group = H_B // KV_B
        swap = _lane_xor_matrix(LANES, hd)
        kv_k = pb_ref[rows, W_B:W_B + KV_W_B]
        kv_v = pb_ref[rows, W_B + KV_W_B:W_B + 2 * KV_W_B]
        scores = []
        for hh in range(H_B):
            q = own_half(pair_cols(pb_ref, 0, hh // 2), hh) * ATTN_SCALE
            if hh % 2 != hh // group:
                q = _dot(q, swap).astype(BF16)
            scores.append(_dot_nt(q, kv_k))
        sinks = jnp.concatenate([jnp.full((n, 1), sink_ref[hh], F32) for hh in range(H_B)], axis=0)
        outs = []
        stacked = []
        for g0 in range(0, H_B, CTX_STACK):
            stacked += _stacked_softmax_attend(scores[g0:g0 + CTX_STACK], [kv_v] * CTX_STACK,
                                               sinks[g0 * n:(g0 + CTX_STACK) * n])
        for hh, o in enumerate(stacked):
            o = o.astype(BF16)
            outs.append(_dot(o, swap).astype(BF16) if hh % 2 != hh // group else o)
        for p in range(H_B // 2):
            mix_ref[rows, W_A + p * LANES:W_A + (p + 1) * LANES] = merge(outs[2 * p], outs[2 * p + 1])

        scores, values = [], []
        for hh in range(H_C):
            q = own_half(pair_cols(pc_ref, 0, hh // 2), hh) * ATTN_SCALE
            scores.append(_dot_nt(q, pair_cols(pc_ref, W_C, hh // 2)))
            values.append(pair_cols(pc_ref, 2 * W_C, hh // 2))
        outs = []
        for g0 in range(0, H_C, CTX_STACK):
            outs += _stacked_softmax_attend(scores[g0:g0 + CTX_STACK], values[g0:g0 + CTX_STACK])
        for p in range(H_C // 2):
            mix_ref[rows, W_A + W_B + p * LANES:W_A + W_B + (p + 1) * LANES] = merge(
                outs[2 * p], outs[2 * p + 1]).astype(BF16)


def _ctx_mixers(pa, pb, pc, sink, decf, decb, gng, gnb):
    return pl.pallas_call(
        _ctx_mixer_kernel,
        out_shape=(
            jax.ShapeDtypeStruct((T_CTX, D_MODEL), BF16),
            jax.ShapeDtypeStruct((BATCH, H_A, HEAD_DIM, HEAD_DIM), F32),
            jax.ShapeDtypeStruct((BATCH, H_A, HEAD_DIM, HEAD_DIM), F32),
        ),
        grid=(BATCH // CTX_SEQS,),
        in_specs=[
            pl.BlockSpec(memory_space=pltpu.SMEM),
            pl.BlockSpec((CTX_SEQS * SEQ, PA_W), lambda b: (b, 0)),
            pl.BlockSpec((CTX_SEQS * SEQ, PB_W), lambda b: (b, 0)),
            pl.BlockSpec((CTX_SEQS * SEQ, PC_W), lambda b: (b, 0)),
            pl.BlockSpec((H_A, 1, SEQ), lambda b: (0, 0, 0)),
            pl.BlockSpec((H_A, 1, SEQ), lambda b: (0, 0, 0)),
            pl.BlockSpec((1, W_A), lambda b: (0, 0)),
            pl.BlockSpec((1, W_A), lambda b: (0, 0)),
        ],
        out_specs=(
            pl.BlockSpec((CTX_SEQS * SEQ, D_MODEL), lambda b: (b, 0)),
            pl.BlockSpec((CTX_SEQS, H_A, HEAD_DIM, HEAD_DIM), lambda b: (b, 0, 0, 0)),
            pl.BlockSpec((CTX_SEQS, H_A, HEAD_DIM, HEAD_DIM), lambda b: (b, 0, 0, 0)),
        ),
        scratch_shapes=[
            pltpu.VMEM((H_A, SEQ, SEQ), F32),
            pltpu.VMEM((H_A // 2, SEQ, LANES), F32),
            pltpu.VMEM((H_A // 2, SEQ, LANES), F32),
        ],
        compiler_params=pltpu.CompilerParams(
            dimension_semantics=("arbitrary",), vmem_limit_bytes=VMEM_LIMIT),
        name="ctx_mixers",
    )(sink, pa, pb, pc, decf, decb, gng, gnb)


def _lat_ret_kernel(pa_ref, stf_ref, stb_ref, decf_ref, decb_ref, gng_ref, gnb_ref, ya_ref):
    c = RET_CHUNK
    hd = HEAD_DIM
    n_chunks = DEC_SEQ // c
    pos = lax.broadcasted_iota(I32, (c, LANES), 0).astype(F32)
    low = lax.broadcasted_iota(I32, (c, LANES), 1) < hd
    same_head = (lax.broadcasted_iota(I32, (LANES, LANES), 0) // hd
                 == lax.broadcasted_iota(I32, (LANES, LANES), 1) // hd)
    zero_blk = jnp.zeros((hd, hd), F32)

    def block_diag(a, b):
        return jnp.concatenate([jnp.concatenate([a, zero_blk], axis=1),
                                jnp.concatenate([zero_blk, b], axis=1)], axis=0)

    def head_mean(x):
        first = jnp.sum(jnp.where(low, x, 0.0), -1, keepdims=True)
        second = jnp.sum(jnp.where(low, 0.0, x), -1, keepdims=True)
        return jnp.where(low, first, second) * (1.0 / hd)

    for p in range(H_A // 2):
        lg_f = [_log_sigmoid(decf_ref[h]) for h in (2 * p, 2 * p + 1)]
        lg_b = [_log_sigmoid(decb_ref[h]) for h in (2 * p, 2 * p + 1)]
        dmat = [_decay_matrix(lg_f[t], lg_b[t], c) for t in range(2)]
        lf = jnp.where(low[0:1], lg_f[0][:, 0:LANES], lg_f[1][:, 0:LANES])
        lb = jnp.where(low[0:1], lg_b[0][:, 0:LANES], lg_b[1][:, 0:LANES])
        zf = jnp.exp(lf * (c - 1.0 - pos)) * ATTN_SCALE
        zb = jnp.exp(lb * pos) * ATTN_SCALE
        xf = jnp.exp(lf * (pos + 1.0))
        xb = jnp.exp(lb * (c - pos))
        gcf = jnp.exp(lf * float(c))
        gcb = jnp.exp(lb * float(c))
        pair = slice(p * LANES, (p + 1) * LANES)

        def chunk(i, base):
            return pa_ref[i * c:(i + 1) * c, base + p * LANES:base + (p + 1) * LANES]

        kv_f, kv_b = [], []
        for i in range(n_chunks):
            kf = chunk(i, W_A).astype(F32)
            v = chunk(i, 2 * W_A)
            kv_f.append(jnp.where(same_head, _dot((kf * zf).T.astype(BF16), v), 0.0))
            kv_b.append(jnp.where(same_head, _dot((kf * zb).T.astype(BF16), v), 0.0))

        s = block_diag(stf_ref[0, 0, 2 * p], stf_ref[0, 0, 2 * p + 1])
        seen_f = []
        for i in range(n_chunks):
            seen_f.append(s)
            s = gcf * s + kv_f[i]
        s = block_diag(stb_ref[0, 0, 2 * p], stb_ref[0, 0, 2 * p + 1])
        seen_b = [None] * n_chunks
        for i in reversed(range(n_chunks)):
            seen_b[i] = s
            s = gcb * s + kv_b[i]

        for i in range(n_chunks):
            q_pair, k_pair, v_pair = chunk(i, 0), chunk(i, W_A), chunk(i, 2 * W_A)
            inner = []
            for t in range(2):
                zero = jnp.zeros_like(q_pair)
                q = jnp.where(low, q_pair, zero) if t == 0 else jnp.where(low, zero, q_pair)
                inner.append(_dot((_dot_nt(q, k_pair) * dmat[t]).astype(BF16), v_pair))
            qf = q_pair.astype(F32)
            lhs = jnp.concatenate([(qf * xf).astype(BF16), (qf * xb).astype(BF16)], axis=1)
            rhs = jnp.concatenate([seen_f[i], seen_b[i]], axis=0).astype(BF16)
            o = jnp.where(low, inner[0], inner[1]) + _dot(lhs, rhs)
            mu = head_mean(o)
            oc = o - mu
            var = head_mean(oc * oc)
            on = oc * lax.rsqrt(var + GN_EPS) * gng_ref[:, pair] + gnb_ref[:, pair]
            y = on * _silu(chunk(i, 3 * W_A).astype(F32))
            ya_ref[i * c:(i + 1) * c, pair] = y.astype(BF16)


def _lat_retention(pa, st_f, st_b, layer, decf, decb, gng, gnb):
    lat0 = T_CTX // DEC_SEQ
    st_spec = pl.BlockSpec((1, 1, H_A, HEAD_DIM, HEAD_DIM), lambda b: (b, layer, 0, 0, 0))
    return pl.pallas_call(
        _lat_ret_kernel,
        out_shape=jax.ShapeDtypeStruct((T_LAT, W_A), BF16),
        grid=(DEC_BATCH,),
        in_specs=[
            pl.BlockSpec((DEC_SEQ, PA_W), lambda b: (lat0 + b, 0)),
            st_spec, st_spec,
            pl.BlockSpec((H_A, 1, RET_CHUNK), lambda b: (0, 0, 0)),
            pl.BlockSpec((H_A, 1, RET_CHUNK), lambda b: (0, 0, 0)),
            pl.BlockSpec((1, W_A), lambda b: (0, 0)),
            pl.BlockSpec((1, W_A), lambda b: (0, 0)),
        ],
        out_specs=pl.BlockSpec((DEC_SEQ, W_A), lambda b: (b, 0)),
        compiler_params=pltpu.CompilerParams(
            dimension_semantics=("arbitrary",), vmem_limit_bytes=VMEM_LIMIT),
        name="lat_retention",
    )(pa, st_f, st_b, decf, decb, gng, gnb)


def _rope(x, cos, sin_signed, swap):
    return x.astype(F32) * cos + _dot(x, swap) * sin_signed


def _lat_win_kernel(sink_ref, pq_ref, pseq_ref, kctx_ref, vctx_ref, cos_ref, sin_ref, yb_ref,
                    krope_ref, kc_ref, vc_ref, mask_ref):
    n = pl.program_id(1)
    hd = HEAD_DIM
    qb = WIN_QB
    band = WIN_BAND
    group = H_B // KV_B
    rot = _lane_xor_matrix(LANES, hd // 2)
    swap = _lane_xor_matrix(LANES, hd)

    @pl.when(n == 0)
    def _():
        k = pseq_ref[:, W_B:W_B + KV_W_B]
        krope_ref[...] = _rope(k, cos_ref[...], sin_ref[...], rot).astype(BF16)
        kc_ref[...] = jnp.concatenate([kctx_ref[0, 0, j] for j in range(KV_B)], axis=1).astype(BF16)
        vc_ref[...] = jnp.concatenate([vctx_ref[0, 0, j] for j in range(KV_B)], axis=1).astype(BF16)
        q_in_blk = lax.broadcasted_iota(I32, (group * qb, band), 0) % qb
        k_in_band = lax.broadcasted_iota(I32, (group * qb, band), 1)
        for ty, lead in enumerate((0, WINDOW, band - qb)):
            mask_ref[ty] = jnp.where(jnp.abs(k_in_band - lead - q_in_blk) <= WINDOW, 0.0, NEG_INF)

    q_rows = pl.ds(pl.multiple_of(n * qb, qb), qb)
    cos_q = cos_ref[q_rows, :]
    sin_q = sin_ref[q_rows, :]
    low = lax.broadcasted_iota(I32, (qb, LANES), 1) < hd
    q_heads = []
    for p in range(H_B // 2):
        q_pair = _rope(pq_ref[:, p * LANES:(p + 1) * LANES], cos_q, sin_q, rot) * ATTN_SCALE
        for hh in (2 * p, 2 * p + 1):
            q = jnp.where(low, q_pair, 0.0) if hh % 2 == 0 else jnp.where(low, 0.0, q_pair)
            q = q.astype(BF16)
            q_heads.append(_dot(q, swap).astype(BF16) if hh % 2 != hh // group else q)

    ws = jnp.clip(n * qb - WINDOW, 0, DEC_SEQ - band)
    k_rows = pl.ds(pl.multiple_of(ws, WINDOW), band)
    n_blk = DEC_SEQ // qb
    band_bias = mask_ref[jnp.where(n == 0, 0, jnp.where(n == n_blk - 1, 2, 1))]
    head_of_row = lax.broadcasted_iota(I32, (group * qb, 1), 0) // qb
    kw = krope_ref[k_rows, :]
    vw = pseq_ref[k_rows, W_B + KV_W_B:W_B + 2 * KV_W_B]
    outs = []
    for j in range(KV_B):
        heads = [j * group + g for g in range(group)]
        qs = jnp.concatenate([q_heads[hh] for hh in heads], axis=0)
        s_loc = _dot_nt(qs, kw) + band_bias
        s_ctx = _dot_nt(qs, kc_ref[...])
        sink = jnp.zeros((group * qb, 1), F32)
        for g, hh in enumerate(heads):
            sink = jnp.where(head_of_row == g, sink_ref[hh], sink)
        m = jnp.maximum(jnp.maximum(jnp.max(s_loc, -1, keepdims=True),
                                    jnp.max(s_ctx, -1, keepdims=True)), sink)
        p_loc = jnp.exp(s_loc - m)
        p_ctx = jnp.exp(s_ctx - m)
        den = (jnp.sum(p_loc, -1, keepdims=True) + jnp.sum(p_ctx, -1, keepdims=True)
               + jnp.exp(sink - m))
        o = ((_dot(p_loc.astype(BF16), vw) + _dot(p_ctx.astype(BF16), vc_ref[...])) / den).astype(BF16)
        for g, hh in enumerate(heads):
            o_h = o[g * qb:(g + 1) * qb]
            outs.append(_dot(o_h, swap).astype(BF16) if hh % 2 != j else o_h)
    for p in range(H_B // 2):
        yb_ref[:, p * LANES:(p + 1) * LANES] = jnp.where(low, outs[2 * p], outs[2 * p + 1])


def _lat_window_attn(pb, cache_k, cache_v, layer, sink, cos_t, sin_t):
    n_blk = DEC_SEQ // WIN_QB
    lat_blk0 = T_CTX // WIN_QB
    lat_seq0 = T_CTX // DEC_SEQ
    ctx_spec = pl.BlockSpec((1, 1, KV_B, PAST_LEN, HEAD_DIM), lambda b, n: (b, layer, 0, 0, 0))
    return pl.pallas_call(
        _lat_win_kernel,
        out_shape=jax.ShapeDtypeStruct((T_LAT, W_B), BF16),
        grid=(DEC_BATCH, n_blk),
        in_specs=[
            pl.BlockSpec(memory_space=pltpu.SMEM),
            pl.BlockSpec((WIN_QB, PB_W), lambda b, n: (lat_blk0 + b * n_blk + n, 0)),
            pl.BlockSpec((DEC_SEQ, PB_W), lambda b, n: (lat_seq0 + b, 0)),
            ctx_spec, ctx_spec,
            pl.BlockSpec((DEC_SEQ, LANES), lambda b, n: (0, 0)),
            pl.BlockSpec((DEC_SEQ, LANES), lambda b, n: (0, 0)),
        ],
        out_specs=pl.BlockSpec((WIN_QB, W_B), lambda b, n: (b * n_blk + n, 0)),
        scratch_shapes=[
            pltpu.VMEM((DEC_SEQ, KV_W_B), BF16),
            pltpu.VMEM((PAST_LEN, KV_W_B), BF16),
            pltpu.VMEM((PAST_LEN, KV_W_B), BF16),
            pltpu.VMEM((3, (H_B // KV_B) * WIN_QB, WIN_BAND), F32),
        ],
        compiler_params=pltpu.CompilerParams(
            dimension_semantics=("arbitrary", "arbitrary"), vmem_limit_bytes=VMEM_LIMIT),
        name="lat_window_attn",
    )(sink, pb, pb, cache_k, cache_v, cos_t, sin_t)


NA_Q = NA_TILE_ROWS * GRID_W
NA_K = NA_KEY_ROWS * GRID_W
NA_TILES = DEC_SEQ // NA_Q
LAT_ROWS = DEC_SEQ // GRID_W


def _na_window_start(tile):
    return jnp.clip(tile * NA_TILE_ROWS - NA_ROWS // 2, 0, LAT_ROWS - NA_KEY_ROWS)


def _lat_na_kernel(pq_ref, pseq_ref, kctx_ref, vctx_ref, bias_ref, yc_ref, kc_ref, vc_ref):
    t = pl.program_id(1)
    hd = HEAD_DIM
    nq = NA_Q

    @pl.when(t == 0)
    def _():
        for p in range(H_C // 2):
            kc_ref[p] = jnp.concatenate([kctx_ref[0, 0, 2 * p], kctx_ref[0, 0, 2 * p + 1]], axis=1).astype(BF16)
            vc_ref[p] = jnp.concatenate([vctx_ref[0, 0, 2 * p], vctx_ref[0, 0, 2 * p + 1]], axis=1).astype(BF16)

    k_rows = pl.ds(pl.multiple_of(_na_window_start(t) * GRID_W, GRID_W), NA_K)
    low = lax.broadcasted_iota(I32, (nq, LANES), 1) < hd
    loc, ctx = [], []
    for hh in range(H_C):
        p = hh // 2
        q_pair = pq_ref[:, p * LANES:(p + 1) * LANES] * ATTN_SCALE
        zero = jnp.zeros_like(q_pair)
        q = jnp.where(low, q_pair, zero) if hh % 2 == 0 else jnp.where(low, zero, q_pair)
        loc.append(_dot_nt(q, pseq_ref[k_rows, W_C + p * LANES:W_C + (p + 1) * LANES]) + bias_ref[0, hh])
        ctx.append(_dot_nt(q, kc_ref[p]))
    outs = []
    for g0 in range(0, H_C, NA_STACK):
        s_loc = jnp.concatenate(loc[g0:g0 + NA_STACK], axis=0)
        s_ctx = jnp.concatenate(ctx[g0:g0 + NA_STACK], axis=0)
        m = jnp.maximum(jnp.max(s_loc, -1, keepdims=True), jnp.max(s_ctx, -1, keepdims=True))
        p_loc = jnp.exp(s_loc - m)
        p_ctx = jnp.exp(s_ctx - m)
        den = jnp.sum(p_loc, -1, keepdims=True) + jnp.sum(p_ctx, -1, keepdims=True)
        p_loc = p_loc.astype(BF16)
        p_ctx = p_ctx.astype(BF16)
        for hh in range(g0, g0 + NA_STACK):
            p = hh // 2
            rows = slice((hh - g0) * nq, (hh - g0 + 1) * nq)
            vw = pseq_ref[k_rows, 2 * W_C + p * LANES:2 * W_C + (p + 1) * LANES]
            outs.append((_dot(p_loc[rows], vw) + _dot(p_ctx[rows], vc_ref[p])) / den[rows])
    for p in range(H_C // 2):
        yc_ref[:, p * LANES:(p + 1) * LANES] = jnp.where(low, outs[2 * p], outs[2 * p + 1]).astype(BF16)


def _na_tile_type(t):
    return jnp.where(t == 0, 0, jnp.where(t == NA_TILES - 1, 2, 1))


def _lat_na_attn(pc, cache_k, cache_v, layer, maskbias):
    lat_tile0 = T_CTX // NA_Q
    lat_seq0 = T_CTX // DEC_SEQ
    ctx_spec = pl.BlockSpec((1, 1, H_C, PAST_LEN, HEAD_DIM), lambda b, t: (b, layer, 0, 0, 0))
    return pl.pallas_call(
        _lat_na_kernel,
        out_shape=jax.ShapeDtypeStruct((T_LAT, W_C), BF16),
        grid=(DEC_BATCH, NA_TILES),
        in_specs=[
            pl.BlockSpec((NA_Q, PC_W), lambda b, t: (lat_tile0 + b * NA_TILES + t, 0)),
            pl.BlockSpec((DEC_SEQ, PC_W), lambda b, t: (lat_seq0 + b, 0)),
            ctx_spec, ctx_spec,
            pl.BlockSpec((1, H_C, NA_Q, NA_K), lambda b, t: (_na_tile_type(t), 0, 0, 0)),
        ],
        out_specs=pl.BlockSpec((NA_Q, W_C), lambda b, t: (b * NA_TILES + t, 0)),
        scratch_shapes=[
            pltpu.VMEM((H_C // 2, PAST_LEN, LANES), BF16),
            pltpu.VMEM((H_C // 2, PAST_LEN, LANES), BF16),
        ],
        compiler_params=pltpu.CompilerParams(
            dimension_semantics=("arbitrary", "arbitrary"), vmem_limit_bytes=VMEM_LIMIT),
        name="lat_na_attn",
    )(pc, pc, cache_k, cache_v, maskbias)


def _na_block_index():
    out = np.zeros((3, NA_TILE_ROWS, NA_KEY_ROWS), np.int32)
    for ty, tile in enumerate((0, 1, NA_TILES - 1)):
        r = tile * NA_TILE_ROWS
        ws = int(np.clip(r - NA_ROWS // 2, 0, LAT_ROWS - NA_KEY_ROWS))
        for qq in range(NA_TILE_ROWS):
            qr = r + qq
            r0 = int(np.clip(qr - NA_ROWS // 2, 0, LAT_ROWS - NA_ROWS))
            for kk in range(NA_KEY_ROWS):
                kr = ws + kk
                out[ty, qq, kk] = kr - qr + NA_ROWS - 1 if r0 <= kr < r0 + NA_ROWS else 2 * NA_ROWS - 1
    return out


def _na_maskbias(rpb):
    qc = np.arange(GRID_W)[:, None]
    kc = np.arange(GRID_W)[None, :]
    c0 = np.clip(qc - NA_COLS // 2, 0, GRID_W - NA_COLS)
    col_ok = (kc >= c0) & (kc < c0 + NA_COLS)
    ci = np.clip(kc - qc + NA_COLS - 1, 0, 2 * NA_COLS - 2)
    onehot = (ci[None] == np.arange(2 * NA_COLS - 1)[:, None, None]).astype(np.float32)
    cols = jnp.einsum("hab,bqk->haqk", rpb, jnp.asarray(onehot), precision=lax.Precision.HIGHEST)
    cols = jnp.where(jnp.asarray(col_ok)[None, None], cols, NEG_INF)
    cols = jnp.concatenate([cols, jnp.full((H_C, 1, GRID_W, GRID_W), NEG_INF, F32)], axis=1)
    block_index = _na_block_index()

    def assemble(cols_ref, out_ref):
        for ty in range(3):
            for qq in range(NA_TILE_ROWS):
                for kk in range(NA_KEY_ROWS):
                    out_ref[ty, 0, qq * GRID_W:(qq + 1) * GRID_W, kk * GRID_W:(kk + 1) * GRID_W] = (
                        cols_ref[0, int(block_index[ty, qq, kk])])

    return pl.pallas_call(
        assemble,
        out_shape=jax.ShapeDtypeStruct((3, H_C, NA_Q, NA_K), F32),
        grid=(H_C,),
        in_specs=[pl.BlockSpec((1, 2 * NA_ROWS, GRID_W, GRID_W), lambda h: (h, 0, 0, 0))],
        out_specs=pl.BlockSpec((3, 1, NA_Q, NA_K), lambda h: (0, h, 0, 0)),
        compiler_params=pltpu.CompilerParams(dimension_semantics=("arbitrary",)),
        name="na_bias_assemble",
    )(cols)


def _rope_tables():
    t = np.arange(DEC_SEQ)
    n_freq = HEAD_DIM // 4
    inv = (ROPE_BASE ** (-np.arange(n_freq, dtype=np.float32) / n_freq)).astype(np.float32)
    row = (t // GRID_W).astype(np.float32)[:, None] * inv
    col = (t % GRID_W).astype(np.float32)[:, None] * inv
    ang = np.concatenate([row, col], -1)
    cos, sin = np.cos(ang), np.sin(ang)
    cos_h = np.concatenate([cos, cos], -1)
    sin_h = np.concatenate([-sin, sin], -1)
    reps = LANES // HEAD_DIM
    return (jnp.asarray(np.tile(cos_h, (1, reps)), F32), jnp.asarray(np.tile(sin_h, (1, reps)), F32))


def _first_index_of(mask, iota, sentinel):
    return jnp.min(jnp.where(mask, iota, sentinel), axis=0, keepdims=True)


def _route(logits, b_col):
    n = logits.shape[1]
    scores = jax.nn.sigmoid(logits)
    sel = scores + b_col
    io_g = lax.broadcasted_iota(I32, (GROUP_SIZE, n), 0)
    gs_rows = []
    for g in range(N_GROUPS):
        s = sel[g * GROUP_SIZE:(g + 1) * GROUP_SIZE]
        m1 = jnp.max(s, axis=0, keepdims=True)
        i1 = _first_index_of(s == m1, io_g, GROUP_SIZE)
        m2 = jnp.max(jnp.where(io_g == i1, PICKED, s), axis=0, keepdims=True)
        gs_rows.append(m1 + m2)
    gs = jnp.concatenate(gs_rows, axis=0)
    io_n = lax.broadcasted_iota(I32, (N_GROUPS, n), 0)
    gsel = jnp.zeros((N_GROUPS, n), F32)
    for _ in range(TOPK_GROUPS):
        mg = jnp.max(gs, axis=0, keepdims=True)
        gi = _first_index_of(gs == mg, io_n, N_GROUPS)
        hit = io_n == gi
        gsel = jnp.where(hit, 1.0, gsel)
        gs = jnp.where(hit, PICKED, gs)
    cand = jnp.concatenate(
        [jnp.where(gsel[g:g + 1] > 0.5, sel[g * GROUP_SIZE:(g + 1) * GROUP_SIZE], NEG_INF)
         for g in range(N_GROUPS)], axis=0)
    io_e = lax.broadcasted_iota(I32, (N_EXPERTS, n), 0)
    picks, raw = [], []
    for _ in range(TOP_K):
        mv = jnp.max(cand, axis=0, keepdims=True)
        ei = _first_index_of(cand == mv, io_e, N_EXPERTS)
        hit = io_e == ei
        picks.append((hit, ei))
        raw.append(jnp.sum(jnp.where(hit, scores, 0.0), axis=0, keepdims=True))
        cand = jnp.where(hit, PICKED, cand)
    return picks, raw


def _post_mixer_kernel(xc_ref, xl_ref, mixc_ref, ya_ref, yb_ref, yc_ref, wout_ref, mod_ref, g_ref, b_ref,
                       wr_ref, br_ref,
                       x1_ref, h2_ref, eidx_ref, wsel_ref, rank_ref, cnt_ref):
    step = pl.program_id(0)
    tm = TM_TOK
    is_ctx = step < T_CTX // (POST_TILES * tm)

    @pl.when(step == 0)
    def _():
        cnt_ref[...] = jnp.zeros_like(cnt_ref)

    def project(s):
        rows = slice(s * tm, (s + 1) * tm)
        ci = _cond_row(step * POST_TILES + s, tm)
        gate1 = mod_ref[pl.ds(ci, 1), 2 * D_MODEL:3 * D_MODEL]
        sh2 = mod_ref[pl.ds(ci, 1), 3 * D_MODEL:4 * D_MODEL]
        sc2 = mod_ref[pl.ds(ci, 1), 4 * D_MODEL:5 * D_MODEL]
        mix_lat = jnp.concatenate([ya_ref[rows, :], yb_ref[rows, :], yc_ref[rows, :]], axis=-1)
        mix = jnp.where(is_ctx, mixc_ref[rows, :], mix_lat)
        y = _dot(mix, wout_ref[...])
        x = jnp.where(is_ctx, xc_ref[rows, :], xl_ref[rows, :])
        x1 = _layer_norm(ALPHA * x + gate1 * y, g_ref[...], b_ref[...])
        x1_ref[rows, :] = x1
        h2 = x1 * (1.0 + sc2) + sh2
        h_hi = h2.astype(BF16)
        h2_ref[rows, :] = h_hi
        h_lo = (h2 - h_hi.astype(F32)).astype(BF16)
        both = (_dot(h_hi, wr_ref[...]) + _dot(h_lo, wr_ref[...])).T
        return both[0:N_EXPERTS] + both[N_EXPERTS:2 * N_EXPERTS]

    def route(s, logits):
        toks = s * tm
        routed = [_route(logits[:, g * LANES:(g + 1) * LANES], br_ref[...]) for g in range(tm // LANES)]
        multi_g = []
        for picks, _ in routed:
            m = jnp.zeros((N_EXPERTS, LANES), F32)
            for hit, _ in picks:
                m = m + jnp.where(hit, 1.0, 0.0)
            multi_g.append(m)
        multi = jnp.concatenate(multi_g, axis=1)
        before = (lax.broadcasted_iota(I32, (tm, tm), 0) < lax.broadcasted_iota(I32, (tm, tm), 1))
        cum = _dot(multi.astype(BF16), jnp.where(before, 1.0, 0.0).astype(BF16))
        pad = jnp.zeros((SUBLANES - TOP_K, LANES), F32)
        for g, (picks, raw) in enumerate(routed):
            lanes = slice(toks + g * LANES, toks + (g + 1) * LANES)
            total = raw[0]
            for r in raw[1:]:
                total = total + r
            scale = ROUTE_SCALE / total
            cum_g = cum[:, g * LANES:(g + 1) * LANES]
            eidx_ref[:, lanes] = jnp.concatenate([ei for _, ei in picks] + [pad.astype(I32)], axis=0)
            wsel_ref[:, lanes] = jnp.concatenate([r * scale for r in raw] + [pad], axis=0)
            rank_ref[:, lanes] = jnp.concatenate(
                [jnp.sum(jnp.where(hit, cum_g, 0.0), axis=0, keepdims=True) for hit, _ in picks] + [pad],
                axis=0).astype(I32)
        tile_lane = lax.broadcasted_iota(I32, (N_EXPERTS, LANES), 1)
        cnt_ref[...] = jnp.where(tile_lane == step * POST_TILES + s,
                                 jnp.sum(multi, axis=1, keepdims=True), cnt_ref[...])

    logits = [project(s) for s in range(POST_TILES)]
    for s in range(POST_TILES):
        route(s, logits[s])


def _post_mixer(x_ctx, x_lat, mix_c, ya, yb, yc, w_out_bf16, mod, ln_g, ln_b, wr_split, b_router_col):
    tm = POST_TILES * TM_TOK
    n_ctx = T_CTX // tm
    ctx_map = lambda i: (jnp.minimum(i, n_ctx - 1), 0)
    lat_map = lambda i: (jnp.maximum(i - n_ctx, 0), 0)
    row_map = lambda i: (i, 0)
    const = lambda i: (0, 0)
    tok_map = lambda i: (0, i)
    return pl.pallas_call(
        _post_mixer_kernel,
        out_shape=(
            jax.ShapeDtypeStruct((T_ALL, D_MODEL), F32),
            jax.ShapeDtypeStruct((T_ALL, D_MODEL), BF16),
            jax.ShapeDtypeStruct((SUBLANES, T_ALL), I32),
            jax.ShapeDtypeStruct((SUBLANES, T_ALL), F32),
            jax.ShapeDtypeStruct((SUBLANES, T_ALL), I32),
            jax.ShapeDtypeStruct((N_EXPERTS, LANES), F32),
        ),
        grid=(T_ALL // tm,),
        in_specs=[
            pl.BlockSpec((tm, D_MODEL), ctx_map),
            pl.BlockSpec((tm, D_MODEL), lat_map),
            pl.BlockSpec((tm, D_MODEL), ctx_map),
            pl.BlockSpec((tm, W_A), lat_map),
            pl.BlockSpec((tm, W_B), lat_map),
            pl.BlockSpec((tm, W_C), lat_map),
            pl.BlockSpec((D_MODEL, D_MODEL), const),
            pl.BlockSpec((N_COND, 6 * D_MODEL), const),
            pl.BlockSpec((1, D_MODEL), const),
            pl.BlockSpec((1, D_MODEL), const),
            pl.BlockSpec((D_MODEL, 2 * N_EXPERTS), const),
            pl.BlockSpec((N_EXPERTS, 1), const),
        ],
        out_specs=(
            pl.BlockSpec((tm, D_MODEL), row_map),
            pl.BlockSpec((tm, D_MODEL), row_map),
            pl.BlockSpec((SUBLANES, tm), tok_map),
            pl.BlockSpec((SUBLANES, tm), tok_map),
            pl.BlockSpec((SUBLANES, tm), tok_map),
            pl.BlockSpec((N_EXPERTS, LANES), const),
        ),
        compiler_params=pltpu.CompilerParams(
            dimension_semantics=("arbitrary",), vmem_limit_bytes=VMEM_LIMIT),
        name="post_mixer",
    )(x_ctx, x_lat, mix_c, ya, yb, yc, w_out_bf16, mod, ln_g, ln_b, wr_split, b_router_col)


PLAN_TILES = 4


def _plan_kernel(eidx_ref, rank_ref, nmat_ref, lslot_ref, unit_ref, gend_ref):
    step = pl.program_id(0)
    units = jnp.floor((nmat_ref[...] + (UNIT - 1.0)) * (1.0 / UNIT))
    units_bf = units.astype(BF16)
    earlier_e = (lax.broadcasted_iota(I32, (N_EXPERTS, N_EXPERTS), 1)
                 < lax.broadcasted_iota(I32, (N_EXPERTS, N_EXPERTS), 0))
    tri_e = jnp.where(earlier_e, 1.0, 0.0).astype(BF16)
    earlier_t = (lax.broadcasted_iota(I32, (LANES, LANES), 0) < lax.broadcasted_iota(I32, (LANES, LANES), 1))
    tri_t = jnp.where(earlier_t, 1.0, 0.0).astype(BF16)
    local_off = _dot(tri_e, units_bf)
    tile_off = _dot(units_bf, tri_t)
    per_expert = jnp.sum(units, axis=1, keepdims=True)
    blocks = jnp.floor((per_expert + (UNITS_PER_BLOCK - 1.0)) * (1.0 / UNITS_PER_BLOCK))
    blocks_l = jnp.broadcast_to(blocks, (N_EXPERTS, LANES))
    start_blk = _dot(tri_e, blocks_l.astype(BF16))
    end_blk = start_blk + blocks_l
    gend_ref[...] = (end_blk * BM).astype(I32)

    tile_lane = lax.broadcasted_iota(I32, (N_EXPERTS, LANES), 1)
    u = lax.broadcasted_iota(I32, (N_EXPERTS, MAX_UNITS), 1).astype(F32)
    io_e = lax.broadcasted_iota(I32, (N_EXPERTS, TM_TOK), 0)
    for s in range(PLAN_TILES):
        i = step * PLAN_TILES + s
        this_tile = tile_lane == i

        def column(a):
            return jnp.sum(jnp.where(this_tile, a, 0.0), axis=1, keepdims=True)

        lo, n_u = column(local_off), column(units)
        base_unit = start_blk[:, 0:1] * UNITS_PER_BLOCK + column(tile_off) - lo
        inside = jnp.where(u >= lo, jnp.where(u < lo + n_u, 1.0, 0.0), 0.0)
        dst_unit = jnp.sum(inside * (base_unit + u), axis=0, keepdims=True)
        used = jnp.sum(inside, axis=0, keepdims=True) > 0.5
        spare = (SPARE_UNIT0 + (i % 2) * MAX_UNITS).astype(F32) + u[0:1, :]
        unit_ref[s] = jnp.where(used, dst_unit, spare).astype(I32)

        toks = slice(s * TM_TOK, (s + 1) * TM_TOK)
        rows = []
        for k in range(TOP_K):
            hit = io_e == eidx_ref[k:k + 1, toks]
            seg = jnp.sum(jnp.where(hit, lo * UNIT, 0.0), axis=0, keepdims=True)
            rows.append(seg.astype(I32) + rank_ref[k:k + 1, toks])
        rows.append(jnp.full((SUBLANES - TOP_K, TM_TOK), -1, I32))
        lslot_ref[:, toks] = jnp.concatenate(rows, axis=0)


def _slot_plan(eidx, rank, nmat):
    tok_map = lambda i: (0, i)
    const = lambda i: (0, 0)
    lslot, unit_tab, gend = pl.pallas_call(
        _plan_kernel,
        out_shape=(
            jax.ShapeDtypeStruct((SUBLANES, T_ALL), I32),
            jax.ShapeDtypeStruct((N_TOK_TILES, 1, MAX_UNITS), I32),
            jax.ShapeDtypeStruct((N_EXPERTS, LANES), I32),
        ),
        grid=(N_TOK_TILES // PLAN_TILES,),
        in_specs=[
            pl.BlockSpec((SUBLANES, PLAN_TILES * TM_TOK), tok_map),
            pl.BlockSpec((SUBLANES, PLAN_TILES * TM_TOK), tok_map),
            pl.BlockSpec((N_EXPERTS, LANES), const),
        ],
        out_specs=(
            pl.BlockSpec((SUBLANES, PLAN_TILES * TM_TOK), tok_map),
            pl.BlockSpec((PLAN_TILES, 1, MAX_UNITS), lambda i: (i, 0, 0)),
            pl.BlockSpec((N_EXPERTS, LANES), const),
        ),
        compiler_params=pltpu.CompilerParams(dimension_semantics=("arbitrary",)),
        name="slot_plan",
    )(eidx, rank, nmat)
    return lslot, unit_tab.reshape(N_TOK_TILES * MAX_UNITS), gend[:, 0]


PACK_W = D_MODEL // 2
HI_HALF = -65536


def _pack_pairs(x):
    lo = lax.bitcast_convert_type(x[:, 0:PACK_W], I32)
    hi = lax.bitcast_convert_type(x[:, PACK_W:D_MODEL], I32)
    return lax.shift_right_logical(lo, 16) | (hi & HI_HALF)


def _unpack_pairs(u):
    lo = lax.bitcast_convert_type(lax.shift_left(u, 16), F32).astype(BF16)
    hi = lax.bitcast_convert_type(u & HI_HALF, F32).astype(BF16)
    return lo, hi


def _unit_rows(unit):
    row = unit * UNIT
    return pl.ds(row if isinstance(unit, int) else pl.multiple_of(row, UNIT), UNIT)


def _unit_copy(src, src_unit, dst, dst_unit, sem):
    return pltpu.make_async_copy(src.at[_unit_rows(src_unit)], dst.at[_unit_rows(dst_unit)], sem)


def _dispatch_kernel(fresh, gend_ref, tab_ref, h2_ref, lslot_ref, *refs):
    xs_hbm, zero_ref, local_ref, sem_zero, sem_rows = refs if fresh else refs[1:]
    i = pl.program_id(0)
    buf = i % 2

    def drain(b):
        pltpu.make_async_copy(local_ref.at[b], xs_hbm.at[pl.ds(0, LOCAL_ROWS)], sem_rows.at[b]).wait()

    def has_rows(e):
        return gend_ref[e] > jnp.where(e == 0, 0, gend_ref[jnp.maximum(e - 1, 0)])

    def zero_copy(e):
        return pltpu.make_async_copy(
            zero_ref, xs_hbm.at[pl.ds(pl.multiple_of(gend_ref[e] - BM, BM), BM)], sem_zero)

    @pl.when(jnp.logical_and(i == 0, fresh))
    def _():
        zero_ref[...] = jnp.zeros_like(zero_ref)

        def start(e, c):
            @pl.when(has_rows(e))
            def _():
                zero_copy(e).start()
            return c

        def wait(e, c):
            @pl.when(has_rows(e))
            def _():
                zero_copy(e).wait()
            return c

        def tail_copy(blk):
            return pltpu.make_async_copy(
                zero_ref, xs_hbm.at[pl.ds(pl.multiple_of(blk * BM, BM), BM)], sem_zero)

        def start_tail(blk, c):
            tail_copy(blk).start()
            return c

        def wait_tail(blk, c):
            tail_copy(blk).wait()
            return c

        n_used = gend_ref[N_EXPERTS - 1] // BM
        lax.fori_loop(0, N_EXPERTS, start, 0)
        lax.fori_loop(n_used, N_BLOCKS_ALL, start_tail, 0)
        lax.fori_loop(0, N_EXPERTS, wait, 0)
        lax.fori_loop(n_used, N_BLOCKS_ALL, wait_tail, 0)

    @pl.when(i >= 2)
    def _():
        drain(buf)

    h2 = h2_ref[...]
    units_per_chunk = PERM_CHUNK // UNIT
    local = local_ref.at[buf]
    for c in range(LOCAL_ROWS // PERM_CHUNK):
        slot = c * PERM_CHUNK + lax.broadcasted_iota(I32, (PERM_CHUNK, TM_TOK), 0)
        p = jnp.zeros((PERM_CHUNK, TM_TOK), F32)
        for k in range(TOP_K):
            p = jnp.where(slot == lslot_ref[k:k + 1, :], 1.0, p)
        local[c * PERM_CHUNK:(c + 1) * PERM_CHUNK, :] = _pack_pairs(_dot(p.astype(BF16), h2))
        for u in range(c * units_per_chunk, (c + 1) * units_per_chunk):
            _unit_copy(local, u, xs_hbm, tab_ref[i * MAX_UNITS + u], sem_rows.at[buf]).start()

    @pl.when(i == N_TOK_TILES - 1)
    def _():
        drain(1 - buf)
        drain(buf)


def _dispatch(h2, lslot, unit_tab, gend, xs_prev):
    fresh = xs_prev is None
    n_prefetch = 2
    return pl.pallas_call(
        functools.partial(_dispatch_kernel, fresh),
        out_shape=jax.ShapeDtypeStruct((N_SLOTS, PACK_W), I32),
        input_output_aliases={} if fresh else {n_prefetch + 2: 0},
        grid_spec=pltpu.PrefetchScalarGridSpec(
            num_scalar_prefetch=n_prefetch,
            grid=(N_TOK_TILES,),
            in_specs=[
                pl.BlockSpec((TM_TOK, D_MODEL), lambda i, ge, tab: (i, 0)),
                pl.BlockSpec((SUBLANES, TM_TOK), lambda i, ge, tab: (0, i)),
            ] + ([] if fresh else [pl.BlockSpec(memory_space=pl.ANY)]),
            out_specs=pl.BlockSpec(memory_space=pl.ANY),
            scratch_shapes=[
                pltpu.VMEM((BM, PACK_W), I32),
                pltpu.VMEM((2, LOCAL_ROWS, PACK_W), I32),
                pltpu.SemaphoreType.DMA,
                pltpu.SemaphoreType.DMA((2,)),
            ],
        ),
        compiler_params=pltpu.CompilerParams(
            dimension_semantics=("arbitrary",), vmem_limit_bytes=VMEM_LIMIT),
        name="moe_dispatch",
    )(gend, unit_tab, h2, lslot, *(() if fresh else (xs_prev,)))


def _expert_kernel(fresh, gend_ref, xs_hbm, wgu_ref, wdown_ref, *refs):
    ys_hbm, xbuf, ybuf, wgu_bf, wdown_bf, sem_in, sem_out = refs if fresh else refs[1:]
    e = pl.program_id(0)
    first = jnp.where(e == 0, 0, gend_ref[jnp.maximum(e - 1, 0)]) // BM
    last = gend_ref[e] // BM
    n_used = gend_ref[N_EXPERTS - 1] // BM

    def rows_of(blk):
        return pl.ds(pl.multiple_of(blk * BM, BM), BM)

    def in_copy(blk):
        return pltpu.make_async_copy(xs_hbm.at[rows_of(blk)], xbuf.at[blk % X_BUFS], sem_in.at[blk % X_BUFS])

    def out_copy(blk):
        return pltpu.make_async_copy(ybuf.at[blk % 2], ys_hbm.at[rows_of(blk)], sem_out.at[blk % 2])

    @pl.when(e == 0)
    def _():
        for a in range(X_AHEAD):
            @pl.when(a < n_used)
            def _(a=a):
                in_copy(a).start()

    @pl.when(last > first)
    def _():
        wgu_bf[...] = wgu_ref[0, 0].astype(BF16)
        wdown_bf[...] = wdown_ref[0, 0].astype(BF16)

    def block(blk, carry):
        @pl.when(blk + X_AHEAD < n_used)
        def _():
            in_copy(blk + X_AHEAD).start()

        in_copy(blk).wait()

        @pl.when(blk >= 2)
        def _():
            out_copy(blk - 2).wait()

        x_lo, x_hi = _unpack_pairs(xbuf[blk % X_BUFS])
        gu = _dot(x_lo, wgu_bf[0:PACK_W, :]) + _dot(x_hi, wgu_bf[PACK_W:D_MODEL, :])
        act = _silu(gu[:, 0:D_EXPERT]) * gu[:, D_EXPERT:2 * D_EXPERT]
        y = _dot(act.astype(BF16), wdown_bf[...])
        ybuf[blk % 2] = _pack_pairs(y.astype(BF16).astype(F32))
        out_copy(blk).start()
        return carry

    lax.fori_loop(first, last, block, 0)

    @pl.when(e == N_EXPERTS - 1)
    def _():
        @pl.when(n_used >= 2)
        def _():
            out_copy(n_used - 2).wait()

        @pl.when(n_used >= 1)
        def _():
            out_copy(n_used - 1).wait()

        if fresh:
            ybuf[0] = jnp.zeros((BM, PACK_W), I32)

            def tail_copy(blk):
                return pltpu.make_async_copy(ybuf.at[0], ys_hbm.at[rows_of(blk)], sem_out.at[0])

            def start(blk, c):
                tail_copy(blk).start()
                return c

            def wait(blk, c):
                tail_copy(blk).wait()
                return c

            lax.fori_loop(n_used, N_BLOCKS_ALL, start, 0)
            lax.fori_loop(n_used, N_BLOCKS_ALL, wait, 0)


def _experts(xs, w_gu, w_down, layer, gend, ys_prev):
    fresh = ys_prev is None
    n_prefetch = 1
    return pl.pallas_call(
        functools.partial(_expert_kernel, fresh),
        out_shape=jax.ShapeDtypeStruct((N_SLOTS, PACK_W), I32),
        input_output_aliases={} if fresh else {n_prefetch + 3: 0},
        grid_spec=pltpu.PrefetchScalarGridSpec(
            num_scalar_prefetch=n_prefetch,
            grid=(N_EXPERTS,),
            in_specs=[
                pl.BlockSpec(memory_space=pl.ANY),
                pl.BlockSpec((1, 1, D_MODEL, 2 * D_EXPERT), lambda e, ge: (layer, e, 0, 0)),
                pl.BlockSpec((1, 1, D_EXPERT, D_MODEL), lambda e, ge: (layer, e, 0, 0)),
            ] + ([] if fresh else [pl.BlockSpec(memory_space=pl.ANY)]),
            out_specs=pl.BlockSpec(memory_space=pl.ANY),
            scratch_shapes=[
                pltpu.VMEM((X_BUFS, BM, PACK_W), I32),
                pltpu.VMEM((2, BM, PACK_W), I32),
                pltpu.VMEM((D_MODEL, 2 * D_EXPERT), BF16),
                pltpu.VMEM((D_EXPERT, D_MODEL), BF16),
                pltpu.SemaphoreType.DMA((X_BUFS,)),
                pltpu.SemaphoreType.DMA((2,)),
            ],
        ),
        compiler_params=pltpu.CompilerParams(
            dimension_semantics=("arbitrary",), vmem_limit_bytes=VMEM_LIMIT),
        name="moe_experts",
    )(gend, xs, w_gu, w_down, *(() if fresh else (ys_prev,)))


def _combine_kernel(tab_ref, x1_ref, h2_ref, lslot_ref, wsel_ref, ys_hbm, wsgu_ref, wsdown_ref, mod_ref,
                    g_ref, b_ref, outc_ref, outl_ref, local_ref, sel_ref, ylo_ref, yhi_ref, sem_rows):
    i = pl.program_id(0)
    tm = TM_TOK
    buf = i % 2

    def fetch_unit(tile, b, u):
        _unit_copy(ys_hbm, tab_ref[tile * MAX_UNITS + u], local_ref.at[b], u, sem_rows.at[b]).start()

    def drain(b):
        pltpu.make_async_copy(ys_hbm.at[pl.ds(0, LOCAL_ROWS)], local_ref.at[b], sem_rows.at[b]).wait()

    @pl.when(i == 0)
    def _():
        def body(u, c):
            fetch_unit(0, 0, u)
            return c

        lax.fori_loop(0, MAX_UNITS, body, 0, unroll=8)

    sgu = _dot(h2_ref[...], wsgu_ref[...])
    act = _silu(sgu[:, 0:D_SHARED]) * sgu[:, D_SHARED:2 * D_SHARED]
    f = _dot(act.astype(BF16), wsdown_ref[...])

    nxt = jnp.minimum(i + 1, N_TOK_TILES - 1)
    n_groups = tm // SEL_ROWS
    units_per_group = MAX_UNITS // n_groups
    slot = lax.broadcasted_iota(I32, (SEL_ROWS, LOCAL_ROWS), 1)
    for g in range(n_groups):
        rows = slice(g * SEL_ROWS, (g + 1) * SEL_ROWS)
        sel = jnp.zeros((SEL_ROWS, LOCAL_ROWS), F32)
        for k in range(TOP_K):
            sel = jnp.where(slot == lslot_ref[rows, k:k + 1], wsel_ref[rows, k:k + 1], sel)
        sel_ref[rows, :] = sel.astype(BF16)
        for u in range(g * units_per_group, (g + 1) * units_per_group):
            fetch_unit(nxt, 1 - buf, u)

    local = local_ref.at[buf]
    drain(buf)
    for c in range(LOCAL_ROWS // PERM_CHUNK):
        rows = slice(c * PERM_CHUNK, (c + 1) * PERM_CHUNK)
        ylo_ref[rows, :], yhi_ref[rows, :] = _unpack_pairs(local[rows, :])
    sel = sel_ref[...]
    f = f + jnp.concatenate([_dot(sel, ylo_ref[...]), _dot(sel, yhi_ref[...])], axis=-1)
    ci = _cond_row(i, tm)
    gate2 = mod_ref[pl.ds(ci, 1), 5 * D_MODEL:6 * D_MODEL]
    out = _layer_norm(ALPHA * x1_ref[...] + gate2 * f, g_ref[...], b_ref[...])

    @pl.when(i < T_CTX // tm)
    def _():
        outc_ref[...] = out

    @pl.when(i >= T_CTX // tm)
    def _():
        outl_ref[...] = out

    @pl.when(i == N_TOK_TILES - 1)
    def _():
        drain(1 - buf)


def _combine(x1, h2, lslot_rows, wsel_rows, unit_tab, ys, w_sgu_bf16, w_sdown_bf16, mod, ln_g, ln_b):
    n_ctx = T_CTX // TM_TOK
    row_map = lambda i, tab: (i, 0)
    const = lambda i, tab: (0, 0)
    return pl.pallas_call(
        _combine_kernel,
        out_shape=(jax.ShapeDtypeStruct((T_CTX, D_MODEL), F32),
                   jax.ShapeDtypeStruct((T_LAT, D_MODEL), F32)),
        grid_spec=pltpu.PrefetchScalarGridSpec(
            num_scalar_prefetch=1,
            grid=(N_TOK_TILES,),
            in_specs=[
                pl.BlockSpec((TM_TOK, D_MODEL), row_map),
                pl.BlockSpec((TM_TOK, D_MODEL), row_map),
                pl.BlockSpec((TM_TOK, SUBLANES), row_map),
                pl.BlockSpec((TM_TOK, SUBLANES), row_map),
                pl.BlockSpec(memory_space=pl.ANY),
                pl.BlockSpec((D_MODEL, 2 * D_SHARED), const),
                pl.BlockSpec((D_SHARED, D_MODEL), const),
                pl.BlockSpec((N_COND, 6 * D_MODEL), const),
                pl.BlockSpec((1, D_MODEL), const),
                pl.BlockSpec((1, D_MODEL), const),
            ],
            out_specs=(pl.BlockSpec((TM_TOK, D_MODEL), lambda i, tab: (jnp.minimum(i, n_ctx - 1), 0)),
                       pl.BlockSpec((TM_TOK, D_MODEL), lambda i, tab: (jnp.maximum(i - n_ctx, 0), 0))),
            scratch_shapes=[
                pltpu.VMEM((2, LOCAL_ROWS, PACK_W), I32),
                pltpu.VMEM((TM_TOK, LOCAL_ROWS), BF16),
                pltpu.VMEM((LOCAL_ROWS, PACK_W), BF16),
                pltpu.VMEM((LOCAL_ROWS, PACK_W), BF16),
                pltpu.SemaphoreType.DMA((2,)),
            ],
        ),
        compiler_params=pltpu.CompilerParams(
            dimension_semantics=("arbitrary",), vmem_limit_bytes=VMEM_LIMIT),
        name="moe_combine",
    )(unit_tab, x1, h2, lslot_rows, wsel_rows, ys, w_sgu_bf16, w_sdown_bf16, mod, ln_g, ln_b)


def _lane_rows(v, width):
    return jnp.broadcast_to(v.astype(F32)[:, None, None], (v.shape[0], 1, width))


def kernel(x_prompt, x_sample, state_ret_fwd, state_ret_bwd, cache_win_k, cache_win_v, cache_na_k, cache_na_v, c, c_ctx, w_in, w_out, ret_decay_fwd, ret_decay_bwd, ret_gn_g, ret_gn_b, win_sink, na_rpb, w_mod, b_mod, ln1_g, ln1_b, ln2_g, ln2_b, w_router, b_router, w_expert_gu, w_expert_down, w_shared_gu, w_shared_down):
    cond = jnp.concatenate(
        [c_ctx[None, :], c, jnp.zeros((N_COND - 1 - DEC_BATCH, D_MODEL), F32)], axis=0)
    mod_all = _modulation(cond, w_mod, b_mod)
    cos_t, sin_t = _rope_tables()

    x_ctx = x_prompt.reshape(T_CTX, D_MODEL)
    x_lat = x_sample.reshape(T_LAT, D_MODEL)
    sf_l, sb_l, caches = [], [], ()
    xs = ys = None
    for l in range(DEPTH):
        mod = mod_all[l]
        pa, pb, pc, *caches = _in_projection(x_ctx, x_lat, mod, w_in[l].astype(BF16), tuple(caches))
        decf_s, decb_s = _lane_rows(ret_decay_fwd[l], SEQ), _lane_rows(ret_decay_bwd[l], SEQ)
        gng, gnb = ret_gn_g[l][None, :], ret_gn_b[l][None, :]
        mix_c, st_f, st_b = _ctx_mixers(pa, pb, pc, win_sink[l], decf_s, decb_s, gng, gnb)
        sf_l.append(st_f)
        sb_l.append(st_b)
        ya = _lat_retention(pa, state_ret_fwd, state_ret_bwd, l,
                            _lane_rows(ret_decay_fwd[l], RET_CHUNK), _lane_rows(ret_decay_bwd[l], RET_CHUNK),
                            gng, gnb)
        yb = _lat_window_attn(pb, cache_win_k, cache_win_v, l, win_sink[l], cos_t, sin_t)
        yc = _lat_na_attn(pc, cache_na_k, cache_na_v, l, _na_maskbias(na_rpb[l]))

        wr_hi = w_router[l].astype(BF16)
        wr_lo = (w_router[l] - wr_hi.astype(F32)).astype(BF16)
        x1, h2, eidx, wsel, rank, counts = _post_mixer(
            x_ctx, x_lat, mix_c, ya, yb, yc, w_out[l].astype(BF16), mod, ln1_g[l][None, :],
            ln1_b[l][None, :], jnp.concatenate([wr_hi, wr_lo], axis=1), b_router[l][:, None])
        lslot, unit_tab, gend = _slot_plan(eidx, rank, counts)
        xs = _dispatch(h2, lslot, unit_tab, gend, xs)
        ys = _experts(xs, w_expert_gu, w_expert_down, l, gend, ys)
        x_ctx, x_lat = _combine(x1, h2, lslot.T, wsel.T, unit_tab, ys, w_shared_gu[l].astype(BF16),
                                w_shared_down[l].astype(BF16), mod, ln2_g[l][None, :], ln2_b[l][None, :])

    y_prompt = x_ctx.reshape(BATCH, SEQ, D_MODEL)
    y_sample = x_lat.reshape(DEC_BATCH, DEC_SEQ, D_MODEL)
    new_sf = jnp.stack(sf_l, axis=1)
    new_sb = jnp.stack(sb_l, axis=1)

    return (y_prompt, y_sample, new_sf, new_sb, *caches)
```
